```python
import math
import jax, jax.numpy as jnp
from jax import lax
import numpy as np


D_MODEL = 1024
BATCH = 2
SEQ = 16384
DEPTH = 2

GRID_W = 64
CTX_LEN = 256
N_MIXERS = 2
EPS = 1e-6

MLA_HEADS = 8
QK_NOPE = 128
QK_ROPE = 64
V_DIM = 128
Q_LORA = 512
KV_LORA = 256
ROPE_THETA = 10000.0
MLA_SCALE = (QK_NOPE + QK_ROPE) ** -0.5
Q_BLOCK = 128

HY_ORDER = 2
HY_DIRS = 2
HY_EMB = 33
HY_BANDS = (HY_EMB - 1) // 2
HY_FILTER_HIDDEN = 64
HY_SHORT = 3
HY_TARGET = 1e-2
HY_FAST_PCT = 0.3
HY_SLOW_PCT = 1.5

N_EXPERTS = 32
TOP_K = 4
D_FF = 1024
SWIGLU_LIMIT = 7.0
SWIGLU_ALPHA = 1.702
MOE_BLOCK = 128

kernel_name = 'hybrid_mla_hyena_moe_diffusion_block'


def rms_norm(x, g):
    xf = x.astype(jnp.float32)
    y = xf * lax.rsqrt(jnp.mean(xf * xf, axis=-1, keepdims=True) + EPS)
    return (y * g.astype(jnp.float32)).astype(x.dtype)


def ada_mod(cond, w, b):
    m = jax.nn.silu(cond) @ w + b
    return jnp.split(m, 6, axis=-1)


def modulate(h, shift, scale):
    return h * (1.0 + scale) + shift


def axial_rope_tables(n_tokens):
    rows = n_tokens // GRID_W
    row = jnp.broadcast_to(jnp.arange(rows, dtype=jnp.float32)[:, None], (rows, GRID_W)).reshape(-1)
    col = jnp.broadcast_to(jnp.arange(GRID_W, dtype=jnp.float32)[None, :], (rows, GRID_W)).reshape(-1)
    axis_dim = QK_ROPE // 2
    inv_freq = 1.0 / (ROPE_THETA ** (jnp.arange(0, axis_dim, 2, dtype=jnp.float32) / axis_dim))
    ang = jnp.concatenate([row[:, None] * inv_freq, col[:, None] * inv_freq], axis=-1)
    return jnp.cos(ang), jnp.sin(ang)


def apply_rope(x, cos, sin):
    x1 = x[..., 0::2].astype(jnp.float32)
    x2 = x[..., 1::2].astype(jnp.float32)
    out = jnp.stack([x1 * cos - x2 * sin, x1 * sin + x2 * cos], axis=-1)
    return out.reshape(x.shape).astype(x.dtype)


def mla_project(h, w_down, g_q, w_uq, g_kv, w_ukv, rope, need_q):
    bsz, n, _ = h.shape
    if need_q:
        lat = h @ w_down
        q_lat, kv_lat, k_pe = jnp.split(lat, [Q_LORA, Q_LORA + KV_LORA], axis=-1)
        q = (rms_norm(q_lat, g_q) @ w_uq).reshape(bsz, n, MLA_HEADS, QK_NOPE + QK_ROPE)
        q_nope, q_pe = jnp.split(q, [QK_NOPE], axis=-1)
    else:
        lat = h @ w_down[:, Q_LORA:]
        kv_lat, k_pe = jnp.split(lat, [KV_LORA], axis=-1)
        q_nope, q_pe = None, None
    kv = (rms_norm(kv_lat, g_kv) @ w_ukv).reshape(bsz, n, MLA_HEADS, QK_NOPE + V_DIM)
    k_nope, v = jnp.split(kv, [QK_NOPE], axis=-1)
    if rope is not None:
        cos, sin = rope
        if need_q:
            q_pe = apply_rope(q_pe, cos[None, :, None, :], sin[None, :, None, :])
        k_pe = apply_rope(k_pe, cos[None], sin[None])
    return q_nope, q_pe, k_nope, k_pe, v


def mla_attend(q_nope, q_pe, k_nope, k_pe, v):
    s = jnp.einsum('bqhd,bkhd->bhqk', q_nope, k_nope) + jnp.einsum('bqhd,bkd->bhqk', q_pe, k_pe)
    p = jax.nn.softmax(s.astype(jnp.float32) * MLA_SCALE, axis=-1).astype(v.dtype)
    return jnp.einsum('bhqk,bkhd->bqhd', p, v)


def mla_latent_attention(q_nope, q_pe, k_nope, k_pe, v):
    bsz, n, _, _ = q_nope.shape
    nb = n // Q_BLOCK
    qn = jnp.swapaxes(q_nope.reshape(bsz, nb, Q_BLOCK, MLA_HEADS, QK_NOPE), 0, 1)
    qp = jnp.swapaxes(q_pe.reshape(bsz, nb, Q_BLOCK, MLA_HEADS, QK_ROPE), 0, 1)
    o = lax.map(lambda qs: mla_attend(qs[0], qs[1], k_nope, k_pe, v), (qn, qp))
    return jnp.swapaxes(o, 0, 1).reshape(bsz, n, MLA_HEADS * V_DIM)


def hyena_filters(n_tokens, f_w1, f_b1, f_freq1, f_w2, f_b2, f_freq2, f_w3, decay):
    f32 = jnp.float32
    d = decay.shape[-1]
    t = jnp.arange(n_tokens, dtype=f32)
    t_norm = t / max(n_tokens - 1, 1)
    bands = jnp.linspace(1e-4, HY_BANDS - 1, HY_BANDS, dtype=f32)
    ang = (2.0 * math.pi / n_tokens) * t[:, None] * bands[None, :]
    z = jnp.concatenate([t_norm[:, None], jnp.cos(ang), -jnp.sin(ang)], axis=-1)
    hdn = jnp.sin(f_freq1.astype(f32) * (z @ f_w1.astype(f32) + f_b1.astype(f32)))
    hdn = jnp.sin(f_freq2.astype(f32) * (hdn @ f_w2.astype(f32) + f_b2.astype(f32)))
    filt = (hdn @ f_w3.astype(f32)).reshape(n_tokens, HY_ORDER, HY_DIRS, d)
    filt = filt * jnp.exp(-t_norm[:, None, None, None] * jnp.abs(decay.astype(f32))[None])
    fwd, bwd = filt[:, :, 0], filt[:, :, 1]
    two_sided = jnp.concatenate([fwd, jnp.zeros((1, HY_ORDER, d), f32), bwd[1:][::-1]], axis=0)
    return jnp.fft.rfft(two_sided, axis=0)


def hyena_mix(h, w_in, b_in, conv_w, conv_b, f_w1, f_b1, f_freq1, f_w2, f_b2, f_freq2, f_w3, decay, fbias, w_out, b_out):
    bsz, n, d = h.shape
    proj = h @ w_in + b_in
    pad = jnp.pad(proj, ((0, 0), (1, 1), (0, 0)))
    proj = pad[:, :-2] * conv_w[0] + pad[:, 1:-1] * conv_w[1] + pad[:, 2:] * conv_w[2] + conv_b
    x0, x1, v = jnp.split(proj, 3, axis=-1)
    kf = hyena_filters(n, f_w1, f_b1, f_freq1, f_w2, f_b2, f_freq2, f_w3, decay)
    z = v.astype(jnp.float32)
    for o, gate in enumerate((x0, x1)):
        zf = jnp.fft.rfft(z, n=2 * n, axis=1)
        conv = jnp.fft.irfft(zf * kf[None, :, o], n=2 * n, axis=1)[:, :n]
        z = gate.astype(jnp.float32) * (conv + z * fbias[o].astype(jnp.float32))
    return z.astype(h.dtype) @ w_out + b_out


def moe_ffn(h, w_r, b_r, w_in, b_in, w_out, b_out):
    n_tok, d = h.shape
    n_sel = n_tok * TOP_K
    logits = (h @ w_r + b_r).astype(jnp.float32)
    top_v, top_i = lax.top_k(logits, TOP_K)
    gates = jax.nn.softmax(top_v, axis=-1)
    e_flat = top_i.reshape(-1)
    g_flat = gates.reshape(-1)
    tok_flat = jnp.arange(n_sel, dtype=jnp.int32) // TOP_K
    order = jnp.argsort(e_flat)
    e_s, tok_s, g_s = e_flat[order], tok_flat[order], g_flat[order]
    counts = jax.ops.segment_sum(jnp.ones_like(e_flat), e_flat, num_segments=N_EXPERTS)
    start = jnp.cumsum(counts) - counts
    padded = (counts + MOE_BLOCK - 1) // MOE_BLOCK * MOE_BLOCK
    pad_end = jnp.cumsum(padded)
    pad_start = pad_end - padded
    dest = pad_start[e_s] + (jnp.arange(n_sel, dtype=jnp.int32) - start[e_s])
    n_rows = (n_sel + MOE_BLOCK - 1) // MOE_BLOCK * MOE_BLOCK + N_EXPERTS * MOE_BLOCK
    n_blocks = n_rows // MOE_BLOCK
    row_tok = jnp.full((n_rows,), n_tok, jnp.int32).at[dest].set(tok_s)
    row_gate = jnp.zeros((n_rows,), h.dtype).at[dest].set(g_s.astype(h.dtype))
    blk_start = jnp.arange(n_blocks, dtype=jnp.int32) * MOE_BLOCK
    blk_exp = jnp.minimum(jnp.searchsorted(pad_end, blk_start, side='right'), N_EXPERTS - 1)
    h_pad = jnp.concatenate([h, jnp.zeros((1, d), h.dtype)], axis=0)
    xb = h_pad[row_tok].reshape(n_blocks, MOE_BLOCK, d)

    def expert_block(args):
        xblk, e = args
        gu = xblk @ w_in[e] + b_in[e]
        gate = jnp.minimum(gu[:, :D_FF], SWIGLU_LIMIT)
        lin = jnp.clip(gu[:, D_FF:], -SWIGLU_LIMIT, SWIGLU_LIMIT)
        act = gate * jax.nn.sigmoid(SWIGLU_ALPHA * gate) * (lin + 1.0)
        return act @ w_out[e] + b_out[e]

    yb = lax.map(expert_block, (xb, blk_exp)).reshape(n_rows, d)
    y = jnp.zeros((n_tok + 1, d), h.dtype).at[row_tok].add(yb * row_gate[:, None])
    return y[:n_tok]


def setup_inputs(seed: int = 0) -> dict:
    key = jax.random.key(seed)
    ks = iter(jax.random.split(key, 48))
    f32 = jnp.float32

    def nrm(shape, s):
        return s * jax.random.normal(next(ks), shape, f32)

    D = D_MODEL
    n_mla = (DEPTH + N_MIXERS - 1) // N_MIXERS
    n_hy = DEPTH // N_MIXERS
    min_mag = -math.log(HY_TARGET) / HY_SLOW_PCT
    max_mag = -math.log(HY_TARGET) / HY_FAST_PCT
    decay_base = jnp.linspace(min_mag, max_mag, D, dtype=f32)
    return {
        'x': nrm((BATCH, SEQ, D), 1.0),
        'c': nrm((BATCH, D), 1.0),
        'ctx': nrm((BATCH, CTX_LEN, D), 1.0),
        'c_ctx': nrm((D,), 1.0),
        'ada_w': nrm((DEPTH, D, 6 * D), D ** -0.5),
        'ada_b': nrm((DEPTH, 6 * D), 0.01),
        'norm_mix_g': 1.0 + nrm((DEPTH, D), 0.02),
        'norm_ffn_g': 1.0 + nrm((DEPTH, D), 0.02),
        'mla_w_down': nrm((n_mla, D, Q_LORA + KV_LORA + QK_ROPE), D ** -0.5),
        'mla_g_q': 1.0 + nrm((n_mla, Q_LORA), 0.02),
        'mla_w_uq': nrm((n_mla, Q_LORA, MLA_HEADS * (QK_NOPE + QK_ROPE)), Q_LORA ** -0.5),
        'mla_g_kv': 1.0 + nrm((n_mla, KV_LORA), 0.02),
        'mla_w_ukv': nrm((n_mla, KV_LORA, MLA_HEADS * (QK_NOPE + V_DIM)), KV_LORA ** -0.5),
        'mla_w_o': nrm((n_mla, MLA_HEADS * V_DIM, D), (MLA_HEADS * V_DIM) ** -0.5),
        'hy_w_in': nrm((n_hy, D, 3 * D), D ** -0.5),
        'hy_b_in': nrm((n_hy, 3 * D), 0.01),
        'hy_conv_w': nrm((n_hy, HY_SHORT, 3 * D), HY_SHORT ** -0.5),
        'hy_conv_b': nrm((n_hy, 3 * D), 0.01),
        'hy_f_w1': nrm((n_hy, HY_EMB, HY_FILTER_HIDDEN), HY_EMB ** -0.5),
        'hy_f_b1': nrm((n_hy, HY_FILTER_HIDDEN), 0.1),
        'hy_f_freq1': 1.0 + nrm((n_hy, HY_FILTER_HIDDEN), 0.05),
        'hy_f_w2': nrm((n_hy, HY_FILTER_HIDDEN, HY_FILTER_HIDDEN), HY_FILTER_HIDDEN ** -0.5),
        'hy_f_b2': nrm((n_hy, HY_FILTER_HIDDEN), 0.1),
        'hy_f_freq2': 1.0 + nrm((n_hy, HY_FILTER_HIDDEN), 0.05),
        'hy_f_w3': nrm((n_hy, HY_FILTER_HIDDEN, HY_ORDER * HY_DIRS * D), 0.02 * HY_FILTER_HIDDEN ** -0.5),
        'hy_decay': jnp.broadcast_to(decay_base, (n_hy, HY_ORDER, HY_DIRS, D)) + nrm((n_hy, HY_ORDER, HY_DIRS, D), 0.1),
        'hy_bias': nrm((n_hy, HY_ORDER, D), 1.0),
        'hy_w_out': nrm((n_hy, D, D), D ** -0.5),
        'hy_b_out': nrm((n_hy, D), 0.01),
        'moe_w_r': nrm((DEPTH, D, N_EXPERTS), D ** -0.5),
        'moe_b_r': nrm((DEPTH, N_EXPERTS), 0.01),
        'moe_w_in': nrm((DEPTH, N_EXPERTS, D, 2 * D_FF), D ** -0.5),
        'moe_b_in': nrm((DEPTH, N_EXPERTS, 2 * D_FF), 0.01),
        'moe_w_out': nrm((DEPTH, N_EXPERTS, D_FF, D), D_FF ** -0.5),
        'moe_b_out': nrm((DEPTH, N_EXPERTS, D), 0.01),
        'final_g': 1.0 + nrm((D,), 0.02),
    }


def reference(x, c, ctx, c_ctx, ada_w, ada_b, norm_mix_g, norm_ffn_g, mla_w_down, mla_g_q, mla_w_uq, mla_g_kv, mla_w_ukv, mla_w_o, hy_w_in, hy_b_in, hy_conv_w, hy_conv_b, hy_f_w1, hy_f_b1, hy_f_freq1, hy_f_w2, hy_f_b2, hy_f_freq2, hy_f_w3, hy_decay, hy_bias, hy_w_out, hy_b_out, moe_w_r, moe_b_r, moe_w_in, moe_b_in, moe_w_out, moe_b_out, final_g):
    bsz, n_lat, d = x.shape
    n_ctx = ctx.shape[1]
    rope = axial_rope_tables(n_lat)
    xl, xc = x, ctx
    for i in range(DEPTH):
        kind, j = i % N_MIXERS, i // N_MIXERS
        ctx_live = any(m % N_MIXERS == 0 for m in range(i + 1, DEPTH))
        sh1, sc1, g1, sh2, sc2, g2 = [m[:, None, :] for m in ada_mod(c, ada_w[i], ada_b[i])]
        csh1, csc1, cg1, csh2, csc2, cg2 = ada_mod(c_ctx, ada_w[i], ada_b[i])
        hl = modulate(rms_norm(xl, norm_mix_g[i]), sh1, sc1)
        if kind == 0:
            hc = modulate(rms_norm(xc, norm_mix_g[i]), csh1, csc1)
            mla_p = (mla_w_down[j], mla_g_q[j], mla_w_uq[j], mla_g_kv[j], mla_w_ukv[j])
            ql_n, ql_p, kl_n, kl_p, vl = mla_project(hl, *mla_p, rope=rope, need_q=True)
            qc_n, qc_p, kc_n, kc_p, vc = mla_project(hc, *mla_p, rope=None, need_q=ctx_live)
            k_n = jnp.concatenate([kl_n, kc_n], axis=1)
            k_p = jnp.concatenate([kl_p, kc_p], axis=1)
            v_all = jnp.concatenate([vl, vc], axis=1)
            ol = mla_latent_attention(ql_n, ql_p, k_n, k_p, v_all) @ mla_w_o[j]
            if ctx_live:
                oc = mla_attend(qc_n, qc_p, kc_n, kc_p, vc).reshape(bsz, n_ctx, MLA_HEADS * V_DIM) @ mla_w_o[j]
        else:
            hy_p = (hy_w_in[j], hy_b_in[j], hy_conv_w[j], hy_conv_b[j], hy_f_w1[j], hy_f_b1[j], hy_f_freq1[j],
                    hy_f_w2[j], hy_f_b2[j], hy_f_freq2[j], hy_f_w3[j], hy_decay[j], hy_bias[j], hy_w_out[j], hy_b_out[j])
            ol = hyena_mix(hl, *hy_p)
            if ctx_live:
                oc = hyena_mix(modulate(rms_norm(xc, norm_mix_g[i]), csh1, csc1), *hy_p)
        xl = xl + g1 * ol
        moe_p = (moe_w_r[i], moe_b_r[i], moe_w_in[i], moe_b_in[i], moe_w_out[i], moe_b_out[i])
        fl = modulate(rms_norm(xl, norm_ffn_g[i]), sh2, sc2).reshape(bsz * n_lat, d)
        if ctx_live:
            xc = xc + cg1 * oc
            fc = modulate(rms_norm(xc, norm_ffn_g[i]), csh2, csc2).reshape(bsz * n_ctx, d)
            y = moe_ffn(jnp.concatenate([fl, fc], axis=0), *moe_p)
            xl = xl + g2 * y[:bsz * n_lat].reshape(bsz, n_lat, d)
            xc = xc + cg2 * y[bsz * n_lat:].reshape(bsz, n_ctx, d)
        else:
            y = moe_ffn(fl, *moe_p)
            xl = xl + g2 * y.reshape(bsz, n_lat, d)
    return rms_norm(xl, final_g)
```

```python
import functools
import math

import jax
import jax.numpy as jnp
from jax import lax
from jax.experimental import pallas as pl
from jax.experimental.pallas import tpu as pltpu

F32 = jnp.float32
BF16 = jnp.bfloat16

EPS = 1e-6
GRID_W = 64
MLA_HEADS = 8
QK_NOPE = 128
QK_ROPE = 64
V_DIM = 128
Q_LORA = 512
KV_LORA = 256
ROPE_THETA = 10000.0
MLA_SCALE = (QK_NOPE + QK_ROPE) ** -0.5
QK_PAD = 256

HY_EMB = 33
HY_BANDS = (HY_EMB - 1) // 2
HY_HID = 64
FFT_N2 = 128
B_GROUP = 8

N_EXPERTS = 32
TOP_K = 4
SWIGLU_LIMIT = 7.0
SWIGLU_ALPHA = 1.702
MOE_TM = 256
LANES = 128

VMEM_LIMIT = 56 * 1024 * 1024


def _cp(sem, vmem=VMEM_LIMIT):
    return pltpu.CompilerParams(dimension_semantics=sem, vmem_limit_bytes=vmem)


def _dot(a, b):
    return jnp.dot(a, b, preferred_element_type=F32)


def _dot_hi(a, b):
    return jnp.dot(a, b, preferred_element_type=F32, precision=lax.Precision.HIGHEST)


def _rms(x, g):
    return x * lax.rsqrt(jnp.mean(x * x, axis=-1, keepdims=True) + EPS) * g


def _ada_body(c_ref, w_ref, b_ref, o_ref):
    c = c_ref[...]
    s = c * jax.nn.sigmoid(c)
    o_ref[0] = _dot(s.astype(BF16), w_ref[0].astype(BF16)) + b_ref[0]


def _ada(cond8, ada_w, ada_b):
    depth, d, n = ada_w.shape
    tn = n // 4
    return pl.pallas_call(
        _ada_body,
        grid=(depth, n // tn),
        in_specs=[
            pl.BlockSpec((8, d), lambda i, j: (0, 0)),
            pl.BlockSpec((1, d, tn), lambda i, j: (i, 0, j)),
            pl.BlockSpec((1, 1, tn), lambda i, j: (i, 0, j)),
        ],
        out_specs=pl.BlockSpec((1, 8, tn), lambda i, j: (i, 0, j)),
        out_shape=jax.ShapeDtypeStruct((depth, 8, n), F32),
        compiler_params=_cp(("arbitrary", "arbitrary")),
        name="ada_mod",
    )(cond8, ada_w, ada_b.reshape(depth, 1, n))


def _mla_proj_body(x_ref, g_ref, sh_ref, sc_ref, wd_ref, gq_ref, gkv_ref, wuk_ref, wuqT_ref, wuvT_ref,
                   ct_ref, st_ref, cT_ref, sT_ref, *out_refs, need_q, tk):
    if need_q:
        qT_ref, k_ref, vT_ref = out_refs
    else:
        k_ref, vT_ref = out_refs
    nh = MLA_HEADS
    x = x_ref[0]
    h = _rms(x, g_ref[...]) * (1.0 + sc_ref[0]) + sh_ref[0]
    lat = _dot(h.astype(BF16), wd_ref[...])
    o_kv = Q_LORA
    o_a = Q_LORA + KV_LORA
    kvn = _rms(lat[:, o_kv:o_a], gkv_ref[...])
    kr = (lat[:, o_a:o_a + LANES] * ct_ref[...] + lat[:, o_a + LANES:o_a + 2 * LANES] * st_ref[...]).astype(BF16)
    knope = _dot(kvn.astype(BF16), wuk_ref[...])
    for hh in range(nh):
        k_ref[0, hh, :, 0:QK_NOPE] = knope[:, hh * QK_NOPE:(hh + 1) * QK_NOPE].astype(BF16)
        k_ref[0, hh, :, QK_NOPE:QK_PAD] = kr
    vT = _dot(wuvT_ref[...], kvn.T.astype(BF16))
    tm = x.shape[0]
    for hh in range(nh):
        for c in range(tm // tk):
            vT_ref[0, hh, c] = vT[hh * V_DIM:(hh + 1) * V_DIM, c * tk:(c + 1) * tk].astype(BF16)
    if need_q:
        qn = _rms(lat[:, :Q_LORA], gq_ref[...])
        qT = _dot(wuqT_ref[...], qn.T.astype(BF16)) * (MLA_SCALE * math.log2(math.e))
        c = cT_ref[...]
        s = sT_ref[...]
        hw = QK_NOPE + QK_ROPE
        half = QK_ROPE // 2
        for hh in range(nh):
            base = hh * hw
            x1 = qT[base + QK_NOPE:base + QK_NOPE + half]
            x2 = qT[base + QK_NOPE + half:base + hw]
            qT_ref[0, hh, 0:QK_NOPE] = qT[base:base + QK_NOPE].astype(BF16)
            qT_ref[0, hh, QK_NOPE:QK_NOPE + half] = (x1 * c - x2 * s).astype(BF16)
            qT_ref[0, hh, QK_NOPE + half:hw] = (x1 * s + x2 * c).astype(BF16)
            qT_ref[0, hh, hw:QK_PAD] = jnp.zeros((QK_PAD - hw, tm), BF16)


def _mla_proj(x, g, sh, sc, wts, tabs, *, need_q, tm, tk):
    bsz, n, d = x.shape
    nh = MLA_HEADS
    wd, gq, gkv, wuk, wuqT, wuvT = wts
    ct, st, cT, sT = tabs
    nsh = sh.shape[0]
    full = lambda a: pl.BlockSpec(a.shape, lambda b, i: (0,) * a.ndim)
    in_specs = [
        pl.BlockSpec((1, tm, d), lambda b, i: (b, i, 0)),
        full(g),
        pl.BlockSpec((1, 1, d), lambda b, i: (b % nsh, 0, 0)),
        pl.BlockSpec((1, 1, d), lambda b, i: (b % nsh, 0, 0)),
        full(wd), full(gq), full(gkv), full(wuk), full(wuqT), full(wuvT),
        pl.BlockSpec((tm, LANES), lambda b, i: (i, 0)),
        pl.BlockSpec((tm, LANES), lambda b, i: (i, 0)),
        pl.BlockSpec((QK_ROPE // 2, tm), lambda b, i: (0, i)),
        pl.BlockSpec((QK_ROPE // 2, tm), lambda b, i: (0, i)),
    ]
    out_specs = [
        pl.BlockSpec((1, nh, tm, QK_PAD), lambda b, i: (b, 0, i, 0)),
        pl.BlockSpec((1, nh, tm // tk, V_DIM, tk), lambda b, i: (b, 0, i, 0, 0)),
    ]
    out_shape = [
        jax.ShapeDtypeStruct((bsz, nh, n, QK_PAD), BF16),
        jax.ShapeDtypeStruct((bsz, nh, n // tk, V_DIM, tk), BF16),
    ]
    if need_q:
        out_specs = [pl.BlockSpec((1, nh, QK_PAD, tm), lambda b, i: (b, 0, 0, i))] + out_specs
        out_shape = [jax.ShapeDtypeStruct((bsz, nh, QK_PAD, n), BF16)] + out_shape
    return pl.pallas_call(
        functools.partial(_mla_proj_body, need_q=need_q, tk=tk),
        grid=(bsz, n // tm),
        in_specs=in_specs,
        out_specs=out_specs,
        out_shape=out_shape,
        compiler_params=_cp(("arbitrary", "arbitrary")),
        name="mla_proj_q" if need_q else "mla_proj_ctx",
    )(x, g, sh, sc, wd, gq, gkv, wuk, wuqT, wuvT, ct, st, cT, sT)


def _attn_body(qT_ref, k_ref, vT_ref, kc_ref, vTc_ref, o_ref, *, tk):
    q = qT_ref[0, 0]
    tq = q.shape[1]
    nchunk = k_ref.shape[2] // tk

    def step(kblk, vblk, carry):
        m, l, acc = carry
        s = _dot(kblk, q)
        m_new = jnp.maximum(m, jnp.max(s, axis=0, keepdims=True))
        alpha = jnp.exp2(m - m_new)
        p = jnp.exp2(s - m_new)
        l = alpha * l + jnp.sum(p, axis=0, keepdims=True)
        acc = alpha * acc + _dot(vblk, p.astype(BF16))
        return m_new, l, acc

    carry = (jnp.full((1, tq), -jnp.inf, F32), jnp.zeros((1, tq), F32), jnp.zeros((V_DIM, tq), F32))
    carry = step(kc_ref[0, 0], vTc_ref[0, 0, 0], carry)

    def body(i, carry):
        kblk = k_ref[0, 0, pl.ds(pl.multiple_of(i * tk, tk), tk), :]
        return step(kblk, vT_ref[0, 0, i], carry)

    _, l, acc = lax.fori_loop(0, nchunk, body, carry)
    o_ref[0, 0] = (acc / l).astype(BF16)


def _attention(qT, k, vT, kc, vTc, *, tq, tk):
    bsz, nh, _, n = qT.shape
    nc = kc.shape[2]
    return pl.pallas_call(
        functools.partial(_attn_body, tk=tk),
        grid=(bsz, nh, n // tq),
        in_specs=[
            pl.BlockSpec((1, 1, QK_PAD, tq), lambda b, h, i: (b, h, 0, i)),
            pl.BlockSpec((1, 1, n, QK_PAD), lambda b, h, i: (b, h, 0, 0)),
            pl.BlockSpec((1, 1, n // tk, V_DIM, tk), lambda b, h, i: (b, h, 0, 0, 0)),
            pl.BlockSpec((1, 1, nc, QK_PAD), lambda b, h, i: (b, h, 0, 0)),
            pl.BlockSpec((1, 1, 1, V_DIM, nc), lambda b, h, i: (b, h, 0, 0, 0)),
        ],
        out_specs=pl.BlockSpec((1, 1, V_DIM, tq), lambda b, h, i: (b, h, 0, i)),
        out_shape=jax.ShapeDtypeStruct((bsz, nh, V_DIM, n), BF16),
        compiler_params=_cp(("arbitrary", "arbitrary", "arbitrary")),
        name="mla_attention",
    )(qT, k, vT, kc, vTc)


def _post_body(o_ref, wo_ref, bo_ref, x_ref, g1_ref, gf_ref, sh_ref, sc_ref, wrh_ref, wrl_ref, br_ref, tri_ref,
               xl_ref, fl_ref, ti_ref, gt_ref, rk_ref, cnt_ref, *, transposed):
    @pl.when((pl.program_id(0) == 0) & (pl.program_id(1) == 0))
    def _():
        cnt_ref[...] = jnp.zeros_like(cnt_ref)

    tm = x_ref.shape[1]
    if transposed:
        oT = o_ref[0].astype(F32).reshape(MLA_HEADS * V_DIM, tm)
        o = oT.T.astype(BF16)
    else:
        o = o_ref[0].astype(BF16)
    y = _dot(o, wo_ref[...]) + bo_ref[...]
    xl = x_ref[0] + g1_ref[0] * y
    xl_ref[0] = xl
    fl = _rms(xl, gf_ref[...]) * (1.0 + sc_ref[0]) + sh_ref[0]
    fl_ref[...] = fl
    flh = fl.astype(BF16)
    fll = (fl - flh.astype(F32)).astype(BF16)
    logits = _dot(flh, wrh_ref[...]) + (_dot(fll, wrh_ref[...]) + _dot(flh, wrl_ref[...])) + br_ref[...]
    lane = lax.broadcasted_iota(jnp.int32, (tm, LANES), 1).astype(F32)
    neg = jnp.float32(-jnp.inf)
    work = jnp.where(lane < N_EXPERTS, logits, neg)
    vals, idxs = [], []
    onehot = jnp.zeros((tm, LANES), F32)
    for _ in range(TOP_K):
        mk = jnp.max(work, axis=-1, keepdims=True)
        ik = jnp.min(jnp.where(work == mk, lane, float(LANES)), axis=-1, keepdims=True)
        sel = lane == ik
        onehot = jnp.where(sel, 1.0, onehot)
        work = jnp.where(sel, neg, work)
        vals.append(mk)
        idxs.append(ik)
    es = [jnp.exp(v - vals[0]) for v in vals]
    den = es[0] + es[1] + es[2] + es[3]
    pre = _dot(tri_ref[...], onehot.astype(BF16)) + cnt_ref[...]
    ti = jnp.zeros((tm, LANES), F32)
    gt = jnp.zeros((tm, LANES), F32)
    rk = jnp.zeros((tm, LANES), F32)
    for kk in range(TOP_K):
        rank = jnp.sum(jnp.where(lane == idxs[kk], pre, 0.0), axis=-1, keepdims=True)
        ti = jnp.where(lane == kk, idxs[kk], ti)
        gt = jnp.where(lane == kk, es[kk] / den, gt)
        rk = jnp.where(lane == kk, rank, rk)
    ti_ref[...] = ti[:, :TOP_K].astype(jnp.int32)
    gt_ref[...] = gt[:, :TOP_K]
    rk_ref[...] = rk[:, :TOP_K].astype(jnp.int32)
    cnt_ref[...] += jnp.sum(onehot, axis=0, keepdims=True)


def _post(o, wo, bo, x, g1, gf, sh, sc, wrh, wrl, br, *, transposed, tm):
    bsz, n, d = x.shape
    t = bsz * n
    nt = n // tm
    tri = (lax.broadcasted_iota(jnp.int32, (tm, tm), 0) > lax.broadcasted_iota(jnp.int32, (tm, tm), 1)).astype(BF16)
    full = lambda a: pl.BlockSpec(a.shape, lambda b, i: (0,) * a.ndim)
    per_b = pl.BlockSpec((1, 1, d), lambda b, i: (b, 0, 0))
    if transposed:
        o_spec = pl.BlockSpec((1, MLA_HEADS, V_DIM, tm), lambda b, i: (b, 0, 0, i))
    else:
        o_spec = pl.BlockSpec((1, tm, d), lambda b, i: (b, i, 0))
    tok = lambda w: pl.BlockSpec((tm, w), lambda b, i: (b * nt + i, 0))
    return pl.pallas_call(
        functools.partial(_post_body, transposed=transposed),
        grid=(bsz, nt),
        in_specs=[o_spec, full(wo), full(bo), pl.BlockSpec((1, tm, d), lambda b, i: (b, i, 0)), per_b, full(gf),
                  per_b, per_b, full(wrh), full(wrl), full(br), full(tri)],
        out_specs=[pl.BlockSpec((1, tm, d), lambda b, i: (b, i, 0)), tok(d), tok(TOP_K), tok(TOP_K), tok(TOP_K),
                   pl.BlockSpec((1, LANES), lambda b, i: (0, 0))],
        out_shape=[jax.ShapeDtypeStruct((bsz, n, d), F32), jax.ShapeDtypeStruct((t, d), F32),
                   jax.ShapeDtypeStruct((t, TOP_K), jnp.int32), jax.ShapeDtypeStruct((t, TOP_K), F32),
                   jax.ShapeDtypeStruct((t, TOP_K), jnp.int32), jax.ShapeDtypeStruct((1, LANES), F32)],
        compiler_params=_cp(("arbitrary", "arbitrary")),
        name="post_attn" if transposed else "post_hyena",
    )(o, wo, bo, x, g1, gf, sh, sc, wrh, wrl, br, tri)


def _dispatch_body(dest_ref, fl_hbm, xs_in, xs_out, sem, *, td):
    del xs_in
    base = pl.program_id(0) * td

    def issue(t, carry):
        for kk in range(TOP_K):
            d = dest_ref[0, 0, t * TOP_K + kk]
            pltpu.make_async_copy(fl_hbm.at[pl.ds(base + t, 1)], xs_out.at[pl.ds(d, 1)], sem).start()
        return carry

    lax.fori_loop(0, td, issue, 0)

    def drain(t, carry):
        pltpu.make_async_copy(fl_hbm.at[pl.ds(0, 1)], xs_out.at[pl.ds(0, 1)], sem).wait()
        return carry

    lax.fori_loop(0, td * TOP_K, drain, 0)


def _dispatch(dest, fl, n_rows, *, td):
    t, d = fl.shape
    dest3 = dest.reshape(t // td, 1, td * TOP_K)
    xs0 = jnp.zeros((n_rows, d), fl.dtype)
    return pl.pallas_call(
        functools.partial(_dispatch_body, td=td),
        grid=(t // td,),
        in_specs=[
            pl.BlockSpec((1, 1, td * TOP_K), lambda i: (i, 0, 0), memory_space=pltpu.SMEM),
            pl.BlockSpec(memory_space=pl.ANY),
            pl.BlockSpec(memory_space=pl.ANY),
        ],
        out_specs=pl.BlockSpec(memory_space=pl.ANY),
        out_shape=jax.ShapeDtypeStruct((n_rows, d), fl.dtype),
        scratch_shapes=[pltpu.SemaphoreType.DMA(())],
        input_output_aliases={2: 0},
        compiler_params=_cp(("arbitrary",)),
        name="moe_dispatch",
    )(dest3, fl, xs0)


def _expert_body(be_ref, nu_ref, xs_ref, win_ref, bin_ref, wout_ref, bout_ref, ys_ref, win_s, wout_s):
    b = pl.program_id(0)
    dff = wout_ref.shape[1]

    @pl.when(b < nu_ref[0])
    def _():
        prev = be_ref[jnp.maximum(b - 1, 0)]

        @pl.when((b == 0) | (prev != be_ref[b]))
        def _():
            win_s[...] = win_ref[0].astype(BF16)
            wout_s[...] = wout_ref[0].astype(BF16)

        x = xs_ref[...].astype(BF16)
        gu = _dot(x, win_s[...]) + bin_ref[0]
        gate = jnp.minimum(gu[:, :dff], SWIGLU_LIMIT)
        lin = jnp.clip(gu[:, dff:], -SWIGLU_LIMIT, SWIGLU_LIMIT)
        act = gate * jax.nn.sigmoid(SWIGLU_ALPHA * gate) * (lin + 1.0)
        ys_ref[...] = _dot(act.astype(BF16), wout_s[...]) + bout_ref[0]

    @pl.when(b >= nu_ref[0])
    def _():
        ys_ref[...] = jnp.zeros_like(ys_ref)


def _experts(blk_exp, n_used, xs, w_in, b_in, w_out, b_out):
    n_rows, d = xs.shape
    ne, _, f2 = w_in.shape
    dff = w_out.shape[1]
    tm = MOE_TM
    grid_spec = pltpu.PrefetchScalarGridSpec(
        num_scalar_prefetch=2,
        grid=(n_rows // tm,),
        in_specs=[
            pl.BlockSpec((tm, d), lambda b, be, nu: (b, 0)),
            pl.BlockSpec((1, d, f2), lambda b, be, nu: (be[b], 0, 0)),
            pl.BlockSpec((1, 1, f2), lambda b, be, nu: (be[b], 0, 0)),
            pl.BlockSpec((1, dff, d), lambda b, be, nu: (be[b], 0, 0)),
            pl.BlockSpec((1, 1, d), lambda b, be, nu: (be[b], 0, 0)),
        ],
        out_specs=pl.BlockSpec((tm, d), lambda b, be, nu: (b, 0)),
        scratch_shapes=[pltpu.VMEM((d, f2), BF16), pltpu.VMEM((dff, d), BF16)],
    )
    return pl.pallas_call(
        _expert_body,
        grid_spec=grid_spec,
        out_shape=jax.ShapeDtypeStruct((n_rows, d), F32),
        compiler_params=_cp(("arbitrary",)),
        name="moe_experts",
    )(blk_exp, n_used, xs, w_in, b_in.reshape(ne, 1, f2), w_out, b_out.reshape(ne, 1, d))


def _combine_body(dest_ref, ys_hbm, gt_ref, xl_ref, g2_ref, fg_ref, out_ref, buf, sem, *, tc, final):
    def issue(t, carry):
        for kk in range(TOP_K):
            d = dest_ref[0, 0, t * TOP_K + kk]
            pltpu.make_async_copy(ys_hbm.at[pl.ds(d, 1)], buf.at[pl.ds(kk * tc + t, 1)], sem).start()
        return carry

    lax.fori_loop(0, tc, issue, 0)

    def drain(t, carry):
        pltpu.make_async_copy(ys_hbm.at[pl.ds(0, 1)], buf.at[pl.ds(0, 1)], sem).wait()
        return carry

    lax.fori_loop(0, tc * TOP_K, drain, 0)
    gt = gt_ref[...]
    y = gt[:, 0:1] * buf[0:tc]
    for kk in range(1, TOP_K):
        y = y + gt[:, kk:kk + 1] * buf[kk * tc:(kk + 1) * tc]
    xl = xl_ref[0] + g2_ref[0] * y
    out_ref[0] = _rms(xl, fg_ref[...]) if final else xl


def _combine(dest, ys, gates, xl, g2, fg, *, tc, final):
    bsz, n, d = xl.shape
    t = bsz * n
    nt = n // tc
    dest3 = dest.reshape(t // tc, 1, tc * TOP_K)
    return pl.pallas_call(
        functools.partial(_combine_body, tc=tc, final=final),
        grid=(bsz, nt),
        in_specs=[
            pl.BlockSpec((1, 1, tc * TOP_K), lambda b, i: (b * nt + i, 0, 0), memory_space=pltpu.SMEM),
            pl.BlockSpec(memory_space=pl.ANY),
            pl.BlockSpec((tc, TOP_K), lambda b, i: (b * nt + i, 0)),
            pl.BlockSpec((1, tc, d), lambda b, i: (b, i, 0)),
            pl.BlockSpec((1, 1, d), lambda b, i: (b, 0, 0)),
            pl.BlockSpec((1, d), lambda b, i: (0, 0)),
        ],
        out_specs=pl.BlockSpec((1, tc, d), lambda b, i: (b, i, 0)),
        out_shape=jax.ShapeDtypeStruct((bsz, n, d), F32),
        scratch_shapes=[pltpu.VMEM((TOP_K * tc, d), F32), pltpu.SemaphoreType.DMA(())],
        compiler_params=_cp(("arbitrary", "arbitrary")),
        name="moe_combine",
    )(dest3, ys, gates, xl, g2, fg)


def _moe(fl, topi, gates, rank, cnt, xl, g2, fg, w_in, b_in, w_out, b_out, *, final):
    t, d = fl.shape
    tm = MOE_TM
    counts = cnt[0, :N_EXPERTS].astype(jnp.int32)
    padded = (counts + tm - 1) // tm * tm
    pad_end = jnp.cumsum(padded)
    pad_start = pad_end - padded
    dest = jnp.take(pad_start, topi) + rank
    nb = t * TOP_K // tm + N_EXPERTS
    blk_start = jnp.arange(nb, dtype=jnp.int32) * tm
    blk_exp = jnp.minimum(jnp.sum((pad_end[None, :] <= blk_start[:, None]).astype(jnp.int32), axis=1), N_EXPERTS - 1)
    n_used = (pad_end[-1:] // tm).astype(jnp.int32)
    xs = _dispatch(dest, fl, nb * tm, td=256)
    ys = _experts(blk_exp, n_used, xs, w_in, b_in, w_out, b_out)
    return _combine(dest, ys, gates, xl, g2, fg, tc=256, final=final)


def _hy_in_body(x_ref, xp_ref, xn_ref, g_ref, sh_ref, sc_ref, w_ref, b_ref, cw_ref, cb_ref, o_ref, *, nt):
    i = pl.program_id(2)
    w = w_ref[...]

    def proj(xx):
        h = _rms(xx, g_ref[...]) * (1.0 + sc_ref[0]) + sh_ref[0]
        return _dot(h.astype(BF16), w) + b_ref[...]

    p = proj(x_ref[0])
    tm = p.shape[0]
    ph = proj(jnp.concatenate([xp_ref[0], xn_ref[0]], axis=0))
    prev = jnp.where(i > 0, ph[7:8], 0.0)
    nxt = jnp.where(i < nt - 1, ph[8:9], 0.0)
    row = lax.broadcasted_iota(jnp.int32, (tm, 1), 0)
    up = jnp.where(row == 0, prev, pltpu.roll(p, 1, axis=0))
    dn = jnp.where(row == tm - 1, nxt, pltpu.roll(p, tm - 1, axis=0))
    cw = cw_ref[...]
    o_ref[0, 0] = up * cw[0:1] + p * cw[1:2] + dn * cw[2:3] + cb_ref[...]


def _hy_in(x, g, sh, sc, w, b, cw, cb, *, tm):
    bsz, n, d = x.shape
    nt = n // tm
    hb = tm // 8
    per_b = pl.BlockSpec((1, 1, d), lambda j, bb, i: (bb, 0, 0))
    return pl.pallas_call(
        functools.partial(_hy_in_body, nt=nt),
        grid=(3, bsz, nt),
        in_specs=[
            pl.BlockSpec((1, tm, d), lambda j, bb, i: (bb, i, 0)),
            pl.BlockSpec((1, 8, d), lambda j, bb, i: (bb, jnp.maximum(i * hb - 1, 0), 0)),
            pl.BlockSpec((1, 8, d), lambda j, bb, i: (bb, jnp.minimum((i + 1) * hb, n // 8 - 1), 0)),
            pl.BlockSpec((1, d), lambda j, bb, i: (0, 0)),
            per_b, per_b,
            pl.BlockSpec((d, d), lambda j, bb, i: (0, j)),
            pl.BlockSpec((1, d), lambda j, bb, i: (0, j)),
            pl.BlockSpec((3, d), lambda j, bb, i: (0, j)),
            pl.BlockSpec((1, d), lambda j, bb, i: (0, j)),
        ],
        out_specs=pl.BlockSpec((1, 1, tm, d), lambda j, bb, i: (j, bb, i, 0)),
        out_shape=jax.ShapeDtypeStruct((3, bsz, n, d), F32),
        compiler_params=_cp(("arbitrary", "arbitrary", "arbitrary")),
        name="hyena_in_proj",
    )(x, x, x, g, sh, sc, w, b, cw, cb)


def _filt_feat_body(w1_ref, b1_ref, f1_ref, w2_ref, b2_ref, f2_ref, o_ref, *, n_lat):
    na = o_ref.shape[1]
    a = lax.broadcasted_iota(jnp.int32, (na, 1), 0)
    lane = lax.broadcasted_iota(jnp.int32, (na, LANES), 1)
    band_idx = jnp.where(lane <= HY_BANDS, lane - 1, lane - 1 - HY_BANDS).astype(F32)
    band = 1e-4 + band_idx * ((HY_BANDS - 1 - 1e-4) / (HY_BANDS - 1))
    for j in range(B_GROUP):
        r = a * FFT_N2 + (pl.program_id(0) * B_GROUP + j)
        pos = jnp.where(r < n_lat, r, 2 * n_lat - r).astype(F32)
        tn = pos / float(max(n_lat - 1, 1))
        ang = ((2.0 * math.pi / n_lat) * pos) * band
        z = jnp.where(lane == 0, tn, jnp.where(lane <= HY_BANDS, jnp.cos(ang),
                                               jnp.where(lane < HY_EMB, -jnp.sin(ang), 0.0)))
        h1 = jnp.sin(f1_ref[...] * (_dot_hi(z, w1_ref[...]) + b1_ref[...]))
        h2 = jnp.sin(f2_ref[...] * (_dot_hi(h1, w2_ref[...]) + b2_ref[...]))
        valid = (r != n_lat).astype(F32)
        o_ref[j] = jnp.where(lane == HY_HID, tn, jnp.where(lane == HY_HID + 1, valid, h2))


def _filt_feat(w1, b1, f1, w2, b2, f2, *, n_lat):
    na = 2 * n_lat // FFT_N2
    w1p = jnp.zeros((LANES, LANES), F32).at[:HY_EMB, :HY_HID].set(w1)
    w2p = jnp.zeros((LANES, LANES), F32).at[:HY_HID, :HY_HID].set(w2)
    padv = lambda v: jnp.zeros((1, LANES), F32).at[0, :HY_HID].set(v)
    full = lambda shp: pl.BlockSpec(shp, lambda i: (0,) * len(shp))
    return pl.pallas_call(
        functools.partial(_filt_feat_body, n_lat=n_lat),
        grid=(FFT_N2 // B_GROUP,),
        in_specs=[full((LANES, LANES)), full((1, LANES)), full((1, LANES)),
                  full((LANES, LANES)), full((1, LANES)), full((1, LANES))],
        out_specs=pl.BlockSpec((B_GROUP, na, LANES), lambda i: (i, 0, 0)),
        out_shape=jax.ShapeDtypeStruct((FFT_N2, na, LANES), F32),
        compiler_params=_cp(("arbitrary",)),
        name="hyena_filter_features",
    )(w1p, padv(b1), padv(f1), w2p, padv(b2), padv(f2))


def _filt_s1_body(hd_ref, w3_ref, dec_ref, tab_ref, o_ref):
    na = hd_ref.shape[1]
    ha = na // 2
    for j in range(B_GROUP):
        f = hd_ref[j]
        tn = f[:, HY_HID:HY_HID + 1]
        valid = f[:, HY_HID + 1:HY_HID + 2]
        top = _dot_hi(f[:ha], w3_ref[0, 0]) * jnp.exp(-tn[:ha] * jnp.abs(dec_ref[0, 0]))
        bot = _dot_hi(f[ha:], w3_ref[0, 1]) * jnp.exp(-tn[ha:] * jnp.abs(dec_ref[0, 1])) * valid[ha:]
        hb = jnp.concatenate([top, bot], axis=0).astype(BF16)
        o_ref[0, :, j, :] = _dot(tab_ref[j], hb)


def _filt_s1(hd, w3r, dec, tab, *, ct):
    _, na, _ = hd.shape
    d = w3r.shape[-1]
    return pl.pallas_call(
        _filt_s1_body,
        grid=(2, FFT_N2 // B_GROUP, d // ct),
        in_specs=[
            pl.BlockSpec((B_GROUP, na, LANES), lambda o, g, c: (g, 0, 0)),
            pl.BlockSpec((1, 2, LANES, ct), lambda o, g, c: (o, 0, 0, c)),
            pl.BlockSpec((1, 2, 1, ct), lambda o, g, c: (o, 0, 0, c)),
            pl.BlockSpec((B_GROUP, 2 * na, na), lambda o, g, c: (g, 0, 0)),
        ],
        out_specs=pl.BlockSpec((1, 2 * na, B_GROUP, ct), lambda o, g, c: (o, 0, g, c)),
        out_shape=jax.ShapeDtypeStruct((2, 2 * na, FFT_N2, d), F32),
        compiler_params=_cp(("arbitrary", "arbitrary", "arbitrary")),
        name="hyena_filter_dft1",
    )(hd, w3r, dec, tab)


def _s2_body(*refs, conv):
    if conv:
        o_ref, kf_ref, ff_ref, fi_ref, g_ref = refs
    else:
        o_ref, ff_ref, g_ref = refs
    ct = o_ref.shape[-1]
    xin = o_ref[...].reshape(2 * FFT_N2, ct).astype(BF16)
    xf = _dot(ff_ref[...], xin)
    if conv:
        xr, xi = xf[:FFT_N2], xf[FFT_N2:]
        kr = kf_ref[0, 0, 0]
        ki = kf_ref[0, 1, 0]
        y = jnp.concatenate([xr * kr - xi * ki, xr * ki + xi * kr], axis=0).astype(BF16)
        xf = _dot(fi_ref[...], y)
    g_ref[...] = xf.reshape(g_ref.shape)


def _s2(o4, kf, order, ff, fi, *, ct, conv):
    n1 = o4.shape[-3]
    d = o4.shape[-1]
    full = lambda a: pl.BlockSpec(a.shape, lambda k, c: (0,) * a.ndim)
    if conv:
        blk = pl.BlockSpec((2, 1, FFT_N2, ct), lambda k, c: (0, k, 0, c))
        in_specs = [blk, pl.BlockSpec((1, 2, 1, FFT_N2, ct), lambda k, c: (order, 0, k, 0, c)), full(ff), full(fi)]
        args = (o4, kf, ff, fi)
        grid = (n1, d // ct)
        out_specs = blk
    else:
        no = o4.shape[0]
        blk = pl.BlockSpec((1, 2, 1, FFT_N2, ct), lambda k, c: (k // n1, 0, k % n1, 0, c))
        in_specs = [blk, full(ff)]
        args = (o4, ff)
        grid = (no * n1, d // ct)
        out_specs = blk
    return pl.pallas_call(
        functools.partial(_s2_body, conv=conv),
        grid=grid,
        in_specs=in_specs,
        out_specs=out_specs,
        out_shape=jax.ShapeDtypeStruct(o4.shape, F32),
        compiler_params=_cp(("arbitrary", "arbitrary")),
        name="hyena_conv_dft2" if conv else "hyena_filter_dft2",
    )(*args)


def _s1_body(z_ref, tab_ref, o_ref):
    for j in range(B_GROUP):
        o_ref[:, j, :] = _dot(tab_ref[j], z_ref[:, j, :].astype(BF16))


def _s1(z3, tab, *, ct):
    rows, _, d = z3.shape
    n_out = tab.shape[1]
    return pl.pallas_call(
        _s1_body,
        grid=(FFT_N2 // B_GROUP, d // ct),
        in_specs=[
            pl.BlockSpec((rows, B_GROUP, ct), lambda g, c: (0, g, c)),
            pl.BlockSpec((B_GROUP, n_out, rows), lambda g, c: (g, 0, 0)),
        ],
        out_specs=pl.BlockSpec((n_out, B_GROUP, ct), lambda g, c: (0, g, c)),
        out_shape=jax.ShapeDtypeStruct((n_out, FFT_N2, d), F32),
        compiler_params=_cp(("arbitrary", "arbitrary")),
        name="hyena_conv_dft1",
    )(z3, tab)


def _s3_body(g_ref, tab_ref, gate_ref, z_ref, fb_ref, o_ref):
    for j in range(B_GROUP):
        o_ref[:, j, :] = _dot(tab_ref[j], g_ref[:, j, :].astype(BF16))
    o_ref[...] = gate_ref[...] * (o_ref[...] + z_ref[...] * fb_ref[...])


def _s3(g3, tab, gate3, z3, fb, *, ct):
    n_in, _, d = g3.shape
    rows = tab.shape[1]
    blk = pl.BlockSpec((rows, B_GROUP, ct), lambda g, c: (0, g, c))
    return pl.pallas_call(
        _s3_body,
        grid=(FFT_N2 // B_GROUP, d // ct),
        in_specs=[
            pl.BlockSpec((n_in, B_GROUP, ct), lambda g, c: (0, g, c)),
            pl.BlockSpec((B_GROUP, rows, n_in), lambda g, c: (g, 0, 0)),
            blk, blk,
            pl.BlockSpec((1, 1, ct), lambda g, c: (0, 0, c)),
        ],
        out_specs=blk,
        out_shape=jax.ShapeDtypeStruct((rows, FFT_N2, d), F32),
        compiler_params=_cp(("arbitrary", "arbitrary")),
        name="hyena_conv_idft1",
    )(g3, tab, gate3, z3, fb)


def _dft_tables(n_lat):
    n = 2 * n_lat
    n1 = n // FFT_N2
    b = jnp.arange(FFT_N2, dtype=jnp.int32)[:, None, None]
    k1 = jnp.arange(n1, dtype=jnp.int32)[None, :, None]
    a = jnp.arange(n1, dtype=jnp.int32)[None, None, :]
    th = ((k1 * (a * FFT_N2 + b)) % n).astype(F32) * (2.0 * math.pi / n)
    cr, sn = jnp.cos(th), jnp.sin(th)
    ha = n1 // 2
    crh, snh = cr[:, :, :ha], sn[:, :, :ha]
    w1 = jnp.concatenate([jnp.concatenate([crh, snh], axis=2), jnp.concatenate([-snh, crh], axis=2)], axis=1)
    w1f = jnp.concatenate([cr, -sn], axis=1)
    v = jnp.swapaxes(w1, 1, 2) * (1.0 / n)
    k2 = jnp.arange(FFT_N2, dtype=jnp.int32)
    th2 = ((k2[:, None] * k2[None, :]) % FFT_N2).astype(F32) * (2.0 * math.pi / FFT_N2)
    c2, s2 = jnp.cos(th2), jnp.sin(th2)
    ff = jnp.concatenate([jnp.concatenate([c2, s2], axis=1), jnp.concatenate([-s2, c2], axis=1)], axis=0)
    fi = jnp.concatenate([jnp.concatenate([c2, -s2], axis=1), jnp.concatenate([s2, c2], axis=1)], axis=0)
    return w1.astype(BF16), w1f.astype(BF16), v.astype(BF16), ff.astype(BF16), fi.astype(BF16)


def _hyena_mix(proj3, fparams, fbias, *, n_lat):
    _, bsz, _, d = proj3.shape
    f_w1, f_b1, f_f1, f_w2, f_b2, f_f2, f_w3, decay = fparams
    na = 2 * n_lat // FFT_N2
    w1, w1f, v, ff, fi = _dft_tables(n_lat)
    ct = min(d, 512)
    hd = _filt_feat(f_w1, f_b1, f_f1, f_w2, f_b2, f_f2, n_lat=n_lat)
    w3r = jnp.transpose(f_w3.reshape(HY_HID, 2, 2, d), (1, 2, 0, 3))
    w3r = jnp.zeros((2, 2, LANES, d), F32).at[:, :, :HY_HID].set(w3r)
    kf1 = _filt_s1(hd, w3r, decay.reshape(2, 2, 1, d), w1f, ct=ct)
    kf = _s2(kf1.reshape(2, 2, na, FFT_N2, d), None, 0, ff, None, ct=d, conv=False)
    p3 = proj3.reshape(3, bsz * (n_lat // FFT_N2), FFT_N2, d)
    z3 = p3[2]
    for o in range(2):
        o1 = _s1(z3, w1, ct=ct)
        g = _s2(o1.reshape(2, na, FFT_N2, d), kf, o, ff, fi, ct=d, conv=True)
        z3 = _s3(g.reshape(2 * na, FFT_N2, d), v, p3[o], z3, fbias[o].reshape(1, 1, d), ct=ct)
    return z3.reshape(bsz, n_lat, d)


def _rope_tables(n_tokens):
    rows = n_tokens // GRID_W
    row = jnp.broadcast_to(jnp.arange(rows, dtype=F32)[:, None], (rows, GRID_W)).reshape(-1)
    col = jnp.broadcast_to(jnp.arange(GRID_W, dtype=F32)[None, :], (rows, GRID_W)).reshape(-1)
    axis_dim = QK_ROPE // 2
    inv_freq = 1.0 / (ROPE_THETA ** (jnp.arange(0, axis_dim, 2, dtype=F32) / axis_dim))
    ang = jnp.concatenate([row[:, None] * inv_freq, col[:, None] * inv_freq], axis=-1)
    return jnp.cos(ang), jnp.sin(ang)


def _mla_weights(w_down, g_q, w_uq, g_kv, w_ukv):
    d = w_down.shape[0]
    nh = MLA_HEADS
    kpe = w_down[:, Q_LORA + KV_LORA:]
    w1, w2 = kpe[:, 0::2], kpe[:, 1::2]
    z = jnp.zeros((d, LANES - QK_ROPE), w_down.dtype)
    wd = jnp.concatenate([w_down[:, :Q_LORA + KV_LORA], w1, w2, z, w2, w1, z], axis=1).astype(BF16)
    uq = w_uq.reshape(Q_LORA, nh, QK_NOPE + QK_ROPE)
    pe = uq[:, :, QK_NOPE:]
    uq = jnp.concatenate([uq[:, :, :QK_NOPE], pe[:, :, 0::2], pe[:, :, 1::2]], axis=2)
    wuqT = uq.reshape(Q_LORA, nh * (QK_NOPE + QK_ROPE)).T.astype(BF16)
    ukv = w_ukv.reshape(KV_LORA, nh, QK_NOPE + V_DIM)
    wuk = ukv[:, :, :QK_NOPE].reshape(KV_LORA, nh * QK_NOPE).astype(BF16)
    wuvT = ukv[:, :, QK_NOPE:].reshape(KV_LORA, nh * V_DIM).T.astype(BF16)
    return wd, g_q.reshape(1, -1), g_kv.reshape(1, -1), wuk, wuqT, wuvT


def kernel(x, c, ctx, c_ctx, ada_w, ada_b, norm_mix_g, norm_ffn_g, mla_w_down, mla_g_q, mla_w_uq, mla_g_kv, mla_w_ukv, mla_w_o, hy_w_in, hy_b_in, hy_conv_w, hy_conv_b, hy_f_w1, hy_f_b1, hy_f_freq1, hy_f_w2, hy_f_b2, hy_f_freq2, hy_f_w3, hy_decay, hy_bias, hy_w_out, hy_b_out, moe_w_r, moe_b_r, moe_w_in, moe_b_in, moe_w_out, moe_b_out, final_g):
    bsz, n_lat, d = x.shape
    n_ctx = ctx.shape[1]
    depth = ada_w.shape[0]
    assert bsz == 2 and d == MLA_HEADS * V_DIM and n_lat % 512 == 0 and n_ctx % 128 == 0
    assert depth == 2

    cond8 = jnp.zeros((8, d), F32).at[:bsz].set(c).at[bsz].set(c_ctx)
    mods = _ada(cond8, ada_w, ada_b)

    def mod(i, j, rows):
        return mods[i, rows, j * d:(j + 1) * d][:, None, :]

    lat_rows = slice(0, bsz)
    ctx_rows = slice(bsz, bsz + 1)
    xl = x
    for i in range(depth):
        kind, j = i % 2, i // 2
        sh1, sc1, g1 = (mod(i, m, lat_rows) for m in range(3))
        sh2, sc2, g2 = (mod(i, m, lat_rows) for m in range(3, 6))
        gm = norm_mix_g[i].reshape(1, d)
        if kind == 0:
            wts = _mla_weights(mla_w_down[j], mla_g_q[j], mla_w_uq[j], mla_g_kv[j], mla_w_ukv[j])
            cos, sin = _rope_tables(n_lat)
            zl = jnp.zeros((n_lat, LANES - QK_ROPE), F32)
            tabs = (jnp.concatenate([cos, cos, zl], axis=1), jnp.concatenate([-sin, sin, zl], axis=1), cos.T, sin.T)
            tq = tk = 512
            qT, k, vT = _mla_proj(xl, gm, sh1, sc1, wts, tabs, need_q=True, tm=tk, tk=tk)
            half = QK_ROPE // 2
            one_c = jnp.concatenate([jnp.ones((n_ctx, QK_ROPE), F32), jnp.zeros((n_ctx, LANES - QK_ROPE), F32)], axis=1)
            tabs_c = (one_c, jnp.zeros((n_ctx, LANES), F32), jnp.ones((half, n_ctx), F32), jnp.zeros((half, n_ctx), F32))
            kc, vTc = _mla_proj(ctx, gm, mod(i, 0, ctx_rows), mod(i, 1, ctx_rows), wts, tabs_c,
                                need_q=False, tm=n_ctx, tk=n_ctx)
            o = _attention(qT, k, vT, kc, vTc, tq=tq, tk=tk)
            wo = mla_w_o[j].astype(BF16)
            bo = jnp.zeros((1, d), F32)
            transposed = True
        else:
            proj3 = _hy_in(xl, gm, sh1, sc1, hy_w_in[j].astype(BF16), hy_b_in[j].reshape(1, -1), hy_conv_w[j],
                           hy_conv_b[j].reshape(1, -1), tm=512)
            fparams = (hy_f_w1[j], hy_f_b1[j], hy_f_freq1[j], hy_f_w2[j], hy_f_b2[j], hy_f_freq2[j], hy_f_w3[j],
                       hy_decay[j])
            o = _hyena_mix(proj3, fparams, hy_bias[j], n_lat=n_lat)
            wo = hy_w_out[j].astype(BF16)
            bo = hy_b_out[j].reshape(1, d)
            transposed = False
        wr = jnp.zeros((d, LANES), F32).at[:, :N_EXPERTS].set(moe_w_r[i])
        wrh = wr.astype(BF16)
        wrl = (wr - wrh.astype(F32)).astype(BF16)
        br = jnp.zeros((1, LANES), F32).at[0, :N_EXPERTS].set(moe_b_r[i])
        xl, fl, topi, gates, rank, cnt = _post(o, wo, bo, xl, g1, norm_ffn_g[i].reshape(1, d), sh2, sc2, wrh, wrl, br,
                                               transposed=transposed, tm=512)
        xl = _moe(fl, topi, gates, rank, cnt, xl, g2, final_g.reshape(1, d), moe_w_in[i], moe_b_in[i],
                  moe_w_out[i], moe_b_out[i], final=(i == depth - 1))
    return xl
```

```python
import functools
import math

import jax
import jax.numpy as jnp
from jax import lax
from jax.experimental import pallas as pl
from jax.experimental.pallas import tpu as pltpu

F32 = jnp.float32
BF16 = jnp.bfloat16

EPS = 1e-6
GRID_W = 64
MLA_HEADS = 8
QK_NOPE = 128
QK_ROPE = 64
V_DIM = 128
Q_LORA = 512
KV_LORA = 256
ROPE_THETA = 10000.0
MLA_SCALE = (QK_NOPE + QK_ROPE) ** -0.5
QK_PAD = 256

HY_EMB = 33
HY_BANDS = (HY_EMB - 1) // 2
HY_HID = 64
FFT_N2 = 128
B_GROUP = 8

N_EXPERTS = 32
TOP_K = 4
SWIGLU_LIMIT = 7.0
SWIGLU_ALPHA = 1.702
MOE_TM = 256
LANES = 128

VMEM_LIMIT = 56 * 1024 * 1024


def _cp(sem, vmem=VMEM_LIMIT):
    return pltpu.CompilerParams(dimension_semantics=sem, vmem_limit_bytes=vmem)


def _dot(a, b):
    return jnp.dot(a, b, preferred_element_type=F32)


def _dot_hi(a, b):
    return jnp.dot(a, b, preferred_element_type=F32, precision=lax.Precision.HIGHEST)


def _rms(x, g):
    return x * lax.rsqrt(jnp.mean(x * x, axis=-1, keepdims=True) + EPS) * g


def _ada_body(c_ref, w_ref, b_ref, o_ref):
    c = c_ref[...]
    s = c * jax.nn.sigmoid(c)
    o_ref[0] = _dot(s.astype(BF16), w_ref[0].astype(BF16)) + b_ref[0]


def _ada(cond8, ada_w, ada_b):
    depth, d, n = ada_w.shape
    tn = n // 4
    return pl.pallas_call(
        _ada_body,
        grid=(depth, n // tn),
        in_specs=[
            pl.BlockSpec((8, d), lambda i, j: (0, 0)),
            pl.BlockSpec((1, d, tn), lambda i, j: (i, 0, j)),
            pl.BlockSpec((1, 1, tn), lambda i, j: (i, 0, j)),
        ],
        out_specs=pl.BlockSpec((1, 8, tn), lambda i, j: (i, 0, j)),
        out_shape=jax.ShapeDtypeStruct((depth, 8, n), F32),
        compiler_params=_cp(("arbitrary", "arbitrary")),
        name="ada_mod",
    )(cond8, ada_w, ada_b.reshape(depth, 1, n))


def _mla_proj_body(x_ref, g_ref, sh_ref, sc_ref, wd_ref, gq_ref, gkv_ref, wuk_ref, wuqT_ref, wuvT_ref,
                   ct_ref, st_ref, cT_ref, sT_ref, *out_refs, need_q, tk):
    if need_q:
        qT_ref, k_ref, vT_ref = out_refs
    else:
        k_ref, vT_ref = out_refs
    nh = MLA_HEADS
    x = x_ref[0]
    h = _rms(x, g_ref[...]) * (1.0 + sc_ref[0]) + sh_ref[0]
    lat = _dot(h.astype(BF16), wd_ref[...])
    o_kv = Q_LORA
    o_a = Q_LORA + KV_LORA
    kvn = _rms(lat[:, o_kv:o_a], gkv_ref[...])
    kr = (lat[:, o_a:o_a + LANES] * ct_ref[...] + lat[:, o_a + LANES:o_a + 2 * LANES] * st_ref[...]).astype(BF16)
    knope = _dot(kvn.astype(BF16), wuk_ref[...])
    for hh in range(nh):
        k_ref[0, hh, :, 0:QK_NOPE] = knope[:, hh * QK_NOPE:(hh + 1) * QK_NOPE].astype(BF16)
        k_ref[0, hh, :, QK_NOPE:QK_PAD] = kr
    vT = _dot(wuvT_ref[...], kvn.T.astype(BF16))
    tm = x.shape[0]
    for hh in range(nh):
        for c in range(tm // tk):
            vT_ref[0, hh, c] = vT[hh * V_DIM:(hh + 1) * V_DIM, c * tk:(c + 1) * tk].astype(BF16)
    if need_q:
        qn = _rms(lat[:, :Q_LORA], gq_ref[...])
        qT = _dot(wuqT_ref[...], qn.T.astype(BF16)) * (MLA_SCALE * math.log2(math.e))
        c = cT_ref[...]
        s = sT_ref[...]
        hw = QK_NOPE + QK_ROPE
        half = QK_ROPE // 2
        for hh in range(nh):
            base = hh * hw
            x1 = qT[base + QK_NOPE:base + QK_NOPE + half]
            x2 = qT[base + QK_NOPE + half:base + hw]
            qT_ref[0, hh, 0:QK_NOPE] = qT[base:base + QK_NOPE].astype(BF16)
            qT_ref[0, hh, QK_NOPE:QK_NOPE + half] = (x1 * c - x2 * s).astype(BF16)
            qT_ref[0, hh, QK_NOPE + half:hw] = (x1 * s + x2 * c).astype(BF16)
            qT_ref[0, hh, hw:QK_PAD] = jnp.zeros((QK_PAD - hw, tm), BF16)


def _mla_proj(x, g, sh, sc, wts, tabs, *, need_q, tm, tk):
    bsz, n, d = x.shape
    nh = MLA_HEADS
    wd, gq, gkv, wuk, wuqT, wuvT = wts
    ct, st, cT, sT = tabs
    nsh = sh.shape[0]
    full = lambda a: pl.BlockSpec(a.shape, lambda b, i: (0,) * a.ndim)
    in_specs = [
        pl.BlockSpec((1, tm, d), lambda b, i: (b, i, 0)),
        full(g),
        pl.BlockSpec((1, 1, d), lambda b, i: (b % nsh, 0, 0)),
        pl.BlockSpec((1, 1, d), lambda b, i: (b % nsh, 0, 0)),
        full(wd), full(gq), full(gkv), full(wuk), full(wuqT), full(wuvT),
        pl.BlockSpec((tm, LANES), lambda b, i: (i, 0)),
        pl.BlockSpec((tm, LANES), lambda b, i: (i, 0)),
        pl.BlockSpec((QK_ROPE // 2, tm), lambda b, i: (0, i)),
        pl.BlockSpec((QK_ROPE // 2, tm), lambda b, i: (0, i)),
    ]
    out_specs = [
        pl.BlockSpec((1, nh, tm, QK_PAD), lambda b, i: (b, 0, i, 0)),
        pl.BlockSpec((1, nh, tm // tk, V_DIM, tk), lambda b, i: (b, 0, i, 0, 0)),
    ]
    out_shape = [
        jax.ShapeDtypeStruct((bsz, nh, n, QK_PAD), BF16),
        jax.ShapeDtypeStruct((bsz, nh, n // tk, V_DIM, tk), BF16),
    ]
    if need_q:
        out_specs = [pl.BlockSpec((1, nh, QK_PAD, tm), lambda b, i: (b, 0, 0, i))] + out_specs
        out_shape = [jax.ShapeDtypeStruct((bsz, nh, QK_PAD, n), BF16)] + out_shape
    return pl.pallas_call(
        functools.partial(_mla_proj_body, need_q=need_q, tk=tk),
        grid=(bsz, n // tm),
        in_specs=in_specs,
        out_specs=out_specs,
        out_shape=out_shape,
        compiler_params=_cp(("arbitrary", "arbitrary")),
        name="mla_proj_q" if need_q else "mla_proj_ctx",
    )(x, g, sh, sc, wd, gq, gkv, wuk, wuqT, wuvT, ct, st, cT, sT)


ACC_ROWS = V_DIM + 16
SM_STRIP = 64


def _attn_body(qT_ref, k_ref, vT_ref, kc_ref, vTc_ref, o_ref, s0, s1, p0, p1, sc, pc, acc, m_scr, *, tk):
    nchunk = k_ref.shape[2] // tk

    def scores(kblk):
        return _dot(kblk, qT_ref[0, 0])

    def softmax_pv(s_ref, p_ref, vblk):
        nk = s_ref.shape[0]
        m_old = m_scr[...]
        m_new = jnp.maximum(m_old, jnp.max(s_ref[...], axis=0, keepdims=True))
        m_scr[...] = m_new
        alpha = jnp.exp2(m_old - m_new)
        for r in range(0, nk, SM_STRIP):
            p_ref[r:r + SM_STRIP] = jnp.exp2(s_ref[r:r + SM_STRIP] - m_new).astype(BF16)
        lhs = jnp.concatenate([vblk, jnp.ones((ACC_ROWS - V_DIM, nk), BF16)], axis=0)
        acc[...] = alpha * acc[...] + _dot(lhs, p_ref[...])

    def kchunk(i):
        return k_ref[0, 0, pl.ds(pl.multiple_of(i * tk, tk), tk), :]

    m_scr[...] = jnp.full(m_scr.shape, -jnp.inf, F32)
    acc[...] = jnp.zeros(acc.shape, F32)
    sc[...] = scores(kc_ref[0, 0])
    s0[...] = scores(kchunk(0))
    softmax_pv(sc, pc, vTc_ref[0, 0, 0])

    def body(j, carry):
        i0 = 2 * j
        s1[...] = scores(kchunk(i0 + 1))
        softmax_pv(s0, p0, vT_ref[0, 0, i0])
        s0[...] = scores(kchunk(jnp.minimum(i0 + 2, nchunk - 1)))
        softmax_pv(s1, p1, vT_ref[0, 0, i0 + 1])
        return carry

    lax.fori_loop(0, nchunk // 2, body, 0)
    o_ref[0, 0] = (acc[0:V_DIM] / acc[V_DIM:V_DIM + 1]).astype(BF16)


def _attention(qT, k, vT, kc, vTc, *, tq, tk):
    bsz, nh, _, n = qT.shape
    nc = kc.shape[2]
    assert (n // tk) % 2 == 0
    return pl.pallas_call(
        functools.partial(_attn_body, tk=tk),
        grid=(bsz, nh, n // tq),
        in_specs=[
            pl.BlockSpec((1, 1, QK_PAD, tq), lambda b, h, i: (b, h, 0, i)),
            pl.BlockSpec((1, 1, n, QK_PAD), lambda b, h, i: (b, h, 0, 0)),
            pl.BlockSpec((1, 1, n // tk, V_DIM, tk), lambda b, h, i: (b, h, 0, 0, 0)),
            pl.BlockSpec((1, 1, nc, QK_PAD), lambda b, h, i: (b, h, 0, 0)),
            pl.BlockSpec((1, 1, 1, V_DIM, nc), lambda b, h, i: (b, h, 0, 0, 0)),
        ],
        out_specs=pl.BlockSpec((1, 1, V_DIM, tq), lambda b, h, i: (b, h, 0, i)),
        out_shape=jax.ShapeDtypeStruct((bsz, nh, V_DIM, n), BF16),
        scratch_shapes=[pltpu.VMEM((tk, tq), F32), pltpu.VMEM((tk, tq), F32),
                        pltpu.VMEM((tk, tq), BF16), pltpu.VMEM((tk, tq), BF16),
                        pltpu.VMEM((nc, tq), F32), pltpu.VMEM((nc, tq), BF16),
                        pltpu.VMEM((ACC_ROWS, tq), F32), pltpu.VMEM((1, tq), F32)],
        compiler_params=_cp(("arbitrary", "arbitrary", "arbitrary")),
        name="mla_attention",
    )(qT, k, vT, kc, vTc)


def _post_body(o_ref, wo_ref, bo_ref, x_ref, g1_ref, gf_ref, sh_ref, sc_ref, wrh_ref, wrl_ref, br_ref, tri_ref,
               xl_ref, fl_ref, ti_ref, gt_ref, rk_ref, cnt_ref, *, transposed):
    @pl.when((pl.program_id(0) == 0) & (pl.program_id(1) == 0))
    def _():
        cnt_ref[...] = jnp.zeros_like(cnt_ref)

    tm = x_ref.shape[1]
    if transposed:
        oT = o_ref[0].astype(F32).reshape(MLA_HEADS * V_DIM, tm)
        o = oT.T.astype(BF16)
    else:
        o = o_ref[0].astype(BF16)
    y = _dot(o, wo_ref[...]) + bo_ref[...]
    xl = x_ref[0] + g1_ref[0] * y
    xl_ref[0] = xl
    fl = _rms(xl, gf_ref[...]) * (1.0 + sc_ref[0]) + sh_ref[0]
    fl_ref[...] = fl
    flh = fl.astype(BF16)
    fll = (fl - flh.astype(F32)).astype(BF16)
    logits = _dot(flh, wrh_ref[...]) + (_dot(fll, wrh_ref[...]) + _dot(flh, wrl_ref[...])) + br_ref[...]
    lane = lax.broadcasted_iota(jnp.int32, (tm, LANES), 1).astype(F32)
    neg = jnp.float32(-jnp.inf)
    work = jnp.where(lane < N_EXPERTS, logits, neg)
    vals, idxs = [], []
    onehot = jnp.zeros((tm, LANES), F32)
    for _ in range(TOP_K):
        mk = jnp.max(work, axis=-1, keepdims=True)
        ik = jnp.min(jnp.where(work == mk, lane, float(LANES)), axis=-1, keepdims=True)
        sel = lane == ik
        onehot = jnp.where(sel, 1.0, onehot)
        work = jnp.where(sel, neg, work)
        vals.append(mk)
        idxs.append(ik)
    es = [jnp.exp(v - vals[0]) for v in vals]
    den = es[0] + es[1] + es[2] + es[3]
    pre = _dot(tri_ref[...], onehot.astype(BF16)) + cnt_ref[...]
    ti = jnp.zeros((tm, LANES), F32)
    gt = jnp.zeros((tm, LANES), F32)
    rk = jnp.zeros((tm, LANES), F32)
    for kk in range(TOP_K):
        rank = jnp.sum(jnp.where(lane == idxs[kk], pre, 0.0), axis=-1, keepdims=True)
        ti = jnp.where(lane == kk, idxs[kk], ti)
        gt = jnp.where(lane == kk, es[kk] / den, gt)
        rk = jnp.where(lane == kk, rank, rk)
    ti_ref[...] = ti[:, :TOP_K].astype(jnp.int32)
    gt_ref[...] = gt[:, :TOP_K]
    rk_ref[...] = rk[:, :TOP_K].astype(jnp.int32)
    cnt_ref[...] += jnp.sum(onehot, axis=0, keepdims=True)


def _post(o, wo, bo, x, g1, gf, sh, sc, wrh, wrl, br, *, transposed, tm):
    bsz, n, d = x.shape
    t = bsz * n
    nt = n // tm
    tri = (lax.broadcasted_iota(jnp.int32, (tm, tm), 0) > lax.broadcasted_iota(jnp.int32, (tm, tm), 1)).astype(BF16)
    full = lambda a: pl.BlockSpec(a.shape, lambda b, i: (0,) * a.ndim)
    per_b = pl.BlockSpec((1, 1, d), lambda b, i: (b, 0, 0))
    if transposed:
        o_spec = pl.BlockSpec((1, MLA_HEADS, V_DIM, tm), lambda b, i: (b, 0, 0, i))
    else:
        o_spec = pl.BlockSpec((1, tm, d), lambda b, i: (b, i, 0))
    tok = lambda w: pl.BlockSpec((tm, w), lambda b, i: (b * nt + i, 0))
    return pl.pallas_call(
        functools.partial(_post_body, transposed=transposed),
        grid=(bsz, nt),
        in_specs=[o_spec, full(wo), full(bo), pl.BlockSpec((1, tm, d), lambda b, i: (b, i, 0)), per_b, full(gf),
                  per_b, per_b, full(wrh), full(wrl), full(br), full(tri)],
        out_specs=[pl.BlockSpec((1, tm, d), lambda b, i: (b, i, 0)), tok(d), tok(TOP_K), tok(TOP_K), tok(TOP_K),
                   pl.BlockSpec((1, LANES), lambda b, i: (0, 0))],
        out_shape=[jax.ShapeDtypeStruct((bsz, n, d), F32), jax.ShapeDtypeStruct((t, d), F32),
                   jax.ShapeDtypeStruct((t, TOP_K), jnp.int32), jax.ShapeDtypeStruct((t, TOP_K), F32),
                   jax.ShapeDtypeStruct((t, TOP_K), jnp.int32), jax.ShapeDtypeStruct((1, LANES), F32)],
        compiler_params=_cp(("arbitrary", "arbitrary")),
        name="post_attn" if transposed else "post_hyena",
    )(o, wo, bo, x, g1, gf, sh, sc, wrh, wrl, br, tri)


def _dispatch_body(dest_ref, fl_ref, xs_in, xs_out, sem, *, td):
    del xs_in

    def issue(t, carry):
        for kk in range(TOP_K):
            d = dest_ref[0, 0, t * TOP_K + kk]
            pltpu.make_async_copy(fl_ref.at[pl.ds(t, 1)], xs_out.at[pl.ds(d, 1)], sem).start()
        return carry

    lax.fori_loop(0, td, issue, 0, unroll=2)

    def drain(t, carry):
        pltpu.make_async_copy(fl_ref.at[pl.ds(0, 1)], xs_out.at[pl.ds(0, 1)], sem).wait()
        return carry

    lax.fori_loop(0, td * TOP_K, drain, 0, unroll=8)


def _dispatch(dest, fl, n_rows, *, td):
    t, d = fl.shape
    dest3 = dest.reshape(t // td, 1, td * TOP_K)
    xs0 = jnp.zeros((n_rows, d), fl.dtype)
    return pl.pallas_call(
        functools.partial(_dispatch_body, td=td),
        grid=(t // td,),
        in_specs=[
            pl.BlockSpec((1, 1, td * TOP_K), lambda i: (i, 0, 0), memory_space=pltpu.SMEM),
            pl.BlockSpec((td, d), lambda i: (i, 0)),
            pl.BlockSpec(memory_space=pl.ANY),
        ],
        out_specs=pl.BlockSpec(memory_space=pl.ANY),
        out_shape=jax.ShapeDtypeStruct((n_rows, d), fl.dtype),
        scratch_shapes=[pltpu.SemaphoreType.DMA(())],
        input_output_aliases={2: 0},
        compiler_params=_cp(("arbitrary",)),
        name="moe_dispatch",
    )(dest3, fl, xs0)


def _expert_body(be_ref, nu_ref, xs_ref, win_ref, bin_ref, wout_ref, bout_ref, ys_ref, win_s, wout_s):
    b = pl.program_id(0)
    dff = wout_ref.shape[1]

    @pl.when(b < nu_ref[0])
    def _():
        prev = be_ref[jnp.maximum(b - 1, 0)]

        @pl.when((b == 0) | (prev != be_ref[b]))
        def _():
            win_s[...] = win_ref[0].astype(BF16)
            wout_s[...] = wout_ref[0].astype(BF16)

        x = xs_ref[...].astype(BF16)
        gu = _dot(x, win_s[...]) + bin_ref[0]
        gate = jnp.minimum(gu[:, :dff], SWIGLU_LIMIT)
        lin = jnp.clip(gu[:, dff:], -SWIGLU_LIMIT, SWIGLU_LIMIT)
        act = gate * jax.nn.sigmoid(SWIGLU_ALPHA * gate) * (lin + 1.0)
        ys_ref[...] = _dot(act.astype(BF16), wout_s[...]) + bout_ref[0]

    @pl.when(b >= nu_ref[0])
    def _():
        ys_ref[...] = jnp.zeros_like(ys_ref)


def _experts(blk_exp, n_used, xs, w_in, b_in, w_out, b_out):
    n_rows, d = xs.shape
    ne, _, f2 = w_in.shape
    dff = w_out.shape[1]
    tm = MOE_TM
    grid_spec = pltpu.PrefetchScalarGridSpec(
        num_scalar_prefetch=2,
        grid=(n_rows // tm,),
        in_specs=[
            pl.BlockSpec((tm, d), lambda b, be, nu: (b, 0)),
            pl.BlockSpec((1, d, f2), lambda b, be, nu: (be[b], 0, 0)),
            pl.BlockSpec((1, 1, f2), lambda b, be, nu: (be[b], 0, 0)),
            pl.BlockSpec((1, dff, d), lambda b, be, nu: (be[b], 0, 0)),
            pl.BlockSpec((1, 1, d), lambda b, be, nu: (be[b], 0, 0)),
        ],
        out_specs=pl.BlockSpec((tm, d), lambda b, be, nu: (b, 0)),
        scratch_shapes=[pltpu.VMEM((d, f2), BF16), pltpu.VMEM((dff, d), BF16)],
    )
    return pl.pallas_call(
        _expert_body,
        grid_spec=grid_spec,
        out_shape=jax.ShapeDtypeStruct((n_rows, d), F32),
        compiler_params=_cp(("arbitrary",)),
        name="moe_experts",
    )(blk_exp, n_used, xs, w_in, b_in.reshape(ne, 1, f2), w_out, b_out.reshape(ne, 1, d))


def _combine_body(dest_ref, ys_hbm, gt_ref, xl_ref, g2_ref, fg_ref, out_ref, buf, sem, *, tc, final):
    def issue(t, carry):
        for kk in range(TOP_K):
            d = dest_ref[0, 0, t * TOP_K + kk]
            pltpu.make_async_copy(ys_hbm.at[pl.ds(d, 1)], buf.at[pl.ds(kk * tc + t, 1)], sem).start()
        return carry

    lax.fori_loop(0, tc, issue, 0, unroll=2)

    def drain(t, carry):
        pltpu.make_async_copy(ys_hbm.at[pl.ds(0, 1)], buf.at[pl.ds(0, 1)], sem).wait()
        return carry

    lax.fori_loop(0, tc * TOP_K, drain, 0, unroll=8)
    gt = gt_ref[...]
    y = gt[:, 0:1] * buf[0:tc]
    for kk in range(1, TOP_K):
        y = y + gt[:, kk:kk + 1] * buf[kk * tc:(kk + 1) * tc]
    xl = xl_ref[0] + g2_ref[0] * y
    out_ref[0] = _rms(xl, fg_ref[...]) if final else xl


def _combine(dest, ys, gates, xl, g2, fg, *, tc, final):
    bsz, n, d = xl.shape
    t = bsz * n
    nt = n // tc
    dest3 = dest.reshape(t // tc, 1, tc * TOP_K)
    return pl.pallas_call(
        functools.partial(_combine_body, tc=tc, final=final),
        grid=(bsz, nt),
        in_specs=[
            pl.BlockSpec((1, 1, tc * TOP_K), lambda b, i: (b * nt + i, 0, 0), memory_space=pltpu.SMEM),
            pl.BlockSpec(memory_space=pl.ANY),
            pl.BlockSpec((tc, TOP_K), lambda b, i: (b * nt + i, 0)),
            pl.BlockSpec((1, tc, d), lambda b, i: (b, i, 0)),
            pl.BlockSpec((1, 1, d), lambda b, i: (b, 0, 0)),
            pl.BlockSpec((1, d), lambda b, i: (0, 0)),
        ],
        out_specs=pl.BlockSpec((1, tc, d), lambda b, i: (b, i, 0)),
        out_shape=jax.ShapeDtypeStruct((bsz, n, d), F32),
        scratch_shapes=[pltpu.VMEM((TOP_K * tc, d), F32), pltpu.SemaphoreType.DMA(())],
        compiler_params=_cp(("arbitrary", "arbitrary")),
        name="moe_combine",
    )(dest3, ys, gates, xl, g2, fg)


def _moe(fl, topi, gates, rank, cnt, xl, g2, fg, w_in, b_in, w_out, b_out, *, final):
    t, d = fl.shape
    tm = MOE_TM
    counts = cnt[0, :N_EXPERTS].astype(jnp.int32)
    padded = (counts + tm - 1) // tm * tm
    pad_end = jnp.cumsum(padded)
    pad_start = pad_end - padded
    dest = jnp.take(pad_start, topi) + rank
    nb = t * TOP_K // tm + N_EXPERTS
    blk_start = jnp.arange(nb, dtype=jnp.int32) * tm
    blk_exp = jnp.minimum(jnp.sum((pad_end[None, :] <= blk_start[:, None]).astype(jnp.int32), axis=1), N_EXPERTS - 1)
    n_used = (pad_end[-1:] // tm).astype(jnp.int32)
    xs = _dispatch(dest, fl, nb * tm, td=256)
    ys = _experts(blk_exp, n_used, xs, w_in, b_in, w_out, b_out)
    return _combine(dest, ys, gates, xl, g2, fg, tc=256, final=final)


def _hy_in_body(x_ref, xp_ref, xn_ref, g_ref, sh_ref, sc_ref, w_ref, b_ref, cw_ref, cb_ref, o_ref, *, nt):
    i = pl.program_id(2)
    w = w_ref[...]

    def proj(xx):
        h = _rms(xx, g_ref[...]) * (1.0 + sc_ref[0]) + sh_ref[0]
        return _dot(h.astype(BF16), w) + b_ref[...]

    p = proj(x_ref[0])
    tm = p.shape[0]
    ph = proj(jnp.concatenate([xp_ref[0], xn_ref[0]], axis=0))
    prev = jnp.where(i > 0, ph[7:8], 0.0)
    nxt = jnp.where(i < nt - 1, ph[8:9], 0.0)
    row = lax.broadcasted_iota(jnp.int32, (tm, 1), 0)
    up = jnp.where(row == 0, prev, pltpu.roll(p, 1, axis=0))
    dn = jnp.where(row == tm - 1, nxt, pltpu.roll(p, tm - 1, axis=0))
    cw = cw_ref[...]
    o_ref[0, 0] = up * cw[0:1] + p * cw[1:2] + dn * cw[2:3] + cb_ref[...]


def _hy_in(x, g, sh, sc, w, b, cw, cb, *, tm):
    bsz, n, d = x.shape
    nt = n // tm
    hb = tm // 8
    per_b = pl.BlockSpec((1, 1, d), lambda j, bb, i: (bb, 0, 0))
    return pl.pallas_call(
        functools.partial(_hy_in_body, nt=nt),
        grid=(3, bsz, nt),
        in_specs=[
            pl.BlockSpec((1, tm, d), lambda j, bb, i: (bb, i, 0)),
            pl.BlockSpec((1, 8, d), lambda j, bb, i: (bb, jnp.maximum(i * hb - 1, 0), 0)),
            pl.BlockSpec((1, 8, d), lambda j, bb, i: (bb, jnp.minimum((i + 1) * hb, n // 8 - 1), 0)),
            pl.BlockSpec((1, d), lambda j, bb, i: (0, 0)),
            per_b, per_b,
            pl.BlockSpec((d, d), lambda j, bb, i: (0, j)),
            pl.BlockSpec((1, d), lambda j, bb, i: (0, j)),
            pl.BlockSpec((3, d), lambda j, bb, i: (0, j)),
            pl.BlockSpec((1, d), lambda j, bb, i: (0, j)),
        ],
        out_specs=pl.BlockSpec((1, 1, tm, d), lambda j, bb, i: (j, bb, i, 0)),
        out_shape=jax.ShapeDtypeStruct((3, bsz, n, d), F32),
        compiler_params=_cp(("arbitrary", "arbitrary", "arbitrary")),
        name="hyena_in_proj",
    )(x, x, x, g, sh, sc, w, b, cw, cb)


def _filt_feat_body(w1_ref, b1_ref, f1_ref, w2_ref, b2_ref, f2_ref, o_ref, *, n_lat):
    na = o_ref.shape[1]
    a = lax.broadcasted_iota(jnp.int32, (na, 1), 0)
    lane = lax.broadcasted_iota(jnp.int32, (na, LANES), 1)
    band_idx = jnp.where(lane <= HY_BANDS, lane - 1, lane - 1 - HY_BANDS).astype(F32)
    band = 1e-4 + band_idx * ((HY_BANDS - 1 - 1e-4) / (HY_BANDS - 1))
    for j in range(B_GROUP):
        r = a * FFT_N2 + (pl.program_id(0) * B_GROUP + j)
        pos = jnp.where(r < n_lat, r, 2 * n_lat - r).astype(F32)
        tn = pos / float(max(n_lat - 1, 1))
        ang = ((2.0 * math.pi / n_lat) * pos) * band
        z = jnp.where(lane == 0, tn, jnp.where(lane <= HY_BANDS, jnp.cos(ang),
                                               jnp.where(lane < HY_EMB, -jnp.sin(ang), 0.0)))
        h1 = jnp.sin(f1_ref[...] * (_dot_hi(z, w1_ref[...]) + b1_ref[...]))
        h2 = jnp.sin(f2_ref[...] * (_dot_hi(h1, w2_ref[...]) + b2_ref[...]))
        valid = (r != n_lat).astype(F32)
        o_ref[j] = jnp.where(lane == HY_HID, tn, jnp.where(lane == HY_HID + 1, valid, h2))


def _filt_feat(w1, b1, f1, w2, b2, f2, *, n_lat):
    na = 2 * n_lat // FFT_N2
    w1p = jnp.zeros((LANES, LANES), F32).at[:HY_EMB, :HY_HID].set(w1)
    w2p = jnp.zeros((LANES, LANES), F32).at[:HY_HID, :HY_HID].set(w2)
    padv = lambda v: jnp.zeros((1, LANES), F32).at[0, :HY_HID].set(v)
    full = lambda shp: pl.BlockSpec(shp, lambda i: (0,) * len(shp))
    return pl.pallas_call(
        functools.partial(_filt_feat_body, n_lat=n_lat),
        grid=(FFT_N2 // B_GROUP,),
        in_specs=[full((LANES, LANES)), full((1, LANES)), full((1, LANES)),
                  full((LANES, LANES)), full((1, LANES)), full((1, LANES))],
        out_specs=pl.BlockSpec((B_GROUP, na, LANES), lambda i: (i, 0, 0)),
        out_shape=jax.ShapeDtypeStruct((FFT_N2, na, LANES), F32),
        compiler_params=_cp(("arbitrary",)),
        name="hyena_filter_features",
    )(w1p, padv(b1), padv(f1), w2p, padv(b2), padv(f2))


def _filt_s1_body(hd_ref, w3_ref, dec_ref, tab_ref, o_ref):
    na = hd_ref.shape[1]
    ha = na // 2
    for j in range(B_GROUP):
        f = hd_ref[j]
        tn = f[:, HY_HID:HY_HID + 1]
        valid = f[:, HY_HID + 1:HY_HID + 2]
        top = _dot_hi(f[:ha], w3_ref[0, 0]) * jnp.exp(-tn[:ha] * jnp.abs(dec_ref[0, 0]))
        bot = _dot_hi(f[ha:], w3_ref[0, 1]) * jnp.exp(-tn[ha:] * jnp.abs(dec_ref[0, 1])) * valid[ha:]
        hb = jnp.concatenate([top, bot], axis=0).astype(BF16)
        o_ref[0, :, j, :] = _dot(tab_ref[j], hb)


def _filt_s1(hd, w3r, dec, tab, *, ct):
    _, na, _ = hd.shape
    d = w3r.shape[-1]
    return pl.pallas_call(
        _filt_s1_body,
        grid=(2, FFT_N2 // B_GROUP, d // ct),
        in_specs=[
            pl.BlockSpec((B_GROUP, na, LANES), lambda o, g, c: (g, 0, 0)),
            pl.BlockSpec((1, 2, LANES, ct), lambda o, g, c: (o, 0, 0, c)),
            pl.BlockSpec((1, 2, 1, ct), lambda o, g, c: (o, 0, 0, c)),
            pl.BlockSpec((B_GROUP, 2 * na, na), lambda o, g, c: (g, 0, 0)),
        ],
        out_specs=pl.BlockSpec((1, 2 * na, B_GROUP, ct), lambda o, g, c: (o, 0, g, c)),
        out_shape=jax.ShapeDtypeStruct((2, 2 * na, FFT_N2, d), F32),
        compiler_params=_cp(("arbitrary", "arbitrary", "arbitrary")),
        name="hyena_filter_dft1",
    )(hd, w3r, dec, tab)


def _s2_body(*refs, conv):
    if conv:
        o_ref, kf_ref, ff_ref, fi_ref, g_ref = refs
    else:
        o_ref, ff_ref, g_ref = refs
    ct = o_ref.shape[-1]
    xin = o_ref[...].reshape(2 * FFT_N2, ct).astype(BF16)
    xf = _dot(ff_ref[...], xin)
    if conv:
        xr, xi = xf[:FFT_N2], xf[FFT_N2:]
        kr = kf_ref[0, 0, 0]
        ki = kf_ref[0, 1, 0]
        y = jnp.concatenate([xr * kr - xi * ki, xr * ki + xi * kr], axis=0).astype(BF16)
        xf = _dot(fi_ref[...], y)
    g_ref[...] = xf.reshape(g_ref.shape)


def _s2(o4, kf, order, ff, fi, *, ct, conv):
    n1 = o4.shape[-3]
    d = o4.shape[-1]
    full = lambda a: pl.BlockSpec(a.shape, lambda k, c: (0,) * a.ndim)
    if conv:
        blk = pl.BlockSpec((2, 1, FFT_N2, ct), lambda k, c: (0, k, 0, c))
        in_specs = [blk, pl.BlockSpec((1, 2, 1, FFT_N2, ct), lambda k, c: (order, 0, k, 0, c)), full(ff), full(fi)]
        args = (o4, kf, ff, fi)
        grid = (n1, d // ct)
        out_specs = blk
    else:
        no = o4.shape[0]
        blk = pl.BlockSpec((1, 2, 1, FFT_N2, ct), lambda k, c: (k // n1, 0, k % n1, 0, c))
        in_specs = [blk, full(ff)]
        args = (o4, ff)
        grid = (no * n1, d // ct)
        out_specs = blk
    return pl.pallas_call(
        functools.partial(_s2_body, conv=conv),
        grid=grid,
        in_specs=in_specs,
        out_specs=out_specs,
        out_shape=jax.ShapeDtypeStruct(o4.shape, F32),
        compiler_params=_cp(("arbitrary", "arbitrary")),
        name="hyena_conv_dft2" if conv else "hyena_filter_dft2",
    )(*args)


def _s1_body(z_ref, tab_ref, o_ref):
    for j in range(B_GROUP):
        o_ref[:, j, :] = _dot(tab_ref[j], z_ref[:, j, :].astype(BF16))


def _s1(z3, tab, *, ct):
    rows, _, d = z3.shape
    n_out = tab.shape[1]
    return pl.pallas_call(
        _s1_body,
        grid=(FFT_N2 // B_GROUP, d // ct),
        in_specs=[
            pl.BlockSpec((rows, B_GROUP, ct), lambda g, c: (0, g, c)),
            pl.BlockSpec((B_GROUP, n_out, rows), lambda g, c: (g, 0, 0)),
        ],
        out_specs=pl.BlockSpec((n_out, B_GROUP, ct), lambda g, c: (0, g, c)),
        out_shape=jax.ShapeDtypeStruct((n_out, FFT_N2, d), F32),
        compiler_params=_cp(("arbitrary", "arbitrary")),
        name="hyena_conv_dft1",
    )(z3, tab)


def _s3_body(g_ref, tab_ref, gate_ref, z_ref, fb_ref, o_ref):
    for j in range(B_GROUP):
        o_ref[:, j, :] = _dot(tab_ref[j], g_ref[:, j, :].astype(BF16))
    o_ref[...] = gate_ref[...] * (o_ref[...] + z_ref[...] * fb_ref[...])


def _s3(g3, tab, gate3, z3, fb, *, ct):
    n_in, _, d = g3.shape
    rows = tab.shape[1]
    blk = pl.BlockSpec((rows, B_GROUP, ct), lambda g, c: (0, g, c))
    return pl.pallas_call(
        _s3_body,
        grid=(FFT_N2 // B_GROUP, d // ct),
        in_specs=[
            pl.BlockSpec((n_in, B_GROUP, ct), lambda g, c: (0, g, c)),
            pl.BlockSpec((B_GROUP, rows, n_in), lambda g, c: (g, 0, 0)),
            blk, blk,
            pl.BlockSpec((1, 1, ct), lambda g, c: (0, 0, c)),
        ],
        out_specs=blk,
        out_shape=jax.ShapeDtypeStruct((rows, FFT_N2, d), F32),
        compiler_params=_cp(("arbitrary", "arbitrary")),
        name="hyena_conv_idft1",
    )(g3, tab, gate3, z3, fb)


def _dft_tables(n_lat):
    n = 2 * n_lat
    n1 = n // FFT_N2
    b = jnp.arange(FFT_N2, dtype=jnp.int32)[:, None, None]
    k1 = jnp.arange(n1, dtype=jnp.int32)[None, :, None]
    a = jnp.arange(n1, dtype=jnp.int32)[None, None, :]
    th = ((k1 * (a * FFT_N2 + b)) % n).astype(F32) * (2.0 * math.pi / n)
    cr, sn = jnp.cos(th), jnp.sin(th)
    ha = n1 // 2
    crh, snh = cr[:, :, :ha], sn[:, :, :ha]
    w1 = jnp.concatenate([jnp.concatenate([crh, snh], axis=2), jnp.concatenate([-snh, crh], axis=2)], axis=1)
    w1f = jnp.concatenate([cr, -sn], axis=1)
    v = jnp.swapaxes(w1, 1, 2) * (1.0 / n)
    k2 = jnp.arange(FFT_N2, dtype=jnp.int32)
    th2 = ((k2[:, None] * k2[None, :]) % FFT_N2).astype(F32) * (2.0 * math.pi / FFT_N2)
    c2, s2 = jnp.cos(th2), jnp.sin(th2)
    ff = jnp.concatenate([jnp.concatenate([c2, s2], axis=1), jnp.concatenate([-s2, c2], axis=1)], axis=0)
    fi = jnp.concatenate([jnp.concatenate([c2, -s2], axis=1), jnp.concatenate([s2, c2], axis=1)], axis=0)
    return w1.astype(BF16), w1f.astype(BF16), v.astype(BF16), ff.astype(BF16), fi.astype(BF16)


def _hyena_mix(proj3, fparams, fbias, *, n_lat):
    _, bsz, _, d = proj3.shape
    f_w1, f_b1, f_f1, f_w2, f_b2, f_f2, f_w3, decay = fparams
    na = 2 * n_lat // FFT_N2
    w1, w1f, v, ff, fi = _dft_tables(n_lat)
    ct = min(d, 512)
    hd = _filt_feat(f_w1, f_b1, f_f1, f_w2, f_b2, f_f2, n_lat=n_lat)
    w3r = jnp.transpose(f_w3.reshape(HY_HID, 2, 2, d), (1, 2, 0, 3))
    w3r = jnp.zeros((2, 2, LANES, d), F32).at[:, :, :HY_HID].set(w3r)
    kf1 = _filt_s1(hd, w3r, decay.reshape(2, 2, 1, d), w1f, ct=ct)
    kf = _s2(kf1.reshape(2, 2, na, FFT_N2, d), None, 0, ff, None, ct=d, conv=False)
    p3 = proj3.reshape(3, bsz * (n_lat // FFT_N2), FFT_N2, d)
    z3 = p3[2]
    for o in range(2):
        o1 = _s1(z3, w1, ct=ct)
        g = _s2(o1.reshape(2, na, FFT_N2, d), kf, o, ff, fi, ct=d, conv=True)
        z3 = _s3(g.reshape(2 * na, FFT_N2, d), v, p3[o], z3, fbias[o].reshape(1, 1, d), ct=ct)
    return z3.reshape(bsz, n_lat, d)


def _rope_tables(n_tokens):
    rows = n_tokens // GRID_W
    row = jnp.broadcast_to(jnp.arange(rows, dtype=F32)[:, None], (rows, GRID_W)).reshape(-1)
    col = jnp.broadcast_to(jnp.arange(GRID_W, dtype=F32)[None, :], (rows, GRID_W)).reshape(-1)
    axis_dim = QK_ROPE // 2
    inv_freq = 1.0 / (ROPE_THETA ** (jnp.arange(0, axis_dim, 2, dtype=F32) / axis_dim))
    ang = jnp.concatenate([row[:, None] * inv_freq, col[:, None] * inv_freq], axis=-1)
    return jnp.cos(ang), jnp.sin(ang)


def _mla_weights(w_down, g_q, w_uq, g_kv, w_ukv):
    d = w_down.shape[0]
    nh = MLA_HEADS
    kpe = w_down[:, Q_LORA + KV_LORA:]
    w1, w2 = kpe[:, 0::2], kpe[:, 1::2]
    z = jnp.zeros((d, LANES - QK_ROPE), w_down.dtype)
    wd = jnp.concatenate([w_down[:, :Q_LORA + KV_LORA], w1, w2, z, w2, w1, z], axis=1).astype(BF16)
    uq = w_uq.reshape(Q_LORA, nh, QK_NOPE + QK_ROPE)
    pe = uq[:, :, QK_NOPE:]
    uq = jnp.concatenate([uq[:, :, :QK_NOPE], pe[:, :, 0::2], pe[:, :, 1::2]], axis=2)
    wuqT = uq.reshape(Q_LORA, nh * (QK_NOPE + QK_ROPE)).T.astype(BF16)
    ukv = w_ukv.reshape(KV_LORA, nh, QK_NOPE + V_DIM)
    wuk = ukv[:, :, :QK_NOPE].reshape(KV_LORA, nh * QK_NOPE).astype(BF16)
    wuvT = ukv[:, :, QK_NOPE:].reshape(KV_LORA, nh * V_DIM).T.astype(BF16)
    return wd, g_q.reshape(1, -1), g_kv.reshape(1, -1), wuk, wuqT, wuvT


def kernel(x, c, ctx, c_ctx, ada_w, ada_b, norm_mix_g, norm_ffn_g, mla_w_down, mla_g_q, mla_w_uq, mla_g_kv, mla_w_ukv, mla_w_o, hy_w_in, hy_b_in, hy_conv_w, hy_conv_b, hy_f_w1, hy_f_b1, hy_f_freq1, hy_f_w2, hy_f_b2, hy_f_freq2, hy_f_w3, hy_decay, hy_bias, hy_w_out, hy_b_out, moe_w_r, moe_b_r, moe_w_in, moe_b_in, moe_w_out, moe_b_out, final_g):
    bsz, n_lat, d = x.shape
    n_ctx = ctx.shape[1]
    depth = ada_w.shape[0]
    assert bsz == 2 and d == MLA_HEADS * V_DIM and n_lat % 512 == 0 and n_ctx % 128 == 0
    assert depth == 2

    cond8 = jnp.zeros((8, d), F32).at[:bsz].set(c).at[bsz].set(c_ctx)
    mods = _ada(cond8, ada_w, ada_b)

    def mod(i, j, rows):
        return mods[i, rows, j * d:(j + 1) * d][:, None, :]

    lat_rows = slice(0, bsz)
    ctx_rows = slice(bsz, bsz + 1)
    xl = x
    for i in range(depth):
        kind, j = i % 2, i // 2
        sh1, sc1, g1 = (mod(i, m, lat_rows) for m in range(3))
        sh2, sc2, g2 = (mod(i, m, lat_rows) for m in range(3, 6))
        gm = norm_mix_g[i].reshape(1, d)
        if kind == 0:
            wts = _mla_weights(mla_w_down[j], mla_g_q[j], mla_w_uq[j], mla_g_kv[j], mla_w_ukv[j])
            cos, sin = _rope_tables(n_lat)
            zl = jnp.zeros((n_lat, LANES - QK_ROPE), F32)
            tabs = (jnp.concatenate([cos, cos, zl], axis=1), jnp.concatenate([-sin, sin, zl], axis=1), cos.T, sin.T)
            tq = tk = 512
            qT, k, vT = _mla_proj(xl, gm, sh1, sc1, wts, tabs, need_q=True, tm=tk, tk=tk)
            half = QK_ROPE // 2
            one_c = jnp.concatenate([jnp.ones((n_ctx, QK_ROPE), F32), jnp.zeros((n_ctx, LANES - QK_ROPE), F32)], axis=1)
            tabs_c = (one_c, jnp.zeros((n_ctx, LANES), F32), jnp.ones((half, n_ctx), F32), jnp.zeros((half, n_ctx), F32))
            kc, vTc = _mla_proj(ctx, gm, mod(i, 0, ctx_rows), mod(i, 1, ctx_rows), wts, tabs_c,
                                need_q=False, tm=n_ctx, tk=n_ctx)
            o = _attention(qT, k, vT, kc, vTc, tq=tq, tk=tk)
            wo = mla_w_o[j].astype(BF16)
            bo = jnp.zeros((1, d), F32)
            transposed = True
        else:
            proj3 = _hy_in(xl, gm, sh1, sc1, hy_w_in[j].astype(BF16), hy_b_in[j].reshape(1, -1), hy_conv_w[j],
                           hy_conv_b[j].reshape(1, -1), tm=512)
            fparams = (hy_f_w1[j], hy_f_b1[j], hy_f_freq1[j], hy_f_w2[j], hy_f_b2[j], hy_f_freq2[j], hy_f_w3[j],
                       hy_decay[j])
            o = _hyena_mix(proj3, fparams, hy_bias[j], n_lat=n_lat)
            wo = hy_w_out[j].astype(BF16)
            bo = hy_b_out[j].reshape(1, d)
            transposed = False
        wr = jnp.zeros((d, LANES), F32).at[:, :N_EXPERTS].set(moe_w_r[i])
        wrh = wr.astype(BF16)
        wrl = (wr - wrh.astype(F32)).astype(BF16)
        br = jnp.zeros((1, LANES), F32).at[0, :N_EXPERTS].set(moe_b_r[i])
        xl, fl, topi, gates, rank, cnt = _post(o, wo, bo, xl, g1, norm_ffn_g[i].reshape(1, d), sh2, sc2, wrh, wrl, br,
                                               transposed=transposed, tm=512)
        xl = _moe(fl, topi, gates, rank, cnt, xl, g2, final_g.reshape(1, d), moe_w_in[i], moe_b_in[i],
                  moe_w_out[i], moe_b_out[i], final=(i == depth - 1))
    return xl
```

```python
import functools
import math

import jax
import jax.numpy as jnp
from jax import lax
from jax.experimental import pallas as pl
from jax.experimental.pallas import tpu as pltpu

F32 = jnp.float32
BF16 = jnp.bfloat16

EPS = 1e-6
GRID_W = 64
MLA_HEADS = 8
QK_NOPE = 128
QK_ROPE = 64
V_DIM = 128
Q_LORA = 512
KV_LORA = 256
ROPE_THETA = 10000.0
MLA_SCALE = (QK_NOPE + QK_ROPE) ** -0.5
QK_PAD = 256

HY_EMB = 33
HY_BANDS = (HY_EMB - 1) // 2
HY_HID = 64
FFT_N2 = 128
B_GROUP = 8

N_EXPERTS = 32
TOP_K = 4
SWIGLU_LIMIT = 7.0
SWIGLU_ALPHA = 1.702
MOE_TM = 256
LANES = 128

VMEM_LIMIT = 56 * 1024 * 1024


def _cp(sem, vmem=VMEM_LIMIT):
    return pltpu.CompilerParams(dimension_semantics=sem, vmem_limit_bytes=vmem)


def _dot(a, b):
    return jnp.dot(a, b, preferred_element_type=F32)


def _dot_hi(a, b):
    return jnp.dot(a, b, preferred_element_type=F32, precision=lax.Precision.HIGHEST)


def _rms(x, g):
    return x * lax.rsqrt(jnp.mean(x * x, axis=-1, keepdims=True) + EPS) * g


def _ada_body(c_ref, w_ref, b_ref, o_ref):
    c = c_ref[...]
    s = c * jax.nn.sigmoid(c)
    o_ref[0] = _dot(s.astype(BF16), w_ref[0].astype(BF16)) + b_ref[0]


def _ada(cond8, ada_w, ada_b):
    depth, d, n = ada_w.shape
    tn = n // 4
    return pl.pallas_call(
        _ada_body,
        grid=(depth, n // tn),
        in_specs=[
            pl.BlockSpec((8, d), lambda i, j: (0, 0)),
            pl.BlockSpec((1, d, tn), lambda i, j: (i, 0, j)),
            pl.BlockSpec((1, 1, tn), lambda i, j: (i, 0, j)),
        ],
        out_specs=pl.BlockSpec((1, 8, tn), lambda i, j: (i, 0, j)),
        out_shape=jax.ShapeDtypeStruct((depth, 8, n), F32),
        compiler_params=_cp(("arbitrary", "arbitrary")),
        name="ada_mod",
    )(cond8, ada_w, ada_b.reshape(depth, 1, n))


def _mla_proj_body(x_ref, g_ref, sh_ref, sc_ref, wd_ref, gq_ref, gkv_ref, wuk_ref, wuqT_ref, wuvT_ref,
                   ct_ref, st_ref, cT_ref, sT_ref, *out_refs, need_q, tk):
    if need_q:
        qT_ref, k_ref, vT_ref = out_refs
    else:
        k_ref, vT_ref = out_refs
    nh = MLA_HEADS
    x = x_ref[0]
    h = _rms(x, g_ref[...]) * (1.0 + sc_ref[0]) + sh_ref[0]
    lat = _dot(h.astype(BF16), wd_ref[...])
    o_kv = Q_LORA
    o_a = Q_LORA + KV_LORA
    kvn = _rms(lat[:, o_kv:o_a], gkv_ref[...])
    kr = (lat[:, o_a:o_a + LANES] * ct_ref[...] + lat[:, o_a + LANES:o_a + 2 * LANES] * st_ref[...]).astype(BF16)
    knope = _dot(kvn.astype(BF16), wuk_ref[...])
    for hh in range(nh):
        k_ref[0, hh, :, 0:QK_NOPE] = knope[:, hh * QK_NOPE:(hh + 1) * QK_NOPE].astype(BF16)
        k_ref[0, hh, :, QK_NOPE:QK_PAD] = kr
    vT = _dot(wuvT_ref[...], kvn.T.astype(BF16))
    tm = x.shape[0]
    for hh in range(nh):
        for c in range(tm // tk):
            vT_ref[0, hh, c] = vT[hh * V_DIM:(hh + 1) * V_DIM, c * tk:(c + 1) * tk].astype(BF16)
    if need_q:
        qn = _rms(lat[:, :Q_LORA], gq_ref[...])
        qT = _dot(wuqT_ref[...], qn.T.astype(BF16)) * (MLA_SCALE * math.log2(math.e))
        c = cT_ref[...]
        s = sT_ref[...]
        hw = QK_NOPE + QK_ROPE
        half = QK_ROPE // 2
        for hh in range(nh):
            base = hh * hw
            x1 = qT[base + QK_NOPE:base + QK_NOPE + half]
            x2 = qT[base + QK_NOPE + half:base + hw]
            qT_ref[0, hh, 0:QK_NOPE] = qT[base:base + QK_NOPE].astype(BF16)
            qT_ref[0, hh, QK_NOPE:QK_NOPE + half] = (x1 * c - x2 * s).astype(BF16)
            qT_ref[0, hh, QK_NOPE + half:hw] = (x1 * s + x2 * c).astype(BF16)
            qT_ref[0, hh, hw:QK_PAD] = jnp.zeros((QK_PAD - hw, tm), BF16)


def _mla_proj(x, g, sh, sc, wts, tabs, *, need_q, tm, tk):
    bsz, n, d = x.shape
    nh = MLA_HEADS
    wd, gq, gkv, wuk, wuqT, wuvT = wts
    ct, st, cT, sT = tabs
    nsh = sh.shape[0]
    full = lambda a: pl.BlockSpec(a.shape, lambda b, i: (0,) * a.ndim)
    in_specs = [
        pl.BlockSpec((1, tm, d), lambda b, i: (b, i, 0)),
        full(g),
        pl.BlockSpec((1, 1, d), lambda b, i: (b % nsh, 0, 0)),
        pl.BlockSpec((1, 1, d), lambda b, i: (b % nsh, 0, 0)),
        full(wd), full(gq), full(gkv), full(wuk), full(wuqT), full(wuvT),
        pl.BlockSpec((tm, LANES), lambda b, i: (i, 0)),
        pl.BlockSpec((tm, LANES), lambda b, i: (i, 0)),
        pl.BlockSpec((QK_ROPE // 2, tm), lambda b, i: (0, i)),
        pl.BlockSpec((QK_ROPE // 2, tm), lambda b, i: (0, i)),
    ]
    out_specs = [
        pl.BlockSpec((1, nh, tm, QK_PAD), lambda b, i: (b, 0, i, 0)),
        pl.BlockSpec((1, nh, tm // tk, V_DIM, tk), lambda b, i: (b, 0, i, 0, 0)),
    ]
    out_shape = [
        jax.ShapeDtypeStruct((bsz, nh, n, QK_PAD), BF16),
        jax.ShapeDtypeStruct((bsz, nh, n // tk, V_DIM, tk), BF16),
    ]
    if need_q:
        out_specs = [pl.BlockSpec((1, nh, QK_PAD, tm), lambda b, i: (b, 0, 0, i))] + out_specs
        out_shape = [jax.ShapeDtypeStruct((bsz, nh, QK_PAD, n), BF16)] + out_shape
    return pl.pallas_call(
        functools.partial(_mla_proj_body, need_q=need_q, tk=tk),
        grid=(bsz, n // tm),
        in_specs=in_specs,
        out_specs=out_specs,
        out_shape=out_shape,
        compiler_params=_cp(("arbitrary", "arbitrary")),
        name="mla_proj_q" if need_q else "mla_proj_ctx",
    )(x, g, sh, sc, wd, gq, gkv, wuk, wuqT, wuvT, ct, st, cT, sT)


ACC_ROWS = V_DIM + 16
SM_STRIP = 64


def _attn_body(qT_ref, k_ref, vT_ref, kc_ref, vTc_ref, o_ref, s0, s1, p0, p1, sc, pc, acc, m_scr, x0, x1, xc,
               a0, a1, ac, *, tk):
    nchunk = k_ref.shape[2] // tk

    def scores(kblk, s_ref, mx_ref):
        r = _dot(kblk, qT_ref[0, 0])
        s_ref[...] = r
        mx_ref[...] = jnp.max(r, axis=0, keepdims=True)

    def probs(s_ref, mx_ref, p_ref, a_ref):
        m_old = m_scr[...]
        m_new = jnp.maximum(m_old, mx_ref[...])
        m_scr[...] = m_new
        a_ref[...] = jnp.exp2(m_old - m_new)
        for r in range(0, s_ref.shape[0], SM_STRIP):
            p_ref[r:r + SM_STRIP] = jnp.exp2(s_ref[r:r + SM_STRIP] - m_new).astype(BF16)

    def accumulate(p_ref, a_ref, vblk):
        lhs = jnp.concatenate([vblk, jnp.ones((ACC_ROWS - V_DIM, vblk.shape[1]), BF16)], axis=0)
        acc[...] = a_ref[...] * acc[...] + _dot(lhs, p_ref[...])

    def kchunk(i):
        i = jnp.minimum(i, nchunk - 1)
        return k_ref[0, 0, pl.ds(pl.multiple_of(i * tk, tk), tk), :]

    def vchunk(i):
        nsub = tk // vT_ref.shape[-1]
        return jnp.concatenate([vT_ref[0, 0, i * nsub + u] for u in range(nsub)], axis=1)

    m_scr[...] = jnp.full(m_scr.shape, -jnp.inf, F32)
    acc[...] = jnp.zeros(acc.shape, F32)
    scores(kc_ref[0, 0], sc, xc)
    scores(kchunk(0), s0, x0)
    probs(sc, xc, pc, ac)
    accumulate(pc, ac, vTc_ref[0, 0, 0])

    def body(j, carry):
        t = 2 * j
        scores(kchunk(t + 1), s1, x1)
        probs(s0, x0, p0, a0)
        accumulate(p0, a0, vchunk(t))
        scores(kchunk(t + 2), s0, x0)
        probs(s1, x1, p1, a1)
        accumulate(p1, a1, vchunk(t + 1))
        return carry

    lax.fori_loop(0, nchunk // 2, body, 0)
    o_ref[0, 0] = (acc[0:V_DIM] / acc[V_DIM:V_DIM + 1]).astype(BF16)


def _attention(qT, k, vT, kc, vTc, *, tq, tk):
    bsz, nh, _, n = qT.shape
    nc = kc.shape[2]
    tv = vT.shape[-1]
    assert (n // tk) % 2 == 0 and tk % tv == 0
    return pl.pallas_call(
        functools.partial(_attn_body, tk=tk),
        grid=(bsz, nh, n // tq),
        in_specs=[
            pl.BlockSpec((1, 1, QK_PAD, tq), lambda b, h, i: (b, h, 0, i)),
            pl.BlockSpec((1, 1, n, QK_PAD), lambda b, h, i: (b, h, 0, 0)),
            pl.BlockSpec((1, 1, n // tv, V_DIM, tv), lambda b, h, i: (b, h, 0, 0, 0)),
            pl.BlockSpec((1, 1, nc, QK_PAD), lambda b, h, i: (b, h, 0, 0)),
            pl.BlockSpec((1, 1, 1, V_DIM, nc), lambda b, h, i: (b, h, 0, 0, 0)),
        ],
        out_specs=pl.BlockSpec((1, 1, V_DIM, tq), lambda b, h, i: (b, h, 0, i)),
        out_shape=jax.ShapeDtypeStruct((bsz, nh, V_DIM, n), BF16),
        scratch_shapes=[pltpu.VMEM((tk, tq), F32), pltpu.VMEM((tk, tq), F32),
                        pltpu.VMEM((tk, tq), BF16), pltpu.VMEM((tk, tq), BF16),
                        pltpu.VMEM((nc, tq), F32), pltpu.VMEM((nc, tq), BF16),
                        pltpu.VMEM((ACC_ROWS, tq), F32), pltpu.VMEM((1, tq), F32),
                        pltpu.VMEM((1, tq), F32), pltpu.VMEM((1, tq), F32), pltpu.VMEM((1, tq), F32),
                        pltpu.VMEM((1, tq), F32), pltpu.VMEM((1, tq), F32), pltpu.VMEM((1, tq), F32)],
        compiler_params=_cp(("arbitrary", "arbitrary", "arbitrary")),
        name="mla_attention",
    )(qT, k, vT, kc, vTc)


def _post_body(o_ref, wo_ref, bo_ref, x_ref, g1_ref, gf_ref, sh_ref, sc_ref, wrh_ref, wrl_ref, br_ref, tri_ref,
               xl_ref, fl_ref, ti_ref, gt_ref, rk_ref, cnt_ref, *, transposed):
    @pl.when((pl.program_id(0) == 0) & (pl.program_id(1) == 0))
    def _():
        cnt_ref[...] = jnp.zeros_like(cnt_ref)

    tm = x_ref.shape[1]
    if transposed:
        oT = o_ref[0].astype(F32).reshape(MLA_HEADS * V_DIM, tm)
        o = oT.T.astype(BF16)
    else:
        o = o_ref[0].astype(BF16)
    y = _dot(o, wo_ref[...]) + bo_ref[...]
    xl = x_ref[0] + g1_ref[0] * y
    xl_ref[0] = xl
    fl = _rms(xl, gf_ref[...]) * (1.0 + sc_ref[0]) + sh_ref[0]
    fl_ref[...] = fl
    flh = fl.astype(BF16)
    fll = (fl - flh.astype(F32)).astype(BF16)
    logits = _dot(flh, wrh_ref[...]) + (_dot(fll, wrh_ref[...]) + _dot(flh, wrl_ref[...])) + br_ref[...]
    lane = lax.broadcasted_iota(jnp.int32, (tm, LANES), 1).astype(F32)
    neg = jnp.float32(-jnp.inf)
    work = jnp.where(lane < N_EXPERTS, logits, neg)
    vals, idxs = [], []
    onehot = jnp.zeros((tm, LANES), F32)
    for _ in range(TOP_K):
        mk = jnp.max(work, axis=-1, keepdims=True)
        ik = jnp.min(jnp.where(work == mk, lane, float(LANES)), axis=-1, keepdims=True)
        sel = lane == ik
        onehot = jnp.where(sel, 1.0, onehot)
        work = jnp.where(sel, neg, work)
        vals.append(mk)
        idxs.append(ik)
    es = [jnp.exp(v - vals[0]) for v in vals]
    den = es[0] + es[1] + es[2] + es[3]
    pre = _dot(tri_ref[...], onehot.astype(BF16)) + cnt_ref[...]
    ti = jnp.zeros((tm, LANES), F32)
    gt = jnp.zeros((tm, LANES), F32)
    rk = jnp.zeros((tm, LANES), F32)
    for kk in range(TOP_K):
        rank = jnp.sum(jnp.where(lane == idxs[kk], pre, 0.0), axis=-1, keepdims=True)
        ti = jnp.where(lane == kk, idxs[kk], ti)
        gt = jnp.where(lane == kk, es[kk] / den, gt)
        rk = jnp.where(lane == kk, rank, rk)
    ti_ref[...] = ti[:, :TOP_K].astype(jnp.int32)
    gt_ref[...] = gt[:, :TOP_K]
    rk_ref[...] = rk[:, :TOP_K].astype(jnp.int32)
    cnt_ref[...] += jnp.sum(onehot, axis=0, keepdims=True)


def _post(o, wo, bo, x, g1, gf, sh, sc, wrh, wrl, br, *, transposed, tm):
    bsz, n, d = x.shape
    t = bsz * n
    nt = n // tm
    tri = (lax.broadcasted_iota(jnp.int32, (tm, tm), 0) > lax.broadcasted_iota(jnp.int32, (tm, tm), 1)).astype(BF16)
    full = lambda a: pl.BlockSpec(a.shape, lambda b, i: (0,) * a.ndim)
    per_b = pl.BlockSpec((1, 1, d), lambda b, i: (b, 0, 0))
    if transposed:
        o_spec = pl.BlockSpec((1, MLA_HEADS, V_DIM, tm), lambda b, i: (b, 0, 0, i))
    else:
        o_spec = pl.BlockSpec((1, tm, d), lambda b, i: (b, i, 0))
    tok = lambda w: pl.BlockSpec((tm, w), lambda b, i: (b * nt + i, 0))
    return pl.pallas_call(
        functools.partial(_post_body, transposed=transposed),
        grid=(bsz, nt),
        in_specs=[o_spec, full(wo), full(bo), pl.BlockSpec((1, tm, d), lambda b, i: (b, i, 0)), per_b, full(gf),
                  per_b, per_b, full(wrh), full(wrl), full(br), full(tri)],
        out_specs=[pl.BlockSpec((1, tm, d), lambda b, i: (b, i, 0)), tok(d), tok(TOP_K), tok(TOP_K), tok(TOP_K),
                   pl.BlockSpec((1, LANES), lambda b, i: (0, 0))],
        out_shape=[jax.ShapeDtypeStruct((bsz, n, d), F32), jax.ShapeDtypeStruct((t, d), F32),
                   jax.ShapeDtypeStruct((t, TOP_K), jnp.int32), jax.ShapeDtypeStruct((t, TOP_K), F32),
                   jax.ShapeDtypeStruct((t, TOP_K), jnp.int32), jax.ShapeDtypeStruct((1, LANES), F32)],
        compiler_params=_cp(("arbitrary", "arbitrary")),
        name="post_attn" if transposed else "post_hyena",
    )(o, wo, bo, x, g1, gf, sh, sc, wrh, wrl, br, tri)


def _dispatch_body(dest_ref, fl_ref, xs_in, xs_out, sem, *, td):
    del xs_in

    def issue(t, carry):
        for kk in range(TOP_K):
            d = dest_ref[0, 0, t * TOP_K + kk]
            pltpu.make_async_copy(fl_ref.at[pl.ds(t, 1)], xs_out.at[pl.ds(d, 1)], sem).start()
        return carry

    lax.fori_loop(0, td, issue, 0, unroll=2)

    def drain(t, carry):
        pltpu.make_async_copy(fl_ref.at[pl.ds(0, 1)], xs_out.at[pl.ds(0, 1)], sem).wait()
        return carry

    lax.fori_loop(0, td * TOP_K, drain, 0, unroll=8)


def _dispatch(dest, fl, n_rows, *, td):
    t, d = fl.shape
    dest3 = dest.reshape(t // td, 1, td * TOP_K)
    xs0 = jnp.zeros((n_rows, d), fl.dtype)
    return pl.pallas_call(
        functools.partial(_dispatch_body, td=td),
        grid=(t // td,),
        in_specs=[
            pl.BlockSpec((1, 1, td * TOP_K), lambda i: (i, 0, 0), memory_space=pltpu.SMEM),
            pl.BlockSpec((td, d), lambda i: (i, 0)),
            pl.BlockSpec(memory_space=pl.ANY),
        ],
        out_specs=pl.BlockSpec(memory_space=pl.ANY),
        out_shape=jax.ShapeDtypeStruct((n_rows, d), fl.dtype),
        scratch_shapes=[pltpu.SemaphoreType.DMA(())],
        input_output_aliases={2: 0},
        compiler_params=_cp(("arbitrary",)),
        name="moe_dispatch",
    )(dest3, fl, xs0)


def _expert_body(be_ref, nu_ref, xs_ref, win_ref, bin_ref, wout_ref, bout_ref, ys_ref, win_s, wout_s):
    b = pl.program_id(0)
    dff = wout_ref.shape[1]

    @pl.when(b < nu_ref[0])
    def _():
        prev = be_ref[jnp.maximum(b - 1, 0)]

        @pl.when((b == 0) | (prev != be_ref[b]))
        def _():
            win_s[...] = win_ref[0].astype(BF16)
            wout_s[...] = wout_ref[0].astype(BF16)

        x = xs_ref[...].astype(BF16)
        gu = _dot(x, win_s[...]) + bin_ref[0]
        gate = jnp.minimum(gu[:, :dff], SWIGLU_LIMIT)
        lin = jnp.clip(gu[:, dff:], -SWIGLU_LIMIT, SWIGLU_LIMIT)
        act = gate * jax.nn.sigmoid(SWIGLU_ALPHA * gate) * (lin + 1.0)
        ys_ref[...] = _dot(act.astype(BF16), wout_s[...]) + bout_ref[0]

    @pl.when(b >= nu_ref[0])
    def _():
        ys_ref[...] = jnp.zeros_like(ys_ref)


def _experts(blk_exp, n_used, xs, w_in, b_in, w_out, b_out):
    n_rows, d = xs.shape
    ne, _, f2 = w_in.shape
    dff = w_out.shape[1]
    tm = MOE_TM
    grid_spec = pltpu.PrefetchScalarGridSpec(
        num_scalar_prefetch=2,
        grid=(n_rows // tm,),
        in_specs=[
            pl.BlockSpec((tm, d), lambda b, be, nu: (b, 0)),
            pl.BlockSpec((1, d, f2), lambda b, be, nu: (be[b], 0, 0)),
            pl.BlockSpec((1, 1, f2), lambda b, be, nu: (be[b], 0, 0)),
            pl.BlockSpec((1, dff, d), lambda b, be, nu: (be[b], 0, 0)),
            pl.BlockSpec((1, 1, d), lambda b, be, nu: (be[b], 0, 0)),
        ],
        out_specs=pl.BlockSpec((tm, d), lambda b, be, nu: (b, 0)),
        scratch_shapes=[pltpu.VMEM((d, f2), BF16), pltpu.VMEM((dff, d), BF16)],
    )
    return pl.pallas_call(
        _expert_body,
        grid_spec=grid_spec,
        out_shape=jax.ShapeDtypeStruct((n_rows, d), F32),
        compiler_params=_cp(("arbitrary",)),
        name="moe_experts",
    )(blk_exp, n_used, xs, w_in, b_in.reshape(ne, 1, f2), w_out, b_out.reshape(ne, 1, d))


def _combine_body(dest_ref, ys_hbm, gt_ref, xl_ref, g2_ref, fg_ref, out_ref, buf, sem, *, tc, final):
    def issue(t, carry):
        for kk in range(TOP_K):
            d = dest_ref[0, 0, t * TOP_K + kk]
            pltpu.make_async_copy(ys_hbm.at[pl.ds(d, 1)], buf.at[pl.ds(kk * tc + t, 1)], sem).start()
        return carry

    lax.fori_loop(0, tc, issue, 0, unroll=2)

    def drain(t, carry):
        pltpu.make_async_copy(ys_hbm.at[pl.ds(0, 1)], buf.at[pl.ds(0, 1)], sem).wait()
        return carry

    lax.fori_loop(0, tc * TOP_K, drain, 0, unroll=8)
    gt = gt_ref[...]
    y = gt[:, 0:1] * buf[0:tc]
    for kk in range(1, TOP_K):
        y = y + gt[:, kk:kk + 1] * buf[kk * tc:(kk + 1) * tc]
    xl = xl_ref[0] + g2_ref[0] * y
    out_ref[0] = _rms(xl, fg_ref[...]) if final else xl


def _combine(dest, ys, gates, xl, g2, fg, *, tc, final):
    bsz, n, d = xl.shape
    t = bsz * n
    nt = n // tc
    dest3 = dest.reshape(t // tc, 1, tc * TOP_K)
    return pl.pallas_call(
        functools.partial(_combine_body, tc=tc, final=final),
        grid=(bsz, nt),
        in_specs=[
            pl.BlockSpec((1, 1, tc * TOP_K), lambda b, i: (b * nt + i, 0, 0), memory_space=pltpu.SMEM),
            pl.BlockSpec(memory_space=pl.ANY),
            pl.BlockSpec((tc, TOP_K), lambda b, i: (b * nt + i, 0)),
            pl.BlockSpec((1, tc, d), lambda b, i: (b, i, 0)),
            pl.BlockSpec((1, 1, d), lambda b, i: (b, 0, 0)),
            pl.BlockSpec((1, d), lambda b, i: (0, 0)),
        ],
        out_specs=pl.BlockSpec((1, tc, d), lambda b, i: (b, i, 0)),
        out_shape=jax.ShapeDtypeStruct((bsz, n, d), F32),
        scratch_shapes=[pltpu.VMEM((TOP_K * tc, d), F32), pltpu.SemaphoreType.DMA(())],
        compiler_params=_cp(("arbitrary", "arbitrary")),
        name="moe_combine",
    )(dest3, ys, gates, xl, g2, fg)


def _moe(fl, topi, gates, rank, cnt, xl, g2, fg, w_in, b_in, w_out, b_out, *, final):
    t, d = fl.shape
    tm = MOE_TM
    counts = cnt[0, :N_EXPERTS].astype(jnp.int32)
    padded = (counts + tm - 1) // tm * tm
    pad_end = jnp.cumsum(padded)
    pad_start = pad_end - padded
    dest = jnp.take(pad_start, topi) + rank
    nb = t * TOP_K // tm + N_EXPERTS
    blk_start = jnp.arange(nb, dtype=jnp.int32) * tm
    blk_exp = jnp.minimum(jnp.sum((pad_end[None, :] <= blk_start[:, None]).astype(jnp.int32), axis=1), N_EXPERTS - 1)
    n_used = (pad_end[-1:] // tm).astype(jnp.int32)
    xs = _dispatch(dest, fl, nb * tm, td=256)
    ys = _experts(blk_exp, n_used, xs, w_in, b_in, w_out, b_out)
    return _combine(dest, ys, gates, xl, g2, fg, tc=256, final=final)


def _hy_in_body(x_ref, xp_ref, xn_ref, g_ref, sh_ref, sc_ref, w_ref, b_ref, cw_ref, cb_ref, o_ref, *, nt):
    i = pl.program_id(2)
    w = w_ref[...]

    def proj(xx):
        h = _rms(xx, g_ref[...]) * (1.0 + sc_ref[0]) + sh_ref[0]
        return _dot(h.astype(BF16), w) + b_ref[...]

    p = proj(x_ref[0])
    tm = p.shape[0]
    ph = proj(jnp.concatenate([xp_ref[0], xn_ref[0]], axis=0))
    prev = jnp.where(i > 0, ph[7:8], 0.0)
    nxt = jnp.where(i < nt - 1, ph[8:9], 0.0)
    row = lax.broadcasted_iota(jnp.int32, (tm, 1), 0)
    up = jnp.where(row == 0, prev, pltpu.roll(p, 1, axis=0))
    dn = jnp.where(row == tm - 1, nxt, pltpu.roll(p, tm - 1, axis=0))
    cw = cw_ref[...]
    o_ref[0, 0] = up * cw[0:1] + p * cw[1:2] + dn * cw[2:3] + cb_ref[...]


def _hy_in(x, g, sh, sc, w, b, cw, cb, *, tm):
    bsz, n, d = x.shape
    nt = n // tm
    hb = tm // 8
    per_b = pl.BlockSpec((1, 1, d), lambda j, bb, i: (bb, 0, 0))
    return pl.pallas_call(
        functools.partial(_hy_in_body, nt=nt),
        grid=(3, bsz, nt),
        in_specs=[
            pl.BlockSpec((1, tm, d), lambda j, bb, i: (bb, i, 0)),
            pl.BlockSpec((1, 8, d), lambda j, bb, i: (bb, jnp.maximum(i * hb - 1, 0), 0)),
            pl.BlockSpec((1, 8, d), lambda j, bb, i: (bb, jnp.minimum((i + 1) * hb, n // 8 - 1), 0)),
            pl.BlockSpec((1, d), lambda j, bb, i: (0, 0)),
            per_b, per_b,
            pl.BlockSpec((d, d), lambda j, bb, i: (0, j)),
            pl.BlockSpec((1, d), lambda j, bb, i: (0, j)),
            pl.BlockSpec((3, d), lambda j, bb, i: (0, j)),
            pl.BlockSpec((1, d), lambda j, bb, i: (0, j)),
        ],
        out_specs=pl.BlockSpec((1, 1, tm, d), lambda j, bb, i: (j, bb, i, 0)),
        out_shape=jax.ShapeDtypeStruct((3, bsz, n, d), F32),
        compiler_params=_cp(("arbitrary", "arbitrary", "arbitrary")),
        name="hyena_in_proj",
    )(x, x, x, g, sh, sc, w, b, cw, cb)


def _filt_feat_body(w1_ref, b1_ref, f1_ref, w2_ref, b2_ref, f2_ref, o_ref, *, n_lat):
    na = o_ref.shape[1]
    a = lax.broadcasted_iota(jnp.int32, (na, 1), 0)
    lane = lax.broadcasted_iota(jnp.int32, (na, LANES), 1)
    band_idx = jnp.where(lane <= HY_BANDS, lane - 1, lane - 1 - HY_BANDS).astype(F32)
    band = 1e-4 + band_idx * ((HY_BANDS - 1 - 1e-4) / (HY_BANDS - 1))
    for j in range(B_GROUP):
        r = a * FFT_N2 + (pl.program_id(0) * B_GROUP + j)
        pos = jnp.where(r < n_lat, r, 2 * n_lat - r).astype(F32)
        tn = pos / float(max(n_lat - 1, 1))
        ang = ((2.0 * math.pi / n_lat) * pos) * band
        z = jnp.where(lane == 0, tn, jnp.where(lane <= HY_BANDS, jnp.cos(ang),
                                               jnp.where(lane < HY_EMB, -jnp.sin(ang), 0.0)))
        h1 = jnp.sin(f1_ref[...] * (_dot_hi(z, w1_ref[...]) + b1_ref[...]))
        h2 = jnp.sin(f2_ref[...] * (_dot_hi(h1, w2_ref[...]) + b2_ref[...]))
        valid = (r != n_lat).astype(F32)
        o_ref[j] = jnp.where(lane == HY_HID, tn, jnp.where(lane == HY_HID + 1, valid, h2))


def _filt_feat(w1, b1, f1, w2, b2, f2, *, n_lat):
    na = 2 * n_lat // FFT_N2
    w1p = jnp.zeros((LANES, LANES), F32).at[:HY_EMB, :HY_HID].set(w1)
    w2p = jnp.zeros((LANES, LANES), F32).at[:HY_HID, :HY_HID].set(w2)
    padv = lambda v: jnp.zeros((1, LANES), F32).at[0, :HY_HID].set(v)
    full = lambda shp: pl.BlockSpec(shp, lambda i: (0,) * len(shp))
    return pl.pallas_call(
        functools.partial(_filt_feat_body, n_lat=n_lat),
        grid=(FFT_N2 // B_GROUP,),
        in_specs=[full((LANES, LANES)), full((1, LANES)), full((1, LANES)),
                  full((LANES, LANES)), full((1, LANES)), full((1, LANES))],
        out_specs=pl.BlockSpec((B_GROUP, na, LANES), lambda i: (i, 0, 0)),
        out_shape=jax.ShapeDtypeStruct((FFT_N2, na, LANES), F32),
        compiler_params=_cp(("arbitrary",)),
        name="hyena_filter_features",
    )(w1p, padv(b1), padv(f1), w2p, padv(b2), padv(f2))


def _filt_s1_body(hd_ref, w3_ref, dec_ref, tab_ref, o_ref):
    na = hd_ref.shape[1]
    ha = na // 2
    for j in range(B_GROUP):
        f = hd_ref[j]
        tn = f[:, HY_HID:HY_HID + 1]
        valid = f[:, HY_HID + 1:HY_HID + 2]
        top = _dot_hi(f[:ha], w3_ref[0, 0]) * jnp.exp(-tn[:ha] * jnp.abs(dec_ref[0, 0]))
        bot = _dot_hi(f[ha:], w3_ref[0, 1]) * jnp.exp(-tn[ha:] * jnp.abs(dec_ref[0, 1])) * valid[ha:]
        hb = jnp.concatenate([top, bot], axis=0).astype(BF16)
        o_ref[0, :, j, :] = _dot(tab_ref[j], hb)


def _filt_s1(hd, w3r, dec, tab, *, ct):
    _, na, _ = hd.shape
    d = w3r.shape[-1]
    return pl.pallas_call(
        _filt_s1_body,
        grid=(2, FFT_N2 // B_GROUP, d // ct),
        in_specs=[
            pl.BlockSpec((B_GROUP, na, LANES), lambda o, g, c: (g, 0, 0)),
            pl.BlockSpec((1, 2, LANES, ct), lambda o, g, c: (o, 0, 0, c)),
            pl.BlockSpec((1, 2, 1, ct), lambda o, g, c: (o, 0, 0, c)),
            pl.BlockSpec((B_GROUP, 2 * na, na), lambda o, g, c: (g, 0, 0)),
        ],
        out_specs=pl.BlockSpec((1, 2 * na, B_GROUP, ct), lambda o, g, c: (o, 0, g, c)),
        out_shape=jax.ShapeDtypeStruct((2, 2 * na, FFT_N2, d), F32),
        compiler_params=_cp(("arbitrary", "arbitrary", "arbitrary")),
        name="hyena_filter_dft1",
    )(hd, w3r, dec, tab)


def _s2_body(*refs, conv):
    if conv:
        o_ref, kf_ref, ff_ref, fi_ref, g_ref = refs
    else:
        o_ref, ff_ref, g_ref = refs
    ct = o_ref.shape[-1]
    xin = o_ref[...].reshape(2 * FFT_N2, ct).astype(BF16)
    xf = _dot(ff_ref[...], xin)
    if conv:
        xr, xi = xf[:FFT_N2], xf[FFT_N2:]
        kr = kf_ref[0, 0, 0]
        ki = kf_ref[0, 1, 0]
        y = jnp.concatenate([xr * kr - xi * ki, xr * ki + xi * kr], axis=0).astype(BF16)
        xf = _dot(fi_ref[...], y)
    g_ref[...] = xf.reshape(g_ref.shape)


def _s2(o4, kf, order, ff, fi, *, ct, conv):
    n1 = o4.shape[-3]
    d = o4.shape[-1]
    full = lambda a: pl.BlockSpec(a.shape, lambda k, c: (0,) * a.ndim)
    if conv:
        blk = pl.BlockSpec((2, 1, FFT_N2, ct), lambda k, c: (0, k, 0, c))
        in_specs = [blk, pl.BlockSpec((1, 2, 1, FFT_N2, ct), lambda k, c: (order, 0, k, 0, c)), full(ff), full(fi)]
        args = (o4, kf, ff, fi)
        grid = (n1, d // ct)
        out_specs = blk
    else:
        no = o4.shape[0]
        blk = pl.BlockSpec((1, 2, 1, FFT_N2, ct), lambda k, c: (k // n1, 0, k % n1, 0, c))
        in_specs = [blk, full(ff)]
        args = (o4, ff)
        grid = (no * n1, d // ct)
        out_specs = blk
    return pl.pallas_call(
        functools.partial(_s2_body, conv=conv),
        grid=grid,
        in_specs=in_specs,
        out_specs=out_specs,
        out_shape=jax.ShapeDtypeStruct(o4.shape, F32),
        compiler_params=_cp(("arbitrary", "arbitrary")),
        name="hyena_conv_dft2" if conv else "hyena_filter_dft2",
    )(*args)


def _s1_body(z_ref, tab_ref, o_ref):
    for j in range(B_GROUP):
        o_ref[:, j, :] = _dot(tab_ref[j], z_ref[:, j, :].astype(BF16))


def _s1(z4, zi, tab, *, ct):
    _, rows, _, d = z4.shape
    n_out = tab.shape[1]
    return pl.pallas_call(
        _s1_body,
        grid=(FFT_N2 // B_GROUP, d // ct),
        in_specs=[
            pl.BlockSpec((None, rows, B_GROUP, ct), lambda g, c: (zi, 0, g, c)),
            pl.BlockSpec((B_GROUP, n_out, rows), lambda g, c: (g, 0, 0)),
        ],
        out_specs=pl.BlockSpec((n_out, B_GROUP, ct), lambda g, c: (0, g, c)),
        out_shape=jax.ShapeDtypeStruct((n_out, FFT_N2, d), F32),
        compiler_params=_cp(("arbitrary", "arbitrary")),
        name="hyena_conv_dft1",
    )(z4, tab)


def _s3_body(g_ref, tab_ref, gate_ref, z_ref, fb_ref, o_ref):
    for j in range(B_GROUP):
        o_ref[:, j, :] = _dot(tab_ref[j], g_ref[:, j, :].astype(BF16))
    o_ref[...] = gate_ref[...] * (o_ref[...] + z_ref[...] * fb_ref[...])


def _s3(g3, tab, gate4, gi, z4, zi, fb, *, ct):
    n_in, _, d = g3.shape
    rows = tab.shape[1]
    blk = pl.BlockSpec((rows, B_GROUP, ct), lambda g, c: (0, g, c))
    return pl.pallas_call(
        _s3_body,
        grid=(FFT_N2 // B_GROUP, d // ct),
        in_specs=[
            pl.BlockSpec((n_in, B_GROUP, ct), lambda g, c: (0, g, c)),
            pl.BlockSpec((B_GROUP, rows, n_in), lambda g, c: (g, 0, 0)),
            pl.BlockSpec((None, rows, B_GROUP, ct), lambda g, c: (gi, 0, g, c)),
            pl.BlockSpec((None, rows, B_GROUP, ct), lambda g, c: (zi, 0, g, c)),
            pl.BlockSpec((1, 1, ct), lambda g, c: (0, 0, c)),
        ],
        out_specs=blk,
        out_shape=jax.ShapeDtypeStruct((rows, FFT_N2, d), F32),
        compiler_params=_cp(("arbitrary", "arbitrary")),
        name="hyena_conv_idft1",
    )(g3, tab, gate4, z4, fb)


def _dft_tables(n_lat):
    n = 2 * n_lat
    n1 = n // FFT_N2
    b = jnp.arange(FFT_N2, dtype=jnp.int32)[:, None, None]
    k1 = jnp.arange(n1, dtype=jnp.int32)[None, :, None]
    a = jnp.arange(n1, dtype=jnp.int32)[None, None, :]
    th = ((k1 * (a * FFT_N2 + b)) % n).astype(F32) * (2.0 * math.pi / n)
    cr, sn = jnp.cos(th), jnp.sin(th)
    ha = n1 // 2
    crh, snh = cr[:, :, :ha], sn[:, :, :ha]
    w1 = jnp.concatenate([jnp.concatenate([crh, snh], axis=2), jnp.concatenate([-snh, crh], axis=2)], axis=1)
    w1f = jnp.concatenate([cr, -sn], axis=1)
    v = jnp.swapaxes(w1, 1, 2) * (1.0 / n)
    k2 = jnp.arange(FFT_N2, dtype=jnp.int32)
    th2 = ((k2[:, None] * k2[None, :]) % FFT_N2).astype(F32) * (2.0 * math.pi / FFT_N2)
    c2, s2 = jnp.cos(th2), jnp.sin(th2)
    ff = jnp.concatenate([jnp.concatenate([c2, s2], axis=1), jnp.concatenate([-s2, c2], axis=1)], axis=0)
    fi = jnp.concatenate([jnp.concatenate([c2, -s2], axis=1), jnp.concatenate([s2, c2], axis=1)], axis=0)
    return w1.astype(BF16), w1f.astype(BF16), v.astype(BF16), ff.astype(BF16), fi.astype(BF16)


def _hyena_mix(proj3, fparams, fbias, *, n_lat):
    _, bsz, _, d = proj3.shape
    f_w1, f_b1, f_f1, f_w2, f_b2, f_f2, f_w3, decay = fparams
    na = 2 * n_lat // FFT_N2
    w1, w1f, v, ff, fi = _dft_tables(n_lat)
    ct = min(d, 512)
    hd = _filt_feat(f_w1, f_b1, f_f1, f_w2, f_b2, f_f2, n_lat=n_lat)
    w3r = jnp.transpose(f_w3.reshape(HY_HID, 2, 2, d), (1, 2, 0, 3))
    w3r = jnp.zeros((2, 2, LANES, d), F32).at[:, :, :HY_HID].set(w3r)
    kf1 = _filt_s1(hd, w3r, decay.reshape(2, 2, 1, d), w1f, ct=ct)
    kf = _s2(kf1.reshape(2, 2, na, FFT_N2, d), None, 0, ff, None, ct=d, conv=False)
    p3 = proj3.reshape(3, bsz * (n_lat // FFT_N2), FFT_N2, d)
    z4, zi = p3, 2
    for o in range(2):
        o1 = _s1(z4, zi, w1, ct=ct)
        g = _s2(o1.reshape(2, na, FFT_N2, d), kf, o, ff, fi, ct=d, conv=True)
        z4 = _s3(g.reshape(2 * na, FFT_N2, d), v, p3, o, z4, zi, fbias[o].reshape(1, 1, d), ct=ct)[None]
        zi = 0
    return z4.reshape(bsz, n_lat, d)


def _rope_tables(n_tokens):
    rows = n_tokens // GRID_W
    row = jnp.broadcast_to(jnp.arange(rows, dtype=F32)[:, None], (rows, GRID_W)).reshape(-1)
    col = jnp.broadcast_to(jnp.arange(GRID_W, dtype=F32)[None, :], (rows, GRID_W)).reshape(-1)
    axis_dim = QK_ROPE // 2
    inv_freq = 1.0 / (ROPE_THETA ** (jnp.arange(0, axis_dim, 2, dtype=F32) / axis_dim))
    ang = jnp.concatenate([row[:, None] * inv_freq, col[:, None] * inv_freq], axis=-1)
    return jnp.cos(ang), jnp.sin(ang)


def _mla_weights(w_down, g_q, w_uq, g_kv, w_ukv):
    d = w_down.shape[0]
    nh = MLA_HEADS
    kpe = w_down[:, Q_LORA + KV_LORA:]
    w1, w2 = kpe[:, 0::2], kpe[:, 1::2]
    z = jnp.zeros((d, LANES - QK_ROPE), w_down.dtype)
    wd = jnp.concatenate([w_down[:, :Q_LORA + KV_LORA], w1, w2, z, w2, w1, z], axis=1).astype(BF16)
    uq = w_uq.reshape(Q_LORA, nh, QK_NOPE + QK_ROPE)
    pe = uq[:, :, QK_NOPE:]
    uq = jnp.concatenate([uq[:, :, :QK_NOPE], pe[:, :, 0::2], pe[:, :, 1::2]], axis=2)
    wuqT = uq.reshape(Q_LORA, nh * (QK_NOPE + QK_ROPE)).T.astype(BF16)
    ukv = w_ukv.reshape(KV_LORA, nh, QK_NOPE + V_DIM)
    wuk = ukv[:, :, :QK_NOPE].reshape(KV_LORA, nh * QK_NOPE).astype(BF16)
    wuvT = ukv[:, :, QK_NOPE:].reshape(KV_LORA, nh * V_DIM).T.astype(BF16)
    return wd, g_q.reshape(1, -1), g_kv.reshape(1, -1), wuk, wuqT, wuvT


def kernel(x, c, ctx, c_ctx, ada_w, ada_b, norm_mix_g, norm_ffn_g, mla_w_down, mla_g_q, mla_w_uq, mla_g_kv, mla_w_ukv, mla_w_o, hy_w_in, hy_b_in, hy_conv_w, hy_conv_b, hy_f_w1, hy_f_b1, hy_f_freq1, hy_f_w2, hy_f_b2, hy_f_freq2, hy_f_w3, hy_decay, hy_bias, hy_w_out, hy_b_out, moe_w_r, moe_b_r, moe_w_in, moe_b_in, moe_w_out, moe_b_out, final_g):
    bsz, n_lat, d = x.shape
    n_ctx = ctx.shape[1]
    depth = ada_w.shape[0]
    assert bsz == 2 and d == MLA_HEADS * V_DIM and n_lat % 512 == 0 and n_ctx % 128 == 0
    assert depth == 2

    cond8 = jnp.zeros((8, d), F32).at[:bsz].set(c).at[bsz].set(c_ctx)
    mods = _ada(cond8, ada_w, ada_b)

    def mod(i, j, rows):
        return mods[i, rows, j * d:(j + 1) * d][:, None, :]

    lat_rows = slice(0, bsz)
    ctx_rows = slice(bsz, bsz + 1)
    xl = x
    for i in range(depth):
        kind, j = i % 2, i // 2
        sh1, sc1, g1 = (mod(i, m, lat_rows) for m in range(3))
        sh2, sc2, g2 = (mod(i, m, lat_rows) for m in range(3, 6))
        gm = norm_mix_g[i].reshape(1, d)
        if kind == 0:
            wts = _mla_weights(mla_w_down[j], mla_g_q[j], mla_w_uq[j], mla_g_kv[j], mla_w_ukv[j])
            cos, sin = _rope_tables(n_lat)
            zl = jnp.zeros((n_lat, LANES - QK_ROPE), F32)
            tabs = (jnp.concatenate([cos, cos, zl], axis=1), jnp.concatenate([-sin, sin, zl], axis=1), cos.T, sin.T)
            tq = tv = 512
            tk = 1024 if n_lat % 2048 == 0 else 512
            qT, k, vT = _mla_proj(xl, gm, sh1, sc1, wts, tabs, need_q=True, tm=tv, tk=tv)
            half = QK_ROPE // 2
            one_c = jnp.concatenate([jnp.ones((n_ctx, QK_ROPE), F32), jnp.zeros((n_ctx, LANES - QK_ROPE), F32)], axis=1)
            tabs_c = (one_c, jnp.zeros((n_ctx, LANES), F32), jnp.ones((half, n_ctx), F32), jnp.zeros((half, n_ctx), F32))
            kc, vTc = _mla_proj(ctx, gm, mod(i, 0, ctx_rows), mod(i, 1, ctx_rows), wts, tabs_c,
                                need_q=False, tm=n_ctx, tk=n_ctx)
            o = _attention(qT, k, vT, kc, vTc, tq=tq, tk=tk)
            wo = mla_w_o[j].astype(BF16)
            bo = jnp.zeros((1, d), F32)
            transposed = True
        else:
            proj3 = _hy_in(xl, gm, sh1, sc1, hy_w_in[j].astype(BF16), hy_b_in[j].reshape(1, -1), hy_conv_w[j],
                           hy_conv_b[j].reshape(1, -1), tm=512)
            fparams = (hy_f_w1[j], hy_f_b1[j], hy_f_freq1[j], hy_f_w2[j], hy_f_b2[j], hy_f_freq2[j], hy_f_w3[j],
                       hy_decay[j])
            o = _hyena_mix(proj3, fparams, hy_bias[j], n_lat=n_lat)
            wo = hy_w_out[j].astype(BF16)
            bo = hy_b_out[j].reshape(1, d)
            transposed = False
        wr = jnp.zeros((d, LANES), F32).at[:, :N_EXPERTS].set(moe_w_r[i])
        wrh = wr.astype(BF16)
        wrl = (wr - wrh.astype(F32)).astype(BF16)
        br = jnp.zeros((1, LANES), F32).at[0, :N_EXPERTS].set(moe_b_r[i])
        xl, fl, topi, gates, rank, cnt = _post(o, wo, bo, xl, g1, norm_ffn_g[i].reshape(1, d), sh2, sc2, wrh, wrl, br,
                                               transposed=transposed, tm=512)
        xl = _moe(fl, topi, gates, rank, cnt, xl, g2, final_g.reshape(1, d), moe_w_in[i], moe_b_in[i],
                  moe_w_out[i], moe_b_out[i], final=(i == depth - 1))
    return xl
```

```python
import functools
import math

import jax
import jax.numpy as jnp
from jax import lax
from jax.experimental import pallas as pl
from jax.experimental.pallas import tpu as pltpu

F32 = jnp.float32
BF16 = jnp.bfloat16

EPS = 1e-6
GRID_W = 64
MLA_HEADS = 8
QK_NOPE = 128
QK_ROPE = 64
V_DIM = 128
Q_LORA = 512
KV_LORA = 256
ROPE_THETA = 10000.0
MLA_SCALE = (QK_NOPE + QK_ROPE) ** -0.5
QK_PAD = 256

HY_EMB = 33
HY_BANDS = (HY_EMB - 1) // 2
HY_HID = 64
FFT_N2 = 128
B_GROUP = 8

N_EXPERTS = 32
TOP_K = 4
SWIGLU_LIMIT = 7.0
SWIGLU_ALPHA = 1.702
MOE_TM = 256
LANES = 128

VMEM_LIMIT = 56 * 1024 * 1024


def _cp(sem, vmem=VMEM_LIMIT):
    return pltpu.CompilerParams(dimension_semantics=sem, vmem_limit_bytes=vmem)


def _dot(a, b):
    return jnp.dot(a, b, preferred_element_type=F32)


def _dot_hi(a, b):
    return jnp.dot(a, b, preferred_element_type=F32, precision=lax.Precision.HIGHEST)


def _dot3(a, b):
    ah = a.astype(BF16)
    al = (a - ah.astype(F32)).astype(BF16)
    bh = b.astype(BF16)
    bl = (b - bh.astype(F32)).astype(BF16)
    return _dot(ah, bh) + (_dot(al, bh) + _dot(ah, bl))


def _rms(x, g):
    return x * lax.rsqrt(jnp.mean(x * x, axis=-1, keepdims=True) + EPS) * g


def _ada_body(c_ref, w_ref, b_ref, o_ref):
    c = c_ref[...]
    s = c * jax.nn.sigmoid(c)
    o_ref[0] = _dot(s.astype(BF16), w_ref[0].astype(BF16)) + b_ref[0]


def _ada(cond8, ada_w, ada_b):
    depth, d, n = ada_w.shape
    tn = n // 4
    return pl.pallas_call(
        _ada_body,
        grid=(depth, n // tn),
        in_specs=[
            pl.BlockSpec((8, d), lambda i, j: (0, 0)),
            pl.BlockSpec((1, d, tn), lambda i, j: (i, 0, j)),
            pl.BlockSpec((1, 1, tn), lambda i, j: (i, 0, j)),
        ],
        out_specs=pl.BlockSpec((1, 8, tn), lambda i, j: (i, 0, j)),
        out_shape=jax.ShapeDtypeStruct((depth, 8, n), F32),
        compiler_params=_cp(("arbitrary", "arbitrary")),
        name="ada_mod",
    )(cond8, ada_w, ada_b.reshape(depth, 1, n))


def _mla_proj_body(x_ref, g_ref, sh_ref, sc_ref, wd_ref, gq_ref, gkv_ref, wuk_ref, wuqT_ref, wuvT_ref,
                   ct_ref, st_ref, cT_ref, sT_ref, *out_refs, need_q, tk):
    if need_q:
        qT_ref, k_ref, vT_ref = out_refs
    else:
        k_ref, vT_ref = out_refs
    nh = MLA_HEADS
    x = x_ref[0]
    h = _rms(x, g_ref[...]) * (1.0 + sc_ref[0]) + sh_ref[0]
    lat = _dot(h.astype(BF16), wd_ref[...])
    o_kv = Q_LORA
    o_a = Q_LORA + KV_LORA
    kvn = _rms(lat[:, o_kv:o_a], gkv_ref[...])
    kr = (lat[:, o_a:o_a + LANES] * ct_ref[...] + lat[:, o_a + LANES:o_a + 2 * LANES] * st_ref[...]).astype(BF16)
    knope = _dot(kvn.astype(BF16), wuk_ref[...])
    for hh in range(nh):
        k_ref[0, hh, :, 0:QK_NOPE] = knope[:, hh * QK_NOPE:(hh + 1) * QK_NOPE].astype(BF16)
        k_ref[0, hh, :, QK_NOPE:QK_PAD] = kr
    vT = _dot(wuvT_ref[...], kvn.T.astype(BF16))
    tm = x.shape[0]
    for hh in range(nh):
        for c in range(tm // tk):
            vT_ref[0, hh, c] = vT[hh * V_DIM:(hh + 1) * V_DIM, c * tk:(c + 1) * tk].astype(BF16)
    if need_q:
        qn = _rms(lat[:, :Q_LORA], gq_ref[...])
        qT = _dot(wuqT_ref[...], qn.T.astype(BF16)) * (MLA_SCALE * math.log2(math.e))
        c = cT_ref[...]
        s = sT_ref[...]
        hw = QK_NOPE + QK_ROPE
        half = QK_ROPE // 2
        for hh in range(nh):
            base = hh * hw
            x1 = qT[base + QK_NOPE:base + QK_NOPE + half]
            x2 = qT[base + QK_NOPE + half:base + hw]
            qT_ref[0, hh, 0:QK_NOPE] = qT[base:base + QK_NOPE].astype(BF16)
            qT_ref[0, hh, QK_NOPE:QK_NOPE + half] = (x1 * c - x2 * s).astype(BF16)
            qT_ref[0, hh, QK_NOPE + half:hw] = (x1 * s + x2 * c).astype(BF16)
            qT_ref[0, hh, hw:QK_PAD] = jnp.zeros((QK_PAD - hw, tm), BF16)


def _mla_proj(x, g, sh, sc, wts, tabs, *, need_q, tm, tk):
    bsz, n, d = x.shape
    nh = MLA_HEADS
    wd, gq, gkv, wuk, wuqT, wuvT = wts
    ct, st, cT, sT = tabs
    nsh = sh.shape[0]
    full = lambda a: pl.BlockSpec(a.shape, lambda b, i: (0,) * a.ndim)
    in_specs = [
        pl.BlockSpec((1, tm, d), lambda b, i: (b, i, 0)),
        full(g),
        pl.BlockSpec((1, 1, d), lambda b, i: (b % nsh, 0, 0)),
        pl.BlockSpec((1, 1, d), lambda b, i: (b % nsh, 0, 0)),
        full(wd), full(gq), full(gkv), full(wuk), full(wuqT), full(wuvT),
        pl.BlockSpec((tm, LANES), lambda b, i: (i, 0)),
        pl.BlockSpec((tm, LANES), lambda b, i: (i, 0)),
        pl.BlockSpec((QK_ROPE // 2, tm), lambda b, i: (0, i)),
        pl.BlockSpec((QK_ROPE // 2, tm), lambda b, i: (0, i)),
    ]
    out_specs = [
        pl.BlockSpec((1, nh, tm, QK_PAD), lambda b, i: (b, 0, i, 0)),
        pl.BlockSpec((1, nh, tm // tk, V_DIM, tk), lambda b, i: (b, 0, i, 0, 0)),
    ]
    out_shape = [
        jax.ShapeDtypeStruct((bsz, nh, n, QK_PAD), BF16),
        jax.ShapeDtypeStruct((bsz, nh, n // tk, V_DIM, tk), BF16),
    ]
    if need_q:
        out_specs = [pl.BlockSpec((1, nh, QK_PAD, tm), lambda b, i: (b, 0, 0, i))] + out_specs
        out_shape = [jax.ShapeDtypeStruct((bsz, nh, QK_PAD, n), BF16)] + out_shape
    return pl.pallas_call(
        functools.partial(_mla_proj_body, need_q=need_q, tk=tk),
        grid=(bsz, n // tm),
        in_specs=in_specs,
        out_specs=out_specs,
        out_shape=out_shape,
        compiler_params=_cp(("arbitrary", "arbitrary")),
        name="mla_proj_q" if need_q else "mla_proj_ctx",
    )(x, g, sh, sc, wd, gq, gkv, wuk, wuqT, wuvT, ct, st, cT, sT)


ACC_ROWS = V_DIM + 16
SM_STRIP = 64


def _attn_body(qT_ref, k_ref, vT_ref, kc_ref, vTc_ref, o_ref, s0, s1, p0, p1, sc, pc, acc, m_scr, x0, x1, xc,
               a0, a1, ac, *, tk):
    nchunk = k_ref.shape[2] // tk

    def scores(kblk, s_ref, mx_ref):
        r = _dot(kblk, qT_ref[0, 0])
        s_ref[...] = r
        mx_ref[...] = jnp.max(r, axis=0, keepdims=True)

    def probs(s_ref, mx_ref, p_ref, a_ref):
        m_old = m_scr[...]
        m_new = jnp.maximum(m_old, mx_ref[...])
        m_scr[...] = m_new
        a_ref[...] = jnp.exp2(m_old - m_new)
        for r in range(0, s_ref.shape[0], SM_STRIP):
            p_ref[r:r + SM_STRIP] = jnp.exp2(s_ref[r:r + SM_STRIP] - m_new).astype(BF16)

    def accumulate(p_ref, a_ref, vblk):
        lhs = jnp.concatenate([vblk, jnp.ones((ACC_ROWS - V_DIM, vblk.shape[1]), BF16)], axis=0)
        acc[...] = a_ref[...] * acc[...] + _dot(lhs, p_ref[...])

    def kchunk(i):
        i = jnp.minimum(i, nchunk - 1)
        return k_ref[0, 0, pl.ds(pl.multiple_of(i * tk, tk), tk), :]

    def vchunk(i):
        nsub = tk // vT_ref.shape[-1]
        return jnp.concatenate([vT_ref[0, 0, i * nsub + u] for u in range(nsub)], axis=1)

    m_scr[...] = jnp.full(m_scr.shape, -jnp.inf, F32)
    acc[...] = jnp.zeros(acc.shape, F32)
    scores(kc_ref[0, 0], sc, xc)
    scores(kchunk(0), s0, x0)
    probs(sc, xc, pc, ac)
    accumulate(pc, ac, vTc_ref[0, 0, 0])

    def body(j, carry):
        t = 2 * j
        scores(kchunk(t + 1), s1, x1)
        probs(s0, x0, p0, a0)
        accumulate(p0, a0, vchunk(t))
        scores(kchunk(t + 2), s0, x0)
        probs(s1, x1, p1, a1)
        accumulate(p1, a1, vchunk(t + 1))
        return carry

    lax.fori_loop(0, nchunk // 2, body, 0)
    o_ref[0, 0] = (acc[0:V_DIM] / acc[V_DIM:V_DIM + 1]).astype(BF16)


def _attention(qT, k, vT, kc, vTc, *, tq, tk):
    bsz, nh, _, n = qT.shape
    nc = kc.shape[2]
    tv = vT.shape[-1]
    assert (n // tk) % 2 == 0 and tk % tv == 0
    return pl.pallas_call(
        functools.partial(_attn_body, tk=tk),
        grid=(bsz, nh, n // tq),
        in_specs=[
            pl.BlockSpec((1, 1, QK_PAD, tq), lambda b, h, i: (b, h, 0, i)),
            pl.BlockSpec((1, 1, n, QK_PAD), lambda b, h, i: (b, h, 0, 0)),
            pl.BlockSpec((1, 1, n // tv, V_DIM, tv), lambda b, h, i: (b, h, 0, 0, 0)),
            pl.BlockSpec((1, 1, nc, QK_PAD), lambda b, h, i: (b, h, 0, 0)),
            pl.BlockSpec((1, 1, 1, V_DIM, nc), lambda b, h, i: (b, h, 0, 0, 0)),
        ],
        out_specs=pl.BlockSpec((1, 1, V_DIM, tq), lambda b, h, i: (b, h, 0, i)),
        out_shape=jax.ShapeDtypeStruct((bsz, nh, V_DIM, n), BF16),
        scratch_shapes=[pltpu.VMEM((tk, tq), F32), pltpu.VMEM((tk, tq), F32),
                        pltpu.VMEM((tk, tq), BF16), pltpu.VMEM((tk, tq), BF16),
                        pltpu.VMEM((nc, tq), F32), pltpu.VMEM((nc, tq), BF16),
                        pltpu.VMEM((ACC_ROWS, tq), F32), pltpu.VMEM((1, tq), F32),
                        pltpu.VMEM((1, tq), F32), pltpu.VMEM((1, tq), F32), pltpu.VMEM((1, tq), F32),
                        pltpu.VMEM((1, tq), F32), pltpu.VMEM((1, tq), F32), pltpu.VMEM((1, tq), F32)],
        compiler_params=_cp(("arbitrary", "arbitrary", "arbitrary")),
        name="mla_attention",
    )(qT, k, vT, kc, vTc)


def _post_body(o_ref, wo_ref, bo_ref, x_ref, g1_ref, gf_ref, sh_ref, sc_ref, wrh_ref, wrl_ref, br_ref, tri_ref,
               xl_ref, fl_ref, ti_ref, gt_ref, rk_ref, cnt_ref, *, transposed):
    @pl.when((pl.program_id(0) == 0) & (pl.program_id(1) == 0))
    def _():
        cnt_ref[...] = jnp.zeros_like(cnt_ref)

    tm = x_ref.shape[1]
    if transposed:
        oT = o_ref[0].astype(F32).reshape(MLA_HEADS * V_DIM, tm)
        o = oT.T.astype(BF16)
    else:
        o = o_ref[0].astype(BF16)
    y = _dot(o, wo_ref[...]) + bo_ref[...]
    xl = x_ref[0] + g1_ref[0] * y
    xl_ref[0] = xl
    fl = _rms(xl, gf_ref[...]) * (1.0 + sc_ref[0]) + sh_ref[0]
    _to_rows(fl_ref, fl)
    flh = fl.astype(BF16)
    fll = (fl - flh.astype(F32)).astype(BF16)
    logits = _dot(flh, wrh_ref[...]) + (_dot(fll, wrh_ref[...]) + _dot(flh, wrl_ref[...])) + br_ref[...]
    lane = lax.broadcasted_iota(jnp.int32, (tm, LANES), 1).astype(F32)
    neg = jnp.float32(-jnp.inf)
    work = jnp.where(lane < N_EXPERTS, logits, neg)
    vals, idxs = [], []
    onehot = jnp.zeros((tm, LANES), F32)
    for _ in range(TOP_K):
        mk = jnp.max(work, axis=-1, keepdims=True)
        ik = jnp.min(jnp.where(work == mk, lane, float(LANES)), axis=-1, keepdims=True)
        sel = lane == ik
        onehot = jnp.where(sel, 1.0, onehot)
        work = jnp.where(sel, neg, work)
        vals.append(mk)
        idxs.append(ik)
    es = [jnp.exp(v - vals[0]) for v in vals]
    den = es[0] + es[1] + es[2] + es[3]
    pre = _dot(tri_ref[...], onehot.astype(BF16)) + cnt_ref[...]
    ti = jnp.zeros((tm, LANES), F32)
    gt = jnp.zeros((tm, LANES), F32)
    rk = jnp.zeros((tm, LANES), F32)
    for kk in range(TOP_K):
        rank = jnp.sum(jnp.where(lane == idxs[kk], pre, 0.0), axis=-1, keepdims=True)
        ti = jnp.where(lane == kk, idxs[kk], ti)
        gt = jnp.where(lane == kk, es[kk] / den, gt)
        rk = jnp.where(lane == kk, rank, rk)
    ti_ref[...] = ti[:, :TOP_K].astype(jnp.int32)
    gt_ref[...] = gt[:, :TOP_K]
    rk_ref[...] = rk[:, :TOP_K].astype(jnp.int32)
    cnt_ref[...] += jnp.sum(onehot, axis=0, keepdims=True)


def _post(o, wo, bo, x, g1, gf, sh, sc, wrh, wrl, br, *, transposed, tm):
    bsz, n, d = x.shape
    t = bsz * n
    nt = n // tm
    tri = (lax.broadcasted_iota(jnp.int32, (tm, tm), 0) > lax.broadcasted_iota(jnp.int32, (tm, tm), 1)).astype(BF16)
    full = lambda a: pl.BlockSpec(a.shape, lambda b, i: (0,) * a.ndim)
    per_b = pl.BlockSpec((1, 1, d), lambda b, i: (b, 0, 0))
    if transposed:
        o_spec = pl.BlockSpec((1, MLA_HEADS, V_DIM, tm), lambda b, i: (b, 0, 0, i))
    else:
        o_spec = pl.BlockSpec((1, tm, d), lambda b, i: (b, i, 0))
    tok = lambda w: pl.BlockSpec((tm, w), lambda b, i: (b * nt + i, 0))
    return pl.pallas_call(
        functools.partial(_post_body, transposed=transposed),
        grid=(bsz, nt),
        in_specs=[o_spec, full(wo), full(bo), pl.BlockSpec((1, tm, d), lambda b, i: (b, i, 0)), per_b, full(gf),
                  per_b, per_b, full(wrh), full(wrl), full(br), full(tri)],
        out_specs=[pl.BlockSpec((1, tm, d), lambda b, i: (b, i, 0)),
                   pl.BlockSpec((tm * ROW_SUB, LANES), lambda b, i: (b * nt + i, 0)),
                   tok(TOP_K), tok(TOP_K), tok(TOP_K), pl.BlockSpec((1, LANES), lambda b, i: (0, 0))],
        out_shape=[jax.ShapeDtypeStruct((bsz, n, d), F32), jax.ShapeDtypeStruct((t * ROW_SUB, LANES), F32),
                   jax.ShapeDtypeStruct((t, TOP_K), jnp.int32), jax.ShapeDtypeStruct((t, TOP_K), F32),
                   jax.ShapeDtypeStruct((t, TOP_K), jnp.int32), jax.ShapeDtypeStruct((1, LANES), F32)],
        compiler_params=_cp(("arbitrary", "arbitrary")),
        name="post_attn" if transposed else "post_hyena",
    )(o, wo, bo, x, g1, gf, sh, sc, wrh, wrl, br, tri)


ROW_SUB = 8


def _row_slice(i):
    return pl.ds(pl.multiple_of(i * ROW_SUB, ROW_SUB), ROW_SUB)


def _to_rows(ref, x):
    for s in range(ROW_SUB):
        ref[pl.ds(s, x.shape[0], stride=ROW_SUB), :] = x[:, s * LANES:(s + 1) * LANES]


def _from_rows(ref, lo, hi):
    return jnp.concatenate([ref[pl.ds(lo * ROW_SUB + s, hi - lo, stride=ROW_SUB), :] for s in range(ROW_SUB)], axis=1)


def _dispatch_body(dest_ref, fl_ref, xs_in, xs_out, sem, *, td):
    del xs_in

    def issue(t, carry):
        for kk in range(TOP_K):
            d = dest_ref[0, 0, t * TOP_K + kk]
            pltpu.make_async_copy(fl_ref.at[_row_slice(t)], xs_out.at[_row_slice(d)], sem).start()
        return carry

    lax.fori_loop(0, td, issue, 0, unroll=2)

    def drain(t, carry):
        pltpu.make_async_copy(fl_ref.at[_row_slice(0)], xs_out.at[_row_slice(0)], sem).wait()
        return carry

    lax.fori_loop(0, td * TOP_K, drain, 0, unroll=8)


def _dispatch(dest, fl, n_rows, *, td):
    t = fl.shape[0] // ROW_SUB
    dest3 = dest.reshape(t // td, 1, td * TOP_K)
    xs0 = jnp.zeros((n_rows * ROW_SUB, LANES), fl.dtype)
    return pl.pallas_call(
        functools.partial(_dispatch_body, td=td),
        grid=(t // td,),
        in_specs=[
            pl.BlockSpec((1, 1, td * TOP_K), lambda i: (i, 0, 0), memory_space=pltpu.SMEM),
            pl.BlockSpec((td * ROW_SUB, LANES), lambda i: (i, 0)),
            pl.BlockSpec(memory_space=pl.ANY),
        ],
        out_specs=pl.BlockSpec(memory_space=pl.ANY),
        out_shape=jax.ShapeDtypeStruct(xs0.shape, fl.dtype),
        scratch_shapes=[pltpu.SemaphoreType.DMA(())],
        input_output_aliases={2: 0},
        compiler_params=_cp(("arbitrary",)),
        name="moe_dispatch",
    )(dest3, fl, xs0)


def _expert_body(be_ref, nu_ref, xs_ref, win_ref, bin_ref, wout_ref, bout_ref, ys_ref, win_s, wout_s):
    b = pl.program_id(0)
    dff = wout_ref.shape[1]

    @pl.when(b < nu_ref[0])
    def _():
        prev = be_ref[jnp.maximum(b - 1, 0)]

        @pl.when((b == 0) | (prev != be_ref[b]))
        def _():
            win_s[...] = win_ref[0].astype(BF16)
            wout_s[...] = wout_ref[0].astype(BF16)

        x = _from_rows(xs_ref, 0, xs_ref.shape[0] // ROW_SUB).astype(BF16)
        gu = _dot(x, win_s[...]) + bin_ref[0]
        gate = jnp.minimum(gu[:, :dff], SWIGLU_LIMIT)
        lin = jnp.clip(gu[:, dff:], -SWIGLU_LIMIT, SWIGLU_LIMIT)
        act = gate * jax.nn.sigmoid(SWIGLU_ALPHA * gate) * (lin + 1.0)
        _to_rows(ys_ref, _dot(act.astype(BF16), wout_s[...]) + bout_ref[0])

    @pl.when(b >= nu_ref[0])
    def _():
        ys_ref[...] = jnp.zeros_like(ys_ref)


def _experts(blk_exp, n_used, xs, w_in, b_in, w_out, b_out):
    n_rows = xs.shape[0] // ROW_SUB
    ne, d, f2 = w_in.shape
    dff = w_out.shape[1]
    tm = MOE_TM
    grid_spec = pltpu.PrefetchScalarGridSpec(
        num_scalar_prefetch=2,
        grid=(n_rows // tm,),
        in_specs=[
            pl.BlockSpec((tm * ROW_SUB, LANES), lambda b, be, nu: (b, 0)),
            pl.BlockSpec((1, d, f2), lambda b, be, nu: (be[b], 0, 0)),
            pl.BlockSpec((1, 1, f2), lambda b, be, nu: (be[b], 0, 0)),
            pl.BlockSpec((1, dff, d), lambda b, be, nu: (be[b], 0, 0)),
            pl.BlockSpec((1, 1, d), lambda b, be, nu: (be[b], 0, 0)),
        ],
        out_specs=pl.BlockSpec((tm * ROW_SUB, LANES), lambda b, be, nu: (b, 0)),
        scratch_shapes=[pltpu.VMEM((d, f2), BF16), pltpu.VMEM((dff, d), BF16)],
    )
    return pl.pallas_call(
        _expert_body,
        grid_spec=grid_spec,
        out_shape=jax.ShapeDtypeStruct(xs.shape, F32),
        compiler_params=_cp(("arbitrary",)),
        name="moe_experts",
    )(blk_exp, n_used, xs, w_in, b_in.reshape(ne, 1, f2), w_out, b_out.reshape(ne, 1, d))


def _combine_body(dest_ref, ys_hbm, gt_ref, xl_ref, g2_ref, fg_ref, out_ref, buf, sem, *, tc, final):
    def issue(t, carry):
        for kk in range(TOP_K):
            d = dest_ref[0, 0, t * TOP_K + kk]
            pltpu.make_async_copy(ys_hbm.at[_row_slice(d)], buf.at[_row_slice(kk * tc + t)], sem).start()
        return carry

    lax.fori_loop(0, tc, issue, 0, unroll=2)

    def drain(t, carry):
        pltpu.make_async_copy(ys_hbm.at[_row_slice(0)], buf.at[_row_slice(0)], sem).wait()
        return carry

    lax.fori_loop(0, tc * TOP_K, drain, 0, unroll=8)
    gt = gt_ref[...]
    y = gt[:, 0:1] * _from_rows(buf, 0, tc)
    for kk in range(1, TOP_K):
        y = y + gt[:, kk:kk + 1] * _from_rows(buf, kk * tc, (kk + 1) * tc)
    xl = xl_ref[0] + g2_ref[0] * y
    out_ref[0] = _rms(xl, fg_ref[...]) if final else xl


def _combine(dest, ys, gates, xl, g2, fg, *, tc, final):
    bsz, n, d = xl.shape
    t = bsz * n
    nt = n // tc
    dest3 = dest.reshape(t // tc, 1, tc * TOP_K)
    return pl.pallas_call(
        functools.partial(_combine_body, tc=tc, final=final),
        grid=(bsz, nt),
        in_specs=[
            pl.BlockSpec((1, 1, tc * TOP_K), lambda b, i: (b * nt + i, 0, 0), memory_space=pltpu.SMEM),
            pl.BlockSpec(memory_space=pl.ANY),
            pl.BlockSpec((tc, TOP_K), lambda b, i: (b * nt + i, 0)),
            pl.BlockSpec((1, tc, d), lambda b, i: (b, i, 0)),
            pl.BlockSpec((1, 1, d), lambda b, i: (b, 0, 0)),
            pl.BlockSpec((1, d), lambda b, i: (0, 0)),
        ],
        out_specs=pl.BlockSpec((1, tc, d), lambda b, i: (b, i, 0)),
        out_shape=jax.ShapeDtypeStruct((bsz, n, d), F32),
        scratch_shapes=[pltpu.VMEM((TOP_K * tc * ROW_SUB, LANES), F32), pltpu.SemaphoreType.DMA(())],
        compiler_params=_cp(("arbitrary", "arbitrary")),
        name="moe_combine",
    )(dest3, ys, gates, xl, g2, fg)


def _moe(fl, topi, gates, rank, cnt, xl, g2, fg, w_in, b_in, w_out, b_out, *, final):
    t = fl.shape[0] // ROW_SUB
    tm = MOE_TM
    counts = cnt[0, :N_EXPERTS].astype(jnp.int32)
    padded = (counts + tm - 1) // tm * tm
    pad_end = jnp.cumsum(padded)
    pad_start = pad_end - padded
    dest = jnp.take(pad_start, topi) + rank
    nb = t * TOP_K // tm + N_EXPERTS
    blk_start = jnp.arange(nb, dtype=jnp.int32) * tm
    blk_exp = jnp.minimum(jnp.sum((pad_end[None, :] <= blk_start[:, None]).astype(jnp.int32), axis=1), N_EXPERTS - 1)
    n_used = (pad_end[-1:] // tm).astype(jnp.int32)
    xs = _dispatch(dest, fl, nb * tm, td=256)
    ys = _experts(blk_exp, n_used, xs, w_in, b_in, w_out, b_out)
    return _combine(dest, ys, gates, xl, g2, fg, tc=256, final=final)


def _hy_in_body(x_ref, xp_ref, xn_ref, g_ref, sh_ref, sc_ref, w_ref, b_ref, cw_ref, cb_ref, o_ref, *, nt):
    i = pl.program_id(2)
    w = w_ref[...]

    def proj(xx):
        h = _rms(xx, g_ref[...]) * (1.0 + sc_ref[0]) + sh_ref[0]
        return _dot(h.astype(BF16), w) + b_ref[...]

    p = proj(x_ref[0])
    tm = p.shape[0]
    ph = proj(jnp.concatenate([xp_ref[0], xn_ref[0]], axis=0))
    prev = jnp.where(i > 0, ph[7:8], 0.0)
    nxt = jnp.where(i < nt - 1, ph[8:9], 0.0)
    row = lax.broadcasted_iota(jnp.int32, (tm, 1), 0)
    up = jnp.where(row == 0, prev, pltpu.roll(p, 1, axis=0))
    dn = jnp.where(row == tm - 1, nxt, pltpu.roll(p, tm - 1, axis=0))
    cw = cw_ref[...]
    o_ref[0, 0] = up * cw[0:1] + p * cw[1:2] + dn * cw[2:3] + cb_ref[...]


def _hy_in(x, g, sh, sc, w, b, cw, cb, *, tm):
    bsz, n, d = x.shape
    nt = n // tm
    hb = tm // 8
    per_b = pl.BlockSpec((1, 1, d), lambda j, bb, i: (bb, 0, 0))
    return pl.pallas_call(
        functools.partial(_hy_in_body, nt=nt),
        grid=(3, bsz, nt),
        in_specs=[
            pl.BlockSpec((1, tm, d), lambda j, bb, i: (bb, i, 0)),
            pl.BlockSpec((1, 8, d), lambda j, bb, i: (bb, jnp.maximum(i * hb - 1, 0), 0)),
            pl.BlockSpec((1, 8, d), lambda j, bb, i: (bb, jnp.minimum((i + 1) * hb, n // 8 - 1), 0)),
            pl.BlockSpec((1, d), lambda j, bb, i: (0, 0)),
            per_b, per_b,
            pl.BlockSpec((d, d), lambda j, bb, i: (0, j)),
            pl.BlockSpec((1, d), lambda j, bb, i: (0, j)),
            pl.BlockSpec((3, d), lambda j, bb, i: (0, j)),
            pl.BlockSpec((1, d), lambda j, bb, i: (0, j)),
        ],
        out_specs=pl.BlockSpec((1, 1, tm, d), lambda j, bb, i: (j, bb, i, 0)),
        out_shape=jax.ShapeDtypeStruct((3, bsz, n, d), F32),
        compiler_params=_cp(("arbitrary", "arbitrary", "arbitrary")),
        name="hyena_in_proj",
    )(x, x, x, g, sh, sc, w, b, cw, cb)


def _filt_feat_body(w1_ref, b1_ref, f1_ref, w2_ref, b2_ref, f2_ref, o_ref, *, n_lat):
    na = o_ref.shape[1]
    a = lax.broadcasted_iota(jnp.int32, (na, 1), 0)
    lane = lax.broadcasted_iota(jnp.int32, (na, LANES), 1)
    band_idx = jnp.where(lane <= HY_BANDS, lane - 1, lane - 1 - HY_BANDS).astype(F32)
    band = 1e-4 + band_idx * ((HY_BANDS - 1 - 1e-4) / (HY_BANDS - 1))
    for j in range(B_GROUP):
        r = a * FFT_N2 + (pl.program_id(0) * B_GROUP + j)
        pos = jnp.where(r < n_lat, r, 2 * n_lat - r).astype(F32)
        tn = pos / float(max(n_lat - 1, 1))
        ang = ((2.0 * math.pi / n_lat) * pos) * band
        z = jnp.where(lane == 0, tn, jnp.where(lane <= HY_BANDS, jnp.cos(ang),
                                               jnp.where(lane < HY_EMB, -jnp.sin(ang), 0.0)))
        h1 = jnp.sin(f1_ref[...] * (_dot_hi(z, w1_ref[...]) + b1_ref[...]))
        h2 = jnp.sin(f2_ref[...] * (_dot_hi(h1, w2_ref[...]) + b2_ref[...]))
        valid = (r != n_lat).astype(F32)
        o_ref[j] = jnp.where(lane == HY_HID, tn, jnp.where(lane == HY_HID + 1, valid, h2))


def _filt_feat(w1, b1, f1, w2, b2, f2, *, n_lat):
    na = 2 * n_lat // FFT_N2
    w1p = jnp.zeros((LANES, LANES), F32).at[:HY_EMB, :HY_HID].set(w1)
    w2p = jnp.zeros((LANES, LANES), F32).at[:HY_HID, :HY_HID].set(w2)
    padv = lambda v: jnp.zeros((1, LANES), F32).at[0, :HY_HID].set(v)
    full = lambda shp: pl.BlockSpec(shp, lambda i: (0,) * len(shp))
    return pl.pallas_call(
        functools.partial(_filt_feat_body, n_lat=n_lat),
        grid=(FFT_N2 // B_GROUP,),
        in_specs=[full((LANES, LANES)), full((1, LANES)), full((1, LANES)),
                  full((LANES, LANES)), full((1, LANES)), full((1, LANES))],
        out_specs=pl.BlockSpec((B_GROUP, na, LANES), lambda i: (i, 0, 0)),
        out_shape=jax.ShapeDtypeStruct((FFT_N2, na, LANES), F32),
        compiler_params=_cp(("arbitrary",)),
        name="hyena_filter_features",
    )(w1p, padv(b1), padv(f1), w2p, padv(b2), padv(f2))


def _filt_s1_body(hd_ref, w3_ref, dec_ref, tab_ref, o_ref):
    na = hd_ref.shape[1]
    ha = na // 2
    o2 = _rows2d(o_ref)
    for j in range(B_GROUP):
        f = hd_ref[j]
        tn = f[:, HY_HID:HY_HID + 1]
        valid = f[:, HY_HID + 1:HY_HID + 2]
        top = _dot3(f[:ha], w3_ref[0, 0]) * jnp.exp(-tn[:ha] * jnp.abs(dec_ref[0, 0]))
        bot = _dot3(f[ha:], w3_ref[0, 1]) * jnp.exp(-tn[ha:] * jnp.abs(dec_ref[0, 1])) * valid[ha:]
        hb = jnp.concatenate([top, bot], axis=0).astype(BF16)
        o2[pl.ds(j, 2 * na, stride=B_GROUP), :] = _dot(tab_ref[j], hb)


def _filt_s1(hd, w3r, dec, tab, *, ct):
    _, na, _ = hd.shape
    d = w3r.shape[-1]
    return pl.pallas_call(
        _filt_s1_body,
        grid=(2, FFT_N2 // B_GROUP, d // ct),
        in_specs=[
            pl.BlockSpec((B_GROUP, na, LANES), lambda o, g, c: (g, 0, 0)),
            pl.BlockSpec((1, 2, LANES, ct), lambda o, g, c: (o, 0, 0, c)),
            pl.BlockSpec((1, 2, 1, ct), lambda o, g, c: (o, 0, 0, c)),
            pl.BlockSpec((B_GROUP, 2 * na, na), lambda o, g, c: (g, 0, 0)),
        ],
        out_specs=pl.BlockSpec((None, 2 * na, B_GROUP, ct), lambda o, g, c: (o, 0, g, c)),
        out_shape=jax.ShapeDtypeStruct((2, 2 * na, FFT_N2, d), F32),
        compiler_params=_cp(("arbitrary", "arbitrary", "arbitrary")),
        name="hyena_filter_dft1",
    )(hd, w3r, dec, tab)


def _s2_body(*refs, conv):
    if conv:
        o_ref, kf_ref, ff_ref, fi_ref, g_ref = refs
    else:
        o_ref, ff_ref, g_ref = refs
    ct = o_ref.shape[-1]
    xin = o_ref[...].reshape(2 * FFT_N2, ct).astype(BF16)
    xf = _dot(ff_ref[...], xin)
    if conv:
        xr, xi = xf[:FFT_N2], xf[FFT_N2:]
        kr = kf_ref[0, 0, 0]
        ki = kf_ref[0, 1, 0]
        y = jnp.concatenate([xr * kr - xi * ki, xr * ki + xi * kr], axis=0).astype(BF16)
        xf = _dot(fi_ref[...], y)
    g_ref[...] = xf.reshape(g_ref.shape)


def _s2(o4, kf, order, ff, fi, *, ct, conv):
    n1 = o4.shape[-3]
    d = o4.shape[-1]
    full = lambda a: pl.BlockSpec(a.shape, lambda k, c: (0,) * a.ndim)
    if conv:
        blk = pl.BlockSpec((2, 1, FFT_N2, ct), lambda k, c: (0, k, 0, c))
        in_specs = [blk, pl.BlockSpec((1, 2, 1, FFT_N2, ct), lambda k, c: (order, 0, k, 0, c)), full(ff), full(fi)]
        args = (o4, kf, ff, fi)
        grid = (n1, d // ct)
        out_specs = blk
    else:
        no = o4.shape[0]
        blk = pl.BlockSpec((1, 2, 1, FFT_N2, ct), lambda k, c: (k // n1, 0, k % n1, 0, c))
        in_specs = [blk, full(ff)]
        args = (o4, ff)
        grid = (no * n1, d // ct)
        out_specs = blk
    return pl.pallas_call(
        functools.partial(_s2_body, conv=conv),
        grid=grid,
        in_specs=in_specs,
        out_specs=out_specs,
        out_shape=jax.ShapeDtypeStruct(o4.shape, F32),
        compiler_params=_cp(("arbitrary", "arbitrary")),
        name="hyena_conv_dft2" if conv else "hyena_filter_dft2",
    )(*args)


def _rows2d(ref):
    lead = ref.shape[:-3]
    return ref.reshape(lead + (ref.shape[-3] * B_GROUP, ref.shape[-1]))


def _s1_body(z_ref, tab_ref, o_ref):
    rows, n_out = z_ref.shape[0], o_ref.shape[0]
    z2, o2 = _rows2d(z_ref), _rows2d(o_ref)
    for j in range(B_GROUP):
        zj = z2[pl.ds(j, rows, stride=B_GROUP), :]
        o2[pl.ds(j, n_out, stride=B_GROUP), :] = _dot(tab_ref[j], zj.astype(BF16))


def _s1(z4, zi, tab, *, ct):
    _, rows, _, d = z4.shape
    n_out = tab.shape[1]
    return pl.pallas_call(
        _s1_body,
        grid=(FFT_N2 // B_GROUP, d // ct),
        in_specs=[
            pl.BlockSpec((None, rows, B_GROUP, ct), lambda g, c: (zi, 0, g, c)),
            pl.BlockSpec((B_GROUP, n_out, rows), lambda g, c: (g, 0, 0)),
        ],
        out_specs=pl.BlockSpec((n_out, B_GROUP, ct), lambda g, c: (0, g, c)),
        out_shape=jax.ShapeDtypeStruct((n_out, FFT_N2, d), F32),
        compiler_params=_cp(("arbitrary", "arbitrary")),
        name="hyena_conv_dft1",
    )(z4, tab)


def _s3_body(g_ref, tab_ref, gate_ref, z_ref, fb_ref, o_ref):
    n_in, rows = g_ref.shape[0], o_ref.shape[0]
    g2, o2 = _rows2d(g_ref), _rows2d(o_ref)
    for j in range(B_GROUP):
        gj = g2[pl.ds(j, n_in, stride=B_GROUP), :]
        o2[pl.ds(j, rows, stride=B_GROUP), :] = _dot(tab_ref[j], gj.astype(BF16))
    o_ref[...] = gate_ref[...] * (o_ref[...] + z_ref[...] * fb_ref[...])


def _s3(g3, tab, gate4, gi, z4, zi, fb, *, ct):
    n_in, _, d = g3.shape
    rows = tab.shape[1]
    blk = pl.BlockSpec((rows, B_GROUP, ct), lambda g, c: (0, g, c))
    return pl.pallas_call(
        _s3_body,
        grid=(FFT_N2 // B_GROUP, d // ct),
        in_specs=[
            pl.BlockSpec((n_in, B_GROUP, ct), lambda g, c: (0, g, c)),
            pl.BlockSpec((B_GROUP, rows, n_in), lambda g, c: (g, 0, 0)),
            pl.BlockSpec((None, rows, B_GROUP, ct), lambda g, c: (gi, 0, g, c)),
            pl.BlockSpec((None, rows, B_GROUP, ct), lambda g, c: (zi, 0, g, c)),
            pl.BlockSpec((1, 1, ct), lambda g, c: (0, 0, c)),
        ],
        out_specs=blk,
        out_shape=jax.ShapeDtypeStruct((rows, FFT_N2, d), F32),
        compiler_params=_cp(("arbitrary", "arbitrary")),
        name="hyena_conv_idft1",
    )(g3, tab, gate4, z4, fb)


def _dft_tables(n_lat):
    n = 2 * n_lat
    n1 = n // FFT_N2
    b = jnp.arange(FFT_N2, dtype=jnp.int32)[:, None, None]
    k1 = jnp.arange(n1, dtype=jnp.int32)[None, :, None]
    a = jnp.arange(n1, dtype=jnp.int32)[None, None, :]
    th = ((k1 * (a * FFT_N2 + b)) % n).astype(F32) * (2.0 * math.pi / n)
    cr, sn = jnp.cos(th), jnp.sin(th)
    ha = n1 // 2
    crh, snh = cr[:, :, :ha], sn[:, :, :ha]
    w1 = jnp.concatenate([jnp.concatenate([crh, snh], axis=2), jnp.concatenate([-snh, crh], axis=2)], axis=1)
    w1f = jnp.concatenate([cr, -sn], axis=1)
    v = jnp.swapaxes(w1, 1, 2) * (1.0 / n)
    k2 = jnp.arange(FFT_N2, dtype=jnp.int32)
    th2 = ((k2[:, None] * k2[None, :]) % FFT_N2).astype(F32) * (2.0 * math.pi / FFT_N2)
    c2, s2 = jnp.cos(th2), jnp.sin(th2)
    ff = jnp.concatenate([jnp.concatenate([c2, s2], axis=1), jnp.concatenate([-s2, c2], axis=1)], axis=0)
    fi = jnp.concatenate([jnp.concatenate([c2, -s2], axis=1), jnp.concatenate([s2, c2], axis=1)], axis=0)
    return w1.astype(BF16), w1f.astype(BF16), v.astype(BF16), ff.astype(BF16), fi.astype(BF16)


def _hyena_mix(proj3, fparams, fbias, *, n_lat):
    _, bsz, _, d = proj3.shape
    f_w1, f_b1, f_f1, f_w2, f_b2, f_f2, f_w3, decay = fparams
    na = 2 * n_lat // FFT_N2
    w1, w1f, v, ff, fi = _dft_tables(n_lat)
    ct = LANES
    hd = _filt_feat(f_w1, f_b1, f_f1, f_w2, f_b2, f_f2, n_lat=n_lat)
    w3r = jnp.transpose(f_w3.reshape(HY_HID, 2, 2, d), (1, 2, 0, 3))
    w3r = jnp.zeros((2, 2, LANES, d), F32).at[:, :, :HY_HID].set(w3r)
    kf1 = _filt_s1(hd, w3r, decay.reshape(2, 2, 1, d), w1f, ct=ct)
    kf = _s2(kf1.reshape(2, 2, na, FFT_N2, d), None, 0, ff, None, ct=d, conv=False)
    p3 = proj3.reshape(3, bsz * (n_lat // FFT_N2), FFT_N2, d)
    z4, zi = p3, 2
    for o in range(2):
        o1 = _s1(z4, zi, w1, ct=ct)
        g = _s2(o1.reshape(2, na, FFT_N2, d), kf, o, ff, fi, ct=d, conv=True)
        z4 = _s3(g.reshape(2 * na, FFT_N2, d), v, p3, o, z4, zi, fbias[o].reshape(1, 1, d), ct=ct)[None]
        zi = 0
    return z4.reshape(bsz, n_lat, d)


def _rope_tables(n_tokens):
    rows = n_tokens // GRID_W
    row = jnp.broadcast_to(jnp.arange(rows, dtype=F32)[:, None], (rows, GRID_W)).reshape(-1)
    col = jnp.broadcast_to(jnp.arange(GRID_W, dtype=F32)[None, :], (rows, GRID_W)).reshape(-1)
    axis_dim = QK_ROPE // 2
    inv_freq = 1.0 / (ROPE_THETA ** (jnp.arange(0, axis_dim, 2, dtype=F32) / axis_dim))
    ang = jnp.concatenate([row[:, None] * inv_freq, col[:, None] * inv_freq], axis=-1)
    return jnp.cos(ang), jnp.sin(ang)


def _mla_weights(w_down, g_q, w_uq, g_kv, w_ukv):
    d = w_down.shape[0]
    nh = MLA_HEADS
    kpe = w_down[:, Q_LORA + KV_LORA:]
    w1, w2 = kpe[:, 0::2], kpe[:, 1::2]
    z = jnp.zeros((d, LANES - QK_ROPE), w_down.dtype)
    wd = jnp.concatenate([w_down[:, :Q_LORA + KV_LORA], w1, w2, z, w2, w1, z], axis=1).astype(BF16)
    uq = w_uq.reshape(Q_LORA, nh, QK_NOPE + QK_ROPE)
    pe = uq[:, :, QK_NOPE:]
    uq = jnp.concatenate([uq[:, :, :QK_NOPE], pe[:, :, 0::2], pe[:, :, 1::2]], axis=2)
    wuqT = uq.reshape(Q_LORA, nh * (QK_NOPE + QK_ROPE)).T.astype(BF16)
    ukv = w_ukv.reshape(KV_LORA, nh, QK_NOPE + V_DIM)
    wuk = ukv[:, :, :QK_NOPE].reshape(KV_LORA, nh * QK_NOPE).astype(BF16)
    wuvT = ukv[:, :, QK_NOPE:].reshape(KV_LORA, nh * V_DIM).T.astype(BF16)
    return wd, g_q.reshape(1, -1), g_kv.reshape(1, -1), wuk, wuqT, wuvT


def kernel(x, c, ctx, c_ctx, ada_w, ada_b, norm_mix_g, norm_ffn_g, mla_w_down, mla_g_q, mla_w_uq, mla_g_kv, mla_w_ukv, mla_w_o, hy_w_in, hy_b_in, hy_conv_w, hy_conv_b, hy_f_w1, hy_f_b1, hy_f_freq1, hy_f_w2, hy_f_b2, hy_f_freq2, hy_f_w3, hy_decay, hy_bias, hy_w_out, hy_b_out, moe_w_r, moe_b_r, moe_w_in, moe_b_in, moe_w_out, moe_b_out, final_g):
    bsz, n_lat, d = x.shape
    n_ctx = ctx.shape[1]
    depth = ada_w.shape[0]
    assert bsz == 2 and d == MLA_HEADS * V_DIM and n_lat % 512 == 0 and n_ctx % 128 == 0
    assert depth == 2

    cond8 = jnp.zeros((8, d), F32).at[:bsz].set(c).at[bsz].set(c_ctx)
    mods = _ada(cond8, ada_w, ada_b)

    def mod(i, j, rows):
        return mods[i, rows, j * d:(j + 1) * d][:, None, :]

    lat_rows = slice(0, bsz)
    ctx_rows = slice(bsz, bsz + 1)
    xl = x
    for i in range(depth):
        kind, j = i % 2, i // 2
        sh1, sc1, g1 = (mod(i, m, lat_rows) for m in range(3))
        sh2, sc2, g2 = (mod(i, m, lat_rows) for m in range(3, 6))
        gm = norm_mix_g[i].reshape(1, d)
        if kind == 0:
            wts = _mla_weights(mla_w_down[j], mla_g_q[j], mla_w_uq[j], mla_g_kv[j], mla_w_ukv[j])
            cos, sin = _rope_tables(n_lat)
            zl = jnp.zeros((n_lat, LANES - QK_ROPE), F32)
            tabs = (jnp.concatenate([cos, cos, zl], axis=1), jnp.concatenate([-sin, sin, zl], axis=1), cos.T, sin.T)
            tq = tv = 512
            tk = 1024 if n_lat % 2048 == 0 else 512
            qT, k, vT = _mla_proj(xl, gm, sh1, sc1, wts, tabs, need_q=True, tm=tv, tk=tv)
            half = QK_ROPE // 2
            one_c = jnp.concatenate([jnp.ones((n_ctx, QK_ROPE), F32), jnp.zeros((n_ctx, LANES - QK_ROPE), F32)], axis=1)
            tabs_c = (one_c, jnp.zeros((n_ctx, LANES), F32), jnp.ones((half, n_ctx), F32), jnp.zeros((half, n_ctx), F32))
            kc, vTc = _mla_proj(ctx, gm, mod(i, 0, ctx_rows), mod(i, 1, ctx_rows), wts, tabs_c,
                                need_q=False, tm=n_ctx, tk=n_ctx)
            o = _attention(qT, k, vT, kc, vTc, tq=tq, tk=tk)
            wo = mla_w_o[j].astype(BF16)
            bo = jnp.zeros((1, d), F32)
            transposed = True
        else:
            proj3 = _hy_in(xl, gm, sh1, sc1, hy_w_in[j].astype(BF16), hy_b_in[j].reshape(1, -1), hy_conv_w[j],
                           hy_conv_b[j].reshape(1, -1), tm=512)
            fparams = (hy_f_w1[j], hy_f_b1[j], hy_f_freq1[j], hy_f_w2[j], hy_f_b2[j], hy_f_freq2[j], hy_f_w3[j],
                       hy_decay[j])
            o = _hyena_mix(proj3, fparams, hy_bias[j], n_lat=n_lat)
            wo = hy_w_out[j].astype(BF16)
            bo = hy_b_out[j].reshape(1, d)
            transposed = False
        wr = jnp.zeros((d, LANES), F32).at[:, :N_EXPERTS].set(moe_w_r[i])
        wrh = wr.astype(BF16)
        wrl = (wr - wrh.astype(F32)).astype(BF16)
        br = jnp.zeros((1, LANES), F32).at[0, :N_EXPERTS].set(moe_b_r[i])
        xl, fl, topi, gates, rank, cnt = _post(o, wo, bo, xl, g1, norm_ffn_g[i].reshape(1, d), sh2, sc2, wrh, wrl, br,
                                               transposed=transposed, tm=512)
        xl = _moe(fl, topi, gates, rank, cnt, xl, g2, final_g.reshape(1, d), moe_w_in[i], moe_b_in[i],
                  moe_w_out[i], moe_b_out[i], final=(i == depth - 1))
    return xl
```

```python
import functools
import math

import jax
import jax.numpy as jnp
from jax import lax
from jax.experimental import pallas as pl
from jax.experimental.pallas import tpu as pltpu

F32 = jnp.float32
BF16 = jnp.bfloat16

EPS = 1e-6
GRID_W = 64
MLA_HEADS = 8
QK_NOPE = 128
QK_ROPE = 64
V_DIM = 128
Q_LORA = 512
KV_LORA = 256
ROPE_THETA = 10000.0
MLA_SCALE = (QK_NOPE + QK_ROPE) ** -0.5
QK_PAD = 256

HY_EMB = 33
HY_BANDS = (HY_EMB - 1) // 2
HY_HID = 64
FFT_N2 = 128
B_GROUP = 8

N_EXPERTS = 32
TOP_K = 4
SWIGLU_LIMIT = 7.0
SWIGLU_ALPHA = 1.702
MOE_TM = 256
LANES = 128

VMEM_LIMIT = 56 * 1024 * 1024


def _cp(sem, vmem=VMEM_LIMIT):
    return pltpu.CompilerParams(dimension_semantics=sem, vmem_limit_bytes=vmem)


def _dot(a, b):
    return jnp.dot(a, b, preferred_element_type=F32)


def _dot_hi(a, b):
    return jnp.dot(a, b, preferred_element_type=F32, precision=lax.Precision.HIGHEST)


def _dot3(a, b):
    ah = a.astype(BF16)
    al = (a - ah.astype(F32)).astype(BF16)
    bh = b.astype(BF16)
    bl = (b - bh.astype(F32)).astype(BF16)
    return _dot(ah, bh) + (_dot(al, bh) + _dot(ah, bl))


def _rms(x, g):
    return x * lax.rsqrt(jnp.mean(x * x, axis=-1, keepdims=True) + EPS) * g


def _ada_body(c_ref, w_ref, b_ref, o_ref):
    c = c_ref[...]
    s = c * jax.nn.sigmoid(c)
    o_ref[0] = _dot(s.astype(BF16), w_ref[0].astype(BF16)) + b_ref[0]


def _ada(cond8, ada_w, ada_b):
    depth, d, n = ada_w.shape
    tn = n // 4
    return pl.pallas_call(
        _ada_body,
        grid=(depth, n // tn),
        in_specs=[
            pl.BlockSpec((8, d), lambda i, j: (0, 0)),
            pl.BlockSpec((1, d, tn), lambda i, j: (i, 0, j)),
            pl.BlockSpec((1, 1, tn), lambda i, j: (i, 0, j)),
        ],
        out_specs=pl.BlockSpec((1, 8, tn), lambda i, j: (i, 0, j)),
        out_shape=jax.ShapeDtypeStruct((depth, 8, n), F32),
        compiler_params=_cp(("arbitrary", "arbitrary")),
        name="ada_mod",
    )(cond8, ada_w, ada_b.reshape(depth, 1, n))


def _mla_proj_body(x_ref, g_ref, sh_ref, sc_ref, wd_ref, gq_ref, gkv_ref, wuk_ref, wuqT_ref, wuvT_ref,
                   ct_ref, st_ref, cT_ref, sT_ref, *out_refs, need_q, tk):
    if need_q:
        qT_ref, k_ref, vT_ref = out_refs
    else:
        k_ref, vT_ref = out_refs
    nh = MLA_HEADS
    x = x_ref[0]
    h = _rms(x, g_ref[...]) * (1.0 + sc_ref[0]) + sh_ref[0]
    lat = _dot(h.astype(BF16), wd_ref[...])
    o_kv = Q_LORA
    o_a = Q_LORA + KV_LORA
    kvn = _rms(lat[:, o_kv:o_a], gkv_ref[...])
    kr = (lat[:, o_a:o_a + LANES] * ct_ref[...] + lat[:, o_a + LANES:o_a + 2 * LANES] * st_ref[...]).astype(BF16)
    knope = _dot(kvn.astype(BF16), wuk_ref[...])
    for hh in range(nh):
        k_ref[0, hh, :, 0:QK_NOPE] = knope[:, hh * QK_NOPE:(hh + 1) * QK_NOPE].astype(BF16)
        k_ref[0, hh, :, QK_NOPE:QK_PAD] = kr
    vT = _dot(wuvT_ref[...], kvn.T.astype(BF16))
    tm = x.shape[0]
    for hh in range(nh):
        for c in range(tm // tk):
            vT_ref[0, hh, c] = vT[hh * V_DIM:(hh + 1) * V_DIM, c * tk:(c + 1) * tk].astype(BF16)
    if need_q:
        qn = _rms(lat[:, :Q_LORA], gq_ref[...])
        qT = _dot(wuqT_ref[...], qn.T.astype(BF16)) * (MLA_SCALE * math.log2(math.e))
        c = cT_ref[...]
        s = sT_ref[...]
        hw = QK_NOPE + QK_ROPE
        half = QK_ROPE // 2
        for hh in range(nh):
            base = hh * hw
            x1 = qT[base + QK_NOPE:base + QK_NOPE + half]
            x2 = qT[base + QK_NOPE + half:base + hw]
            qT_ref[0, hh, 0:QK_NOPE] = qT[base:base + QK_NOPE].astype(BF16)
            qT_ref[0, hh, QK_NOPE:QK_NOPE + half] = (x1 * c - x2 * s).astype(BF16)
            qT_ref[0, hh, QK_NOPE + half:hw] = (x1 * s + x2 * c).astype(BF16)
            qT_ref[0, hh, hw:QK_PAD] = jnp.zeros((QK_PAD - hw, tm), BF16)


def _mla_proj(x, g, sh, sc, wts, tabs, *, need_q, tm, tk):
    bsz, n, d = x.shape
    nh = MLA_HEADS
    wd, gq, gkv, wuk, wuqT, wuvT = wts
    ct, st, cT, sT = tabs
    nsh = sh.shape[0]
    full = lambda a: pl.BlockSpec(a.shape, lambda b, i: (0,) * a.ndim)
    in_specs = [
        pl.BlockSpec((1, tm, d), lambda b, i: (b, i, 0)),
        full(g),
        pl.BlockSpec((1, 1, d), lambda b, i: (b % nsh, 0, 0)),
        pl.BlockSpec((1, 1, d), lambda b, i: (b % nsh, 0, 0)),
        full(wd), full(gq), full(gkv), full(wuk), full(wuqT), full(wuvT),
        pl.BlockSpec((tm, LANES), lambda b, i: (i, 0)),
        pl.BlockSpec((tm, LANES), lambda b, i: (i, 0)),
        pl.BlockSpec((QK_ROPE // 2, tm), lambda b, i: (0, i)),
        pl.BlockSpec((QK_ROPE // 2, tm), lambda b, i: (0, i)),
    ]
    out_specs = [
        pl.BlockSpec((1, nh, tm, QK_PAD), lambda b, i: (b, 0, i, 0)),
        pl.BlockSpec((1, nh, tm // tk, V_DIM, tk), lambda b, i: (b, 0, i, 0, 0)),
    ]
    out_shape = [
        jax.ShapeDtypeStruct((bsz, nh, n, QK_PAD), BF16),
        jax.ShapeDtypeStruct((bsz, nh, n // tk, V_DIM, tk), BF16),
    ]
    if need_q:
        out_specs = [pl.BlockSpec((1, nh, QK_PAD, tm), lambda b, i: (b, 0, 0, i))] + out_specs
        out_shape = [jax.ShapeDtypeStruct((bsz, nh, QK_PAD, n), BF16)] + out_shape
    return pl.pallas_call(
        functools.partial(_mla_proj_body, need_q=need_q, tk=tk),
        grid=(bsz, n // tm),
        in_specs=in_specs,
        out_specs=out_specs,
        out_shape=out_shape,
        compiler_params=_cp(("arbitrary", "arbitrary")),
        name="mla_proj_q" if need_q else "mla_proj_ctx",
    )(x, g, sh, sc, wd, gq, gkv, wuk, wuqT, wuvT, ct, st, cT, sT)


ACC_ROWS = V_DIM + 16
SM_STRIP = 64


def _attn_body(qT_ref, k_ref, vT_ref, kc_ref, vTc_ref, o_ref, s0, s1, p0, p1, sc, pc, acc, m_scr, x0, x1, xc,
               a0, a1, ac, *, tk):
    nchunk = k_ref.shape[2] // tk

    def scores(kblk, s_ref, mx_ref):
        r = _dot(kblk, qT_ref[0, 0])
        s_ref[...] = r
        mx_ref[...] = jnp.max(r, axis=0, keepdims=True)

    def probs(s_ref, mx_ref, p_ref, a_ref):
        m_old = m_scr[...]
        m_new = jnp.maximum(m_old, mx_ref[...])
        m_scr[...] = m_new
        a_ref[...] = jnp.exp2(m_old - m_new)
        for r in range(0, s_ref.shape[0], SM_STRIP):
            p_ref[r:r + SM_STRIP] = jnp.exp2(s_ref[r:r + SM_STRIP] - m_new).astype(BF16)

    def accumulate(p_ref, a_ref, vblk):
        lhs = jnp.concatenate([vblk, jnp.ones((ACC_ROWS - V_DIM, vblk.shape[1]), BF16)], axis=0)
        acc[...] = a_ref[...] * acc[...] + _dot(lhs, p_ref[...])

    def kchunk(i):
        i = jnp.minimum(i, nchunk - 1)
        return k_ref[0, 0, pl.ds(pl.multiple_of(i * tk, tk), tk), :]

    def vchunk(i):
        nsub = tk // vT_ref.shape[-1]
        return jnp.concatenate([vT_ref[0, 0, i * nsub + u] for u in range(nsub)], axis=1)

    m_scr[...] = jnp.full(m_scr.shape, -jnp.inf, F32)
    acc[...] = jnp.zeros(acc.shape, F32)
    scores(kc_ref[0, 0], sc, xc)
    scores(kchunk(0), s0, x0)
    probs(sc, xc, pc, ac)
    accumulate(pc, ac, vTc_ref[0, 0, 0])

    def body(j, carry):
        t = 2 * j
        scores(kchunk(t + 1), s1, x1)
        probs(s0, x0, p0, a0)
        accumulate(p0, a0, vchunk(t))
        scores(kchunk(t + 2), s0, x0)
        probs(s1, x1, p1, a1)
        accumulate(p1, a1, vchunk(t + 1))
        return carry

    lax.fori_loop(0, nchunk // 2, body, 0)
    o_ref[0, 0] = (acc[0:V_DIM] / acc[V_DIM:V_DIM + 1]).astype(BF16)


def _attention(qT, k, vT, kc, vTc, *, tq, tk):
    bsz, nh, _, n = qT.shape
    nc = kc.shape[2]
    tv = vT.shape[-1]
    assert (n // tk) % 2 == 0 and tk % tv == 0
    return pl.pallas_call(
        functools.partial(_attn_body, tk=tk),
        grid=(bsz, nh, n // tq),
        in_specs=[
            pl.BlockSpec((1, 1, QK_PAD, tq), lambda b, h, i: (b, h, 0, i)),
            pl.BlockSpec((1, 1, n, QK_PAD), lambda b, h, i: (b, h, 0, 0)),
            pl.BlockSpec((1, 1, n // tv, V_DIM, tv), lambda b, h, i: (b, h, 0, 0, 0)),
            pl.BlockSpec((1, 1, nc, QK_PAD), lambda b, h, i: (b, h, 0, 0)),
            pl.BlockSpec((1, 1, 1, V_DIM, nc), lambda b, h, i: (b, h, 0, 0, 0)),
        ],
        out_specs=pl.BlockSpec((1, 1, V_DIM, tq), lambda b, h, i: (b, h, 0, i)),
        out_shape=jax.ShapeDtypeStruct((bsz, nh, V_DIM, n), BF16),
        scratch_shapes=[pltpu.VMEM((tk, tq), F32), pltpu.VMEM((tk, tq), F32),
                        pltpu.VMEM((tk, tq), BF16), pltpu.VMEM((tk, tq), BF16),
                        pltpu.VMEM((nc, tq), F32), pltpu.VMEM((nc, tq), BF16),
                        pltpu.VMEM((ACC_ROWS, tq), F32), pltpu.VMEM((1, tq), F32),
                        pltpu.VMEM((1, tq), F32), pltpu.VMEM((1, tq), F32), pltpu.VMEM((1, tq), F32),
                        pltpu.VMEM((1, tq), F32), pltpu.VMEM((1, tq), F32), pltpu.VMEM((1, tq), F32)],
        compiler_params=_cp(("arbitrary", "arbitrary", "arbitrary")),
        name="mla_attention",
    )(qT, k, vT, kc, vTc)


def _post_body(o_ref, wo_ref, bo_ref, x_ref, g1_ref, gf_ref, sh_ref, sc_ref, wrh_ref, wrl_ref, br_ref, tri_ref,
               xl_ref, fl_ref, ti_ref, gt_ref, rk_ref, cnt_ref, *, transposed):
    @pl.when((pl.program_id(0) == 0) & (pl.program_id(1) == 0))
    def _():
        cnt_ref[...] = jnp.zeros_like(cnt_ref)

    tm = x_ref.shape[1]
    if transposed:
        oT = o_ref[0].astype(F32).reshape(MLA_HEADS * V_DIM, tm)
        o = oT.T.astype(BF16)
    else:
        o = o_ref[0].astype(BF16)
    y = _dot(o, wo_ref[...]) + bo_ref[...]
    xl = x_ref[0] + g1_ref[0] * y
    xl_ref[0] = xl
    fl = _rms(xl, gf_ref[...]) * (1.0 + sc_ref[0]) + sh_ref[0]
    _to_rows(fl_ref, fl)
    flh = fl.astype(BF16)
    fll = (fl - flh.astype(F32)).astype(BF16)
    logits = _dot(flh, wrh_ref[...]) + (_dot(fll, wrh_ref[...]) + _dot(flh, wrl_ref[...])) + br_ref[...]
    lane = lax.broadcasted_iota(jnp.int32, (tm, LANES), 1).astype(F32)
    neg = jnp.float32(-jnp.inf)
    work = jnp.where(lane < N_EXPERTS, logits, neg)
    vals, idxs = [], []
    onehot = jnp.zeros((tm, LANES), F32)
    for _ in range(TOP_K):
        mk = jnp.max(work, axis=-1, keepdims=True)
        ik = jnp.min(jnp.where(work == mk, lane, float(LANES)), axis=-1, keepdims=True)
        sel = lane == ik
        onehot = jnp.where(sel, 1.0, onehot)
        work = jnp.where(sel, neg, work)
        vals.append(mk)
        idxs.append(ik)
    es = [jnp.exp(v - vals[0]) for v in vals]
    den = es[0] + es[1] + es[2] + es[3]
    pre = _dot(tri_ref[...], onehot.astype(BF16)) + cnt_ref[...]
    ti = jnp.zeros((tm, LANES), F32)
    gt = jnp.zeros((tm, LANES), F32)
    rk = jnp.zeros((tm, LANES), F32)
    for kk in range(TOP_K):
        rank = jnp.sum(jnp.where(lane == idxs[kk], pre, 0.0), axis=-1, keepdims=True)
        ti = jnp.where(lane == kk, idxs[kk], ti)
        gt = jnp.where(lane == kk, es[kk] / den, gt)
        rk = jnp.where(lane == kk, rank, rk)
    ti_ref[...] = ti[:, :TOP_K].astype(jnp.int32)
    gt_ref[...] = gt[:, :TOP_K]
    rk_ref[...] = rk[:, :TOP_K].astype(jnp.int32)
    cnt_ref[...] += jnp.sum(onehot, axis=0, keepdims=True)


def _post(o, wo, bo, x, g1, gf, sh, sc, wrh, wrl, br, *, transposed, tm):
    bsz, n, d = x.shape
    t = bsz * n
    nt = n // tm
    tri = (lax.broadcasted_iota(jnp.int32, (tm, tm), 0) > lax.broadcasted_iota(jnp.int32, (tm, tm), 1)).astype(BF16)
    full = lambda a: pl.BlockSpec(a.shape, lambda b, i: (0,) * a.ndim)
    per_b = pl.BlockSpec((1, 1, d), lambda b, i: (b, 0, 0))
    if transposed:
        o_spec = pl.BlockSpec((1, MLA_HEADS, V_DIM, tm), lambda b, i: (b, 0, 0, i))
    else:
        o_spec = pl.BlockSpec((1, tm, d), lambda b, i: (b, i, 0))
    tok = lambda w: pl.BlockSpec((tm, w), lambda b, i: (b * nt + i, 0))
    return pl.pallas_call(
        functools.partial(_post_body, transposed=transposed),
        grid=(bsz, nt),
        in_specs=[o_spec, full(wo), full(bo), pl.BlockSpec((1, tm, d), lambda b, i: (b, i, 0)), per_b, full(gf),
                  per_b, per_b, full(wrh), full(wrl), full(br), full(tri)],
        out_specs=[pl.BlockSpec((1, tm, d), lambda b, i: (b, i, 0)),
                   pl.BlockSpec((tm * ROW_SUB, LANES), lambda b, i: (b * nt + i, 0)),
                   tok(TOP_K), tok(TOP_K), tok(TOP_K), pl.BlockSpec((1, LANES), lambda b, i: (0, 0))],
        out_shape=[jax.ShapeDtypeStruct((bsz, n, d), F32), jax.ShapeDtypeStruct((t * ROW_SUB, LANES), F32),
                   jax.ShapeDtypeStruct((t, TOP_K), jnp.int32), jax.ShapeDtypeStruct((t, TOP_K), F32),
                   jax.ShapeDtypeStruct((t, TOP_K), jnp.int32), jax.ShapeDtypeStruct((1, LANES), F32)],
        compiler_params=_cp(("arbitrary", "arbitrary")),
        name="post_attn" if transposed else "post_hyena",
    )(o, wo, bo, x, g1, gf, sh, sc, wrh, wrl, br, tri)


ROW_SUB = 8


def _row_slice(i):
    return pl.ds(pl.multiple_of(i * ROW_SUB, ROW_SUB), ROW_SUB)


def _to_rows(ref, x):
    for s in range(ROW_SUB):
        ref[pl.ds(s, x.shape[0], stride=ROW_SUB), :] = x[:, s * LANES:(s + 1) * LANES]


def _from_rows(ref, lo, hi):
    return jnp.concatenate([ref[pl.ds(lo * ROW_SUB + s, hi - lo, stride=ROW_SUB), :] for s in range(ROW_SUB)], axis=1)


def _dispatch_body(pe_ref, pd_ref, dest_ref, fl_ref, xs_out, zbuf, sem, *, td):
    @pl.when(pl.program_id(0) == 0)
    def _():
        zbuf[...] = jnp.zeros(zbuf.shape, zbuf.dtype)
        for e in range(N_EXPERTS):
            @pl.when(pd_ref[e] > 0)
            def _():
                start = pl.multiple_of((pe_ref[e] - MOE_TM) * ROW_SUB, ROW_SUB)
                cp = pltpu.make_async_copy(zbuf, xs_out.at[pl.ds(start, MOE_TM * ROW_SUB)], sem)
                cp.start()
                cp.wait()

    def issue(t, carry):
        for kk in range(TOP_K):
            d = dest_ref[0, 0, t * TOP_K + kk]
            pltpu.make_async_copy(fl_ref.at[_row_slice(t)], xs_out.at[_row_slice(d)], sem).start()
        return carry

    lax.fori_loop(0, td, issue, 0, unroll=2)

    def drain(t, carry):
        pltpu.make_async_copy(fl_ref.at[_row_slice(0)], xs_out.at[_row_slice(0)], sem).wait()
        return carry

    lax.fori_loop(0, td * TOP_K, drain, 0, unroll=8)


def _dispatch(pad_end, padded, dest, fl, n_rows, *, td):
    t = fl.shape[0] // ROW_SUB
    dest3 = dest.reshape(t // td, 1, td * TOP_K)
    grid_spec = pltpu.PrefetchScalarGridSpec(
        num_scalar_prefetch=2,
        grid=(t // td,),
        in_specs=[
            pl.BlockSpec((1, 1, td * TOP_K), lambda i, pe, pd: (i, 0, 0), memory_space=pltpu.SMEM),
            pl.BlockSpec((td * ROW_SUB, LANES), lambda i, pe, pd: (i, 0)),
        ],
        out_specs=pl.BlockSpec(memory_space=pl.ANY),
        scratch_shapes=[pltpu.VMEM((MOE_TM * ROW_SUB, LANES), fl.dtype), pltpu.SemaphoreType.DMA(())],
    )
    return pl.pallas_call(
        functools.partial(_dispatch_body, td=td),
        grid_spec=grid_spec,
        out_shape=jax.ShapeDtypeStruct((n_rows * ROW_SUB, LANES), fl.dtype),
        compiler_params=_cp(("arbitrary",)),
        name="moe_dispatch",
    )(pad_end, padded, dest3, fl)


def _expert_body(be_ref, nu_ref, xs_ref, win_ref, bin_ref, wout_ref, bout_ref, ys_ref, win_s, wout_s):
    b = pl.program_id(0)
    dff = wout_ref.shape[1]

    @pl.when(b < nu_ref[0])
    def _():
        prev = be_ref[jnp.maximum(b - 1, 0)]

        @pl.when((b == 0) | (prev != be_ref[b]))
        def _():
            win_s[...] = win_ref[0].astype(BF16)
            wout_s[...] = wout_ref[0].astype(BF16)

        x = _from_rows(xs_ref, 0, xs_ref.shape[0] // ROW_SUB).astype(BF16)
        gu = _dot(x, win_s[...]) + bin_ref[0]
        gate = jnp.minimum(gu[:, :dff], SWIGLU_LIMIT)
        lin = jnp.clip(gu[:, dff:], -SWIGLU_LIMIT, SWIGLU_LIMIT)
        act = gate * jax.nn.sigmoid(SWIGLU_ALPHA * gate) * (lin + 1.0)
        _to_rows(ys_ref, _dot(act.astype(BF16), wout_s[...]) + bout_ref[0])

    @pl.when(b >= nu_ref[0])
    def _():
        ys_ref[...] = jnp.zeros_like(ys_ref)


def _experts(blk_exp, n_used, xs, layer, w_in, b_in, w_out, b_out):
    n_rows = xs.shape[0] // ROW_SUB
    depth, ne, d, f2 = w_in.shape
    dff = w_out.shape[2]
    tm = MOE_TM
    grid_spec = pltpu.PrefetchScalarGridSpec(
        num_scalar_prefetch=2,
        grid=(n_rows // tm,),
        in_specs=[
            pl.BlockSpec((tm * ROW_SUB, LANES), lambda b, be, nu: (jnp.minimum(b, nu[0] - 1), 0)),
            pl.BlockSpec((None, 1, d, f2), lambda b, be, nu: (layer, be[b], 0, 0)),
            pl.BlockSpec((None, 1, 1, f2), lambda b, be, nu: (layer, be[b], 0, 0)),
            pl.BlockSpec((None, 1, dff, d), lambda b, be, nu: (layer, be[b], 0, 0)),
            pl.BlockSpec((None, 1, 1, d), lambda b, be, nu: (layer, be[b], 0, 0)),
        ],
        out_specs=pl.BlockSpec((tm * ROW_SUB, LANES), lambda b, be, nu: (b, 0)),
        scratch_shapes=[pltpu.VMEM((d, f2), BF16), pltpu.VMEM((dff, d), BF16)],
    )
    return pl.pallas_call(
        _expert_body,
        grid_spec=grid_spec,
        out_shape=jax.ShapeDtypeStruct(xs.shape, F32),
        compiler_params=_cp(("arbitrary",)),
        name="moe_experts",
    )(blk_exp, n_used, xs, w_in, b_in.reshape(depth, ne, 1, f2), w_out, b_out.reshape(depth, ne, 1, d))


def _combine_body(dest_ref, ys_hbm, gt_ref, xl_ref, g2_ref, fg_ref, out_ref, buf, sem, *, tc, final):
    def issue(t, carry):
        for kk in range(TOP_K):
            d = dest_ref[0, 0, t * TOP_K + kk]
            pltpu.make_async_copy(ys_hbm.at[_row_slice(d)], buf.at[_row_slice(kk * tc + t)], sem).start()
        return carry

    lax.fori_loop(0, tc, issue, 0, unroll=2)

    def drain(t, carry):
        pltpu.make_async_copy(ys_hbm.at[_row_slice(0)], buf.at[_row_slice(0)], sem).wait()
        return carry

    lax.fori_loop(0, tc * TOP_K, drain, 0, unroll=8)
    gt = gt_ref[...]
    y = gt[:, 0:1] * _from_rows(buf, 0, tc)
    for kk in range(1, TOP_K):
        y = y + gt[:, kk:kk + 1] * _from_rows(buf, kk * tc, (kk + 1) * tc)
    xl = xl_ref[0] + g2_ref[0] * y
    out_ref[0] = _rms(xl, fg_ref[...]) if final else xl


def _combine(dest, ys, gates, xl, g2, fg, *, tc, final):
    bsz, n, d = xl.shape
    t = bsz * n
    nt = n // tc
    dest3 = dest.reshape(t // tc, 1, tc * TOP_K)
    return pl.pallas_call(
        functools.partial(_combine_body, tc=tc, final=final),
        grid=(bsz, nt),
        in_specs=[
            pl.BlockSpec((1, 1, tc * TOP_K), lambda b, i: (b * nt + i, 0, 0), memory_space=pltpu.SMEM),
            pl.BlockSpec(memory_space=pl.ANY),
            pl.BlockSpec((tc, TOP_K), lambda b, i: (b * nt + i, 0)),
            pl.BlockSpec((1, tc, d), lambda b, i: (b, i, 0)),
            pl.BlockSpec((1, 1, d), lambda b, i: (b, 0, 0)),
            pl.BlockSpec((1, d), lambda b, i: (0, 0)),
        ],
        out_specs=pl.BlockSpec((1, tc, d), lambda b, i: (b, i, 0)),
        out_shape=jax.ShapeDtypeStruct((bsz, n, d), F32),
        scratch_shapes=[pltpu.VMEM((TOP_K * tc * ROW_SUB, LANES), F32), pltpu.SemaphoreType.DMA(())],
        compiler_params=_cp(("arbitrary", "arbitrary")),
        name="moe_combine",
    )(dest3, ys, gates, xl, g2, fg)


def _moe(fl, topi, gates, rank, cnt, xl, g2, fg, layer, w_in, b_in, w_out, b_out, *, final):
    t = fl.shape[0] // ROW_SUB
    tm = MOE_TM
    counts = cnt[0, :N_EXPERTS].astype(jnp.int32)
    padded = (counts + tm - 1) // tm * tm
    pad_end = jnp.cumsum(padded)
    pad_start = pad_end - padded
    dest = jnp.take(pad_start, topi) + rank
    nb = t * TOP_K // tm + N_EXPERTS
    blk_start = jnp.arange(nb, dtype=jnp.int32) * tm
    blk_exp = jnp.minimum(jnp.sum((pad_end[None, :] <= blk_start[:, None]).astype(jnp.int32), axis=1), N_EXPERTS - 1)
    n_used = (pad_end[-1:] // tm).astype(jnp.int32)
    xs = _dispatch(pad_end, padded, dest, fl, nb * tm, td=256)
    ys = _experts(blk_exp, n_used, xs, layer, w_in, b_in, w_out, b_out)
    return _combine(dest, ys, gates, xl, g2, fg, tc=256, final=final)


def _hy_in_body(x_ref, xp_ref, xn_ref, g_ref, sh_ref, sc_ref, w_ref, b_ref, cw_ref, cb_ref, o_ref, *, nt):
    i = pl.program_id(2)
    w = w_ref[...]

    def proj(xx):
        h = _rms(xx, g_ref[...]) * (1.0 + sc_ref[0]) + sh_ref[0]
        return _dot(h.astype(BF16), w) + b_ref[...]

    p = proj(x_ref[0])
    tm = p.shape[0]
    ph = proj(jnp.concatenate([xp_ref[0], xn_ref[0]], axis=0))
    prev = jnp.where(i > 0, ph[7:8], 0.0)
    nxt = jnp.where(i < nt - 1, ph[8:9], 0.0)
    row = lax.broadcasted_iota(jnp.int32, (tm, 1), 0)
    up = jnp.where(row == 0, prev, pltpu.roll(p, 1, axis=0))
    dn = jnp.where(row == tm - 1, nxt, pltpu.roll(p, tm - 1, axis=0))
    cw = cw_ref[...]
    o_ref[0, 0] = up * cw[0:1] + p * cw[1:2] + dn * cw[2:3] + cb_ref[...]


def _hy_in(x, g, sh, sc, w, b, cw, cb, *, tm):
    bsz, n, d = x.shape
    nt = n // tm
    hb = tm // 8
    per_b = pl.BlockSpec((1, 1, d), lambda j, bb, i: (bb, 0, 0))
    return pl.pallas_call(
        functools.partial(_hy_in_body, nt=nt),
        grid=(3, bsz, nt),
        in_specs=[
            pl.BlockSpec((1, tm, d), lambda j, bb, i: (bb, i, 0)),
            pl.BlockSpec((1, 8, d), lambda j, bb, i: (bb, jnp.maximum(i * hb - 1, 0), 0)),
            pl.BlockSpec((1, 8, d), lambda j, bb, i: (bb, jnp.minimum((i + 1) * hb, n // 8 - 1), 0)),
            pl.BlockSpec((1, d), lambda j, bb, i: (0, 0)),
            per_b, per_b,
            pl.BlockSpec((d, d), lambda j, bb, i: (0, j)),
            pl.BlockSpec((1, d), lambda j, bb, i: (0, j)),
            pl.BlockSpec((3, d), lambda j, bb, i: (0, j)),
            pl.BlockSpec((1, d), lambda j, bb, i: (0, j)),
        ],
        out_specs=pl.BlockSpec((1, 1, tm, d), lambda j, bb, i: (j, bb, i, 0)),
        out_shape=jax.ShapeDtypeStruct((3, bsz, n, d), F32),
        compiler_params=_cp(("arbitrary", "arbitrary", "arbitrary")),
        name="hyena_in_proj",
    )(x, x, x, g, sh, sc, w, b, cw, cb)


def _filt_feat_body(w1_ref, b1_ref, f1_ref, w2_ref, b2_ref, f2_ref, o_ref, *, n_lat):
    na = o_ref.shape[1]
    a = lax.broadcasted_iota(jnp.int32, (na, 1), 0)
    lane = lax.broadcasted_iota(jnp.int32, (na, LANES), 1)
    band_idx = jnp.where(lane <= HY_BANDS, lane - 1, lane - 1 - HY_BANDS).astype(F32)
    band = 1e-4 + band_idx * ((HY_BANDS - 1 - 1e-4) / (HY_BANDS - 1))
    for j in range(B_GROUP):
        r = a * FFT_N2 + (pl.program_id(0) * B_GROUP + j)
        pos = jnp.where(r < n_lat, r, 2 * n_lat - r).astype(F32)
        tn = pos / float(max(n_lat - 1, 1))
        ang = ((2.0 * math.pi / n_lat) * pos) * band
        z = jnp.where(lane == 0, tn, jnp.where(lane <= HY_BANDS, jnp.cos(ang),
                                               jnp.where(lane < HY_EMB, -jnp.sin(ang), 0.0)))
        h1 = jnp.sin(f1_ref[...] * (_dot_hi(z, w1_ref[...]) + b1_ref[...]))
        h2 = jnp.sin(f2_ref[...] * (_dot_hi(h1, w2_ref[...]) + b2_ref[...]))
        valid = (r != n_lat).astype(F32)
        o_ref[j] = jnp.where(lane == HY_HID, tn, jnp.where(lane == HY_HID + 1, valid, h2))


def _filt_feat(w1, b1, f1, w2, b2, f2, *, n_lat):
    na = 2 * n_lat // FFT_N2
    w1p = jnp.zeros((LANES, LANES), F32).at[:HY_EMB, :HY_HID].set(w1)
    w2p = jnp.zeros((LANES, LANES), F32).at[:HY_HID, :HY_HID].set(w2)
    padv = lambda v: jnp.zeros((1, LANES), F32).at[0, :HY_HID].set(v)
    full = lambda shp: pl.BlockSpec(shp, lambda i: (0,) * len(shp))
    return pl.pallas_call(
        functools.partial(_filt_feat_body, n_lat=n_lat),
        grid=(FFT_N2 // B_GROUP,),
        in_specs=[full((LANES, LANES)), full((1, LANES)), full((1, LANES)),
                  full((LANES, LANES)), full((1, LANES)), full((1, LANES))],
        out_specs=pl.BlockSpec((B_GROUP, na, LANES), lambda i: (i, 0, 0)),
        out_shape=jax.ShapeDtypeStruct((FFT_N2, na, LANES), F32),
        compiler_params=_cp(("arbitrary",)),
        name="hyena_filter_features",
    )(w1p, padv(b1), padv(f1), w2p, padv(b2), padv(f2))


def _filt_s1_body(hd_ref, w3_ref, dec_ref, tab_ref, o_ref):
    na = hd_ref.shape[1]
    ha = na // 2
    o2 = _rows2d(o_ref)
    for j in range(B_GROUP):
        f = hd_ref[j]
        tn = f[:, HY_HID:HY_HID + 1]
        valid = f[:, HY_HID + 1:HY_HID + 2]
        top = _dot3(f[:ha], w3_ref[0, 0]) * jnp.exp(-tn[:ha] * jnp.abs(dec_ref[0, 0]))
        bot = _dot3(f[ha:], w3_ref[0, 1]) * jnp.exp(-tn[ha:] * jnp.abs(dec_ref[0, 1])) * valid[ha:]
        hb = jnp.concatenate([top, bot], axis=0).astype(BF16)
        o2[pl.ds(j, 2 * na, stride=B_GROUP), :] = _dot(tab_ref[j], hb)


def _filt_s1(hd, w3r, dec, tab, *, ct):
    _, na, _ = hd.shape
    d = w3r.shape[-1]
    return pl.pallas_call(
        _filt_s1_body,
        grid=(2, FFT_N2 // B_GROUP, d // ct),
        in_specs=[
            pl.BlockSpec((B_GROUP, na, LANES), lambda o, g, c: (g, 0, 0)),
            pl.BlockSpec((1, 2, LANES, ct), lambda o, g, c: (o, 0, 0, c)),
            pl.BlockSpec((1, 2, 1, ct), lambda o, g, c: (o, 0, 0, c)),
            pl.BlockSpec((B_GROUP, 2 * na, na), lambda o, g, c: (g, 0, 0)),
        ],
        out_specs=pl.BlockSpec((None, 2 * na, B_GROUP, ct), lambda o, g, c: (o, 0, g, c)),
        out_shape=jax.ShapeDtypeStruct((2, 2 * na, FFT_N2, d), F32),
        compiler_params=_cp(("arbitrary", "arbitrary", "arbitrary")),
        name="hyena_filter_dft1",
    )(hd, w3r, dec, tab)


S2_KB = 4


def _s2_body(*refs, conv):
    if conv:
        o_ref, kf_ref, ff_ref, fi_ref, g_ref = refs
    else:
        o_ref, ff_ref, g_ref = refs
    for u in range(S2_KB):
        xin = jnp.concatenate([o_ref[0, u], o_ref[1, u]], axis=0).astype(BF16)
        xf = _dot(ff_ref[...], xin)
        if conv:
            xr, xi = xf[:FFT_N2], xf[FFT_N2:]
            kr = kf_ref[0, u].astype(F32)
            ki = kf_ref[1, u].astype(F32)
            y = jnp.concatenate([xr * kr - xi * ki, xr * ki + xi * kr], axis=0).astype(BF16)
            xf = _dot(fi_ref[...], y)
        g_ref[0, u] = xf[:FFT_N2].astype(g_ref.dtype)
        g_ref[1, u] = xf[FFT_N2:].astype(g_ref.dtype)


def _s2(o4, kf, order, ff, fi, *, ct, conv):
    n1 = o4.shape[-3]
    d = o4.shape[-1]
    full = lambda a: pl.BlockSpec(a.shape, lambda k, c: (0,) * a.ndim)
    nk = n1 // S2_KB
    if conv:
        blk = pl.BlockSpec((2, S2_KB, FFT_N2, ct), lambda k, c: (0, k, 0, c))
        in_specs = [blk, pl.BlockSpec((None, 2, S2_KB, FFT_N2, ct), lambda k, c: (order, 0, k, 0, c)), full(ff),
                    full(fi)]
        args = (o4, kf, ff, fi)
        grid = (nk, d // ct)
        out_specs = blk
        out_dtype = F32
    else:
        no = o4.shape[0]
        blk = pl.BlockSpec((None, 2, S2_KB, FFT_N2, ct), lambda k, c: (k // nk, 0, k % nk, 0, c))
        in_specs = [blk, full(ff)]
        args = (o4, ff)
        grid = (no * nk, d // ct)
        out_specs = blk
        out_dtype = BF16
    return pl.pallas_call(
        functools.partial(_s2_body, conv=conv),
        grid=grid,
        in_specs=in_specs,
        out_specs=out_specs,
        out_shape=jax.ShapeDtypeStruct(o4.shape, out_dtype),
        compiler_params=_cp(("arbitrary", "arbitrary")),
        name="hyena_conv_dft2" if conv else "hyena_filter_dft2",
    )(*args)


def _rows2d(ref):
    lead = ref.shape[:-3]
    return ref.reshape(lead + (ref.shape[-3] * B_GROUP, ref.shape[-1]))


def _s1_body(z_ref, tab_ref, o_ref):
    rows, n_out = z_ref.shape[0], o_ref.shape[0]
    z2, o2 = _rows2d(z_ref), _rows2d(o_ref)
    for j in range(B_GROUP):
        zj = z2[pl.ds(j, rows, stride=B_GROUP), :]
        o2[pl.ds(j, n_out, stride=B_GROUP), :] = _dot(tab_ref[j], zj.astype(BF16))


def _s1(z4, zi, tab, *, ct):
    _, rows, _, d = z4.shape
    n_out = tab.shape[1]
    return pl.pallas_call(
        _s1_body,
        grid=(FFT_N2 // B_GROUP, d // ct),
        in_specs=[
            pl.BlockSpec((None, rows, B_GROUP, ct), lambda g, c: (zi, 0, g, c)),
            pl.BlockSpec((B_GROUP, n_out, rows), lambda g, c: (g, 0, 0)),
        ],
        out_specs=pl.BlockSpec((n_out, B_GROUP, ct), lambda g, c: (0, g, c)),
        out_shape=jax.ShapeDtypeStruct((n_out, FFT_N2, d), F32),
        compiler_params=_cp(("arbitrary", "arbitrary")),
        name="hyena_conv_dft1",
    )(z4, tab)


def _s3_body(g_ref, tab_ref, gate_ref, z_ref, fb_ref, o_ref):
    n_in, rows = g_ref.shape[0], o_ref.shape[0]
    g2, o2 = _rows2d(g_ref), _rows2d(o_ref)
    for j in range(B_GROUP):
        gj = g2[pl.ds(j, n_in, stride=B_GROUP), :]
        o2[pl.ds(j, rows, stride=B_GROUP), :] = _dot(tab_ref[j], gj.astype(BF16))
    o_ref[...] = gate_ref[...] * (o_ref[...] + z_ref[...] * fb_ref[...])


def _s3(g3, tab, gate4, gi, z4, zi, fb, *, ct):
    n_in, _, d = g3.shape
    rows = tab.shape[1]
    blk = pl.BlockSpec((rows, B_GROUP, ct), lambda g, c: (0, g, c))
    return pl.pallas_call(
        _s3_body,
        grid=(FFT_N2 // B_GROUP, d // ct),
        in_specs=[
            pl.BlockSpec((n_in, B_GROUP, ct), lambda g, c: (0, g, c)),
            pl.BlockSpec((B_GROUP, rows, n_in), lambda g, c: (g, 0, 0)),
            pl.BlockSpec((None, rows, B_GROUP, ct), lambda g, c: (gi, 0, g, c)),
            pl.BlockSpec((None, rows, B_GROUP, ct), lambda g, c: (zi, 0, g, c)),
            pl.BlockSpec((1, 1, ct), lambda g, c: (0, 0, c)),
        ],
        out_specs=blk,
        out_shape=jax.ShapeDtypeStruct((rows, FFT_N2, d), F32),
        compiler_params=_cp(("arbitrary", "arbitrary")),
        name="hyena_conv_idft1",
    )(g3, tab, gate4, z4, fb)


def _dft_tables(n_lat):
    n = 2 * n_lat
    n1 = n // FFT_N2
    b = jnp.arange(FFT_N2, dtype=jnp.int32)[:, None, None]
    k1 = jnp.arange(n1, dtype=jnp.int32)[None, :, None]
    a = jnp.arange(n1, dtype=jnp.int32)[None, None, :]
    th = ((k1 * (a * FFT_N2 + b)) % n).astype(F32) * (2.0 * math.pi / n)
    cr, sn = jnp.cos(th), jnp.sin(th)
    ha = n1 // 2
    crh, snh = cr[:, :, :ha], sn[:, :, :ha]
    w1 = jnp.concatenate([jnp.concatenate([crh, snh], axis=2), jnp.concatenate([-snh, crh], axis=2)], axis=1)
    w1f = jnp.concatenate([cr, -sn], axis=1)
    v = jnp.swapaxes(w1, 1, 2) * (1.0 / n)
    k2 = jnp.arange(FFT_N2, dtype=jnp.int32)
    th2 = ((k2[:, None] * k2[None, :]) % FFT_N2).astype(F32) * (2.0 * math.pi / FFT_N2)
    c2, s2 = jnp.cos(th2), jnp.sin(th2)
    ff = jnp.concatenate([jnp.concatenate([c2, s2], axis=1), jnp.concatenate([-s2, c2], axis=1)], axis=0)
    fi = jnp.concatenate([jnp.concatenate([c2, -s2], axis=1), jnp.concatenate([s2, c2], axis=1)], axis=0)
    return w1.astype(BF16), w1f.astype(BF16), v.astype(BF16), ff.astype(BF16), fi.astype(BF16)


def _hyena_mix(proj3, fparams, fbias, *, n_lat):
    _, bsz, _, d = proj3.shape
    f_w1, f_b1, f_f1, f_w2, f_b2, f_f2, f_w3, decay = fparams
    na = 2 * n_lat // FFT_N2
    w1, w1f, v, ff, fi = _dft_tables(n_lat)
    ct = LANES
    hd = _filt_feat(f_w1, f_b1, f_f1, f_w2, f_b2, f_f2, n_lat=n_lat)
    w3r = jnp.transpose(f_w3.reshape(HY_HID, 2, 2, d), (1, 2, 0, 3))
    w3r = jnp.zeros((2, 2, LANES, d), F32).at[:, :, :HY_HID].set(w3r)
    kf1 = _filt_s1(hd, w3r, decay.reshape(2, 2, 1, d), w1f, ct=ct)
    kf = _s2(kf1.reshape(2, 2, na, FFT_N2, d), None, 0, ff, None, ct=d, conv=False)
    p3 = proj3.reshape(3, bsz * (n_lat // FFT_N2), FFT_N2, d)
    z4, zi = p3, 2
    for o in range(2):
        o1 = _s1(z4, zi, w1, ct=ct)
        g = _s2(o1.reshape(2, na, FFT_N2, d), kf, o, ff, fi, ct=d, conv=True)
        z4 = _s3(g.reshape(2 * na, FFT_N2, d), v, p3, o, z4, zi, fbias[o].reshape(1, 1, d), ct=ct)[None]
        zi = 0
    return z4.reshape(bsz, n_lat, d)


def _rope_tables(n_tokens):
    rows = n_tokens // GRID_W
    row = jnp.broadcast_to(jnp.arange(rows, dtype=F32)[:, None], (rows, GRID_W)).reshape(-1)
    col = jnp.broadcast_to(jnp.arange(GRID_W, dtype=F32)[None, :], (rows, GRID_W)).reshape(-1)
    axis_dim = QK_ROPE // 2
    inv_freq = 1.0 / (ROPE_THETA ** (jnp.arange(0, axis_dim, 2, dtype=F32) / axis_dim))
    ang = jnp.concatenate([row[:, None] * inv_freq, col[:, None] * inv_freq], axis=-1)
    return jnp.cos(ang), jnp.sin(ang)


def _mla_weights(w_down, g_q, w_uq, g_kv, w_ukv):
    d = w_down.shape[0]
    nh = MLA_HEADS
    kpe = w_down[:, Q_LORA + KV_LORA:]
    w1, w2 = kpe[:, 0::2], kpe[:, 1::2]
    z = jnp.zeros((d, LANES - QK_ROPE), w_down.dtype)
    wd = jnp.concatenate([w_down[:, :Q_LORA + KV_LORA], w1, w2, z, w2, w1, z], axis=1).astype(BF16)
    uq = w_uq.reshape(Q_LORA, nh, QK_NOPE + QK_ROPE)
    pe = uq[:, :, QK_NOPE:]
    uq = jnp.concatenate([uq[:, :, :QK_NOPE], pe[:, :, 0::2], pe[:, :, 1::2]], axis=2)
    wuqT = uq.reshape(Q_LORA, nh * (QK_NOPE + QK_ROPE)).T.astype(BF16)
    ukv = w_ukv.reshape(KV_LORA, nh, QK_NOPE + V_DIM)
    wuk = ukv[:, :, :QK_NOPE].reshape(KV_LORA, nh * QK_NOPE).astype(BF16)
    wuvT = ukv[:, :, QK_NOPE:].reshape(KV_LORA, nh * V_DIM).T.astype(BF16)
    return wd, g_q.reshape(1, -1), g_kv.reshape(1, -1), wuk, wuqT, wuvT


def kernel(x, c, ctx, c_ctx, ada_w, ada_b, norm_mix_g, norm_ffn_g, mla_w_down, mla_g_q, mla_w_uq, mla_g_kv, mla_w_ukv, mla_w_o, hy_w_in, hy_b_in, hy_conv_w, hy_conv_b, hy_f_w1, hy_f_b1, hy_f_freq1, hy_f_w2, hy_f_b2, hy_f_freq2, hy_f_w3, hy_decay, hy_bias, hy_w_out, hy_b_out, moe_w_r, moe_b_r, moe_w_in, moe_b_in, moe_w_out, moe_b_out, final_g):
    bsz, n_lat, d = x.shape
    n_ctx = ctx.shape[1]
    depth = ada_w.shape[0]
    assert bsz == 2 and d == MLA_HEADS * V_DIM and n_lat % 512 == 0 and n_ctx % 128 == 0
    assert depth == 2

    cond8 = jnp.zeros((8, d), F32).at[:bsz].set(c).at[bsz].set(c_ctx)
    mods = _ada(cond8, ada_w, ada_b)

    def mod(i, j, rows):
        return mods[i, rows, j * d:(j + 1) * d][:, None, :]

    lat_rows = slice(0, bsz)
    ctx_rows = slice(bsz, bsz + 1)
    xl = x
    for i in range(depth):
        kind, j = i % 2, i // 2
        sh1, sc1, g1 = (mod(i, m, lat_rows) for m in range(3))
        sh2, sc2, g2 = (mod(i, m, lat_rows) for m in range(3, 6))
        gm = norm_mix_g[i].reshape(1, d)
        if kind == 0:
            wts = _mla_weights(mla_w_down[j], mla_g_q[j], mla_w_uq[j], mla_g_kv[j], mla_w_ukv[j])
            cos, sin = _rope_tables(n_lat)
            zl = jnp.zeros((n_lat, LANES - QK_ROPE), F32)
            tabs = (jnp.concatenate([cos, cos, zl], axis=1), jnp.concatenate([-sin, sin, zl], axis=1), cos.T, sin.T)
            tq = tv = 512
            tk = 1024 if n_lat % 2048 == 0 else 512
            qT, k, vT = _mla_proj(xl, gm, sh1, sc1, wts, tabs, need_q=True, tm=tv, tk=tv)
            half = QK_ROPE // 2
            one_c = jnp.concatenate([jnp.ones((n_ctx, QK_ROPE), F32), jnp.zeros((n_ctx, LANES - QK_ROPE), F32)], axis=1)
            tabs_c = (one_c, jnp.zeros((n_ctx, LANES), F32), jnp.ones((half, n_ctx), F32), jnp.zeros((half, n_ctx), F32))
            kc, vTc = _mla_proj(ctx, gm, mod(i, 0, ctx_rows), mod(i, 1, ctx_rows), wts, tabs_c,
                                need_q=False, tm=n_ctx, tk=n_ctx)
            o = _attention(qT, k, vT, kc, vTc, tq=tq, tk=tk)
            wo = mla_w_o[j].astype(BF16)
            bo = jnp.zeros((1, d), F32)
            transposed = True
        else:
            proj3 = _hy_in(xl, gm, sh1, sc1, hy_w_in[j].astype(BF16), hy_b_in[j].reshape(1, -1), hy_conv_w[j],
                           hy_conv_b[j].reshape(1, -1), tm=512)
            fparams = (hy_f_w1[j], hy_f_b1[j], hy_f_freq1[j], hy_f_w2[j], hy_f_b2[j], hy_f_freq2[j], hy_f_w3[j],
                       hy_decay[j])
            o = _hyena_mix(proj3, fparams, hy_bias[j], n_lat=n_lat)
            wo = hy_w_out[j].astype(BF16)
            bo = hy_b_out[j].reshape(1, d)
            transposed = False
        wr = jnp.zeros((d, LANES), F32).at[:, :N_EXPERTS].set(moe_w_r[i])
        wrh = wr.astype(BF16)
        wrl = (wr - wrh.astype(F32)).astype(BF16)
        br = jnp.zeros((1, LANES), F32).at[0, :N_EXPERTS].set(moe_b_r[i])
        xl, fl, topi, gates, rank, cnt = _post(o, wo, bo, xl, g1, norm_ffn_g[i].reshape(1, d), sh2, sc2, wrh, wrl, br,
                                               transposed=transposed, tm=512)
        xl = _moe(fl, topi, gates, rank, cnt, xl, g2, final_g.reshape(1, d), i, moe_w_in, moe_b_in,
                  moe_w_out, moe_b_out, final=(i == depth - 1))
    return xl
```

```python
import functools
import math

import jax
import jax.numpy as jnp
from jax import lax
from jax.experimental import pallas as pl
from jax.experimental.pallas import tpu as pltpu

F32 = jnp.float32
BF16 = jnp.bfloat16

EPS = 1e-6
GRID_W = 64
MLA_HEADS = 8
QK_NOPE = 128
QK_ROPE = 64
V_DIM = 128
Q_LORA = 512
KV_LORA = 256
ROPE_THETA = 10000.0
MLA_SCALE = (QK_NOPE + QK_ROPE) ** -0.5
QK_PAD = 256

HY_EMB = 33
HY_BANDS = (HY_EMB - 1) // 2
HY_HID = 64
FFT_N2 = 128
B_GROUP = 8

N_EXPERTS = 32
TOP_K = 4
SWIGLU_LIMIT = 7.0
SWIGLU_ALPHA = 1.702
MOE_TM = 256
LANES = 128

VMEM_LIMIT = 56 * 1024 * 1024


def _cp(sem, vmem=VMEM_LIMIT):
    return pltpu.CompilerParams(dimension_semantics=sem, vmem_limit_bytes=vmem)


def _dot(a, b):
    return jnp.dot(a, b, preferred_element_type=F32)


def _dot_hi(a, b):
    return jnp.dot(a, b, preferred_element_type=F32, precision=lax.Precision.HIGHEST)


def _dot3(a, b):
    ah = a.astype(BF16)
    al = (a - ah.astype(F32)).astype(BF16)
    bh = b.astype(BF16)
    bl = (b - bh.astype(F32)).astype(BF16)
    return _dot(ah, bh) + (_dot(al, bh) + _dot(ah, bl))


def _rms(x, g):
    return x * lax.rsqrt(jnp.mean(x * x, axis=-1, keepdims=True) + EPS) * g


def _ada_body(c_ref, w_ref, b_ref, o_ref):
    c = c_ref[...]
    s = c * jax.nn.sigmoid(c)
    o_ref[0] = _dot(s.astype(BF16), w_ref[0].astype(BF16)) + b_ref[0]


def _ada(cond8, ada_w, ada_b):
    depth, d, n = ada_w.shape
    tn = n // 4
    return pl.pallas_call(
        _ada_body,
        grid=(depth, n // tn),
        in_specs=[
            pl.BlockSpec((8, d), lambda i, j: (0, 0)),
            pl.BlockSpec((1, d, tn), lambda i, j: (i, 0, j)),
            pl.BlockSpec((1, 1, tn), lambda i, j: (i, 0, j)),
        ],
        out_specs=pl.BlockSpec((1, 8, tn), lambda i, j: (i, 0, j)),
        out_shape=jax.ShapeDtypeStruct((depth, 8, n), F32),
        compiler_params=_cp(("arbitrary", "arbitrary")),
        name="ada_mod",
    )(cond8, ada_w, ada_b.reshape(depth, 1, n))


def _mla_proj_body(x_ref, g_ref, sh_ref, sc_ref, wd_ref, gq_ref, gkv_ref, wuk_ref, wuqT_ref, wuvT_ref,
                   ct_ref, st_ref, cT_ref, sT_ref, *out_refs, need_q, tk):
    if need_q:
        qT_ref, k_ref, vT_ref = out_refs
    else:
        k_ref, vT_ref = out_refs
    nh = MLA_HEADS
    x = x_ref[0]
    h = _rms(x, g_ref[...]) * (1.0 + sc_ref[0]) + sh_ref[0]
    lat = _dot(h.astype(BF16), wd_ref[...])
    o_kv = Q_LORA
    o_a = Q_LORA + KV_LORA
    kvn = _rms(lat[:, o_kv:o_a], gkv_ref[...])
    kr = (lat[:, o_a:o_a + LANES] * ct_ref[...] + lat[:, o_a + LANES:o_a + 2 * LANES] * st_ref[...]).astype(BF16)
    knope = _dot(kvn.astype(BF16), wuk_ref[...])
    for hh in range(nh):
        k_ref[0, hh, :, 0:QK_NOPE] = knope[:, hh * QK_NOPE:(hh + 1) * QK_NOPE].astype(BF16)
        k_ref[0, hh, :, QK_NOPE:QK_PAD] = kr
    vT = _dot(wuvT_ref[...], kvn.T.astype(BF16))
    tm = x.shape[0]
    for hh in range(nh):
        for c in range(tm // tk):
            vT_ref[0, hh, c] = vT[hh * V_DIM:(hh + 1) * V_DIM, c * tk:(c + 1) * tk].astype(BF16)
    if need_q:
        qn = _rms(lat[:, :Q_LORA], gq_ref[...])
        qT = _dot(wuqT_ref[...], qn.T.astype(BF16)) * (MLA_SCALE * math.log2(math.e))
        c = cT_ref[...]
        s = sT_ref[...]
        hw = QK_NOPE + QK_ROPE
        half = QK_ROPE // 2
        for hh in range(nh):
            base = hh * hw
            x1 = qT[base + QK_NOPE:base + QK_NOPE + half]
            x2 = qT[base + QK_NOPE + half:base + hw]
            qT_ref[0, hh, 0:QK_NOPE] = qT[base:base + QK_NOPE].astype(BF16)
            qT_ref[0, hh, QK_NOPE:QK_NOPE + half] = (x1 * c - x2 * s).astype(BF16)
            qT_ref[0, hh, QK_NOPE + half:hw] = (x1 * s + x2 * c).astype(BF16)
            qT_ref[0, hh, hw:QK_PAD] = jnp.zeros((QK_PAD - hw, tm), BF16)


def _mla_proj(x, g, sh, sc, wts, tabs, *, need_q, tm, tk):
    bsz, n, d = x.shape
    nh = MLA_HEADS
    wd, gq, gkv, wuk, wuqT, wuvT = wts
    ct, st, cT, sT = tabs
    nsh = sh.shape[0]
    full = lambda a: pl.BlockSpec(a.shape, lambda b, i: (0,) * a.ndim)
    in_specs = [
        pl.BlockSpec((1, tm, d), lambda b, i: (b, i, 0)),
        full(g),
        pl.BlockSpec((1, 1, d), lambda b, i: (b % nsh, 0, 0)),
        pl.BlockSpec((1, 1, d), lambda b, i: (b % nsh, 0, 0)),
        full(wd), full(gq), full(gkv), full(wuk), full(wuqT), full(wuvT),
        pl.BlockSpec((tm, LANES), lambda b, i: (i, 0)),
        pl.BlockSpec((tm, LANES), lambda b, i: (i, 0)),
        pl.BlockSpec((QK_ROPE // 2, tm), lambda b, i: (0, i)),
        pl.BlockSpec((QK_ROPE // 2, tm), lambda b, i: (0, i)),
    ]
    out_specs = [
        pl.BlockSpec((1, nh, tm, QK_PAD), lambda b, i: (b, 0, i, 0)),
        pl.BlockSpec((1, nh, tm // tk, V_DIM, tk), lambda b, i: (b, 0, i, 0, 0)),
    ]
    out_shape = [
        jax.ShapeDtypeStruct((bsz, nh, n, QK_PAD), BF16),
        jax.ShapeDtypeStruct((bsz, nh, n // tk, V_DIM, tk), BF16),
    ]
    if need_q:
        out_specs = [pl.BlockSpec((1, nh, QK_PAD, tm), lambda b, i: (b, 0, 0, i))] + out_specs
        out_shape = [jax.ShapeDtypeStruct((bsz, nh, QK_PAD, n), BF16)] + out_shape
    return pl.pallas_call(
        functools.partial(_mla_proj_body, need_q=need_q, tk=tk),
        grid=(bsz, n // tm),
        in_specs=in_specs,
        out_specs=out_specs,
        out_shape=out_shape,
        compiler_params=_cp(("arbitrary", "arbitrary")),
        name="mla_proj_q" if need_q else "mla_proj_ctx",
    )(x, g, sh, sc, wd, gq, gkv, wuk, wuqT, wuvT, ct, st, cT, sT)


SM_STRIP = 64
SUBLANES = 8


def _attn_body(qT_ref, k_ref, vT_ref, kc_ref, vTc_ref, o_ref, s0, s1, p0, p1, sc, pc, acc, m_scr, x0, x1, xc,
               a0, a1, ac, l_scr, d0, d1, dc, *, tk):
    nchunk = k_ref.shape[2] // tk

    def scores(kblk, s_ref, mx_ref):
        r = _dot(kblk, qT_ref[0, 0])
        s_ref[...] = r
        mx_ref[...] = jnp.max(r, axis=0, keepdims=True)

    def probs(s_ref, mx_ref, p_ref, a_ref, d_ref):
        m_old = m_scr[...]
        m_new = jnp.maximum(m_old, mx_ref[...])
        m_scr[...] = m_new
        alpha = jnp.exp2(m_old - m_new)
        a_ref[...] = alpha
        part = None
        for r in range(0, s_ref.shape[0], SM_STRIP):
            p = jnp.exp2(s_ref[r:r + SM_STRIP] - m_new)
            p_ref[r:r + SM_STRIP] = p.astype(BF16)
            ps = jnp.sum(p.reshape(SM_STRIP // SUBLANES, SUBLANES, p.shape[1]), axis=0)
            part = ps if part is None else part + ps
        d_ref[...] = part

    def accumulate(p_ref, a_ref, d_ref, vblk):
        acc[...] = a_ref[...] * acc[...] + _dot(vblk, p_ref[...])
        l_scr[...] = a_ref[...] * l_scr[...] + d_ref[...]

    def kchunk(i):
        i = jnp.minimum(i, nchunk - 1)
        return k_ref[0, 0, pl.ds(pl.multiple_of(i * tk, tk), tk), :]

    def vchunk(i):
        nsub = tk // vT_ref.shape[-1]
        return jnp.concatenate([vT_ref[0, 0, i * nsub + u] for u in range(nsub)], axis=1)

    m_scr[...] = jnp.full(m_scr.shape, -jnp.inf, F32)
    acc[...] = jnp.zeros(acc.shape, F32)
    l_scr[...] = jnp.zeros(l_scr.shape, F32)
    scores(kc_ref[0, 0], sc, xc)
    scores(kchunk(0), s0, x0)
    probs(sc, xc, pc, ac, dc)
    scores(kchunk(1), s1, x1)
    accumulate(pc, ac, dc, vTc_ref[0, 0, 0])
    probs(s0, x0, p0, a0, d0)

    def body(j, carry):
        t = 2 * j
        scores(kchunk(t + 2), s0, x0)
        accumulate(p0, a0, d0, vchunk(t))
        probs(s1, x1, p1, a1, d1)
        scores(kchunk(t + 3), s1, x1)
        accumulate(p1, a1, d1, vchunk(t + 1))
        probs(s0, x0, p0, a0, d0)
        return carry

    lax.fori_loop(0, nchunk // 2, body, 0)
    o_ref[0, 0] = (acc[...] / jnp.sum(l_scr[...], axis=0, keepdims=True)).astype(BF16)


def _attention(qT, k, vT, kc, vTc, *, tq, tk):
    bsz, nh, _, n = qT.shape
    nc = kc.shape[2]
    tv = vT.shape[-1]
    assert (n // tk) % 2 == 0 and tk % tv == 0
    return pl.pallas_call(
        functools.partial(_attn_body, tk=tk),
        grid=(bsz, nh, n // tq),
        in_specs=[
            pl.BlockSpec((1, 1, QK_PAD, tq), lambda b, h, i: (b, h, 0, i)),
            pl.BlockSpec((1, 1, n, QK_PAD), lambda b, h, i: (b, h, 0, 0)),
            pl.BlockSpec((1, 1, n // tv, V_DIM, tv), lambda b, h, i: (b, h, 0, 0, 0)),
            pl.BlockSpec((1, 1, nc, QK_PAD), lambda b, h, i: (b, h, 0, 0)),
            pl.BlockSpec((1, 1, 1, V_DIM, nc), lambda b, h, i: (b, h, 0, 0, 0)),
        ],
        out_specs=pl.BlockSpec((1, 1, V_DIM, tq), lambda b, h, i: (b, h, 0, i)),
        out_shape=jax.ShapeDtypeStruct((bsz, nh, V_DIM, n), BF16),
        scratch_shapes=[pltpu.VMEM((tk, tq), F32), pltpu.VMEM((tk, tq), F32),
                        pltpu.VMEM((tk, tq), BF16), pltpu.VMEM((tk, tq), BF16),
                        pltpu.VMEM((nc, tq), F32), pltpu.VMEM((nc, tq), BF16),
                        pltpu.VMEM((V_DIM, tq), F32), pltpu.VMEM((1, tq), F32),
                        pltpu.VMEM((1, tq), F32), pltpu.VMEM((1, tq), F32), pltpu.VMEM((1, tq), F32),
                        pltpu.VMEM((1, tq), F32), pltpu.VMEM((1, tq), F32), pltpu.VMEM((1, tq), F32),
                        pltpu.VMEM((SUBLANES, tq), F32), pltpu.VMEM((SUBLANES, tq), F32),
                        pltpu.VMEM((SUBLANES, tq), F32), pltpu.VMEM((SUBLANES, tq), F32)],
        compiler_params=_cp(("arbitrary", "arbitrary", "arbitrary")),
        name="mla_attention",
    )(qT, k, vT, kc, vTc)


def _post_body(o_ref, wo_ref, bo_ref, x_ref, g1_ref, gf_ref, sh_ref, sc_ref, wrh_ref, wrl_ref, br_ref, tri_ref,
               xl_ref, fl_ref, ti_ref, gt_ref, rk_ref, cnt_ref, *, transposed):
    @pl.when((pl.program_id(0) == 0) & (pl.program_id(1) == 0))
    def _():
        cnt_ref[...] = jnp.zeros_like(cnt_ref)

    tm = x_ref.shape[1]
    if transposed:
        oT = o_ref[0].astype(F32).reshape(MLA_HEADS * V_DIM, tm)
        o = oT.T.astype(BF16)
    else:
        o = o_ref[0].astype(BF16)
    y = _dot(o, wo_ref[...]) + bo_ref[...]
    xl = x_ref[0] + g1_ref[0] * y
    xl_ref[0] = xl
    fl = _rms(xl, gf_ref[...]) * (1.0 + sc_ref[0]) + sh_ref[0]
    _to_rows(fl_ref, fl)
    flh = fl.astype(BF16)
    fll = (fl - flh.astype(F32)).astype(BF16)
    logits = _dot(flh, wrh_ref[...]) + (_dot(fll, wrh_ref[...]) + _dot(flh, wrl_ref[...])) + br_ref[...]
    lane = lax.broadcasted_iota(jnp.int32, (tm, LANES), 1).astype(F32)
    neg = jnp.float32(-jnp.inf)
    work = jnp.where(lane < N_EXPERTS, logits, neg)
    vals, idxs = [], []
    onehot = jnp.zeros((tm, LANES), F32)
    for _ in range(TOP_K):
        mk = jnp.max(work, axis=-1, keepdims=True)
        ik = jnp.min(jnp.where(work == mk, lane, float(LANES)), axis=-1, keepdims=True)
        sel = lane == ik
        onehot = jnp.where(sel, 1.0, onehot)
        work = jnp.where(sel, neg, work)
        vals.append(mk)
        idxs.append(ik)
    es = [jnp.exp(v - vals[0]) for v in vals]
    den = es[0] + es[1] + es[2] + es[3]
    pre = _dot(tri_ref[...], onehot.astype(BF16)) + cnt_ref[...]
    ti = jnp.zeros((tm, LANES), F32)
    gt = jnp.zeros((tm, LANES), F32)
    rk = jnp.zeros((tm, LANES), F32)
    for kk in range(TOP_K):
        rank = jnp.sum(jnp.where(lane == idxs[kk], pre, 0.0), axis=-1, keepdims=True)
        ti = jnp.where(lane == kk, idxs[kk], ti)
        gt = jnp.where(lane == kk, es[kk] / den, gt)
        rk = jnp.where(lane == kk, rank, rk)
    ti_ref[...] = ti[:, :TOP_K].astype(jnp.int32)
    gt_ref[...] = gt[:, :TOP_K]
    rk_ref[...] = rk[:, :TOP_K].astype(jnp.int32)
    cnt_ref[...] += jnp.sum(onehot, axis=0, keepdims=True)


def _post(o, wo, bo, x, g1, gf, sh, sc, wrh, wrl, br, *, transposed, tm):
    bsz, n, d = x.shape
    t = bsz * n
    nt = n // tm
    tri = (lax.broadcasted_iota(jnp.int32, (tm, tm), 0) > lax.broadcasted_iota(jnp.int32, (tm, tm), 1)).astype(BF16)
    full = lambda a: pl.BlockSpec(a.shape, lambda b, i: (0,) * a.ndim)
    per_b = pl.BlockSpec((1, 1, d), lambda b, i: (b, 0, 0))
    if transposed:
        o_spec = pl.BlockSpec((1, MLA_HEADS, V_DIM, tm), lambda b, i: (b, 0, 0, i))
    else:
        o_spec = pl.BlockSpec((1, tm, d), lambda b, i: (b, i, 0))
    tok = lambda w: pl.BlockSpec((tm, w), lambda b, i: (b * nt + i, 0))
    return pl.pallas_call(
        functools.partial(_post_body, transposed=transposed),
        grid=(bsz, nt),
        in_specs=[o_spec, full(wo), full(bo), pl.BlockSpec((1, tm, d), lambda b, i: (b, i, 0)), per_b, full(gf),
                  per_b, per_b, full(wrh), full(wrl), full(br), full(tri)],
        out_specs=[pl.BlockSpec((1, tm, d), lambda b, i: (b, i, 0)),
                   pl.BlockSpec((tm * ROW_SUB, LANES), lambda b, i: (b * nt + i, 0)),
                   tok(TOP_K), tok(TOP_K), tok(TOP_K), pl.BlockSpec((1, LANES), lambda b, i: (0, 0))],
        out_shape=[jax.ShapeDtypeStruct((bsz, n, d), F32), jax.ShapeDtypeStruct((t * ROW_SUB, LANES), F32),
                   jax.ShapeDtypeStruct((t, TOP_K), jnp.int32), jax.ShapeDtypeStruct((t, TOP_K), F32),
                   jax.ShapeDtypeStruct((t, TOP_K), jnp.int32), jax.ShapeDtypeStruct((1, LANES), F32)],
        compiler_params=_cp(("arbitrary", "arbitrary")),
        name="post_attn" if transposed else "post_hyena",
    )(o, wo, bo, x, g1, gf, sh, sc, wrh, wrl, br, tri)


ROW_SUB = 8


def _row_slice(i):
    return pl.ds(pl.multiple_of(i * ROW_SUB, ROW_SUB), ROW_SUB)


def _to_rows(ref, x):
    for s in range(ROW_SUB):
        ref[pl.ds(s, x.shape[0], stride=ROW_SUB), :] = x[:, s * LANES:(s + 1) * LANES]


def _from_rows(ref, lo, hi):
    return jnp.concatenate([ref[pl.ds(lo * ROW_SUB + s, hi - lo, stride=ROW_SUB), :] for s in range(ROW_SUB)], axis=1)


def _dispatch_body(pe_ref, pd_ref, dest_ref, fl_ref, xs_out, zbuf, sem, *, td):
    @pl.when(pl.program_id(0) == 0)
    def _():
        zbuf[...] = jnp.zeros(zbuf.shape, zbuf.dtype)
        for e in range(N_EXPERTS):
            @pl.when(pd_ref[e] > 0)
            def _():
                start = pl.multiple_of((pe_ref[e] - MOE_TM) * ROW_SUB, ROW_SUB)
                cp = pltpu.make_async_copy(zbuf, xs_out.at[pl.ds(start, MOE_TM * ROW_SUB)], sem)
                cp.start()
                cp.wait()

    def issue(t, carry):
        for kk in range(TOP_K):
            d = dest_ref[0, 0, t * TOP_K + kk]
            pltpu.make_async_copy(fl_ref.at[_row_slice(t)], xs_out.at[_row_slice(d)], sem).start(priority=kk % 2)
        return carry

    lax.fori_loop(0, td, issue, 0, unroll=2)

    def drain(t, carry):
        pltpu.make_async_copy(fl_ref.at[_row_slice(0)], xs_out.at[_row_slice(0)], sem).wait()
        return carry

    lax.fori_loop(0, td * TOP_K, drain, 0, unroll=8)


def _dispatch(pad_end, padded, dest, fl, n_rows, *, td):
    t = fl.shape[0] // ROW_SUB
    dest3 = dest.reshape(t // td, 1, td * TOP_K)
    grid_spec = pltpu.PrefetchScalarGridSpec(
        num_scalar_prefetch=2,
        grid=(t // td,),
        in_specs=[
            pl.BlockSpec((1, 1, td * TOP_K), lambda i, pe, pd: (i, 0, 0), memory_space=pltpu.SMEM),
            pl.BlockSpec((td * ROW_SUB, LANES), lambda i, pe, pd: (i, 0)),
        ],
        out_specs=pl.BlockSpec(memory_space=pl.ANY),
        scratch_shapes=[pltpu.VMEM((MOE_TM * ROW_SUB, LANES), fl.dtype), pltpu.SemaphoreType.DMA(())],
    )
    return pl.pallas_call(
        functools.partial(_dispatch_body, td=td),
        grid_spec=grid_spec,
        out_shape=jax.ShapeDtypeStruct((n_rows * ROW_SUB, LANES), fl.dtype),
        compiler_params=_cp(("arbitrary",)),
        name="moe_dispatch",
    )(pad_end, padded, dest3, fl)


def _expert_body(be_ref, nu_ref, xs_ref, win_ref, bin_ref, wout_ref, bout_ref, ys_ref, win_s, wout_s):
    b = pl.program_id(0)
    dff = wout_ref.shape[1]

    @pl.when(b < nu_ref[0])
    def _():
        prev = be_ref[jnp.maximum(b - 1, 0)]

        @pl.when((b == 0) | (prev != be_ref[b]))
        def _():
            win_s[...] = win_ref[0].astype(BF16)
            wout_s[...] = wout_ref[0].astype(BF16)

        x = _from_rows(xs_ref, 0, xs_ref.shape[0] // ROW_SUB).astype(BF16)
        gu = _dot(x, win_s[...]) + bin_ref[0]
        gate = jnp.minimum(gu[:, :dff], SWIGLU_LIMIT)
        lin = jnp.clip(gu[:, dff:], -SWIGLU_LIMIT, SWIGLU_LIMIT)
        act = gate * jax.nn.sigmoid(SWIGLU_ALPHA * gate) * (lin + 1.0)
        _to_rows(ys_ref, _dot(act.astype(BF16), wout_s[...]) + bout_ref[0])

    @pl.when(b >= nu_ref[0])
    def _():
        ys_ref[...] = jnp.zeros_like(ys_ref)


def _experts(blk_exp, n_used, xs, layer, w_in, b_in, w_out, b_out):
    n_rows = xs.shape[0] // ROW_SUB
    depth, ne, d, f2 = w_in.shape
    dff = w_out.shape[2]
    tm = MOE_TM
    grid_spec = pltpu.PrefetchScalarGridSpec(
        num_scalar_prefetch=2,
        grid=(n_rows // tm,),
        in_specs=[
            pl.BlockSpec((tm * ROW_SUB, LANES), lambda b, be, nu: (jnp.minimum(b, nu[0] - 1), 0)),
            pl.BlockSpec((None, 1, d, f2), lambda b, be, nu: (layer, be[b], 0, 0)),
            pl.BlockSpec((None, 1, 1, f2), lambda b, be, nu: (layer, be[b], 0, 0)),
            pl.BlockSpec((None, 1, dff, d), lambda b, be, nu: (layer, be[b], 0, 0)),
            pl.BlockSpec((None, 1, 1, d), lambda b, be, nu: (layer, be[b], 0, 0)),
        ],
        out_specs=pl.BlockSpec((tm * ROW_SUB, LANES), lambda b, be, nu: (b, 0)),
        scratch_shapes=[pltpu.VMEM((d, f2), BF16), pltpu.VMEM((dff, d), BF16)],
    )
    return pl.pallas_call(
        _expert_body,
        grid_spec=grid_spec,
        out_shape=jax.ShapeDtypeStruct(xs.shape, F32),
        compiler_params=_cp(("arbitrary",)),
        name="moe_experts",
    )(blk_exp, n_used, xs, w_in, b_in.reshape(depth, ne, 1, f2), w_out, b_out.reshape(depth, ne, 1, d))


def _combine_body(dest_ref, ys_hbm, gt_ref, xl_ref, g2_ref, fg_ref, out_ref, buf, sem, *, tc, final):
    def issue(t, carry):
        for kk in range(TOP_K):
            d = dest_ref[0, 0, t * TOP_K + kk]
            pltpu.make_async_copy(ys_hbm.at[_row_slice(d)], buf.at[_row_slice(kk * tc + t)], sem).start(
                priority=kk % 2)
        return carry

    lax.fori_loop(0, tc, issue, 0, unroll=2)

    def drain(t, carry):
        pltpu.make_async_copy(ys_hbm.at[_row_slice(0)], buf.at[_row_slice(0)], sem).wait()
        return carry

    lax.fori_loop(0, tc * TOP_K, drain, 0, unroll=8)
    gt = gt_ref[...]
    y = gt[:, 0:1] * _from_rows(buf, 0, tc)
    for kk in range(1, TOP_K):
        y = y + gt[:, kk:kk + 1] * _from_rows(buf, kk * tc, (kk + 1) * tc)
    xl = xl_ref[0] + g2_ref[0] * y
    out_ref[0] = _rms(xl, fg_ref[...]) if final else xl


def _combine(dest, ys, gates, xl, g2, fg, *, tc, final):
    bsz, n, d = xl.shape
    t = bsz * n
    nt = n // tc
    dest3 = dest.reshape(t // tc, 1, tc * TOP_K)
    return pl.pallas_call(
        functools.partial(_combine_body, tc=tc, final=final),
        grid=(bsz, nt),
        in_specs=[
            pl.BlockSpec((1, 1, tc * TOP_K), lambda b, i: (b * nt + i, 0, 0), memory_space=pltpu.SMEM),
            pl.BlockSpec(memory_space=pl.ANY),
            pl.BlockSpec((tc, TOP_K), lambda b, i: (b * nt + i, 0)),
            pl.BlockSpec((1, tc, d), lambda b, i: (b, i, 0)),
            pl.BlockSpec((1, 1, d), lambda b, i: (b, 0, 0)),
            pl.BlockSpec((1, d), lambda b, i: (0, 0)),
        ],
        out_specs=pl.BlockSpec((1, tc, d), lambda b, i: (b, i, 0)),
        out_shape=jax.ShapeDtypeStruct((bsz, n, d), F32),
        scratch_shapes=[pltpu.VMEM((TOP_K * tc * ROW_SUB, LANES), F32), pltpu.SemaphoreType.DMA(())],
        compiler_params=_cp(("arbitrary", "arbitrary")),
        name="moe_combine",
    )(dest3, ys, gates, xl, g2, fg)


def _moe(fl, topi, gates, rank, cnt, xl, g2, fg, layer, w_in, b_in, w_out, b_out, *, final):
    t = fl.shape[0] // ROW_SUB
    tm = MOE_TM
    counts = cnt[0, :N_EXPERTS].astype(jnp.int32)
    padded = (counts + tm - 1) // tm * tm
    pad_end = jnp.cumsum(padded)
    pad_start = pad_end - padded
    dest = jnp.take(pad_start, topi) + rank
    nb = t * TOP_K // tm + N_EXPERTS
    blk_start = jnp.arange(nb, dtype=jnp.int32) * tm
    blk_exp = jnp.minimum(jnp.sum((pad_end[None, :] <= blk_start[:, None]).astype(jnp.int32), axis=1), N_EXPERTS - 1)
    n_used = (pad_end[-1:] // tm).astype(jnp.int32)
    xs = _dispatch(pad_end, padded, dest, fl, nb * tm, td=256)
    ys = _experts(blk_exp, n_used, xs, layer, w_in, b_in, w_out, b_out)
    return _combine(dest, ys, gates, xl, g2, fg, tc=256, final=final)


def _hy_in_body(x_ref, xp_ref, xn_ref, g_ref, sh_ref, sc_ref, w_ref, b_ref, cw_ref, cb_ref, o_ref, *, nt):
    i = pl.program_id(2)
    w = w_ref[...]

    def proj(xx):
        h = _rms(xx, g_ref[...]) * (1.0 + sc_ref[0]) + sh_ref[0]
        return _dot(h.astype(BF16), w) + b_ref[...]

    p = proj(x_ref[0])
    tm = p.shape[0]
    ph = proj(jnp.concatenate([xp_ref[0], xn_ref[0]], axis=0))
    prev = jnp.where(i > 0, ph[7:8], 0.0)
    nxt = jnp.where(i < nt - 1, ph[8:9], 0.0)
    row = lax.broadcasted_iota(jnp.int32, (tm, 1), 0)
    up = jnp.where(row == 0, prev, pltpu.roll(p, 1, axis=0))
    dn = jnp.where(row == tm - 1, nxt, pltpu.roll(p, tm - 1, axis=0))
    cw = cw_ref[...]
    o_ref[0, 0] = up * cw[0:1] + p * cw[1:2] + dn * cw[2:3] + cb_ref[...]


def _hy_in(x, g, sh, sc, w, b, cw, cb, *, tm):
    bsz, n, d = x.shape
    nt = n // tm
    hb = tm // 8
    per_b = pl.BlockSpec((1, 1, d), lambda j, bb, i: (bb, 0, 0))
    return pl.pallas_call(
        functools.partial(_hy_in_body, nt=nt),
        grid=(3, bsz, nt),
        in_specs=[
            pl.BlockSpec((1, tm, d), lambda j, bb, i: (bb, i, 0)),
            pl.BlockSpec((1, 8, d), lambda j, bb, i: (bb, jnp.maximum(i * hb - 1, 0), 0)),
            pl.BlockSpec((1, 8, d), lambda j, bb, i: (bb, jnp.minimum((i + 1) * hb, n // 8 - 1), 0)),
            pl.BlockSpec((1, d), lambda j, bb, i: (0, 0)),
            per_b, per_b,
            pl.BlockSpec((d, d), lambda j, bb, i: (0, j)),
            pl.BlockSpec((1, d), lambda j, bb, i: (0, j)),
            pl.BlockSpec((3, d), lambda j, bb, i: (0, j)),
            pl.BlockSpec((1, d), lambda j, bb, i: (0, j)),
        ],
        out_specs=pl.BlockSpec((1, 1, tm, d), lambda j, bb, i: (j, bb, i, 0)),
        out_shape=jax.ShapeDtypeStruct((3, bsz, n, d), F32),
        compiler_params=_cp(("arbitrary", "arbitrary", "arbitrary")),
        name="hyena_in_proj",
    )(x, x, x, g, sh, sc, w, b, cw, cb)


def _filt_feat_body(w1_ref, b1_ref, f1_ref, w2_ref, b2_ref, f2_ref, o_ref, *, n_lat):
    na = o_ref.shape[1]
    a = lax.broadcasted_iota(jnp.int32, (na, 1), 0)
    lane = lax.broadcasted_iota(jnp.int32, (na, LANES), 1)
    band_idx = jnp.where(lane <= HY_BANDS, lane - 1, lane - 1 - HY_BANDS).astype(F32)
    band = 1e-4 + band_idx * ((HY_BANDS - 1 - 1e-4) / (HY_BANDS - 1))
    for j in range(B_GROUP):
        r = a * FFT_N2 + (pl.program_id(0) * B_GROUP + j)
        pos = jnp.where(r < n_lat, r, 2 * n_lat - r).astype(F32)
        tn = pos / float(max(n_lat - 1, 1))
        ang = ((2.0 * math.pi / n_lat) * pos) * band
        z = jnp.where(lane == 0, tn, jnp.where(lane <= HY_BANDS, jnp.cos(ang),
                                               jnp.where(lane < HY_EMB, -jnp.sin(ang), 0.0)))
        h1 = jnp.sin(f1_ref[...] * (_dot_hi(z, w1_ref[...]) + b1_ref[...]))
        h2 = jnp.sin(f2_ref[...] * (_dot_hi(h1, w2_ref[...]) + b2_ref[...]))
        valid = (r != n_lat).astype(F32)
        o_ref[j] = jnp.where(lane == HY_HID, tn, jnp.where(lane == HY_HID + 1, valid, h2))


def _filt_feat(w1, b1, f1, w2, b2, f2, *, n_lat):
    na = 2 * n_lat // FFT_N2
    w1p = jnp.zeros((LANES, LANES), F32).at[:HY_EMB, :HY_HID].set(w1)
    w2p = jnp.zeros((LANES, LANES), F32).at[:HY_HID, :HY_HID].set(w2)
    padv = lambda v: jnp.zeros((1, LANES), F32).at[0, :HY_HID].set(v)
    full = lambda shp: pl.BlockSpec(shp, lambda i: (0,) * len(shp))
    return pl.pallas_call(
        functools.partial(_filt_feat_body, n_lat=n_lat),
        grid=(FFT_N2 // B_GROUP,),
        in_specs=[full((LANES, LANES)), full((1, LANES)), full((1, LANES)),
                  full((LANES, LANES)), full((1, LANES)), full((1, LANES))],
        out_specs=pl.BlockSpec((B_GROUP, na, LANES), lambda i: (i, 0, 0)),
        out_shape=jax.ShapeDtypeStruct((FFT_N2, na, LANES), F32),
        compiler_params=_cp(("arbitrary",)),
        name="hyena_filter_features",
    )(w1p, padv(b1), padv(f1), w2p, padv(b2), padv(f2))


def _filt_s1_body(hd_ref, w3_ref, dec_ref, tab_ref, o_ref):
    na = hd_ref.shape[1]
    ha = na // 2
    o2 = _rows2d(o_ref)
    for j in range(B_GROUP):
        f = hd_ref[j]
        tn = f[:, HY_HID:HY_HID + 1]
        valid = f[:, HY_HID + 1:HY_HID + 2]
        top = _dot3(f[:ha], w3_ref[0, 0]) * jnp.exp(-tn[:ha] * jnp.abs(dec_ref[0, 0]))
        bot = _dot3(f[ha:], w3_ref[0, 1]) * jnp.exp(-tn[ha:] * jnp.abs(dec_ref[0, 1])) * valid[ha:]
        hb = jnp.concatenate([top, bot], axis=0).astype(BF16)
        o2[pl.ds(j, 2 * na, stride=B_GROUP), :] = _dot(tab_ref[j], hb)


def _filt_s1(hd, w3r, dec, tab, *, ct):
    _, na, _ = hd.shape
    d = w3r.shape[-1]
    return pl.pallas_call(
        _filt_s1_body,
        grid=(2, FFT_N2 // B_GROUP, d // ct),
        in_specs=[
            pl.BlockSpec((B_GROUP, na, LANES), lambda o, g, c: (g, 0, 0)),
            pl.BlockSpec((1, 2, LANES, ct), lambda o, g, c: (o, 0, 0, c)),
            pl.BlockSpec((1, 2, 1, ct), lambda o, g, c: (o, 0, 0, c)),
            pl.BlockSpec((B_GROUP, 2 * na, na), lambda o, g, c: (g, 0, 0)),
        ],
        out_specs=pl.BlockSpec((None, 2 * na, B_GROUP, ct), lambda o, g, c: (o, 0, g, c)),
        out_shape=jax.ShapeDtypeStruct((2, 2 * na, FFT_N2, d), F32),
        compiler_params=_cp(("arbitrary", "arbitrary", "arbitrary")),
        name="hyena_filter_dft1",
    )(hd, w3r, dec, tab)


S2_KB = 4


def _s2_body(*refs, conv):
    if conv:
        o_ref, kf_ref, ff_ref, fi_ref, g_ref = refs
    else:
        o_ref, ff_ref, g_ref = refs
    for u in range(S2_KB):
        xin = jnp.concatenate([o_ref[0, u], o_ref[1, u]], axis=0).astype(BF16)
        xf = _dot(ff_ref[...], xin)
        if conv:
            xr, xi = xf[:FFT_N2], xf[FFT_N2:]
            kr = kf_ref[0, u].astype(F32)
            ki = kf_ref[1, u].astype(F32)
            y = jnp.concatenate([xr * kr - xi * ki, xr * ki + xi * kr], axis=0).astype(BF16)
            xf = _dot(fi_ref[...], y)
        g_ref[0, u] = xf[:FFT_N2].astype(g_ref.dtype)
        g_ref[1, u] = xf[FFT_N2:].astype(g_ref.dtype)


def _s2(o4, kf, order, ff, fi, *, ct, conv):
    n1 = o4.shape[-3]
    d = o4.shape[-1]
    full = lambda a: pl.BlockSpec(a.shape, lambda k, c: (0,) * a.ndim)
    nk = n1 // S2_KB
    if conv:
        blk = pl.BlockSpec((2, S2_KB, FFT_N2, ct), lambda k, c: (0, k, 0, c))
        in_specs = [blk, pl.BlockSpec((None, 2, S2_KB, FFT_N2, ct), lambda k, c: (order, 0, k, 0, c)), full(ff),
                    full(fi)]
        args = (o4, kf, ff, fi)
        grid = (nk, d // ct)
        out_specs = blk
        out_dtype = F32
    else:
        no = o4.shape[0]
        blk = pl.BlockSpec((None, 2, S2_KB, FFT_N2, ct), lambda k, c: (k // nk, 0, k % nk, 0, c))
        in_specs = [blk, full(ff)]
        args = (o4, ff)
        grid = (no * nk, d // ct)
        out_specs = blk
        out_dtype = BF16
    return pl.pallas_call(
        functools.partial(_s2_body, conv=conv),
        grid=grid,
        in_specs=in_specs,
        out_specs=out_specs,
        out_shape=jax.ShapeDtypeStruct(o4.shape, out_dtype),
        compiler_params=_cp(("arbitrary", "arbitrary")),
        name="hyena_conv_dft2" if conv else "hyena_filter_dft2",
    )(*args)


def _rows2d(ref):
    lead = ref.shape[:-3]
    return ref.reshape(lead + (ref.shape[-3] * B_GROUP, ref.shape[-1]))


def _s1_body(z_ref, tab_ref, o_ref):
    rows, n_out = z_ref.shape[0], o_ref.shape[0]
    z2, o2 = _rows2d(z_ref), _rows2d(o_ref)
    for j in range(B_GROUP):
        zj = z2[pl.ds(j, rows, stride=B_GROUP), :]
        o2[pl.ds(j, n_out, stride=B_GROUP), :] = _dot(tab_ref[j], zj.astype(BF16))


def _s1(z4, zi, tab, *, ct):
    _, rows, _, d = z4.shape
    n_out = tab.shape[1]
    return pl.pallas_call(
        _s1_body,
        grid=(FFT_N2 // B_GROUP, d // ct),
        in_specs=[
            pl.BlockSpec((None, rows, B_GROUP, ct), lambda g, c: (zi, 0, g, c)),
            pl.BlockSpec((B_GROUP, n_out, rows), lambda g, c: (g, 0, 0)),
        ],
        out_specs=pl.BlockSpec((n_out, B_GROUP, ct), lambda g, c: (0, g, c)),
        out_shape=jax.ShapeDtypeStruct((n_out, FFT_N2, d), F32),
        compiler_params=_cp(("arbitrary", "arbitrary")),
        name="hyena_conv_dft1",
    )(z4, tab)


def _s3_body(g_ref, tab_ref, gate_ref, z_ref, fb_ref, o_ref):
    n_in, rows = g_ref.shape[0], o_ref.shape[0]
    g2, o2 = _rows2d(g_ref), _rows2d(o_ref)
    for j in range(B_GROUP):
        gj = g2[pl.ds(j, n_in, stride=B_GROUP), :]
        o2[pl.ds(j, rows, stride=B_GROUP), :] = _dot(tab_ref[j], gj.astype(BF16))
    o_ref[...] = gate_ref[...] * (o_ref[...] + z_ref[...] * fb_ref[...])


def _s3(g3, tab, gate4, gi, z4, zi, fb, *, ct):
    n_in, _, d = g3.shape
    rows = tab.shape[1]
    blk = pl.BlockSpec((rows, B_GROUP, ct), lambda g, c: (0, g, c))
    return pl.pallas_call(
        _s3_body,
        grid=(FFT_N2 // B_GROUP, d // ct),
        in_specs=[
            pl.BlockSpec((n_in, B_GROUP, ct), lambda g, c: (0, g, c)),
            pl.BlockSpec((B_GROUP, rows, n_in), lambda g, c: (g, 0, 0)),
            pl.BlockSpec((None, rows, B_GROUP, ct), lambda g, c: (gi, 0, g, c)),
            pl.BlockSpec((None, rows, B_GROUP, ct), lambda g, c: (zi, 0, g, c)),
            pl.BlockSpec((1, 1, ct), lambda g, c: (0, 0, c)),
        ],
        out_specs=blk,
        out_shape=jax.ShapeDtypeStruct((rows, FFT_N2, d), F32),
        compiler_params=_cp(("arbitrary", "arbitrary")),
        name="hyena_conv_idft1",
    )(g3, tab, gate4, z4, fb)


def _dft_tables(n_lat):
    n = 2 * n_lat
    n1 = n // FFT_N2
    b = jnp.arange(FFT_N2, dtype=jnp.int32)[:, None, None]
    k1 = jnp.arange(n1, dtype=jnp.int32)[None, :, None]
    a = jnp.arange(n1, dtype=jnp.int32)[None, None, :]
    th = ((k1 * (a * FFT_N2 + b)) % n).astype(F32) * (2.0 * math.pi / n)
    cr, sn = jnp.cos(th), jnp.sin(th)
    ha = n1 // 2
    crh, snh = cr[:, :, :ha], sn[:, :, :ha]
    w1 = jnp.concatenate([jnp.concatenate([crh, snh], axis=2), jnp.concatenate([-snh, crh], axis=2)], axis=1)
    w1f = jnp.concatenate([cr, -sn], axis=1)
    v = jnp.swapaxes(w1, 1, 2) * (1.0 / n)
    k2 = jnp.arange(FFT_N2, dtype=jnp.int32)
    th2 = ((k2[:, None] * k2[None, :]) % FFT_N2).astype(F32) * (2.0 * math.pi / FFT_N2)
    c2, s2 = jnp.cos(th2), jnp.sin(th2)
    ff = jnp.concatenate([jnp.concatenate([c2, s2], axis=1), jnp.concatenate([-s2, c2], axis=1)], axis=0)
    fi = jnp.concatenate([jnp.concatenate([c2, -s2], axis=1), jnp.concatenate([s2, c2], axis=1)], axis=0)
    return w1.astype(BF16), w1f.astype(BF16), v.astype(BF16), ff.astype(BF16), fi.astype(BF16)


def _hyena_mix(proj3, fparams, fbias, *, n_lat):
    _, bsz, _, d = proj3.shape
    f_w1, f_b1, f_f1, f_w2, f_b2, f_f2, f_w3, decay = fparams
    na = 2 * n_lat // FFT_N2
    w1, w1f, v, ff, fi = _dft_tables(n_lat)
    ct = LANES
    hd = _filt_feat(f_w1, f_b1, f_f1, f_w2, f_b2, f_f2, n_lat=n_lat)
    w3r = jnp.transpose(f_w3.reshape(HY_HID, 2, 2, d), (1, 2, 0, 3))
    w3r = jnp.zeros((2, 2, LANES, d), F32).at[:, :, :HY_HID].set(w3r)
    kf1 = _filt_s1(hd, w3r, decay.reshape(2, 2, 1, d), w1f, ct=ct)
    kf = _s2(kf1.reshape(2, 2, na, FFT_N2, d), None, 0, ff, None, ct=d, conv=False)
    p3 = proj3.reshape(3, bsz * (n_lat // FFT_N2), FFT_N2, d)
    z4, zi = p3, 2
    for o in range(2):
        o1 = _s1(z4, zi, w1, ct=ct)
        g = _s2(o1.reshape(2, na, FFT_N2, d), kf, o, ff, fi, ct=d, conv=True)
        z4 = _s3(g.reshape(2 * na, FFT_N2, d), v, p3, o, z4, zi, fbias[o].reshape(1, 1, d), ct=ct)[None]
        zi = 0
    return z4.reshape(bsz, n_lat, d)


def _rope_tables(n_tokens):
    rows = n_tokens // GRID_W
    row = jnp.broadcast_to(jnp.arange(rows, dtype=F32)[:, None], (rows, GRID_W)).reshape(-1)
    col = jnp.broadcast_to(jnp.arange(GRID_W, dtype=F32)[None, :], (rows, GRID_W)).reshape(-1)
    axis_dim = QK_ROPE // 2
    inv_freq = 1.0 / (ROPE_THETA ** (jnp.arange(0, axis_dim, 2, dtype=F32) / axis_dim))
    ang = jnp.concatenate([row[:, None] * inv_freq, col[:, None] * inv_freq], axis=-1)
    return jnp.cos(ang), jnp.sin(ang)


def _mla_weights(w_down, g_q, w_uq, g_kv, w_ukv):
    d = w_down.shape[0]
    nh = MLA_HEADS
    kpe = w_down[:, Q_LORA + KV_LORA:]
    w1, w2 = kpe[:, 0::2], kpe[:, 1::2]
    z = jnp.zeros((d, LANES - QK_ROPE), w_down.dtype)
    wd = jnp.concatenate([w_down[:, :Q_LORA + KV_LORA], w1, w2, z, w2, w1, z], axis=1).astype(BF16)
    uq = w_uq.reshape(Q_LORA, nh, QK_NOPE + QK_ROPE)
    pe = uq[:, :, QK_NOPE:]
    uq = jnp.concatenate([uq[:, :, :QK_NOPE], pe[:, :, 0::2], pe[:, :, 1::2]], axis=2)
    wuqT = uq.reshape(Q_LORA, nh * (QK_NOPE + QK_ROPE)).T.astype(BF16)
    ukv = w_ukv.reshape(KV_LORA, nh, QK_NOPE + V_DIM)
    wuk = ukv[:, :, :QK_NOPE].reshape(KV_LORA, nh * QK_NOPE).astype(BF16)
    wuvT = ukv[:, :, QK_NOPE:].reshape(KV_LORA, nh * V_DIM).T.astype(BF16)
    return wd, g_q.reshape(1, -1), g_kv.reshape(1, -1), wuk, wuqT, wuvT


def kernel(x, c, ctx, c_ctx, ada_w, ada_b, norm_mix_g, norm_ffn_g, mla_w_down, mla_g_q, mla_w_uq, mla_g_kv, mla_w_ukv, mla_w_o, hy_w_in, hy_b_in, hy_conv_w, hy_conv_b, hy_f_w1, hy_f_b1, hy_f_freq1, hy_f_w2, hy_f_b2, hy_f_freq2, hy_f_w3, hy_decay, hy_bias, hy_w_out, hy_b_out, moe_w_r, moe_b_r, moe_w_in, moe_b_in, moe_w_out, moe_b_out, final_g):
    bsz, n_lat, d = x.shape
    n_ctx = ctx.shape[1]
    depth = ada_w.shape[0]
    assert bsz == 2 and d == MLA_HEADS * V_DIM and n_lat % 512 == 0 and n_ctx % 128 == 0
    assert depth == 2

    cond8 = jnp.zeros((8, d), F32).at[:bsz].set(c).at[bsz].set(c_ctx)
    mods = _ada(cond8, ada_w, ada_b)

    def mod(i, j, rows):
        return mods[i, rows, j * d:(j + 1) * d][:, None, :]

    lat_rows = slice(0, bsz)
    ctx_rows = slice(bsz, bsz + 1)
    xl = x
    for i in range(depth):
        kind, j = i % 2, i // 2
        sh1, sc1, g1 = (mod(i, m, lat_rows) for m in range(3))
        sh2, sc2, g2 = (mod(i, m, lat_rows) for m in range(3, 6))
        gm = norm_mix_g[i].reshape(1, d)
        if kind == 0:
            wts = _mla_weights(mla_w_down[j], mla_g_q[j], mla_w_uq[j], mla_g_kv[j], mla_w_ukv[j])
            cos, sin = _rope_tables(n_lat)
            zl = jnp.zeros((n_lat, LANES - QK_ROPE), F32)
            tabs = (jnp.concatenate([cos, cos, zl], axis=1), jnp.concatenate([-sin, sin, zl], axis=1), cos.T, sin.T)
            tq = tv = 512
            tk = 1024 if n_lat % 2048 == 0 else 512
            qT, k, vT = _mla_proj(xl, gm, sh1, sc1, wts, tabs, need_q=True, tm=tv, tk=tv)
            half = QK_ROPE // 2
            one_c = jnp.concatenate([jnp.ones((n_ctx, QK_ROPE), F32), jnp.zeros((n_ctx, LANES - QK_ROPE), F32)], axis=1)
            tabs_c = (one_c, jnp.zeros((n_ctx, LANES), F32), jnp.ones((half, n_ctx), F32), jnp.zeros((half, n_ctx), F32))
            kc, vTc = _mla_proj(ctx, gm, mod(i, 0, ctx_rows), mod(i, 1, ctx_rows), wts, tabs_c,
                                need_q=False, tm=n_ctx, tk=n_ctx)
            o = _attention(qT, k, vT, kc, vTc, tq=tq, tk=tk)
            wo = mla_w_o[j].astype(BF16)
            bo = jnp.zeros((1, d), F32)
            transposed = True
        else:
            proj3 = _hy_in(xl, gm, sh1, sc1, hy_w_in[j].astype(BF16), hy_b_in[j].reshape(1, -1), hy_conv_w[j],
                           hy_conv_b[j].reshape(1, -1), tm=512)
            fparams = (hy_f_w1[j], hy_f_b1[j], hy_f_freq1[j], hy_f_w2[j], hy_f_b2[j], hy_f_freq2[j], hy_f_w3[j],
                       hy_decay[j])
            o = _hyena_mix(proj3, fparams, hy_bias[j], n_lat=n_lat)
            wo = hy_w_out[j].astype(BF16)
            bo = hy_b_out[j].reshape(1, d)
            transposed = False
        wr = jnp.zeros((d, LANES), F32).at[:, :N_EXPERTS].set(moe_w_r[i])
        wrh = wr.astype(BF16)
        wrl = (wr - wrh.astype(F32)).astype(BF16)
        br = jnp.zeros((1, LANES), F32).at[0, :N_EXPERTS].set(moe_b_r[i])
        xl, fl, topi, gates, rank, cnt = _post(o, wo, bo, xl, g1, norm_ffn_g[i].reshape(1, d), sh2, sc2, wrh, wrl, br,
                                               transposed=transposed, tm=512)
        xl = _moe(fl, topi, gates, rank, cnt, xl, g2, final_g.reshape(1, d), i, moe_w_in, moe_b_in,
                  moe_w_out, moe_b_out, final=(i == depth - 1))
    return xl
```

```python
import functools
import math

import jax
import jax.numpy as jnp
from jax import lax
from jax.experimental import pallas as pl
from jax.experimental.pallas import tpu as pltpu

F32 = jnp.float32
BF16 = jnp.bfloat16

EPS = 1e-6
GRID_W = 64
MLA_HEADS = 8
QK_NOPE = 128
QK_ROPE = 64
V_DIM = 128
Q_LORA = 512
KV_LORA = 256
ROPE_THETA = 10000.0
MLA_SCALE = (QK_NOPE + QK_ROPE) ** -0.5
QK_PAD = 256

HY_EMB = 33
HY_BANDS = (HY_EMB - 1) // 2
HY_HID = 64
FFT_N2 = 128
B_GROUP = 8

N_EXPERTS = 32
TOP_K = 4
SWIGLU_LIMIT = 7.0
SWIGLU_ALPHA = 1.702
MOE_TM = 256
LANES = 128

VMEM_LIMIT = 56 * 1024 * 1024


def _cp(sem, vmem=VMEM_LIMIT):
    return pltpu.CompilerParams(dimension_semantics=sem, vmem_limit_bytes=vmem)


def _dot(a, b):
    return jnp.dot(a, b, preferred_element_type=F32)


def _dot_hi(a, b):
    return jnp.dot(a, b, preferred_element_type=F32, precision=lax.Precision.HIGHEST)


def _dot3(a, b):
    ah = a.astype(BF16)
    al = (a - ah.astype(F32)).astype(BF16)
    bh = b.astype(BF16)
    bl = (b - bh.astype(F32)).astype(BF16)
    return _dot(ah, bh) + (_dot(al, bh) + _dot(ah, bl))


def _rms(x, g):
    return x * lax.rsqrt(jnp.mean(x * x, axis=-1, keepdims=True) + EPS) * g


def _ada_body(c_ref, w_ref, b_ref, o_ref):
    c = c_ref[...]
    s = c * jax.nn.sigmoid(c)
    o_ref[0] = _dot(s.astype(BF16), w_ref[0].astype(BF16)) + b_ref[0]


def _ada(cond8, ada_w, ada_b):
    depth, d, n = ada_w.shape
    tn = n // 4
    return pl.pallas_call(
        _ada_body,
        grid=(depth, n // tn),
        in_specs=[
            pl.BlockSpec((8, d), lambda i, j: (0, 0)),
            pl.BlockSpec((1, d, tn), lambda i, j: (i, 0, j)),
            pl.BlockSpec((1, 1, tn), lambda i, j: (i, 0, j)),
        ],
        out_specs=pl.BlockSpec((1, 8, tn), lambda i, j: (i, 0, j)),
        out_shape=jax.ShapeDtypeStruct((depth, 8, n), F32),
        compiler_params=_cp(("arbitrary", "arbitrary")),
        name="ada_mod",
    )(cond8, ada_w, ada_b.reshape(depth, 1, n))


def _mla_proj_body(x_ref, g_ref, sh_ref, sc_ref, wd_ref, gq_ref, gkv_ref, wuk_ref, wuqT_ref, wuvT_ref,
                   ct_ref, st_ref, cT_ref, sT_ref, *out_refs, need_q, tk):
    if need_q:
        qT_ref, k_ref, vT_ref = out_refs
    else:
        k_ref, vT_ref = out_refs
    nh = MLA_HEADS
    x = x_ref[0]
    h = _rms(x, g_ref[...]) * (1.0 + sc_ref[0]) + sh_ref[0]
    lat = _dot(h.astype(BF16), wd_ref[...])
    o_kv = Q_LORA
    o_a = Q_LORA + KV_LORA
    kvn = _rms(lat[:, o_kv:o_a], gkv_ref[...])
    kr = (lat[:, o_a:o_a + LANES] * ct_ref[...] + lat[:, o_a + LANES:o_a + 2 * LANES] * st_ref[...]).astype(BF16)
    knope = _dot(kvn.astype(BF16), wuk_ref[...])
    for hh in range(nh):
        k_ref[0, hh, :, 0:QK_NOPE] = knope[:, hh * QK_NOPE:(hh + 1) * QK_NOPE].astype(BF16)
        k_ref[0, hh, :, QK_NOPE:QK_PAD] = kr
    vT = _dot(wuvT_ref[...], kvn.T.astype(BF16))
    tm = x.shape[0]
    for hh in range(nh):
        for c in range(tm // tk):
            vT_ref[0, hh, c] = vT[hh * V_DIM:(hh + 1) * V_DIM, c * tk:(c + 1) * tk].astype(BF16)
    if need_q:
        qn = _rms(lat[:, :Q_LORA], gq_ref[...])
        qT = _dot(wuqT_ref[...], qn.T.astype(BF16)) * (MLA_SCALE * math.log2(math.e))
        c = cT_ref[...]
        s = sT_ref[...]
        hw = QK_NOPE + QK_ROPE
        half = QK_ROPE // 2
        for hh in range(nh):
            base = hh * hw
            x1 = qT[base + QK_NOPE:base + QK_NOPE + half]
            x2 = qT[base + QK_NOPE + half:base + hw]
            qT_ref[0, hh, 0:QK_NOPE] = qT[base:base + QK_NOPE].astype(BF16)
            qT_ref[0, hh, QK_NOPE:QK_NOPE + half] = (x1 * c - x2 * s).astype(BF16)
            qT_ref[0, hh, QK_NOPE + half:hw] = (x1 * s + x2 * c).astype(BF16)
            qT_ref[0, hh, hw:QK_PAD] = jnp.zeros((QK_PAD - hw, tm), BF16)


def _mla_proj(x, g, sh, sc, wts, tabs, *, need_q, tm, tk):
    bsz, n, d = x.shape
    nh = MLA_HEADS
    wd, gq, gkv, wuk, wuqT, wuvT = wts
    ct, st, cT, sT = tabs
    nsh = sh.shape[0]
    full = lambda a: pl.BlockSpec(a.shape, lambda b, i: (0,) * a.ndim)
    in_specs = [
        pl.BlockSpec((1, tm, d), lambda b, i: (b, i, 0)),
        full(g),
        pl.BlockSpec((1, 1, d), lambda b, i: (b % nsh, 0, 0)),
        pl.BlockSpec((1, 1, d), lambda b, i: (b % nsh, 0, 0)),
        full(wd), full(gq), full(gkv), full(wuk), full(wuqT), full(wuvT),
        pl.BlockSpec((tm, LANES), lambda b, i: (i, 0)),
        pl.BlockSpec((tm, LANES), lambda b, i: (i, 0)),
        pl.BlockSpec((QK_ROPE // 2, tm), lambda b, i: (0, i)),
        pl.BlockSpec((QK_ROPE // 2, tm), lambda b, i: (0, i)),
    ]
    out_specs = [
        pl.BlockSpec((1, nh, tm, QK_PAD), lambda b, i: (b, 0, i, 0)),
        pl.BlockSpec((1, nh, tm // tk, V_DIM, tk), lambda b, i: (b, 0, i, 0, 0)),
    ]
    out_shape = [
        jax.ShapeDtypeStruct((bsz, nh, n, QK_PAD), BF16),
        jax.ShapeDtypeStruct((bsz, nh, n // tk, V_DIM, tk), BF16),
    ]
    if need_q:
        out_specs = [pl.BlockSpec((1, nh, QK_PAD, tm), lambda b, i: (b, 0, 0, i))] + out_specs
        out_shape = [jax.ShapeDtypeStruct((bsz, nh, QK_PAD, n), BF16)] + out_shape
    return pl.pallas_call(
        functools.partial(_mla_proj_body, need_q=need_q, tk=tk),
        grid=(bsz, n // tm),
        in_specs=in_specs,
        out_specs=out_specs,
        out_shape=out_shape,
        compiler_params=_cp(("arbitrary", "arbitrary")),
        name="mla_proj_q" if need_q else "mla_proj_ctx",
    )(x, g, sh, sc, wd, gq, gkv, wuk, wuqT, wuvT, ct, st, cT, sT)


SM_STRIP = 64
SUBLANES = 8


def _attn_body(qT_ref, k_ref, vT_ref, kc_ref, vTc_ref, o_ref, s0, s1, p0, p1, sc, pc, acc, m_scr, x0, x1, xc,
               a0, a1, ac, l_scr, d0, d1, dc, *, tk):
    nchunk = k_ref.shape[2] // tk

    def scores(kblk, s_ref, mx_ref):
        r = _dot(kblk, qT_ref[0, 0])
        s_ref[...] = r
        mx_ref[...] = jnp.max(r, axis=0, keepdims=True)

    def probs(s_ref, mx_ref, p_ref, a_ref, d_ref):
        m_old = m_scr[...]
        m_new = jnp.maximum(m_old, mx_ref[...])
        m_scr[...] = m_new
        alpha = jnp.exp2(m_old - m_new)
        a_ref[...] = alpha
        part = None
        for r in range(0, s_ref.shape[0], SM_STRIP):
            p = jnp.exp2(s_ref[r:r + SM_STRIP] - m_new)
            p_ref[r:r + SM_STRIP] = p.astype(BF16)
            ps = jnp.sum(p.reshape(SM_STRIP // SUBLANES, SUBLANES, p.shape[1]), axis=0)
            part = ps if part is None else part + ps
        d_ref[...] = part

    def accumulate(p_ref, a_ref, d_ref, vblk):
        acc[...] = a_ref[...] * acc[...] + _dot(vblk, p_ref[...])
        l_scr[...] = a_ref[...] * l_scr[...] + d_ref[...]

    def kchunk(i):
        return k_ref[0, 0, pl.ds(pl.multiple_of(i * tk, tk), tk), :]

    def vchunk(i):
        nsub = tk // vT_ref.shape[-1]
        return jnp.concatenate([vT_ref[0, 0, i * nsub + u] for u in range(nsub)], axis=1)

    m_scr[...] = jnp.full(m_scr.shape, -jnp.inf, F32)
    acc[...] = jnp.zeros(acc.shape, F32)
    l_scr[...] = jnp.zeros(l_scr.shape, F32)
    scores(kc_ref[0, 0], sc, xc)
    scores(kchunk(0), s0, x0)
    probs(sc, xc, pc, ac, dc)
    scores(kchunk(1), s1, x1)
    accumulate(pc, ac, dc, vTc_ref[0, 0, 0])
    probs(s0, x0, p0, a0, d0)

    def body(j, carry):
        t = 2 * j
        scores(kchunk(t + 2), s0, x0)
        accumulate(p0, a0, d0, vchunk(t))
        probs(s1, x1, p1, a1, d1)
        scores(kchunk(t + 3), s1, x1)
        accumulate(p1, a1, d1, vchunk(t + 1))
        probs(s0, x0, p0, a0, d0)
        return carry

    lax.fori_loop(0, nchunk // 2 - 1, body, 0)
    accumulate(p0, a0, d0, vchunk(nchunk - 2))
    probs(s1, x1, p1, a1, d1)
    accumulate(p1, a1, d1, vchunk(nchunk - 1))
    o_ref[0, 0] = (acc[...] / jnp.sum(l_scr[...], axis=0, keepdims=True)).astype(BF16)


def _attention(qT, k, vT, kc, vTc, *, tq, tk):
    bsz, nh, _, n = qT.shape
    nc = kc.shape[2]
    tv = vT.shape[-1]
    assert (n // tk) % 2 == 0 and tk % tv == 0
    return pl.pallas_call(
        functools.partial(_attn_body, tk=tk),
        grid=(bsz, nh, n // tq),
        in_specs=[
            pl.BlockSpec((1, 1, QK_PAD, tq), lambda b, h, i: (b, h, 0, i)),
            pl.BlockSpec((1, 1, n, QK_PAD), lambda b, h, i: (b, h, 0, 0)),
            pl.BlockSpec((1, 1, n // tv, V_DIM, tv), lambda b, h, i: (b, h, 0, 0, 0)),
            pl.BlockSpec((1, 1, nc, QK_PAD), lambda b, h, i: (b, h, 0, 0)),
            pl.BlockSpec((1, 1, 1, V_DIM, nc), lambda b, h, i: (b, h, 0, 0, 0)),
        ],
        out_specs=pl.BlockSpec((1, 1, V_DIM, tq), lambda b, h, i: (b, h, 0, i)),
        out_shape=jax.ShapeDtypeStruct((bsz, nh, V_DIM, n), BF16),
        scratch_shapes=[pltpu.VMEM((tk, tq), F32), pltpu.VMEM((tk, tq), F32),
                        pltpu.VMEM((tk, tq), BF16), pltpu.VMEM((tk, tq), BF16),
                        pltpu.VMEM((nc, tq), F32), pltpu.VMEM((nc, tq), BF16),
                        pltpu.VMEM((V_DIM, tq), F32), pltpu.VMEM((1, tq), F32),
                        pltpu.VMEM((1, tq), F32), pltpu.VMEM((1, tq), F32), pltpu.VMEM((1, tq), F32),
                        pltpu.VMEM((1, tq), F32), pltpu.VMEM((1, tq), F32), pltpu.VMEM((1, tq), F32),
                        pltpu.VMEM((SUBLANES, tq), F32), pltpu.VMEM((SUBLANES, tq), F32),
                        pltpu.VMEM((SUBLANES, tq), F32), pltpu.VMEM((SUBLANES, tq), F32)],
        compiler_params=_cp(("arbitrary", "arbitrary", "arbitrary")),
        name="mla_attention",
    )(qT, k, vT, kc, vTc)


def _post_body(o_ref, wo_ref, bo_ref, x_ref, g1_ref, gf_ref, sh_ref, sc_ref, wrh_ref, wrl_ref, br_ref, tri_ref,
               xl_ref, fl_ref, ti_ref, gt_ref, rk_ref, cnt_ref, *, transposed):
    @pl.when((pl.program_id(0) == 0) & (pl.program_id(1) == 0))
    def _():
        cnt_ref[...] = jnp.zeros_like(cnt_ref)

    tm = x_ref.shape[1]
    if transposed:
        oT = o_ref[0].astype(F32).reshape(MLA_HEADS * V_DIM, tm)
        o = oT.T.astype(BF16)
    else:
        o = o_ref[0].astype(BF16)
    y = _dot(o, wo_ref[...]) + bo_ref[...]
    xl = x_ref[0] + g1_ref[0] * y
    xl_ref[0] = xl
    fl = _rms(xl, gf_ref[...]) * (1.0 + sc_ref[0]) + sh_ref[0]
    _to_rows(fl_ref, fl)
    flh = fl.astype(BF16)
    fll = (fl - flh.astype(F32)).astype(BF16)
    logits = _dot(flh, wrh_ref[...]) + (_dot(fll, wrh_ref[...]) + _dot(flh, wrl_ref[...])) + br_ref[...]
    lane = lax.broadcasted_iota(jnp.int32, (tm, LANES), 1).astype(F32)
    neg = jnp.float32(-jnp.inf)
    work = jnp.where(lane < N_EXPERTS, logits, neg)
    vals, idxs = [], []
    onehot = jnp.zeros((tm, LANES), F32)
    for _ in range(TOP_K):
        mk = jnp.max(work, axis=-1, keepdims=True)
        ik = jnp.min(jnp.where(work == mk, lane, float(LANES)), axis=-1, keepdims=True)
        sel = lane == ik
        onehot = jnp.where(sel, 1.0, onehot)
        work = jnp.where(sel, neg, work)
        vals.append(mk)
        idxs.append(ik)
    es = [jnp.exp(v - vals[0]) for v in vals]
    den = es[0] + es[1] + es[2] + es[3]
    pre = _dot(tri_ref[...], onehot.astype(BF16)) + cnt_ref[...]
    ti = jnp.zeros((tm, LANES), F32)
    gt = jnp.zeros((tm, LANES), F32)
    rk = jnp.zeros((tm, LANES), F32)
    for kk in range(TOP_K):
        rank = jnp.sum(jnp.where(lane == idxs[kk], pre, 0.0), axis=-1, keepdims=True)
        ti = jnp.where(lane == kk, idxs[kk], ti)
        gt = jnp.where(lane == kk, es[kk] / den, gt)
        rk = jnp.where(lane == kk, rank, rk)
    ti_ref[...] = ti[:, :TOP_K].astype(jnp.int32)
    gt_ref[...] = gt[:, :TOP_K]
    rk_ref[...] = rk[:, :TOP_K].astype(jnp.int32)
    cnt_ref[...] += jnp.sum(onehot, axis=0, keepdims=True)


def _post(o, wo, bo, x, g1, gf, sh, sc, wrh, wrl, br, *, transposed, tm):
    bsz, n, d = x.shape
    t = bsz * n
    nt = n // tm
    tri = (lax.broadcasted_iota(jnp.int32, (tm, tm), 0) > lax.broadcasted_iota(jnp.int32, (tm, tm), 1)).astype(BF16)
    full = lambda a: pl.BlockSpec(a.shape, lambda b, i: (0,) * a.ndim)
    per_b = pl.BlockSpec((1, 1, d), lambda b, i: (b, 0, 0))
    if transposed:
        o_spec = pl.BlockSpec((1, MLA_HEADS, V_DIM, tm), lambda b, i: (b, 0, 0, i))
    else:
        o_spec = pl.BlockSpec((1, tm, d), lambda b, i: (b, i, 0))
    tok = lambda w: pl.BlockSpec((tm, w), lambda b, i: (b * nt + i, 0))
    return pl.pallas_call(
        functools.partial(_post_body, transposed=transposed),
        grid=(bsz, nt),
        in_specs=[o_spec, full(wo), full(bo), pl.BlockSpec((1, tm, d), lambda b, i: (b, i, 0)), per_b, full(gf),
                  per_b, per_b, full(wrh), full(wrl), full(br), full(tri)],
        out_specs=[pl.BlockSpec((1, tm, d), lambda b, i: (b, i, 0)),
                   pl.BlockSpec((tm * ROW_SUB, LANES), lambda b, i: (b * nt + i, 0)),
                   tok(TOP_K), tok(TOP_K), tok(TOP_K), pl.BlockSpec((1, LANES), lambda b, i: (0, 0))],
        out_shape=[jax.ShapeDtypeStruct((bsz, n, d), F32), jax.ShapeDtypeStruct((t * ROW_SUB, LANES), F32),
                   jax.ShapeDtypeStruct((t, TOP_K), jnp.int32), jax.ShapeDtypeStruct((t, TOP_K), F32),
                   jax.ShapeDtypeStruct((t, TOP_K), jnp.int32), jax.ShapeDtypeStruct((1, LANES), F32)],
        compiler_params=_cp(("arbitrary", "arbitrary")),
        name="post_attn" if transposed else "post_hyena",
    )(o, wo, bo, x, g1, gf, sh, sc, wrh, wrl, br, tri)


ROW_SUB = 8


def _row_slice(i):
    return pl.ds(pl.multiple_of(i * ROW_SUB, ROW_SUB), ROW_SUB)


def _to_rows(ref, x):
    for s in range(ROW_SUB):
        ref[pl.ds(s, x.shape[0], stride=ROW_SUB), :] = x[:, s * LANES:(s + 1) * LANES]


def _from_rows(ref, lo, hi):
    return jnp.concatenate([ref[pl.ds(lo * ROW_SUB + s, hi - lo, stride=ROW_SUB), :] for s in range(ROW_SUB)], axis=1)


def _dispatch_body(pe_ref, pd_ref, dest_ref, fl_ref, xs_out, zbuf, sem, *, td):
    @pl.when(pl.program_id(0) == 0)
    def _():
        zbuf[...] = jnp.zeros(zbuf.shape, zbuf.dtype)
        for e in range(N_EXPERTS):
            @pl.when(pd_ref[e] > 0)
            def _():
                start = pl.multiple_of((pe_ref[e] - MOE_TM) * ROW_SUB, ROW_SUB)
                cp = pltpu.make_async_copy(zbuf, xs_out.at[pl.ds(start, MOE_TM * ROW_SUB)], sem)
                cp.start()
                cp.wait()

    def issue(t, carry):
        for kk in range(TOP_K):
            d = dest_ref[0, 0, t * TOP_K + kk]
            pltpu.make_async_copy(fl_ref.at[_row_slice(t)], xs_out.at[_row_slice(d)], sem).start(priority=kk % 2)
        return carry

    lax.fori_loop(0, td, issue, 0, unroll=2)

    def drain(t, carry):
        pltpu.make_async_copy(fl_ref.at[_row_slice(0)], xs_out.at[_row_slice(0)], sem).wait()
        return carry

    lax.fori_loop(0, td * TOP_K, drain, 0, unroll=8)


def _dispatch(pad_end, padded, dest, fl, n_rows, *, td):
    t = fl.shape[0] // ROW_SUB
    dest3 = dest.reshape(t // td, 1, td * TOP_K)
    grid_spec = pltpu.PrefetchScalarGridSpec(
        num_scalar_prefetch=2,
        grid=(t // td,),
        in_specs=[
            pl.BlockSpec((1, 1, td * TOP_K), lambda i, pe, pd: (i, 0, 0), memory_space=pltpu.SMEM),
            pl.BlockSpec((td * ROW_SUB, LANES), lambda i, pe, pd: (i, 0)),
        ],
        out_specs=pl.BlockSpec(memory_space=pl.ANY),
        scratch_shapes=[pltpu.VMEM((MOE_TM * ROW_SUB, LANES), fl.dtype), pltpu.SemaphoreType.DMA(())],
    )
    return pl.pallas_call(
        functools.partial(_dispatch_body, td=td),
        grid_spec=grid_spec,
        out_shape=jax.ShapeDtypeStruct((n_rows * ROW_SUB, LANES), fl.dtype),
        compiler_params=_cp(("arbitrary",)),
        name="moe_dispatch",
    )(pad_end, padded, dest3, fl)


def _expert_body(be_ref, nu_ref, xs_ref, win_ref, bin_ref, wout_ref, bout_ref, ys_ref, win_s, wout_s):
    b = pl.program_id(0)
    dff = wout_ref.shape[1]

    @pl.when(b < nu_ref[0])
    def _():
        prev = be_ref[jnp.maximum(b - 1, 0)]

        @pl.when((b == 0) | (prev != be_ref[b]))
        def _():
            win_s[...] = win_ref[0].astype(BF16)
            wout_s[...] = wout_ref[0].astype(BF16)

        x = _from_rows(xs_ref, 0, xs_ref.shape[0] // ROW_SUB).astype(BF16)
        gu = _dot(x, win_s[...]) + bin_ref[0]
        gate = jnp.minimum(gu[:, :dff], SWIGLU_LIMIT)
        lin = jnp.clip(gu[:, dff:], -SWIGLU_LIMIT, SWIGLU_LIMIT)
        act = gate * jax.nn.sigmoid(SWIGLU_ALPHA * gate) * (lin + 1.0)
        _to_rows(ys_ref, _dot(act.astype(BF16), wout_s[...]) + bout_ref[0])

    @pl.when(b >= nu_ref[0])
    def _():
        ys_ref[...] = jnp.zeros_like(ys_ref)


def _experts(blk_exp, n_used, xs, layer, w_in, b_in, w_out, b_out):
    n_rows = xs.shape[0] // ROW_SUB
    depth, ne, d, f2 = w_in.shape
    dff = w_out.shape[2]
    tm = MOE_TM
    grid_spec = pltpu.PrefetchScalarGridSpec(
        num_scalar_prefetch=2,
        grid=(n_rows // tm,),
        in_specs=[
            pl.BlockSpec((tm * ROW_SUB, LANES), lambda b, be, nu: (jnp.minimum(b, nu[0] - 1), 0)),
            pl.BlockSpec((None, 1, d, f2), lambda b, be, nu: (layer, be[b], 0, 0)),
            pl.BlockSpec((None, 1, 1, f2), lambda b, be, nu: (layer, be[b], 0, 0)),
            pl.BlockSpec((None, 1, dff, d), lambda b, be, nu: (layer, be[b], 0, 0)),
            pl.BlockSpec((None, 1, 1, d), lambda b, be, nu: (layer, be[b], 0, 0)),
        ],
        out_specs=pl.BlockSpec((tm * ROW_SUB, LANES), lambda b, be, nu: (b, 0)),
        scratch_shapes=[pltpu.VMEM((d, f2), BF16), pltpu.VMEM((dff, d), BF16)],
    )
    return pl.pallas_call(
        _expert_body,
        grid_spec=grid_spec,
        out_shape=jax.ShapeDtypeStruct(xs.shape, F32),
        compiler_params=_cp(("arbitrary",)),
        name="moe_experts",
    )(blk_exp, n_used, xs, w_in, b_in.reshape(depth, ne, 1, f2), w_out, b_out.reshape(depth, ne, 1, d))


def _combine_body(dest_ref, ys_hbm, gt_ref, xl_ref, g2_ref, fg_ref, out_ref, buf, sem, *, tc, final):
    def issue(t, carry):
        for kk in range(TOP_K):
            d = dest_ref[0, 0, t * TOP_K + kk]
            pltpu.make_async_copy(ys_hbm.at[_row_slice(d)], buf.at[_row_slice(kk * tc + t)], sem).start(
                priority=kk % 2)
        return carry

    lax.fori_loop(0, tc, issue, 0, unroll=2)

    def drain(t, carry):
        pltpu.make_async_copy(ys_hbm.at[_row_slice(0)], buf.at[_row_slice(0)], sem).wait()
        return carry

    lax.fori_loop(0, tc * TOP_K, drain, 0, unroll=8)
    gt = gt_ref[...]
    y = gt[:, 0:1] * _from_rows(buf, 0, tc)
    for kk in range(1, TOP_K):
        y = y + gt[:, kk:kk + 1] * _from_rows(buf, kk * tc, (kk + 1) * tc)
    xl = xl_ref[0] + g2_ref[0] * y
    out_ref[0] = _rms(xl, fg_ref[...]) if final else xl


def _combine(dest, ys, gates, xl, g2, fg, *, tc, final):
    bsz, n, d = xl.shape
    t = bsz * n
    nt = n // tc
    dest3 = dest.reshape(t // tc, 1, tc * TOP_K)
    return pl.pallas_call(
        functools.partial(_combine_body, tc=tc, final=final),
        grid=(bsz, nt),
        in_specs=[
            pl.BlockSpec((1, 1, tc * TOP_K), lambda b, i: (b * nt + i, 0, 0), memory_space=pltpu.SMEM),
            pl.BlockSpec(memory_space=pl.ANY),
            pl.BlockSpec((tc, TOP_K), lambda b, i: (b * nt + i, 0)),
            pl.BlockSpec((1, tc, d), lambda b, i: (b, i, 0)),
            pl.BlockSpec((1, 1, d), lambda b, i: (b, 0, 0)),
            pl.BlockSpec((1, d), lambda b, i: (0, 0)),
        ],
        out_specs=pl.BlockSpec((1, tc, d), lambda b, i: (b, i, 0)),
        out_shape=jax.ShapeDtypeStruct((bsz, n, d), F32),
        scratch_shapes=[pltpu.VMEM((TOP_K * tc * ROW_SUB, LANES), F32), pltpu.SemaphoreType.DMA(())],
        compiler_params=_cp(("arbitrary", "arbitrary")),
        name="moe_combine",
    )(dest3, ys, gates, xl, g2, fg)


def _moe(fl, topi, gates, rank, cnt, xl, g2, fg, layer, w_in, b_in, w_out, b_out, *, final):
    t = fl.shape[0] // ROW_SUB
    tm = MOE_TM
    counts = cnt[0, :N_EXPERTS].astype(jnp.int32)
    padded = (counts + tm - 1) // tm * tm
    pad_end = jnp.cumsum(padded)
    pad_start = pad_end - padded
    dest = jnp.take(pad_start, topi) + rank
    nb = t * TOP_K // tm + N_EXPERTS
    blk_start = jnp.arange(nb, dtype=jnp.int32) * tm
    blk_exp = jnp.minimum(jnp.sum((pad_end[None, :] <= blk_start[:, None]).astype(jnp.int32), axis=1), N_EXPERTS - 1)
    n_used = (pad_end[-1:] // tm).astype(jnp.int32)
    xs = _dispatch(pad_end, padded, dest, fl, nb * tm, td=256)
    ys = _experts(blk_exp, n_used, xs, layer, w_in, b_in, w_out, b_out)
    return _combine(dest, ys, gates, xl, g2, fg, tc=256, final=final)


def _hy_in_body(x_ref, xp_ref, xn_ref, g_ref, sh_ref, sc_ref, w_ref, b_ref, cw_ref, cb_ref, o_ref, *, nt):
    i = pl.program_id(2)
    w = w_ref[...]

    def proj(xx):
        h = _rms(xx, g_ref[...]) * (1.0 + sc_ref[0]) + sh_ref[0]
        return _dot(h.astype(BF16), w) + b_ref[...]

    p = proj(x_ref[0])
    tm = p.shape[0]
    ph = proj(jnp.concatenate([xp_ref[0], xn_ref[0]], axis=0))
    prev = jnp.where(i > 0, ph[7:8], 0.0)
    nxt = jnp.where(i < nt - 1, ph[8:9], 0.0)
    row = lax.broadcasted_iota(jnp.int32, (tm, 1), 0)
    up = jnp.where(row == 0, prev, pltpu.roll(p, 1, axis=0))
    dn = jnp.where(row == tm - 1, nxt, pltpu.roll(p, tm - 1, axis=0))
    cw = cw_ref[...]
    o_ref[0, 0] = up * cw[0:1] + p * cw[1:2] + dn * cw[2:3] + cb_ref[...]


def _hy_in(x, g, sh, sc, w, b, cw, cb, *, tm):
    bsz, n, d = x.shape
    nt = n // tm
    hb = tm // 8
    per_b = pl.BlockSpec((1, 1, d), lambda j, bb, i: (bb, 0, 0))
    return pl.pallas_call(
        functools.partial(_hy_in_body, nt=nt),
        grid=(3, bsz, nt),
        in_specs=[
            pl.BlockSpec((1, tm, d), lambda j, bb, i: (bb, i, 0)),
            pl.BlockSpec((1, 8, d), lambda j, bb, i: (bb, jnp.maximum(i * hb - 1, 0), 0)),
            pl.BlockSpec((1, 8, d), lambda j, bb, i: (bb, jnp.minimum((i + 1) * hb, n // 8 - 1), 0)),
            pl.BlockSpec((1, d), lambda j, bb, i: (0, 0)),
            per_b, per_b,
            pl.BlockSpec((d, d), lambda j, bb, i: (0, j)),
            pl.BlockSpec((1, d), lambda j, bb, i: (0, j)),
            pl.BlockSpec((3, d), lambda j, bb, i: (0, j)),
            pl.BlockSpec((1, d), lambda j, bb, i: (0, j)),
        ],
        out_specs=pl.BlockSpec((1, 1, tm, d), lambda j, bb, i: (j, bb, i, 0)),
        out_shape=jax.ShapeDtypeStruct((3, bsz, n, d), F32),
        compiler_params=_cp(("arbitrary", "arbitrary", "arbitrary")),
        name="hyena_in_proj",
    )(x, x, x, g, sh, sc, w, b, cw, cb)


def _filt_feat_body(w1_ref, b1_ref, f1_ref, w2_ref, b2_ref, f2_ref, o_ref, *, n_lat):
    na = o_ref.shape[1]
    a = lax.broadcasted_iota(jnp.int32, (na, 1), 0)
    lane = lax.broadcasted_iota(jnp.int32, (na, LANES), 1)
    band_idx = jnp.where(lane <= HY_BANDS, lane - 1, lane - 1 - HY_BANDS).astype(F32)
    band = 1e-4 + band_idx * ((HY_BANDS - 1 - 1e-4) / (HY_BANDS - 1))
    for j in range(B_GROUP):
        r = a * FFT_N2 + (pl.program_id(0) * B_GROUP + j)
        pos = jnp.where(r < n_lat, r, 2 * n_lat - r).astype(F32)
        tn = pos / float(max(n_lat - 1, 1))
        ang = ((2.0 * math.pi / n_lat) * pos) * band
        z = jnp.where(lane == 0, tn, jnp.where(lane <= HY_BANDS, jnp.cos(ang),
                                               jnp.where(lane < HY_EMB, -jnp.sin(ang), 0.0)))
        h1 = jnp.sin(f1_ref[...] * (_dot_hi(z, w1_ref[...]) + b1_ref[...]))
        h2 = jnp.sin(f2_ref[...] * (_dot_hi(h1, w2_ref[...]) + b2_ref[...]))
        valid = (r != n_lat).astype(F32)
        o_ref[j] = jnp.where(lane == HY_HID, tn, jnp.where(lane == HY_HID + 1, valid, h2))


def _filt_feat(w1, b1, f1, w2, b2, f2, *, n_lat):
    na = 2 * n_lat // FFT_N2
    w1p = jnp.zeros((LANES, LANES), F32).at[:HY_EMB, :HY_HID].set(w1)
    w2p = jnp.zeros((LANES, LANES), F32).at[:HY_HID, :HY_HID].set(w2)
    padv = lambda v: jnp.zeros((1, LANES), F32).at[0, :HY_HID].set(v)
    full = lambda shp: pl.BlockSpec(shp, lambda i: (0,) * len(shp))
    return pl.pallas_call(
        functools.partial(_filt_feat_body, n_lat=n_lat),
        grid=(FFT_N2 // B_GROUP,),
        in_specs=[full((LANES, LANES)), full((1, LANES)), full((1, LANES)),
                  full((LANES, LANES)), full((1, LANES)), full((1, LANES))],
        out_specs=pl.BlockSpec((B_GROUP, na, LANES), lambda i: (i, 0, 0)),
        out_shape=jax.ShapeDtypeStruct((FFT_N2, na, LANES), F32),
        compiler_params=_cp(("arbitrary",)),
        name="hyena_filter_features",
    )(w1p, padv(b1), padv(f1), w2p, padv(b2), padv(f2))


def _filt_s1_body(hd_ref, w3_ref, dec_ref, tab_ref, o_ref):
    na = hd_ref.shape[1]
    ha = na // 2
    o2 = _rows2d(o_ref)
    for j in range(B_GROUP):
        f = hd_ref[j]
        tn = f[:, HY_HID:HY_HID + 1]
        valid = f[:, HY_HID + 1:HY_HID + 2]
        top = _dot3(f[:ha], w3_ref[0, 0]) * jnp.exp(-tn[:ha] * jnp.abs(dec_ref[0, 0]))
        bot = _dot3(f[ha:], w3_ref[0, 1]) * jnp.exp(-tn[ha:] * jnp.abs(dec_ref[0, 1])) * valid[ha:]
        hb = jnp.concatenate([top, bot], axis=0).astype(BF16)
        o2[pl.ds(j, 2 * na, stride=B_GROUP), :] = _dot(tab_ref[j], hb)


def _filt_s1(hd, w3r, dec, tab, *, ct):
    _, na, _ = hd.shape
    d = w3r.shape[-1]
    return pl.pallas_call(
        _filt_s1_body,
        grid=(2, FFT_N2 // B_GROUP, d // ct),
        in_specs=[
            pl.BlockSpec((B_GROUP, na, LANES), lambda o, g, c: (g, 0, 0)),
            pl.BlockSpec((1, 2, LANES, ct), lambda o, g, c: (o, 0, 0, c)),
            pl.BlockSpec((1, 2, 1, ct), lambda o, g, c: (o, 0, 0, c)),
            pl.BlockSpec((B_GROUP, 2 * na, na), lambda o, g, c: (g, 0, 0)),
        ],
        out_specs=pl.BlockSpec((None, 2 * na, B_GROUP, ct), lambda o, g, c: (o, 0, g, c)),
        out_shape=jax.ShapeDtypeStruct((2, 2 * na, FFT_N2, d), F32),
        compiler_params=_cp(("arbitrary", "arbitrary", "arbitrary")),
        name="hyena_filter_dft1",
    )(hd, w3r, dec, tab)


S2_KB = 4


def _s2_body(*refs, conv):
    if conv:
        o_ref, kf_ref, ff_ref, fi_ref, g_ref = refs
    else:
        o_ref, ff_ref, g_ref = refs
    for u in range(S2_KB):
        xin = jnp.concatenate([o_ref[0, u], o_ref[1, u]], axis=0).astype(BF16)
        xf = _dot(ff_ref[...], xin)
        if conv:
            xr, xi = xf[:FFT_N2], xf[FFT_N2:]
            kr = kf_ref[0, u].astype(F32)
            ki = kf_ref[1, u].astype(F32)
            y = jnp.concatenate([xr * kr - xi * ki, xr * ki + xi * kr], axis=0).astype(BF16)
            xf = _dot(fi_ref[...], y)
        g_ref[0, u] = xf[:FFT_N2].astype(g_ref.dtype)
        g_ref[1, u] = xf[FFT_N2:].astype(g_ref.dtype)


def _s2(o4, kf, order, ff, fi, *, ct, conv):
    n1 = o4.shape[-3]
    d = o4.shape[-1]
    full = lambda a: pl.BlockSpec(a.shape, lambda k, c: (0,) * a.ndim)
    nk = n1 // S2_KB
    if conv:
        blk = pl.BlockSpec((2, S2_KB, FFT_N2, ct), lambda k, c: (0, k, 0, c))
        in_specs = [blk, pl.BlockSpec((None, 2, S2_KB, FFT_N2, ct), lambda k, c: (order, 0, k, 0, c)), full(ff),
                    full(fi)]
        args = (o4, kf, ff, fi)
        grid = (nk, d // ct)
        out_specs = blk
        out_dtype = F32
    else:
        no = o4.shape[0]
        blk = pl.BlockSpec((None, 2, S2_KB, FFT_N2, ct), lambda k, c: (k // nk, 0, k % nk, 0, c))
        in_specs = [blk, full(ff)]
        args = (o4, ff)
        grid = (no * nk, d // ct)
        out_specs = blk
        out_dtype = BF16
    return pl.pallas_call(
        functools.partial(_s2_body, conv=conv),
        grid=grid,
        in_specs=in_specs,
        out_specs=out_specs,
        out_shape=jax.ShapeDtypeStruct(o4.shape, out_dtype),
        compiler_params=_cp(("arbitrary", "arbitrary")),
        name="hyena_conv_dft2" if conv else "hyena_filter_dft2",
    )(*args)


def _rows2d(ref):
    lead = ref.shape[:-3]
    return ref.reshape(lead + (ref.shape[-3] * B_GROUP, ref.shape[-1]))


def _s1_body(z_ref, tab_ref, o_ref):
    rows, n_out = z_ref.shape[0], o_ref.shape[0]
    z2, o2 = _rows2d(z_ref), _rows2d(o_ref)
    for j in range(B_GROUP):
        zj = z2[pl.ds(j, rows, stride=B_GROUP), :]
        o2[pl.ds(j, n_out, stride=B_GROUP), :] = _dot(tab_ref[j], zj.astype(BF16))


def _s1(z4, zi, tab, *, ct):
    _, rows, _, d = z4.shape
    n_out = tab.shape[1]
    return pl.pallas_call(
        _s1_body,
        grid=(FFT_N2 // B_GROUP, d // ct),
        in_specs=[
            pl.BlockSpec((None, rows, B_GROUP, ct), lambda g, c: (zi, 0, g, c)),
            pl.BlockSpec((B_GROUP, n_out, rows), lambda g, c: (g, 0, 0)),
        ],
        out_specs=pl.BlockSpec((n_out, B_GROUP, ct), lambda g, c: (0, g, c)),
        out_shape=jax.ShapeDtypeStruct((n_out, FFT_N2, d), F32),
        compiler_params=_cp(("arbitrary", "arbitrary")),
        name="hyena_conv_dft1",
    )(z4, tab)


def _s3_body(g_ref, tab_ref, gate_ref, z_ref, fb_ref, o_ref):
    n_in, rows = g_ref.shape[0], o_ref.shape[0]
    g2, o2 = _rows2d(g_ref), _rows2d(o_ref)
    for j in range(B_GROUP):
        gj = g2[pl.ds(j, n_in, stride=B_GROUP), :]
        o2[pl.ds(j, rows, stride=B_GROUP), :] = _dot(tab_ref[j], gj.astype(BF16))
    o_ref[...] = gate_ref[...] * (o_ref[...] + z_ref[...] * fb_ref[...])


def _s3(g3, tab, gate4, gi, z4, zi, fb, *, ct):
    n_in, _, d = g3.shape
    rows = tab.shape[1]
    blk = pl.BlockSpec((rows, B_GROUP, ct), lambda g, c: (0, g, c))
    return pl.pallas_call(
        _s3_body,
        grid=(FFT_N2 // B_GROUP, d // ct),
        in_specs=[
            pl.BlockSpec((n_in, B_GROUP, ct), lambda g, c: (0, g, c)),
            pl.BlockSpec((B_GROUP, rows, n_in), lambda g, c: (g, 0, 0)),
            pl.BlockSpec((None, rows, B_GROUP, ct), lambda g, c: (gi, 0, g, c)),
            pl.BlockSpec((None, rows, B_GROUP, ct), lambda g, c: (zi, 0, g, c)),
            pl.BlockSpec((1, 1, ct), lambda g, c: (0, 0, c)),
        ],
        out_specs=blk,
        out_shape=jax.ShapeDtypeStruct((rows, FFT_N2, d), F32),
        compiler_params=_cp(("arbitrary", "arbitrary")),
        name="hyena_conv_idft1",
    )(g3, tab, gate4, z4, fb)


def _dft_tables(n_lat):
    n = 2 * n_lat
    n1 = n // FFT_N2
    b = jnp.arange(FFT_N2, dtype=jnp.int32)[:, None, None]
    k1 = jnp.arange(n1, dtype=jnp.int32)[None, :, None]
    a = jnp.arange(n1, dtype=jnp.int32)[None, None, :]
    th = ((k1 * (a * FFT_N2 + b)) % n).astype(F32) * (2.0 * math.pi / n)
    cr, sn = jnp.cos(th), jnp.sin(th)
    ha = n1 // 2
    crh, snh = cr[:, :, :ha], sn[:, :, :ha]
    w1 = jnp.concatenate([jnp.concatenate([crh, snh], axis=2), jnp.concatenate([-snh, crh], axis=2)], axis=1)
    w1f = jnp.concatenate([cr, -sn], axis=1)
    v = jnp.swapaxes(w1, 1, 2) * (1.0 / n)
    k2 = jnp.arange(FFT_N2, dtype=jnp.int32)
    th2 = ((k2[:, None] * k2[None, :]) % FFT_N2).astype(F32) * (2.0 * math.pi / FFT_N2)
    c2, s2 = jnp.cos(th2), jnp.sin(th2)
    ff = jnp.concatenate([jnp.concatenate([c2, s2], axis=1), jnp.concatenate([-s2, c2], axis=1)], axis=0)
    fi = jnp.concatenate([jnp.concatenate([c2, -s2], axis=1), jnp.concatenate([s2, c2], axis=1)], axis=0)
    return w1.astype(BF16), w1f.astype(BF16), v.astype(BF16), ff.astype(BF16), fi.astype(BF16)


def _hyena_mix(proj3, fparams, fbias, *, n_lat):
    _, bsz, _, d = proj3.shape
    f_w1, f_b1, f_f1, f_w2, f_b2, f_f2, f_w3, decay = fparams
    na = 2 * n_lat // FFT_N2
    w1, w1f, v, ff, fi = _dft_tables(n_lat)
    ct = LANES
    hd = _filt_feat(f_w1, f_b1, f_f1, f_w2, f_b2, f_f2, n_lat=n_lat)
    w3r = jnp.transpose(f_w3.reshape(HY_HID, 2, 2, d), (1, 2, 0, 3))
    w3r = jnp.zeros((2, 2, LANES, d), F32).at[:, :, :HY_HID].set(w3r)
    kf1 = _filt_s1(hd, w3r, decay.reshape(2, 2, 1, d), w1f, ct=ct)
    kf = _s2(kf1.reshape(2, 2, na, FFT_N2, d), None, 0, ff, None, ct=d, conv=False)
    p3 = proj3.reshape(3, bsz * (n_lat // FFT_N2), FFT_N2, d)
    z4, zi = p3, 2
    for o in range(2):
        o1 = _s1(z4, zi, w1, ct=ct)
        g = _s2(o1.reshape(2, na, FFT_N2, d), kf, o, ff, fi, ct=d, conv=True)
        z4 = _s3(g.reshape(2 * na, FFT_N2, d), v, p3, o, z4, zi, fbias[o].reshape(1, 1, d), ct=ct)[None]
        zi = 0
    return z4.reshape(bsz, n_lat, d)


def _rope_tables(n_tokens):
    rows = n_tokens // GRID_W
    row = jnp.broadcast_to(jnp.arange(rows, dtype=F32)[:, None], (rows, GRID_W)).reshape(-1)
    col = jnp.broadcast_to(jnp.arange(GRID_W, dtype=F32)[None, :], (rows, GRID_W)).reshape(-1)
    axis_dim = QK_ROPE // 2
    inv_freq = 1.0 / (ROPE_THETA ** (jnp.arange(0, axis_dim, 2, dtype=F32) / axis_dim))
    ang = jnp.concatenate([row[:, None] * inv_freq, col[:, None] * inv_freq], axis=-1)
    return jnp.cos(ang), jnp.sin(ang)


def _mla_weights(w_down, g_q, w_uq, g_kv, w_ukv):
    d = w_down.shape[0]
    nh = MLA_HEADS
    kpe = w_down[:, Q_LORA + KV_LORA:]
    w1, w2 = kpe[:, 0::2], kpe[:, 1::2]
    z = jnp.zeros((d, LANES - QK_ROPE), w_down.dtype)
    wd = jnp.concatenate([w_down[:, :Q_LORA + KV_LORA], w1, w2, z, w2, w1, z], axis=1).astype(BF16)
    uq = w_uq.reshape(Q_LORA, nh, QK_NOPE + QK_ROPE)
    pe = uq[:, :, QK_NOPE:]
    uq = jnp.concatenate([uq[:, :, :QK_NOPE], pe[:, :, 0::2], pe[:, :, 1::2]], axis=2)
    wuqT = uq.reshape(Q_LORA, nh * (QK_NOPE + QK_ROPE)).T.astype(BF16)
    ukv = w_ukv.reshape(KV_LORA, nh, QK_NOPE + V_DIM)
    wuk = ukv[:, :, :QK_NOPE].reshape(KV_LORA, nh * QK_NOPE).astype(BF16)
    wuvT = ukv[:, :, QK_NOPE:].reshape(KV_LORA, nh * V_DIM).T.astype(BF16)
    return wd, g_q.reshape(1, -1), g_kv.reshape(1, -1), wuk, wuqT, wuvT


def kernel(x, c, ctx, c_ctx, ada_w, ada_b, norm_mix_g, norm_ffn_g, mla_w_down, mla_g_q, mla_w_uq, mla_g_kv, mla_w_ukv, mla_w_o, hy_w_in, hy_b_in, hy_conv_w, hy_conv_b, hy_f_w1, hy_f_b1, hy_f_freq1, hy_f_w2, hy_f_b2, hy_f_freq2, hy_f_w3, hy_decay, hy_bias, hy_w_out, hy_b_out, moe_w_r, moe_b_r, moe_w_in, moe_b_in, moe_w_out, moe_b_out, final_g):
    bsz, n_lat, d = x.shape
    n_ctx = ctx.shape[1]
    depth = ada_w.shape[0]
    assert bsz == 2 and d == MLA_HEADS * V_DIM and n_lat % 512 == 0 and n_ctx % 128 == 0
    assert depth == 2

    cond8 = jnp.zeros((8, d), F32).at[:bsz].set(c).at[bsz].set(c_ctx)
    mods = _ada(cond8, ada_w, ada_b)

    def mod(i, j, rows):
        return mods[i, rows, j * d:(j + 1) * d][:, None, :]

    lat_rows = slice(0, bsz)
    ctx_rows = slice(bsz, bsz + 1)
    xl = x
    for i in range(depth):
        kind, j = i % 2, i // 2
        sh1, sc1, g1 = (mod(i, m, lat_rows) for m in range(3))
        sh2, sc2, g2 = (mod(i, m, lat_rows) for m in range(3, 6))
        gm = norm_mix_g[i].reshape(1, d)
        if kind == 0:
            wts = _mla_weights(mla_w_down[j], mla_g_q[j], mla_w_uq[j], mla_g_kv[j], mla_w_ukv[j])
            cos, sin = _rope_tables(n_lat)
            zl = jnp.zeros((n_lat, LANES - QK_ROPE), F32)
            tabs = (jnp.concatenate([cos, cos, zl], axis=1), jnp.concatenate([-sin, sin, zl], axis=1), cos.T, sin.T)
            tq = tv = 512
            tk = 2048 if n_lat % 4096 == 0 else 512
            qT, k, vT = _mla_proj(xl, gm, sh1, sc1, wts, tabs, need_q=True, tm=tv, tk=tv)
            half = QK_ROPE // 2
            one_c = jnp.concatenate([jnp.ones((n_ctx, QK_ROPE), F32), jnp.zeros((n_ctx, LANES - QK_ROPE), F32)], axis=1)
            tabs_c = (one_c, jnp.zeros((n_ctx, LANES), F32), jnp.ones((half, n_ctx), F32), jnp.zeros((half, n_ctx), F32))
            kc, vTc = _mla_proj(ctx, gm, mod(i, 0, ctx_rows), mod(i, 1, ctx_rows), wts, tabs_c,
                                need_q=False, tm=n_ctx, tk=n_ctx)
            o = _attention(qT, k, vT, kc, vTc, tq=tq, tk=tk)
            wo = mla_w_o[j].astype(BF16)
            bo = jnp.zeros((1, d), F32)
            transposed = True
        else:
            proj3 = _hy_in(xl, gm, sh1, sc1, hy_w_in[j].astype(BF16), hy_b_in[j].reshape(1, -1), hy_conv_w[j],
                           hy_conv_b[j].reshape(1, -1), tm=512)
            fparams = (hy_f_w1[j], hy_f_b1[j], hy_f_freq1[j], hy_f_w2[j], hy_f_b2[j], hy_f_freq2[j], hy_f_w3[j],
                       hy_decay[j])
            o = _hyena_mix(proj3, fparams, hy_bias[j], n_lat=n_lat)
            wo = hy_w_out[j].astype(BF16)
            bo = hy_b_out[j].reshape(1, d)
            transposed = False
        wr = jnp.zeros((d, LANES), F32).at[:, :N_EXPERTS].set(moe_w_r[i])
        wrh = wr.astype(BF16)
        wrl = (wr - wrh.astype(F32)).astype(BF16)
        br = jnp.zeros((1, LANES), F32).at[0, :N_EXPERTS].set(moe_b_r[i])
        xl, fl, topi, gates, rank, cnt = _post(o, wo, bo, xl, g1, norm_ffn_g[i].reshape(1, d), sh2, sc2, wrh, wrl, br,
                                               transposed=transposed, tm=512)
        xl = _moe(fl, topi, gates, rank, cnt, xl, g2, final_g.reshape(1, d), i, moe_w_in, moe_b_in,
                  moe_w_out, moe_b_out, final=(i == depth - 1))
    return xl
```

```python
import functools
import math

import jax
import jax.numpy as jnp
from jax import lax
from jax.experimental import pallas as pl
from jax.experimental.pallas import tpu as pltpu

F32 = jnp.float32
BF16 = jnp.bfloat16

EPS = 1e-6
GRID_W = 64
MLA_HEADS = 8
QK_NOPE = 128
QK_ROPE = 64
V_DIM = 128
Q_LORA = 512
KV_LORA = 256
ROPE_THETA = 10000.0
MLA_SCALE = (QK_NOPE + QK_ROPE) ** -0.5
QK_PAD = 256

HY_EMB = 33
HY_BANDS = (HY_EMB - 1) // 2
HY_HID = 64
FFT_N2 = 128
B_GROUP = 8

N_EXPERTS = 32
TOP_K = 4
SWIGLU_LIMIT = 7.0
SWIGLU_ALPHA = 1.702
MOE_TM = 256
LANES = 128

VMEM_LIMIT = 56 * 1024 * 1024


def _cp(sem, vmem=VMEM_LIMIT):
    return pltpu.CompilerParams(dimension_semantics=sem, vmem_limit_bytes=vmem)


def _dot(a, b):
    return jnp.dot(a, b, preferred_element_type=F32)


def _dot_hi(a, b):
    return jnp.dot(a, b, preferred_element_type=F32, precision=lax.Precision.HIGHEST)


def _dot3(a, b):
    ah = a.astype(BF16)
    al = (a - ah.astype(F32)).astype(BF16)
    bh = b.astype(BF16)
    bl = (b - bh.astype(F32)).astype(BF16)
    return _dot(ah, bh) + (_dot(al, bh) + _dot(ah, bl))


def _rms(x, g):
    return x * lax.rsqrt(jnp.mean(x * x, axis=-1, keepdims=True) + EPS) * g


def _ada_body(c_ref, w_ref, b_ref, o_ref):
    c = c_ref[...]
    s = c * jax.nn.sigmoid(c)
    o_ref[0] = _dot(s.astype(BF16), w_ref[0].astype(BF16)) + b_ref[0]


def _ada(cond8, ada_w, ada_b):
    depth, d, n = ada_w.shape
    tn = n // 4
    return pl.pallas_call(
        _ada_body,
        grid=(depth, n // tn),
        in_specs=[
            pl.BlockSpec((8, d), lambda i, j: (0, 0)),
            pl.BlockSpec((1, d, tn), lambda i, j: (i, 0, j)),
            pl.BlockSpec((1, 1, tn), lambda i, j: (i, 0, j)),
        ],
        out_specs=pl.BlockSpec((1, 8, tn), lambda i, j: (i, 0, j)),
        out_shape=jax.ShapeDtypeStruct((depth, 8, n), F32),
        compiler_params=_cp(("arbitrary", "arbitrary")),
        name="ada_mod",
    )(cond8, ada_w, ada_b.reshape(depth, 1, n))


def _mla_proj_body(x_ref, g_ref, sh_ref, sc_ref, wd_ref, gq_ref, gkv_ref, wuk_ref, wuqT_ref, wuvT_ref,
                   ct_ref, st_ref, cT_ref, sT_ref, *out_refs, need_q, tk):
    if need_q:
        qT_ref, k_ref, vT_ref = out_refs
    else:
        k_ref, vT_ref = out_refs
    nh = MLA_HEADS
    x = x_ref[0]
    h = _rms(x, g_ref[...]) * (1.0 + sc_ref[0]) + sh_ref[0]
    lat = _dot(h.astype(BF16), wd_ref[...])
    o_kv = Q_LORA
    o_a = Q_LORA + KV_LORA
    kvn = _rms(lat[:, o_kv:o_a], gkv_ref[...])
    kr = (lat[:, o_a:o_a + LANES] * ct_ref[...] + lat[:, o_a + LANES:o_a + 2 * LANES] * st_ref[...]).astype(BF16)
    knope = _dot(kvn.astype(BF16), wuk_ref[...])
    for hh in range(nh):
        k_ref[0, hh, :, 0:QK_NOPE] = knope[:, hh * QK_NOPE:(hh + 1) * QK_NOPE].astype(BF16)
        k_ref[0, hh, :, QK_NOPE:QK_PAD] = kr
    vT = _dot(wuvT_ref[...], kvn.T.astype(BF16))
    tm = x.shape[0]
    for hh in range(nh):
        for c in range(tm // tk):
            vT_ref[0, hh, c] = vT[hh * V_DIM:(hh + 1) * V_DIM, c * tk:(c + 1) * tk].astype(BF16)
    if need_q:
        qn = _rms(lat[:, :Q_LORA], gq_ref[...])
        qT = _dot(wuqT_ref[...], qn.T.astype(BF16)) * (MLA_SCALE * math.log2(math.e))
        c = cT_ref[...]
        s = sT_ref[...]
        hw = QK_NOPE + QK_ROPE
        half = QK_ROPE // 2
        for hh in range(nh):
            base = hh * hw
            x1 = qT[base + QK_NOPE:base + QK_NOPE + half]
            x2 = qT[base + QK_NOPE + half:base + hw]
            qT_ref[0, hh, 0:QK_NOPE] = qT[base:base + QK_NOPE].astype(BF16)
            qT_ref[0, hh, QK_NOPE:QK_NOPE + half] = (x1 * c - x2 * s).astype(BF16)
            qT_ref[0, hh, QK_NOPE + half:hw] = (x1 * s + x2 * c).astype(BF16)
            qT_ref[0, hh, hw:QK_PAD] = jnp.zeros((QK_PAD - hw, tm), BF16)


def _mla_proj(x, g, sh, sc, wts, tabs, *, need_q, tm, tk):
    bsz, n, d = x.shape
    nh = MLA_HEADS
    wd, gq, gkv, wuk, wuqT, wuvT = wts
    ct, st, cT, sT = tabs
    nsh = sh.shape[0]
    full = lambda a: pl.BlockSpec(a.shape, lambda b, i: (0,) * a.ndim)
    in_specs = [
        pl.BlockSpec((1, tm, d), lambda b, i: (b, i, 0)),
        full(g),
        pl.BlockSpec((1, 1, d), lambda b, i: (b % nsh, 0, 0)),
        pl.BlockSpec((1, 1, d), lambda b, i: (b % nsh, 0, 0)),
        full(wd), full(gq), full(gkv), full(wuk), full(wuqT), full(wuvT),
        pl.BlockSpec((tm, LANES), lambda b, i: (i, 0)),
        pl.BlockSpec((tm, LANES), lambda b, i: (i, 0)),
        pl.BlockSpec((QK_ROPE // 2, tm), lambda b, i: (0, i)),
        pl.BlockSpec((QK_ROPE // 2, tm), lambda b, i: (0, i)),
    ]
    out_specs = [
        pl.BlockSpec((1, nh, tm, QK_PAD), lambda b, i: (b, 0, i, 0)),
        pl.BlockSpec((1, nh, tm // tk, V_DIM, tk), lambda b, i: (b, 0, i, 0, 0)),
    ]
    out_shape = [
        jax.ShapeDtypeStruct((bsz, nh, n, QK_PAD), BF16),
        jax.ShapeDtypeStruct((bsz, nh, n // tk, V_DIM, tk), BF16),
    ]
    if need_q:
        out_specs = [pl.BlockSpec((1, nh, QK_PAD, tm), lambda b, i: (b, 0, 0, i))] + out_specs
        out_shape = [jax.ShapeDtypeStruct((bsz, nh, QK_PAD, n), BF16)] + out_shape
    return pl.pallas_call(
        functools.partial(_mla_proj_body, need_q=need_q, tk=tk),
        grid=(bsz, n // tm),
        in_specs=in_specs,
        out_specs=out_specs,
        out_shape=out_shape,
        compiler_params=_cp(("arbitrary", "arbitrary")),
        name="mla_proj_q" if need_q else "mla_proj_ctx",
    )(x, g, sh, sc, wd, gq, gkv, wuk, wuqT, wuvT, ct, st, cT, sT)


SM_STRIP = 64
SUBLANES = 8


def _attn_body(qT_ref, k_ref, vT_ref, kc_ref, vTc_ref, o_ref, s0, s1, p0, p1, sc, pc, acc, m_scr, x0, x1, xc,
               a0, a1, ac, l_scr, d0, d1, dc, *, tk):
    nchunk = k_ref.shape[2] // tk

    def scores(kblk, s_ref, mx_ref):
        r = _dot(kblk, qT_ref[0, 0])
        s_ref[...] = r
        mx_ref[...] = jnp.max(r, axis=0, keepdims=True)

    def probs(s_ref, mx_ref, p_ref, a_ref, d_ref):
        m_old = m_scr[...]
        m_new = jnp.maximum(m_old, mx_ref[...])
        m_scr[...] = m_new
        alpha = jnp.exp2(m_old - m_new)
        a_ref[...] = alpha
        part = None
        for r in range(0, s_ref.shape[0], SM_STRIP):
            p = jnp.exp2(s_ref[r:r + SM_STRIP] - m_new)
            p_ref[r:r + SM_STRIP] = p.astype(BF16)
            ps = jnp.sum(p.reshape(SM_STRIP // SUBLANES, SUBLANES, p.shape[1]), axis=0)
            part = ps if part is None else part + ps
        d_ref[...] = part

    def accumulate(p_ref, a_ref, d_ref, vblk):
        acc[...] = a_ref[...] * acc[...] + _dot(vblk, p_ref[...])
        l_scr[...] = a_ref[...] * l_scr[...] + d_ref[...]

    def kchunk(i):
        return k_ref[0, 0, pl.ds(pl.multiple_of(i * tk, tk), tk), :]

    def vchunk(i):
        nsub = tk // vT_ref.shape[-1]
        return jnp.concatenate([vT_ref[0, 0, i * nsub + u] for u in range(nsub)], axis=1)

    m_scr[...] = jnp.full(m_scr.shape, -jnp.inf, F32)
    acc[...] = jnp.zeros(acc.shape, F32)
    l_scr[...] = jnp.zeros(l_scr.shape, F32)
    scores(kc_ref[0, 0], sc, xc)
    scores(kchunk(0), s0, x0)
    probs(sc, xc, pc, ac, dc)
    scores(kchunk(1), s1, x1)
    accumulate(pc, ac, dc, vTc_ref[0, 0, 0])
    probs(s0, x0, p0, a0, d0)

    def body(j, carry):
        t = 2 * j
        scores(kchunk(t + 2), s0, x0)
        accumulate(p0, a0, d0, vchunk(t))
        probs(s1, x1, p1, a1, d1)
        scores(kchunk(t + 3), s1, x1)
        accumulate(p1, a1, d1, vchunk(t + 1))
        probs(s0, x0, p0, a0, d0)
        return carry

    lax.fori_loop(0, nchunk // 2 - 1, body, 0)
    accumulate(p0, a0, d0, vchunk(nchunk - 2))
    probs(s1, x1, p1, a1, d1)
    accumulate(p1, a1, d1, vchunk(nchunk - 1))
    o_ref[0, 0] = (acc[...] / jnp.sum(l_scr[...], axis=0, keepdims=True)).astype(BF16)


def _attention(qT, k, vT, kc, vTc, *, tq, tk):
    bsz, nh, _, n = qT.shape
    nc = kc.shape[2]
    tv = vT.shape[-1]
    assert (n // tk) % 2 == 0 and tk % tv == 0
    return pl.pallas_call(
        functools.partial(_attn_body, tk=tk),
        grid=(bsz, nh, n // tq),
        in_specs=[
            pl.BlockSpec((1, 1, QK_PAD, tq), lambda b, h, i: (b, h, 0, i)),
            pl.BlockSpec((1, 1, n, QK_PAD), lambda b, h, i: (b, h, 0, 0)),
            pl.BlockSpec((1, 1, n // tv, V_DIM, tv), lambda b, h, i: (b, h, 0, 0, 0)),
            pl.BlockSpec((1, 1, nc, QK_PAD), lambda b, h, i: (b, h, 0, 0)),
            pl.BlockSpec((1, 1, 1, V_DIM, nc), lambda b, h, i: (b, h, 0, 0, 0)),
        ],
        out_specs=pl.BlockSpec((1, 1, V_DIM, tq), lambda b, h, i: (b, h, 0, i)),
        out_shape=jax.ShapeDtypeStruct((bsz, nh, V_DIM, n), BF16),
        scratch_shapes=[pltpu.VMEM((tk, tq), F32), pltpu.VMEM((tk, tq), F32),
                        pltpu.VMEM((tk, tq), BF16), pltpu.VMEM((tk, tq), BF16),
                        pltpu.VMEM((nc, tq), F32), pltpu.VMEM((nc, tq), BF16),
                        pltpu.VMEM((V_DIM, tq), F32), pltpu.VMEM((1, tq), F32),
                        pltpu.VMEM((1, tq), F32), pltpu.VMEM((1, tq), F32), pltpu.VMEM((1, tq), F32),
                        pltpu.VMEM((1, tq), F32), pltpu.VMEM((1, tq), F32), pltpu.VMEM((1, tq), F32),
                        pltpu.VMEM((SUBLANES, tq), F32), pltpu.VMEM((SUBLANES, tq), F32),
                        pltpu.VMEM((SUBLANES, tq), F32), pltpu.VMEM((SUBLANES, tq), F32)],
        compiler_params=_cp(("arbitrary", "arbitrary", "arbitrary")),
        name="mla_attention",
    )(qT, k, vT, kc, vTc)


def _post_body(o_ref, wo_ref, bo_ref, x_ref, g1_ref, gf_ref, sh_ref, sc_ref, wrh_ref, wrl_ref, br_ref, tri_ref,
               xl_ref, fl_ref, ti_ref, gt_ref, rk_ref, cnt_ref, *, transposed):
    @pl.when((pl.program_id(0) == 0) & (pl.program_id(1) == 0))
    def _():
        cnt_ref[...] = jnp.zeros_like(cnt_ref)

    tm = x_ref.shape[1]
    if transposed:
        oT = o_ref[0].astype(F32).reshape(MLA_HEADS * V_DIM, tm)
        o = oT.T.astype(BF16)
    else:
        o = o_ref[0].astype(BF16)
    y = _dot(o, wo_ref[...]) + bo_ref[...]
    xl = x_ref[0] + g1_ref[0] * y
    xl_ref[0] = xl
    fl = _rms(xl, gf_ref[...]) * (1.0 + sc_ref[0]) + sh_ref[0]
    _to_rows(fl_ref, fl)
    flh = fl.astype(BF16)
    fll = (fl - flh.astype(F32)).astype(BF16)
    logits = _dot(flh, wrh_ref[...]) + (_dot(fll, wrh_ref[...]) + _dot(flh, wrl_ref[...])) + br_ref[...]
    lane = lax.broadcasted_iota(jnp.int32, (tm, LANES), 1).astype(F32)
    neg = jnp.float32(-jnp.inf)
    work = jnp.where(lane < N_EXPERTS, logits, neg)
    vals, idxs = [], []
    onehot = jnp.zeros((tm, LANES), F32)
    for _ in range(TOP_K):
        mk = jnp.max(work, axis=-1, keepdims=True)
        ik = jnp.min(jnp.where(work == mk, lane, float(LANES)), axis=-1, keepdims=True)
        sel = lane == ik
        onehot = jnp.where(sel, 1.0, onehot)
        work = jnp.where(sel, neg, work)
        vals.append(mk)
        idxs.append(ik)
    es = [jnp.exp(v - vals[0]) for v in vals]
    den = es[0] + es[1] + es[2] + es[3]
    pre = _dot(tri_ref[...], onehot.astype(BF16)) + cnt_ref[...]
    ti = jnp.zeros((tm, LANES), F32)
    gt = jnp.zeros((tm, LANES), F32)
    rk = jnp.zeros((tm, LANES), F32)
    for kk in range(TOP_K):
        rank = jnp.sum(jnp.where(lane == idxs[kk], pre, 0.0), axis=-1, keepdims=True)
        ti = jnp.where(lane == kk, idxs[kk], ti)
        gt = jnp.where(lane == kk, es[kk] / den, gt)
        rk = jnp.where(lane == kk, rank, rk)
    ti_ref[...] = ti[:, :TOP_K].astype(jnp.int32)
    gt_ref[...] = gt[:, :TOP_K]
    rk_ref[...] = rk[:, :TOP_K].astype(jnp.int32)
    cnt_ref[...] += jnp.sum(onehot, axis=0, keepdims=True)


def _post(o, wo, bo, x, g1, gf, sh, sc, wrh, wrl, br, *, transposed, tm):
    bsz, n, d = x.shape
    t = bsz * n
    nt = n // tm
    tri = (lax.broadcasted_iota(jnp.int32, (tm, tm), 0) > lax.broadcasted_iota(jnp.int32, (tm, tm), 1)).astype(BF16)
    full = lambda a: pl.BlockSpec(a.shape, lambda b, i: (0,) * a.ndim)
    per_b = pl.BlockSpec((1, 1, d), lambda b, i: (b, 0, 0))
    if transposed:
        o_spec = pl.BlockSpec((1, MLA_HEADS, V_DIM, tm), lambda b, i: (b, 0, 0, i))
    else:
        o_spec = pl.BlockSpec((1, tm, d), lambda b, i: (b, i, 0))
    tok = lambda w: pl.BlockSpec((tm, w), lambda b, i: (b * nt + i, 0))
    return pl.pallas_call(
        functools.partial(_post_body, transposed=transposed),
        grid=(bsz, nt),
        in_specs=[o_spec, full(wo), full(bo), pl.BlockSpec((1, tm, d), lambda b, i: (b, i, 0)), per_b, full(gf),
                  per_b, per_b, full(wrh), full(wrl), full(br), full(tri)],
        out_specs=[pl.BlockSpec((1, tm, d), lambda b, i: (b, i, 0)),
                   pl.BlockSpec((tm * ROW_SUB, LANES), lambda b, i: (b * nt + i, 0)),
                   tok(TOP_K), tok(TOP_K), tok(TOP_K), pl.BlockSpec((1, LANES), lambda b, i: (0, 0))],
        out_shape=[jax.ShapeDtypeStruct((bsz, n, d), F32), jax.ShapeDtypeStruct((t * ROW_SUB, LANES), F32),
                   jax.ShapeDtypeStruct((t, TOP_K), jnp.int32), jax.ShapeDtypeStruct((t, TOP_K), F32),
                   jax.ShapeDtypeStruct((t, TOP_K), jnp.int32), jax.ShapeDtypeStruct((1, LANES), F32)],
        compiler_params=_cp(("arbitrary", "arbitrary")),
        name="post_attn" if transposed else "post_hyena",
    )(o, wo, bo, x, g1, gf, sh, sc, wrh, wrl, br, tri)


ROW_SUB = 8


def _row_slice(i):
    return pl.ds(pl.multiple_of(i * ROW_SUB, ROW_SUB), ROW_SUB)


def _to_rows(ref, x):
    for s in range(ROW_SUB):
        ref[pl.ds(s, x.shape[0], stride=ROW_SUB), :] = x[:, s * LANES:(s + 1) * LANES]


def _from_rows(ref, lo, hi):
    return jnp.concatenate([ref[pl.ds(lo * ROW_SUB + s, hi - lo, stride=ROW_SUB), :] for s in range(ROW_SUB)], axis=1)


def _dispatch_body(pe_ref, pd_ref, dest_ref, fl_ref, xs_out, zbuf, sem, *, td):
    @pl.when(pl.program_id(0) == 0)
    def _():
        zbuf[...] = jnp.zeros(zbuf.shape, zbuf.dtype)
        for e in range(N_EXPERTS):
            @pl.when(pd_ref[e] > 0)
            def _():
                start = pl.multiple_of((pe_ref[e] - MOE_TM) * ROW_SUB, ROW_SUB)
                cp = pltpu.make_async_copy(zbuf, xs_out.at[pl.ds(start, MOE_TM * ROW_SUB)], sem)
                cp.start()
                cp.wait()

    def issue(t, carry):
        for kk in range(TOP_K):
            d = dest_ref[0, 0, t * TOP_K + kk]
            pltpu.make_async_copy(fl_ref.at[_row_slice(t)], xs_out.at[_row_slice(d)], sem).start(priority=kk % 2)
        return carry

    lax.fori_loop(0, td, issue, 0, unroll=2)

    def drain(t, carry):
        pltpu.make_async_copy(fl_ref.at[_row_slice(0)], xs_out.at[_row_slice(0)], sem).wait()
        return carry

    lax.fori_loop(0, td * TOP_K, drain, 0, unroll=8)


def _dispatch(pad_end, padded, dest, fl, n_rows, *, td):
    t = fl.shape[0] // ROW_SUB
    dest3 = dest.reshape(t // td, 1, td * TOP_K)
    grid_spec = pltpu.PrefetchScalarGridSpec(
        num_scalar_prefetch=2,
        grid=(t // td,),
        in_specs=[
            pl.BlockSpec((1, 1, td * TOP_K), lambda i, pe, pd: (i, 0, 0), memory_space=pltpu.SMEM),
            pl.BlockSpec((td * ROW_SUB, LANES), lambda i, pe, pd: (i, 0)),
        ],
        out_specs=pl.BlockSpec(memory_space=pl.ANY),
        scratch_shapes=[pltpu.VMEM((MOE_TM * ROW_SUB, LANES), fl.dtype), pltpu.SemaphoreType.DMA(())],
    )
    return pl.pallas_call(
        functools.partial(_dispatch_body, td=td),
        grid_spec=grid_spec,
        out_shape=jax.ShapeDtypeStruct((n_rows * ROW_SUB, LANES), fl.dtype),
        compiler_params=_cp(("arbitrary",)),
        name="moe_dispatch",
    )(pad_end, padded, dest3, fl)


def _expert_body(be_ref, nu_ref, xs_ref, win_ref, bin_ref, wout_ref, bout_ref, ys_ref, win_s, wout_s):
    b = pl.program_id(0)
    dff = wout_ref.shape[1]

    @pl.when(b < nu_ref[0])
    def _():
        prev = be_ref[jnp.maximum(b - 1, 0)]

        @pl.when((b == 0) | (prev != be_ref[b]))
        def _():
            win_s[...] = win_ref[0].astype(BF16)
            wout_s[...] = wout_ref[0].astype(BF16)

        x = _from_rows(xs_ref, 0, xs_ref.shape[0] // ROW_SUB).astype(BF16)
        gu = _dot(x, win_s[...]) + bin_ref[0]
        gate = jnp.minimum(gu[:, :dff], SWIGLU_LIMIT)
        lin = jnp.clip(gu[:, dff:], -SWIGLU_LIMIT, SWIGLU_LIMIT)
        act = gate * jax.nn.sigmoid(SWIGLU_ALPHA * gate) * (lin + 1.0)
        _to_rows(ys_ref, _dot(act.astype(BF16), wout_s[...]) + bout_ref[0])

    @pl.when(b >= nu_ref[0])
    def _():
        ys_ref[...] = jnp.zeros_like(ys_ref)


def _experts(blk_exp, n_used, xs, layer, w_in, b_in, w_out, b_out):
    n_rows = xs.shape[0] // ROW_SUB
    depth, ne, d, f2 = w_in.shape
    dff = w_out.shape[2]
    tm = MOE_TM
    grid_spec = pltpu.PrefetchScalarGridSpec(
        num_scalar_prefetch=2,
        grid=(n_rows // tm,),
        in_specs=[
            pl.BlockSpec((tm * ROW_SUB, LANES), lambda b, be, nu: (jnp.minimum(b, nu[0] - 1), 0)),
            pl.BlockSpec((None, 1, d, f2), lambda b, be, nu: (layer, be[b], 0, 0)),
            pl.BlockSpec((None, 1, 1, f2), lambda b, be, nu: (layer, be[b], 0, 0)),
            pl.BlockSpec((None, 1, dff, d), lambda b, be, nu: (layer, be[b], 0, 0)),
            pl.BlockSpec((None, 1, 1, d), lambda b, be, nu: (layer, be[b], 0, 0)),
        ],
        out_specs=pl.BlockSpec((tm * ROW_SUB, LANES), lambda b, be, nu: (b, 0)),
        scratch_shapes=[pltpu.VMEM((d, f2), BF16), pltpu.VMEM((dff, d), BF16)],
    )
    return pl.pallas_call(
        _expert_body,
        grid_spec=grid_spec,
        out_shape=jax.ShapeDtypeStruct(xs.shape, F32),
        compiler_params=_cp(("arbitrary",)),
        name="moe_experts",
    )(blk_exp, n_used, xs, w_in, b_in.reshape(depth, ne, 1, f2), w_out, b_out.reshape(depth, ne, 1, d))


def _combine_body(dest_ref, ys_hbm, gt_ref, xl_ref, g2_ref, fg_ref, out_ref, buf, sem, *, tc, final):
    def issue(t, carry):
        for kk in range(TOP_K):
            d = dest_ref[0, 0, t * TOP_K + kk]
            pltpu.make_async_copy(ys_hbm.at[_row_slice(d)], buf.at[_row_slice(kk * tc + t)], sem).start(
                priority=kk % 2)
        return carry

    lax.fori_loop(0, tc, issue, 0, unroll=2)

    def drain(t, carry):
        pltpu.make_async_copy(ys_hbm.at[_row_slice(0)], buf.at[_row_slice(0)], sem).wait()
        return carry

    lax.fori_loop(0, tc * TOP_K, drain, 0, unroll=8)
    gt = gt_ref[...]
    y = gt[:, 0:1] * _from_rows(buf, 0, tc)
    for kk in range(1, TOP_K):
        y = y + gt[:, kk:kk + 1] * _from_rows(buf, kk * tc, (kk + 1) * tc)
    xl = xl_ref[0] + g2_ref[0] * y
    out_ref[0] = _rms(xl, fg_ref[...]) if final else xl


def _combine(dest, ys, gates, xl, g2, fg, *, tc, final):
    bsz, n, d = xl.shape
    t = bsz * n
    nt = n // tc
    dest3 = dest.reshape(t // tc, 1, tc * TOP_K)
    return pl.pallas_call(
        functools.partial(_combine_body, tc=tc, final=final),
        grid=(bsz, nt),
        in_specs=[
            pl.BlockSpec((1, 1, tc * TOP_K), lambda b, i: (b * nt + i, 0, 0), memory_space=pltpu.SMEM),
            pl.BlockSpec(memory_space=pl.ANY),
            pl.BlockSpec((tc, TOP_K), lambda b, i: (b * nt + i, 0)),
            pl.BlockSpec((1, tc, d), lambda b, i: (b, i, 0)),
            pl.BlockSpec((1, 1, d), lambda b, i: (b, 0, 0)),
            pl.BlockSpec((1, d), lambda b, i: (0, 0)),
        ],
        out_specs=pl.BlockSpec((1, tc, d), lambda b, i: (b, i, 0)),
        out_shape=jax.ShapeDtypeStruct((bsz, n, d), F32),
        scratch_shapes=[pltpu.VMEM((TOP_K * tc * ROW_SUB, LANES), F32), pltpu.SemaphoreType.DMA(())],
        compiler_params=_cp(("arbitrary", "arbitrary")),
        name="moe_combine",
    )(dest3, ys, gates, xl, g2, fg)


def _moe(fl, topi, gates, rank, cnt, xl, g2, fg, layer, w_in, b_in, w_out, b_out, *, final):
    t = fl.shape[0] // ROW_SUB
    tm = MOE_TM
    counts = cnt[0, :N_EXPERTS].astype(jnp.int32)
    padded = (counts + tm - 1) // tm * tm
    pad_end = jnp.cumsum(padded)
    pad_start = pad_end - padded
    dest = jnp.take(pad_start, topi) + rank
    nb = t * TOP_K // tm + N_EXPERTS
    blk_start = jnp.arange(nb, dtype=jnp.int32) * tm
    blk_exp = jnp.minimum(jnp.sum((pad_end[None, :] <= blk_start[:, None]).astype(jnp.int32), axis=1), N_EXPERTS - 1)
    n_used = (pad_end[-1:] // tm).astype(jnp.int32)
    xs = _dispatch(pad_end, padded, dest, fl, nb * tm, td=256)
    ys = _experts(blk_exp, n_used, xs, layer, w_in, b_in, w_out, b_out)
    return _combine(dest, ys, gates, xl, g2, fg, tc=256, final=final)


def _hy_in_body(x_ref, xp_ref, xn_ref, g_ref, sh_ref, sc_ref, w_ref, b_ref, cw_ref, cb_ref, o_ref, *, nt):
    i = pl.program_id(2)
    w = w_ref[...]

    def proj(xx):
        h = _rms(xx, g_ref[...]) * (1.0 + sc_ref[0]) + sh_ref[0]
        return _dot(h.astype(BF16), w) + b_ref[...]

    p = proj(x_ref[0])
    tm = p.shape[0]
    ph = proj(jnp.concatenate([xp_ref[0], xn_ref[0]], axis=0))
    prev = jnp.where(i > 0, ph[7:8], 0.0)
    nxt = jnp.where(i < nt - 1, ph[8:9], 0.0)
    row = lax.broadcasted_iota(jnp.int32, (tm, 1), 0)
    up = jnp.where(row == 0, prev, pltpu.roll(p, 1, axis=0))
    dn = jnp.where(row == tm - 1, nxt, pltpu.roll(p, tm - 1, axis=0))
    cw = cw_ref[...]
    o_ref[0, 0] = up * cw[0:1] + p * cw[1:2] + dn * cw[2:3] + cb_ref[...]


def _hy_in(x, g, sh, sc, w, b, cw, cb, *, tm):
    bsz, n, d = x.shape
    nt = n // tm
    hb = tm // 8
    per_b = pl.BlockSpec((1, 1, d), lambda j, bb, i: (bb, 0, 0))
    return pl.pallas_call(
        functools.partial(_hy_in_body, nt=nt),
        grid=(3, bsz, nt),
        in_specs=[
            pl.BlockSpec((1, tm, d), lambda j, bb, i: (bb, i, 0)),
            pl.BlockSpec((1, 8, d), lambda j, bb, i: (bb, jnp.maximum(i * hb - 1, 0), 0)),
            pl.BlockSpec((1, 8, d), lambda j, bb, i: (bb, jnp.minimum((i + 1) * hb, n // 8 - 1), 0)),
            pl.BlockSpec((1, d), lambda j, bb, i: (0, 0)),
            per_b, per_b,
            pl.BlockSpec((d, d), lambda j, bb, i: (0, j)),
            pl.BlockSpec((1, d), lambda j, bb, i: (0, j)),
            pl.BlockSpec((3, d), lambda j, bb, i: (0, j)),
            pl.BlockSpec((1, d), lambda j, bb, i: (0, j)),
        ],
        out_specs=pl.BlockSpec((1, 1, tm, d), lambda j, bb, i: (j, bb, i, 0)),
        out_shape=jax.ShapeDtypeStruct((3, bsz, n, d), F32),
        compiler_params=_cp(("arbitrary", "arbitrary", "arbitrary")),
        name="hyena_in_proj",
    )(x, x, x, g, sh, sc, w, b, cw, cb)


def _filt_feat_body(w1_ref, b1_ref, f1_ref, w2_ref, b2_ref, f2_ref, o_ref, *, n_lat):
    na = o_ref.shape[1]
    a = lax.broadcasted_iota(jnp.int32, (na, 1), 0)
    lane = lax.broadcasted_iota(jnp.int32, (na, LANES), 1)
    band_idx = jnp.where(lane <= HY_BANDS, lane - 1, lane - 1 - HY_BANDS).astype(F32)
    band = 1e-4 + band_idx * ((HY_BANDS - 1 - 1e-4) / (HY_BANDS - 1))
    for j in range(B_GROUP):
        r = a * FFT_N2 + (pl.program_id(0) * B_GROUP + j)
        pos = jnp.where(r < n_lat, r, 2 * n_lat - r).astype(F32)
        tn = pos / float(max(n_lat - 1, 1))
        ang = ((2.0 * math.pi / n_lat) * pos) * band
        z = jnp.where(lane == 0, tn, jnp.where(lane <= HY_BANDS, jnp.cos(ang),
                                               jnp.where(lane < HY_EMB, -jnp.sin(ang), 0.0)))
        h1 = jnp.sin(f1_ref[...] * (_dot_hi(z, w1_ref[...]) + b1_ref[...]))
        h2 = jnp.sin(f2_ref[...] * (_dot_hi(h1, w2_ref[...]) + b2_ref[...]))
        valid = (r != n_lat).astype(F32)
        o_ref[j] = jnp.where(lane == HY_HID, tn, jnp.where(lane == HY_HID + 1, valid, h2))


def _filt_feat(w1, b1, f1, w2, b2, f2, *, n_lat):
    na = 2 * n_lat // FFT_N2
    w1p = jnp.zeros((LANES, LANES), F32).at[:HY_EMB, :HY_HID].set(w1)
    w2p = jnp.zeros((LANES, LANES), F32).at[:HY_HID, :HY_HID].set(w2)
    padv = lambda v: jnp.zeros((1, LANES), F32).at[0, :HY_HID].set(v)
    full = lambda shp: pl.BlockSpec(shp, lambda i: (0,) * len(shp))
    return pl.pallas_call(
        functools.partial(_filt_feat_body, n_lat=n_lat),
        grid=(FFT_N2 // B_GROUP,),
        in_specs=[full((LANES, LANES)), full((1, LANES)), full((1, LANES)),
                  full((LANES, LANES)), full((1, LANES)), full((1, LANES))],
        out_specs=pl.BlockSpec((B_GROUP, na, LANES), lambda i: (i, 0, 0)),
        out_shape=jax.ShapeDtypeStruct((FFT_N2, na, LANES), F32),
        compiler_params=_cp(("arbitrary",)),
        name="hyena_filter_features",
    )(w1p, padv(b1), padv(f1), w2p, padv(b2), padv(f2))


def _filt_s1_body(hd_ref, w3_ref, dec_ref, tab_ref, o_ref):
    na = hd_ref.shape[1]
    ha = na // 2
    o2 = _rows2d(o_ref)
    for j in range(B_GROUP):
        f = hd_ref[j]
        tn = f[:, HY_HID:HY_HID + 1]
        valid = f[:, HY_HID + 1:HY_HID + 2]
        top = _dot3(f[:ha], w3_ref[0, 0]) * jnp.exp(-tn[:ha] * jnp.abs(dec_ref[0, 0]))
        bot = _dot3(f[ha:], w3_ref[0, 1]) * jnp.exp(-tn[ha:] * jnp.abs(dec_ref[0, 1])) * valid[ha:]
        hb = jnp.concatenate([top, bot], axis=0).astype(BF16)
        o2[pl.ds(j, 2 * na, stride=B_GROUP), :] = _dot(tab_ref[j], hb)


def _filt_s1(hd, w3r, dec, tab, *, ct):
    _, na, _ = hd.shape
    d = w3r.shape[-1]
    return pl.pallas_call(
        _filt_s1_body,
        grid=(2, FFT_N2 // B_GROUP, d // ct),
        in_specs=[
            pl.BlockSpec((B_GROUP, na, LANES), lambda o, g, c: (g, 0, 0)),
            pl.BlockSpec((1, 2, LANES, ct), lambda o, g, c: (o, 0, 0, c)),
            pl.BlockSpec((1, 2, 1, ct), lambda o, g, c: (o, 0, 0, c)),
            pl.BlockSpec((B_GROUP, 2 * na, na), lambda o, g, c: (g, 0, 0)),
        ],
        out_specs=pl.BlockSpec((None, 2 * na, B_GROUP, ct), lambda o, g, c: (o, 0, g, c)),
        out_shape=jax.ShapeDtypeStruct((2, 2 * na, FFT_N2, d), F32),
        compiler_params=_cp(("arbitrary", "arbitrary", "arbitrary")),
        name="hyena_filter_dft1",
    )(hd, w3r, dec, tab)


S2_KB = 4


def _s2_body(*refs, conv):
    if conv:
        o_ref, kf_ref, ff_ref, fi_ref, g_ref = refs
    else:
        o_ref, ff_ref, g_ref = refs
    for u in range(S2_KB):
        xin = jnp.concatenate([o_ref[0, u], o_ref[1, u]], axis=0).astype(BF16)
        xf = _dot(ff_ref[...], xin)
        if conv:
            xr, xi = xf[:FFT_N2], xf[FFT_N2:]
            kr = kf_ref[0, u].astype(F32)
            ki = kf_ref[1, u].astype(F32)
            y = jnp.concatenate([xr * kr - xi * ki, xr * ki + xi * kr], axis=0).astype(BF16)
            xf = _dot(fi_ref[...], y)
        g_ref[0, u] = xf[:FFT_N2].astype(g_ref.dtype)
        g_ref[1, u] = xf[FFT_N2:].astype(g_ref.dtype)


def _s2(o4, kf, order, ff, fi, *, ct, conv):
    n1 = o4.shape[-3]
    d = o4.shape[-1]
    full = lambda a: pl.BlockSpec(a.shape, lambda k, c: (0,) * a.ndim)
    nk = n1 // S2_KB
    if conv:
        blk = pl.BlockSpec((2, S2_KB, FFT_N2, ct), lambda k, c: (0, k, 0, c))
        in_specs = [blk, pl.BlockSpec((None, 2, S2_KB, FFT_N2, ct), lambda k, c: (order, 0, k, 0, c)), full(ff),
                    full(fi)]
        args = (o4, kf, ff, fi)
        grid = (nk, d // ct)
        out_specs = blk
        out_dtype = F32
    else:
        no = o4.shape[0]
        blk = pl.BlockSpec((None, 2, S2_KB, FFT_N2, ct), lambda k, c: (k // nk, 0, k % nk, 0, c))
        in_specs = [blk, full(ff)]
        args = (o4, ff)
        grid = (no * nk, d // ct)
        out_specs = blk
        out_dtype = BF16
    return pl.pallas_call(
        functools.partial(_s2_body, conv=conv),
        grid=grid,
        in_specs=in_specs,
        out_specs=out_specs,
        out_shape=jax.ShapeDtypeStruct(o4.shape, out_dtype),
        compiler_params=_cp(("arbitrary", "arbitrary")),
        name="hyena_conv_dft2" if conv else "hyena_filter_dft2",
    )(*args)


def _rows2d(ref):
    lead = ref.shape[:-3]
    return ref.reshape(lead + (ref.shape[-3] * B_GROUP, ref.shape[-1]))


def _s1_body(z_ref, tab_ref, o_ref):
    rows, n_out = z_ref.shape[0], o_ref.shape[0]
    z2, o2 = _rows2d(z_ref), _rows2d(o_ref)
    for j in range(B_GROUP):
        zj = z2[pl.ds(j, rows, stride=B_GROUP), :]
        o2[pl.ds(j, n_out, stride=B_GROUP), :] = _dot(tab_ref[j], zj.astype(BF16))


def _s1(z4, zi, tab, *, ct):
    _, rows, _, d = z4.shape
    n_out = tab.shape[1]
    return pl.pallas_call(
        _s1_body,
        grid=(FFT_N2 // B_GROUP, d // ct),
        in_specs=[
            pl.BlockSpec((None, rows, B_GROUP, ct), lambda g, c: (zi, 0, g, c)),
            pl.BlockSpec((B_GROUP, n_out, rows), lambda g, c: (g, 0, 0)),
        ],
        out_specs=pl.BlockSpec((n_out, B_GROUP, ct), lambda g, c: (0, g, c)),
        out_shape=jax.ShapeDtypeStruct((n_out, FFT_N2, d), F32),
        compiler_params=_cp(("arbitrary", "arbitrary")),
        name="hyena_conv_dft1",
    )(z4, tab)


def _s3_body(g_ref, tab_ref, gate_ref, z_ref, fb_ref, *rest, chain):
    if chain:
        tab1_ref, o_ref, o1_ref = rest
    else:
        (o_ref,) = rest
    n_in, rows = g_ref.shape[0], o_ref.shape[0]
    g2, o2 = _rows2d(g_ref), _rows2d(o_ref)
    for j in range(B_GROUP):
        gj = g2[pl.ds(j, n_in, stride=B_GROUP), :]
        o2[pl.ds(j, rows, stride=B_GROUP), :] = _dot(tab_ref[j], gj.astype(BF16))
    o_ref[...] = gate_ref[...] * (o_ref[...] + z_ref[...] * fb_ref[...])
    if chain:
        n_out = o1_ref.shape[0]
        q2 = _rows2d(o1_ref)
        for j in range(B_GROUP):
            zj = o2[pl.ds(j, rows, stride=B_GROUP), :]
            q2[pl.ds(j, n_out, stride=B_GROUP), :] = _dot(tab1_ref[j], zj.astype(BF16))


def _s3(g3, tab, gate4, gi, z4, zi, fb, tab1, *, ct):
    n_in, _, d = g3.shape
    rows = tab.shape[1]
    chain = tab1 is not None
    blk = pl.BlockSpec((rows, B_GROUP, ct), lambda g, c: (0, g, c))
    in_specs = [
        pl.BlockSpec((n_in, B_GROUP, ct), lambda g, c: (0, g, c)),
        pl.BlockSpec((B_GROUP, rows, n_in), lambda g, c: (g, 0, 0)),
        pl.BlockSpec((None, rows, B_GROUP, ct), lambda g, c: (gi, 0, g, c)),
        pl.BlockSpec((None, rows, B_GROUP, ct), lambda g, c: (zi, 0, g, c)),
        pl.BlockSpec((1, 1, ct), lambda g, c: (0, 0, c)),
    ]
    args = [g3, tab, gate4, z4, fb]
    out_specs = blk
    out_shape = jax.ShapeDtypeStruct((rows, FFT_N2, d), F32)
    if chain:
        n_out = tab1.shape[1]
        in_specs.append(pl.BlockSpec((B_GROUP, n_out, rows), lambda g, c: (g, 0, 0)))
        args.append(tab1)
        out_specs = [blk, pl.BlockSpec((n_out, B_GROUP, ct), lambda g, c: (0, g, c))]
        out_shape = [out_shape, jax.ShapeDtypeStruct((n_out, FFT_N2, d), F32)]
    return pl.pallas_call(
        functools.partial(_s3_body, chain=chain),
        grid=(FFT_N2 // B_GROUP, d // ct),
        in_specs=in_specs,
        out_specs=out_specs,
        out_shape=out_shape,
        compiler_params=_cp(("arbitrary", "arbitrary")),
        name="hyena_conv_idft1_dft1" if chain else "hyena_conv_idft1",
    )(*args)


def _dft_tables(n_lat):
    n = 2 * n_lat
    n1 = n // FFT_N2
    k1 = jnp.arange(n1, dtype=jnp.int32)
    th_a = ((k1[:, None] * k1[None, :]) % n1).astype(F32) * (2.0 * math.pi / n1)
    th_b = (jnp.arange(FFT_N2, dtype=jnp.int32)[:, None] * k1[None, :]).astype(F32) * (2.0 * math.pi / n)
    ca, sa = jnp.cos(th_a)[None], jnp.sin(th_a)[None]
    cb, sb = jnp.cos(th_b)[:, :, None], jnp.sin(th_b)[:, :, None]
    cr = ca * cb - sa * sb
    sn = sa * cb + ca * sb
    ha = n1 // 2
    crh, snh = cr[:, :, :ha], sn[:, :, :ha]
    w1 = jnp.concatenate([jnp.concatenate([crh, snh], axis=2), jnp.concatenate([-snh, crh], axis=2)], axis=1)
    w1f = jnp.concatenate([cr, -sn], axis=1)
    v = jnp.swapaxes(w1, 1, 2) * (1.0 / n)
    k2 = jnp.arange(FFT_N2, dtype=jnp.int32)
    th2 = ((k2[:, None] * k2[None, :]) % FFT_N2).astype(F32) * (2.0 * math.pi / FFT_N2)
    c2, s2 = jnp.cos(th2), jnp.sin(th2)
    ff = jnp.concatenate([jnp.concatenate([c2, s2], axis=1), jnp.concatenate([-s2, c2], axis=1)], axis=0)
    fi = jnp.concatenate([jnp.concatenate([c2, -s2], axis=1), jnp.concatenate([s2, c2], axis=1)], axis=0)
    return w1.astype(BF16), w1f.astype(BF16), v.astype(BF16), ff.astype(BF16), fi.astype(BF16)


def _hyena_mix(proj3, fparams, fbias, *, n_lat):
    _, bsz, _, d = proj3.shape
    f_w1, f_b1, f_f1, f_w2, f_b2, f_f2, f_w3, decay = fparams
    na = 2 * n_lat // FFT_N2
    w1, w1f, v, ff, fi = _dft_tables(n_lat)
    ct = LANES
    hd = _filt_feat(f_w1, f_b1, f_f1, f_w2, f_b2, f_f2, n_lat=n_lat)
    w3r = jnp.transpose(f_w3.reshape(HY_HID, 2, 2, d), (1, 2, 0, 3))
    w3r = jnp.zeros((2, 2, LANES, d), F32).at[:, :, :HY_HID].set(w3r)
    kf1 = _filt_s1(hd, w3r, decay.reshape(2, 2, 1, d), w1f, ct=ct)
    kf = _s2(kf1.reshape(2, 2, na, FFT_N2, d), None, 0, ff, None, ct=d, conv=False)
    p3 = proj3.reshape(3, bsz * (n_lat // FFT_N2), FFT_N2, d)
    o1 = _s1(p3, 2, w1, ct=ct)
    g = _s2(o1.reshape(2, na, FFT_N2, d), kf, 0, ff, fi, ct=d, conv=True)
    z, o1 = _s3(g.reshape(2 * na, FFT_N2, d), v, p3, 0, p3, 2, fbias[0].reshape(1, 1, d), w1, ct=ct)
    g = _s2(o1.reshape(2, na, FFT_N2, d), kf, 1, ff, fi, ct=d, conv=True)
    z = _s3(g.reshape(2 * na, FFT_N2, d), v, p3, 1, z[None], 0, fbias[1].reshape(1, 1, d), None, ct=ct)
    return z.reshape(bsz, n_lat, d)


def _rope_tables(n_tokens):
    rows = n_tokens // GRID_W
    row = jnp.broadcast_to(jnp.arange(rows, dtype=F32)[:, None], (rows, GRID_W)).reshape(-1)
    col = jnp.broadcast_to(jnp.arange(GRID_W, dtype=F32)[None, :], (rows, GRID_W)).reshape(-1)
    axis_dim = QK_ROPE // 2
    inv_freq = 1.0 / (ROPE_THETA ** (jnp.arange(0, axis_dim, 2, dtype=F32) / axis_dim))
    ang = jnp.concatenate([row[:, None] * inv_freq, col[:, None] * inv_freq], axis=-1)
    return jnp.cos(ang), jnp.sin(ang)


def _mla_weights(w_down, g_q, w_uq, g_kv, w_ukv):
    d = w_down.shape[0]
    nh = MLA_HEADS
    kpe = w_down[:, Q_LORA + KV_LORA:]
    w1, w2 = kpe[:, 0::2], kpe[:, 1::2]
    z = jnp.zeros((d, LANES - QK_ROPE), w_down.dtype)
    wd = jnp.concatenate([w_down[:, :Q_LORA + KV_LORA], w1, w2, z, w2, w1, z], axis=1).astype(BF16)
    uq = w_uq.reshape(Q_LORA, nh, QK_NOPE + QK_ROPE)
    pe = uq[:, :, QK_NOPE:]
    uq = jnp.concatenate([uq[:, :, :QK_NOPE], pe[:, :, 0::2], pe[:, :, 1::2]], axis=2)
    wuqT = uq.reshape(Q_LORA, nh * (QK_NOPE + QK_ROPE)).T.astype(BF16)
    ukv = w_ukv.reshape(KV_LORA, nh, QK_NOPE + V_DIM)
    wuk = ukv[:, :, :QK_NOPE].reshape(KV_LORA, nh * QK_NOPE).astype(BF16)
    wuvT = ukv[:, :, QK_NOPE:].reshape(KV_LORA, nh * V_DIM).T.astype(BF16)
    return wd, g_q.reshape(1, -1), g_kv.reshape(1, -1), wuk, wuqT, wuvT


def kernel(x, c, ctx, c_ctx, ada_w, ada_b, norm_mix_g, norm_ffn_g, mla_w_down, mla_g_q, mla_w_uq, mla_g_kv, mla_w_ukv, mla_w_o, hy_w_in, hy_b_in, hy_conv_w, hy_conv_b, hy_f_w1, hy_f_b1, hy_f_freq1, hy_f_w2, hy_f_b2, hy_f_freq2, hy_f_w3, hy_decay, hy_bias, hy_w_out, hy_b_out, moe_w_r, moe_b_r, moe_w_in, moe_b_in, moe_w_out, moe_b_out, final_g):
    bsz, n_lat, d = x.shape
    n_ctx = ctx.shape[1]
    depth = ada_w.shape[0]
    assert bsz == 2 and d == MLA_HEADS * V_DIM and n_lat % 512 == 0 and n_ctx % 128 == 0
    assert depth == 2

    cond8 = jnp.zeros((8, d), F32).at[:bsz].set(c).at[bsz].set(c_ctx)
    mods = _ada(cond8, ada_w, ada_b)

    def mod(i, j, rows):
        return mods[i, rows, j * d:(j + 1) * d][:, None, :]

    lat_rows = slice(0, bsz)
    ctx_rows = slice(bsz, bsz + 1)
    xl = x
    for i in range(depth):
        kind, j = i % 2, i // 2
        sh1, sc1, g1 = (mod(i, m, lat_rows) for m in range(3))
        sh2, sc2, g2 = (mod(i, m, lat_rows) for m in range(3, 6))
        gm = norm_mix_g[i].reshape(1, d)
        if kind == 0:
            wts = _mla_weights(mla_w_down[j], mla_g_q[j], mla_w_uq[j], mla_g_kv[j], mla_w_ukv[j])
            cos, sin = _rope_tables(n_lat)
            zl = jnp.zeros((n_lat, LANES - QK_ROPE), F32)
            tabs = (jnp.concatenate([cos, cos, zl], axis=1), jnp.concatenate([-sin, sin, zl], axis=1), cos.T, sin.T)
            tq = tv = 512
            tk = 2048 if n_lat % 4096 == 0 else 512
            qT, k, vT = _mla_proj(xl, gm, sh1, sc1, wts, tabs, need_q=True, tm=tv, tk=tv)
            half = QK_ROPE // 2
            one_c = jnp.concatenate([jnp.ones((n_ctx, QK_ROPE), F32), jnp.zeros((n_ctx, LANES - QK_ROPE), F32)], axis=1)
            tabs_c = (one_c, jnp.zeros((n_ctx, LANES), F32), jnp.ones((half, n_ctx), F32), jnp.zeros((half, n_ctx), F32))
            kc, vTc = _mla_proj(ctx, gm, mod(i, 0, ctx_rows), mod(i, 1, ctx_rows), wts, tabs_c,
                                need_q=False, tm=n_ctx, tk=n_ctx)
            o = _attention(qT, k, vT, kc, vTc, tq=tq, tk=tk)
            wo = mla_w_o[j].astype(BF16)
            bo = jnp.zeros((1, d), F32)
            transposed = True
        else:
            proj3 = _hy_in(xl, gm, sh1, sc1, hy_w_in[j].astype(BF16), hy_b_in[j].reshape(1, -1), hy_conv_w[j],
                           hy_conv_b[j].reshape(1, -1), tm=512)
            fparams = (hy_f_w1[j], hy_f_b1[j], hy_f_freq1[j], hy_f_w2[j], hy_f_b2[j], hy_f_freq2[j], hy_f_w3[j],
                       hy_decay[j])
            o = _hyena_mix(proj3, fparams, hy_bias[j], n_lat=n_lat)
            wo = hy_w_out[j].astype(BF16)
            bo = hy_b_out[j].reshape(1, d)
            transposed = False
        wr = jnp.zeros((d, LANES), F32).at[:, :N_EXPERTS].set(moe_w_r[i])
        wrh = wr.astype(BF16)
        wrl = (wr - wrh.astype(F32)).astype(BF16)
        br = jnp.zeros((1, LANES), F32).at[0, :N_EXPERTS].set(moe_b_r[i])
        xl, fl, topi, gates, rank, cnt = _post(o, wo, bo, xl, g1, norm_ffn_g[i].reshape(1, d), sh2, sc2, wrh, wrl, br,
                                               transposed=transposed, tm=512)
        xl = _moe(fl, topi, gates, rank, cnt, xl, g2, final_g.reshape(1, d), i, moe_w_in, moe_b_in,
                  moe_w_out, moe_b_out, final=(i == depth - 1))
    return xl
```

```python
import functools
import math

import jax
import jax.numpy as jnp
from jax import lax
from jax.experimental import pallas as pl
from jax.experimental.pallas import tpu as pltpu

F32 = jnp.float32
BF16 = jnp.bfloat16

EPS = 1e-6
GRID_W = 64
MLA_HEADS = 8
QK_NOPE = 128
QK_ROPE = 64
V_DIM = 128
Q_LORA = 512
KV_LORA = 256
ROPE_THETA = 10000.0
MLA_SCALE = (QK_NOPE + QK_ROPE) ** -0.5
QK_PAD = 256

HY_EMB = 33
HY_BANDS = (HY_EMB - 1) // 2
HY_HID = 64
FFT_N2 = 128
B_GROUP = 8

N_EXPERTS = 32
TOP_K = 4
SWIGLU_LIMIT = 7.0
SWIGLU_ALPHA = 1.702
MOE_TM = 256
LANES = 128

VMEM_LIMIT = 56 * 1024 * 1024


def _cp(sem, vmem=VMEM_LIMIT):
    return pltpu.CompilerParams(dimension_semantics=sem, vmem_limit_bytes=vmem)


def _dot(a, b):
    return jnp.dot(a, b, preferred_element_type=F32)


def _dot_hi(a, b):
    return jnp.dot(a, b, preferred_element_type=F32, precision=lax.Precision.HIGHEST)


def _rms(x, g):
    return x * lax.rsqrt(jnp.mean(x * x, axis=-1, keepdims=True) + EPS) * g


def _ada_body(c_ref, w_ref, b_ref, o_ref):
    c = c_ref[...]
    s = c * jax.nn.sigmoid(c)
    o_ref[0] = _dot(s.astype(BF16), w_ref[0].astype(BF16)) + b_ref[0]


def _ada(cond8, ada_w, ada_b):
    depth, d, n = ada_w.shape
    tn = n // 4
    return pl.pallas_call(
        _ada_body,
        grid=(depth, n // tn),
        in_specs=[
            pl.BlockSpec((8, d), lambda i, j: (0, 0)),
            pl.BlockSpec((1, d, tn), lambda i, j: (i, 0, j)),
            pl.BlockSpec((1, 1, tn), lambda i, j: (i, 0, j)),
        ],
        out_specs=pl.BlockSpec((1, 8, tn), lambda i, j: (i, 0, j)),
        out_shape=jax.ShapeDtypeStruct((depth, 8, n), F32),
        compiler_params=_cp(("arbitrary", "arbitrary")),
        name="ada_mod",
    )(cond8, ada_w, ada_b.reshape(depth, 1, n))


def _mla_proj_body(x_ref, g_ref, sh_ref, sc_ref, wd_ref, gq_ref, gkv_ref, wuk_ref, wuqT_ref, wuvT_ref,
                   ct_ref, st_ref, cT_ref, sT_ref, *out_refs, need_q, tk):
    if need_q:
        qT_ref, k_ref, vT_ref = out_refs
    else:
        k_ref, vT_ref = out_refs
    nh = MLA_HEADS
    x = x_ref[0]
    h = _rms(x, g_ref[...]) * (1.0 + sc_ref[0]) + sh_ref[0]
    lat = _dot(h.astype(BF16), wd_ref[...])
    o_kv = Q_LORA
    o_a = Q_LORA + KV_LORA
    kvn = _rms(lat[:, o_kv:o_a], gkv_ref[...])
    kr = (lat[:, o_a:o_a + LANES] * ct_ref[...] + lat[:, o_a + LANES:o_a + 2 * LANES] * st_ref[...]).astype(BF16)
    knope = _dot(kvn.astype(BF16), wuk_ref[...])
    for hh in range(nh):
        k_ref[0, hh, :, 0:QK_NOPE] = knope[:, hh * QK_NOPE:(hh + 1) * QK_NOPE].astype(BF16)
        k_ref[0, hh, :, QK_NOPE:QK_PAD] = kr
    vT = _dot(wuvT_ref[...], kvn.T.astype(BF16))
    tm = x.shape[0]
    for hh in range(nh):
        for c in range(tm // tk):
            vT_ref[0, hh, c] = vT[hh * V_DIM:(hh + 1) * V_DIM, c * tk:(c + 1) * tk].astype(BF16)
    if need_q:
        qn = _rms(lat[:, :Q_LORA], gq_ref[...])
        qT = _dot(wuqT_ref[...], qn.T.astype(BF16)) * (MLA_SCALE * math.log2(math.e))
        c = cT_ref[...]
        s = sT_ref[...]
        hw = QK_NOPE + QK_ROPE
        half = QK_ROPE // 2
        for hh in range(nh):
            base = hh * hw
            x1 = qT[base + QK_NOPE:base + QK_NOPE + half]
            x2 = qT[base + QK_NOPE + half:base + hw]
            qT_ref[0, hh, 0:QK_NOPE] = qT[base:base + QK_NOPE].astype(BF16)
            qT_ref[0, hh, QK_NOPE:QK_NOPE + half] = (x1 * c - x2 * s).astype(BF16)
            qT_ref[0, hh, QK_NOPE + half:hw] = (x1 * s + x2 * c).astype(BF16)
            qT_ref[0, hh, hw:QK_PAD] = jnp.zeros((QK_PAD - hw, tm), BF16)


def _mla_proj(x, g, sh, sc, wts, tabs, *, need_q, tm, tk):
    bsz, n, d = x.shape
    nh = MLA_HEADS
    wd, gq, gkv, wuk, wuqT, wuvT = wts
    ct, st, cT, sT = tabs
    nsh = sh.shape[0]
    full = lambda a: pl.BlockSpec(a.shape, lambda b, i: (0,) * a.ndim)
    in_specs = [
        pl.BlockSpec((1, tm, d), lambda b, i: (b, i, 0)),
        full(g),
        pl.BlockSpec((1, 1, d), lambda b, i: (b % nsh, 0, 0)),
        pl.BlockSpec((1, 1, d), lambda b, i: (b % nsh, 0, 0)),
        full(wd), full(gq), full(gkv), full(wuk), full(wuqT), full(wuvT),
        pl.BlockSpec((tm, LANES), lambda b, i: (i, 0)),
        pl.BlockSpec((tm, LANES), lambda b, i: (i, 0)),
        pl.BlockSpec((QK_ROPE // 2, tm), lambda b, i: (0, i)),
        pl.BlockSpec((QK_ROPE // 2, tm), lambda b, i: (0, i)),
    ]
    out_specs = [
        pl.BlockSpec((1, nh, tm, QK_PAD), lambda b, i: (b, 0, i, 0)),
        pl.BlockSpec((1, nh, tm // tk, V_DIM, tk), lambda b, i: (b, 0, i, 0, 0)),
    ]
    out_shape = [
        jax.ShapeDtypeStruct((bsz, nh, n, QK_PAD), BF16),
        jax.ShapeDtypeStruct((bsz, nh, n // tk, V_DIM, tk), BF16),
    ]
    if need_q:
        out_specs = [pl.BlockSpec((1, nh, QK_PAD, tm), lambda b, i: (b, 0, 0, i))] + out_specs
        out_shape = [jax.ShapeDtypeStruct((bsz, nh, QK_PAD, n), BF16)] + out_shape
    return pl.pallas_call(
        functools.partial(_mla_proj_body, need_q=need_q, tk=tk),
        grid=(bsz, n // tm),
        in_specs=in_specs,
        out_specs=out_specs,
        out_shape=out_shape,
        compiler_params=_cp(("arbitrary", "arbitrary")),
        name="mla_proj_q" if need_q else "mla_proj_ctx",
    )(x, g, sh, sc, wd, gq, gkv, wuk, wuqT, wuvT, ct, st, cT, sT)


SM_STRIP = 64
SUBLANES = 8


def _attn_body(qT_ref, k_ref, vT_ref, kc_ref, vTc_ref, o_ref, s0, s1, p0, p1, sc, pc, acc, m_scr, x0, x1, xc,
               a0, a1, ac, l_scr, d0, d1, dc, *, tk):
    nchunk = k_ref.shape[2] // tk

    def scores(kblk, s_ref, mx_ref):
        r = _dot(kblk, qT_ref[0, 0])
        s_ref[...] = r
        mx_ref[...] = jnp.max(r, axis=0, keepdims=True)

    def probs(s_ref, mx_ref, p_ref, a_ref, d_ref):
        m_old = m_scr[...]
        m_new = jnp.maximum(m_old, mx_ref[...])
        m_scr[...] = m_new
        alpha = jnp.exp2(m_old - m_new)
        a_ref[...] = alpha
        part = None
        for r in range(0, s_ref.shape[0], SM_STRIP):
            p = jnp.exp2(s_ref[r:r + SM_STRIP] - m_new)
            p_ref[r:r + SM_STRIP] = p.astype(BF16)
            ps = jnp.sum(p.reshape(SM_STRIP // SUBLANES, SUBLANES, p.shape[1]), axis=0)
            part = ps if part is None else part + ps
        d_ref[...] = part

    def accumulate(p_ref, a_ref, d_ref, vblk):
        acc[...] = a_ref[...] * acc[...] + _dot(vblk, p_ref[...])
        l_scr[...] = a_ref[...] * l_scr[...] + d_ref[...]

    def kchunk(i):
        return k_ref[0, 0, pl.ds(pl.multiple_of(i * tk, tk), tk), :]

    def vchunk(i):
        nsub = tk // vT_ref.shape[-1]
        return jnp.concatenate([vT_ref[0, 0, i * nsub + u] for u in range(nsub)], axis=1)

    m_scr[...] = jnp.full(m_scr.shape, -jnp.inf, F32)
    acc[...] = jnp.zeros(acc.shape, F32)
    l_scr[...] = jnp.zeros(l_scr.shape, F32)
    scores(kc_ref[0, 0], sc, xc)
    scores(kchunk(0), s0, x0)
    probs(sc, xc, pc, ac, dc)
    scores(kchunk(1), s1, x1)
    accumulate(pc, ac, dc, vTc_ref[0, 0, 0])
    probs(s0, x0, p0, a0, d0)

    def body(j, carry):
        t = 2 * j
        scores(kchunk(t + 2), s0, x0)
        accumulate(p0, a0, d0, vchunk(t))
        probs(s1, x1, p1, a1, d1)
        scores(kchunk(t + 3), s1, x1)
        accumulate(p1, a1, d1, vchunk(t + 1))
        probs(s0, x0, p0, a0, d0)
        return carry

    lax.fori_loop(0, nchunk // 2 - 1, body, 0)
    accumulate(p0, a0, d0, vchunk(nchunk - 2))
    probs(s1, x1, p1, a1, d1)
    accumulate(p1, a1, d1, vchunk(nchunk - 1))
    o_ref[0, 0] = (acc[...] / jnp.sum(l_scr[...], axis=0, keepdims=True)).astype(BF16)


def _attention(qT, k, vT, kc, vTc, *, tq, tk):
    bsz, nh, _, n = qT.shape
    nc = kc.shape[2]
    tv = vT.shape[-1]
    assert (n // tk) % 2 == 0 and tk % tv == 0
    return pl.pallas_call(
        functools.partial(_attn_body, tk=tk),
        grid=(bsz, nh, n // tq),
        in_specs=[
            pl.BlockSpec((1, 1, QK_PAD, tq), lambda b, h, i: (b, h, 0, i)),
            pl.BlockSpec((1, 1, n, QK_PAD), lambda b, h, i: (b, h, 0, 0)),
            pl.BlockSpec((1, 1, n // tv, V_DIM, tv), lambda b, h, i: (b, h, 0, 0, 0)),
            pl.BlockSpec((1, 1, nc, QK_PAD), lambda b, h, i: (b, h, 0, 0)),
            pl.BlockSpec((1, 1, 1, V_DIM, nc), lambda b, h, i: (b, h, 0, 0, 0)),
        ],
        out_specs=pl.BlockSpec((1, 1, V_DIM, tq), lambda b, h, i: (b, h, 0, i)),
        out_shape=jax.ShapeDtypeStruct((bsz, nh, V_DIM, n), BF16),
        scratch_shapes=[pltpu.VMEM((tk, tq), F32), pltpu.VMEM((tk, tq), F32),
                        pltpu.VMEM((tk, tq), BF16), pltpu.VMEM((tk, tq), BF16),
                        pltpu.VMEM((nc, tq), F32), pltpu.VMEM((nc, tq), BF16),
                        pltpu.VMEM((V_DIM, tq), F32), pltpu.VMEM((1, tq), F32),
                        pltpu.VMEM((1, tq), F32), pltpu.VMEM((1, tq), F32), pltpu.VMEM((1, tq), F32),
                        pltpu.VMEM((1, tq), F32), pltpu.VMEM((1, tq), F32), pltpu.VMEM((1, tq), F32),
                        pltpu.VMEM((SUBLANES, tq), F32), pltpu.VMEM((SUBLANES, tq), F32),
                        pltpu.VMEM((SUBLANES, tq), F32), pltpu.VMEM((SUBLANES, tq), F32)],
        compiler_params=_cp(("arbitrary", "arbitrary", "arbitrary")),
        name="mla_attention",
    )(qT, k, vT, kc, vTc)


def _post_body(o_ref, wo_ref, bo_ref, x_ref, g1_ref, gf_ref, sh_ref, sc_ref, wrh_ref, wrl_ref, br_ref, tri_ref,
               xl_ref, fl_ref, ti_ref, gt_ref, rk_ref, cnt_ref, *, transposed):
    @pl.when((pl.program_id(0) == 0) & (pl.program_id(1) == 0))
    def _():
        cnt_ref[...] = jnp.zeros_like(cnt_ref)

    tm = x_ref.shape[1]
    if transposed:
        oT = o_ref[0].astype(F32).reshape(MLA_HEADS * V_DIM, tm)
        o = oT.T.astype(BF16)
    else:
        o = o_ref[0].astype(BF16)
    y = _dot(o, wo_ref[...]) + bo_ref[...]
    xl = x_ref[0] + g1_ref[0] * y
    xl_ref[0] = xl
    fl = _rms(xl, gf_ref[...]) * (1.0 + sc_ref[0]) + sh_ref[0]
    _to_rows(fl_ref, fl)
    flh = fl.astype(BF16)
    fll = (fl - flh.astype(F32)).astype(BF16)
    logits = _dot(flh, wrh_ref[...]) + (_dot(fll, wrh_ref[...]) + _dot(flh, wrl_ref[...])) + br_ref[...]
    lane = lax.broadcasted_iota(jnp.int32, (tm, LANES), 1).astype(F32)
    neg = jnp.float32(-jnp.inf)
    work = jnp.where(lane < N_EXPERTS, logits, neg)
    vals, idxs = [], []
    onehot = jnp.zeros((tm, LANES), F32)
    for _ in range(TOP_K):
        mk = jnp.max(work, axis=-1, keepdims=True)
        ik = jnp.min(jnp.where(work == mk, lane, float(LANES)), axis=-1, keepdims=True)
        sel = lane == ik
        onehot = jnp.where(sel, 1.0, onehot)
        work = jnp.where(sel, neg, work)
        vals.append(mk)
        idxs.append(ik)
    es = [jnp.exp(v - vals[0]) for v in vals]
    den = es[0] + es[1] + es[2] + es[3]
    pre = _dot(tri_ref[...], onehot.astype(BF16)) + cnt_ref[...]
    ti = jnp.zeros((tm, LANES), F32)
    gt = jnp.zeros((tm, LANES), F32)
    rk = jnp.zeros((tm, LANES), F32)
    for kk in range(TOP_K):
        rank = jnp.sum(jnp.where(lane == idxs[kk], pre, 0.0), axis=-1, keepdims=True)
        ti = jnp.where(lane == kk, idxs[kk], ti)
        gt = jnp.where(lane == kk, es[kk] / den, gt)
        rk = jnp.where(lane == kk, rank, rk)
    ti_ref[...] = ti[:, :TOP_K].astype(jnp.int32)
    gt_ref[...] = gt[:, :TOP_K]
    rk_ref[...] = rk[:, :TOP_K].astype(jnp.int32)
    cnt_ref[...] += jnp.sum(onehot, axis=0, keepdims=True)


def _post(o, wo, bo, x, g1, gf, sh, sc, wrh, wrl, br, *, transposed, tm):
    bsz, n, d = x.shape
    t = bsz * n
    nt = n // tm
    tri = (lax.broadcasted_iota(jnp.int32, (tm, tm), 0) > lax.broadcasted_iota(jnp.int32, (tm, tm), 1)).astype(BF16)
    full = lambda a: pl.BlockSpec(a.shape, lambda b, i: (0,) * a.ndim)
    per_b = pl.BlockSpec((1, 1, d), lambda b, i: (b, 0, 0))
    if transposed:
        o_spec = pl.BlockSpec((1, MLA_HEADS, V_DIM, tm), lambda b, i: (b, 0, 0, i))
    else:
        o_spec = pl.BlockSpec((1, tm, d), lambda b, i: (b, i, 0))
    tok = lambda w: pl.BlockSpec((tm, w), lambda b, i: (b * nt + i, 0))
    return pl.pallas_call(
        functools.partial(_post_body, transposed=transposed),
        grid=(bsz, nt),
        in_specs=[o_spec, full(wo), full(bo), pl.BlockSpec((1, tm, d), lambda b, i: (b, i, 0)), per_b, full(gf),
                  per_b, per_b, full(wrh), full(wrl), full(br), full(tri)],
        out_specs=[pl.BlockSpec((1, tm, d), lambda b, i: (b, i, 0)),
                   pl.BlockSpec((tm * ROW_SUB, LANES), lambda b, i: (b * nt + i, 0)),
                   tok(TOP_K), tok(TOP_K), tok(TOP_K), pl.BlockSpec((1, LANES), lambda b, i: (0, 0))],
        out_shape=[jax.ShapeDtypeStruct((bsz, n, d), F32), jax.ShapeDtypeStruct((t * ROW_SUB, LANES), F32),
                   jax.ShapeDtypeStruct((t, TOP_K), jnp.int32), jax.ShapeDtypeStruct((t, TOP_K), F32),
                   jax.ShapeDtypeStruct((t, TOP_K), jnp.int32), jax.ShapeDtypeStruct((1, LANES), F32)],
        compiler_params=_cp(("arbitrary", "arbitrary")),
        name="post_attn" if transposed else "post_hyena",
    )(o, wo, bo, x, g1, gf, sh, sc, wrh, wrl, br, tri)


ROW_SUB = 8


def _row_slice(i):
    return pl.ds(pl.multiple_of(i * ROW_SUB, ROW_SUB), ROW_SUB)


def _to_rows(ref, x):
    for s in range(ROW_SUB):
        ref[pl.ds(s, x.shape[0], stride=ROW_SUB), :] = x[:, s * LANES:(s + 1) * LANES]


def _from_rows(ref, lo, hi):
    return jnp.concatenate([ref[pl.ds(lo * ROW_SUB + s, hi - lo, stride=ROW_SUB), :] for s in range(ROW_SUB)], axis=1)


def _dispatch_body(pe_ref, pd_ref, dest_ref, fl_ref, xs_out, zbuf, sem, *, td):
    @pl.when(pl.program_id(0) == 0)
    def _():
        zbuf[...] = jnp.zeros(zbuf.shape, zbuf.dtype)
        for e in range(N_EXPERTS):
            @pl.when(pd_ref[e] > 0)
            def _():
                start = pl.multiple_of((pe_ref[e] - MOE_TM) * ROW_SUB, ROW_SUB)
                cp = pltpu.make_async_copy(zbuf, xs_out.at[pl.ds(start, MOE_TM * ROW_SUB)], sem)
                cp.start()
                cp.wait()

    def issue(t, carry):
        for kk in range(TOP_K):
            d = dest_ref[0, 0, t * TOP_K + kk]
            pltpu.make_async_copy(fl_ref.at[_row_slice(t)], xs_out.at[_row_slice(d)], sem).start(priority=kk % 2)
        return carry

    lax.fori_loop(0, td, issue, 0, unroll=2)

    def drain(t, carry):
        pltpu.make_async_copy(fl_ref.at[_row_slice(0)], xs_out.at[_row_slice(0)], sem).wait()
        return carry

    lax.fori_loop(0, td * TOP_K, drain, 0, unroll=8)


def _dispatch(pad_end, padded, dest, fl, n_rows, *, td):
    t = fl.shape[0] // ROW_SUB
    dest3 = dest.reshape(t // td, 1, td * TOP_K)
    grid_spec = pltpu.PrefetchScalarGridSpec(
        num_scalar_prefetch=2,
        grid=(t // td,),
        in_specs=[
            pl.BlockSpec((1, 1, td * TOP_K), lambda i, pe, pd: (i, 0, 0), memory_space=pltpu.SMEM),
            pl.BlockSpec((td * ROW_SUB, LANES), lambda i, pe, pd: (i, 0)),
        ],
        out_specs=pl.BlockSpec(memory_space=pl.ANY),
        scratch_shapes=[pltpu.VMEM((MOE_TM * ROW_SUB, LANES), fl.dtype), pltpu.SemaphoreType.DMA(())],
    )
    return pl.pallas_call(
        functools.partial(_dispatch_body, td=td),
        grid_spec=grid_spec,
        out_shape=jax.ShapeDtypeStruct((n_rows * ROW_SUB, LANES), fl.dtype),
        compiler_params=_cp(("arbitrary",)),
        name="moe_dispatch",
    )(pad_end, padded, dest3, fl)


def _expert_body(be_ref, nu_ref, xs_ref, win_ref, bin_ref, wout_ref, bout_ref, ys_ref, win_s, wout_s):
    b = pl.program_id(0)
    dff = wout_ref.shape[1]

    @pl.when(b < nu_ref[0])
    def _():
        prev = be_ref[jnp.maximum(b - 1, 0)]

        @pl.when((b == 0) | (prev != be_ref[b]))
        def _():
            win_s[...] = win_ref[0].astype(BF16)
            wout_s[...] = wout_ref[0].astype(BF16)

        x = _from_rows(xs_ref, 0, xs_ref.shape[0] // ROW_SUB).astype(BF16)
        gu = _dot(x, win_s[...]) + bin_ref[0]
        gate = jnp.minimum(gu[:, :dff], SWIGLU_LIMIT)
        lin = jnp.clip(gu[:, dff:], -SWIGLU_LIMIT, SWIGLU_LIMIT)
        act = gate * jax.nn.sigmoid(SWIGLU_ALPHA * gate) * (lin + 1.0)
        _to_rows(ys_ref, _dot(act.astype(BF16), wout_s[...]) + bout_ref[0])

    @pl.when(b >= nu_ref[0])
    def _():
        ys_ref[...] = jnp.zeros_like(ys_ref)


def _experts(blk_exp, n_used, xs, layer, w_in, b_in, w_out, b_out):
    n_rows = xs.shape[0] // ROW_SUB
    depth, ne, d, f2 = w_in.shape
    dff = w_out.shape[2]
    tm = MOE_TM
    grid_spec = pltpu.PrefetchScalarGridSpec(
        num_scalar_prefetch=2,
        grid=(n_rows // tm,),
        in_specs=[
            pl.BlockSpec((tm * ROW_SUB, LANES), lambda b, be, nu: (jnp.minimum(b, nu[0] - 1), 0)),
            pl.BlockSpec((None, 1, d, f2), lambda b, be, nu: (layer, be[b], 0, 0)),
            pl.BlockSpec((None, 1, 1, f2), lambda b, be, nu: (layer, be[b], 0, 0)),
            pl.BlockSpec((None, 1, dff, d), lambda b, be, nu: (layer, be[b], 0, 0)),
            pl.BlockSpec((None, 1, 1, d), lambda b, be, nu: (layer, be[b], 0, 0)),
        ],
        out_specs=pl.BlockSpec((tm * ROW_SUB, LANES), lambda b, be, nu: (b, 0)),
        scratch_shapes=[pltpu.VMEM((d, f2), BF16), pltpu.VMEM((dff, d), BF16)],
    )
    return pl.pallas_call(
        _expert_body,
        grid_spec=grid_spec,
        out_shape=jax.ShapeDtypeStruct(xs.shape, F32),
        compiler_params=_cp(("arbitrary",)),
        name="moe_experts",
    )(blk_exp, n_used, xs, w_in, b_in.reshape(depth, ne, 1, f2), w_out, b_out.reshape(depth, ne, 1, d))


def _combine_body(dest_ref, ys_hbm, gt_ref, xl_ref, g2_ref, fg_ref, out_ref, buf, sem, *, tc, final):
    def issue(t, carry):
        for kk in range(TOP_K):
            d = dest_ref[0, 0, t * TOP_K + kk]
            pltpu.make_async_copy(ys_hbm.at[_row_slice(d)], buf.at[_row_slice(kk * tc + t)], sem).start(
                priority=kk % 2)
        return carry

    lax.fori_loop(0, tc, issue, 0, unroll=2)

    def drain(t, carry):
        pltpu.make_async_copy(ys_hbm.at[_row_slice(0)], buf.at[_row_slice(0)], sem).wait()
        return carry

    lax.fori_loop(0, tc * TOP_K, drain, 0, unroll=8)
    gt = gt_ref[...]
    y = gt[:, 0:1] * _from_rows(buf, 0, tc)
    for kk in range(1, TOP_K):
        y = y + gt[:, kk:kk + 1] * _from_rows(buf, kk * tc, (kk + 1) * tc)
    xl = xl_ref[0] + g2_ref[0] * y
    out_ref[0] = _rms(xl, fg_ref[...]) if final else xl


def _combine(dest, ys, gates, xl, g2, fg, *, tc, final):
    bsz, n, d = xl.shape
    t = bsz * n
    nt = n // tc
    dest3 = dest.reshape(t // tc, 1, tc * TOP_K)
    return pl.pallas_call(
        functools.partial(_combine_body, tc=tc, final=final),
        grid=(bsz, nt),
        in_specs=[
            pl.BlockSpec((1, 1, tc * TOP_K), lambda b, i: (b * nt + i, 0, 0), memory_space=pltpu.SMEM),
            pl.BlockSpec(memory_space=pl.ANY),
            pl.BlockSpec((tc, TOP_K), lambda b, i: (b * nt + i, 0)),
            pl.BlockSpec((1, tc, d), lambda b, i: (b, i, 0)),
            pl.BlockSpec((1, 1, d), lambda b, i: (b, 0, 0)),
            pl.BlockSpec((1, d), lambda b, i: (0, 0)),
        ],
        out_specs=pl.BlockSpec((1, tc, d), lambda b, i: (b, i, 0)),
        out_shape=jax.ShapeDtypeStruct((bsz, n, d), F32),
        scratch_shapes=[pltpu.VMEM((TOP_K * tc * ROW_SUB, LANES), F32), pltpu.SemaphoreType.DMA(())],
        compiler_params=_cp(("arbitrary", "arbitrary")),
        name="moe_combine",
    )(dest3, ys, gates, xl, g2, fg)


def _moe(fl, topi, gates, rank, cnt, xl, g2, fg, layer, w_in, b_in, w_out, b_out, *, final):
    t = fl.shape[0] // ROW_SUB
    tm = MOE_TM
    counts = cnt[0, :N_EXPERTS].astype(jnp.int32)
    padded = (counts + tm - 1) // tm * tm
    pad_end = jnp.cumsum(padded)
    pad_start = pad_end - padded
    dest = jnp.take(pad_start, topi) + rank
    nb = t * TOP_K // tm + N_EXPERTS
    blk_start = jnp.arange(nb, dtype=jnp.int32) * tm
    blk_exp = jnp.minimum(jnp.sum((pad_end[None, :] <= blk_start[:, None]).astype(jnp.int32), axis=1), N_EXPERTS - 1)
    n_used = (pad_end[-1:] // tm).astype(jnp.int32)
    xs = _dispatch(pad_end, padded, dest, fl, nb * tm, td=256)
    ys = _experts(blk_exp, n_used, xs, layer, w_in, b_in, w_out, b_out)
    return _combine(dest, ys, gates, xl, g2, fg, tc=256, final=final)


def _hy_in_body(x_ref, xp_ref, xn_ref, g_ref, sh_ref, sc_ref, w_ref, b_ref, cw_ref, cb_ref, o_ref, *, nt):
    i = pl.program_id(2)
    w = w_ref[...]

    def proj(xx):
        h = _rms(xx, g_ref[...]) * (1.0 + sc_ref[0]) + sh_ref[0]
        return _dot(h.astype(BF16), w) + b_ref[...]

    p = proj(x_ref[0])
    tm = p.shape[0]
    ph = proj(jnp.concatenate([xp_ref[0], xn_ref[0]], axis=0))
    prev = jnp.where(i > 0, ph[7:8], 0.0)
    nxt = jnp.where(i < nt - 1, ph[8:9], 0.0)
    row = lax.broadcasted_iota(jnp.int32, (tm, 1), 0)
    up = jnp.where(row == 0, prev, pltpu.roll(p, 1, axis=0))
    dn = jnp.where(row == tm - 1, nxt, pltpu.roll(p, tm - 1, axis=0))
    cw = cw_ref[...]
    o_ref[0, 0] = up * cw[0:1] + p * cw[1:2] + dn * cw[2:3] + cb_ref[...]


def _hy_in(x, g, sh, sc, w, b, cw, cb, *, tm):
    bsz, n, d = x.shape
    nt = n // tm
    hb = tm // 8
    per_b = pl.BlockSpec((1, 1, d), lambda j, bb, i: (bb, 0, 0))
    return pl.pallas_call(
        functools.partial(_hy_in_body, nt=nt),
        grid=(3, bsz, nt),
        in_specs=[
            pl.BlockSpec((1, tm, d), lambda j, bb, i: (bb, i, 0)),
            pl.BlockSpec((1, 8, d), lambda j, bb, i: (bb, jnp.maximum(i * hb - 1, 0), 0)),
            pl.BlockSpec((1, 8, d), lambda j, bb, i: (bb, jnp.minimum((i + 1) * hb, n // 8 - 1), 0)),
            pl.BlockSpec((1, d), lambda j, bb, i: (0, 0)),
            per_b, per_b,
            pl.BlockSpec((d, d), lambda j, bb, i: (0, j)),
            pl.BlockSpec((1, d), lambda j, bb, i: (0, j)),
            pl.BlockSpec((3, d), lambda j, bb, i: (0, j)),
            pl.BlockSpec((1, d), lambda j, bb, i: (0, j)),
        ],
        out_specs=pl.BlockSpec((1, 1, tm, d), lambda j, bb, i: (j, bb, i, 0)),
        out_shape=jax.ShapeDtypeStruct((3, bsz, n, d), F32),
        compiler_params=_cp(("arbitrary", "arbitrary", "arbitrary")),
        name="hyena_in_proj",
    )(x, x, x, g, sh, sc, w, b, cw, cb)


def _filt_feat_body(w1_ref, b1_ref, f1_ref, w2_ref, b2_ref, f2_ref, o_ref, *, n_lat):
    na = o_ref.shape[1]
    a = lax.broadcasted_iota(jnp.int32, (na, 1), 0)
    lane = lax.broadcasted_iota(jnp.int32, (na, LANES), 1)
    band_idx = jnp.where(lane <= HY_BANDS, lane - 1, lane - 1 - HY_BANDS).astype(F32)
    band = 1e-4 + band_idx * ((HY_BANDS - 1 - 1e-4) / (HY_BANDS - 1))
    for j in range(B_GROUP):
        r = a * FFT_N2 + (pl.program_id(0) * B_GROUP + j)
        pos = jnp.where(r < n_lat, r, 2 * n_lat - r).astype(F32)
        tn = pos / float(max(n_lat - 1, 1))
        ang = ((2.0 * math.pi / n_lat) * pos) * band
        z = jnp.where(lane == 0, tn, jnp.where(lane <= HY_BANDS, jnp.cos(ang),
                                               jnp.where(lane < HY_EMB, -jnp.sin(ang), 0.0)))
        h1 = jnp.sin(f1_ref[...] * (_dot_hi(z, w1_ref[...]) + b1_ref[...]))
        h2 = jnp.sin(f2_ref[...] * (_dot_hi(h1, w2_ref[...]) + b2_ref[...]))
        valid = (r != n_lat).astype(F32)
        o_ref[j] = jnp.where(lane == HY_HID, tn, jnp.where(lane == HY_HID + 1, valid, h2))


def _filt_feat(w1, b1, f1, w2, b2, f2, *, n_lat):
    na = 2 * n_lat // FFT_N2
    w1p = jnp.zeros((LANES, LANES), F32).at[:HY_EMB, :HY_HID].set(w1)
    w2p = jnp.zeros((LANES, LANES), F32).at[:HY_HID, :HY_HID].set(w2)
    padv = lambda v: jnp.zeros((1, LANES), F32).at[0, :HY_HID].set(v)
    full = lambda shp: pl.BlockSpec(shp, lambda i: (0,) * len(shp))
    return pl.pallas_call(
        functools.partial(_filt_feat_body, n_lat=n_lat),
        grid=(FFT_N2 // B_GROUP,),
        in_specs=[full((LANES, LANES)), full((1, LANES)), full((1, LANES)),
                  full((LANES, LANES)), full((1, LANES)), full((1, LANES))],
        out_specs=pl.BlockSpec((B_GROUP, na, LANES), lambda i: (i, 0, 0)),
        out_shape=jax.ShapeDtypeStruct((FFT_N2, na, LANES), F32),
        compiler_params=_cp(("arbitrary",)),
        name="hyena_filter_features",
    )(w1p, padv(b1), padv(f1), w2p, padv(b2), padv(f2))


def _filt_s1_body(hd_ref, w3_ref, dec_ref, tab_ref, o_ref):
    na = hd_ref.shape[1]
    ha = na // 2
    o2 = _rows2d(o_ref)
    ft = hd_ref[:, :ha, :].reshape(B_GROUP * ha, LANES)
    fb = hd_ref[:, ha:, :].reshape(B_GROUP * ha, LANES)
    top = _dot(ft.astype(BF16), w3_ref[0, 0].astype(BF16))
    top = top * jnp.exp(-ft[:, HY_HID:HY_HID + 1] * jnp.abs(dec_ref[0, 0]))
    bot = _dot(fb.astype(BF16), w3_ref[0, 1].astype(BF16))
    bot = bot * (jnp.exp(-fb[:, HY_HID:HY_HID + 1] * jnp.abs(dec_ref[0, 1])) * fb[:, HY_HID + 1:HY_HID + 2])
    for j in range(B_GROUP):
        hb = jnp.concatenate([top[j * ha:(j + 1) * ha], bot[j * ha:(j + 1) * ha]], axis=0).astype(BF16)
        o2[pl.ds(j, 2 * na, stride=B_GROUP), :] = _dot(tab_ref[j], hb)


def _filt_s1(hd, w3r, dec, tab, *, ct):
    _, na, _ = hd.shape
    d = w3r.shape[-1]
    return pl.pallas_call(
        _filt_s1_body,
        grid=(2, FFT_N2 // B_GROUP, d // ct),
        in_specs=[
            pl.BlockSpec((B_GROUP, na, LANES), lambda o, g, c: (g, 0, 0)),
            pl.BlockSpec((1, 2, LANES, ct), lambda o, g, c: (o, 0, 0, c)),
            pl.BlockSpec((1, 2, 1, ct), lambda o, g, c: (o, 0, 0, c)),
            pl.BlockSpec((B_GROUP, 2 * na, na), lambda o, g, c: (g, 0, 0)),
        ],
        out_specs=pl.BlockSpec((None, 2 * na, B_GROUP, ct), lambda o, g, c: (o, 0, g, c)),
        out_shape=jax.ShapeDtypeStruct((2, 2 * na, FFT_N2, d), F32),
        compiler_params=_cp(("arbitrary", "arbitrary", "arbitrary")),
        name="hyena_filter_dft1",
    )(hd, w3r, dec, tab)


S2_KB = 4


def _s2_body(*refs, conv):
    if conv:
        o_ref, kf_ref, ff_ref, fi_ref, g_ref = refs
    else:
        o_ref, ff_ref, g_ref = refs
    for u in range(S2_KB):
        xin = jnp.concatenate([o_ref[0, u], o_ref[1, u]], axis=0).astype(BF16)
        xf = _dot(ff_ref[...], xin)
        if conv:
            xr, xi = xf[:FFT_N2], xf[FFT_N2:]
            kr = kf_ref[0, u].astype(F32)
            ki = kf_ref[1, u].astype(F32)
            y = jnp.concatenate([xr * kr - xi * ki, xr * ki + xi * kr], axis=0).astype(BF16)
            xf = _dot(fi_ref[...], y)
        g_ref[0, u] = xf[:FFT_N2].astype(g_ref.dtype)
        g_ref[1, u] = xf[FFT_N2:].astype(g_ref.dtype)


def _s2(o4, kf, order, ff, fi, *, ct, conv):
    n1 = o4.shape[-3]
    d = o4.shape[-1]
    full = lambda a: pl.BlockSpec(a.shape, lambda k, c: (0,) * a.ndim)
    nk = n1 // S2_KB
    if conv:
        blk = pl.BlockSpec((2, S2_KB, FFT_N2, ct), lambda k, c: (0, k, 0, c))
        in_specs = [blk, pl.BlockSpec((None, 2, S2_KB, FFT_N2, ct), lambda k, c: (order, 0, k, 0, c)), full(ff),
                    full(fi)]
        args = (o4, kf, ff, fi)
        grid = (nk, d // ct)
        out_specs = blk
        out_dtype = F32
    else:
        no = o4.shape[0]
        blk = pl.BlockSpec((None, 2, S2_KB, FFT_N2, ct), lambda k, c: (k // nk, 0, k % nk, 0, c))
        in_specs = [blk, full(ff)]
        args = (o4, ff)
        grid = (no * nk, d // ct)
        out_specs = blk
        out_dtype = BF16
    return pl.pallas_call(
        functools.partial(_s2_body, conv=conv),
        grid=grid,
        in_specs=in_specs,
        out_specs=out_specs,
        out_shape=jax.ShapeDtypeStruct(o4.shape, out_dtype),
        compiler_params=_cp(("arbitrary", "arbitrary")),
        name="hyena_conv_dft2" if conv else "hyena_filter_dft2",
    )(*args)


def _rows2d(ref):
    lead = ref.shape[:-3]
    return ref.reshape(lead + (ref.shape[-3] * B_GROUP, ref.shape[-1]))


def _s1_body(z_ref, tab_ref, o_ref):
    rows, n_out = z_ref.shape[0], o_ref.shape[0]
    z2, o2 = _rows2d(z_ref), _rows2d(o_ref)
    for j in range(B_GROUP):
        zj = z2[pl.ds(j, rows, stride=B_GROUP), :]
        o2[pl.ds(j, n_out, stride=B_GROUP), :] = _dot(tab_ref[j], zj.astype(BF16))


def _s1(z4, zi, tab, *, ct):
    _, rows, _, d = z4.shape
    n_out = tab.shape[1]
    return pl.pallas_call(
        _s1_body,
        grid=(FFT_N2 // B_GROUP, d // ct),
        in_specs=[
            pl.BlockSpec((None, rows, B_GROUP, ct), lambda g, c: (zi, 0, g, c)),
            pl.BlockSpec((B_GROUP, n_out, rows), lambda g, c: (g, 0, 0)),
        ],
        out_specs=pl.BlockSpec((n_out, B_GROUP, ct), lambda g, c: (0, g, c)),
        out_shape=jax.ShapeDtypeStruct((n_out, FFT_N2, d), F32),
        compiler_params=_cp(("arbitrary", "arbitrary")),
        name="hyena_conv_dft1",
    )(z4, tab)


def _s3_body(g_ref, tab_ref, gate_ref, z_ref, fb_ref, *rest, chain):
    if chain:
        tab1_ref, o_ref, o1_ref = rest
    else:
        (o_ref,) = rest
    n_in, rows = g_ref.shape[0], o_ref.shape[0]
    g2, o2 = _rows2d(g_ref), _rows2d(o_ref)
    for j in range(B_GROUP):
        gj = g2[pl.ds(j, n_in, stride=B_GROUP), :]
        o2[pl.ds(j, rows, stride=B_GROUP), :] = _dot(tab_ref[j], gj.astype(BF16))
    o_ref[...] = gate_ref[...] * (o_ref[...] + z_ref[...] * fb_ref[...])
    if chain:
        n_out = o1_ref.shape[0]
        q2 = _rows2d(o1_ref)
        for j in range(B_GROUP):
            zj = o2[pl.ds(j, rows, stride=B_GROUP), :]
            q2[pl.ds(j, n_out, stride=B_GROUP), :] = _dot(tab1_ref[j], zj.astype(BF16))


def _s3(g3, tab, gate4, gi, z4, zi, fb, tab1, *, ct):
    n_in, _, d = g3.shape
    rows = tab.shape[1]
    chain = tab1 is not None
    blk = pl.BlockSpec((rows, B_GROUP, ct), lambda g, c: (0, g, c))
    in_specs = [
        pl.BlockSpec((n_in, B_GROUP, ct), lambda g, c: (0, g, c)),
        pl.BlockSpec((B_GROUP, rows, n_in), lambda g, c: (g, 0, 0)),
        pl.BlockSpec((None, rows, B_GROUP, ct), lambda g, c: (gi, 0, g, c)),
        pl.BlockSpec((None, rows, B_GROUP, ct), lambda g, c: (zi, 0, g, c)),
        pl.BlockSpec((1, 1, ct), lambda g, c: (0, 0, c)),
    ]
    args = [g3, tab, gate4, z4, fb]
    out_specs = blk
    out_shape = jax.ShapeDtypeStruct((rows, FFT_N2, d), F32)
    if chain:
        n_out = tab1.shape[1]
        in_specs.append(pl.BlockSpec((B_GROUP, n_out, rows), lambda g, c: (g, 0, 0)))
        args.append(tab1)
        out_specs = [blk, pl.BlockSpec((n_out, B_GROUP, ct), lambda g, c: (0, g, c))]
        out_shape = [out_shape, jax.ShapeDtypeStruct((n_out, FFT_N2, d), F32)]
    return pl.pallas_call(
        functools.partial(_s3_body, chain=chain),
        grid=(FFT_N2 // B_GROUP, d // ct),
        in_specs=in_specs,
        out_specs=out_specs,
        out_shape=out_shape,
        compiler_params=_cp(("arbitrary", "arbitrary")),
        name="hyena_conv_idft1_dft1" if chain else "hyena_conv_idft1",
    )(*args)


def _dft_tables(n_lat):
    n = 2 * n_lat
    n1 = n // FFT_N2
    k1 = jnp.arange(n1, dtype=jnp.int32)
    th_a = ((k1[:, None] * k1[None, :]) % n1).astype(F32) * (2.0 * math.pi / n1)
    th_b = (jnp.arange(FFT_N2, dtype=jnp.int32)[:, None] * k1[None, :]).astype(F32) * (2.0 * math.pi / n)
    ca, sa = jnp.cos(th_a)[None], jnp.sin(th_a)[None]
    cb, sb = jnp.cos(th_b)[:, :, None], jnp.sin(th_b)[:, :, None]
    cr = ca * cb - sa * sb
    sn = sa * cb + ca * sb
    ha = n1 // 2
    crh, snh = cr[:, :, :ha], sn[:, :, :ha]
    w1 = jnp.concatenate([jnp.concatenate([crh, snh], axis=2), jnp.concatenate([-snh, crh], axis=2)], axis=1)
    w1f = jnp.concatenate([cr, -sn], axis=1)
    v = jnp.swapaxes(w1, 1, 2) * (1.0 / n)
    k2 = jnp.arange(FFT_N2, dtype=jnp.int32)
    th2 = ((k2[:, None] * k2[None, :]) % FFT_N2).astype(F32) * (2.0 * math.pi / FFT_N2)
    c2, s2 = jnp.cos(th2), jnp.sin(th2)
    ff = jnp.concatenate([jnp.concatenate([c2, s2], axis=1), jnp.concatenate([-s2, c2], axis=1)], axis=0)
    fi = jnp.concatenate([jnp.concatenate([c2, -s2], axis=1), jnp.concatenate([s2, c2], axis=1)], axis=0)
    return w1.astype(BF16), w1f.astype(BF16), v.astype(BF16), ff.astype(BF16), fi.astype(BF16)


def _hyena_mix(proj3, fparams, fbias, *, n_lat):
    _, bsz, _, d = proj3.shape
    f_w1, f_b1, f_f1, f_w2, f_b2, f_f2, f_w3, decay = fparams
    na = 2 * n_lat // FFT_N2
    w1, w1f, v, ff, fi = _dft_tables(n_lat)
    ct = LANES
    hd = _filt_feat(f_w1, f_b1, f_f1, f_w2, f_b2, f_f2, n_lat=n_lat)
    w3r = jnp.transpose(f_w3.reshape(HY_HID, 2, 2, d), (1, 2, 0, 3))
    w3r = jnp.zeros((2, 2, LANES, d), F32).at[:, :, :HY_HID].set(w3r)
    kf1 = _filt_s1(hd, w3r, decay.reshape(2, 2, 1, d), w1f, ct=ct)
    kf = _s2(kf1.reshape(2, 2, na, FFT_N2, d), None, 0, ff, None, ct=d, conv=False)
    p3 = proj3.reshape(3, bsz * (n_lat // FFT_N2), FFT_N2, d)
    o1 = _s1(p3, 2, w1, ct=ct)
    g = _s2(o1.reshape(2, na, FFT_N2, d), kf, 0, ff, fi, ct=d, conv=True)
    z, o1 = _s3(g.reshape(2 * na, FFT_N2, d), v, p3, 0, p3, 2, fbias[0].reshape(1, 1, d), w1, ct=ct)
    g = _s2(o1.reshape(2, na, FFT_N2, d), kf, 1, ff, fi, ct=d, conv=True)
    z = _s3(g.reshape(2 * na, FFT_N2, d), v, p3, 1, z[None], 0, fbias[1].reshape(1, 1, d), None, ct=ct)
    return z.reshape(bsz, n_lat, d)


def _rope_tables(n_tokens):
    rows = n_tokens // GRID_W
    row = jnp.broadcast_to(jnp.arange(rows, dtype=F32)[:, None], (rows, GRID_W)).reshape(-1)
    col = jnp.broadcast_to(jnp.arange(GRID_W, dtype=F32)[None, :], (rows, GRID_W)).reshape(-1)
    axis_dim = QK_ROPE // 2
    inv_freq = 1.0 / (ROPE_THETA ** (jnp.arange(0, axis_dim, 2, dtype=F32) / axis_dim))
    ang = jnp.concatenate([row[:, None] * inv_freq, col[:, None] * inv_freq], axis=-1)
    return jnp.cos(ang), jnp.sin(ang)


def _mla_weights(w_down, g_q, w_uq, g_kv, w_ukv):
    d = w_down.shape[0]
    nh = MLA_HEADS
    kpe = w_down[:, Q_LORA + KV_LORA:]
    w1, w2 = kpe[:, 0::2], kpe[:, 1::2]
    z = jnp.zeros((d, LANES - QK_ROPE), w_down.dtype)
    wd = jnp.concatenate([w_down[:, :Q_LORA + KV_LORA], w1, w2, z, w2, w1, z], axis=1).astype(BF16)
    uq = w_uq.reshape(Q_LORA, nh, QK_NOPE + QK_ROPE)
    pe = uq[:, :, QK_NOPE:]
    uq = jnp.concatenate([uq[:, :, :QK_NOPE], pe[:, :, 0::2], pe[:, :, 1::2]], axis=2)
    wuqT = uq.reshape(Q_LORA, nh * (QK_NOPE + QK_ROPE)).T.astype(BF16)
    ukv = w_ukv.reshape(KV_LORA, nh, QK_NOPE + V_DIM)
    wuk = ukv[:, :, :QK_NOPE].reshape(KV_LORA, nh * QK_NOPE).astype(BF16)
    wuvT = ukv[:, :, QK_NOPE:].reshape(KV_LORA, nh * V_DIM).T.astype(BF16)
    return wd, g_q.reshape(1, -1), g_kv.reshape(1, -1), wuk, wuqT, wuvT


def kernel(x, c, ctx, c_ctx, ada_w, ada_b, norm_mix_g, norm_ffn_g, mla_w_down, mla_g_q, mla_w_uq, mla_g_kv, mla_w_ukv, mla_w_o, hy_w_in, hy_b_in, hy_conv_w, hy_conv_b, hy_f_w1, hy_f_b1, hy_f_freq1, hy_f_w2, hy_f_b2, hy_f_freq2, hy_f_w3, hy_decay, hy_bias, hy_w_out, hy_b_out, moe_w_r, moe_b_r, moe_w_in, moe_b_in, moe_w_out, moe_b_out, final_g):
    bsz, n_lat, d = x.shape
    n_ctx = ctx.shape[1]
    depth = ada_w.shape[0]
    assert bsz == 2 and d == MLA_HEADS * V_DIM and n_lat % 512 == 0 and n_ctx % 128 == 0
    assert depth == 2

    cond8 = jnp.zeros((8, d), F32).at[:bsz].set(c).at[bsz].set(c_ctx)
    mods = _ada(cond8, ada_w, ada_b)

    def mod(i, j, rows):
        return mods[i, rows, j * d:(j + 1) * d][:, None, :]

    lat_rows = slice(0, bsz)
    ctx_rows = slice(bsz, bsz + 1)
    xl = x
    for i in range(depth):
        kind, j = i % 2, i // 2
        sh1, sc1, g1 = (mod(i, m, lat_rows) for m in range(3))
        sh2, sc2, g2 = (mod(i, m, lat_rows) for m in range(3, 6))
        gm = norm_mix_g[i].reshape(1, d)
        if kind == 0:
            wts = _mla_weights(mla_w_down[j], mla_g_q[j], mla_w_uq[j], mla_g_kv[j], mla_w_ukv[j])
            cos, sin = _rope_tables(n_lat)
            zl = jnp.zeros((n_lat, LANES - QK_ROPE), F32)
            tabs = (jnp.concatenate([cos, cos, zl], axis=1), jnp.concatenate([-sin, sin, zl], axis=1), cos.T, sin.T)
            tq = tv = 512
            tk = 2048 if n_lat % 4096 == 0 else 512
            qT, k, vT = _mla_proj(xl, gm, sh1, sc1, wts, tabs, need_q=True, tm=tv, tk=tv)
            half = QK_ROPE // 2
            one_c = jnp.concatenate([jnp.ones((n_ctx, QK_ROPE), F32), jnp.zeros((n_ctx, LANES - QK_ROPE), F32)], axis=1)
            tabs_c = (one_c, jnp.zeros((n_ctx, LANES), F32), jnp.ones((half, n_ctx), F32), jnp.zeros((half, n_ctx), F32))
            kc, vTc = _mla_proj(ctx, gm, mod(i, 0, ctx_rows), mod(i, 1, ctx_rows), wts, tabs_c,
                                need_q=False, tm=n_ctx, tk=n_ctx)
            o = _attention(qT, k, vT, kc, vTc, tq=tq, tk=tk)
            wo = mla_w_o[j].astype(BF16)
            bo = jnp.zeros((1, d), F32)
            transposed = True
        else:
            proj3 = _hy_in(xl, gm, sh1, sc1, hy_w_in[j].astype(BF16), hy_b_in[j].reshape(1, -1), hy_conv_w[j],
                           hy_conv_b[j].reshape(1, -1), tm=512)
            fparams = (hy_f_w1[j], hy_f_b1[j], hy_f_freq1[j], hy_f_w2[j], hy_f_b2[j], hy_f_freq2[j], hy_f_w3[j],
                       hy_decay[j])
            o = _hyena_mix(proj3, fparams, hy_bias[j], n_lat=n_lat)
            wo = hy_w_out[j].astype(BF16)
            bo = hy_b_out[j].reshape(1, d)
            transposed = False
        wr = jnp.zeros((d, LANES), F32).at[:, :N_EXPERTS].set(moe_w_r[i])
        wrh = wr.astype(BF16)
        wrl = (wr - wrh.astype(F32)).astype(BF16)
        br = jnp.zeros((1, LANES), F32).at[0, :N_EXPERTS].set(moe_b_r[i])
        xl, fl, topi, gates, rank, cnt = _post(o, wo, bo, xl, g1, norm_ffn_g[i].reshape(1, d), sh2, sc2, wrh, wrl, br,
                                               transposed=transposed, tm=512)
        xl = _moe(fl, topi, gates, rank, cnt, xl, g2, final_g.reshape(1, d), i, moe_w_in, moe_b_in,
                  moe_w_out, moe_b_out, final=(i == depth - 1))
    return xl
```

```python
import functools
import math

import jax
import jax.numpy as jnp
from jax import lax
from jax.experimental import pallas as pl
from jax.experimental.pallas import tpu as pltpu

F32 = jnp.float32
BF16 = jnp.bfloat16

EPS = 1e-6
GRID_W = 64
MLA_HEADS = 8
QK_NOPE = 128
QK_ROPE = 64
V_DIM = 128
Q_LORA = 512
KV_LORA = 256
ROPE_THETA = 10000.0
MLA_SCALE = (QK_NOPE + QK_ROPE) ** -0.5
QK_PAD = 256

HY_EMB = 33
HY_BANDS = (HY_EMB - 1) // 2
HY_HID = 64
FFT_N2 = 128
B_GROUP = 8

N_EXPERTS = 32
TOP_K = 4
SWIGLU_LIMIT = 7.0
SWIGLU_ALPHA = 1.702
MOE_TM = 256
LANES = 128

VMEM_LIMIT = 56 * 1024 * 1024


def _cp(sem, vmem=VMEM_LIMIT):
    return pltpu.CompilerParams(dimension_semantics=sem, vmem_limit_bytes=vmem)


def _dot(a, b):
    return jnp.dot(a, b, preferred_element_type=F32)


def _dot_hi(a, b):
    return jnp.dot(a, b, preferred_element_type=F32, precision=lax.Precision.HIGHEST)


def _rms(x, g):
    return x * lax.rsqrt(jnp.mean(x * x, axis=-1, keepdims=True) + EPS) * g


def _ada_body(c_ref, w_ref, b_ref, o_ref):
    c = c_ref[...]
    s = c * jax.nn.sigmoid(c)
    o_ref[0] = _dot(s.astype(BF16), w_ref[0].astype(BF16)) + b_ref[0]


def _ada(cond8, ada_w, ada_b):
    depth, d, n = ada_w.shape
    tn = n // 4
    return pl.pallas_call(
        _ada_body,
        grid=(depth, n // tn),
        in_specs=[
            pl.BlockSpec((8, d), lambda i, j: (0, 0)),
            pl.BlockSpec((1, d, tn), lambda i, j: (i, 0, j)),
            pl.BlockSpec((1, 1, tn), lambda i, j: (i, 0, j)),
        ],
        out_specs=pl.BlockSpec((1, 8, tn), lambda i, j: (i, 0, j)),
        out_shape=jax.ShapeDtypeStruct((depth, 8, n), F32),
        compiler_params=_cp(("arbitrary", "arbitrary")),
        name="ada_mod",
    )(cond8, ada_w, ada_b.reshape(depth, 1, n))


def _mla_proj_body(x_ref, g_ref, sh_ref, sc_ref, wd_ref, gq_ref, gkv_ref, wuk_ref, wuqT_ref, wuvT_ref,
                   ct_ref, st_ref, cT_ref, sT_ref, *out_refs, need_q, tk):
    if need_q:
        qT_ref, k_ref, vT_ref = out_refs
    else:
        k_ref, vT_ref = out_refs
    nh = MLA_HEADS
    x = x_ref[0]
    h = _rms(x, g_ref[...]) * (1.0 + sc_ref[0]) + sh_ref[0]
    lat = _dot(h.astype(BF16), wd_ref[...])
    o_kv = Q_LORA
    o_a = Q_LORA + KV_LORA
    kvn = _rms(lat[:, o_kv:o_a], gkv_ref[...])
    kr = (lat[:, o_a:o_a + LANES] * ct_ref[...] + lat[:, o_a + LANES:o_a + 2 * LANES] * st_ref[...]).astype(BF16)
    knope = _dot(kvn.astype(BF16), wuk_ref[...])
    for hh in range(nh):
        k_ref[0, hh, :, 0:QK_NOPE] = knope[:, hh * QK_NOPE:(hh + 1) * QK_NOPE].astype(BF16)
        k_ref[0, hh, :, QK_NOPE:QK_PAD] = kr
    vT = _dot(wuvT_ref[...], kvn.T.astype(BF16))
    tm = x.shape[0]
    for hh in range(nh):
        for c in range(tm // tk):
            vT_ref[0, hh, c] = vT[hh * V_DIM:(hh + 1) * V_DIM, c * tk:(c + 1) * tk].astype(BF16)
    if need_q:
        qn = _rms(lat[:, :Q_LORA], gq_ref[...])
        qT = _dot(wuqT_ref[...], qn.T.astype(BF16)) * (MLA_SCALE * math.log2(math.e))
        c = cT_ref[...]
        s = sT_ref[...]
        hw = QK_NOPE + QK_ROPE
        half = QK_ROPE // 2
        for hh in range(nh):
            base = hh * hw
            x1 = qT[base + QK_NOPE:base + QK_NOPE + half]
            x2 = qT[base + QK_NOPE + half:base + hw]
            qT_ref[0, hh, 0:QK_NOPE] = qT[base:base + QK_NOPE].astype(BF16)
            qT_ref[0, hh, QK_NOPE:QK_NOPE + half] = (x1 * c - x2 * s).astype(BF16)
            qT_ref[0, hh, QK_NOPE + half:hw] = (x1 * s + x2 * c).astype(BF16)
            qT_ref[0, hh, hw:QK_PAD] = jnp.zeros((QK_PAD - hw, tm), BF16)


def _mla_proj(x, g, sh, sc, wts, tabs, *, need_q, tm, tk):
    bsz, n, d = x.shape
    nh = MLA_HEADS
    wd, gq, gkv, wuk, wuqT, wuvT = wts
    ct, st, cT, sT = tabs
    nsh = sh.shape[0]
    full = lambda a: pl.BlockSpec(a.shape, lambda b, i: (0,) * a.ndim)
    in_specs = [
        pl.BlockSpec((1, tm, d), lambda b, i: (b, i, 0)),
        full(g),
        pl.BlockSpec((1, 1, d), lambda b, i: (b % nsh, 0, 0)),
        pl.BlockSpec((1, 1, d), lambda b, i: (b % nsh, 0, 0)),
        full(wd), full(gq), full(gkv), full(wuk), full(wuqT), full(wuvT),
        pl.BlockSpec((tm, LANES), lambda b, i: (i, 0)),
        pl.BlockSpec((tm, LANES), lambda b, i: (i, 0)),
        pl.BlockSpec((QK_ROPE // 2, tm), lambda b, i: (0, i)),
        pl.BlockSpec((QK_ROPE // 2, tm), lambda b, i: (0, i)),
    ]
    out_specs = [
        pl.BlockSpec((1, nh, tm, QK_PAD), lambda b, i: (b, 0, i, 0)),
        pl.BlockSpec((1, nh, tm // tk, V_DIM, tk), lambda b, i: (b, 0, i, 0, 0)),
    ]
    out_shape = [
        jax.ShapeDtypeStruct((bsz, nh, n, QK_PAD), BF16),
        jax.ShapeDtypeStruct((bsz, nh, n // tk, V_DIM, tk), BF16),
    ]
    if need_q:
        out_specs = [pl.BlockSpec((1, nh, QK_PAD, tm), lambda b, i: (b, 0, 0, i))] + out_specs
        out_shape = [jax.ShapeDtypeStruct((bsz, nh, QK_PAD, n), BF16)] + out_shape
    return pl.pallas_call(
        functools.partial(_mla_proj_body, need_q=need_q, tk=tk),
        grid=(bsz, n // tm),
        in_specs=in_specs,
        out_specs=out_specs,
        out_shape=out_shape,
        compiler_params=_cp(("arbitrary", "arbitrary")),
        name="mla_proj_q" if need_q else "mla_proj_ctx",
    )(x, g, sh, sc, wd, gq, gkv, wuk, wuqT, wuvT, ct, st, cT, sT)


SM_STRIP = 64
SUBLANES = 8


def _attn_body(qT_ref, k_ref, vT_ref, kc_ref, vTc_ref, o_ref, s0, s1, p0, p1, sc, pc, acc, m_scr, x0, x1, xc,
               a0, a1, ac, l_scr, d0, d1, dc, *, tk):
    nchunk = k_ref.shape[2] // tk

    def scores(kblk, s_ref, mx_ref):
        r = _dot(kblk, qT_ref[0, 0])
        s_ref[...] = r
        mx_ref[...] = jnp.max(r, axis=0, keepdims=True)

    def probs(s_ref, mx_ref, p_ref, a_ref, d_ref):
        m_old = m_scr[...]
        m_new = jnp.maximum(m_old, mx_ref[...])
        m_scr[...] = m_new
        alpha = jnp.exp2(m_old - m_new)
        a_ref[...] = alpha
        part = None
        for r in range(0, s_ref.shape[0], SM_STRIP):
            p = jnp.exp2(s_ref[r:r + SM_STRIP] - m_new)
            p_ref[r:r + SM_STRIP] = p.astype(BF16)
            ps = jnp.sum(p.reshape(SM_STRIP // SUBLANES, SUBLANES, p.shape[1]), axis=0)
            part = ps if part is None else part + ps
        d_ref[...] = part

    def accumulate(p_ref, a_ref, d_ref, vblk):
        acc[...] = a_ref[...] * acc[...] + _dot(vblk, p_ref[...])
        l_scr[...] = a_ref[...] * l_scr[...] + d_ref[...]

    def kchunk(i):
        return k_ref[0, 0, pl.ds(pl.multiple_of(i * tk, tk), tk), :]

    def vchunk(i):
        nsub = tk // vT_ref.shape[-1]
        return jnp.concatenate([vT_ref[0, 0, i * nsub + u] for u in range(nsub)], axis=1)

    m_scr[...] = jnp.full(m_scr.shape, -jnp.inf, F32)
    acc[...] = jnp.zeros(acc.shape, F32)
    l_scr[...] = jnp.zeros(l_scr.shape, F32)
    scores(kc_ref[0, 0], sc, xc)
    scores(kchunk(0), s0, x0)
    probs(sc, xc, pc, ac, dc)
    scores(kchunk(1), s1, x1)
    accumulate(pc, ac, dc, vTc_ref[0, 0, 0])
    probs(s0, x0, p0, a0, d0)

    def body(j, carry):
        t = 2 * j
        scores(kchunk(t + 2), s0, x0)
        accumulate(p0, a0, d0, vchunk(t))
        probs(s1, x1, p1, a1, d1)
        scores(kchunk(t + 3), s1, x1)
        accumulate(p1, a1, d1, vchunk(t + 1))
        probs(s0, x0, p0, a0, d0)
        return carry

    lax.fori_loop(0, nchunk // 2 - 1, body, 0)
    accumulate(p0, a0, d0, vchunk(nchunk - 2))
    probs(s1, x1, p1, a1, d1)
    accumulate(p1, a1, d1, vchunk(nchunk - 1))
    o_ref[0, 0] = (acc[...] / jnp.sum(l_scr[...], axis=0, keepdims=True)).astype(BF16)


def _attention(qT, k, vT, kc, vTc, *, tq, tk):
    bsz, nh, _, n = qT.shape
    nc = kc.shape[2]
    tv = vT.shape[-1]
    assert (n // tk) % 2 == 0 and tk % tv == 0
    return pl.pallas_call(
        functools.partial(_attn_body, tk=tk),
        grid=(bsz, nh, n // tq),
        in_specs=[
            pl.BlockSpec((1, 1, QK_PAD, tq), lambda b, h, i: (b, h, 0, i)),
            pl.BlockSpec((1, 1, n, QK_PAD), lambda b, h, i: (b, h, 0, 0)),
            pl.BlockSpec((1, 1, n // tv, V_DIM, tv), lambda b, h, i: (b, h, 0, 0, 0)),
            pl.BlockSpec((1, 1, nc, QK_PAD), lambda b, h, i: (b, h, 0, 0)),
            pl.BlockSpec((1, 1, 1, V_DIM, nc), lambda b, h, i: (b, h, 0, 0, 0)),
        ],
        out_specs=pl.BlockSpec((1, 1, V_DIM, tq), lambda b, h, i: (b, h, 0, i)),
        out_shape=jax.ShapeDtypeStruct((bsz, nh, V_DIM, n), BF16),
        scratch_shapes=[pltpu.VMEM((tk, tq), F32), pltpu.VMEM((tk, tq), F32),
                        pltpu.VMEM((tk, tq), BF16), pltpu.VMEM((tk, tq), BF16),
                        pltpu.VMEM((nc, tq), F32), pltpu.VMEM((nc, tq), BF16),
                        pltpu.VMEM((V_DIM, tq), F32), pltpu.VMEM((1, tq), F32),
                        pltpu.VMEM((1, tq), F32), pltpu.VMEM((1, tq), F32), pltpu.VMEM((1, tq), F32),
                        pltpu.VMEM((1, tq), F32), pltpu.VMEM((1, tq), F32), pltpu.VMEM((1, tq), F32),
                        pltpu.VMEM((SUBLANES, tq), F32), pltpu.VMEM((SUBLANES, tq), F32),
                        pltpu.VMEM((SUBLANES, tq), F32), pltpu.VMEM((SUBLANES, tq), F32)],
        compiler_params=_cp(("arbitrary", "arbitrary", "arbitrary")),
        name="mla_attention",
    )(qT, k, vT, kc, vTc)


def _post_body(o_ref, wo_ref, bo_ref, x_ref, g1_ref, gf_ref, sh_ref, sc_ref, wrh_ref, wrl_ref, br_ref, tri_ref,
               xl_ref, fl_ref, ti_ref, gt_ref, rk_ref, cnt_ref, *, transposed):
    @pl.when((pl.program_id(0) == 0) & (pl.program_id(1) == 0))
    def _():
        cnt_ref[...] = jnp.zeros_like(cnt_ref)

    tm = x_ref.shape[1]
    if transposed:
        oT = o_ref[0].astype(F32).reshape(MLA_HEADS * V_DIM, tm)
        o = oT.T.astype(BF16)
    else:
        o = o_ref[0].astype(BF16)
    y = _dot(o, wo_ref[...]) + bo_ref[...]
    xl = x_ref[0] + g1_ref[0] * y
    xl_ref[0] = xl
    fl = _rms(xl, gf_ref[...]) * (1.0 + sc_ref[0]) + sh_ref[0]
    _to_rows(fl_ref, fl)
    flh = fl.astype(BF16)
    fll = (fl - flh.astype(F32)).astype(BF16)
    logits = _dot(flh, wrh_ref[...]) + (_dot(fll, wrh_ref[...]) + _dot(flh, wrl_ref[...])) + br_ref[...]
    lane = lax.broadcasted_iota(jnp.int32, (tm, LANES), 1).astype(F32)
    neg = jnp.float32(-jnp.inf)
    work = jnp.where(lane < N_EXPERTS, logits, neg)
    vals, idxs = [], []
    onehot = jnp.zeros((tm, LANES), F32)
    for _ in range(TOP_K):
        mk = jnp.max(work, axis=-1, keepdims=True)
        ik = jnp.min(jnp.where(work == mk, lane, float(LANES)), axis=-1, keepdims=True)
        sel = lane == ik
        onehot = jnp.where(sel, 1.0, onehot)
        work = jnp.where(sel, neg, work)
        vals.append(mk)
        idxs.append(ik)
    es = [jnp.exp(v - vals[0]) for v in vals]
    den = es[0] + es[1] + es[2] + es[3]
    pre = _dot(tri_ref[...], onehot.astype(BF16)) + cnt_ref[...]
    ti = jnp.zeros((tm, LANES), F32)
    gt = jnp.zeros((tm, LANES), F32)
    rk = jnp.zeros((tm, LANES), F32)
    for kk in range(TOP_K):
        rank = jnp.sum(jnp.where(lane == idxs[kk], pre, 0.0), axis=-1, keepdims=True)
        ti = jnp.where(lane == kk, idxs[kk], ti)
        gt = jnp.where(lane == kk, es[kk] / den, gt)
        rk = jnp.where(lane == kk, rank, rk)
    ti_ref[...] = ti[:, :TOP_K].astype(jnp.int32)
    gt_ref[...] = gt[:, :TOP_K]
    rk_ref[...] = rk[:, :TOP_K].astype(jnp.int32)
    cnt_ref[...] += jnp.sum(onehot, axis=0, keepdims=True)


def _post(o, wo, bo, x, g1, gf, sh, sc, wrh, wrl, br, *, transposed, tm):
    bsz, n, d = x.shape
    t = bsz * n
    nt = n // tm
    tri = (lax.broadcasted_iota(jnp.int32, (tm, tm), 0) > lax.broadcasted_iota(jnp.int32, (tm, tm), 1)).astype(BF16)
    full = lambda a: pl.BlockSpec(a.shape, lambda b, i: (0,) * a.ndim)
    per_b = pl.BlockSpec((1, 1, d), lambda b, i: (b, 0, 0))
    if transposed:
        o_spec = pl.BlockSpec((1, MLA_HEADS, V_DIM, tm), lambda b, i: (b, 0, 0, i))
    else:
        o_spec = pl.BlockSpec((1, tm, d), lambda b, i: (b, i, 0))
    tok = lambda w: pl.BlockSpec((tm, w), lambda b, i: (b * nt + i, 0))
    return pl.pallas_call(
        functools.partial(_post_body, transposed=transposed),
        grid=(bsz, nt),
        in_specs=[o_spec, full(wo), full(bo), pl.BlockSpec((1, tm, d), lambda b, i: (b, i, 0)), per_b, full(gf),
                  per_b, per_b, full(wrh), full(wrl), full(br), full(tri)],
        out_specs=[pl.BlockSpec((1, tm, d), lambda b, i: (b, i, 0)),
                   pl.BlockSpec((tm * ROW_SUB, LANES), lambda b, i: (b * nt + i, 0)),
                   tok(TOP_K), tok(TOP_K), tok(TOP_K), pl.BlockSpec((1, LANES), lambda b, i: (0, 0))],
        out_shape=[jax.ShapeDtypeStruct((bsz, n, d), F32), jax.ShapeDtypeStruct((t * ROW_SUB, LANES), F32),
                   jax.ShapeDtypeStruct((t, TOP_K), jnp.int32), jax.ShapeDtypeStruct((t, TOP_K), F32),
                   jax.ShapeDtypeStruct((t, TOP_K), jnp.int32), jax.ShapeDtypeStruct((1, LANES), F32)],
        compiler_params=_cp(("arbitrary", "arbitrary")),
        name="post_attn" if transposed else "post_hyena",
    )(o, wo, bo, x, g1, gf, sh, sc, wrh, wrl, br, tri)


ROW_SUB = 8


def _row_slice(i):
    return pl.ds(pl.multiple_of(i * ROW_SUB, ROW_SUB), ROW_SUB)


def _to_rows(ref, x):
    for s in range(ROW_SUB):
        ref[pl.ds(s, x.shape[0], stride=ROW_SUB), :] = x[:, s * LANES:(s + 1) * LANES]


def _from_rows(ref, lo, hi):
    return jnp.concatenate([ref[pl.ds(lo * ROW_SUB + s, hi - lo, stride=ROW_SUB), :] for s in range(ROW_SUB)], axis=1)


def _dispatch_body(pe_ref, pd_ref, dest_ref, fl_ref, xs_out, zbuf, sem, *, td):
    @pl.when(pl.program_id(0) == 0)
    def _():
        zbuf[...] = jnp.zeros(zbuf.shape, zbuf.dtype)
        for e in range(N_EXPERTS):
            @pl.when(pd_ref[e] > 0)
            def _():
                start = pl.multiple_of((pe_ref[e] - MOE_TM) * ROW_SUB, ROW_SUB)
                cp = pltpu.make_async_copy(zbuf, xs_out.at[pl.ds(start, MOE_TM * ROW_SUB)], sem)
                cp.start()
                cp.wait()

    def issue(t, carry):
        for kk in range(TOP_K):
            d = dest_ref[0, 0, t * TOP_K + kk]
            pltpu.make_async_copy(fl_ref.at[_row_slice(t)], xs_out.at[_row_slice(d)], sem).start(priority=kk % 2)
        return carry

    lax.fori_loop(0, td, issue, 0, unroll=2)

    def drain(t, carry):
        pltpu.make_async_copy(fl_ref.at[_row_slice(0)], xs_out.at[_row_slice(0)], sem).wait()
        return carry

    lax.fori_loop(0, td * TOP_K, drain, 0, unroll=8)


def _dispatch(pad_end, padded, dest, fl, n_rows, *, td):
    t = fl.shape[0] // ROW_SUB
    dest3 = dest.reshape(t // td, 1, td * TOP_K)
    grid_spec = pltpu.PrefetchScalarGridSpec(
        num_scalar_prefetch=2,
        grid=(t // td,),
        in_specs=[
            pl.BlockSpec((1, 1, td * TOP_K), lambda i, pe, pd: (i, 0, 0), memory_space=pltpu.SMEM),
            pl.BlockSpec((td * ROW_SUB, LANES), lambda i, pe, pd: (i, 0)),
        ],
        out_specs=pl.BlockSpec(memory_space=pl.ANY),
        scratch_shapes=[pltpu.VMEM((MOE_TM * ROW_SUB, LANES), fl.dtype), pltpu.SemaphoreType.DMA(())],
    )
    return pl.pallas_call(
        functools.partial(_dispatch_body, td=td),
        grid_spec=grid_spec,
        out_shape=jax.ShapeDtypeStruct((n_rows * ROW_SUB, LANES), fl.dtype),
        compiler_params=_cp(("arbitrary",)),
        name="moe_dispatch",
    )(pad_end, padded, dest3, fl)


def _expert_body(be_ref, nu_ref, xs_ref, win_ref, bin_ref, wout_ref, bout_ref, ys_ref, win_s, wout_s):
    b = pl.program_id(0)
    dff = wout_ref.shape[1]

    @pl.when(b < nu_ref[0])
    def _():
        prev = be_ref[jnp.maximum(b - 1, 0)]

        @pl.when((b == 0) | (prev != be_ref[b]))
        def _():
            win_s[...] = win_ref[0].astype(BF16)
            wout_s[...] = wout_ref[0].astype(BF16)

        x = _from_rows(xs_ref, 0, xs_ref.shape[0] // ROW_SUB).astype(BF16)
        gu = _dot(x, win_s[...]) + bin_ref[0]
        gate = jnp.minimum(gu[:, :dff], SWIGLU_LIMIT)
        lin = jnp.clip(gu[:, dff:], -SWIGLU_LIMIT, SWIGLU_LIMIT)
        act = gate * jax.nn.sigmoid(SWIGLU_ALPHA * gate) * (lin + 1.0)
        _to_rows(ys_ref, _dot(act.astype(BF16), wout_s[...]) + bout_ref[0])

    @pl.when(b >= nu_ref[0])
    def _():
        ys_ref[...] = jnp.zeros_like(ys_ref)


def _experts(blk_exp, n_used, xs, layer, w_in, b_in, w_out, b_out):
    n_rows = xs.shape[0] // ROW_SUB
    depth, ne, d, f2 = w_in.shape
    dff = w_out.shape[2]
    tm = MOE_TM
    grid_spec = pltpu.PrefetchScalarGridSpec(
        num_scalar_prefetch=2,
        grid=(n_rows // tm,),
        in_specs=[
            pl.BlockSpec((tm * ROW_SUB, LANES), lambda b, be, nu: (jnp.minimum(b, nu[0] - 1), 0)),
            pl.BlockSpec((None, 1, d, f2), lambda b, be, nu: (layer, be[b], 0, 0)),
            pl.BlockSpec((None, 1, 1, f2), lambda b, be, nu: (layer, be[b], 0, 0)),
            pl.BlockSpec((None, 1, dff, d), lambda b, be, nu: (layer, be[b], 0, 0)),
            pl.BlockSpec((None, 1, 1, d), lambda b, be, nu: (layer, be[b], 0, 0)),
        ],
        out_specs=pl.BlockSpec((tm * ROW_SUB, LANES), lambda b, be, nu: (b, 0)),
        scratch_shapes=[pltpu.VMEM((d, f2), BF16), pltpu.VMEM((dff, d), BF16)],
    )
    return pl.pallas_call(
        _expert_body,
        grid_spec=grid_spec,
        out_shape=jax.ShapeDtypeStruct(xs.shape, F32),
        compiler_params=_cp(("arbitrary",)),
        name="moe_experts",
    )(blk_exp, n_used, xs, w_in, b_in.reshape(depth, ne, 1, f2), w_out, b_out.reshape(depth, ne, 1, d))


def _combine_body(dest_ref, ys_hbm, gt_ref, xl_ref, g2_ref, fg_ref, out_ref, buf, sem, *, tc, final):
    def issue(t, carry):
        for kk in range(TOP_K):
            d = dest_ref[0, 0, t * TOP_K + kk]
            pltpu.make_async_copy(ys_hbm.at[_row_slice(d)], buf.at[_row_slice(kk * tc + t)], sem).start(
                priority=kk % 2)
        return carry

    lax.fori_loop(0, tc, issue, 0, unroll=2)

    def drain(t, carry):
        pltpu.make_async_copy(ys_hbm.at[_row_slice(0)], buf.at[_row_slice(0)], sem).wait()
        return carry

    lax.fori_loop(0, tc * TOP_K, drain, 0, unroll=8)
    gt = gt_ref[...]
    y = gt[:, 0:1] * _from_rows(buf, 0, tc)
    for kk in range(1, TOP_K):
        y = y + gt[:, kk:kk + 1] * _from_rows(buf, kk * tc, (kk + 1) * tc)
    xl = xl_ref[0] + g2_ref[0] * y
    out_ref[0] = _rms(xl, fg_ref[...]) if final else xl


def _combine(dest, ys, gates, xl, g2, fg, *, tc, final):
    bsz, n, d = xl.shape
    t = bsz * n
    nt = n // tc
    dest3 = dest.reshape(t // tc, 1, tc * TOP_K)
    return pl.pallas_call(
        functools.partial(_combine_body, tc=tc, final=final),
        grid=(bsz, nt),
        in_specs=[
            pl.BlockSpec((1, 1, tc * TOP_K), lambda b, i: (b * nt + i, 0, 0), memory_space=pltpu.SMEM),
            pl.BlockSpec(memory_space=pl.ANY),
            pl.BlockSpec((tc, TOP_K), lambda b, i: (b * nt + i, 0)),
            pl.BlockSpec((1, tc, d), lambda b, i: (b, i, 0)),
            pl.BlockSpec((1, 1, d), lambda b, i: (b, 0, 0)),
            pl.BlockSpec((1, d), lambda b, i: (0, 0)),
        ],
        out_specs=pl.BlockSpec((1, tc, d), lambda b, i: (b, i, 0)),
        out_shape=jax.ShapeDtypeStruct((bsz, n, d), F32),
        scratch_shapes=[pltpu.VMEM((TOP_K * tc * ROW_SUB, LANES), F32), pltpu.SemaphoreType.DMA(())],
        compiler_params=_cp(("arbitrary", "arbitrary")),
        name="moe_combine",
    )(dest3, ys, gates, xl, g2, fg)


def _moe(fl, topi, gates, rank, cnt, xl, g2, fg, layer, w_in, b_in, w_out, b_out, *, final):
    t = fl.shape[0] // ROW_SUB
    tm = MOE_TM
    counts = cnt[0, :N_EXPERTS].astype(jnp.int32)
    padded = (counts + tm - 1) // tm * tm
    pad_end = jnp.cumsum(padded)
    pad_start = pad_end - padded
    dest = jnp.take(pad_start, topi) + rank
    nb = t * TOP_K // tm + N_EXPERTS
    blk_start = jnp.arange(nb, dtype=jnp.int32) * tm
    blk_exp = jnp.minimum(jnp.sum((pad_end[None, :] <= blk_start[:, None]).astype(jnp.int32), axis=1), N_EXPERTS - 1)
    n_used = (pad_end[-1:] // tm).astype(jnp.int32)
    xs = _dispatch(pad_end, padded, dest, fl, nb * tm, td=256)
    ys = _experts(blk_exp, n_used, xs, layer, w_in, b_in, w_out, b_out)
    return _combine(dest, ys, gates, xl, g2, fg, tc=256, final=final)


def _hy_in_body(x_ref, xp_ref, xn_ref, g_ref, sh_ref, sc_ref, w_ref, b_ref, cw_ref, cb_ref, o_ref, *, nt):
    i = pl.program_id(2)
    w = w_ref[...]

    def proj(xx):
        h = _rms(xx, g_ref[...]) * (1.0 + sc_ref[0]) + sh_ref[0]
        return _dot(h.astype(BF16), w) + b_ref[...]

    p = proj(x_ref[0])
    tm = p.shape[0]
    ph = proj(jnp.concatenate([xp_ref[0], xn_ref[0]], axis=0))
    prev = jnp.where(i > 0, ph[7:8], 0.0)
    nxt = jnp.where(i < nt - 1, ph[8:9], 0.0)
    row = lax.broadcasted_iota(jnp.int32, (tm, 1), 0)
    up = jnp.where(row == 0, prev, pltpu.roll(p, 1, axis=0))
    dn = jnp.where(row == tm - 1, nxt, pltpu.roll(p, tm - 1, axis=0))
    cw = cw_ref[...]
    o_ref[0, 0] = up * cw[0:1] + p * cw[1:2] + dn * cw[2:3] + cb_ref[...]


def _hy_in(x, g, sh, sc, w, b, cw, cb, *, tm):
    bsz, n, d = x.shape
    nt = n // tm
    hb = tm // 8
    per_b = pl.BlockSpec((1, 1, d), lambda j, bb, i: (bb, 0, 0))
    return pl.pallas_call(
        functools.partial(_hy_in_body, nt=nt),
        grid=(3, bsz, nt),
        in_specs=[
            pl.BlockSpec((1, tm, d), lambda j, bb, i: (bb, i, 0)),
            pl.BlockSpec((1, 8, d), lambda j, bb, i: (bb, jnp.maximum(i * hb - 1, 0), 0)),
            pl.BlockSpec((1, 8, d), lambda j, bb, i: (bb, jnp.minimum((i + 1) * hb, n // 8 - 1), 0)),
            pl.BlockSpec((1, d), lambda j, bb, i: (0, 0)),
            per_b, per_b,
            pl.BlockSpec((d, d), lambda j, bb, i: (0, j)),
            pl.BlockSpec((1, d), lambda j, bb, i: (0, j)),
            pl.BlockSpec((3, d), lambda j, bb, i: (0, j)),
            pl.BlockSpec((1, d), lambda j, bb, i: (0, j)),
        ],
        out_specs=pl.BlockSpec((1, 1, tm, d), lambda j, bb, i: (j, bb, i, 0)),
        out_shape=jax.ShapeDtypeStruct((3, bsz, n, d), F32),
        compiler_params=_cp(("arbitrary", "arbitrary", "arbitrary")),
        name="hyena_in_proj",
    )(x, x, x, g, sh, sc, w, b, cw, cb)


def _filt_feat_body(w1_ref, b1_ref, f1_ref, w2_ref, b2_ref, f2_ref, o_ref, *, n_lat):
    na = o_ref.shape[1]
    a = lax.broadcasted_iota(jnp.int32, (na, 1), 0)
    lane = lax.broadcasted_iota(jnp.int32, (na, LANES), 1)
    band_idx = jnp.where(lane <= HY_BANDS, lane - 1, lane - 1 - HY_BANDS).astype(F32)
    band = 1e-4 + band_idx * ((HY_BANDS - 1 - 1e-4) / (HY_BANDS - 1))
    for j in range(B_GROUP):
        r = a * FFT_N2 + (pl.program_id(0) * B_GROUP + j)
        pos = jnp.where(r < n_lat, r, 2 * n_lat - r).astype(F32)
        tn = pos / float(max(n_lat - 1, 1))
        ang = ((2.0 * math.pi / n_lat) * pos) * band
        z = jnp.where(lane == 0, tn, jnp.where(lane <= HY_BANDS, jnp.cos(ang),
                                               jnp.where(lane < HY_EMB, -jnp.sin(ang), 0.0)))
        h1 = jnp.sin(f1_ref[...] * (_dot_hi(z, w1_ref[...]) + b1_ref[...]))
        h2 = jnp.sin(f2_ref[...] * (_dot_hi(h1, w2_ref[...]) + b2_ref[...]))
        valid = (r != n_lat).astype(F32)
        o_ref[j] = jnp.where(lane == HY_HID, tn, jnp.where(lane == HY_HID + 1, valid, h2))


def _filt_feat(w1, b1, f1, w2, b2, f2, *, n_lat):
    na = 2 * n_lat // FFT_N2
    w1p = jnp.zeros((LANES, LANES), F32).at[:HY_EMB, :HY_HID].set(w1)
    w2p = jnp.zeros((LANES, LANES), F32).at[:HY_HID, :HY_HID].set(w2)
    padv = lambda v: jnp.zeros((1, LANES), F32).at[0, :HY_HID].set(v)
    full = lambda shp: pl.BlockSpec(shp, lambda i: (0,) * len(shp))
    return pl.pallas_call(
        functools.partial(_filt_feat_body, n_lat=n_lat),
        grid=(FFT_N2 // B_GROUP,),
        in_specs=[full((LANES, LANES)), full((1, LANES)), full((1, LANES)),
                  full((LANES, LANES)), full((1, LANES)), full((1, LANES))],
        out_specs=pl.BlockSpec((B_GROUP, na, LANES), lambda i: (i, 0, 0)),
        out_shape=jax.ShapeDtypeStruct((FFT_N2, na, LANES), F32),
        compiler_params=_cp(("arbitrary",)),
        name="hyena_filter_features",
    )(w1p, padv(b1), padv(f1), w2p, padv(b2), padv(f2))


U32 = jnp.uint32
HI16 = 0xFFFF0000


def _pack_c(re, im):
    lo = lax.bitcast_convert_type(re.astype(BF16).astype(F32), U32) >> 16
    hi = lax.bitcast_convert_type(im.astype(BF16).astype(F32), U32) & U32(HI16)
    return hi | lo


def _unpack_c(u):
    re = lax.bitcast_convert_type(u << 16, F32)
    im = lax.bitcast_convert_type(u & U32(HI16), F32)
    return jnp.concatenate([re, im], axis=0).astype(BF16)


def _filt_s1_body(hd_ref, w3_ref, dec_ref, tab_ref, o_ref):
    na = hd_ref.shape[1]
    ha = na // 2
    o2 = _rows2d(o_ref)
    ft = hd_ref[:, :ha, :].reshape(B_GROUP * ha, LANES)
    fb = hd_ref[:, ha:, :].reshape(B_GROUP * ha, LANES)
    top = _dot(ft.astype(BF16), w3_ref[0, 0].astype(BF16))
    top = top * jnp.exp(-ft[:, HY_HID:HY_HID + 1] * jnp.abs(dec_ref[0, 0]))
    bot = _dot(fb.astype(BF16), w3_ref[0, 1].astype(BF16))
    bot = bot * (jnp.exp(-fb[:, HY_HID:HY_HID + 1] * jnp.abs(dec_ref[0, 1])) * fb[:, HY_HID + 1:HY_HID + 2])
    for j in range(B_GROUP):
        hb = jnp.concatenate([top[j * ha:(j + 1) * ha], bot[j * ha:(j + 1) * ha]], axis=0).astype(BF16)
        r = _dot(tab_ref[j], hb)
        o2[pl.ds(j, na, stride=B_GROUP), :] = _pack_c(r[:na], r[na:])


def _filt_s1(hd, w3r, dec, tab, *, ct):
    _, na, _ = hd.shape
    d = w3r.shape[-1]
    return pl.pallas_call(
        _filt_s1_body,
        grid=(2, FFT_N2 // B_GROUP, d // ct),
        in_specs=[
            pl.BlockSpec((B_GROUP, na, LANES), lambda o, g, c: (g, 0, 0)),
            pl.BlockSpec((1, 2, LANES, ct), lambda o, g, c: (o, 0, 0, c)),
            pl.BlockSpec((1, 2, 1, ct), lambda o, g, c: (o, 0, 0, c)),
            pl.BlockSpec((B_GROUP, 2 * na, na), lambda o, g, c: (g, 0, 0)),
        ],
        out_specs=pl.BlockSpec((None, na, B_GROUP, ct), lambda o, g, c: (o, 0, g, c)),
        out_shape=jax.ShapeDtypeStruct((2, na, FFT_N2, d), U32),
        compiler_params=_cp(("arbitrary", "arbitrary", "arbitrary")),
        name="hyena_filter_dft1",
    )(hd, w3r, dec, tab)


S2_KB = 4


def _s2_body(*refs, conv):
    if conv:
        o_ref, kf_ref, ff_ref, fi_ref, g_ref = refs
    else:
        o_ref, ff_ref, g_ref = refs
    for u in range(S2_KB):
        xf = _dot(ff_ref[...], _unpack_c(o_ref[u]))
        if conv:
            xr, xi = xf[:FFT_N2], xf[FFT_N2:]
            kr = kf_ref[0, u].astype(F32)
            ki = kf_ref[1, u].astype(F32)
            y = jnp.concatenate([xr * kr - xi * ki, xr * ki + xi * kr], axis=0).astype(BF16)
            xf = _dot(fi_ref[...], y)
            g_ref[u] = _pack_c(xf[:FFT_N2], xf[FFT_N2:])
        else:
            g_ref[0, u] = xf[:FFT_N2].astype(BF16)
            g_ref[1, u] = xf[FFT_N2:].astype(BF16)


def _s2(o3, kf, order, ff, fi, *, ct, conv):
    n1 = o3.shape[-3]
    d = o3.shape[-1]
    full = lambda a: pl.BlockSpec(a.shape, lambda k, c: (0,) * a.ndim)
    nk = n1 // S2_KB
    if conv:
        blk = pl.BlockSpec((S2_KB, FFT_N2, ct), lambda k, c: (k, 0, c))
        in_specs = [blk, pl.BlockSpec((None, 2, S2_KB, FFT_N2, ct), lambda k, c: (order, 0, k, 0, c)), full(ff),
                    full(fi)]
        args = (o3, kf, ff, fi)
        grid = (nk, d // ct)
        out_specs = blk
        out_shape = jax.ShapeDtypeStruct(o3.shape, U32)
    else:
        no = o3.shape[0]
        in_specs = [pl.BlockSpec((None, S2_KB, FFT_N2, ct), lambda k, c: (k // nk, k % nk, 0, c)), full(ff)]
        args = (o3, ff)
        grid = (no * nk, d // ct)
        out_specs = pl.BlockSpec((None, 2, S2_KB, FFT_N2, ct), lambda k, c: (k // nk, 0, k % nk, 0, c))
        out_shape = jax.ShapeDtypeStruct((no, 2, n1, FFT_N2, d), BF16)
    return pl.pallas_call(
        functools.partial(_s2_body, conv=conv),
        grid=grid,
        in_specs=in_specs,
        out_specs=out_specs,
        out_shape=out_shape,
        compiler_params=_cp(("arbitrary", "arbitrary")),
        name="hyena_conv_dft2" if conv else "hyena_filter_dft2",
    )(*args)


def _rows2d(ref):
    lead = ref.shape[:-3]
    return ref.reshape(lead + (ref.shape[-3] * B_GROUP, ref.shape[-1]))


def _s1_body(z_ref, tab_ref, o_ref):
    rows, n1 = z_ref.shape[0], o_ref.shape[0]
    z2, o2 = _rows2d(z_ref), _rows2d(o_ref)
    for j in range(B_GROUP):
        zj = z2[pl.ds(j, rows, stride=B_GROUP), :]
        r = _dot(tab_ref[j], zj.astype(BF16))
        o2[pl.ds(j, n1, stride=B_GROUP), :] = _pack_c(r[:n1], r[n1:])


def _s1(z4, zi, tab, *, ct):
    _, rows, _, d = z4.shape
    n1 = tab.shape[1] // 2
    return pl.pallas_call(
        _s1_body,
        grid=(FFT_N2 // B_GROUP, d // ct),
        in_specs=[
            pl.BlockSpec((None, rows, B_GROUP, ct), lambda g, c: (zi, 0, g, c)),
            pl.BlockSpec((B_GROUP, 2 * n1, rows), lambda g, c: (g, 0, 0)),
        ],
        out_specs=pl.BlockSpec((n1, B_GROUP, ct), lambda g, c: (0, g, c)),
        out_shape=jax.ShapeDtypeStruct((n1, FFT_N2, d), U32),
        compiler_params=_cp(("arbitrary", "arbitrary")),
        name="hyena_conv_dft1",
    )(z4, tab)


def _s3_body(g_ref, tab_ref, gate_ref, z_ref, fb_ref, *rest, chain):
    if chain:
        tab1_ref, o_ref, o1_ref = rest
    else:
        (o_ref,) = rest
    n1, rows = g_ref.shape[0], o_ref.shape[0]
    g2, o2 = _rows2d(g_ref), _rows2d(o_ref)
    for j in range(B_GROUP):
        gj = _unpack_c(g2[pl.ds(j, n1, stride=B_GROUP), :])
        o2[pl.ds(j, rows, stride=B_GROUP), :] = _dot(tab_ref[j], gj)
    o_ref[...] = gate_ref[...] * (o_ref[...] + z_ref[...] * fb_ref[...])
    if chain:
        q2 = _rows2d(o1_ref)
        for j in range(B_GROUP):
            zj = o2[pl.ds(j, rows, stride=B_GROUP), :]
            r = _dot(tab1_ref[j], zj.astype(BF16))
            q2[pl.ds(j, n1, stride=B_GROUP), :] = _pack_c(r[:n1], r[n1:])


def _s3(g3, tab, gate4, gi, z4, zi, fb, tab1, *, ct):
    n1, _, d = g3.shape
    rows = tab.shape[1]
    chain = tab1 is not None
    blk = pl.BlockSpec((rows, B_GROUP, ct), lambda g, c: (0, g, c))
    cblk = pl.BlockSpec((n1, B_GROUP, ct), lambda g, c: (0, g, c))
    in_specs = [
        cblk,
        pl.BlockSpec((B_GROUP, rows, 2 * n1), lambda g, c: (g, 0, 0)),
        pl.BlockSpec((None, rows, B_GROUP, ct), lambda g, c: (gi, 0, g, c)),
        pl.BlockSpec((None, rows, B_GROUP, ct), lambda g, c: (zi, 0, g, c)),
        pl.BlockSpec((1, 1, ct), lambda g, c: (0, 0, c)),
    ]
    args = [g3, tab, gate4, z4, fb]
    out_specs = blk
    out_shape = jax.ShapeDtypeStruct((rows, FFT_N2, d), F32)
    if chain:
        in_specs.append(pl.BlockSpec((B_GROUP, 2 * n1, rows), lambda g, c: (g, 0, 0)))
        args.append(tab1)
        out_specs = [blk, cblk]
        out_shape = [out_shape, jax.ShapeDtypeStruct((n1, FFT_N2, d), U32)]
    return pl.pallas_call(
        functools.partial(_s3_body, chain=chain),
        grid=(FFT_N2 // B_GROUP, d // ct),
        in_specs=in_specs,
        out_specs=out_specs,
        out_shape=out_shape,
        compiler_params=_cp(("arbitrary", "arbitrary")),
        name="hyena_conv_idft1_dft1" if chain else "hyena_conv_idft1",
    )(*args)


def _dft_tables(n_lat):
    n = 2 * n_lat
    n1 = n // FFT_N2
    k1 = jnp.arange(n1, dtype=jnp.int32)
    th_a = ((k1[:, None] * k1[None, :]) % n1).astype(F32) * (2.0 * math.pi / n1)
    th_b = (jnp.arange(FFT_N2, dtype=jnp.int32)[:, None] * k1[None, :]).astype(F32) * (2.0 * math.pi / n)
    ca, sa = jnp.cos(th_a)[None], jnp.sin(th_a)[None]
    cb, sb = jnp.cos(th_b)[:, :, None], jnp.sin(th_b)[:, :, None]
    cr = ca * cb - sa * sb
    sn = sa * cb + ca * sb
    ha = n1 // 2
    crh, snh = cr[:, :, :ha], sn[:, :, :ha]
    w1 = jnp.concatenate([jnp.concatenate([crh, snh], axis=2), jnp.concatenate([-snh, crh], axis=2)], axis=1)
    w1f = jnp.concatenate([cr, -sn], axis=1)
    v = jnp.swapaxes(w1, 1, 2) * (1.0 / n)
    k2 = jnp.arange(FFT_N2, dtype=jnp.int32)
    th2 = ((k2[:, None] * k2[None, :]) % FFT_N2).astype(F32) * (2.0 * math.pi / FFT_N2)
    c2, s2 = jnp.cos(th2), jnp.sin(th2)
    ff = jnp.concatenate([jnp.concatenate([c2, s2], axis=1), jnp.concatenate([-s2, c2], axis=1)], axis=0)
    fi = jnp.concatenate([jnp.concatenate([c2, -s2], axis=1), jnp.concatenate([s2, c2], axis=1)], axis=0)
    return w1.astype(BF16), w1f.astype(BF16), v.astype(BF16), ff.astype(BF16), fi.astype(BF16)


def _hyena_mix(proj3, fparams, fbias, *, n_lat):
    _, bsz, _, d = proj3.shape
    f_w1, f_b1, f_f1, f_w2, f_b2, f_f2, f_w3, decay = fparams
    na = 2 * n_lat // FFT_N2
    w1, w1f, v, ff, fi = _dft_tables(n_lat)
    ct = LANES
    hd = _filt_feat(f_w1, f_b1, f_f1, f_w2, f_b2, f_f2, n_lat=n_lat)
    w3r = jnp.transpose(f_w3.reshape(HY_HID, 2, 2, d), (1, 2, 0, 3))
    w3r = jnp.zeros((2, 2, LANES, d), F32).at[:, :, :HY_HID].set(w3r)
    kf1 = _filt_s1(hd, w3r, decay.reshape(2, 2, 1, d), w1f, ct=ct)
    kf = _s2(kf1, None, 0, ff, None, ct=d, conv=False)
    p3 = proj3.reshape(3, bsz * (n_lat // FFT_N2), FFT_N2, d)
    o1 = _s1(p3, 2, w1, ct=ct)
    g = _s2(o1, kf, 0, ff, fi, ct=d, conv=True)
    z, o1 = _s3(g, v, p3, 0, p3, 2, fbias[0].reshape(1, 1, d), w1, ct=ct)
    g = _s2(o1, kf, 1, ff, fi, ct=d, conv=True)
    z = _s3(g, v, p3, 1, z[None], 0, fbias[1].reshape(1, 1, d), None, ct=ct)
    return z.reshape(bsz, n_lat, d)


def _rope_tables(n_tokens):
    rows = n_tokens // GRID_W
    row = jnp.broadcast_to(jnp.arange(rows, dtype=F32)[:, None], (rows, GRID_W)).reshape(-1)
    col = jnp.broadcast_to(jnp.arange(GRID_W, dtype=F32)[None, :], (rows, GRID_W)).reshape(-1)
    axis_dim = QK_ROPE // 2
    inv_freq = 1.0 / (ROPE_THETA ** (jnp.arange(0, axis_dim, 2, dtype=F32) / axis_dim))
    ang = jnp.concatenate([row[:, None] * inv_freq, col[:, None] * inv_freq], axis=-1)
    return jnp.cos(ang), jnp.sin(ang)


def _mla_weights(w_down, g_q, w_uq, g_kv, w_ukv):
    d = w_down.shape[0]
    nh = MLA_HEADS
    kpe = w_down[:, Q_LORA + KV_LORA:]
    w1, w2 = kpe[:, 0::2], kpe[:, 1::2]
    z = jnp.zeros((d, LANES - QK_ROPE), w_down.dtype)
    wd = jnp.concatenate([w_down[:, :Q_LORA + KV_LORA], w1, w2, z, w2, w1, z], axis=1).astype(BF16)
    uq = w_uq.reshape(Q_LORA, nh, QK_NOPE + QK_ROPE)
    pe = uq[:, :, QK_NOPE:]
    uq = jnp.concatenate([uq[:, :, :QK_NOPE], pe[:, :, 0::2], pe[:, :, 1::2]], axis=2)
    wuqT = uq.reshape(Q_LORA, nh * (QK_NOPE + QK_ROPE)).T.astype(BF16)
    ukv = w_ukv.reshape(KV_LORA, nh, QK_NOPE + V_DIM)
    wuk = ukv[:, :, :QK_NOPE].reshape(KV_LORA, nh * QK_NOPE).astype(BF16)
    wuvT = ukv[:, :, QK_NOPE:].reshape(KV_LORA, nh * V_DIM).T.astype(BF16)
    return wd, g_q.reshape(1, -1), g_kv.reshape(1, -1), wuk, wuqT, wuvT


def kernel(x, c, ctx, c_ctx, ada_w, ada_b, norm_mix_g, norm_ffn_g, mla_w_down, mla_g_q, mla_w_uq, mla_g_kv, mla_w_ukv, mla_w_o, hy_w_in, hy_b_in, hy_conv_w, hy_conv_b, hy_f_w1, hy_f_b1, hy_f_freq1, hy_f_w2, hy_f_b2, hy_f_freq2, hy_f_w3, hy_decay, hy_bias, hy_w_out, hy_b_out, moe_w_r, moe_b_r, moe_w_in, moe_b_in, moe_w_out, moe_b_out, final_g):
    bsz, n_lat, d = x.shape
    n_ctx = ctx.shape[1]
    depth = ada_w.shape[0]
    assert bsz == 2 and d == MLA_HEADS * V_DIM and n_lat % 512 == 0 and n_ctx % 128 == 0
    assert depth == 2

    cond8 = jnp.zeros((8, d), F32).at[:bsz].set(c).at[bsz].set(c_ctx)
    mods = _ada(cond8, ada_w, ada_b)

    def mod(i, j, rows):
        return mods[i, rows, j * d:(j + 1) * d][:, None, :]

    lat_rows = slice(0, bsz)
    ctx_rows = slice(bsz, bsz + 1)
    xl = x
    for i in range(depth):
        kind, j = i % 2, i // 2
        sh1, sc1, g1 = (mod(i, m, lat_rows) for m in range(3))
        sh2, sc2, g2 = (mod(i, m, lat_rows) for m in range(3, 6))
        gm = norm_mix_g[i].reshape(1, d)
        if kind == 0:
            wts = _mla_weights(mla_w_down[j], mla_g_q[j], mla_w_uq[j], mla_g_kv[j], mla_w_ukv[j])
            cos, sin = _rope_tables(n_lat)
            zl = jnp.zeros((n_lat, LANES - QK_ROPE), F32)
            tabs = (jnp.concatenate([cos, cos, zl], axis=1), jnp.concatenate([-sin, sin, zl], axis=1), cos.T, sin.T)
            tq = tv = 512
            tk = 2048 if n_lat % 4096 == 0 else 512
            qT, k, vT = _mla_proj(xl, gm, sh1, sc1, wts, tabs, need_q=True, tm=tv, tk=tv)
            half = QK_ROPE // 2
            one_c = jnp.concatenate([jnp.ones((n_ctx, QK_ROPE), F32), jnp.zeros((n_ctx, LANES - QK_ROPE), F32)], axis=1)
            tabs_c = (one_c, jnp.zeros((n_ctx, LANES), F32), jnp.ones((half, n_ctx), F32), jnp.zeros((half, n_ctx), F32))
            kc, vTc = _mla_proj(ctx, gm, mod(i, 0, ctx_rows), mod(i, 1, ctx_rows), wts, tabs_c,
                                need_q=False, tm=n_ctx, tk=n_ctx)
            o = _attention(qT, k, vT, kc, vTc, tq=tq, tk=tk)
            wo = mla_w_o[j].astype(BF16)
            bo = jnp.zeros((1, d), F32)
            transposed = True
        else:
            proj3 = _hy_in(xl, gm, sh1, sc1, hy_w_in[j].astype(BF16), hy_b_in[j].reshape(1, -1), hy_conv_w[j],
                           hy_conv_b[j].reshape(1, -1), tm=512)
            fparams = (hy_f_w1[j], hy_f_b1[j], hy_f_freq1[j], hy_f_w2[j], hy_f_b2[j], hy_f_freq2[j], hy_f_w3[j],
                       hy_decay[j])
            o = _hyena_mix(proj3, fparams, hy_bias[j], n_lat=n_lat)
            wo = hy_w_out[j].astype(BF16)
            bo = hy_b_out[j].reshape(1, d)
            transposed = False
        wr = jnp.zeros((d, LANES), F32).at[:, :N_EXPERTS].set(moe_w_r[i])
        wrh = wr.astype(BF16)
        wrl = (wr - wrh.astype(F32)).astype(BF16)
        br = jnp.zeros((1, LANES), F32).at[0, :N_EXPERTS].set(moe_b_r[i])
        xl, fl, topi, gates, rank, cnt = _post(o, wo, bo, xl, g1, norm_ffn_g[i].reshape(1, d), sh2, sc2, wrh, wrl, br,
                                               transposed=transposed, tm=512)
        xl = _moe(fl, topi, gates, rank, cnt, xl, g2, final_g.reshape(1, d), i, moe_w_in, moe_b_in,
                  moe_w_out, moe_b_out, final=(i == depth - 1))
    return xl
```

```python
import functools
import math

import jax
import jax.numpy as jnp
from jax import lax
from jax.experimental import pallas as pl
from jax.experimental.pallas import tpu as pltpu

F32 = jnp.float32
BF16 = jnp.bfloat16

EPS = 1e-6
GRID_W = 64
MLA_HEADS = 8
QK_NOPE = 128
QK_ROPE = 64
V_DIM = 128
Q_LORA = 512
KV_LORA = 256
ROPE_THETA = 10000.0
MLA_SCALE = (QK_NOPE + QK_ROPE) ** -0.5
QK_PAD = 256

HY_EMB = 33
HY_BANDS = (HY_EMB - 1) // 2
HY_HID = 64
FFT_N2 = 128
B_GROUP = 8

N_EXPERTS = 32
TOP_K = 4
SWIGLU_LIMIT = 7.0
SWIGLU_ALPHA = 1.702
MOE_TM = 256
LANES = 128

VMEM_LIMIT = 56 * 1024 * 1024


def _cp(sem, vmem=VMEM_LIMIT):
    return pltpu.CompilerParams(dimension_semantics=sem, vmem_limit_bytes=vmem)


def _dot(a, b):
    return jnp.dot(a, b, preferred_element_type=F32)


def _dot_hi(a, b):
    return jnp.dot(a, b, preferred_element_type=F32, precision=lax.Precision.HIGHEST)


def _rms(x, g):
    return x * lax.rsqrt(jnp.mean(x * x, axis=-1, keepdims=True) + EPS) * g


def _ada_body(c_ref, w_ref, b_ref, o_ref):
    c = c_ref[...]
    s = c * jax.nn.sigmoid(c)
    o_ref[0] = _dot(s.astype(BF16), w_ref[0].astype(BF16)) + b_ref[0]


def _ada(cond8, ada_w, ada_b):
    depth, d, n = ada_w.shape
    tn = n // 4
    return pl.pallas_call(
        _ada_body,
        grid=(depth, n // tn),
        in_specs=[
            pl.BlockSpec((8, d), lambda i, j: (0, 0)),
            pl.BlockSpec((1, d, tn), lambda i, j: (i, 0, j)),
            pl.BlockSpec((1, 1, tn), lambda i, j: (i, 0, j)),
        ],
        out_specs=pl.BlockSpec((1, 8, tn), lambda i, j: (i, 0, j)),
        out_shape=jax.ShapeDtypeStruct((depth, 8, n), F32),
        compiler_params=_cp(("arbitrary", "arbitrary")),
        name="ada_mod",
    )(cond8, ada_w, ada_b.reshape(depth, 1, n))


def _mla_proj_body(x_ref, g_ref, sh_ref, sc_ref, wd_ref, gq_ref, gkv_ref, wuk_ref, wuqT_ref, wuvT_ref,
                   ct_ref, st_ref, cT_ref, sT_ref, *out_refs, need_q, tk):
    if need_q:
        qT_ref, k_ref, vT_ref = out_refs
    else:
        k_ref, vT_ref = out_refs
    nh = MLA_HEADS
    x = x_ref[0]
    h = _rms(x, g_ref[...]) * (1.0 + sc_ref[0]) + sh_ref[0]
    lat = _dot(h.astype(BF16), wd_ref[...])
    o_kv = Q_LORA
    o_a = Q_LORA + KV_LORA
    kvn = _rms(lat[:, o_kv:o_a], gkv_ref[...])
    kr = (lat[:, o_a:o_a + LANES] * ct_ref[...] + lat[:, o_a + LANES:o_a + 2 * LANES] * st_ref[...]).astype(BF16)
    knope = _dot(kvn.astype(BF16), wuk_ref[...])
    for hh in range(nh):
        k_ref[0, hh, :, 0:QK_NOPE] = knope[:, hh * QK_NOPE:(hh + 1) * QK_NOPE].astype(BF16)
        k_ref[0, hh, :, QK_NOPE:QK_PAD] = kr
    vT = _dot(wuvT_ref[...], kvn.T.astype(BF16))
    tm = x.shape[0]
    for hh in range(nh):
        for c in range(tm // tk):
            vT_ref[0, hh, c] = vT[hh * V_DIM:(hh + 1) * V_DIM, c * tk:(c + 1) * tk].astype(BF16)
    if need_q:
        qn = _rms(lat[:, :Q_LORA], gq_ref[...])
        qT = _dot(wuqT_ref[...], qn.T.astype(BF16)) * (MLA_SCALE * math.log2(math.e))
        c = cT_ref[...]
        s = sT_ref[...]
        hw = QK_NOPE + QK_ROPE
        half = QK_ROPE // 2
        for hh in range(nh):
            base = hh * hw
            x1 = qT[base + QK_NOPE:base + QK_NOPE + half]
            x2 = qT[base + QK_NOPE + half:base + hw]
            qT_ref[0, hh, 0:QK_NOPE] = qT[base:base + QK_NOPE].astype(BF16)
            qT_ref[0, hh, QK_NOPE:QK_NOPE + half] = (x1 * c - x2 * s).astype(BF16)
            qT_ref[0, hh, QK_NOPE + half:hw] = (x1 * s + x2 * c).astype(BF16)
            qT_ref[0, hh, hw:QK_PAD] = jnp.zeros((QK_PAD - hw, tm), BF16)


def _mla_proj(x, g, sh, sc, wts, tabs, *, need_q, tm, tk):
    bsz, n, d = x.shape
    nh = MLA_HEADS
    wd, gq, gkv, wuk, wuqT, wuvT = wts
    ct, st, cT, sT = tabs
    nsh = sh.shape[0]
    full = lambda a: pl.BlockSpec(a.shape, lambda b, i: (0,) * a.ndim)
    in_specs = [
        pl.BlockSpec((1, tm, d), lambda b, i: (b, i, 0)),
        full(g),
        pl.BlockSpec((1, 1, d), lambda b, i: (b % nsh, 0, 0)),
        pl.BlockSpec((1, 1, d), lambda b, i: (b % nsh, 0, 0)),
        full(wd), full(gq), full(gkv), full(wuk), full(wuqT), full(wuvT),
        pl.BlockSpec((tm, LANES), lambda b, i: (i, 0)),
        pl.BlockSpec((tm, LANES), lambda b, i: (i, 0)),
        pl.BlockSpec((QK_ROPE // 2, tm), lambda b, i: (0, i)),
        pl.BlockSpec((QK_ROPE // 2, tm), lambda b, i: (0, i)),
    ]
    out_specs = [
        pl.BlockSpec((1, nh, tm, QK_PAD), lambda b, i: (b, 0, i, 0)),
        pl.BlockSpec((1, nh, tm // tk, V_DIM, tk), lambda b, i: (b, 0, i, 0, 0)),
    ]
    out_shape = [
        jax.ShapeDtypeStruct((bsz, nh, n, QK_PAD), BF16),
        jax.ShapeDtypeStruct((bsz, nh, n // tk, V_DIM, tk), BF16),
    ]
    if need_q:
        out_specs = [pl.BlockSpec((1, nh, QK_PAD, tm), lambda b, i: (b, 0, 0, i))] + out_specs
        out_shape = [jax.ShapeDtypeStruct((bsz, nh, QK_PAD, n), BF16)] + out_shape
    return pl.pallas_call(
        functools.partial(_mla_proj_body, need_q=need_q, tk=tk),
        grid=(bsz, n // tm),
        in_specs=in_specs,
        out_specs=out_specs,
        out_shape=out_shape,
        compiler_params=_cp(("arbitrary", "arbitrary")),
        name="mla_proj_q" if need_q else "mla_proj_ctx",
    )(x, g, sh, sc, wd, gq, gkv, wuk, wuqT, wuvT, ct, st, cT, sT)


SM_STRIP = 64
SUBLANES = 8


def _attn_body(qT_ref, k_ref, vT_ref, kc_ref, vTc_ref, o_ref, s0, s1, p0, p1, sc, pc, acc, m_scr, x0, x1, xc,
               a0, a1, ac, l_scr, d0, d1, dc, *, tk):
    nchunk = k_ref.shape[2] // tk

    def scores(kblk, s_ref, mx_ref):
        r = _dot(kblk, qT_ref[0, 0])
        s_ref[...] = r
        mx_ref[...] = jnp.max(r, axis=0, keepdims=True)

    def probs(s_ref, mx_ref, p_ref, a_ref, d_ref):
        m_old = m_scr[...]
        m_new = jnp.maximum(m_old, mx_ref[...])
        m_scr[...] = m_new
        alpha = jnp.exp2(m_old - m_new)
        a_ref[...] = alpha
        part = None
        for r in range(0, s_ref.shape[0], SM_STRIP):
            p = jnp.exp2(s_ref[r:r + SM_STRIP] - m_new)
            p_ref[r:r + SM_STRIP] = p.astype(BF16)
            ps = jnp.sum(p.reshape(SM_STRIP // SUBLANES, SUBLANES, p.shape[1]), axis=0)
            part = ps if part is None else part + ps
        d_ref[...] = part

    def accumulate(p_ref, a_ref, d_ref, vblk):
        acc[...] = a_ref[...] * acc[...] + _dot(vblk, p_ref[...])
        l_scr[...] = a_ref[...] * l_scr[...] + d_ref[...]

    def kchunk(i):
        return k_ref[0, 0, pl.ds(pl.multiple_of(i * tk, tk), tk), :]

    def vchunk(i):
        nsub = tk // vT_ref.shape[-1]
        return jnp.concatenate([vT_ref[0, 0, i * nsub + u] for u in range(nsub)], axis=1)

    m_scr[...] = jnp.full(m_scr.shape, -jnp.inf, F32)
    acc[...] = jnp.zeros(acc.shape, F32)
    l_scr[...] = jnp.zeros(l_scr.shape, F32)
    scores(kc_ref[0, 0], sc, xc)
    scores(kchunk(0), s0, x0)
    probs(sc, xc, pc, ac, dc)
    scores(kchunk(1), s1, x1)
    accumulate(pc, ac, dc, vTc_ref[0, 0, 0])
    probs(s0, x0, p0, a0, d0)

    def body(j, carry):
        t = 2 * j
        scores(kchunk(t + 2), s0, x0)
        accumulate(p0, a0, d0, vchunk(t))
        probs(s1, x1, p1, a1, d1)
        scores(kchunk(t + 3), s1, x1)
        accumulate(p1, a1, d1, vchunk(t + 1))
        probs(s0, x0, p0, a0, d0)
        return carry

    lax.fori_loop(0, nchunk // 2 - 1, body, 0)
    accumulate(p0, a0, d0, vchunk(nchunk - 2))
    probs(s1, x1, p1, a1, d1)
    accumulate(p1, a1, d1, vchunk(nchunk - 1))
    o_ref[0, 0] = (acc[...] / jnp.sum(l_scr[...], axis=0, keepdims=True)).astype(BF16)


def _attention(qT, k, vT, kc, vTc, *, tq, tk):
    bsz, nh, _, n = qT.shape
    nc = kc.shape[2]
    tv = vT.shape[-1]
    assert (n // tk) % 2 == 0 and tk % tv == 0
    return pl.pallas_call(
        functools.partial(_attn_body, tk=tk),
        grid=(bsz, nh, n // tq),
        in_specs=[
            pl.BlockSpec((1, 1, QK_PAD, tq), lambda b, h, i: (b, h, 0, i)),
            pl.BlockSpec((1, 1, n, QK_PAD), lambda b, h, i: (b, h, 0, 0)),
            pl.BlockSpec((1, 1, n // tv, V_DIM, tv), lambda b, h, i: (b, h, 0, 0, 0)),
            pl.BlockSpec((1, 1, nc, QK_PAD), lambda b, h, i: (b, h, 0, 0)),
            pl.BlockSpec((1, 1, 1, V_DIM, nc), lambda b, h, i: (b, h, 0, 0, 0)),
        ],
        out_specs=pl.BlockSpec((1, 1, V_DIM, tq), lambda b, h, i: (b, h, 0, i)),
        out_shape=jax.ShapeDtypeStruct((bsz, nh, V_DIM, n), BF16),
        scratch_shapes=[pltpu.VMEM((tk, tq), F32), pltpu.VMEM((tk, tq), F32),
                        pltpu.VMEM((tk, tq), BF16), pltpu.VMEM((tk, tq), BF16),
                        pltpu.VMEM((nc, tq), F32), pltpu.VMEM((nc, tq), BF16),
                        pltpu.VMEM((V_DIM, tq), F32), pltpu.VMEM((1, tq), F32),
                        pltpu.VMEM((1, tq), F32), pltpu.VMEM((1, tq), F32), pltpu.VMEM((1, tq), F32),
                        pltpu.VMEM((1, tq), F32), pltpu.VMEM((1, tq), F32), pltpu.VMEM((1, tq), F32),
                        pltpu.VMEM((SUBLANES, tq), F32), pltpu.VMEM((SUBLANES, tq), F32),
                        pltpu.VMEM((SUBLANES, tq), F32), pltpu.VMEM((SUBLANES, tq), F32)],
        compiler_params=_cp(("arbitrary", "arbitrary", "arbitrary")),
        name="mla_attention",
    )(qT, k, vT, kc, vTc)


def _post_body(o_ref, wo_ref, bo_ref, x_ref, g1_ref, gf_ref, sh_ref, sc_ref, wrh_ref, wrl_ref, br_ref, tri_ref,
               xl_ref, fl_ref, ti_ref, gt_ref, rk_ref, cnt_ref, *, transposed):
    @pl.when((pl.program_id(0) == 0) & (pl.program_id(1) == 0))
    def _():
        cnt_ref[...] = jnp.zeros_like(cnt_ref)

    tm = x_ref.shape[1]
    if transposed:
        oT = o_ref[0].astype(F32).reshape(MLA_HEADS * V_DIM, tm)
        o = oT.T.astype(BF16)
    else:
        o = o_ref[0].astype(BF16)
    y = _dot(o, wo_ref[...]) + bo_ref[...]
    xl = x_ref[0] + g1_ref[0] * y
    xl_ref[0] = xl
    fl = _rms(xl, gf_ref[...]) * (1.0 + sc_ref[0]) + sh_ref[0]
    _to_rows(fl_ref, fl)
    flh = fl.astype(BF16)
    fll = (fl - flh.astype(F32)).astype(BF16)
    logits = _dot(flh, wrh_ref[...]) + (_dot(fll, wrh_ref[...]) + _dot(flh, wrl_ref[...])) + br_ref[...]
    lane = lax.broadcasted_iota(jnp.int32, (tm, LANES), 1).astype(F32)
    neg = jnp.float32(-jnp.inf)
    work = jnp.where(lane < N_EXPERTS, logits, neg)
    vals, idxs = [], []
    onehot = jnp.zeros((tm, LANES), F32)
    for _ in range(TOP_K):
        mk = jnp.max(work, axis=-1, keepdims=True)
        ik = jnp.min(jnp.where(work == mk, lane, float(LANES)), axis=-1, keepdims=True)
        sel = lane == ik
        onehot = jnp.where(sel, 1.0, onehot)
        work = jnp.where(sel, neg, work)
        vals.append(mk)
        idxs.append(ik)
    es = [jnp.exp(v - vals[0]) for v in vals]
    den = es[0] + es[1] + es[2] + es[3]
    pre = _dot(tri_ref[...], onehot.astype(BF16)) + cnt_ref[...]
    ti = jnp.zeros((tm, LANES), F32)
    gt = jnp.zeros((tm, LANES), F32)
    rk = jnp.zeros((tm, LANES), F32)
    for kk in range(TOP_K):
        rank = jnp.sum(jnp.where(lane == idxs[kk], pre, 0.0), axis=-1, keepdims=True)
        ti = jnp.where(lane == kk, idxs[kk], ti)
        gt = jnp.where(lane == kk, es[kk] / den, gt)
        rk = jnp.where(lane == kk, rank, rk)
    ti_ref[...] = ti[:, :TOP_K].astype(jnp.int32)
    gt_ref[...] = gt[:, :TOP_K]
    rk_ref[...] = rk[:, :TOP_K].astype(jnp.int32)
    cnt_ref[...] += jnp.sum(onehot, axis=0, keepdims=True)


def _post(o, wo, bo, x, g1, gf, sh, sc, wrh, wrl, br, *, transposed, tm):
    bsz, n, d = x.shape
    t = bsz * n
    nt = n // tm
    tri = (lax.broadcasted_iota(jnp.int32, (tm, tm), 0) > lax.broadcasted_iota(jnp.int32, (tm, tm), 1)).astype(BF16)
    full = lambda a: pl.BlockSpec(a.shape, lambda b, i: (0,) * a.ndim)
    per_b = pl.BlockSpec((1, 1, d), lambda b, i: (b, 0, 0))
    if transposed:
        o_spec = pl.BlockSpec((1, MLA_HEADS, V_DIM, tm), lambda b, i: (b, 0, 0, i))
    else:
        o_spec = pl.BlockSpec((1, tm, d), lambda b, i: (b, i, 0))
    tok = lambda w: pl.BlockSpec((tm, w), lambda b, i: (b * nt + i, 0))
    return pl.pallas_call(
        functools.partial(_post_body, transposed=transposed),
        grid=(bsz, nt),
        in_specs=[o_spec, full(wo), full(bo), pl.BlockSpec((1, tm, d), lambda b, i: (b, i, 0)), per_b, full(gf),
                  per_b, per_b, full(wrh), full(wrl), full(br), full(tri)],
        out_specs=[pl.BlockSpec((1, tm, d), lambda b, i: (b, i, 0)),
                   pl.BlockSpec((tm * ROW_SUB, LANES), lambda b, i: (b * nt + i, 0)),
                   tok(TOP_K), tok(TOP_K), tok(TOP_K), pl.BlockSpec((1, LANES), lambda b, i: (0, 0))],
        out_shape=[jax.ShapeDtypeStruct((bsz, n, d), F32), jax.ShapeDtypeStruct((t * ROW_SUB, LANES), F32),
                   jax.ShapeDtypeStruct((t, TOP_K), jnp.int32), jax.ShapeDtypeStruct((t, TOP_K), F32),
                   jax.ShapeDtypeStruct((t, TOP_K), jnp.int32), jax.ShapeDtypeStruct((1, LANES), F32)],
        compiler_params=_cp(("arbitrary", "arbitrary")),
        name="post_attn" if transposed else "post_hyena",
    )(o, wo, bo, x, g1, gf, sh, sc, wrh, wrl, br, tri)


ROW_SUB = 8


def _row_slice(i):
    return pl.ds(pl.multiple_of(i * ROW_SUB, ROW_SUB), ROW_SUB)


def _to_rows(ref, x):
    for s in range(ROW_SUB):
        ref[pl.ds(s, x.shape[0], stride=ROW_SUB), :] = x[:, s * LANES:(s + 1) * LANES]


def _from_rows(ref, lo, hi):
    return jnp.concatenate([ref[pl.ds(lo * ROW_SUB + s, hi - lo, stride=ROW_SUB), :] for s in range(ROW_SUB)], axis=1)


def _dispatch_body(pe_ref, pd_ref, dest_ref, fl_ref, xs_out, zbuf, sem, *, td):
    @pl.when(pl.program_id(0) == 0)
    def _():
        zbuf[...] = jnp.zeros(zbuf.shape, zbuf.dtype)
        for e in range(N_EXPERTS):
            @pl.when(pd_ref[e] > 0)
            def _():
                start = pl.multiple_of((pe_ref[e] - MOE_TM) * ROW_SUB, ROW_SUB)
                cp = pltpu.make_async_copy(zbuf, xs_out.at[pl.ds(start, MOE_TM * ROW_SUB)], sem)
                cp.start()
                cp.wait()

    def issue(t, carry):
        for kk in range(TOP_K):
            d = dest_ref[0, 0, t * TOP_K + kk]
            pltpu.make_async_copy(fl_ref.at[_row_slice(t)], xs_out.at[_row_slice(d)], sem).start(priority=kk % 2)
        return carry

    lax.fori_loop(0, td, issue, 0, unroll=2)

    def drain(t, carry):
        pltpu.make_async_copy(fl_ref.at[_row_slice(0)], xs_out.at[_row_slice(0)], sem).wait()
        return carry

    lax.fori_loop(0, td * TOP_K, drain, 0, unroll=8)


def _dispatch(pad_end, padded, dest, fl, n_rows, *, td):
    t = fl.shape[0] // ROW_SUB
    dest3 = dest.reshape(t // td, 1, td * TOP_K)
    grid_spec = pltpu.PrefetchScalarGridSpec(
        num_scalar_prefetch=2,
        grid=(t // td,),
        in_specs=[
            pl.BlockSpec((1, 1, td * TOP_K), lambda i, pe, pd: (i, 0, 0), memory_space=pltpu.SMEM),
            pl.BlockSpec((td * ROW_SUB, LANES), lambda i, pe, pd: (i, 0)),
        ],
        out_specs=pl.BlockSpec(memory_space=pl.ANY),
        scratch_shapes=[pltpu.VMEM((MOE_TM * ROW_SUB, LANES), fl.dtype), pltpu.SemaphoreType.DMA(())],
    )
    return pl.pallas_call(
        functools.partial(_dispatch_body, td=td),
        grid_spec=grid_spec,
        out_shape=jax.ShapeDtypeStruct((n_rows * ROW_SUB, LANES), fl.dtype),
        compiler_params=_cp(("arbitrary",)),
        name="moe_dispatch",
    )(pad_end, padded, dest3, fl)


def _expert_body(be_ref, nu_ref, xs_ref, win_ref, bin_ref, wout_ref, bout_ref, ys_ref, win_s, wout_s):
    b = pl.program_id(0)
    dff = wout_ref.shape[1]

    @pl.when(b < nu_ref[0])
    def _():
        prev = be_ref[jnp.maximum(b - 1, 0)]

        @pl.when((b == 0) | (prev != be_ref[b]))
        def _():
            win_s[...] = win_ref[0].astype(BF16)
            wout_s[...] = wout_ref[0].astype(BF16)

        x = _from_rows(xs_ref, 0, xs_ref.shape[0] // ROW_SUB).astype(BF16)
        gu = _dot(x, win_s[...]) + bin_ref[0]
        gate = jnp.minimum(gu[:, :dff], SWIGLU_LIMIT)
        lin = jnp.clip(gu[:, dff:], -SWIGLU_LIMIT, SWIGLU_LIMIT)
        act = gate * jax.nn.sigmoid(SWIGLU_ALPHA * gate) * (lin + 1.0)
        _to_rows(ys_ref, _dot(act.astype(BF16), wout_s[...]) + bout_ref[0])

    @pl.when(b >= nu_ref[0])
    def _():
        ys_ref[...] = jnp.zeros_like(ys_ref)


def _experts(blk_exp, n_used, xs, layer, w_in, b_in, w_out, b_out):
    n_rows = xs.shape[0] // ROW_SUB
    depth, ne, d, f2 = w_in.shape
    dff = w_out.shape[2]
    tm = MOE_TM
    grid_spec = pltpu.PrefetchScalarGridSpec(
        num_scalar_prefetch=2,
        grid=(n_rows // tm,),
        in_specs=[
            pl.BlockSpec((tm * ROW_SUB, LANES), lambda b, be, nu: (jnp.minimum(b, nu[0] - 1), 0)),
            pl.BlockSpec((None, 1, d, f2), lambda b, be, nu: (layer, be[b], 0, 0)),
            pl.BlockSpec((None, 1, 1, f2), lambda b, be, nu: (layer, be[b], 0, 0)),
            pl.BlockSpec((None, 1, dff, d), lambda b, be, nu: (layer, be[b], 0, 0)),
            pl.BlockSpec((None, 1, 1, d), lambda b, be, nu: (layer, be[b], 0, 0)),
        ],
        out_specs=pl.BlockSpec((tm * ROW_SUB, LANES), lambda b, be, nu: (b, 0)),
        scratch_shapes=[pltpu.VMEM((d, f2), BF16), pltpu.VMEM((dff, d), BF16)],
    )
    return pl.pallas_call(
        _expert_body,
        grid_spec=grid_spec,
        out_shape=jax.ShapeDtypeStruct(xs.shape, F32),
        compiler_params=_cp(("arbitrary",)),
        name="moe_experts",
    )(blk_exp, n_used, xs, w_in, b_in.reshape(depth, ne, 1, f2), w_out, b_out.reshape(depth, ne, 1, d))


def _combine_body(dest_ref, ys_hbm, gt_ref, xl_ref, g2_ref, fg_ref, out_ref, buf, sem, *, tc, final):
    def issue(t, carry):
        for kk in range(TOP_K):
            d = dest_ref[0, 0, t * TOP_K + kk]
            pltpu.make_async_copy(ys_hbm.at[_row_slice(d)], buf.at[_row_slice(kk * tc + t)], sem).start(
                priority=kk % 2)
        return carry

    lax.fori_loop(0, tc, issue, 0, unroll=2)

    def drain(t, carry):
        pltpu.make_async_copy(ys_hbm.at[_row_slice(0)], buf.at[_row_slice(0)], sem).wait()
        return carry

    lax.fori_loop(0, tc * TOP_K, drain, 0, unroll=8)
    gt = gt_ref[...]
    y = gt[:, 0:1] * _from_rows(buf, 0, tc)
    for kk in range(1, TOP_K):
        y = y + gt[:, kk:kk + 1] * _from_rows(buf, kk * tc, (kk + 1) * tc)
    xl = xl_ref[0] + g2_ref[0] * y
    out_ref[0] = _rms(xl, fg_ref[...]) if final else xl


def _combine(dest, ys, gates, xl, g2, fg, *, tc, final):
    bsz, n, d = xl.shape
    t = bsz * n
    nt = n // tc
    dest3 = dest.reshape(t // tc, 1, tc * TOP_K)
    return pl.pallas_call(
        functools.partial(_combine_body, tc=tc, final=final),
        grid=(bsz, nt),
        in_specs=[
            pl.BlockSpec((1, 1, tc * TOP_K), lambda b, i: (b * nt + i, 0, 0), memory_space=pltpu.SMEM),
            pl.BlockSpec(memory_space=pl.ANY),
            pl.BlockSpec((tc, TOP_K), lambda b, i: (b * nt + i, 0)),
            pl.BlockSpec((1, tc, d), lambda b, i: (b, i, 0)),
            pl.BlockSpec((1, 1, d), lambda b, i: (b, 0, 0)),
            pl.BlockSpec((1, d), lambda b, i: (0, 0)),
        ],
        out_specs=pl.BlockSpec((1, tc, d), lambda b, i: (b, i, 0)),
        out_shape=jax.ShapeDtypeStruct((bsz, n, d), F32),
        scratch_shapes=[pltpu.VMEM((TOP_K * tc * ROW_SUB, LANES), F32), pltpu.SemaphoreType.DMA(())],
        compiler_params=_cp(("arbitrary", "arbitrary")),
        name="moe_combine",
    )(dest3, ys, gates, xl, g2, fg)


def _moe(fl, topi, gates, rank, cnt, xl, g2, fg, layer, w_in, b_in, w_out, b_out, *, final):
    t = fl.shape[0] // ROW_SUB
    tm = MOE_TM
    counts = cnt[0, :N_EXPERTS].astype(jnp.int32)
    padded = (counts + tm - 1) // tm * tm
    pad_end = jnp.cumsum(padded)
    pad_start = pad_end - padded
    dest = jnp.take(pad_start, topi) + rank
    nb = t * TOP_K // tm + N_EXPERTS
    blk_start = jnp.arange(nb, dtype=jnp.int32) * tm
    blk_exp = jnp.minimum(jnp.sum((pad_end[None, :] <= blk_start[:, None]).astype(jnp.int32), axis=1), N_EXPERTS - 1)
    n_used = (pad_end[-1:] // tm).astype(jnp.int32)
    xs = _dispatch(pad_end, padded, dest, fl, nb * tm, td=512)
    ys = _experts(blk_exp, n_used, xs, layer, w_in, b_in, w_out, b_out)
    return _combine(dest, ys, gates, xl, g2, fg, tc=512, final=final)


def _hy_in_body(x_ref, xp_ref, xn_ref, g_ref, sh_ref, sc_ref, w_ref, b_ref, cw_ref, cb_ref, o_ref, *, nt):
    i = pl.program_id(1)
    d = x_ref.shape[2]

    def normed(xx):
        return (_rms(xx, g_ref[...]) * (1.0 + sc_ref[0]) + sh_ref[0]).astype(BF16)

    h = normed(x_ref[0])
    hh = normed(jnp.concatenate([xp_ref[0], xn_ref[0]], axis=0))
    tm = h.shape[0]
    row = lax.broadcasted_iota(jnp.int32, (tm, 1), 0)
    for j in range(3):
        cols = slice(j * d, (j + 1) * d)
        p = _dot(h, w_ref[:, cols]) + b_ref[:, cols]
        ph = _dot(hh, w_ref[:, cols]) + b_ref[:, cols]
        prev = jnp.where(i > 0, ph[7:8], 0.0)
        nxt = jnp.where(i < nt - 1, ph[8:9], 0.0)
        up = jnp.where(row == 0, prev, pltpu.roll(p, 1, axis=0))
        dn = jnp.where(row == tm - 1, nxt, pltpu.roll(p, tm - 1, axis=0))
        cw = cw_ref[:, cols]
        o_ref[j, 0] = up * cw[0:1] + p * cw[1:2] + dn * cw[2:3] + cb_ref[:, cols]


def _hy_in(x, g, sh, sc, w, b, cw, cb, *, tm):
    bsz, n, d = x.shape
    nt = n // tm
    hb = tm // 8
    per_b = pl.BlockSpec((1, 1, d), lambda bb, i: (bb, 0, 0))
    full = lambda a: pl.BlockSpec(a.shape, lambda bb, i: (0,) * a.ndim)
    return pl.pallas_call(
        functools.partial(_hy_in_body, nt=nt),
        grid=(bsz, nt),
        in_specs=[
            pl.BlockSpec((1, tm, d), lambda bb, i: (bb, i, 0)),
            pl.BlockSpec((1, 8, d), lambda bb, i: (bb, jnp.maximum(i * hb - 1, 0), 0)),
            pl.BlockSpec((1, 8, d), lambda bb, i: (bb, jnp.minimum((i + 1) * hb, n // 8 - 1), 0)),
            full(g), per_b, per_b, full(w), full(b), full(cw), full(cb),
        ],
        out_specs=pl.BlockSpec((3, 1, tm, d), lambda bb, i: (0, bb, i, 0)),
        out_shape=jax.ShapeDtypeStruct((3, bsz, n, d), F32),
        compiler_params=_cp(("arbitrary", "arbitrary")),
        name="hyena_in_proj",
    )(x, x, x, g, sh, sc, w, b, cw, cb)


def _filt_feat_body(w1_ref, b1_ref, f1_ref, w2_ref, b2_ref, f2_ref, o_ref, *, n_lat):
    na = o_ref.shape[1]
    a = lax.broadcasted_iota(jnp.int32, (na, 1), 0)
    lane = lax.broadcasted_iota(jnp.int32, (na, LANES), 1)
    band_idx = jnp.where(lane <= HY_BANDS, lane - 1, lane - 1 - HY_BANDS).astype(F32)
    band = 1e-4 + band_idx * ((HY_BANDS - 1 - 1e-4) / (HY_BANDS - 1))
    for j in range(B_GROUP):
        r = a * FFT_N2 + (pl.program_id(0) * B_GROUP + j)
        pos = jnp.where(r < n_lat, r, 2 * n_lat - r).astype(F32)
        tn = pos / float(max(n_lat - 1, 1))
        ang = ((2.0 * math.pi / n_lat) * pos) * band
        z = jnp.where(lane == 0, tn, jnp.where(lane <= HY_BANDS, jnp.cos(ang),
                                               jnp.where(lane < HY_EMB, -jnp.sin(ang), 0.0)))
        h1 = jnp.sin(f1_ref[...] * (_dot_hi(z, w1_ref[...]) + b1_ref[...]))
        h2 = jnp.sin(f2_ref[...] * (_dot_hi(h1, w2_ref[...]) + b2_ref[...]))
        valid = (r != n_lat).astype(F32)
        o_ref[j] = jnp.where(lane == HY_HID, tn, jnp.where(lane == HY_HID + 1, valid, h2))


def _filt_feat(w1, b1, f1, w2, b2, f2, *, n_lat):
    na = 2 * n_lat // FFT_N2
    w1p = jnp.zeros((LANES, LANES), F32).at[:HY_EMB, :HY_HID].set(w1)
    w2p = jnp.zeros((LANES, LANES), F32).at[:HY_HID, :HY_HID].set(w2)
    padv = lambda v: jnp.zeros((1, LANES), F32).at[0, :HY_HID].set(v)
    full = lambda shp: pl.BlockSpec(shp, lambda i: (0,) * len(shp))
    return pl.pallas_call(
        functools.partial(_filt_feat_body, n_lat=n_lat),
        grid=(FFT_N2 // B_GROUP,),
        in_specs=[full((LANES, LANES)), full((1, LANES)), full((1, LANES)),
                  full((LANES, LANES)), full((1, LANES)), full((1, LANES))],
        out_specs=pl.BlockSpec((B_GROUP, na, LANES), lambda i: (i, 0, 0)),
        out_shape=jax.ShapeDtypeStruct((FFT_N2, na, LANES), F32),
        compiler_params=_cp(("arbitrary",)),
        name="hyena_filter_features",
    )(w1p, padv(b1), padv(f1), w2p, padv(b2), padv(f2))


U32 = jnp.uint32
HI16 = 0xFFFF0000


def _pack_c(re, im):
    lo = lax.bitcast_convert_type(re.astype(BF16).astype(F32), U32) >> 16
    hi = lax.bitcast_convert_type(im.astype(BF16).astype(F32), U32) & U32(HI16)
    return hi | lo


def _unpack_c(u):
    re = lax.bitcast_convert_type(u << 16, F32)
    im = lax.bitcast_convert_type(u & U32(HI16), F32)
    return jnp.concatenate([re, im], axis=0).astype(BF16)


def _filt_s1_body(hd_ref, w3_ref, dec_ref, tab_ref, o_ref):
    na = hd_ref.shape[1]
    ha = na // 2
    o2 = _rows2d(o_ref)
    ft = hd_ref[:, :ha, :].reshape(B_GROUP * ha, LANES)
    fb = hd_ref[:, ha:, :].reshape(B_GROUP * ha, LANES)
    top = _dot(ft.astype(BF16), w3_ref[0, 0].astype(BF16))
    top = top * jnp.exp(-ft[:, HY_HID:HY_HID + 1] * jnp.abs(dec_ref[0, 0]))
    bot = _dot(fb.astype(BF16), w3_ref[0, 1].astype(BF16))
    bot = bot * (jnp.exp(-fb[:, HY_HID:HY_HID + 1] * jnp.abs(dec_ref[0, 1])) * fb[:, HY_HID + 1:HY_HID + 2])
    for j in range(B_GROUP):
        hb = jnp.concatenate([top[j * ha:(j + 1) * ha], bot[j * ha:(j + 1) * ha]], axis=0).astype(BF16)
        r = _dot(tab_ref[j], hb)
        o2[pl.ds(j, na, stride=B_GROUP), :] = _pack_c(r[:na], r[na:])


def _filt_s1(hd, w3r, dec, tab, *, ct):
    _, na, _ = hd.shape
    d = w3r.shape[-1]
    return pl.pallas_call(
        _filt_s1_body,
        grid=(2, FFT_N2 // B_GROUP, d // ct),
        in_specs=[
            pl.BlockSpec((B_GROUP, na, LANES), lambda o, g, c: (g, 0, 0)),
            pl.BlockSpec((1, 2, LANES, ct), lambda o, g, c: (o, 0, 0, c)),
            pl.BlockSpec((1, 2, 1, ct), lambda o, g, c: (o, 0, 0, c)),
            pl.BlockSpec((B_GROUP, 2 * na, na), lambda o, g, c: (g, 0, 0)),
        ],
        out_specs=pl.BlockSpec((None, na, B_GROUP, ct), lambda o, g, c: (o, 0, g, c)),
        out_shape=jax.ShapeDtypeStruct((2, na, FFT_N2, d), U32),
        compiler_params=_cp(("arbitrary", "arbitrary", "arbitrary")),
        name="hyena_filter_dft1",
    )(hd, w3r, dec, tab)


S2_KB = 4


def _s2_body(*refs, conv):
    if conv:
        o_ref, kf_ref, ff_ref, fi_ref, g_ref = refs
    else:
        o_ref, ff_ref, g_ref = refs
    for u in range(S2_KB):
        xf = _dot(ff_ref[...], _unpack_c(o_ref[u]))
        if conv:
            xr, xi = xf[:FFT_N2], xf[FFT_N2:]
            kr = kf_ref[0, u].astype(F32)
            ki = kf_ref[1, u].astype(F32)
            y = jnp.concatenate([xr * kr - xi * ki, xr * ki + xi * kr], axis=0).astype(BF16)
            xf = _dot(fi_ref[...], y)
            g_ref[u] = _pack_c(xf[:FFT_N2], xf[FFT_N2:])
        else:
            g_ref[0, u] = xf[:FFT_N2].astype(BF16)
            g_ref[1, u] = xf[FFT_N2:].astype(BF16)


def _s2(o3, kf, order, ff, fi, *, ct, conv):
    n1 = o3.shape[-3]
    d = o3.shape[-1]
    full = lambda a: pl.BlockSpec(a.shape, lambda k, c: (0,) * a.ndim)
    nk = n1 // S2_KB
    if conv:
        blk = pl.BlockSpec((S2_KB, FFT_N2, ct), lambda k, c: (k, 0, c))
        in_specs = [blk, pl.BlockSpec((None, 2, S2_KB, FFT_N2, ct), lambda k, c: (order, 0, k, 0, c)), full(ff),
                    full(fi)]
        args = (o3, kf, ff, fi)
        grid = (nk, d // ct)
        out_specs = blk
        out_shape = jax.ShapeDtypeStruct(o3.shape, U32)
    else:
        no = o3.shape[0]
        in_specs = [pl.BlockSpec((None, S2_KB, FFT_N2, ct), lambda k, c: (k // nk, k % nk, 0, c)), full(ff)]
        args = (o3, ff)
        grid = (no * nk, d // ct)
        out_specs = pl.BlockSpec((None, 2, S2_KB, FFT_N2, ct), lambda k, c: (k // nk, 0, k % nk, 0, c))
        out_shape = jax.ShapeDtypeStruct((no, 2, n1, FFT_N2, d), BF16)
    return pl.pallas_call(
        functools.partial(_s2_body, conv=conv),
        grid=grid,
        in_specs=in_specs,
        out_specs=out_specs,
        out_shape=out_shape,
        compiler_params=_cp(("arbitrary", "arbitrary")),
        name="hyena_conv_dft2" if conv else "hyena_filter_dft2",
    )(*args)


def _rows2d(ref):
    lead = ref.shape[:-3]
    return ref.reshape(lead + (ref.shape[-3] * B_GROUP, ref.shape[-1]))


def _s1_body(z_ref, tab_ref, o_ref):
    rows, n1 = z_ref.shape[0], o_ref.shape[0]
    z2, o2 = _rows2d(z_ref), _rows2d(o_ref)
    for j in range(B_GROUP):
        zj = z2[pl.ds(j, rows, stride=B_GROUP), :]
        r = _dot(tab_ref[j], zj.astype(BF16))
        o2[pl.ds(j, n1, stride=B_GROUP), :] = _pack_c(r[:n1], r[n1:])


def _s1(z4, zi, tab, *, ct):
    _, rows, _, d = z4.shape
    n1 = tab.shape[1] // 2
    return pl.pallas_call(
        _s1_body,
        grid=(FFT_N2 // B_GROUP, d // ct),
        in_specs=[
            pl.BlockSpec((None, rows, B_GROUP, ct), lambda g, c: (zi, 0, g, c)),
            pl.BlockSpec((B_GROUP, 2 * n1, rows), lambda g, c: (g, 0, 0)),
        ],
        out_specs=pl.BlockSpec((n1, B_GROUP, ct), lambda g, c: (0, g, c)),
        out_shape=jax.ShapeDtypeStruct((n1, FFT_N2, d), U32),
        compiler_params=_cp(("arbitrary", "arbitrary")),
        name="hyena_conv_dft1",
    )(z4, tab)


def _s3_body(g_ref, tab_ref, gate_ref, z_ref, fb_ref, *rest, chain):
    if chain:
        tab1_ref, o_ref, o1_ref = rest
    else:
        (o_ref,) = rest
    n1, rows = g_ref.shape[0], o_ref.shape[0]
    g2, o2 = _rows2d(g_ref), _rows2d(o_ref)
    for j in range(B_GROUP):
        gj = _unpack_c(g2[pl.ds(j, n1, stride=B_GROUP), :])
        o2[pl.ds(j, rows, stride=B_GROUP), :] = _dot(tab_ref[j], gj)
    o_ref[...] = gate_ref[...] * (o_ref[...] + z_ref[...] * fb_ref[...])
    if chain:
        q2 = _rows2d(o1_ref)
        for j in range(B_GROUP):
            zj = o2[pl.ds(j, rows, stride=B_GROUP), :]
            r = _dot(tab1_ref[j], zj.astype(BF16))
            q2[pl.ds(j, n1, stride=B_GROUP), :] = _pack_c(r[:n1], r[n1:])


def _s3(g3, tab, gate4, gi, z4, zi, fb, tab1, *, ct):
    n1, _, d = g3.shape
    rows = tab.shape[1]
    chain = tab1 is not None
    blk = pl.BlockSpec((rows, B_GROUP, ct), lambda g, c: (0, g, c))
    cblk = pl.BlockSpec((n1, B_GROUP, ct), lambda g, c: (0, g, c))
    in_specs = [
        cblk,
        pl.BlockSpec((B_GROUP, rows, 2 * n1), lambda g, c: (g, 0, 0)),
        pl.BlockSpec((None, rows, B_GROUP, ct), lambda g, c: (gi, 0, g, c)),
        pl.BlockSpec((None, rows, B_GROUP, ct), lambda g, c: (zi, 0, g, c)),
        pl.BlockSpec((1, 1, ct), lambda g, c: (0, 0, c)),
    ]
    args = [g3, tab, gate4, z4, fb]
    out_specs = blk
    out_shape = jax.ShapeDtypeStruct((rows, FFT_N2, d), F32)
    if chain:
        in_specs.append(pl.BlockSpec((B_GROUP, 2 * n1, rows), lambda g, c: (g, 0, 0)))
        args.append(tab1)
        out_specs = [blk, cblk]
        out_shape = [out_shape, jax.ShapeDtypeStruct((n1, FFT_N2, d), U32)]
    return pl.pallas_call(
        functools.partial(_s3_body, chain=chain),
        grid=(FFT_N2 // B_GROUP, d // ct),
        in_specs=in_specs,
        out_specs=out_specs,
        out_shape=out_shape,
        compiler_params=_cp(("arbitrary", "arbitrary")),
        name="hyena_conv_idft1_dft1" if chain else "hyena_conv_idft1",
    )(*args)


def _dft_tables(n_lat):
    n = 2 * n_lat
    n1 = n // FFT_N2
    k1 = jnp.arange(n1, dtype=jnp.int32)
    th_a = ((k1[:, None] * k1[None, :]) % n1).astype(F32) * (2.0 * math.pi / n1)
    th_b = (jnp.arange(FFT_N2, dtype=jnp.int32)[:, None] * k1[None, :]).astype(F32) * (2.0 * math.pi / n)
    ca, sa = jnp.cos(th_a)[None], jnp.sin(th_a)[None]
    cb, sb = jnp.cos(th_b)[:, :, None], jnp.sin(th_b)[:, :, None]
    cr = ca * cb - sa * sb
    sn = sa * cb + ca * sb
    ha = n1 // 2
    crh, snh = cr[:, :, :ha], sn[:, :, :ha]
    w1 = jnp.concatenate([jnp.concatenate([crh, snh], axis=2), jnp.concatenate([-snh, crh], axis=2)], axis=1)
    w1f = jnp.concatenate([cr, -sn], axis=1)
    v = jnp.swapaxes(w1, 1, 2) * (1.0 / n)
    k2 = jnp.arange(FFT_N2, dtype=jnp.int32)
    th2 = ((k2[:, None] * k2[None, :]) % FFT_N2).astype(F32) * (2.0 * math.pi / FFT_N2)
    c2, s2 = jnp.cos(th2), jnp.sin(th2)
    ff = jnp.concatenate([jnp.concatenate([c2, s2], axis=1), jnp.concatenate([-s2, c2], axis=1)], axis=0)
    fi = jnp.concatenate([jnp.concatenate([c2, -s2], axis=1), jnp.concatenate([s2, c2], axis=1)], axis=0)
    return w1.astype(BF16), w1f.astype(BF16), v.astype(BF16), ff.astype(BF16), fi.astype(BF16)


def _hyena_mix(proj3, fparams, fbias, *, n_lat):
    _, bsz, _, d = proj3.shape
    f_w1, f_b1, f_f1, f_w2, f_b2, f_f2, f_w3, decay = fparams
    na = 2 * n_lat // FFT_N2
    w1, w1f, v, ff, fi = _dft_tables(n_lat)
    ct = LANES
    hd = _filt_feat(f_w1, f_b1, f_f1, f_w2, f_b2, f_f2, n_lat=n_lat)
    w3r = jnp.transpose(f_w3.reshape(HY_HID, 2, 2, d), (1, 2, 0, 3))
    w3r = jnp.zeros((2, 2, LANES, d), F32).at[:, :, :HY_HID].set(w3r)
    kf1 = _filt_s1(hd, w3r, decay.reshape(2, 2, 1, d), w1f, ct=ct)
    kf = _s2(kf1, None, 0, ff, None, ct=d, conv=False)
    p3 = proj3.reshape(3, bsz * (n_lat // FFT_N2), FFT_N2, d)
    o1 = _s1(p3, 2, w1, ct=ct)
    g = _s2(o1, kf, 0, ff, fi, ct=d, conv=True)
    z, o1 = _s3(g, v, p3, 0, p3, 2, fbias[0].reshape(1, 1, d), w1, ct=ct)
    g = _s2(o1, kf, 1, ff, fi, ct=d, conv=True)
    z = _s3(g, v, p3, 1, z[None], 0, fbias[1].reshape(1, 1, d), None, ct=ct)
    return z.reshape(bsz, n_lat, d)


def _rope_tables(n_tokens):
    rows = n_tokens // GRID_W
    row = jnp.broadcast_to(jnp.arange(rows, dtype=F32)[:, None], (rows, GRID_W)).reshape(-1)
    col = jnp.broadcast_to(jnp.arange(GRID_W, dtype=F32)[None, :], (rows, GRID_W)).reshape(-1)
    axis_dim = QK_ROPE // 2
    inv_freq = 1.0 / (ROPE_THETA ** (jnp.arange(0, axis_dim, 2, dtype=F32) / axis_dim))
    ang = jnp.concatenate([row[:, None] * inv_freq, col[:, None] * inv_freq], axis=-1)
    return jnp.cos(ang), jnp.sin(ang)


def _mla_weights(w_down, g_q, w_uq, g_kv, w_ukv):
    d = w_down.shape[0]
    nh = MLA_HEADS
    kpe = w_down[:, Q_LORA + KV_LORA:]
    w1, w2 = kpe[:, 0::2], kpe[:, 1::2]
    z = jnp.zeros((d, LANES - QK_ROPE), w_down.dtype)
    wd = jnp.concatenate([w_down[:, :Q_LORA + KV_LORA], w1, w2, z, w2, w1, z], axis=1).astype(BF16)
    uq = w_uq.reshape(Q_LORA, nh, QK_NOPE + QK_ROPE)
    pe = uq[:, :, QK_NOPE:]
    uq = jnp.concatenate([uq[:, :, :QK_NOPE], pe[:, :, 0::2], pe[:, :, 1::2]], axis=2)
    wuqT = uq.reshape(Q_LORA, nh * (QK_NOPE + QK_ROPE)).T.astype(BF16)
    ukv = w_ukv.reshape(KV_LORA, nh, QK_NOPE + V_DIM)
    wuk = ukv[:, :, :QK_NOPE].reshape(KV_LORA, nh * QK_NOPE).astype(BF16)
    wuvT = ukv[:, :, QK_NOPE:].reshape(KV_LORA, nh * V_DIM).T.astype(BF16)
    return wd, g_q.reshape(1, -1), g_kv.reshape(1, -1), wuk, wuqT, wuvT


def kernel(x, c, ctx, c_ctx, ada_w, ada_b, norm_mix_g, norm_ffn_g, mla_w_down, mla_g_q, mla_w_uq, mla_g_kv, mla_w_ukv, mla_w_o, hy_w_in, hy_b_in, hy_conv_w, hy_conv_b, hy_f_w1, hy_f_b1, hy_f_freq1, hy_f_w2, hy_f_b2, hy_f_freq2, hy_f_w3, hy_decay, hy_bias, hy_w_out, hy_b_out, moe_w_r, moe_b_r, moe_w_in, moe_b_in, moe_w_out, moe_b_out, final_g):
    bsz, n_lat, d = x.shape
    n_ctx = ctx.shape[1]
    depth = ada_w.shape[0]
    assert bsz == 2 and d == MLA_HEADS * V_DIM and n_lat % 512 == 0 and n_ctx % 128 == 0
    assert depth == 2

    cond8 = jnp.zeros((8, d), F32).at[:bsz].set(c).at[bsz].set(c_ctx)
    mods = _ada(cond8, ada_w, ada_b)

    def mod(i, j, rows):
        return mods[i, rows, j * d:(j + 1) * d][:, None, :]

    lat_rows = slice(0, bsz)
    ctx_rows = slice(bsz, bsz + 1)
    xl = x
    for i in range(depth):
        kind, j = i % 2, i // 2
        sh1, sc1, g1 = (mod(i, m, lat_rows) for m in range(3))
        sh2, sc2, g2 = (mod(i, m, lat_rows) for m in range(3, 6))
        gm = norm_mix_g[i].reshape(1, d)
        if kind == 0:
            wts = _mla_weights(mla_w_down[j], mla_g_q[j], mla_w_uq[j], mla_g_kv[j], mla_w_ukv[j])
            cos, sin = _rope_tables(n_lat)
            zl = jnp.zeros((n_lat, LANES - QK_ROPE), F32)
            tabs = (jnp.concatenate([cos, cos, zl], axis=1), jnp.concatenate([-sin, sin, zl], axis=1), cos.T, sin.T)
            tq = tv = 512
            tk = 2048 if n_lat % 4096 == 0 else 512
            qT, k, vT = _mla_proj(xl, gm, sh1, sc1, wts, tabs, need_q=True, tm=tv, tk=tv)
            half = QK_ROPE // 2
            one_c = jnp.concatenate([jnp.ones((n_ctx, QK_ROPE), F32), jnp.zeros((n_ctx, LANES - QK_ROPE), F32)], axis=1)
            tabs_c = (one_c, jnp.zeros((n_ctx, LANES), F32), jnp.ones((half, n_ctx), F32), jnp.zeros((half, n_ctx), F32))
            kc, vTc = _mla_proj(ctx, gm, mod(i, 0, ctx_rows), mod(i, 1, ctx_rows), wts, tabs_c,
                                need_q=False, tm=n_ctx, tk=n_ctx)
            o = _attention(qT, k, vT, kc, vTc, tq=tq, tk=tk)
            wo = mla_w_o[j].astype(BF16)
            bo = jnp.zeros((1, d), F32)
            transposed = True
        else:
            proj3 = _hy_in(xl, gm, sh1, sc1, hy_w_in[j].astype(BF16), hy_b_in[j].reshape(1, -1), hy_conv_w[j],
                           hy_conv_b[j].reshape(1, -1), tm=512)
            fparams = (hy_f_w1[j], hy_f_b1[j], hy_f_freq1[j], hy_f_w2[j], hy_f_b2[j], hy_f_freq2[j], hy_f_w3[j],
                       hy_decay[j])
            o = _hyena_mix(proj3, fparams, hy_bias[j], n_lat=n_lat)
            wo = hy_w_out[j].astype(BF16)
            bo = hy_b_out[j].reshape(1, d)
            transposed = False
        wr = jnp.zeros((d, LANES), F32).at[:, :N_EXPERTS].set(moe_w_r[i])
        wrh = wr.astype(BF16)
        wrl = (wr - wrh.astype(F32)).astype(BF16)
        br = jnp.zeros((1, LANES), F32).at[0, :N_EXPERTS].set(moe_b_r[i])
        xl, fl, topi, gates, rank, cnt = _post(o, wo, bo, xl, g1, norm_ffn_g[i].reshape(1, d), sh2, sc2, wrh, wrl, br,
                                               transposed=transposed, tm=512)
        xl = _moe(fl, topi, gates, rank, cnt, xl, g2, final_g.reshape(1, d), i, moe_w_in, moe_b_in,
                  moe_w_out, moe_b_out, final=(i == depth - 1))
    return xl
```

```python
import functools
import math

import jax
import jax.numpy as jnp
from jax import lax
from jax.experimental import pallas as pl
from jax.experimental.pallas import tpu as pltpu

F32 = jnp.float32
BF16 = jnp.bfloat16

EPS = 1e-6
GRID_W = 64
MLA_HEADS = 8
QK_NOPE = 128
QK_ROPE = 64
V_DIM = 128
Q_LORA = 512
KV_LORA = 256
ROPE_THETA = 10000.0
MLA_SCALE = (QK_NOPE + QK_ROPE) ** -0.5
QK_PAD = 256

HY_EMB = 33
HY_BANDS = (HY_EMB - 1) // 2
HY_HID = 64
FFT_N2 = 128
B_GROUP = 8

N_EXPERTS = 32
TOP_K = 4
SWIGLU_LIMIT = 7.0
SWIGLU_ALPHA = 1.702
MOE_TM = 256
LANES = 128

VMEM_LIMIT = 56 * 1024 * 1024


def _cp(sem, vmem=VMEM_LIMIT):
    return pltpu.CompilerParams(dimension_semantics=sem, vmem_limit_bytes=vmem)


def _dot(a, b):
    return jnp.dot(a, b, preferred_element_type=F32)


def _dot_hi(a, b):
    return jnp.dot(a, b, preferred_element_type=F32, precision=lax.Precision.HIGHEST)


def _rms(x, g):
    return x * lax.rsqrt(jnp.mean(x * x, axis=-1, keepdims=True) + EPS) * g


def _ada_body(c_ref, w_ref, b_ref, o_ref):
    c = c_ref[...]
    s = c * jax.nn.sigmoid(c)
    o_ref[0] = _dot(s.astype(BF16), w_ref[0].astype(BF16)) + b_ref[0]


def _ada(cond8, ada_w, ada_b):
    depth, d, n = ada_w.shape
    tn = n // 4
    return pl.pallas_call(
        _ada_body,
        grid=(depth, n // tn),
        in_specs=[
            pl.BlockSpec((8, d), lambda i, j: (0, 0)),
            pl.BlockSpec((1, d, tn), lambda i, j: (i, 0, j)),
            pl.BlockSpec((1, 1, tn), lambda i, j: (i, 0, j)),
        ],
        out_specs=pl.BlockSpec((1, 8, tn), lambda i, j: (i, 0, j)),
        out_shape=jax.ShapeDtypeStruct((depth, 8, n), F32),
        compiler_params=_cp(("arbitrary", "arbitrary")),
        name="ada_mod",
    )(cond8, ada_w, ada_b.reshape(depth, 1, n))


def _mla_proj_body(x_ref, g_ref, sh_ref, sc_ref, wd_ref, gq_ref, gkv_ref, wuk_ref, wuqT_ref, wuvT_ref,
                   ct_ref, st_ref, cT_ref, sT_ref, *out_refs, need_q, tk):
    if need_q:
        qT_ref, k_ref, vT_ref = out_refs
    else:
        k_ref, vT_ref = out_refs
    nh = MLA_HEADS
    x = x_ref[0]
    h = _rms(x, g_ref[...]) * (1.0 + sc_ref[0]) + sh_ref[0]
    lat = _dot(h.astype(BF16), wd_ref[...])
    o_kv = Q_LORA
    o_a = Q_LORA + KV_LORA
    kvn = _rms(lat[:, o_kv:o_a], gkv_ref[...])
    kr = (lat[:, o_a:o_a + LANES] * ct_ref[...] + lat[:, o_a + LANES:o_a + 2 * LANES] * st_ref[...]).astype(BF16)
    knope = _dot(kvn.astype(BF16), wuk_ref[...])
    for hh in range(nh):
        k_ref[0, hh, :, 0:QK_NOPE] = knope[:, hh * QK_NOPE:(hh + 1) * QK_NOPE].astype(BF16)
        k_ref[0, hh, :, QK_NOPE:QK_PAD] = kr
    vT = _dot(wuvT_ref[...], kvn.T.astype(BF16))
    tm = x.shape[0]
    for hh in range(nh):
        for c in range(tm // tk):
            vT_ref[0, hh, c] = vT[hh * V_DIM:(hh + 1) * V_DIM, c * tk:(c + 1) * tk].astype(BF16)
    if need_q:
        qn = _rms(lat[:, :Q_LORA], gq_ref[...])
        qT = _dot(wuqT_ref[...], qn.T.astype(BF16)) * (MLA_SCALE * math.log2(math.e))
        c = cT_ref[...]
        s = sT_ref[...]
        hw = QK_NOPE + QK_ROPE
        half = QK_ROPE // 2
        for hh in range(nh):
            base = hh * hw
            x1 = qT[base + QK_NOPE:base + QK_NOPE + half]
            x2 = qT[base + QK_NOPE + half:base + hw]
            qT_ref[0, hh, 0:QK_NOPE] = qT[base:base + QK_NOPE].astype(BF16)
            qT_ref[0, hh, QK_NOPE:QK_NOPE + half] = (x1 * c - x2 * s).astype(BF16)
            qT_ref[0, hh, QK_NOPE + half:hw] = (x1 * s + x2 * c).astype(BF16)
            qT_ref[0, hh, hw:QK_PAD] = jnp.zeros((QK_PAD - hw, tm), BF16)


def _mla_proj(x, g, sh, sc, wts, tabs, *, need_q, tm, tk):
    bsz, n, d = x.shape
    nh = MLA_HEADS
    wd, gq, gkv, wuk, wuqT, wuvT = wts
    ct, st, cT, sT = tabs
    nsh = sh.shape[0]
    full = lambda a: pl.BlockSpec(a.shape, lambda b, i: (0,) * a.ndim)
    in_specs = [
        pl.BlockSpec((1, tm, d), lambda b, i: (b, i, 0)),
        full(g),
        pl.BlockSpec((1, 1, d), lambda b, i: (b % nsh, 0, 0)),
        pl.BlockSpec((1, 1, d), lambda b, i: (b % nsh, 0, 0)),
        full(wd), full(gq), full(gkv), full(wuk), full(wuqT), full(wuvT),
        pl.BlockSpec((tm, LANES), lambda b, i: (i, 0)),
        pl.BlockSpec((tm, LANES), lambda b, i: (i, 0)),
        pl.BlockSpec((QK_ROPE // 2, tm), lambda b, i: (0, i)),
        pl.BlockSpec((QK_ROPE // 2, tm), lambda b, i: (0, i)),
    ]
    out_specs = [
        pl.BlockSpec((1, nh, tm, QK_PAD), lambda b, i: (b, 0, i, 0)),
        pl.BlockSpec((1, nh, tm // tk, V_DIM, tk), lambda b, i: (b, 0, i, 0, 0)),
    ]
    out_shape = [
        jax.ShapeDtypeStruct((bsz, nh, n, QK_PAD), BF16),
        jax.ShapeDtypeStruct((bsz, nh, n // tk, V_DIM, tk), BF16),
    ]
    if need_q:
        out_specs = [pl.BlockSpec((1, nh, QK_PAD, tm), lambda b, i: (b, 0, 0, i))] + out_specs
        out_shape = [jax.ShapeDtypeStruct((bsz, nh, QK_PAD, n), BF16)] + out_shape
    return pl.pallas_call(
        functools.partial(_mla_proj_body, need_q=need_q, tk=tk),
        grid=(bsz, n // tm),
        in_specs=in_specs,
        out_specs=out_specs,
        out_shape=out_shape,
        compiler_params=_cp(("arbitrary", "arbitrary")),
        name="mla_proj_q" if need_q else "mla_proj_ctx",
    )(x, g, sh, sc, wd, gq, gkv, wuk, wuqT, wuvT, ct, st, cT, sT)


SM_STRIP = 64
SUBLANES = 8


def _attn_body(qT_ref, k_ref, vT_ref, kc_ref, vTc_ref, o_ref, s0, s1, p0, p1, sc, pc, acc, m_scr, x0, x1, xc,
               a0, a1, ac, l_scr, d0, d1, dc, *, tk):
    nchunk = k_ref.shape[2] // tk

    def scores(kblk, s_ref, mx_ref):
        r = _dot(kblk, qT_ref[0, 0])
        s_ref[...] = r
        mx_ref[...] = jnp.max(r, axis=0, keepdims=True)

    def probs(s_ref, mx_ref, p_ref, a_ref, d_ref):
        m_old = m_scr[...]
        m_new = jnp.maximum(m_old, mx_ref[...])
        m_scr[...] = m_new
        alpha = jnp.exp2(m_old - m_new)
        a_ref[...] = alpha
        part = None
        for r in range(0, s_ref.shape[0], SM_STRIP):
            p = jnp.exp2(s_ref[r:r + SM_STRIP] - m_new)
            p_ref[r:r + SM_STRIP] = p.astype(BF16)
            ps = jnp.sum(p.reshape(SM_STRIP // SUBLANES, SUBLANES, p.shape[1]), axis=0)
            part = ps if part is None else part + ps
        d_ref[...] = part

    def accumulate(p_ref, a_ref, d_ref, vblk):
        acc[...] = a_ref[...] * acc[...] + _dot(vblk, p_ref[...])
        l_scr[...] = a_ref[...] * l_scr[...] + d_ref[...]

    def kchunk(i):
        return k_ref[0, 0, pl.ds(pl.multiple_of(i * tk, tk), tk), :]

    def vchunk(i):
        nsub = tk // vT_ref.shape[-1]
        return jnp.concatenate([vT_ref[0, 0, i * nsub + u] for u in range(nsub)], axis=1)

    m_scr[...] = jnp.full(m_scr.shape, -jnp.inf, F32)
    acc[...] = jnp.zeros(acc.shape, F32)
    l_scr[...] = jnp.zeros(l_scr.shape, F32)
    scores(kc_ref[0, 0], sc, xc)
    scores(kchunk(0), s0, x0)
    probs(sc, xc, pc, ac, dc)
    scores(kchunk(1), s1, x1)
    accumulate(pc, ac, dc, vTc_ref[0, 0, 0])
    probs(s0, x0, p0, a0, d0)

    def body(j, carry):
        t = 2 * j
        scores(kchunk(t + 2), s0, x0)
        accumulate(p0, a0, d0, vchunk(t))
        probs(s1, x1, p1, a1, d1)
        scores(kchunk(t + 3), s1, x1)
        accumulate(p1, a1, d1, vchunk(t + 1))
        probs(s0, x0, p0, a0, d0)
        return carry

    lax.fori_loop(0, nchunk // 2 - 1, body, 0)
    accumulate(p0, a0, d0, vchunk(nchunk - 2))
    probs(s1, x1, p1, a1, d1)
    accumulate(p1, a1, d1, vchunk(nchunk - 1))
    o_ref[0, 0] = (acc[...] / jnp.sum(l_scr[...], axis=0, keepdims=True)).astype(BF16)


def _attention(qT, k, vT, kc, vTc, *, tq, tk):
    bsz, nh, _, n = qT.shape
    nc = kc.shape[2]
    tv = vT.shape[-1]
    assert (n // tk) % 2 == 0 and tk % tv == 0
    return pl.pallas_call(
        functools.partial(_attn_body, tk=tk),
        grid=(bsz, nh, n // tq),
        in_specs=[
            pl.BlockSpec((1, 1, QK_PAD, tq), lambda b, h, i: (b, h, 0, i)),
            pl.BlockSpec((1, 1, n, QK_PAD), lambda b, h, i: (b, h, 0, 0)),
            pl.BlockSpec((1, 1, n // tv, V_DIM, tv), lambda b, h, i: (b, h, 0, 0, 0)),
            pl.BlockSpec((1, 1, nc, QK_PAD), lambda b, h, i: (b, h, 0, 0)),
            pl.BlockSpec((1, 1, 1, V_DIM, nc), lambda b, h, i: (b, h, 0, 0, 0)),
        ],
        out_specs=pl.BlockSpec((1, 1, V_DIM, tq), lambda b, h, i: (b, h, 0, i)),
        out_shape=jax.ShapeDtypeStruct((bsz, nh, V_DIM, n), BF16),
        scratch_shapes=[pltpu.VMEM((tk, tq), F32), pltpu.VMEM((tk, tq), F32),
                        pltpu.VMEM((tk, tq), BF16), pltpu.VMEM((tk, tq), BF16),
                        pltpu.VMEM((nc, tq), F32), pltpu.VMEM((nc, tq), BF16),
                        pltpu.VMEM((V_DIM, tq), F32), pltpu.VMEM((1, tq), F32),
                        pltpu.VMEM((1, tq), F32), pltpu.VMEM((1, tq), F32), pltpu.VMEM((1, tq), F32),
                        pltpu.VMEM((1, tq), F32), pltpu.VMEM((1, tq), F32), pltpu.VMEM((1, tq), F32),
                        pltpu.VMEM((SUBLANES, tq), F32), pltpu.VMEM((SUBLANES, tq), F32),
                        pltpu.VMEM((SUBLANES, tq), F32), pltpu.VMEM((SUBLANES, tq), F32)],
        compiler_params=_cp(("arbitrary", "arbitrary", "arbitrary")),
        name="mla_attention",
    )(qT, k, vT, kc, vTc)


def _post_body(o_ref, wo_ref, bo_ref, x_ref, g1_ref, gf_ref, sh_ref, sc_ref, wrh_ref, wrl_ref, br_ref, tri_ref,
               xl_ref, fl_ref, ti_ref, gt_ref, rk_ref, cnt_ref, *, transposed):
    @pl.when((pl.program_id(0) == 0) & (pl.program_id(1) == 0))
    def _():
        cnt_ref[...] = jnp.zeros_like(cnt_ref)

    tm = x_ref.shape[1]
    if transposed:
        oT = o_ref[0].astype(F32).reshape(MLA_HEADS * V_DIM, tm)
        o = oT.T.astype(BF16)
    else:
        o = o_ref[0].astype(BF16)
    y = _dot(o, wo_ref[...]) + bo_ref[...]
    xl = x_ref[0] + g1_ref[0] * y
    xl_ref[0] = xl
    fl = _rms(xl, gf_ref[...]) * (1.0 + sc_ref[0]) + sh_ref[0]
    _to_rows(fl_ref, fl)
    flh = fl.astype(BF16)
    fll = (fl - flh.astype(F32)).astype(BF16)
    logits = _dot(flh, wrh_ref[...]) + (_dot(fll, wrh_ref[...]) + _dot(flh, wrl_ref[...])) + br_ref[...]
    lane = lax.broadcasted_iota(jnp.int32, (tm, LANES), 1).astype(F32)
    neg = jnp.float32(-jnp.inf)
    work = jnp.where(lane < N_EXPERTS, logits, neg)
    vals, idxs = [], []
    onehot = jnp.zeros((tm, LANES), F32)
    for _ in range(TOP_K):
        mk = jnp.max(work, axis=-1, keepdims=True)
        ik = jnp.min(jnp.where(work == mk, lane, float(LANES)), axis=-1, keepdims=True)
        sel = lane == ik
        onehot = jnp.where(sel, 1.0, onehot)
        work = jnp.where(sel, neg, work)
        vals.append(mk)
        idxs.append(ik)
    es = [jnp.exp(v - vals[0]) for v in vals]
    den = es[0] + es[1] + es[2] + es[3]
    pre = _dot(tri_ref[...], onehot.astype(BF16)) + cnt_ref[...]
    ti = jnp.zeros((tm, LANES), F32)
    gt = jnp.zeros((tm, LANES), F32)
    rk = jnp.zeros((tm, LANES), F32)
    for kk in range(TOP_K):
        rank = jnp.sum(jnp.where(lane == idxs[kk], pre, 0.0), axis=-1, keepdims=True)
        ti = jnp.where(lane == kk, idxs[kk], ti)
        gt = jnp.where(lane == kk, es[kk] / den, gt)
        rk = jnp.where(lane == kk, rank, rk)
    ti_ref[...] = ti[:, :TOP_K].astype(jnp.int32)
    gt_ref[...] = gt[:, :TOP_K]
    rk_ref[...] = rk[:, :TOP_K].astype(jnp.int32)
    cnt_ref[...] += jnp.sum(onehot, axis=0, keepdims=True)


def _post(o, wo, bo, x, g1, gf, sh, sc, wrh, wrl, br, *, transposed, tm):
    bsz, n, d = x.shape
    t = bsz * n
    nt = n // tm
    tri = (lax.broadcasted_iota(jnp.int32, (tm, tm), 0) > lax.broadcasted_iota(jnp.int32, (tm, tm), 1)).astype(BF16)
    full = lambda a: pl.BlockSpec(a.shape, lambda b, i: (0,) * a.ndim)
    per_b = pl.BlockSpec((1, 1, d), lambda b, i: (b, 0, 0))
    if transposed:
        o_spec = pl.BlockSpec((1, MLA_HEADS, V_DIM, tm), lambda b, i: (b, 0, 0, i))
    else:
        o_spec = pl.BlockSpec((1, tm, d), lambda b, i: (b, i, 0))
    tok = lambda w: pl.BlockSpec((tm, w), lambda b, i: (b * nt + i, 0))
    return pl.pallas_call(
        functools.partial(_post_body, transposed=transposed),
        grid=(bsz, nt),
        in_specs=[o_spec, full(wo), full(bo), pl.BlockSpec((1, tm, d), lambda b, i: (b, i, 0)), per_b, full(gf),
                  per_b, per_b, full(wrh), full(wrl), full(br), full(tri)],
        out_specs=[pl.BlockSpec((1, tm, d), lambda b, i: (b, i, 0)),
                   pl.BlockSpec((tm * ROW_SUB, LANES), lambda b, i: (b * nt + i, 0)),
                   tok(TOP_K), tok(TOP_K), tok(TOP_K), pl.BlockSpec((1, LANES), lambda b, i: (0, 0))],
        out_shape=[jax.ShapeDtypeStruct((bsz, n, d), F32), jax.ShapeDtypeStruct((t * ROW_SUB, LANES), F32),
                   jax.ShapeDtypeStruct((t, TOP_K), jnp.int32), jax.ShapeDtypeStruct((t, TOP_K), F32),
                   jax.ShapeDtypeStruct((t, TOP_K), jnp.int32), jax.ShapeDtypeStruct((1, LANES), F32)],
        compiler_params=_cp(("arbitrary", "arbitrary")),
        name="post_attn" if transposed else "post_hyena",
    )(o, wo, bo, x, g1, gf, sh, sc, wrh, wrl, br, tri)


ROW_SUB = 8


def _row_slice(i):
    return pl.ds(pl.multiple_of(i * ROW_SUB, ROW_SUB), ROW_SUB)


def _to_rows(ref, x):
    for s in range(ROW_SUB):
        ref[pl.ds(s, x.shape[0], stride=ROW_SUB), :] = x[:, s * LANES:(s + 1) * LANES]


def _from_rows(ref, lo, hi):
    return jnp.concatenate([ref[pl.ds(lo * ROW_SUB + s, hi - lo, stride=ROW_SUB), :] for s in range(ROW_SUB)], axis=1)


def _dispatch_body(pe_ref, pd_ref, dest_ref, fl_ref, xs_out, zbuf, sem, *, td):
    @pl.when(pl.program_id(0) == 0)
    def _():
        zbuf[...] = jnp.zeros(zbuf.shape, zbuf.dtype)
        for e in range(N_EXPERTS):
            @pl.when(pd_ref[e] > 0)
            def _():
                start = pl.multiple_of((pe_ref[e] - MOE_TM) * ROW_SUB, ROW_SUB)
                cp = pltpu.make_async_copy(zbuf, xs_out.at[pl.ds(start, MOE_TM * ROW_SUB)], sem)
                cp.start()
                cp.wait()

    def issue(t, carry):
        for kk in range(TOP_K):
            d = dest_ref[0, 0, t * TOP_K + kk]
            pltpu.make_async_copy(fl_ref.at[_row_slice(t)], xs_out.at[_row_slice(d)], sem).start(priority=kk % 2)
        return carry

    lax.fori_loop(0, td, issue, 0, unroll=2)

    def drain(t, carry):
        pltpu.make_async_copy(fl_ref.at[_row_slice(0)], xs_out.at[_row_slice(0)], sem).wait()
        return carry

    lax.fori_loop(0, td * TOP_K, drain, 0, unroll=8)


def _dispatch(pad_end, padded, dest, fl, n_rows, *, td):
    t = fl.shape[0] // ROW_SUB
    dest3 = dest.reshape(t // td, 1, td * TOP_K)
    grid_spec = pltpu.PrefetchScalarGridSpec(
        num_scalar_prefetch=2,
        grid=(t // td,),
        in_specs=[
            pl.BlockSpec((1, 1, td * TOP_K), lambda i, pe, pd: (i, 0, 0), memory_space=pltpu.SMEM),
            pl.BlockSpec((td * ROW_SUB, LANES), lambda i, pe, pd: (i, 0)),
        ],
        out_specs=pl.BlockSpec(memory_space=pl.ANY),
        scratch_shapes=[pltpu.VMEM((MOE_TM * ROW_SUB, LANES), fl.dtype), pltpu.SemaphoreType.DMA(())],
    )
    return pl.pallas_call(
        functools.partial(_dispatch_body, td=td),
        grid_spec=grid_spec,
        out_shape=jax.ShapeDtypeStruct((n_rows * ROW_SUB, LANES), fl.dtype),
        compiler_params=_cp(("arbitrary",)),
        name="moe_dispatch",
    )(pad_end, padded, dest3, fl)


def _expert_body(be_ref, nu_ref, xs_ref, win_ref, bin_ref, wout_ref, bout_ref, ys_ref, win_s, wout_s):
    b = pl.program_id(0)
    dff = wout_ref.shape[1]

    @pl.when(b < nu_ref[0])
    def _():
        prev = be_ref[jnp.maximum(b - 1, 0)]

        @pl.when((b == 0) | (prev != be_ref[b]))
        def _():
            win_s[...] = win_ref[0].astype(BF16)
            wout_s[...] = wout_ref[0].astype(BF16)

        x = _from_rows(xs_ref, 0, xs_ref.shape[0] // ROW_SUB).astype(BF16)
        gu = _dot(x, win_s[...]) + bin_ref[0]
        gate = jnp.minimum(gu[:, :dff], SWIGLU_LIMIT)
        lin = jnp.clip(gu[:, dff:], -SWIGLU_LIMIT, SWIGLU_LIMIT)
        act = gate * jax.nn.sigmoid(SWIGLU_ALPHA * gate) * (lin + 1.0)
        _to_rows(ys_ref, _dot(act.astype(BF16), wout_s[...]) + bout_ref[0])

    @pl.when(b >= nu_ref[0])
    def _():
        ys_ref[...] = jnp.zeros_like(ys_ref)


def _experts(blk_exp, n_used, xs, layer, w_in, b_in, w_out, b_out):
    n_rows = xs.shape[0] // ROW_SUB
    depth, ne, d, f2 = w_in.shape
    dff = w_out.shape[2]
    tm = MOE_TM
    grid_spec = pltpu.PrefetchScalarGridSpec(
        num_scalar_prefetch=2,
        grid=(n_rows // tm,),
        in_specs=[
            pl.BlockSpec((tm * ROW_SUB, LANES), lambda b, be, nu: (jnp.minimum(b, nu[0] - 1), 0)),
            pl.BlockSpec((None, 1, d, f2), lambda b, be, nu: (layer, be[b], 0, 0)),
            pl.BlockSpec((None, 1, 1, f2), lambda b, be, nu: (layer, be[b], 0, 0)),
            pl.BlockSpec((None, 1, dff, d), lambda b, be, nu: (layer, be[b], 0, 0)),
            pl.BlockSpec((None, 1, 1, d), lambda b, be, nu: (layer, be[b], 0, 0)),
        ],
        out_specs=pl.BlockSpec((tm * ROW_SUB, LANES), lambda b, be, nu: (b, 0)),
        scratch_shapes=[pltpu.VMEM((d, f2), BF16), pltpu.VMEM((dff, d), BF16)],
    )
    return pl.pallas_call(
        _expert_body,
        grid_spec=grid_spec,
        out_shape=jax.ShapeDtypeStruct(xs.shape, F32),
        compiler_params=_cp(("arbitrary",)),
        name="moe_experts",
    )(blk_exp, n_used, xs, w_in, b_in.reshape(depth, ne, 1, f2), w_out, b_out.reshape(depth, ne, 1, d))


def _combine_body(dest_ref, ys_hbm, gt_ref, xl_ref, g2_ref, fg_ref, out_ref, buf, sem, *, tc, final):
    def issue(t, carry):
        for kk in range(TOP_K):
            d = dest_ref[0, 0, t * TOP_K + kk]
            pltpu.make_async_copy(ys_hbm.at[_row_slice(d)], buf.at[_row_slice(kk * tc + t)], sem).start(
                priority=kk % 2)
        return carry

    lax.fori_loop(0, tc, issue, 0, unroll=2)

    def drain(t, carry):
        pltpu.make_async_copy(ys_hbm.at[_row_slice(0)], buf.at[_row_slice(0)], sem).wait()
        return carry

    lax.fori_loop(0, tc * TOP_K, drain, 0, unroll=8)
    gt = gt_ref[...]
    y = gt[:, 0:1] * _from_rows(buf, 0, tc)
    for kk in range(1, TOP_K):
        y = y + gt[:, kk:kk + 1] * _from_rows(buf, kk * tc, (kk + 1) * tc)
    xl = xl_ref[0] + g2_ref[0] * y
    out_ref[0] = _rms(xl, fg_ref[...]) if final else xl


def _combine(dest, ys, gates, xl, g2, fg, *, tc, final):
    bsz, n, d = xl.shape
    t = bsz * n
    nt = n // tc
    dest3 = dest.reshape(t // tc, 1, tc * TOP_K)
    return pl.pallas_call(
        functools.partial(_combine_body, tc=tc, final=final),
        grid=(bsz, nt),
        in_specs=[
            pl.BlockSpec((1, 1, tc * TOP_K), lambda b, i: (b * nt + i, 0, 0), memory_space=pltpu.SMEM),
            pl.BlockSpec(memory_space=pl.ANY),
            pl.BlockSpec((tc, TOP_K), lambda b, i: (b * nt + i, 0)),
            pl.BlockSpec((1, tc, d), lambda b, i: (b, i, 0)),
            pl.BlockSpec((1, 1, d), lambda b, i: (b, 0, 0)),
            pl.BlockSpec((1, d), lambda b, i: (0, 0)),
        ],
        out_specs=pl.BlockSpec((1, tc, d), lambda b, i: (b, i, 0)),
        out_shape=jax.ShapeDtypeStruct((bsz, n, d), F32),
        scratch_shapes=[pltpu.VMEM((TOP_K * tc * ROW_SUB, LANES), F32), pltpu.SemaphoreType.DMA(())],
        compiler_params=_cp(("arbitrary", "arbitrary")),
        name="moe_combine",
    )(dest3, ys, gates, xl, g2, fg)


def _moe(fl, topi, gates, rank, cnt, xl, g2, fg, layer, w_in, b_in, w_out, b_out, *, final):
    t = fl.shape[0] // ROW_SUB
    tm = MOE_TM
    counts = cnt[0, :N_EXPERTS].astype(jnp.int32)
    padded = (counts + tm - 1) // tm * tm
    pad_end = jnp.cumsum(padded)
    pad_start = pad_end - padded
    dest = jnp.take(pad_start, topi) + rank
    nb = t * TOP_K // tm + N_EXPERTS
    blk_start = jnp.arange(nb, dtype=jnp.int32) * tm
    blk_exp = jnp.minimum(jnp.sum((pad_end[None, :] <= blk_start[:, None]).astype(jnp.int32), axis=1), N_EXPERTS - 1)
    n_used = (pad_end[-1:] // tm).astype(jnp.int32)
    xs = _dispatch(pad_end, padded, dest, fl, nb * tm, td=512)
    ys = _experts(blk_exp, n_used, xs, layer, w_in, b_in, w_out, b_out)
    return _combine(dest, ys, gates, xl, g2, fg, tc=512, final=final)


def _hy_in_body(x_ref, xp_ref, xn_ref, g_ref, sh_ref, sc_ref, w_ref, b_ref, cw_ref, cb_ref, o_ref, *, nt):
    i = pl.program_id(1)
    d = x_ref.shape[2]

    def normed(xx):
        return (_rms(xx, g_ref[...]) * (1.0 + sc_ref[0]) + sh_ref[0]).astype(BF16)

    h = normed(x_ref[0])
    hh = normed(jnp.concatenate([xp_ref[0], xn_ref[0]], axis=0))
    tm = h.shape[0]
    row = lax.broadcasted_iota(jnp.int32, (tm, 1), 0)
    for j in range(3):
        cols = slice(j * d, (j + 1) * d)
        p = _dot(h, w_ref[:, cols]) + b_ref[:, cols]
        ph = _dot(hh, w_ref[:, cols]) + b_ref[:, cols]
        prev = jnp.where(i > 0, ph[7:8], 0.0)
        nxt = jnp.where(i < nt - 1, ph[8:9], 0.0)
        up = jnp.where(row == 0, prev, pltpu.roll(p, 1, axis=0))
        dn = jnp.where(row == tm - 1, nxt, pltpu.roll(p, tm - 1, axis=0))
        cw = cw_ref[:, cols]
        o_ref[j, 0] = up * cw[0:1] + p * cw[1:2] + dn * cw[2:3] + cb_ref[:, cols]


def _hy_in(x, g, sh, sc, w, b, cw, cb, *, tm):
    bsz, n, d = x.shape
    nt = n // tm
    hb = tm // 8
    per_b = pl.BlockSpec((1, 1, d), lambda bb, i: (bb, 0, 0))
    full = lambda a: pl.BlockSpec(a.shape, lambda bb, i: (0,) * a.ndim)
    return pl.pallas_call(
        functools.partial(_hy_in_body, nt=nt),
        grid=(bsz, nt),
        in_specs=[
            pl.BlockSpec((1, tm, d), lambda bb, i: (bb, i, 0)),
            pl.BlockSpec((1, 8, d), lambda bb, i: (bb, jnp.maximum(i * hb - 1, 0), 0)),
            pl.BlockSpec((1, 8, d), lambda bb, i: (bb, jnp.minimum((i + 1) * hb, n // 8 - 1), 0)),
            full(g), per_b, per_b, full(w), full(b), full(cw), full(cb),
        ],
        out_specs=pl.BlockSpec((3, 1, tm, d), lambda bb, i: (0, bb, i, 0)),
        out_shape=jax.ShapeDtypeStruct((3, bsz, n, d), F32),
        compiler_params=_cp(("arbitrary", "arbitrary")),
        name="hyena_in_proj",
    )(x, x, x, g, sh, sc, w, b, cw, cb)


def _filt_feat_body(w1_ref, b1_ref, f1_ref, w2_ref, b2_ref, f2_ref, o_ref, *, n_lat):
    na = o_ref.shape[1]
    a = lax.broadcasted_iota(jnp.int32, (na, 1), 0)
    lane = lax.broadcasted_iota(jnp.int32, (na, LANES), 1)
    band_idx = jnp.where(lane <= HY_BANDS, lane - 1, lane - 1 - HY_BANDS).astype(F32)
    band = 1e-4 + band_idx * ((HY_BANDS - 1 - 1e-4) / (HY_BANDS - 1))
    for j in range(B_GROUP):
        r = a * FFT_N2 + (pl.program_id(0) * B_GROUP + j)
        pos = jnp.where(r < n_lat, r, 2 * n_lat - r).astype(F32)
        tn = pos / float(max(n_lat - 1, 1))
        ang = ((2.0 * math.pi / n_lat) * pos) * band
        z = jnp.where(lane == 0, tn, jnp.where(lane <= HY_BANDS, jnp.cos(ang),
                                               jnp.where(lane < HY_EMB, -jnp.sin(ang), 0.0)))
        h1 = jnp.sin(f1_ref[...] * (_dot_hi(z, w1_ref[...]) + b1_ref[...]))
        h2 = jnp.sin(f2_ref[...] * (_dot_hi(h1, w2_ref[...]) + b2_ref[...]))
        valid = (r != n_lat).astype(F32)
        o_ref[j] = jnp.where(lane == HY_HID, tn, jnp.where(lane == HY_HID + 1, valid, h2))


def _filt_feat(w1, b1, f1, w2, b2, f2, *, n_lat):
    na = 2 * n_lat // FFT_N2
    w1p = jnp.zeros((LANES, LANES), F32).at[:HY_EMB, :HY_HID].set(w1)
    w2p = jnp.zeros((LANES, LANES), F32).at[:HY_HID, :HY_HID].set(w2)
    padv = lambda v: jnp.zeros((1, LANES), F32).at[0, :HY_HID].set(v)
    full = lambda shp: pl.BlockSpec(shp, lambda i: (0,) * len(shp))
    return pl.pallas_call(
        functools.partial(_filt_feat_body, n_lat=n_lat),
        grid=(FFT_N2 // B_GROUP,),
        in_specs=[full((LANES, LANES)), full((1, LANES)), full((1, LANES)),
                  full((LANES, LANES)), full((1, LANES)), full((1, LANES))],
        out_specs=pl.BlockSpec((B_GROUP, na, LANES), lambda i: (i, 0, 0)),
        out_shape=jax.ShapeDtypeStruct((FFT_N2, na, LANES), F32),
        compiler_params=_cp(("arbitrary",)),
        name="hyena_filter_features",
    )(w1p, padv(b1), padv(f1), w2p, padv(b2), padv(f2))


U32 = jnp.uint32
HI16 = 0xFFFF0000


def _pack_c(re, im):
    lo = lax.bitcast_convert_type(re.astype(BF16).astype(F32), U32) >> 16
    hi = lax.bitcast_convert_type(im.astype(BF16).astype(F32), U32) & U32(HI16)
    return hi | lo


def _unpack_c(u):
    re = lax.bitcast_convert_type(u << 16, F32)
    im = lax.bitcast_convert_type(u & U32(HI16), F32)
    return jnp.concatenate([re, im], axis=0).astype(BF16)


def _filt_s1_body(hd_ref, w3_ref, dec_ref, tab_ref, oa_ref, ob_ref):
    na = hd_ref.shape[1]
    ha = na // 2
    oa2, ob2 = _rows2d(oa_ref), _rows2d(ob_ref)
    ft = hd_ref[:, :ha, :].reshape(B_GROUP * ha, LANES)
    fb = hd_ref[:, ha:, :].reshape(B_GROUP * ha, LANES)
    top = _dot(ft.astype(BF16), w3_ref[0, 0].astype(BF16))
    top = top * jnp.exp(-ft[:, HY_HID:HY_HID + 1] * jnp.abs(dec_ref[0, 0]))
    bot = _dot(fb.astype(BF16), w3_ref[0, 1].astype(BF16))
    bot = bot * (jnp.exp(-fb[:, HY_HID:HY_HID + 1] * jnp.abs(dec_ref[0, 1])) * fb[:, HY_HID + 1:HY_HID + 2])
    for j in range(B_GROUP):
        hb = jnp.concatenate([top[j * ha:(j + 1) * ha], bot[j * ha:(j + 1) * ha]], axis=0).astype(BF16)
        r = _dot(tab_ref[j], hb)
        packed = _pack_c(r[:na], r[na:])
        oa2[pl.ds(j, na, stride=B_GROUP), :] = packed[:, :LANES]
        ob2[pl.ds(j, na, stride=B_GROUP), :] = packed[:, LANES:]


def _filt_s1(hd, w3r, dec, tab):
    _, na, _ = hd.shape
    d = w3r.shape[-1]
    ct = 2 * LANES
    half = pl.BlockSpec((None, na, B_GROUP, LANES), lambda o, g, c: (o, 0, g, c))
    return pl.pallas_call(
        _filt_s1_body,
        grid=(2, FFT_N2 // B_GROUP, d // ct),
        in_specs=[
            pl.BlockSpec((B_GROUP, na, LANES), lambda o, g, c: (g, 0, 0)),
            pl.BlockSpec((1, 2, LANES, ct), lambda o, g, c: (o, 0, 0, c)),
            pl.BlockSpec((1, 2, 1, ct), lambda o, g, c: (o, 0, 0, c)),
            pl.BlockSpec((B_GROUP, 2 * na, na), lambda o, g, c: (g, 0, 0)),
        ],
        out_specs=[half, half],
        out_shape=[jax.ShapeDtypeStruct((2, na, FFT_N2, d // 2), U32)] * 2,
        compiler_params=_cp(("arbitrary", "arbitrary", "arbitrary")),
        name="hyena_filter_dft1",
    )(hd, w3r, dec, tab)


S2_KB = 4


def _s2_body(*refs, conv):
    if conv:
        o_ref, kf_ref, ff_ref, fi_ref, g_ref = refs
    else:
        oa_ref, ob_ref, ff_ref, g_ref = refs
    for u in range(S2_KB):
        if conv:
            xin = o_ref[u]
        else:
            xa, xb = oa_ref[u], ob_ref[u]
            tiles = []
            for c0 in range(0, xa.shape[1], LANES):
                tiles += [xa[:, c0:c0 + LANES], xb[:, c0:c0 + LANES]]
            xin = jnp.concatenate(tiles, axis=1)
        xf = _dot(ff_ref[...], _unpack_c(xin))
        if conv:
            xr, xi = xf[:FFT_N2], xf[FFT_N2:]
            kr = kf_ref[0, u].astype(F32)
            ki = kf_ref[1, u].astype(F32)
            y = jnp.concatenate([xr * kr - xi * ki, xr * ki + xi * kr], axis=0).astype(BF16)
            xf = _dot(fi_ref[...], y)
            g_ref[u] = _pack_c(xf[:FFT_N2], xf[FFT_N2:])
        else:
            g_ref[0, u] = xf[:FFT_N2].astype(BF16)
            g_ref[1, u] = xf[FFT_N2:].astype(BF16)


def _s2(o3, kf, order, ff, fi, *, ct, conv):
    n1 = (o3 if conv else o3[0]).shape[-3]
    d = o3.shape[-1] if conv else 2 * o3[0].shape[-1]
    full = lambda a: pl.BlockSpec(a.shape, lambda k, c: (0,) * a.ndim)
    nk = n1 // S2_KB
    if conv:
        blk = pl.BlockSpec((S2_KB, FFT_N2, ct), lambda k, c: (k, 0, c))
        in_specs = [blk, pl.BlockSpec((None, 2, S2_KB, FFT_N2, ct), lambda k, c: (order, 0, k, 0, c)), full(ff),
                    full(fi)]
        args = (o3, kf, ff, fi)
        grid = (nk, d // ct)
        out_specs = blk
        out_shape = jax.ShapeDtypeStruct(o3.shape, U32)
    else:
        no = o3[0].shape[0]
        hblk = pl.BlockSpec((None, S2_KB, FFT_N2, ct // 2), lambda k, c: (k // nk, k % nk, 0, c))
        in_specs = [hblk, hblk, full(ff)]
        args = (o3[0], o3[1], ff)
        grid = (no * nk, d // ct)
        out_specs = pl.BlockSpec((None, 2, S2_KB, FFT_N2, ct), lambda k, c: (k // nk, 0, k % nk, 0, c))
        out_shape = jax.ShapeDtypeStruct((no, 2, n1, FFT_N2, d), BF16)
    return pl.pallas_call(
        functools.partial(_s2_body, conv=conv),
        grid=grid,
        in_specs=in_specs,
        out_specs=out_specs,
        out_shape=out_shape,
        compiler_params=_cp(("arbitrary", "arbitrary")),
        name="hyena_conv_dft2" if conv else "hyena_filter_dft2",
    )(*args)


def _rows2d(ref):
    lead = ref.shape[:-3]
    return ref.reshape(lead + (ref.shape[-3] * B_GROUP, ref.shape[-1]))


def _s1_body(z_ref, tab_ref, o_ref):
    rows, n1 = z_ref.shape[0], o_ref.shape[0]
    z2, o2 = _rows2d(z_ref), _rows2d(o_ref)
    for j in range(B_GROUP):
        zj = z2[pl.ds(j, rows, stride=B_GROUP), :]
        r = _dot(tab_ref[j], zj.astype(BF16))
        o2[pl.ds(j, n1, stride=B_GROUP), :] = _pack_c(r[:n1], r[n1:])


def _s1(z4, zi, tab, *, ct):
    _, rows, _, d = z4.shape
    n1 = tab.shape[1] // 2
    return pl.pallas_call(
        _s1_body,
        grid=(FFT_N2 // B_GROUP, d // ct),
        in_specs=[
            pl.BlockSpec((None, rows, B_GROUP, ct), lambda g, c: (zi, 0, g, c)),
            pl.BlockSpec((B_GROUP, 2 * n1, rows), lambda g, c: (g, 0, 0)),
        ],
        out_specs=pl.BlockSpec((n1, B_GROUP, ct), lambda g, c: (0, g, c)),
        out_shape=jax.ShapeDtypeStruct((n1, FFT_N2, d), U32),
        compiler_params=_cp(("arbitrary", "arbitrary")),
        name="hyena_conv_dft1",
    )(z4, tab)


def _s3_body(g_ref, tab_ref, gate_ref, z_ref, fb_ref, *rest, chain):
    if chain:
        tab1_ref, o_ref, o1_ref = rest
    else:
        (o_ref,) = rest
    n1, rows = g_ref.shape[0], o_ref.shape[0]
    g2, o2 = _rows2d(g_ref), _rows2d(o_ref)
    for j in range(B_GROUP):
        gj = _unpack_c(g2[pl.ds(j, n1, stride=B_GROUP), :])
        o2[pl.ds(j, rows, stride=B_GROUP), :] = _dot(tab_ref[j], gj)
    o_ref[...] = gate_ref[...] * (o_ref[...] + z_ref[...] * fb_ref[...])
    if chain:
        q2 = _rows2d(o1_ref)
        for j in range(B_GROUP):
            zj = o2[pl.ds(j, rows, stride=B_GROUP), :]
            r = _dot(tab1_ref[j], zj.astype(BF16))
            q2[pl.ds(j, n1, stride=B_GROUP), :] = _pack_c(r[:n1], r[n1:])


def _s3(g3, tab, gate4, gi, z4, zi, fb, tab1, *, ct):
    n1, _, d = g3.shape
    rows = tab.shape[1]
    chain = tab1 is not None
    blk = pl.BlockSpec((rows, B_GROUP, ct), lambda g, c: (0, g, c))
    cblk = pl.BlockSpec((n1, B_GROUP, ct), lambda g, c: (0, g, c))
    in_specs = [
        cblk,
        pl.BlockSpec((B_GROUP, rows, 2 * n1), lambda g, c: (g, 0, 0)),
        pl.BlockSpec((None, rows, B_GROUP, ct), lambda g, c: (gi, 0, g, c)),
        pl.BlockSpec((None, rows, B_GROUP, ct), lambda g, c: (zi, 0, g, c)),
        pl.BlockSpec((1, 1, ct), lambda g, c: (0, 0, c)),
    ]
    args = [g3, tab, gate4, z4, fb]
    out_specs = blk
    out_shape = jax.ShapeDtypeStruct((rows, FFT_N2, d), F32)
    if chain:
        in_specs.append(pl.BlockSpec((B_GROUP, 2 * n1, rows), lambda g, c: (g, 0, 0)))
        args.append(tab1)
        out_specs = [blk, cblk]
        out_shape = [out_shape, jax.ShapeDtypeStruct((n1, FFT_N2, d), U32)]
    return pl.pallas_call(
        functools.partial(_s3_body, chain=chain),
        grid=(FFT_N2 // B_GROUP, d // ct),
        in_specs=in_specs,
        out_specs=out_specs,
        out_shape=out_shape,
        compiler_params=_cp(("arbitrary", "arbitrary")),
        name="hyena_conv_idft1_dft1" if chain else "hyena_conv_idft1",
    )(*args)


def _dft_tables(n_lat):
    n = 2 * n_lat
    n1 = n // FFT_N2
    k1 = jnp.arange(n1, dtype=jnp.int32)
    th_a = ((k1[:, None] * k1[None, :]) % n1).astype(F32) * (2.0 * math.pi / n1)
    th_b = (jnp.arange(FFT_N2, dtype=jnp.int32)[:, None] * k1[None, :]).astype(F32) * (2.0 * math.pi / n)
    ca, sa = jnp.cos(th_a)[None], jnp.sin(th_a)[None]
    cb, sb = jnp.cos(th_b)[:, :, None], jnp.sin(th_b)[:, :, None]
    cr = ca * cb - sa * sb
    sn = sa * cb + ca * sb
    ha = n1 // 2
    crh, snh = cr[:, :, :ha], sn[:, :, :ha]
    w1 = jnp.concatenate([jnp.concatenate([crh, snh], axis=2), jnp.concatenate([-snh, crh], axis=2)], axis=1)
    w1f = jnp.concatenate([cr, -sn], axis=1)
    v = jnp.swapaxes(w1, 1, 2) * (1.0 / n)
    k2 = jnp.arange(FFT_N2, dtype=jnp.int32)
    th2 = ((k2[:, None] * k2[None, :]) % FFT_N2).astype(F32) * (2.0 * math.pi / FFT_N2)
    c2, s2 = jnp.cos(th2), jnp.sin(th2)
    ff = jnp.concatenate([jnp.concatenate([c2, s2], axis=1), jnp.concatenate([-s2, c2], axis=1)], axis=0)
    fi = jnp.concatenate([jnp.concatenate([c2, -s2], axis=1), jnp.concatenate([s2, c2], axis=1)], axis=0)
    return w1.astype(BF16), w1f.astype(BF16), v.astype(BF16), ff.astype(BF16), fi.astype(BF16)


def _hyena_mix(proj3, fparams, fbias, *, n_lat):
    _, bsz, _, d = proj3.shape
    f_w1, f_b1, f_f1, f_w2, f_b2, f_f2, f_w3, decay = fparams
    na = 2 * n_lat // FFT_N2
    w1, w1f, v, ff, fi = _dft_tables(n_lat)
    ct = LANES
    hd = _filt_feat(f_w1, f_b1, f_f1, f_w2, f_b2, f_f2, n_lat=n_lat)
    w3r = jnp.transpose(f_w3.reshape(HY_HID, 2, 2, d), (1, 2, 0, 3))
    w3r = jnp.zeros((2, 2, LANES, d), F32).at[:, :, :HY_HID].set(w3r)
    kf1 = _filt_s1(hd, w3r, decay.reshape(2, 2, 1, d), w1f)
    kf = _s2(kf1, None, 0, ff, None, ct=d, conv=False)
    p3 = proj3.reshape(3, bsz * (n_lat // FFT_N2), FFT_N2, d)
    o1 = _s1(p3, 2, w1, ct=ct)
    g = _s2(o1, kf, 0, ff, fi, ct=d, conv=True)
    z, o1 = _s3(g, v, p3, 0, p3, 2, fbias[0].reshape(1, 1, d), w1, ct=ct)
    g = _s2(o1, kf, 1, ff, fi, ct=d, conv=True)
    z = _s3(g, v, p3, 1, z[None], 0, fbias[1].reshape(1, 1, d), None, ct=ct)
    return z.reshape(bsz, n_lat, d)


def _rope_tables(n_tokens):
    rows = n_tokens // GRID_W
    row = jnp.broadcast_to(jnp.arange(rows, dtype=F32)[:, None], (rows, GRID_W)).reshape(-1)
    col = jnp.broadcast_to(jnp.arange(GRID_W, dtype=F32)[None, :], (rows, GRID_W)).reshape(-1)
    axis_dim = QK_ROPE // 2
    inv_freq = 1.0 / (ROPE_THETA ** (jnp.arange(0, axis_dim, 2, dtype=F32) / axis_dim))
    ang = jnp.concatenate([row[:, None] * inv_freq, col[:, None] * inv_freq], axis=-1)
    return jnp.cos(ang), jnp.sin(ang)


def _mla_weights(w_down, g_q, w_uq, g_kv, w_ukv):
    d = w_down.shape[0]
    nh = MLA_HEADS
    kpe = w_down[:, Q_LORA + KV_LORA:]
    w1, w2 = kpe[:, 0::2], kpe[:, 1::2]
    z = jnp.zeros((d, LANES - QK_ROPE), w_down.dtype)
    wd = jnp.concatenate([w_down[:, :Q_LORA + KV_LORA], w1, w2, z, w2, w1, z], axis=1).astype(BF16)
    uq = w_uq.reshape(Q_LORA, nh, QK_NOPE + QK_ROPE)
    pe = uq[:, :, QK_NOPE:]
    uq = jnp.concatenate([uq[:, :, :QK_NOPE], pe[:, :, 0::2], pe[:, :, 1::2]], axis=2)
    wuqT = uq.reshape(Q_LORA, nh * (QK_NOPE + QK_ROPE)).T.astype(BF16)
    ukv = w_ukv.reshape(KV_LORA, nh, QK_NOPE + V_DIM)
    wuk = ukv[:, :, :QK_NOPE].reshape(KV_LORA, nh * QK_NOPE).astype(BF16)
    wuvT = ukv[:, :, QK_NOPE:].reshape(KV_LORA, nh * V_DIM).T.astype(BF16)
    return wd, g_q.reshape(1, -1), g_kv.reshape(1, -1), wuk, wuqT, wuvT


def kernel(x, c, ctx, c_ctx, ada_w, ada_b, norm_mix_g, norm_ffn_g, mla_w_down, mla_g_q, mla_w_uq, mla_g_kv, mla_w_ukv, mla_w_o, hy_w_in, hy_b_in, hy_conv_w, hy_conv_b, hy_f_w1, hy_f_b1, hy_f_freq1, hy_f_w2, hy_f_b2, hy_f_freq2, hy_f_w3, hy_decay, hy_bias, hy_w_out, hy_b_out, moe_w_r, moe_b_r, moe_w_in, moe_b_in, moe_w_out, moe_b_out, final_g):
    bsz, n_lat, d = x.shape
    n_ctx = ctx.shape[1]
    depth = ada_w.shape[0]
    assert bsz == 2 and d == MLA_HEADS * V_DIM and n_lat % 512 == 0 and n_ctx % 128 == 0
    assert depth == 2

    cond8 = jnp.zeros((8, d), F32).at[:bsz].set(c).at[bsz].set(c_ctx)
    mods = _ada(cond8, ada_w, ada_b)

    def mod(i, j, rows):
        return mods[i, rows, j * d:(j + 1) * d][:, None, :]

    lat_rows = slice(0, bsz)
    ctx_rows = slice(bsz, bsz + 1)
    xl = x
    for i in range(depth):
        kind, j = i % 2, i // 2
        sh1, sc1, g1 = (mod(i, m, lat_rows) for m in range(3))
        sh2, sc2, g2 = (mod(i, m, lat_rows) for m in range(3, 6))
        gm = norm_mix_g[i].reshape(1, d)
        if kind == 0:
            wts = _mla_weights(mla_w_down[j], mla_g_q[j], mla_w_uq[j], mla_g_kv[j], mla_w_ukv[j])
            cos, sin = _rope_tables(n_lat)
            zl = jnp.zeros((n_lat, LANES - QK_ROPE), F32)
            tabs = (jnp.concatenate([cos, cos, zl], axis=1), jnp.concatenate([-sin, sin, zl], axis=1), cos.T, sin.T)
            tq = tv = 512
            tk = 2048 if n_lat % 4096 == 0 else 512
            qT, k, vT = _mla_proj(xl, gm, sh1, sc1, wts, tabs, need_q=True, tm=tv, tk=tv)
            half = QK_ROPE // 2
            one_c = jnp.concatenate([jnp.ones((n_ctx, QK_ROPE), F32), jnp.zeros((n_ctx, LANES - QK_ROPE), F32)], axis=1)
            tabs_c = (one_c, jnp.zeros((n_ctx, LANES), F32), jnp.ones((half, n_ctx), F32), jnp.zeros((half, n_ctx), F32))
            kc, vTc = _mla_proj(ctx, gm, mod(i, 0, ctx_rows), mod(i, 1, ctx_rows), wts, tabs_c,
                                need_q=False, tm=n_ctx, tk=n_ctx)
            o = _attention(qT, k, vT, kc, vTc, tq=tq, tk=tk)
            wo = mla_w_o[j].astype(BF16)
            bo = jnp.zeros((1, d), F32)
            transposed = True
        else:
            proj3 = _hy_in(xl, gm, sh1, sc1, hy_w_in[j].astype(BF16), hy_b_in[j].reshape(1, -1), hy_conv_w[j],
                           hy_conv_b[j].reshape(1, -1), tm=512)
            fparams = (hy_f_w1[j], hy_f_b1[j], hy_f_freq1[j], hy_f_w2[j], hy_f_b2[j], hy_f_freq2[j], hy_f_w3[j],
                       hy_decay[j])
            o = _hyena_mix(proj3, fparams, hy_bias[j], n_lat=n_lat)
            wo = hy_w_out[j].astype(BF16)
            bo = hy_b_out[j].reshape(1, d)
            transposed = False
        wr = jnp.zeros((d, LANES), F32).at[:, :N_EXPERTS].set(moe_w_r[i])
        wrh = wr.astype(BF16)
        wrl = (wr - wrh.astype(F32)).astype(BF16)
        br = jnp.zeros((1, LANES), F32).at[0, :N_EXPERTS].set(moe_b_r[i])
        xl, fl, topi, gates, rank, cnt = _post(o, wo, bo, xl, g1, norm_ffn_g[i].reshape(1, d), sh2, sc2, wrh, wrl, br,
                                               transposed=transposed, tm=512)
        xl = _moe(fl, topi, gates, rank, cnt, xl, g2, final_g.reshape(1, d), i, moe_w_in, moe_b_in,
                  moe_w_out, moe_b_out, final=(i == depth - 1))
    return xl
```

```python
import functools
import math

import jax
import jax.numpy as jnp
from jax import lax
from jax.experimental import pallas as pl
from jax.experimental.pallas import tpu as pltpu

F32 = jnp.float32
BF16 = jnp.bfloat16

EPS = 1e-6
GRID_W = 64
MLA_HEADS = 8
QK_NOPE = 128
QK_ROPE = 64
V_DIM = 128
Q_LORA = 512
KV_LORA = 256
ROPE_THETA = 10000.0
MLA_SCALE = (QK_NOPE + QK_ROPE) ** -0.5
QK_PAD = 256

HY_EMB = 33
HY_BANDS = (HY_EMB - 1) // 2
HY_HID = 64
FFT_N2 = 128
B_GROUP = 8

N_EXPERTS = 32
TOP_K = 4
SWIGLU_LIMIT = 7.0
SWIGLU_ALPHA = 1.702
MOE_TM = 256
LANES = 128

VMEM_LIMIT = 56 * 1024 * 1024


def _cp(sem, vmem=VMEM_LIMIT):
    return pltpu.CompilerParams(dimension_semantics=sem, vmem_limit_bytes=vmem)


def _dot(a, b):
    return jnp.dot(a, b, preferred_element_type=F32)


def _dot_hi(a, b):
    return jnp.dot(a, b, preferred_element_type=F32, precision=lax.Precision.HIGHEST)


def _rms(x, g):
    return x * lax.rsqrt(jnp.mean(x * x, axis=-1, keepdims=True) + EPS) * g


def _ada_body(c_ref, w_ref, b_ref, o_ref):
    c = c_ref[...]
    s = c * jax.nn.sigmoid(c)
    o_ref[0] = _dot(s.astype(BF16), w_ref[0].astype(BF16)) + b_ref[0]


def _ada(cond8, ada_w, ada_b):
    depth, d, n = ada_w.shape
    tn = n // 4
    return pl.pallas_call(
        _ada_body,
        grid=(depth, n // tn),
        in_specs=[
            pl.BlockSpec((8, d), lambda i, j: (0, 0)),
            pl.BlockSpec((1, d, tn), lambda i, j: (i, 0, j)),
            pl.BlockSpec((1, 1, tn), lambda i, j: (i, 0, j)),
        ],
        out_specs=pl.BlockSpec((1, 8, tn), lambda i, j: (i, 0, j)),
        out_shape=jax.ShapeDtypeStruct((depth, 8, n), F32),
        compiler_params=_cp(("arbitrary", "arbitrary")),
        name="ada_mod",
    )(cond8, ada_w, ada_b.reshape(depth, 1, n))


def _mla_proj_body(x_ref, g_ref, sh_ref, sc_ref, wd_ref, gq_ref, gkv_ref, wuk_ref, wuqT_ref, wuvT_ref,
                   ct_ref, st_ref, cT_ref, sT_ref, *out_refs, need_q, tk):
    if need_q:
        qT_ref, k_ref, vT_ref = out_refs
    else:
        k_ref, vT_ref = out_refs
    nh = MLA_HEADS
    x = x_ref[0]
    h = _rms(x, g_ref[...]) * (1.0 + sc_ref[0]) + sh_ref[0]
    lat = _dot(h.astype(BF16), wd_ref[...])
    o_kv = Q_LORA
    o_a = Q_LORA + KV_LORA
    kvn = _rms(lat[:, o_kv:o_a], gkv_ref[...])
    kr = (lat[:, o_a:o_a + LANES] * ct_ref[...] + lat[:, o_a + LANES:o_a + 2 * LANES] * st_ref[...]).astype(BF16)
    knope = _dot(kvn.astype(BF16), wuk_ref[...])
    for hh in range(nh):
        k_ref[0, hh, :, 0:QK_NOPE] = knope[:, hh * QK_NOPE:(hh + 1) * QK_NOPE].astype(BF16)
        k_ref[0, hh, :, QK_NOPE:QK_PAD] = kr
    vT = _dot(wuvT_ref[...], kvn.T.astype(BF16))
    tm = x.shape[0]
    for hh in range(nh):
        for c in range(tm // tk):
            vT_ref[0, hh, c] = vT[hh * V_DIM:(hh + 1) * V_DIM, c * tk:(c + 1) * tk].astype(BF16)
    if need_q:
        qn = _rms(lat[:, :Q_LORA], gq_ref[...])
        qT = _dot(wuqT_ref[...], qn.T.astype(BF16)) * (MLA_SCALE * math.log2(math.e))
        c = cT_ref[...]
        s = sT_ref[...]
        hw = QK_NOPE + QK_ROPE
        half = QK_ROPE // 2
        for hh in range(nh):
            base = hh * hw
            x1 = qT[base + QK_NOPE:base + QK_NOPE + half]
            x2 = qT[base + QK_NOPE + half:base + hw]
            qT_ref[0, hh, 0:QK_NOPE] = qT[base:base + QK_NOPE].astype(BF16)
            qT_ref[0, hh, QK_NOPE:QK_NOPE + half] = (x1 * c - x2 * s).astype(BF16)
            qT_ref[0, hh, QK_NOPE + half:hw] = (x1 * s + x2 * c).astype(BF16)
            qT_ref[0, hh, hw:QK_PAD] = jnp.zeros((QK_PAD - hw, tm), BF16)


def _mla_proj(x, g, sh, sc, wts, tabs, *, need_q, tm, tk):
    bsz, n, d = x.shape
    nh = MLA_HEADS
    wd, gq, gkv, wuk, wuqT, wuvT = wts
    ct, st, cT, sT = tabs
    nsh = sh.shape[0]
    full = lambda a: pl.BlockSpec(a.shape, lambda b, i: (0,) * a.ndim)
    in_specs = [
        pl.BlockSpec((1, tm, d), lambda b, i: (b, i, 0)),
        full(g),
        pl.BlockSpec((1, 1, d), lambda b, i: (b % nsh, 0, 0)),
        pl.BlockSpec((1, 1, d), lambda b, i: (b % nsh, 0, 0)),
        full(wd), full(gq), full(gkv), full(wuk), full(wuqT), full(wuvT),
        pl.BlockSpec((tm, LANES), lambda b, i: (i, 0)),
        pl.BlockSpec((tm, LANES), lambda b, i: (i, 0)),
        pl.BlockSpec((QK_ROPE // 2, tm), lambda b, i: (0, i)),
        pl.BlockSpec((QK_ROPE // 2, tm), lambda b, i: (0, i)),
    ]
    out_specs = [
        pl.BlockSpec((1, nh, tm, QK_PAD), lambda b, i: (b, 0, i, 0)),
        pl.BlockSpec((1, nh, tm // tk, V_DIM, tk), lambda b, i: (b, 0, i, 0, 0)),
    ]
    out_shape = [
        jax.ShapeDtypeStruct((bsz, nh, n, QK_PAD), BF16),
        jax.ShapeDtypeStruct((bsz, nh, n // tk, V_DIM, tk), BF16),
    ]
    if need_q:
        out_specs = [pl.BlockSpec((1, nh, QK_PAD, tm), lambda b, i: (b, 0, 0, i))] + out_specs
        out_shape = [jax.ShapeDtypeStruct((bsz, nh, QK_PAD, n), BF16)] + out_shape
    return pl.pallas_call(
        functools.partial(_mla_proj_body, need_q=need_q, tk=tk),
        grid=(bsz, n // tm),
        in_specs=in_specs,
        out_specs=out_specs,
        out_shape=out_shape,
        compiler_params=_cp(("arbitrary", "arbitrary")),
        name="mla_proj_q" if need_q else "mla_proj_ctx",
    )(x, g, sh, sc, wd, gq, gkv, wuk, wuqT, wuvT, ct, st, cT, sT)


SM_STRIP = 64
SUBLANES = 8


def _attn_body(qT_ref, k_ref, vT_ref, kc_ref, vTc_ref, o_ref, s0, s1, p0, p1, sc, pc, acc, m_scr, x0, x1, xc,
               a0, a1, ac, l_scr, d0, d1, dc, *, tk):
    nchunk = k_ref.shape[2] // tk

    def scores(kblk, s_ref, mx_ref):
        r = _dot(kblk, qT_ref[0, 0])
        s_ref[...] = r
        mx_ref[...] = jnp.max(r, axis=0, keepdims=True)

    def probs(s_ref, mx_ref, p_ref, a_ref, d_ref):
        m_old = m_scr[...]
        m_new = jnp.maximum(m_old, mx_ref[...])
        m_scr[...] = m_new
        alpha = jnp.exp2(m_old - m_new)
        a_ref[...] = alpha
        part = None
        for r in range(0, s_ref.shape[0], SM_STRIP):
            p = jnp.exp2(s_ref[r:r + SM_STRIP] - m_new)
            p_ref[r:r + SM_STRIP] = p.astype(BF16)
            ps = jnp.sum(p.reshape(SM_STRIP // SUBLANES, SUBLANES, p.shape[1]), axis=0)
            part = ps if part is None else part + ps
        d_ref[...] = part

    def accumulate(p_ref, a_ref, d_ref, vblk):
        acc[...] = a_ref[...] * acc[...] + _dot(vblk, p_ref[...])
        l_scr[...] = a_ref[...] * l_scr[...] + d_ref[...]

    def kchunk(i):
        return k_ref[0, 0, pl.ds(pl.multiple_of(i * tk, tk), tk), :]

    def vchunk(i):
        nsub = tk // vT_ref.shape[-1]
        return jnp.concatenate([vT_ref[0, 0, i * nsub + u] for u in range(nsub)], axis=1)

    m_scr[...] = jnp.full(m_scr.shape, -jnp.inf, F32)
    acc[...] = jnp.zeros(acc.shape, F32)
    l_scr[...] = jnp.zeros(l_scr.shape, F32)
    scores(kc_ref[0, 0], sc, xc)
    scores(kchunk(0), s0, x0)
    probs(sc, xc, pc, ac, dc)
    scores(kchunk(1), s1, x1)
    accumulate(pc, ac, dc, vTc_ref[0, 0, 0])
    probs(s0, x0, p0, a0, d0)

    def body(j, carry):
        t = 2 * j
        scores(kchunk(t + 2), s0, x0)
        accumulate(p0, a0, d0, vchunk(t))
        probs(s1, x1, p1, a1, d1)
        scores(kchunk(t + 3), s1, x1)
        accumulate(p1, a1, d1, vchunk(t + 1))
        probs(s0, x0, p0, a0, d0)
        return carry

    lax.fori_loop(0, nchunk // 2 - 1, body, 0)
    accumulate(p0, a0, d0, vchunk(nchunk - 2))
    probs(s1, x1, p1, a1, d1)
    accumulate(p1, a1, d1, vchunk(nchunk - 1))
    o_ref[0, 0] = (acc[...] / jnp.sum(l_scr[...], axis=0, keepdims=True)).astype(BF16)


def _attention(qT, k, vT, kc, vTc, *, tq, tk):
    bsz, nh, _, n = qT.shape
    nc = kc.shape[2]
    tv = vT.shape[-1]
    assert (n // tk) % 2 == 0 and tk % tv == 0
    return pl.pallas_call(
        functools.partial(_attn_body, tk=tk),
        grid=(bsz, nh, n // tq),
        in_specs=[
            pl.BlockSpec((1, 1, QK_PAD, tq), lambda b, h, i: (b, h, 0, i)),
            pl.BlockSpec((1, 1, n, QK_PAD), lambda b, h, i: (b, h, 0, 0)),
            pl.BlockSpec((1, 1, n // tv, V_DIM, tv), lambda b, h, i: (b, h, 0, 0, 0)),
            pl.BlockSpec((1, 1, nc, QK_PAD), lambda b, h, i: (b, h, 0, 0)),
            pl.BlockSpec((1, 1, 1, V_DIM, nc), lambda b, h, i: (b, h, 0, 0, 0)),
        ],
        out_specs=pl.BlockSpec((1, 1, V_DIM, tq), lambda b, h, i: (b, h, 0, i)),
        out_shape=jax.ShapeDtypeStruct((bsz, nh, V_DIM, n), BF16),
        scratch_shapes=[pltpu.VMEM((tk, tq), F32), pltpu.VMEM((tk, tq), F32),
                        pltpu.VMEM((tk, tq), BF16), pltpu.VMEM((tk, tq), BF16),
                        pltpu.VMEM((nc, tq), F32), pltpu.VMEM((nc, tq), BF16),
                        pltpu.VMEM((V_DIM, tq), F32), pltpu.VMEM((1, tq), F32),
                        pltpu.VMEM((1, tq), F32), pltpu.VMEM((1, tq), F32), pltpu.VMEM((1, tq), F32),
                        pltpu.VMEM((1, tq), F32), pltpu.VMEM((1, tq), F32), pltpu.VMEM((1, tq), F32),
                        pltpu.VMEM((SUBLANES, tq), F32), pltpu.VMEM((SUBLANES, tq), F32),
                        pltpu.VMEM((SUBLANES, tq), F32), pltpu.VMEM((SUBLANES, tq), F32)],
        compiler_params=_cp(("arbitrary", "arbitrary", "arbitrary")),
        name="mla_attention",
    )(qT, k, vT, kc, vTc)


def _post_body(*refs, transposed):
    if transposed:
        o_ref, *refs = refs
    else:
        o_ref, ob_ref, *refs = refs
    (wo_ref, bo_ref, x_ref, g1_ref, gf_ref, sh_ref, sc_ref, wrh_ref, wrl_ref, br_ref, tri_ref,
     xl_ref, fl_ref, ti_ref, gt_ref, rk_ref, cnt_ref) = refs

    @pl.when((pl.program_id(0) == 0) & (pl.program_id(1) == 0))
    def _():
        cnt_ref[...] = jnp.zeros_like(cnt_ref)

    tm = x_ref.shape[1]
    if transposed:
        oT = o_ref[0].astype(F32).reshape(MLA_HEADS * V_DIM, tm)
        o = oT.T.astype(BF16)
    else:
        o = _interleave_tiles(o_ref[0], ob_ref[0]).astype(BF16)
    y = _dot(o, wo_ref[...]) + bo_ref[...]
    xl = x_ref[0] + g1_ref[0] * y
    xl_ref[0] = xl
    fl = _rms(xl, gf_ref[...]) * (1.0 + sc_ref[0]) + sh_ref[0]
    _to_rows(fl_ref, fl)
    flh = fl.astype(BF16)
    fll = (fl - flh.astype(F32)).astype(BF16)
    logits = _dot(flh, wrh_ref[...]) + (_dot(fll, wrh_ref[...]) + _dot(flh, wrl_ref[...])) + br_ref[...]
    lane = lax.broadcasted_iota(jnp.int32, (tm, LANES), 1).astype(F32)
    neg = jnp.float32(-jnp.inf)
    work = jnp.where(lane < N_EXPERTS, logits, neg)
    vals, idxs = [], []
    onehot = jnp.zeros((tm, LANES), F32)
    for _ in range(TOP_K):
        mk = jnp.max(work, axis=-1, keepdims=True)
        ik = jnp.min(jnp.where(work == mk, lane, float(LANES)), axis=-1, keepdims=True)
        sel = lane == ik
        onehot = jnp.where(sel, 1.0, onehot)
        work = jnp.where(sel, neg, work)
        vals.append(mk)
        idxs.append(ik)
    es = [jnp.exp(v - vals[0]) for v in vals]
    den = es[0] + es[1] + es[2] + es[3]
    pre = _dot(tri_ref[...], onehot.astype(BF16)) + cnt_ref[...]
    ti = jnp.zeros((tm, LANES), F32)
    gt = jnp.zeros((tm, LANES), F32)
    rk = jnp.zeros((tm, LANES), F32)
    for kk in range(TOP_K):
        rank = jnp.sum(jnp.where(lane == idxs[kk], pre, 0.0), axis=-1, keepdims=True)
        ti = jnp.where(lane == kk, idxs[kk], ti)
        gt = jnp.where(lane == kk, es[kk] / den, gt)
        rk = jnp.where(lane == kk, rank, rk)
    ti_ref[...] = ti[:, :TOP_K].astype(jnp.int32)
    gt_ref[...] = gt[:, :TOP_K]
    rk_ref[...] = rk[:, :TOP_K].astype(jnp.int32)
    cnt_ref[...] += jnp.sum(onehot, axis=0, keepdims=True)


def _post(o, wo, bo, x, g1, gf, sh, sc, wrh, wrl, br, *, transposed, tm):
    bsz, n, d = x.shape
    t = bsz * n
    nt = n // tm
    tri = (lax.broadcasted_iota(jnp.int32, (tm, tm), 0) > lax.broadcasted_iota(jnp.int32, (tm, tm), 1)).astype(BF16)
    full = lambda a: pl.BlockSpec(a.shape, lambda b, i: (0,) * a.ndim)
    per_b = pl.BlockSpec((1, 1, d), lambda b, i: (b, 0, 0))
    if transposed:
        o_args = [o]
        o_specs = [pl.BlockSpec((1, MLA_HEADS, V_DIM, tm), lambda b, i: (b, 0, 0, i))]
    else:
        o_args = list(o)
        o_specs = [pl.BlockSpec((1, tm, d // 2), lambda b, i: (b, i, 0))] * 2
    tok = lambda w: pl.BlockSpec((tm, w), lambda b, i: (b * nt + i, 0))
    return pl.pallas_call(
        functools.partial(_post_body, transposed=transposed),
        grid=(bsz, nt),
        in_specs=[*o_specs, full(wo), full(bo), pl.BlockSpec((1, tm, d), lambda b, i: (b, i, 0)), per_b, full(gf),
                  per_b, per_b, full(wrh), full(wrl), full(br), full(tri)],
        out_specs=[pl.BlockSpec((1, tm, d), lambda b, i: (b, i, 0)),
                   pl.BlockSpec((tm * ROW_SUB, LANES), lambda b, i: (b * nt + i, 0)),
                   tok(TOP_K), tok(TOP_K), tok(TOP_K), pl.BlockSpec((1, LANES), lambda b, i: (0, 0))],
        out_shape=[jax.ShapeDtypeStruct((bsz, n, d), F32), jax.ShapeDtypeStruct((t * ROW_SUB, LANES), F32),
                   jax.ShapeDtypeStruct((t, TOP_K), jnp.int32), jax.ShapeDtypeStruct((t, TOP_K), F32),
                   jax.ShapeDtypeStruct((t, TOP_K), jnp.int32), jax.ShapeDtypeStruct((1, LANES), F32)],
        compiler_params=_cp(("arbitrary", "arbitrary")),
        name="post_attn" if transposed else "post_hyena",
    )(*o_args, wo, bo, x, g1, gf, sh, sc, wrh, wrl, br, tri)


ROW_SUB = 8


def _row_slice(i):
    return pl.ds(pl.multiple_of(i * ROW_SUB, ROW_SUB), ROW_SUB)


def _to_rows(ref, x):
    for s in range(ROW_SUB):
        ref[pl.ds(s, x.shape[0], stride=ROW_SUB), :] = x[:, s * LANES:(s + 1) * LANES]


def _from_rows(ref, lo, hi):
    return jnp.concatenate([ref[pl.ds(lo * ROW_SUB + s, hi - lo, stride=ROW_SUB), :] for s in range(ROW_SUB)], axis=1)


def _dispatch_body(pe_ref, pd_ref, dest_ref, fl_ref, xs_out, zbuf, sem, *, td):
    @pl.when(pl.program_id(0) == 0)
    def _():
        zbuf[...] = jnp.zeros(zbuf.shape, zbuf.dtype)
        for e in range(N_EXPERTS):
            @pl.when(pd_ref[e] > 0)
            def _():
                start = pl.multiple_of((pe_ref[e] - MOE_TM) * ROW_SUB, ROW_SUB)
                cp = pltpu.make_async_copy(zbuf, xs_out.at[pl.ds(start, MOE_TM * ROW_SUB)], sem)
                cp.start()
                cp.wait()

    def issue(t, carry):
        for kk in range(TOP_K):
            d = dest_ref[0, 0, t * TOP_K + kk]
            pltpu.make_async_copy(fl_ref.at[_row_slice(t)], xs_out.at[_row_slice(d)], sem).start(priority=kk % 2)
        return carry

    lax.fori_loop(0, td, issue, 0, unroll=2)

    def drain(t, carry):
        pltpu.make_async_copy(fl_ref.at[_row_slice(0)], xs_out.at[_row_slice(0)], sem).wait()
        return carry

    lax.fori_loop(0, td * TOP_K, drain, 0, unroll=8)


def _dispatch(pad_end, padded, dest, fl, n_rows, *, td):
    t = fl.shape[0] // ROW_SUB
    dest3 = dest.reshape(t // td, 1, td * TOP_K)
    grid_spec = pltpu.PrefetchScalarGridSpec(
        num_scalar_prefetch=2,
        grid=(t // td,),
        in_specs=[
            pl.BlockSpec((1, 1, td * TOP_K), lambda i, pe, pd: (i, 0, 0), memory_space=pltpu.SMEM),
            pl.BlockSpec((td * ROW_SUB, LANES), lambda i, pe, pd: (i, 0)),
        ],
        out_specs=pl.BlockSpec(memory_space=pl.ANY),
        scratch_shapes=[pltpu.VMEM((MOE_TM * ROW_SUB, LANES), fl.dtype), pltpu.SemaphoreType.DMA(())],
    )
    return pl.pallas_call(
        functools.partial(_dispatch_body, td=td),
        grid_spec=grid_spec,
        out_shape=jax.ShapeDtypeStruct((n_rows * ROW_SUB, LANES), fl.dtype),
        compiler_params=_cp(("arbitrary",)),
        name="moe_dispatch",
    )(pad_end, padded, dest3, fl)


def _expert_body(be_ref, nu_ref, xs_ref, win_ref, bin_ref, wout_ref, bout_ref, ys_ref, win_s, wout_s):
    b = pl.program_id(0)
    dff = wout_ref.shape[1]

    @pl.when(b < nu_ref[0])
    def _():
        prev = be_ref[jnp.maximum(b - 1, 0)]

        @pl.when((b == 0) | (prev != be_ref[b]))
        def _():
            win_s[...] = win_ref[0].astype(BF16)
            wout_s[...] = wout_ref[0].astype(BF16)

        x = _from_rows(xs_ref, 0, xs_ref.shape[0] // ROW_SUB).astype(BF16)
        gu = _dot(x, win_s[...]) + bin_ref[0]
        gate = jnp.minimum(gu[:, :dff], SWIGLU_LIMIT)
        lin = jnp.clip(gu[:, dff:], -SWIGLU_LIMIT, SWIGLU_LIMIT)
        act = gate * jax.nn.sigmoid(SWIGLU_ALPHA * gate) * (lin + 1.0)
        _to_rows(ys_ref, _dot(act.astype(BF16), wout_s[...]) + bout_ref[0])

    @pl.when(b >= nu_ref[0])
    def _():
        ys_ref[...] = jnp.zeros_like(ys_ref)


def _experts(blk_exp, n_used, xs, layer, w_in, b_in, w_out, b_out):
    n_rows = xs.shape[0] // ROW_SUB
    depth, ne, d, f2 = w_in.shape
    dff = w_out.shape[2]
    tm = MOE_TM
    grid_spec = pltpu.PrefetchScalarGridSpec(
        num_scalar_prefetch=2,
        grid=(n_rows // tm,),
        in_specs=[
            pl.BlockSpec((tm * ROW_SUB, LANES), lambda b, be, nu: (jnp.minimum(b, nu[0] - 1), 0)),
            pl.BlockSpec((None, 1, d, f2), lambda b, be, nu: (layer, be[b], 0, 0)),
            pl.BlockSpec((None, 1, 1, f2), lambda b, be, nu: (layer, be[b], 0, 0)),
            pl.BlockSpec((None, 1, dff, d), lambda b, be, nu: (layer, be[b], 0, 0)),
            pl.BlockSpec((None, 1, 1, d), lambda b, be, nu: (layer, be[b], 0, 0)),
        ],
        out_specs=pl.BlockSpec((tm * ROW_SUB, LANES), lambda b, be, nu: (b, 0)),
        scratch_shapes=[pltpu.VMEM((d, f2), BF16), pltpu.VMEM((dff, d), BF16)],
    )
    return pl.pallas_call(
        _expert_body,
        grid_spec=grid_spec,
        out_shape=jax.ShapeDtypeStruct(xs.shape, F32),
        compiler_params=_cp(("arbitrary",)),
        name="moe_experts",
    )(blk_exp, n_used, xs, w_in, b_in.reshape(depth, ne, 1, f2), w_out, b_out.reshape(depth, ne, 1, d))


def _combine_body(dest_ref, ys_hbm, gt_ref, xl_ref, g2_ref, fg_ref, out_ref, buf, sem, *, tc, final):
    def issue(t, carry):
        for kk in range(TOP_K):
            d = dest_ref[0, 0, t * TOP_K + kk]
            pltpu.make_async_copy(ys_hbm.at[_row_slice(d)], buf.at[_row_slice(kk * tc + t)], sem).start(
                priority=kk % 2)
        return carry

    lax.fori_loop(0, tc, issue, 0, unroll=2)

    def drain(t, carry):
        pltpu.make_async_copy(ys_hbm.at[_row_slice(0)], buf.at[_row_slice(0)], sem).wait()
        return carry

    lax.fori_loop(0, tc * TOP_K, drain, 0, unroll=8)
    gt = gt_ref[...]
    y = gt[:, 0:1] * _from_rows(buf, 0, tc)
    for kk in range(1, TOP_K):
        y = y + gt[:, kk:kk + 1] * _from_rows(buf, kk * tc, (kk + 1) * tc)
    xl = xl_ref[0] + g2_ref[0] * y
    out_ref[0] = _rms(xl, fg_ref[...]) if final else xl


def _combine(dest, ys, gates, xl, g2, fg, *, tc, final):
    bsz, n, d = xl.shape
    t = bsz * n
    nt = n // tc
    dest3 = dest.reshape(t // tc, 1, tc * TOP_K)
    return pl.pallas_call(
        functools.partial(_combine_body, tc=tc, final=final),
        grid=(bsz, nt),
        in_specs=[
            pl.BlockSpec((1, 1, tc * TOP_K), lambda b, i: (b * nt + i, 0, 0), memory_space=pltpu.SMEM),
            pl.BlockSpec(memory_space=pl.ANY),
            pl.BlockSpec((tc, TOP_K), lambda b, i: (b * nt + i, 0)),
            pl.BlockSpec((1, tc, d), lambda b, i: (b, i, 0)),
            pl.BlockSpec((1, 1, d), lambda b, i: (b, 0, 0)),
            pl.BlockSpec((1, d), lambda b, i: (0, 0)),
        ],
        out_specs=pl.BlockSpec((1, tc, d), lambda b, i: (b, i, 0)),
        out_shape=jax.ShapeDtypeStruct((bsz, n, d), F32),
        scratch_shapes=[pltpu.VMEM((TOP_K * tc * ROW_SUB, LANES), F32), pltpu.SemaphoreType.DMA(())],
        compiler_params=_cp(("arbitrary", "arbitrary")),
        name="moe_combine",
    )(dest3, ys, gates, xl, g2, fg)


def _moe(fl, topi, gates, rank, cnt, xl, g2, fg, layer, w_in, b_in, w_out, b_out, *, final):
    t = fl.shape[0] // ROW_SUB
    tm = MOE_TM
    counts = cnt[0, :N_EXPERTS].astype(jnp.int32)
    padded = (counts + tm - 1) // tm * tm
    pad_end = jnp.cumsum(padded)
    pad_start = pad_end - padded
    dest = jnp.take(pad_start, topi) + rank
    nb = t * TOP_K // tm + N_EXPERTS
    blk_start = jnp.arange(nb, dtype=jnp.int32) * tm
    blk_exp = jnp.minimum(jnp.sum((pad_end[None, :] <= blk_start[:, None]).astype(jnp.int32), axis=1), N_EXPERTS - 1)
    n_used = (pad_end[-1:] // tm).astype(jnp.int32)
    xs = _dispatch(pad_end, padded, dest, fl, nb * tm, td=512)
    ys = _experts(blk_exp, n_used, xs, layer, w_in, b_in, w_out, b_out)
    return _combine(dest, ys, gates, xl, g2, fg, tc=512, final=final)


def _hy_in_body(x_ref, xp_ref, xn_ref, g_ref, sh_ref, sc_ref, w_ref, b_ref, cw_ref, cb_ref, o_ref, *, nt):
    i = pl.program_id(1)
    d = x_ref.shape[2]

    def normed(xx):
        return (_rms(xx, g_ref[...]) * (1.0 + sc_ref[0]) + sh_ref[0]).astype(BF16)

    h = normed(x_ref[0])
    hh = normed(jnp.concatenate([xp_ref[0], xn_ref[0]], axis=0))
    tm = h.shape[0]
    row = lax.broadcasted_iota(jnp.int32, (tm, 1), 0)
    for j in range(3):
        cols = slice(j * d, (j + 1) * d)
        p = _dot(h, w_ref[:, cols]) + b_ref[:, cols]
        ph = _dot(hh, w_ref[:, cols]) + b_ref[:, cols]
        prev = jnp.where(i > 0, ph[7:8], 0.0)
        nxt = jnp.where(i < nt - 1, ph[8:9], 0.0)
        up = jnp.where(row == 0, prev, pltpu.roll(p, 1, axis=0))
        dn = jnp.where(row == tm - 1, nxt, pltpu.roll(p, tm - 1, axis=0))
        cw = cw_ref[:, cols]
        o_ref[j, 0] = up * cw[0:1] + p * cw[1:2] + dn * cw[2:3] + cb_ref[:, cols]


def _hy_in(x, g, sh, sc, w, b, cw, cb, *, tm):
    bsz, n, d = x.shape
    nt = n // tm
    hb = tm // 8
    per_b = pl.BlockSpec((1, 1, d), lambda bb, i: (bb, 0, 0))
    full = lambda a: pl.BlockSpec(a.shape, lambda bb, i: (0,) * a.ndim)
    return pl.pallas_call(
        functools.partial(_hy_in_body, nt=nt),
        grid=(bsz, nt),
        in_specs=[
            pl.BlockSpec((1, tm, d), lambda bb, i: (bb, i, 0)),
            pl.BlockSpec((1, 8, d), lambda bb, i: (bb, jnp.maximum(i * hb - 1, 0), 0)),
            pl.BlockSpec((1, 8, d), lambda bb, i: (bb, jnp.minimum((i + 1) * hb, n // 8 - 1), 0)),
            full(g), per_b, per_b, full(w), full(b), full(cw), full(cb),
        ],
        out_specs=pl.BlockSpec((3, 1, tm, d), lambda bb, i: (0, bb, i, 0)),
        out_shape=jax.ShapeDtypeStruct((3, bsz, n, d), F32),
        compiler_params=_cp(("arbitrary", "arbitrary")),
        name="hyena_in_proj",
    )(x, x, x, g, sh, sc, w, b, cw, cb)


def _filt_feat_body(w1_ref, b1_ref, f1_ref, w2_ref, b2_ref, f2_ref, o_ref, *, n_lat):
    na = o_ref.shape[1]
    a = lax.broadcasted_iota(jnp.int32, (na, 1), 0)
    lane = lax.broadcasted_iota(jnp.int32, (na, LANES), 1)
    band_idx = jnp.where(lane <= HY_BANDS, lane - 1, lane - 1 - HY_BANDS).astype(F32)
    band = 1e-4 + band_idx * ((HY_BANDS - 1 - 1e-4) / (HY_BANDS - 1))
    for j in range(B_GROUP):
        r = a * FFT_N2 + (pl.program_id(0) * B_GROUP + j)
        pos = jnp.where(r < n_lat, r, 2 * n_lat - r).astype(F32)
        tn = pos / float(max(n_lat - 1, 1))
        ang = ((2.0 * math.pi / n_lat) * pos) * band
        z = jnp.where(lane == 0, tn, jnp.where(lane <= HY_BANDS, jnp.cos(ang),
                                               jnp.where(lane < HY_EMB, -jnp.sin(ang), 0.0)))
        h1 = jnp.sin(f1_ref[...] * (_dot_hi(z, w1_ref[...]) + b1_ref[...]))
        h2 = jnp.sin(f2_ref[...] * (_dot_hi(h1, w2_ref[...]) + b2_ref[...]))
        valid = (r != n_lat).astype(F32)
        o_ref[j] = jnp.where(lane == HY_HID, tn, jnp.where(lane == HY_HID + 1, valid, h2))


def _filt_feat(w1, b1, f1, w2, b2, f2, *, n_lat):
    na = 2 * n_lat // FFT_N2
    w1p = jnp.zeros((LANES, LANES), F32).at[:HY_EMB, :HY_HID].set(w1)
    w2p = jnp.zeros((LANES, LANES), F32).at[:HY_HID, :HY_HID].set(w2)
    padv = lambda v: jnp.zeros((1, LANES), F32).at[0, :HY_HID].set(v)
    full = lambda shp: pl.BlockSpec(shp, lambda i: (0,) * len(shp))
    return pl.pallas_call(
        functools.partial(_filt_feat_body, n_lat=n_lat),
        grid=(FFT_N2 // B_GROUP,),
        in_specs=[full((LANES, LANES)), full((1, LANES)), full((1, LANES)),
                  full((LANES, LANES)), full((1, LANES)), full((1, LANES))],
        out_specs=pl.BlockSpec((B_GROUP, na, LANES), lambda i: (i, 0, 0)),
        out_shape=jax.ShapeDtypeStruct((FFT_N2, na, LANES), F32),
        compiler_params=_cp(("arbitrary",)),
        name="hyena_filter_features",
    )(w1p, padv(b1), padv(f1), w2p, padv(b2), padv(f2))


U32 = jnp.uint32
HI16 = 0xFFFF0000


def _pack_c(re, im):
    lo = lax.bitcast_convert_type(re.astype(BF16).astype(F32), U32) >> 16
    hi = lax.bitcast_convert_type(im.astype(BF16).astype(F32), U32) & U32(HI16)
    return hi | lo


def _unpack_c(u):
    re = lax.bitcast_convert_type(u << 16, F32)
    im = lax.bitcast_convert_type(u & U32(HI16), F32)
    return jnp.concatenate([re, im], axis=0).astype(BF16)


def _interleave_tiles(xa, xb):
    tiles = []
    for c0 in range(0, xa.shape[-1], LANES):
        tiles += [xa[..., c0:c0 + LANES], xb[..., c0:c0 + LANES]]
    return jnp.concatenate(tiles, axis=-1)


def _filt_s1_body(hd_ref, w3_ref, dec_ref, tab_ref, oa_ref, ob_ref):
    na = hd_ref.shape[1]
    ha = na // 2
    oa2, ob2 = _rows2d(oa_ref), _rows2d(ob_ref)
    ft = hd_ref[:, :ha, :].reshape(B_GROUP * ha, LANES)
    fb = hd_ref[:, ha:, :].reshape(B_GROUP * ha, LANES)
    top = _dot(ft.astype(BF16), w3_ref[0, 0].astype(BF16))
    top = top * jnp.exp(-ft[:, HY_HID:HY_HID + 1] * jnp.abs(dec_ref[0, 0]))
    bot = _dot(fb.astype(BF16), w3_ref[0, 1].astype(BF16))
    bot = bot * (jnp.exp(-fb[:, HY_HID:HY_HID + 1] * jnp.abs(dec_ref[0, 1])) * fb[:, HY_HID + 1:HY_HID + 2])
    for j in range(B_GROUP):
        hb = jnp.concatenate([top[j * ha:(j + 1) * ha], bot[j * ha:(j + 1) * ha]], axis=0).astype(BF16)
        r = _dot(tab_ref[j], hb)
        packed = _pack_c(r[:na], r[na:])
        oa2[pl.ds(j, na, stride=B_GROUP), :] = packed[:, :LANES]
        ob2[pl.ds(j, na, stride=B_GROUP), :] = packed[:, LANES:]


def _filt_s1(hd, w3r, dec, tab):
    _, na, _ = hd.shape
    d = w3r.shape[-1]
    ct = 2 * LANES
    half = pl.BlockSpec((None, na, B_GROUP, LANES), lambda o, g, c: (o, 0, g, c))
    return pl.pallas_call(
        _filt_s1_body,
        grid=(2, FFT_N2 // B_GROUP, d // ct),
        in_specs=[
            pl.BlockSpec((B_GROUP, na, LANES), lambda o, g, c: (g, 0, 0)),
            pl.BlockSpec((1, 2, LANES, ct), lambda o, g, c: (o, 0, 0, c)),
            pl.BlockSpec((1, 2, 1, ct), lambda o, g, c: (o, 0, 0, c)),
            pl.BlockSpec((B_GROUP, 2 * na, na), lambda o, g, c: (g, 0, 0)),
        ],
        out_specs=[half, half],
        out_shape=[jax.ShapeDtypeStruct((2, na, FFT_N2, d // 2), U32)] * 2,
        compiler_params=_cp(("arbitrary", "arbitrary", "arbitrary")),
        name="hyena_filter_dft1",
    )(hd, w3r, dec, tab)


S2_KB = 4


def _s2_body(*refs, conv):
    if conv:
        oa_ref, ob_ref, kf_ref, ff_ref, fi_ref, g_ref = refs
    else:
        oa_ref, ob_ref, ff_ref, g_ref = refs
    for u in range(S2_KB):
        xf = _dot(ff_ref[...], _unpack_c(_interleave_tiles(oa_ref[u], ob_ref[u])))
        if conv:
            xr, xi = xf[:FFT_N2], xf[FFT_N2:]
            kr = kf_ref[0, u].astype(F32)
            ki = kf_ref[1, u].astype(F32)
            y = jnp.concatenate([xr * kr - xi * ki, xr * ki + xi * kr], axis=0).astype(BF16)
            xf = _dot(fi_ref[...], y)
            g_ref[u] = _pack_c(xf[:FFT_N2], xf[FFT_N2:])
        else:
            g_ref[0, u] = xf[:FFT_N2].astype(BF16)
            g_ref[1, u] = xf[FFT_N2:].astype(BF16)


def _s2(o3, kf, order, ff, fi, *, ct, conv):
    n1 = o3[0].shape[-3]
    d = 2 * o3[0].shape[-1]
    full = lambda a: pl.BlockSpec(a.shape, lambda k, c: (0,) * a.ndim)
    nk = n1 // S2_KB
    if conv:
        hblk = pl.BlockSpec((S2_KB, FFT_N2, ct // 2), lambda k, c: (k, 0, c))
        in_specs = [hblk, hblk, pl.BlockSpec((None, 2, S2_KB, FFT_N2, ct), lambda k, c: (order, 0, k, 0, c)),
                    full(ff), full(fi)]
        args = (o3[0], o3[1], kf, ff, fi)
        grid = (nk, d // ct)
        out_specs = pl.BlockSpec((S2_KB, FFT_N2, ct), lambda k, c: (k, 0, c))
        out_shape = jax.ShapeDtypeStruct((n1, FFT_N2, d), U32)
    else:
        no = o3[0].shape[0]
        hblk = pl.BlockSpec((None, S2_KB, FFT_N2, ct // 2), lambda k, c: (k // nk, k % nk, 0, c))
        in_specs = [hblk, hblk, full(ff)]
        args = (o3[0], o3[1], ff)
        grid = (no * nk, d // ct)
        out_specs = pl.BlockSpec((None, 2, S2_KB, FFT_N2, ct), lambda k, c: (k // nk, 0, k % nk, 0, c))
        out_shape = jax.ShapeDtypeStruct((no, 2, n1, FFT_N2, d), BF16)
    return pl.pallas_call(
        functools.partial(_s2_body, conv=conv),
        grid=grid,
        in_specs=in_specs,
        out_specs=out_specs,
        out_shape=out_shape,
        compiler_params=_cp(("arbitrary", "arbitrary")),
        name="hyena_conv_dft2" if conv else "hyena_filter_dft2",
    )(*args)


def _rows2d(ref):
    lead = ref.shape[:-3]
    return ref.reshape(lead + (ref.shape[-3] * B_GROUP, ref.shape[-1]))


def _s1_body(za_ref, zb_ref, tab_ref, oa_ref, ob_ref):
    rows, n1 = za_ref.shape[0], oa_ref.shape[0]
    za2, zb2, oa2, ob2 = _rows2d(za_ref), _rows2d(zb_ref), _rows2d(oa_ref), _rows2d(ob_ref)
    for j in range(B_GROUP):
        sl = pl.ds(j, rows, stride=B_GROUP)
        zj = jnp.concatenate([za2[sl, :], zb2[sl, :]], axis=1)
        r = _dot(tab_ref[j], zj.astype(BF16))
        packed = _pack_c(r[:n1], r[n1:])
        oa2[pl.ds(j, n1, stride=B_GROUP), :] = packed[:, :LANES]
        ob2[pl.ds(j, n1, stride=B_GROUP), :] = packed[:, LANES:]


def _s1(z4, zi, tab):
    _, rows, _, d = z4.shape
    n1 = tab.shape[1] // 2
    half = pl.BlockSpec((n1, B_GROUP, LANES), lambda g, c: (0, g, c))
    return pl.pallas_call(
        _s1_body,
        grid=(FFT_N2 // B_GROUP, d // (2 * LANES)),
        in_specs=[
            pl.BlockSpec((None, rows, B_GROUP, LANES), lambda g, c: (zi, 0, g, 2 * c)),
            pl.BlockSpec((None, rows, B_GROUP, LANES), lambda g, c: (zi, 0, g, 2 * c + 1)),
            pl.BlockSpec((B_GROUP, 2 * n1, rows), lambda g, c: (g, 0, 0)),
        ],
        out_specs=[half, half],
        out_shape=[jax.ShapeDtypeStruct((n1, FFT_N2, d // 2), U32)] * 2,
        compiler_params=_cp(("arbitrary", "arbitrary")),
        name="hyena_conv_dft1",
    )(z4, z4, tab)


def _s3_body(ga_ref, gb_ref, tab_ref, gta_ref, gtb_ref, za_ref, zb_ref, fb_ref, *rest, chain):
    if chain:
        tab1_ref, oa_ref, ob_ref, qa_ref, qb_ref = rest
    else:
        oa_ref, ob_ref = rest
    n1, rows = ga_ref.shape[0], oa_ref.shape[0]
    ga2, gb2, oa2, ob2 = _rows2d(ga_ref), _rows2d(gb_ref), _rows2d(oa_ref), _rows2d(ob_ref)
    for j in range(B_GROUP):
        sl = pl.ds(j, n1, stride=B_GROUP)
        gj = _unpack_c(jnp.concatenate([ga2[sl, :], gb2[sl, :]], axis=1))
        y = _dot(tab_ref[j], gj)
        oa2[pl.ds(j, rows, stride=B_GROUP), :] = y[:, :LANES]
        ob2[pl.ds(j, rows, stride=B_GROUP), :] = y[:, LANES:]
    fb = fb_ref[...]
    oa_ref[...] = gta_ref[...] * (oa_ref[...] + za_ref[...] * fb[:, :, :LANES])
    ob_ref[...] = gtb_ref[...] * (ob_ref[...] + zb_ref[...] * fb[:, :, LANES:])
    if chain:
        qa2, qb2 = _rows2d(qa_ref), _rows2d(qb_ref)
        for j in range(B_GROUP):
            sl = pl.ds(j, rows, stride=B_GROUP)
            zj = jnp.concatenate([oa2[sl, :], ob2[sl, :]], axis=1)
            r = _dot(tab1_ref[j], zj.astype(BF16))
            packed = _pack_c(r[:n1], r[n1:])
            qa2[pl.ds(j, n1, stride=B_GROUP), :] = packed[:, :LANES]
            qb2[pl.ds(j, n1, stride=B_GROUP), :] = packed[:, LANES:]


def _s3(g3, tab, gate4, gi, zsrc, fb, tab1):
    n1, _, d = g3.shape
    rows = tab.shape[1]
    chain = tab1 is not None
    half = pl.BlockSpec((rows, B_GROUP, LANES), lambda g, c: (0, g, c))
    chalf = pl.BlockSpec((n1, B_GROUP, LANES), lambda g, c: (0, g, c))
    nat = lambda idx, par: pl.BlockSpec((None, rows, B_GROUP, LANES), lambda g, c: (idx, 0, g, 2 * c + par))
    if isinstance(zsrc[1], int):
        z_specs, z_args = [nat(zsrc[1], 0), nat(zsrc[1], 1)], [zsrc[0], zsrc[0]]
    else:
        z_specs, z_args = [half, half], list(zsrc)
    in_specs = [
        pl.BlockSpec((n1, B_GROUP, LANES), lambda g, c: (0, g, 2 * c)),
        pl.BlockSpec((n1, B_GROUP, LANES), lambda g, c: (0, g, 2 * c + 1)),
        pl.BlockSpec((B_GROUP, rows, 2 * n1), lambda g, c: (g, 0, 0)),
        nat(gi, 0), nat(gi, 1), *z_specs,
        pl.BlockSpec((1, 1, 2 * LANES), lambda g, c: (0, 0, c)),
    ]
    args = [g3, g3, tab, gate4, gate4, *z_args, fb]
    zshape = jax.ShapeDtypeStruct((rows, FFT_N2, d // 2), F32)
    out_specs, out_shape = [half, half], [zshape, zshape]
    if chain:
        in_specs.append(pl.BlockSpec((B_GROUP, 2 * n1, rows), lambda g, c: (g, 0, 0)))
        args.append(tab1)
        out_specs += [chalf, chalf]
        out_shape += [jax.ShapeDtypeStruct((n1, FFT_N2, d // 2), U32)] * 2
    return pl.pallas_call(
        functools.partial(_s3_body, chain=chain),
        grid=(FFT_N2 // B_GROUP, d // (2 * LANES)),
        in_specs=in_specs,
        out_specs=out_specs,
        out_shape=out_shape,
        compiler_params=_cp(("arbitrary", "arbitrary")),
        name="hyena_conv_idft1_dft1" if chain else "hyena_conv_idft1",
    )(*args)


def _dft_tables(n_lat):
    n = 2 * n_lat
    n1 = n // FFT_N2
    k1 = jnp.arange(n1, dtype=jnp.int32)
    th_a = ((k1[:, None] * k1[None, :]) % n1).astype(F32) * (2.0 * math.pi / n1)
    th_b = (jnp.arange(FFT_N2, dtype=jnp.int32)[:, None] * k1[None, :]).astype(F32) * (2.0 * math.pi / n)
    ca, sa = jnp.cos(th_a)[None], jnp.sin(th_a)[None]
    cb, sb = jnp.cos(th_b)[:, :, None], jnp.sin(th_b)[:, :, None]
    cr = ca * cb - sa * sb
    sn = sa * cb + ca * sb
    ha = n1 // 2
    crh, snh = cr[:, :, :ha], sn[:, :, :ha]
    w1 = jnp.concatenate([jnp.concatenate([crh, snh], axis=2), jnp.concatenate([-snh, crh], axis=2)], axis=1)
    w1f = jnp.concatenate([cr, -sn], axis=1)
    v = jnp.swapaxes(w1, 1, 2) * (1.0 / n)
    k2 = jnp.arange(FFT_N2, dtype=jnp.int32)
    th2 = ((k2[:, None] * k2[None, :]) % FFT_N2).astype(F32) * (2.0 * math.pi / FFT_N2)
    c2, s2 = jnp.cos(th2), jnp.sin(th2)
    ff = jnp.concatenate([jnp.concatenate([c2, s2], axis=1), jnp.concatenate([-s2, c2], axis=1)], axis=0)
    fi = jnp.concatenate([jnp.concatenate([c2, -s2], axis=1), jnp.concatenate([s2, c2], axis=1)], axis=0)
    return w1.astype(BF16), w1f.astype(BF16), v.astype(BF16), ff.astype(BF16), fi.astype(BF16)


def _hyena_mix(proj3, fparams, fbias, *, n_lat):
    _, bsz, _, d = proj3.shape
    f_w1, f_b1, f_f1, f_w2, f_b2, f_f2, f_w3, decay = fparams
    na = 2 * n_lat // FFT_N2
    w1, w1f, v, ff, fi = _dft_tables(n_lat)
    hd = _filt_feat(f_w1, f_b1, f_f1, f_w2, f_b2, f_f2, n_lat=n_lat)
    w3r = jnp.transpose(f_w3.reshape(HY_HID, 2, 2, d), (1, 2, 0, 3))
    w3r = jnp.zeros((2, 2, LANES, d), F32).at[:, :, :HY_HID].set(w3r)
    kf1 = _filt_s1(hd, w3r, decay.reshape(2, 2, 1, d), w1f)
    kf = _s2(kf1, None, 0, ff, None, ct=d, conv=False)
    p3 = proj3.reshape(3, bsz * (n_lat // FFT_N2), FFT_N2, d)
    o1 = _s1(p3, 2, w1)
    g = _s2(o1, kf, 0, ff, fi, ct=d, conv=True)
    za, zb, *o1 = _s3(g, v, p3, 0, (p3, 2), fbias[0].reshape(1, 1, d), w1)
    g = _s2(o1, kf, 1, ff, fi, ct=d, conv=True)
    za, zb = _s3(g, v, p3, 1, (za, zb), fbias[1].reshape(1, 1, d), None)
    return za.reshape(bsz, n_lat, d // 2), zb.reshape(bsz, n_lat, d // 2)


def _rope_tables(n_tokens):
    rows = n_tokens // GRID_W
    row = jnp.broadcast_to(jnp.arange(rows, dtype=F32)[:, None], (rows, GRID_W)).reshape(-1)
    col = jnp.broadcast_to(jnp.arange(GRID_W, dtype=F32)[None, :], (rows, GRID_W)).reshape(-1)
    axis_dim = QK_ROPE // 2
    inv_freq = 1.0 / (ROPE_THETA ** (jnp.arange(0, axis_dim, 2, dtype=F32) / axis_dim))
    ang = jnp.concatenate([row[:, None] * inv_freq, col[:, None] * inv_freq], axis=-1)
    return jnp.cos(ang), jnp.sin(ang)


def _mla_weights(w_down, g_q, w_uq, g_kv, w_ukv):
    d = w_down.shape[0]
    nh = MLA_HEADS
    kpe = w_down[:, Q_LORA + KV_LORA:]
    w1, w2 = kpe[:, 0::2], kpe[:, 1::2]
    z = jnp.zeros((d, LANES - QK_ROPE), w_down.dtype)
    wd = jnp.concatenate([w_down[:, :Q_LORA + KV_LORA], w1, w2, z, w2, w1, z], axis=1).astype(BF16)
    uq = w_uq.reshape(Q_LORA, nh, QK_NOPE + QK_ROPE)
    pe = uq[:, :, QK_NOPE:]
    uq = jnp.concatenate([uq[:, :, :QK_NOPE], pe[:, :, 0::2], pe[:, :, 1::2]], axis=2)
    wuqT = uq.reshape(Q_LORA, nh * (QK_NOPE + QK_ROPE)).T.astype(BF16)
    ukv = w_ukv.reshape(KV_LORA, nh, QK_NOPE + V_DIM)
    wuk = ukv[:, :, :QK_NOPE].reshape(KV_LORA, nh * QK_NOPE).astype(BF16)
    wuvT = ukv[:, :, QK_NOPE:].reshape(KV_LORA, nh * V_DIM).T.astype(BF16)
    return wd, g_q.reshape(1, -1), g_kv.reshape(1, -1), wuk, wuqT, wuvT


def kernel(x, c, ctx, c_ctx, ada_w, ada_b, norm_mix_g, norm_ffn_g, mla_w_down, mla_g_q, mla_w_uq, mla_g_kv, mla_w_ukv, mla_w_o, hy_w_in, hy_b_in, hy_conv_w, hy_conv_b, hy_f_w1, hy_f_b1, hy_f_freq1, hy_f_w2, hy_f_b2, hy_f_freq2, hy_f_w3, hy_decay, hy_bias, hy_w_out, hy_b_out, moe_w_r, moe_b_r, moe_w_in, moe_b_in, moe_w_out, moe_b_out, final_g):
    bsz, n_lat, d = x.shape
    n_ctx = ctx.shape[1]
    depth = ada_w.shape[0]
    assert bsz == 2 and d == MLA_HEADS * V_DIM and n_lat % 512 == 0 and n_ctx % 128 == 0
    assert depth == 2

    cond8 = jnp.zeros((8, d), F32).at[:bsz].set(c).at[bsz].set(c_ctx)
    mods = _ada(cond8, ada_w, ada_b)

    def mod(i, j, rows):
        return mods[i, rows, j * d:(j + 1) * d][:, None, :]

    lat_rows = slice(0, bsz)
    ctx_rows = slice(bsz, bsz + 1)
    xl = x
    for i in range(depth):
        kind, j = i % 2, i // 2
        sh1, sc1, g1 = (mod(i, m, lat_rows) for m in range(3))
        sh2, sc2, g2 = (mod(i, m, lat_rows) for m in range(3, 6))
        gm = norm_mix_g[i].reshape(1, d)
        if kind == 0:
            wts = _mla_weights(mla_w_down[j], mla_g_q[j], mla_w_uq[j], mla_g_kv[j], mla_w_ukv[j])
            cos, sin = _rope_tables(n_lat)
            zl = jnp.zeros((n_lat, LANES - QK_ROPE), F32)
            tabs = (jnp.concatenate([cos, cos, zl], axis=1), jnp.concatenate([-sin, sin, zl], axis=1), cos.T, sin.T)
            tq = tv = 512
            tk = 2048 if n_lat % 4096 == 0 else 512
            qT, k, vT = _mla_proj(xl, gm, sh1, sc1, wts, tabs, need_q=True, tm=tv, tk=tv)
            half = QK_ROPE // 2
            one_c = jnp.concatenate([jnp.ones((n_ctx, QK_ROPE), F32), jnp.zeros((n_ctx, LANES - QK_ROPE), F32)], axis=1)
            tabs_c = (one_c, jnp.zeros((n_ctx, LANES), F32), jnp.ones((half, n_ctx), F32), jnp.zeros((half, n_ctx), F32))
            kc, vTc = _mla_proj(ctx, gm, mod(i, 0, ctx_rows), mod(i, 1, ctx_rows), wts, tabs_c,
                                need_q=False, tm=n_ctx, tk=n_ctx)
            o = _attention(qT, k, vT, kc, vTc, tq=tq, tk=tk)
            wo = mla_w_o[j].astype(BF16)
            bo = jnp.zeros((1, d), F32)
            transposed = True
        else:
            proj3 = _hy_in(xl, gm, sh1, sc1, hy_w_in[j].astype(BF16), hy_b_in[j].reshape(1, -1), hy_conv_w[j],
                           hy_conv_b[j].reshape(1, -1), tm=512)
            fparams = (hy_f_w1[j], hy_f_b1[j], hy_f_freq1[j], hy_f_w2[j], hy_f_b2[j], hy_f_freq2[j], hy_f_w3[j],
                       hy_decay[j])
            o = _hyena_mix(proj3, fparams, hy_bias[j], n_lat=n_lat)
            wo = hy_w_out[j].astype(BF16)
            bo = hy_b_out[j].reshape(1, d)
            transposed = False
        wr = jnp.zeros((d, LANES), F32).at[:, :N_EXPERTS].set(moe_w_r[i])
        wrh = wr.astype(BF16)
        wrl = (wr - wrh.astype(F32)).astype(BF16)
        br = jnp.zeros((1, LANES), F32).at[0, :N_EXPERTS].set(moe_b_r[i])
        xl, fl, topi, gates, rank, cnt = _post(o, wo, bo, xl, g1, norm_ffn_g[i].reshape(1, d), sh2, sc2, wrh, wrl, br,
                                               transposed=transposed, tm=512)
        xl = _moe(fl, topi, gates, rank, cnt, xl, g2, final_g.reshape(1, d), i, moe_w_in, moe_b_in,
                  moe_w_out, moe_b_out, final=(i == depth - 1))
    return xl
```

```python
import functools
import math

import jax
import jax.numpy as jnp
from jax import lax
from jax.experimental import pallas as pl
from jax.experimental.pallas import tpu as pltpu

F32 = jnp.float32
BF16 = jnp.bfloat16

EPS = 1e-6
GRID_W = 64
MLA_HEADS = 8
QK_NOPE = 128
QK_ROPE = 64
V_DIM = 128
Q_LORA = 512
KV_LORA = 256
ROPE_THETA = 10000.0
MLA_SCALE = (QK_NOPE + QK_ROPE) ** -0.5
QK_PAD = 256

HY_EMB = 33
HY_BANDS = (HY_EMB - 1) // 2
HY_HID = 64
FFT_N2 = 128
B_GROUP = 8

N_EXPERTS = 32
TOP_K = 4
SWIGLU_LIMIT = 7.0
SWIGLU_ALPHA = 1.702
MOE_TM = 256
LANES = 128

VMEM_LIMIT = 56 * 1024 * 1024


def _cp(sem, vmem=VMEM_LIMIT):
    return pltpu.CompilerParams(dimension_semantics=sem, vmem_limit_bytes=vmem)


def _dot(a, b):
    return jnp.dot(a, b, preferred_element_type=F32)


def _dot_hi(a, b):
    return jnp.dot(a, b, preferred_element_type=F32, precision=lax.Precision.HIGHEST)


def _rms(x, g):
    return x * lax.rsqrt(jnp.mean(x * x, axis=-1, keepdims=True) + EPS) * g


def _ada_body(c_ref, w_ref, b_ref, o_ref):
    c = c_ref[...]
    s = c * jax.nn.sigmoid(c)
    o_ref[0] = _dot(s.astype(BF16), w_ref[0].astype(BF16)) + b_ref[0]


def _ada(cond8, ada_w, ada_b):
    depth, d, n = ada_w.shape
    tn = n // 4
    return pl.pallas_call(
        _ada_body,
        grid=(depth, n // tn),
        in_specs=[
            pl.BlockSpec((8, d), lambda i, j: (0, 0)),
            pl.BlockSpec((1, d, tn), lambda i, j: (i, 0, j)),
            pl.BlockSpec((1, 1, tn), lambda i, j: (i, 0, j)),
        ],
        out_specs=pl.BlockSpec((1, 8, tn), lambda i, j: (i, 0, j)),
        out_shape=jax.ShapeDtypeStruct((depth, 8, n), F32),
        compiler_params=_cp(("arbitrary", "arbitrary")),
        name="ada_mod",
    )(cond8, ada_w, ada_b.reshape(depth, 1, n))


def _mla_proj_body(x_ref, g_ref, sh_ref, sc_ref, wd_ref, gq_ref, gkv_ref, wuk_ref, wuqT_ref, wuvT_ref,
                   ct_ref, st_ref, cT_ref, sT_ref, *out_refs, need_q, tk):
    if need_q:
        qT_ref, k_ref, vT_ref = out_refs
    else:
        k_ref, vT_ref = out_refs
    nh = MLA_HEADS
    x = x_ref[0]
    h = _rms(x, g_ref[...]) * (1.0 + sc_ref[0]) + sh_ref[0]
    lat = _dot(h.astype(BF16), wd_ref[...])
    o_kv = Q_LORA
    o_a = Q_LORA + KV_LORA
    kvn = _rms(lat[:, o_kv:o_a], gkv_ref[...])
    kr = (lat[:, o_a:o_a + LANES] * ct_ref[...] + lat[:, o_a + LANES:o_a + 2 * LANES] * st_ref[...]).astype(BF16)
    knope = _dot(kvn.astype(BF16), wuk_ref[...])
    for hh in range(nh):
        k_ref[0, hh, :, 0:QK_NOPE] = knope[:, hh * QK_NOPE:(hh + 1) * QK_NOPE].astype(BF16)
        k_ref[0, hh, :, QK_NOPE:QK_PAD] = kr
    vT = _dot(wuvT_ref[...], kvn.T.astype(BF16))
    tm = x.shape[0]
    for hh in range(nh):
        for c in range(tm // tk):
            vT_ref[0, hh, c] = vT[hh * V_DIM:(hh + 1) * V_DIM, c * tk:(c + 1) * tk].astype(BF16)
    if need_q:
        qn = _rms(lat[:, :Q_LORA], gq_ref[...])
        qT = _dot(wuqT_ref[...], qn.T.astype(BF16)) * (MLA_SCALE * math.log2(math.e))
        c = cT_ref[...]
        s = sT_ref[...]
        hw = QK_NOPE + QK_ROPE
        half = QK_ROPE // 2
        for hh in range(nh):
            base = hh * hw
            x1 = qT[base + QK_NOPE:base + QK_NOPE + half]
            x2 = qT[base + QK_NOPE + half:base + hw]
            qT_ref[0, hh, 0:QK_NOPE] = qT[base:base + QK_NOPE].astype(BF16)
            qT_ref[0, hh, QK_NOPE:QK_NOPE + half] = (x1 * c - x2 * s).astype(BF16)
            qT_ref[0, hh, QK_NOPE + half:hw] = (x1 * s + x2 * c).astype(BF16)
            qT_ref[0, hh, hw:QK_PAD] = jnp.zeros((QK_PAD - hw, tm), BF16)


def _mla_proj(x, g, sh, sc, wts, tabs, *, need_q, tm, tk):
    bsz, n, d = x.shape
    nh = MLA_HEADS
    wd, gq, gkv, wuk, wuqT, wuvT = wts
    ct, st, cT, sT = tabs
    nsh = sh.shape[0]
    full = lambda a: pl.BlockSpec(a.shape, lambda b, i: (0,) * a.ndim)
    in_specs = [
        pl.BlockSpec((1, tm, d), lambda b, i: (b, i, 0)),
        full(g),
        pl.BlockSpec((1, 1, d), lambda b, i: (b % nsh, 0, 0)),
        pl.BlockSpec((1, 1, d), lambda b, i: (b % nsh, 0, 0)),
        full(wd), full(gq), full(gkv), full(wuk), full(wuqT), full(wuvT),
        pl.BlockSpec((tm, LANES), lambda b, i: (i, 0)),
        pl.BlockSpec((tm, LANES), lambda b, i: (i, 0)),
        pl.BlockSpec((QK_ROPE // 2, tm), lambda b, i: (0, i)),
        pl.BlockSpec((QK_ROPE // 2, tm), lambda b, i: (0, i)),
    ]
    out_specs = [
        pl.BlockSpec((1, nh, tm, QK_PAD), lambda b, i: (b, 0, i, 0)),
        pl.BlockSpec((1, nh, tm // tk, V_DIM, tk), lambda b, i: (b, 0, i, 0, 0)),
    ]
    out_shape = [
        jax.ShapeDtypeStruct((bsz, nh, n, QK_PAD), BF16),
        jax.ShapeDtypeStruct((bsz, nh, n // tk, V_DIM, tk), BF16),
    ]
    if need_q:
        out_specs = [pl.BlockSpec((1, nh, QK_PAD, tm), lambda b, i: (b, 0, 0, i))] + out_specs
        out_shape = [jax.ShapeDtypeStruct((bsz, nh, QK_PAD, n), BF16)] + out_shape
    return pl.pallas_call(
        functools.partial(_mla_proj_body, need_q=need_q, tk=tk),
        grid=(bsz, n // tm),
        in_specs=in_specs,
        out_specs=out_specs,
        out_shape=out_shape,
        compiler_params=_cp(("arbitrary", "arbitrary")),
        name="mla_proj_q" if need_q else "mla_proj_ctx",
    )(x, g, sh, sc, wd, gq, gkv, wuk, wuqT, wuvT, ct, st, cT, sT)


SM_STRIP = 64
SUBLANES = 8


def _attn_body(qT_ref, k_ref, vT_ref, kc_ref, vTc_ref, o_ref, s0, s1, p0, p1, sc, pc, acc, m_scr, x0, x1, xc,
               a0, a1, ac, l_scr, d0, d1, dc, *, tk):
    nchunk = k_ref.shape[2] // tk

    def scores(kblk, s_ref, mx_ref):
        r = _dot(kblk, qT_ref[0, 0])
        s_ref[...] = r
        mx_ref[...] = jnp.max(r, axis=0, keepdims=True)

    def probs(s_ref, mx_ref, p_ref, a_ref, d_ref):
        m_old = m_scr[...]
        m_new = jnp.maximum(m_old, mx_ref[...])
        m_scr[...] = m_new
        alpha = jnp.exp2(m_old - m_new)
        a_ref[...] = alpha
        part = None
        for r in range(0, s_ref.shape[0], SM_STRIP):
            p = jnp.exp2(s_ref[r:r + SM_STRIP] - m_new)
            p_ref[r:r + SM_STRIP] = p.astype(BF16)
            ps = jnp.sum(p.reshape(SM_STRIP // SUBLANES, SUBLANES, p.shape[1]), axis=0)
            part = ps if part is None else part + ps
        d_ref[...] = part

    def accumulate(p_ref, a_ref, d_ref, vblk):
        acc[...] = a_ref[...] * acc[...] + _dot(vblk, p_ref[...])
        l_scr[...] = a_ref[...] * l_scr[...] + d_ref[...]

    def kchunk(i):
        return k_ref[0, 0, pl.ds(pl.multiple_of(i * tk, tk), tk), :]

    def vchunk(i):
        nsub = tk // vT_ref.shape[-1]
        return jnp.concatenate([vT_ref[0, 0, i * nsub + u] for u in range(nsub)], axis=1)

    m_scr[...] = jnp.full(m_scr.shape, -jnp.inf, F32)
    acc[...] = jnp.zeros(acc.shape, F32)
    l_scr[...] = jnp.zeros(l_scr.shape, F32)
    scores(kc_ref[0, 0], sc, xc)
    scores(kchunk(0), s0, x0)
    probs(sc, xc, pc, ac, dc)
    scores(kchunk(1), s1, x1)
    accumulate(pc, ac, dc, vTc_ref[0, 0, 0])
    probs(s0, x0, p0, a0, d0)

    def body(j, carry):
        t = 2 * j
        scores(kchunk(t + 2), s0, x0)
        accumulate(p0, a0, d0, vchunk(t))
        probs(s1, x1, p1, a1, d1)
        scores(kchunk(t + 3), s1, x1)
        accumulate(p1, a1, d1, vchunk(t + 1))
        probs(s0, x0, p0, a0, d0)
        return carry

    lax.fori_loop(0, nchunk // 2 - 1, body, 0)
    accumulate(p0, a0, d0, vchunk(nchunk - 2))
    probs(s1, x1, p1, a1, d1)
    accumulate(p1, a1, d1, vchunk(nchunk - 1))
    o_ref[0, 0] = (acc[...] / jnp.sum(l_scr[...], axis=0, keepdims=True)).astype(BF16)


def _attention(qT, k, vT, kc, vTc, *, tq, tk):
    bsz, nh, _, n = qT.shape
    nc = kc.shape[2]
    tv = vT.shape[-1]
    assert (n // tk) % 2 == 0 and tk % tv == 0
    return pl.pallas_call(
        functools.partial(_attn_body, tk=tk),
        grid=(bsz, nh, n // tq),
        in_specs=[
            pl.BlockSpec((1, 1, QK_PAD, tq), lambda b, h, i: (b, h, 0, i)),
            pl.BlockSpec((1, 1, n, QK_PAD), lambda b, h, i: (b, h, 0, 0)),
            pl.BlockSpec((1, 1, n // tv, V_DIM, tv), lambda b, h, i: (b, h, 0, 0, 0)),
            pl.BlockSpec((1, 1, nc, QK_PAD), lambda b, h, i: (b, h, 0, 0)),
            pl.BlockSpec((1, 1, 1, V_DIM, nc), lambda b, h, i: (b, h, 0, 0, 0)),
        ],
        out_specs=pl.BlockSpec((1, 1, V_DIM, tq), lambda b, h, i: (b, h, 0, i)),
        out_shape=jax.ShapeDtypeStruct((bsz, nh, V_DIM, n), BF16),
        scratch_shapes=[pltpu.VMEM((tk, tq), F32), pltpu.VMEM((tk, tq), F32),
                        pltpu.VMEM((tk, tq), BF16), pltpu.VMEM((tk, tq), BF16),
                        pltpu.VMEM((nc, tq), F32), pltpu.VMEM((nc, tq), BF16),
                        pltpu.VMEM((V_DIM, tq), F32), pltpu.VMEM((1, tq), F32),
                        pltpu.VMEM((1, tq), F32), pltpu.VMEM((1, tq), F32), pltpu.VMEM((1, tq), F32),
                        pltpu.VMEM((1, tq), F32), pltpu.VMEM((1, tq), F32), pltpu.VMEM((1, tq), F32),
                        pltpu.VMEM((SUBLANES, tq), F32), pltpu.VMEM((SUBLANES, tq), F32),
                        pltpu.VMEM((SUBLANES, tq), F32), pltpu.VMEM((SUBLANES, tq), F32)],
        compiler_params=_cp(("arbitrary", "arbitrary", "arbitrary")),
        name="mla_attention",
    )(qT, k, vT, kc, vTc)


def _post_body(*refs, transposed):
    if transposed:
        o_ref, *refs = refs
    else:
        o_ref, ob_ref, *refs = refs
    (wo_ref, bo_ref, x_ref, g1_ref, gf_ref, sh_ref, sc_ref, wrh_ref, wrl_ref, br_ref, tri_ref,
     xl_ref, fl_ref, ti_ref, gt_ref, rk_ref, cnt_ref) = refs

    @pl.when((pl.program_id(0) == 0) & (pl.program_id(1) == 0))
    def _():
        cnt_ref[...] = jnp.zeros_like(cnt_ref)

    tm = x_ref.shape[1]
    if transposed:
        oT = o_ref[0].astype(F32).reshape(MLA_HEADS * V_DIM, tm)
        o = oT.T.astype(BF16)
    else:
        o = _interleave_tiles(o_ref[0], ob_ref[0]).astype(BF16)
    y = _dot(o, wo_ref[...]) + bo_ref[...]
    xl = x_ref[0] + g1_ref[0] * y
    xl_ref[0] = xl
    fl = _rms(xl, gf_ref[...]) * (1.0 + sc_ref[0]) + sh_ref[0]
    _to_rows(fl_ref, fl)
    flh = fl.astype(BF16)
    fll = (fl - flh.astype(F32)).astype(BF16)
    logits = _dot(flh, wrh_ref[...]) + (_dot(fll, wrh_ref[...]) + _dot(flh, wrl_ref[...])) + br_ref[...]
    lane = lax.broadcasted_iota(jnp.int32, (tm, LANES), 1).astype(F32)
    neg = jnp.float32(-jnp.inf)
    work = jnp.where(lane < N_EXPERTS, logits, neg)
    vals, idxs = [], []
    onehot = jnp.zeros((tm, LANES), F32)
    for _ in range(TOP_K):
        mk = jnp.max(work, axis=-1, keepdims=True)
        ik = jnp.min(jnp.where(work == mk, lane, float(LANES)), axis=-1, keepdims=True)
        sel = lane == ik
        onehot = jnp.where(sel, 1.0, onehot)
        work = jnp.where(sel, neg, work)
        vals.append(mk)
        idxs.append(ik)
    es = [jnp.exp(v - vals[0]) for v in vals]
    den = es[0] + es[1] + es[2] + es[3]
    pre = _dot(tri_ref[...], onehot.astype(BF16)) + cnt_ref[...]
    ti = jnp.zeros((tm, LANES), F32)
    gt = jnp.zeros((tm, LANES), F32)
    rk = jnp.zeros((tm, LANES), F32)
    for kk in range(TOP_K):
        rank = jnp.sum(jnp.where(lane == idxs[kk], pre, 0.0), axis=-1, keepdims=True)
        ti = jnp.where(lane == kk, idxs[kk], ti)
        gt = jnp.where(lane == kk, es[kk] / den, gt)
        rk = jnp.where(lane == kk, rank, rk)
    ti_ref[...] = ti.T[:SUBLANES].astype(jnp.int32)
    rk_ref[...] = rk.T[:SUBLANES].astype(jnp.int32)
    gt_ref[...] = gt[:, :TOP_K]
    cnt_ref[...] += jnp.sum(onehot, axis=0, keepdims=True)


def _post(o, wo, bo, x, g1, gf, sh, sc, wrh, wrl, br, *, transposed, tm):
    bsz, n, d = x.shape
    t = bsz * n
    nt = n // tm
    tri = (lax.broadcasted_iota(jnp.int32, (tm, tm), 0) > lax.broadcasted_iota(jnp.int32, (tm, tm), 1)).astype(BF16)
    full = lambda a: pl.BlockSpec(a.shape, lambda b, i: (0,) * a.ndim)
    per_b = pl.BlockSpec((1, 1, d), lambda b, i: (b, 0, 0))
    if transposed:
        o_args = [o]
        o_specs = [pl.BlockSpec((1, MLA_HEADS, V_DIM, tm), lambda b, i: (b, 0, 0, i))]
    else:
        o_args = list(o)
        o_specs = [pl.BlockSpec((1, tm, d // 2), lambda b, i: (b, i, 0))] * 2
    tok = lambda w: pl.BlockSpec((tm, w), lambda b, i: (b * nt + i, 0))
    kmaj = pl.BlockSpec((SUBLANES, tm), lambda b, i: (0, b * nt + i))
    return pl.pallas_call(
        functools.partial(_post_body, transposed=transposed),
        grid=(bsz, nt),
        in_specs=[*o_specs, full(wo), full(bo), pl.BlockSpec((1, tm, d), lambda b, i: (b, i, 0)), per_b, full(gf),
                  per_b, per_b, full(wrh), full(wrl), full(br), full(tri)],
        out_specs=[pl.BlockSpec((1, tm, d), lambda b, i: (b, i, 0)),
                   pl.BlockSpec((tm * ROW_SUB, LANES), lambda b, i: (b * nt + i, 0)),
                   kmaj, tok(TOP_K), kmaj, pl.BlockSpec((1, LANES), lambda b, i: (0, 0))],
        out_shape=[jax.ShapeDtypeStruct((bsz, n, d), F32), jax.ShapeDtypeStruct((t * ROW_SUB, LANES), F32),
                   jax.ShapeDtypeStruct((SUBLANES, t), jnp.int32), jax.ShapeDtypeStruct((t, TOP_K), F32),
                   jax.ShapeDtypeStruct((SUBLANES, t), jnp.int32), jax.ShapeDtypeStruct((1, LANES), F32)],
        compiler_params=_cp(("arbitrary", "arbitrary")),
        name="post_attn" if transposed else "post_hyena",
    )(*o_args, wo, bo, x, g1, gf, sh, sc, wrh, wrl, br, tri)


ROW_SUB = 8


def _row_slice(i):
    return pl.ds(pl.multiple_of(i * ROW_SUB, ROW_SUB), ROW_SUB)


def _to_rows(ref, x):
    for s in range(ROW_SUB):
        ref[pl.ds(s, x.shape[0], stride=ROW_SUB), :] = x[:, s * LANES:(s + 1) * LANES]


def _from_rows(ref, lo, hi):
    return jnp.concatenate([ref[pl.ds(lo * ROW_SUB + s, hi - lo, stride=ROW_SUB), :] for s in range(ROW_SUB)], axis=1)


def _dest_blocks(dest, tile):
    return jnp.swapaxes(dest.reshape(dest.shape[0], -1, tile), 0, 1)


def _dispatch_body(pe_ref, pd_ref, dest_ref, fl_ref, xs_out, zbuf, sem, *, td):
    @pl.when(pl.program_id(0) == 0)
    def _():
        zbuf[...] = jnp.zeros(zbuf.shape, zbuf.dtype)
        for e in range(N_EXPERTS):
            @pl.when(pd_ref[e] > 0)
            def _():
                start = pl.multiple_of((pe_ref[e] - MOE_TM) * ROW_SUB, ROW_SUB)
                cp = pltpu.make_async_copy(zbuf, xs_out.at[pl.ds(start, MOE_TM * ROW_SUB)], sem)
                cp.start()
                cp.wait()

    def issue(t, carry):
        for kk in range(TOP_K):
            d = dest_ref[0, kk, t]
            pltpu.make_async_copy(fl_ref.at[_row_slice(t)], xs_out.at[_row_slice(d)], sem).start(priority=kk % 2)
        return carry

    lax.fori_loop(0, td, issue, 0, unroll=2)

    def drain(t, carry):
        pltpu.make_async_copy(fl_ref.at[_row_slice(0)], xs_out.at[_row_slice(0)], sem).wait()
        return carry

    lax.fori_loop(0, td * TOP_K, drain, 0, unroll=8)


def _dispatch(pad_end, padded, dest, fl, n_rows, *, td):
    t = fl.shape[0] // ROW_SUB
    dest3 = _dest_blocks(dest, td)
    grid_spec = pltpu.PrefetchScalarGridSpec(
        num_scalar_prefetch=2,
        grid=(t // td,),
        in_specs=[
            pl.BlockSpec((1, SUBLANES, td), lambda i, pe, pd: (i, 0, 0), memory_space=pltpu.SMEM),
            pl.BlockSpec((td * ROW_SUB, LANES), lambda i, pe, pd: (i, 0)),
        ],
        out_specs=pl.BlockSpec(memory_space=pl.ANY),
        scratch_shapes=[pltpu.VMEM((MOE_TM * ROW_SUB, LANES), fl.dtype), pltpu.SemaphoreType.DMA(())],
    )
    return pl.pallas_call(
        functools.partial(_dispatch_body, td=td),
        grid_spec=grid_spec,
        out_shape=jax.ShapeDtypeStruct((n_rows * ROW_SUB, LANES), fl.dtype),
        compiler_params=_cp(("arbitrary",)),
        name="moe_dispatch",
    )(pad_end, padded, dest3, fl)


def _expert_body(be_ref, nu_ref, xs_ref, win_ref, bin_ref, wout_ref, bout_ref, ys_ref, win_s, wout_s):
    b = pl.program_id(0)
    dff = wout_ref.shape[1]

    @pl.when(b < nu_ref[0])
    def _():
        prev = be_ref[jnp.maximum(b - 1, 0)]

        @pl.when((b == 0) | (prev != be_ref[b]))
        def _():
            win_s[...] = win_ref[0].astype(BF16)
            wout_s[...] = wout_ref[0].astype(BF16)

        x = _from_rows(xs_ref, 0, xs_ref.shape[0] // ROW_SUB).astype(BF16)
        gu = _dot(x, win_s[...]) + bin_ref[0]
        gate = jnp.minimum(gu[:, :dff], SWIGLU_LIMIT)
        lin = jnp.clip(gu[:, dff:], -SWIGLU_LIMIT, SWIGLU_LIMIT)
        act = gate * jax.nn.sigmoid(SWIGLU_ALPHA * gate) * (lin + 1.0)
        _to_rows(ys_ref, _dot(act.astype(BF16), wout_s[...]) + bout_ref[0])

    @pl.when(b >= nu_ref[0])
    def _():
        ys_ref[...] = jnp.zeros_like(ys_ref)


def _experts(blk_exp, n_used, xs, layer, w_in, b_in, w_out, b_out):
    n_rows = xs.shape[0] // ROW_SUB
    depth, ne, d, f2 = w_in.shape
    dff = w_out.shape[2]
    tm = MOE_TM
    grid_spec = pltpu.PrefetchScalarGridSpec(
        num_scalar_prefetch=2,
        grid=(n_rows // tm,),
        in_specs=[
            pl.BlockSpec((tm * ROW_SUB, LANES), lambda b, be, nu: (jnp.minimum(b, nu[0] - 1), 0)),
            pl.BlockSpec((None, 1, d, f2), lambda b, be, nu: (layer, be[b], 0, 0)),
            pl.BlockSpec((None, 1, 1, f2), lambda b, be, nu: (layer, be[b], 0, 0)),
            pl.BlockSpec((None, 1, dff, d), lambda b, be, nu: (layer, be[b], 0, 0)),
            pl.BlockSpec((None, 1, 1, d), lambda b, be, nu: (layer, be[b], 0, 0)),
        ],
        out_specs=pl.BlockSpec((tm * ROW_SUB, LANES), lambda b, be, nu: (b, 0)),
        scratch_shapes=[pltpu.VMEM((d, f2), BF16), pltpu.VMEM((dff, d), BF16)],
    )
    return pl.pallas_call(
        _expert_body,
        grid_spec=grid_spec,
        out_shape=jax.ShapeDtypeStruct(xs.shape, F32),
        compiler_params=_cp(("arbitrary",)),
        name="moe_experts",
    )(blk_exp, n_used, xs, w_in, b_in.reshape(depth, ne, 1, f2), w_out, b_out.reshape(depth, ne, 1, d))


def _combine_body(dest_ref, ys_hbm, gt_ref, xl_ref, g2_ref, fg_ref, out_ref, buf, sem, *, tc, final):
    def issue(t, carry):
        for kk in range(TOP_K):
            d = dest_ref[0, kk, t]
            pltpu.make_async_copy(ys_hbm.at[_row_slice(d)], buf.at[_row_slice(kk * tc + t)], sem).start(
                priority=kk % 2)
        return carry

    lax.fori_loop(0, tc, issue, 0, unroll=2)

    def drain(t, carry):
        pltpu.make_async_copy(ys_hbm.at[_row_slice(0)], buf.at[_row_slice(0)], sem).wait()
        return carry

    lax.fori_loop(0, tc * TOP_K, drain, 0, unroll=8)
    gt = gt_ref[...]
    y = gt[:, 0:1] * _from_rows(buf, 0, tc)
    for kk in range(1, TOP_K):
        y = y + gt[:, kk:kk + 1] * _from_rows(buf, kk * tc, (kk + 1) * tc)
    xl = xl_ref[0] + g2_ref[0] * y
    out_ref[0] = _rms(xl, fg_ref[...]) if final else xl


def _combine(dest, ys, gates, xl, g2, fg, *, tc, final):
    bsz, n, d = xl.shape
    t = bsz * n
    nt = n // tc
    dest3 = _dest_blocks(dest, tc)
    return pl.pallas_call(
        functools.partial(_combine_body, tc=tc, final=final),
        grid=(bsz, nt),
        in_specs=[
            pl.BlockSpec((1, SUBLANES, tc), lambda b, i: (b * nt + i, 0, 0), memory_space=pltpu.SMEM),
            pl.BlockSpec(memory_space=pl.ANY),
            pl.BlockSpec((tc, TOP_K), lambda b, i: (b * nt + i, 0)),
            pl.BlockSpec((1, tc, d), lambda b, i: (b, i, 0)),
            pl.BlockSpec((1, 1, d), lambda b, i: (b, 0, 0)),
            pl.BlockSpec((1, d), lambda b, i: (0, 0)),
        ],
        out_specs=pl.BlockSpec((1, tc, d), lambda b, i: (b, i, 0)),
        out_shape=jax.ShapeDtypeStruct((bsz, n, d), F32),
        scratch_shapes=[pltpu.VMEM((TOP_K * tc * ROW_SUB, LANES), F32), pltpu.SemaphoreType.DMA(())],
        compiler_params=_cp(("arbitrary", "arbitrary")),
        name="moe_combine",
    )(dest3, ys, gates, xl, g2, fg)


def _moe(fl, topi, gates, rank, cnt, xl, g2, fg, layer, w_in, b_in, w_out, b_out, *, final):
    t = fl.shape[0] // ROW_SUB
    tm = MOE_TM
    counts = cnt[0, :N_EXPERTS].astype(jnp.int32)
    padded = (counts + tm - 1) // tm * tm
    pad_end = jnp.cumsum(padded)
    pad_start = pad_end - padded
    dest = jnp.take(pad_start, topi) + rank
    nb = t * TOP_K // tm + N_EXPERTS
    blk_start = jnp.arange(nb, dtype=jnp.int32) * tm
    blk_exp = jnp.minimum(jnp.sum((pad_end[None, :] <= blk_start[:, None]).astype(jnp.int32), axis=1), N_EXPERTS - 1)
    n_used = (pad_end[-1:] // tm).astype(jnp.int32)
    xs = _dispatch(pad_end, padded, dest, fl, nb * tm, td=512)
    ys = _experts(blk_exp, n_used, xs, layer, w_in, b_in, w_out, b_out)
    return _combine(dest, ys, gates, xl, g2, fg, tc=512, final=final)


def _hy_in_body(x_ref, xp_ref, xn_ref, g_ref, sh_ref, sc_ref, w_ref, b_ref, cw_ref, cb_ref, o_ref, *, nt):
    i = pl.program_id(1)
    d = x_ref.shape[2]

    def normed(xx):
        return (_rms(xx, g_ref[...]) * (1.0 + sc_ref[0]) + sh_ref[0]).astype(BF16)

    h = normed(x_ref[0])
    hh = normed(jnp.concatenate([xp_ref[0], xn_ref[0]], axis=0))
    tm = h.shape[0]
    row = lax.broadcasted_iota(jnp.int32, (tm, 1), 0)
    for j in range(3):
        cols = slice(j * d, (j + 1) * d)
        p = _dot(h, w_ref[:, cols]) + b_ref[:, cols]
        ph = _dot(hh, w_ref[:, cols]) + b_ref[:, cols]
        prev = jnp.where(i > 0, ph[7:8], 0.0)
        nxt = jnp.where(i < nt - 1, ph[8:9], 0.0)
        up = jnp.where(row == 0, prev, pltpu.roll(p, 1, axis=0))
        dn = jnp.where(row == tm - 1, nxt, pltpu.roll(p, tm - 1, axis=0))
        cw = cw_ref[:, cols]
        o_ref[j, 0] = up * cw[0:1] + p * cw[1:2] + dn * cw[2:3] + cb_ref[:, cols]


def _hy_in(x, g, sh, sc, w, b, cw, cb, *, tm):
    bsz, n, d = x.shape
    nt = n // tm
    hb = tm // 8
    per_b = pl.BlockSpec((1, 1, d), lambda bb, i: (bb, 0, 0))
    full = lambda a: pl.BlockSpec(a.shape, lambda bb, i: (0,) * a.ndim)
    return pl.pallas_call(
        functools.partial(_hy_in_body, nt=nt),
        grid=(bsz, nt),
        in_specs=[
            pl.BlockSpec((1, tm, d), lambda bb, i: (bb, i, 0)),
            pl.BlockSpec((1, 8, d), lambda bb, i: (bb, jnp.maximum(i * hb - 1, 0), 0)),
            pl.BlockSpec((1, 8, d), lambda bb, i: (bb, jnp.minimum((i + 1) * hb, n // 8 - 1), 0)),
            full(g), per_b, per_b, full(w), full(b), full(cw), full(cb),
        ],
        out_specs=pl.BlockSpec((3, 1, tm, d), lambda bb, i: (0, bb, i, 0)),
        out_shape=jax.ShapeDtypeStruct((3, bsz, n, d), F32),
        compiler_params=_cp(("arbitrary", "arbitrary")),
        name="hyena_in_proj",
    )(x, x, x, g, sh, sc, w, b, cw, cb)


def _filt_feat_body(w1_ref, b1_ref, f1_ref, w2_ref, b2_ref, f2_ref, o_ref, *, n_lat):
    na = o_ref.shape[1]
    a = lax.broadcasted_iota(jnp.int32, (na, 1), 0)
    lane = lax.broadcasted_iota(jnp.int32, (na, LANES), 1)
    band_idx = jnp.where(lane <= HY_BANDS, lane - 1, lane - 1 - HY_BANDS).astype(F32)
    band = 1e-4 + band_idx * ((HY_BANDS - 1 - 1e-4) / (HY_BANDS - 1))
    for j in range(B_GROUP):
        r = a * FFT_N2 + (pl.program_id(0) * B_GROUP + j)
        pos = jnp.where(r < n_lat, r, 2 * n_lat - r).astype(F32)
        tn = pos / float(max(n_lat - 1, 1))
        ang = ((2.0 * math.pi / n_lat) * pos) * band
        z = jnp.where(lane == 0, tn, jnp.where(lane <= HY_BANDS, jnp.cos(ang),
                                               jnp.where(lane < HY_EMB, -jnp.sin(ang), 0.0)))
        h1 = jnp.sin(f1_ref[...] * (_dot_hi(z, w1_ref[...]) + b1_ref[...]))
        h2 = jnp.sin(f2_ref[...] * (_dot_hi(h1, w2_ref[...]) + b2_ref[...]))
        valid = (r != n_lat).astype(F32)
        o_ref[j] = jnp.where(lane == HY_HID, tn, jnp.where(lane == HY_HID + 1, valid, h2))


def _filt_feat(w1, b1, f1, w2, b2, f2, *, n_lat):
    na = 2 * n_lat // FFT_N2
    w1p = jnp.zeros((LANES, LANES), F32).at[:HY_EMB, :HY_HID].set(w1)
    w2p = jnp.zeros((LANES, LANES), F32).at[:HY_HID, :HY_HID].set(w2)
    padv = lambda v: jnp.zeros((1, LANES), F32).at[0, :HY_HID].set(v)
    full = lambda shp: pl.BlockSpec(shp, lambda i: (0,) * len(shp))
    return pl.pallas_call(
        functools.partial(_filt_feat_body, n_lat=n_lat),
        grid=(FFT_N2 // B_GROUP,),
        in_specs=[full((LANES, LANES)), full((1, LANES)), full((1, LANES)),
                  full((LANES, LANES)), full((1, LANES)), full((1, LANES))],
        out_specs=pl.BlockSpec((B_GROUP, na, LANES), lambda i: (i, 0, 0)),
        out_shape=jax.ShapeDtypeStruct((FFT_N2, na, LANES), F32),
        compiler_params=_cp(("arbitrary",)),
        name="hyena_filter_features",
    )(w1p, padv(b1), padv(f1), w2p, padv(b2), padv(f2))


U32 = jnp.uint32
HI16 = 0xFFFF0000


def _pack_c(re, im):
    lo = lax.bitcast_convert_type(re.astype(BF16).astype(F32), U32) >> 16
    hi = lax.bitcast_convert_type(im.astype(BF16).astype(F32), U32) & U32(HI16)
    return hi | lo


def _unpack_c(u):
    re = lax.bitcast_convert_type(u << 16, F32)
    im = lax.bitcast_convert_type(u & U32(HI16), F32)
    return jnp.concatenate([re, im], axis=0).astype(BF16)


def _interleave_tiles(xa, xb):
    tiles = []
    for c0 in range(0, xa.shape[-1], LANES):
        tiles += [xa[..., c0:c0 + LANES], xb[..., c0:c0 + LANES]]
    return jnp.concatenate(tiles, axis=-1)


def _filt_s1_body(hd_ref, w3_ref, dec_ref, tab_ref, oa_ref, ob_ref):
    na = hd_ref.shape[1]
    ha = na // 2
    oa2, ob2 = _rows2d(oa_ref), _rows2d(ob_ref)
    ft = hd_ref[:, :ha, :].reshape(B_GROUP * ha, LANES)
    fb = hd_ref[:, ha:, :].reshape(B_GROUP * ha, LANES)
    top = _dot(ft.astype(BF16), w3_ref[0, 0].astype(BF16))
    top = top * jnp.exp(-ft[:, HY_HID:HY_HID + 1] * jnp.abs(dec_ref[0, 0]))
    bot = _dot(fb.astype(BF16), w3_ref[0, 1].astype(BF16))
    bot = bot * (jnp.exp(-fb[:, HY_HID:HY_HID + 1] * jnp.abs(dec_ref[0, 1])) * fb[:, HY_HID + 1:HY_HID + 2])
    for j in range(B_GROUP):
        hb = jnp.concatenate([top[j * ha:(j + 1) * ha], bot[j * ha:(j + 1) * ha]], axis=0).astype(BF16)
        r = _dot(tab_ref[j], hb)
        packed = _pack_c(r[:na], r[na:])
        oa2[pl.ds(j, na, stride=B_GROUP), :] = packed[:, :LANES]
        ob2[pl.ds(j, na, stride=B_GROUP), :] = packed[:, LANES:]


def _filt_s1(hd, w3r, dec, tab):
    _, na, _ = hd.shape
    d = w3r.shape[-1]
    ct = 2 * LANES
    half = pl.BlockSpec((None, na, B_GROUP, LANES), lambda o, g, c: (o, 0, g, c))
    return pl.pallas_call(
        _filt_s1_body,
        grid=(2, FFT_N2 // B_GROUP, d // ct),
        in_specs=[
            pl.BlockSpec((B_GROUP, na, LANES), lambda o, g, c: (g, 0, 0)),
            pl.BlockSpec((1, 2, LANES, ct), lambda o, g, c: (o, 0, 0, c)),
            pl.BlockSpec((1, 2, 1, ct), lambda o, g, c: (o, 0, 0, c)),
            pl.BlockSpec((B_GROUP, 2 * na, na), lambda o, g, c: (g, 0, 0)),
        ],
        out_specs=[half, half],
        out_shape=[jax.ShapeDtypeStruct((2, na, FFT_N2, d // 2), U32)] * 2,
        compiler_params=_cp(("arbitrary", "arbitrary", "arbitrary")),
        name="hyena_filter_dft1",
    )(hd, w3r, dec, tab)


S2_KB = 4


def _s2_body(*refs, conv):
    if conv:
        oa_ref, ob_ref, kf_ref, ff_ref, fi_ref, g_ref = refs
    else:
        oa_ref, ob_ref, ff_ref, g_ref = refs
    for u in range(S2_KB):
        xf = _dot(ff_ref[...], _unpack_c(_interleave_tiles(oa_ref[u], ob_ref[u])))
        if conv:
            xr, xi = xf[:FFT_N2], xf[FFT_N2:]
            kr = kf_ref[0, u].astype(F32)
            ki = kf_ref[1, u].astype(F32)
            y = jnp.concatenate([xr * kr - xi * ki, xr * ki + xi * kr], axis=0).astype(BF16)
            xf = _dot(fi_ref[...], y)
            g_ref[u] = _pack_c(xf[:FFT_N2], xf[FFT_N2:])
        else:
            g_ref[0, u] = xf[:FFT_N2].astype(BF16)
            g_ref[1, u] = xf[FFT_N2:].astype(BF16)


def _s2(o3, kf, order, ff, fi, *, ct, conv):
    n1 = o3[0].shape[-3]
    d = 2 * o3[0].shape[-1]
    full = lambda a: pl.BlockSpec(a.shape, lambda k, c: (0,) * a.ndim)
    nk = n1 // S2_KB
    if conv:
        hblk = pl.BlockSpec((S2_KB, FFT_N2, ct // 2), lambda k, c: (k, 0, c))
        in_specs = [hblk, hblk, pl.BlockSpec((None, 2, S2_KB, FFT_N2, ct), lambda k, c: (order, 0, k, 0, c)),
                    full(ff), full(fi)]
        args = (o3[0], o3[1], kf, ff, fi)
        grid = (nk, d // ct)
        out_specs = pl.BlockSpec((S2_KB, FFT_N2, ct), lambda k, c: (k, 0, c))
        out_shape = jax.ShapeDtypeStruct((n1, FFT_N2, d), U32)
    else:
        no = o3[0].shape[0]
        hblk = pl.BlockSpec((None, S2_KB, FFT_N2, ct // 2), lambda k, c: (k // nk, k % nk, 0, c))
        in_specs = [hblk, hblk, full(ff)]
        args = (o3[0], o3[1], ff)
        grid = (no * nk, d // ct)
        out_specs = pl.BlockSpec((None, 2, S2_KB, FFT_N2, ct), lambda k, c: (k // nk, 0, k % nk, 0, c))
        out_shape = jax.ShapeDtypeStruct((no, 2, n1, FFT_N2, d), BF16)
    return pl.pallas_call(
        functools.partial(_s2_body, conv=conv),
        grid=grid,
        in_specs=in_specs,
        out_specs=out_specs,
        out_shape=out_shape,
        compiler_params=_cp(("arbitrary", "arbitrary")),
        name="hyena_conv_dft2" if conv else "hyena_filter_dft2",
    )(*args)


def _rows2d(ref):
    lead = ref.shape[:-3]
    return ref.reshape(lead + (ref.shape[-3] * B_GROUP, ref.shape[-1]))


def _s1_body(za_ref, zb_ref, tab_ref, oa_ref, ob_ref):
    rows, n1 = za_ref.shape[0], oa_ref.shape[0]
    za2, zb2, oa2, ob2 = _rows2d(za_ref), _rows2d(zb_ref), _rows2d(oa_ref), _rows2d(ob_ref)
    for j in range(B_GROUP):
        sl = pl.ds(j, rows, stride=B_GROUP)
        zj = jnp.concatenate([za2[sl, :], zb2[sl, :]], axis=1)
        r = _dot(tab_ref[j], zj.astype(BF16))
        packed = _pack_c(r[:n1], r[n1:])
        oa2[pl.ds(j, n1, stride=B_GROUP), :] = packed[:, :LANES]
        ob2[pl.ds(j, n1, stride=B_GROUP), :] = packed[:, LANES:]


def _s1(z4, zi, tab):
    _, rows, _, d = z4.shape
    n1 = tab.shape[1] // 2
    half = pl.BlockSpec((n1, B_GROUP, LANES), lambda g, c: (0, g, c))
    return pl.pallas_call(
        _s1_body,
        grid=(FFT_N2 // B_GROUP, d // (2 * LANES)),
        in_specs=[
            pl.BlockSpec((None, rows, B_GROUP, LANES), lambda g, c: (zi, 0, g, 2 * c)),
            pl.BlockSpec((None, rows, B_GROUP, LANES), lambda g, c: (zi, 0, g, 2 * c + 1)),
            pl.BlockSpec((B_GROUP, 2 * n1, rows), lambda g, c: (g, 0, 0)),
        ],
        out_specs=[half, half],
        out_shape=[jax.ShapeDtypeStruct((n1, FFT_N2, d // 2), U32)] * 2,
        compiler_params=_cp(("arbitrary", "arbitrary")),
        name="hyena_conv_dft1",
    )(z4, z4, tab)


def _s3_body(ga_ref, gb_ref, tab_ref, gta_ref, gtb_ref, za_ref, zb_ref, fb_ref, *rest, chain):
    if chain:
        tab1_ref, oa_ref, ob_ref, qa_ref, qb_ref = rest
    else:
        oa_ref, ob_ref = rest
    n1, rows = ga_ref.shape[0], oa_ref.shape[0]
    ga2, gb2, oa2, ob2 = _rows2d(ga_ref), _rows2d(gb_ref), _rows2d(oa_ref), _rows2d(ob_ref)
    for j in range(B_GROUP):
        sl = pl.ds(j, n1, stride=B_GROUP)
        gj = _unpack_c(jnp.concatenate([ga2[sl, :], gb2[sl, :]], axis=1))
        y = _dot(tab_ref[j], gj)
        oa2[pl.ds(j, rows, stride=B_GROUP), :] = y[:, :LANES]
        ob2[pl.ds(j, rows, stride=B_GROUP), :] = y[:, LANES:]
    fb = fb_ref[...]
    oa_ref[...] = gta_ref[...] * (oa_ref[...] + za_ref[...] * fb[:, :, :LANES])
    ob_ref[...] = gtb_ref[...] * (ob_ref[...] + zb_ref[...] * fb[:, :, LANES:])
    if chain:
        qa2, qb2 = _rows2d(qa_ref), _rows2d(qb_ref)
        for j in range(B_GROUP):
            sl = pl.ds(j, rows, stride=B_GROUP)
            zj = jnp.concatenate([oa2[sl, :], ob2[sl, :]], axis=1)
            r = _dot(tab1_ref[j], zj.astype(BF16))
            packed = _pack_c(r[:n1], r[n1:])
            qa2[pl.ds(j, n1, stride=B_GROUP), :] = packed[:, :LANES]
            qb2[pl.ds(j, n1, stride=B_GROUP), :] = packed[:, LANES:]


def _s3(g3, tab, gate4, gi, zsrc, fb, tab1):
    n1, _, d = g3.shape
    rows = tab.shape[1]
    chain = tab1 is not None
    half = pl.BlockSpec((rows, B_GROUP, LANES), lambda g, c: (0, g, c))
    chalf = pl.BlockSpec((n1, B_GROUP, LANES), lambda g, c: (0, g, c))
    nat = lambda idx, par: pl.BlockSpec((None, rows, B_GROUP, LANES), lambda g, c: (idx, 0, g, 2 * c + par))
    if isinstance(zsrc[1], int):
        z_specs, z_args = [nat(zsrc[1], 0), nat(zsrc[1], 1)], [zsrc[0], zsrc[0]]
    else:
        z_specs, z_args = [half, half], list(zsrc)
    in_specs = [
        pl.BlockSpec((n1, B_GROUP, LANES), lambda g, c: (0, g, 2 * c)),
        pl.BlockSpec((n1, B_GROUP, LANES), lambda g, c: (0, g, 2 * c + 1)),
        pl.BlockSpec((B_GROUP, rows, 2 * n1), lambda g, c: (g, 0, 0)),
        nat(gi, 0), nat(gi, 1), *z_specs,
        pl.BlockSpec((1, 1, 2 * LANES), lambda g, c: (0, 0, c)),
    ]
    args = [g3, g3, tab, gate4, gate4, *z_args, fb]
    zshape = jax.ShapeDtypeStruct((rows, FFT_N2, d // 2), F32)
    out_specs, out_shape = [half, half], [zshape, zshape]
    if chain:
        in_specs.append(pl.BlockSpec((B_GROUP, 2 * n1, rows), lambda g, c: (g, 0, 0)))
        args.append(tab1)
        out_specs += [chalf, chalf]
        out_shape += [jax.ShapeDtypeStruct((n1, FFT_N2, d // 2), U32)] * 2
    return pl.pallas_call(
        functools.partial(_s3_body, chain=chain),
        grid=(FFT_N2 // B_GROUP, d // (2 * LANES)),
        in_specs=in_specs,
        out_specs=out_specs,
        out_shape=out_shape,
        compiler_params=_cp(("arbitrary", "arbitrary")),
        name="hyena_conv_idft1_dft1" if chain else "hyena_conv_idft1",
    )(*args)


def _dft_tables(n_lat):
    n = 2 * n_lat
    n1 = n // FFT_N2
    k1 = jnp.arange(n1, dtype=jnp.int32)
    th_a = ((k1[:, None] * k1[None, :]) % n1).astype(F32) * (2.0 * math.pi / n1)
    th_b = (jnp.arange(FFT_N2, dtype=jnp.int32)[:, None] * k1[None, :]).astype(F32) * (2.0 * math.pi / n)
    ca, sa = jnp.cos(th_a)[None], jnp.sin(th_a)[None]
    cb, sb = jnp.cos(th_b)[:, :, None], jnp.sin(th_b)[:, :, None]
    cr = ca * cb - sa * sb
    sn = sa * cb + ca * sb
    ha = n1 // 2
    crh, snh = cr[:, :, :ha], sn[:, :, :ha]
    w1 = jnp.concatenate([jnp.concatenate([crh, snh], axis=2), jnp.concatenate([-snh, crh], axis=2)], axis=1)
    w1f = jnp.concatenate([cr, -sn], axis=1)
    v = jnp.swapaxes(w1, 1, 2) * (1.0 / n)
    k2 = jnp.arange(FFT_N2, dtype=jnp.int32)
    th2 = ((k2[:, None] * k2[None, :]) % FFT_N2).astype(F32) * (2.0 * math.pi / FFT_N2)
    c2, s2 = jnp.cos(th2), jnp.sin(th2)
    ff = jnp.concatenate([jnp.concatenate([c2, s2], axis=1), jnp.concatenate([-s2, c2], axis=1)], axis=0)
    fi = jnp.concatenate([jnp.concatenate([c2, -s2], axis=1), jnp.concatenate([s2, c2], axis=1)], axis=0)
    return w1.astype(BF16), w1f.astype(BF16), v.astype(BF16), ff.astype(BF16), fi.astype(BF16)


def _hyena_mix(proj3, fparams, fbias, *, n_lat):
    _, bsz, _, d = proj3.shape
    f_w1, f_b1, f_f1, f_w2, f_b2, f_f2, f_w3, decay = fparams
    na = 2 * n_lat // FFT_N2
    w1, w1f, v, ff, fi = _dft_tables(n_lat)
    hd = _filt_feat(f_w1, f_b1, f_f1, f_w2, f_b2, f_f2, n_lat=n_lat)
    w3r = jnp.transpose(f_w3.reshape(HY_HID, 2, 2, d), (1, 2, 0, 3))
    w3r = jnp.zeros((2, 2, LANES, d), F32).at[:, :, :HY_HID].set(w3r)
    kf1 = _filt_s1(hd, w3r, decay.reshape(2, 2, 1, d), w1f)
    kf = _s2(kf1, None, 0, ff, None, ct=d, conv=False)
    p3 = proj3.reshape(3, bsz * (n_lat // FFT_N2), FFT_N2, d)
    o1 = _s1(p3, 2, w1)
    g = _s2(o1, kf, 0, ff, fi, ct=d, conv=True)
    za, zb, *o1 = _s3(g, v, p3, 0, (p3, 2), fbias[0].reshape(1, 1, d), w1)
    g = _s2(o1, kf, 1, ff, fi, ct=d, conv=True)
    za, zb = _s3(g, v, p3, 1, (za, zb), fbias[1].reshape(1, 1, d), None)
    return za.reshape(bsz, n_lat, d // 2), zb.reshape(bsz, n_lat, d // 2)


def _rope_tables(n_tokens):
    rows = n_tokens // GRID_W
    row = jnp.broadcast_to(jnp.arange(rows, dtype=F32)[:, None], (rows, GRID_W)).reshape(-1)
    col = jnp.broadcast_to(jnp.arange(GRID_W, dtype=F32)[None, :], (rows, GRID_W)).reshape(-1)
    axis_dim = QK_ROPE // 2
    inv_freq = 1.0 / (ROPE_THETA ** (jnp.arange(0, axis_dim, 2, dtype=F32) / axis_dim))
    ang = jnp.concatenate([row[:, None] * inv_freq, col[:, None] * inv_freq], axis=-1)
    return jnp.cos(ang), jnp.sin(ang)


def _mla_weights(w_down, g_q, w_uq, g_kv, w_ukv):
    d = w_down.shape[0]
    nh = MLA_HEADS
    kpe = w_down[:, Q_LORA + KV_LORA:]
    w1, w2 = kpe[:, 0::2], kpe[:, 1::2]
    z = jnp.zeros((d, LANES - QK_ROPE), w_down.dtype)
    wd = jnp.concatenate([w_down[:, :Q_LORA + KV_LORA], w1, w2, z, w2, w1, z], axis=1).astype(BF16)
    uq = w_uq.reshape(Q_LORA, nh, QK_NOPE + QK_ROPE)
    pe = uq[:, :, QK_NOPE:]
    uq = jnp.concatenate([uq[:, :, :QK_NOPE], pe[:, :, 0::2], pe[:, :, 1::2]], axis=2)
    wuqT = uq.reshape(Q_LORA, nh * (QK_NOPE + QK_ROPE)).T.astype(BF16)
    ukv = w_ukv.reshape(KV_LORA, nh, QK_NOPE + V_DIM)
    wuk = ukv[:, :, :QK_NOPE].reshape(KV_LORA, nh * QK_NOPE).astype(BF16)
    wuvT = ukv[:, :, QK_NOPE:].reshape(KV_LORA, nh * V_DIM).T.astype(BF16)
    return wd, g_q.reshape(1, -1), g_kv.reshape(1, -1), wuk, wuqT, wuvT


def kernel(x, c, ctx, c_ctx, ada_w, ada_b, norm_mix_g, norm_ffn_g, mla_w_down, mla_g_q, mla_w_uq, mla_g_kv, mla_w_ukv, mla_w_o, hy_w_in, hy_b_in, hy_conv_w, hy_conv_b, hy_f_w1, hy_f_b1, hy_f_freq1, hy_f_w2, hy_f_b2, hy_f_freq2, hy_f_w3, hy_decay, hy_bias, hy_w_out, hy_b_out, moe_w_r, moe_b_r, moe_w_in, moe_b_in, moe_w_out, moe_b_out, final_g):
    bsz, n_lat, d = x.shape
    n_ctx = ctx.shape[1]
    depth = ada_w.shape[0]
    assert bsz == 2 and d == MLA_HEADS * V_DIM and n_lat % 512 == 0 and n_ctx % 128 == 0
    assert depth == 2

    cond8 = jnp.zeros((8, d), F32).at[:bsz].set(c).at[bsz].set(c_ctx)
    mods = _ada(cond8, ada_w, ada_b)

    def mod(i, j, rows):
        return mods[i, rows, j * d:(j + 1) * d][:, None, :]

    lat_rows = slice(0, bsz)
    ctx_rows = slice(bsz, bsz + 1)
    xl = x
    for i in range(depth):
        kind, j = i % 2, i // 2
        sh1, sc1, g1 = (mod(i, m, lat_rows) for m in range(3))
        sh2, sc2, g2 = (mod(i, m, lat_rows) for m in range(3, 6))
        gm = norm_mix_g[i].reshape(1, d)
        if kind == 0:
            wts = _mla_weights(mla_w_down[j], mla_g_q[j], mla_w_uq[j], mla_g_kv[j], mla_w_ukv[j])
            cos, sin = _rope_tables(n_lat)
            zl = jnp.zeros((n_lat, LANES - QK_ROPE), F32)
            tabs = (jnp.concatenate([cos, cos, zl], axis=1), jnp.concatenate([-sin, sin, zl], axis=1), cos.T, sin.T)
            tq = tv = 512
            tk = 2048 if n_lat % 4096 == 0 else 512
            qT, k, vT = _mla_proj(xl, gm, sh1, sc1, wts, tabs, need_q=True, tm=tv, tk=tv)
            half = QK_ROPE // 2
            one_c = jnp.concatenate([jnp.ones((n_ctx, QK_ROPE), F32), jnp.zeros((n_ctx, LANES - QK_ROPE), F32)], axis=1)
            tabs_c = (one_c, jnp.zeros((n_ctx, LANES), F32), jnp.ones((half, n_ctx), F32), jnp.zeros((half, n_ctx), F32))
            kc, vTc = _mla_proj(ctx, gm, mod(i, 0, ctx_rows), mod(i, 1, ctx_rows), wts, tabs_c,
                                need_q=False, tm=n_ctx, tk=n_ctx)
            o = _attention(qT, k, vT, kc, vTc, tq=tq, tk=tk)
            wo = mla_w_o[j].astype(BF16)
            bo = jnp.zeros((1, d), F32)
            transposed = True
        else:
            proj3 = _hy_in(xl, gm, sh1, sc1, hy_w_in[j].astype(BF16), hy_b_in[j].reshape(1, -1), hy_conv_w[j],
                           hy_conv_b[j].reshape(1, -1), tm=512)
            fparams = (hy_f_w1[j], hy_f_b1[j], hy_f_freq1[j], hy_f_w2[j], hy_f_b2[j], hy_f_freq2[j], hy_f_w3[j],
                       hy_decay[j])
            o = _hyena_mix(proj3, fparams, hy_bias[j], n_lat=n_lat)
            wo = hy_w_out[j].astype(BF16)
            bo = hy_b_out[j].reshape(1, d)
            transposed = False
        wr = jnp.zeros((d, LANES), F32).at[:, :N_EXPERTS].set(moe_w_r[i])
        wrh = wr.astype(BF16)
        wrl = (wr - wrh.astype(F32)).astype(BF16)
        br = jnp.zeros((1, LANES), F32).at[0, :N_EXPERTS].set(moe_b_r[i])
        xl, fl, topi, gates, rank, cnt = _post(o, wo, bo, xl, g1, norm_ffn_g[i].reshape(1, d), sh2, sc2, wrh, wrl, br,
                                               transposed=transposed, tm=512)
        xl = _moe(fl, topi, gates, rank, cnt, xl, g2, final_g.reshape(1, d), i, moe_w_in, moe_b_in,
                  moe_w_out, moe_b_out, final=(i == depth - 1))
    return xl
```

```python
import functools
import math

import jax
import jax.numpy as jnp
from jax import lax
from jax.experimental import pallas as pl
from jax.experimental.pallas import tpu as pltpu

F32 = jnp.float32
BF16 = jnp.bfloat16

EPS = 1e-6
GRID_W = 64
MLA_HEADS = 8
QK_NOPE = 128
QK_ROPE = 64
V_DIM = 128
Q_LORA = 512
KV_LORA = 256
ROPE_THETA = 10000.0
MLA_SCALE = (QK_NOPE + QK_ROPE) ** -0.5
QK_PAD = 256

HY_EMB = 33
HY_BANDS = (HY_EMB - 1) // 2
HY_HID = 64
FFT_N2 = 128
B_GROUP = 8

N_EXPERTS = 32
TOP_K = 4
SWIGLU_LIMIT = 7.0
SWIGLU_ALPHA = 1.702
MOE_TM = 256
LANES = 128

VMEM_LIMIT = 56 * 1024 * 1024


def _cp(sem, vmem=VMEM_LIMIT):
    return pltpu.CompilerParams(dimension_semantics=sem, vmem_limit_bytes=vmem)


def _dot(a, b):
    return jnp.dot(a, b, preferred_element_type=F32)


def _dot_hi(a, b):
    return jnp.dot(a, b, preferred_element_type=F32, precision=lax.Precision.HIGHEST)


def _rms(x, g):
    return x * lax.rsqrt(jnp.mean(x * x, axis=-1, keepdims=True) + EPS) * g


def _ada_body(c_ref, w_ref, b_ref, o_ref):
    c = c_ref[...]
    s = c * jax.nn.sigmoid(c)
    o_ref[0] = _dot(s.astype(BF16), w_ref[0].astype(BF16)) + b_ref[0]


def _ada(cond8, ada_w, ada_b):
    depth, d, n = ada_w.shape
    tn = n // 4
    return pl.pallas_call(
        _ada_body,
        grid=(depth, n // tn),
        in_specs=[
            pl.BlockSpec((8, d), lambda i, j: (0, 0)),
            pl.BlockSpec((1, d, tn), lambda i, j: (i, 0, j)),
            pl.BlockSpec((1, 1, tn), lambda i, j: (i, 0, j)),
        ],
        out_specs=pl.BlockSpec((1, 8, tn), lambda i, j: (i, 0, j)),
        out_shape=jax.ShapeDtypeStruct((depth, 8, n), F32),
        compiler_params=_cp(("arbitrary", "arbitrary")),
        name="ada_mod",
    )(cond8, ada_w, ada_b.reshape(depth, 1, n))


def _mla_proj_body(x_ref, g_ref, sh_ref, sc_ref, wd_ref, gq_ref, gkv_ref, wuk_ref, wuqT_ref, wuvT_ref,
                   ct_ref, st_ref, cT_ref, sT_ref, *out_refs, need_q, tk):
    if need_q:
        qT_ref, k_ref, vT_ref = out_refs
    else:
        k_ref, vT_ref = out_refs
    nh = MLA_HEADS
    x = x_ref[0]
    h = _rms(x, g_ref[...]) * (1.0 + sc_ref[0]) + sh_ref[0]
    lat = _dot(h.astype(BF16), wd_ref[...])
    o_kv = Q_LORA
    o_a = Q_LORA + KV_LORA
    kvn = _rms(lat[:, o_kv:o_a], gkv_ref[...])
    kr = (lat[:, o_a:o_a + LANES] * ct_ref[...] + lat[:, o_a + LANES:o_a + 2 * LANES] * st_ref[...]).astype(BF16)
    knope = _dot(kvn.astype(BF16), wuk_ref[...])
    for hh in range(nh):
        k_ref[0, hh, :, 0:QK_NOPE] = knope[:, hh * QK_NOPE:(hh + 1) * QK_NOPE].astype(BF16)
        k_ref[0, hh, :, QK_NOPE:QK_PAD] = kr
    vT = _dot(wuvT_ref[...], kvn.T.astype(BF16))
    tm = x.shape[0]
    for hh in range(nh):
        for c in range(tm // tk):
            vT_ref[0, hh, c] = vT[hh * V_DIM:(hh + 1) * V_DIM, c * tk:(c + 1) * tk].astype(BF16)
    if need_q:
        qn = _rms(lat[:, :Q_LORA], gq_ref[...])
        qT = _dot(wuqT_ref[...], qn.T.astype(BF16)) * (MLA_SCALE * math.log2(math.e))
        c = cT_ref[...]
        s = sT_ref[...]
        hw = QK_NOPE + QK_ROPE
        half = QK_ROPE // 2
        for hh in range(nh):
            base = hh * hw
            x1 = qT[base + QK_NOPE:base + QK_NOPE + half]
            x2 = qT[base + QK_NOPE + half:base + hw]
            qT_ref[0, hh, 0:QK_NOPE] = qT[base:base + QK_NOPE].astype(BF16)
            qT_ref[0, hh, QK_NOPE:QK_NOPE + half] = (x1 * c - x2 * s).astype(BF16)
            qT_ref[0, hh, QK_NOPE + half:hw] = (x1 * s + x2 * c).astype(BF16)
            qT_ref[0, hh, hw:QK_PAD] = jnp.zeros((QK_PAD - hw, tm), BF16)


def _mla_proj(x, g, sh, sc, wts, tabs, *, need_q, tm, tk):
    bsz, n, d = x.shape
    nh = MLA_HEADS
    wd, gq, gkv, wuk, wuqT, wuvT = wts
    ct, st, cT, sT = tabs
    nsh = sh.shape[0]
    full = lambda a: pl.BlockSpec(a.shape, lambda b, i: (0,) * a.ndim)
    in_specs = [
        pl.BlockSpec((1, tm, d), lambda b, i: (b, i, 0)),
        full(g),
        pl.BlockSpec((1, 1, d), lambda b, i: (b % nsh, 0, 0)),
        pl.BlockSpec((1, 1, d), lambda b, i: (b % nsh, 0, 0)),
        full(wd), full(gq), full(gkv), full(wuk), full(wuqT), full(wuvT),
        pl.BlockSpec((tm, LANES), lambda b, i: (i, 0)),
        pl.BlockSpec((tm, LANES), lambda b, i: (i, 0)),
        pl.BlockSpec((QK_ROPE // 2, tm), lambda b, i: (0, i)),
        pl.BlockSpec((QK_ROPE // 2, tm), lambda b, i: (0, i)),
    ]
    out_specs = [
        pl.BlockSpec((1, nh, tm, QK_PAD), lambda b, i: (b, 0, i, 0)),
        pl.BlockSpec((1, nh, tm // tk, V_DIM, tk), lambda b, i: (b, 0, i, 0, 0)),
    ]
    out_shape = [
        jax.ShapeDtypeStruct((bsz, nh, n, QK_PAD), BF16),
        jax.ShapeDtypeStruct((bsz, nh, n // tk, V_DIM, tk), BF16),
    ]
    if need_q:
        out_specs = [pl.BlockSpec((1, nh, QK_PAD, tm), lambda b, i: (b, 0, 0, i))] + out_specs
        out_shape = [jax.ShapeDtypeStruct((bsz, nh, QK_PAD, n), BF16)] + out_shape
    return pl.pallas_call(
        functools.partial(_mla_proj_body, need_q=need_q, tk=tk),
        grid=(bsz, n // tm),
        in_specs=in_specs,
        out_specs=out_specs,
        out_shape=out_shape,
        compiler_params=_cp(("arbitrary", "arbitrary")),
        name="mla_proj_q" if need_q else "mla_proj_ctx",
    )(x, g, sh, sc, wd, gq, gkv, wuk, wuqT, wuvT, ct, st, cT, sT)


SM_STRIP = 64
SUBLANES = 8


def _attn_body(qT_ref, k_ref, vT_ref, kc_ref, vTc_ref, o_ref, s0, s1, p0, p1, sc, pc, acc, m_scr, x0, x1, xc,
               a0, a1, ac, l_scr, d0, d1, dc, *, tk):
    nchunk = k_ref.shape[2] // tk

    def scores(kblk, s_ref, mx_ref):
        r = _dot(kblk, qT_ref[0, 0])
        s_ref[...] = r
        mx_ref[...] = jnp.max(r, axis=0, keepdims=True)

    def probs(s_ref, mx_ref, p_ref, a_ref, d_ref):
        m_old = m_scr[...]
        m_new = jnp.maximum(m_old, mx_ref[...])
        m_scr[...] = m_new
        alpha = jnp.exp2(m_old - m_new)
        a_ref[...] = alpha
        part = None
        for r in range(0, s_ref.shape[0], SM_STRIP):
            p = jnp.exp2(s_ref[r:r + SM_STRIP] - m_new)
            p_ref[r:r + SM_STRIP] = p.astype(BF16)
            ps = jnp.sum(p.reshape(SM_STRIP // SUBLANES, SUBLANES, p.shape[1]), axis=0)
            part = ps if part is None else part + ps
        d_ref[...] = part

    def accumulate(p_ref, a_ref, d_ref, vblk):
        acc[...] = a_ref[...] * acc[...] + _dot(vblk, p_ref[...])
        l_scr[...] = a_ref[...] * l_scr[...] + d_ref[...]

    def kchunk(i):
        return k_ref[0, 0, pl.ds(pl.multiple_of(i * tk, tk), tk), :]

    def vchunk(i):
        nsub = tk // vT_ref.shape[-1]
        return jnp.concatenate([vT_ref[0, 0, i * nsub + u] for u in range(nsub)], axis=1)

    m_scr[...] = jnp.full(m_scr.shape, -jnp.inf, F32)
    acc[...] = jnp.zeros(acc.shape, F32)
    l_scr[...] = jnp.zeros(l_scr.shape, F32)
    scores(kc_ref[0, 0], sc, xc)
    scores(kchunk(0), s0, x0)
    probs(sc, xc, pc, ac, dc)
    scores(kchunk(1), s1, x1)
    accumulate(pc, ac, dc, vTc_ref[0, 0, 0])
    probs(s0, x0, p0, a0, d0)

    def body(j, carry):
        t = 2 * j
        scores(kchunk(t + 2), s0, x0)
        accumulate(p0, a0, d0, vchunk(t))
        probs(s1, x1, p1, a1, d1)
        scores(kchunk(t + 3), s1, x1)
        accumulate(p1, a1, d1, vchunk(t + 1))
        probs(s0, x0, p0, a0, d0)
        return carry

    lax.fori_loop(0, nchunk // 2 - 1, body, 0)
    accumulate(p0, a0, d0, vchunk(nchunk - 2))
    probs(s1, x1, p1, a1, d1)
    accumulate(p1, a1, d1, vchunk(nchunk - 1))
    o_ref[0, 0] = (acc[...] / jnp.sum(l_scr[...], axis=0, keepdims=True)).astype(BF16)


def _attention(qT, k, vT, kc, vTc, *, tq, tk):
    bsz, nh, _, n = qT.shape
    nc = kc.shape[2]
    tv = vT.shape[-1]
    assert (n // tk) % 2 == 0 and tk % tv == 0
    return pl.pallas_call(
        functools.partial(_attn_body, tk=tk),
        grid=(bsz, nh, n // tq),
        in_specs=[
            pl.BlockSpec((1, 1, QK_PAD, tq), lambda b, h, i: (b, h, 0, i)),
            pl.BlockSpec((1, 1, n, QK_PAD), lambda b, h, i: (b, h, 0, 0)),
            pl.BlockSpec((1, 1, n // tv, V_DIM, tv), lambda b, h, i: (b, h, 0, 0, 0)),
            pl.BlockSpec((1, 1, nc, QK_PAD), lambda b, h, i: (b, h, 0, 0)),
            pl.BlockSpec((1, 1, 1, V_DIM, nc), lambda b, h, i: (b, h, 0, 0, 0)),
        ],
        out_specs=pl.BlockSpec((1, 1, V_DIM, tq), lambda b, h, i: (b, h, 0, i)),
        out_shape=jax.ShapeDtypeStruct((bsz, nh, V_DIM, n), BF16),
        scratch_shapes=[pltpu.VMEM((tk, tq), F32), pltpu.VMEM((tk, tq), F32),
                        pltpu.VMEM((tk, tq), BF16), pltpu.VMEM((tk, tq), BF16),
                        pltpu.VMEM((nc, tq), F32), pltpu.VMEM((nc, tq), BF16),
                        pltpu.VMEM((V_DIM, tq), F32), pltpu.VMEM((1, tq), F32),
                        pltpu.VMEM((1, tq), F32), pltpu.VMEM((1, tq), F32), pltpu.VMEM((1, tq), F32),
                        pltpu.VMEM((1, tq), F32), pltpu.VMEM((1, tq), F32), pltpu.VMEM((1, tq), F32),
                        pltpu.VMEM((SUBLANES, tq), F32), pltpu.VMEM((SUBLANES, tq), F32),
                        pltpu.VMEM((SUBLANES, tq), F32), pltpu.VMEM((SUBLANES, tq), F32)],
        compiler_params=_cp(("arbitrary", "arbitrary", "arbitrary")),
        name="mla_attention",
    )(qT, k, vT, kc, vTc)


def _post_body(*refs, transposed):
    if transposed:
        o_ref, *refs = refs
    else:
        o_ref, ob_ref, *refs = refs
    (wo_ref, bo_ref, x_ref, g1_ref, gf_ref, sh_ref, sc_ref, wrh_ref, wrl_ref, br_ref, tri_ref,
     xl_ref, fl_ref, ti_ref, gt_ref, rk_ref, cnt_ref) = refs

    @pl.when((pl.program_id(0) == 0) & (pl.program_id(1) == 0))
    def _():
        cnt_ref[...] = jnp.zeros_like(cnt_ref)

    tm = x_ref.shape[1]
    if transposed:
        oT = o_ref[0].astype(F32).reshape(MLA_HEADS * V_DIM, tm)
        o = oT.T.astype(BF16)
    else:
        o = _interleave_tiles(o_ref[0], ob_ref[0]).astype(BF16)
    y = _dot(o, wo_ref[...]) + bo_ref[...]
    xl = x_ref[0] + g1_ref[0] * y
    xl_ref[0] = xl
    fl = _rms(xl, gf_ref[...]) * (1.0 + sc_ref[0]) + sh_ref[0]
    _to_rows(fl_ref, fl)
    flh = fl.astype(BF16)
    fll = (fl - flh.astype(F32)).astype(BF16)
    logits = _dot(flh, wrh_ref[...]) + (_dot(fll, wrh_ref[...]) + _dot(flh, wrl_ref[...])) + br_ref[...]
    lane = lax.broadcasted_iota(jnp.int32, (tm, LANES), 1).astype(F32)
    neg = jnp.float32(-jnp.inf)
    work = jnp.where(lane < N_EXPERTS, logits, neg)
    vals, idxs = [], []
    onehot = jnp.zeros((tm, LANES), F32)
    for _ in range(TOP_K):
        mk = jnp.max(work, axis=-1, keepdims=True)
        ik = jnp.min(jnp.where(work == mk, lane, float(LANES)), axis=-1, keepdims=True)
        sel = lane == ik
        onehot = jnp.where(sel, 1.0, onehot)
        work = jnp.where(sel, neg, work)
        vals.append(mk)
        idxs.append(ik)
    es = [jnp.exp(v - vals[0]) for v in vals]
    den = es[0] + es[1] + es[2] + es[3]
    pre = _dot(tri_ref[...], onehot.astype(BF16)) + cnt_ref[...]
    ti = jnp.zeros((tm, LANES), F32)
    gt = jnp.zeros((tm, LANES), F32)
    rk = jnp.zeros((tm, LANES), F32)
    for kk in range(TOP_K):
        rank = jnp.sum(jnp.where(lane == idxs[kk], pre, 0.0), axis=-1, keepdims=True)
        ti = jnp.where(lane == kk, idxs[kk], ti)
        gt = jnp.where(lane == kk, es[kk] / den, gt)
        rk = jnp.where(lane == kk, rank, rk)
    ti_ref[...] = ti.T[:SUBLANES].astype(jnp.int32)
    rk_ref[...] = rk.T[:SUBLANES].astype(jnp.int32)
    gt_ref[...] = gt[:, :TOP_K]
    cnt_ref[...] += jnp.sum(onehot, axis=0, keepdims=True)


def _post(o, wo, bo, x, g1, gf, sh, sc, wrh, wrl, br, *, transposed, tm):
    bsz, n, d = x.shape
    t = bsz * n
    nt = n // tm
    tri = (lax.broadcasted_iota(jnp.int32, (tm, tm), 0) > lax.broadcasted_iota(jnp.int32, (tm, tm), 1)).astype(BF16)
    full = lambda a: pl.BlockSpec(a.shape, lambda b, i: (0,) * a.ndim)
    per_b = pl.BlockSpec((1, 1, d), lambda b, i: (b, 0, 0))
    if transposed:
        o_args = [o]
        o_specs = [pl.BlockSpec((1, MLA_HEADS, V_DIM, tm), lambda b, i: (b, 0, 0, i))]
    else:
        o_args = list(o)
        o_specs = [pl.BlockSpec((1, tm, d // 2), lambda b, i: (b, i, 0))] * 2
    tok = lambda w: pl.BlockSpec((tm, w), lambda b, i: (b * nt + i, 0))
    kmaj = pl.BlockSpec((SUBLANES, tm), lambda b, i: (0, b * nt + i))
    return pl.pallas_call(
        functools.partial(_post_body, transposed=transposed),
        grid=(bsz, nt),
        in_specs=[*o_specs, full(wo), full(bo), pl.BlockSpec((1, tm, d), lambda b, i: (b, i, 0)), per_b, full(gf),
                  per_b, per_b, full(wrh), full(wrl), full(br), full(tri)],
        out_specs=[pl.BlockSpec((1, tm, d), lambda b, i: (b, i, 0)),
                   pl.BlockSpec((tm * ROW_SUB, LANES), lambda b, i: (b * nt + i, 0)),
                   kmaj, tok(TOP_K), kmaj, pl.BlockSpec((1, LANES), lambda b, i: (0, 0))],
        out_shape=[jax.ShapeDtypeStruct((bsz, n, d), F32), jax.ShapeDtypeStruct((t * ROW_SUB, LANES), F32),
                   jax.ShapeDtypeStruct((SUBLANES, t), jnp.int32), jax.ShapeDtypeStruct((t, TOP_K), F32),
                   jax.ShapeDtypeStruct((SUBLANES, t), jnp.int32), jax.ShapeDtypeStruct((1, LANES), F32)],
        compiler_params=_cp(("arbitrary", "arbitrary")),
        name="post_attn" if transposed else "post_hyena",
    )(*o_args, wo, bo, x, g1, gf, sh, sc, wrh, wrl, br, tri)


ROW_SUB = 8


def _row_slice(i):
    return pl.ds(pl.multiple_of(i * ROW_SUB, ROW_SUB), ROW_SUB)


def _to_rows(ref, x):
    for s in range(ROW_SUB):
        ref[pl.ds(s, x.shape[0], stride=ROW_SUB), :] = x[:, s * LANES:(s + 1) * LANES]


def _from_rows(ref, lo, hi):
    return jnp.concatenate([ref[pl.ds(lo * ROW_SUB + s, hi - lo, stride=ROW_SUB), :] for s in range(ROW_SUB)], axis=1)


def _dest_blocks(dest, tile):
    return jnp.swapaxes(dest.reshape(dest.shape[0], -1, tile), 0, 1)


def _dispatch_body(pe_ref, pd_ref, dest_ref, fl_ref, xs_out, zbuf, sem, *, td):
    @pl.when(pl.program_id(0) == 0)
    def _():
        zbuf[...] = jnp.zeros(zbuf.shape, zbuf.dtype)
        for e in range(N_EXPERTS):
            @pl.when(pd_ref[e] > 0)
            def _():
                start = pl.multiple_of((pe_ref[e] - MOE_TM) * ROW_SUB, ROW_SUB)
                cp = pltpu.make_async_copy(zbuf, xs_out.at[pl.ds(start, MOE_TM * ROW_SUB)], sem)
                cp.start()
                cp.wait()

    def issue(t, carry):
        for kk in range(TOP_K):
            d = dest_ref[0, kk, t]
            pltpu.make_async_copy(fl_ref.at[_row_slice(t)], xs_out.at[_row_slice(d)], sem).start(priority=kk % 2)
        return carry

    lax.fori_loop(0, td, issue, 0, unroll=2)

    def drain(t, carry):
        pltpu.make_async_copy(fl_ref.at[_row_slice(0)], xs_out.at[_row_slice(0)], sem).wait()
        return carry

    lax.fori_loop(0, td * TOP_K, drain, 0, unroll=8)


def _dispatch(pad_end, padded, dest, fl, n_rows, *, td):
    t = fl.shape[0] // ROW_SUB
    dest3 = _dest_blocks(dest, td)
    grid_spec = pltpu.PrefetchScalarGridSpec(
        num_scalar_prefetch=2,
        grid=(t // td,),
        in_specs=[
            pl.BlockSpec((1, SUBLANES, td), lambda i, pe, pd: (i, 0, 0), memory_space=pltpu.SMEM),
            pl.BlockSpec((td * ROW_SUB, LANES), lambda i, pe, pd: (i, 0)),
        ],
        out_specs=pl.BlockSpec(memory_space=pl.ANY),
        scratch_shapes=[pltpu.VMEM((MOE_TM * ROW_SUB, LANES), fl.dtype), pltpu.SemaphoreType.DMA(())],
    )
    return pl.pallas_call(
        functools.partial(_dispatch_body, td=td),
        grid_spec=grid_spec,
        out_shape=jax.ShapeDtypeStruct((n_rows * ROW_SUB, LANES), fl.dtype),
        compiler_params=_cp(("arbitrary",)),
        name="moe_dispatch",
    )(pad_end, padded, dest3, fl)


def _expert_body(be_ref, nu_ref, xs_ref, win_ref, bin_ref, wout_ref, bout_ref, ys_ref, win_s, wout_s):
    b = pl.program_id(0)
    dff = wout_ref.shape[1]

    @pl.when(b < nu_ref[0])
    def _():
        prev = be_ref[jnp.maximum(b - 1, 0)]

        @pl.when((b == 0) | (prev != be_ref[b]))
        def _():
            win_s[...] = win_ref[0].astype(BF16)
            wout_s[...] = wout_ref[0].astype(BF16)

        x = _from_rows(xs_ref, 0, xs_ref.shape[0] // ROW_SUB).astype(BF16)
        gu = _dot(x, win_s[...]) + bin_ref[0]
        gate = jnp.minimum(gu[:, :dff], SWIGLU_LIMIT)
        lin = jnp.clip(gu[:, dff:], -SWIGLU_LIMIT, SWIGLU_LIMIT)
        act = gate * jax.nn.sigmoid(SWIGLU_ALPHA * gate) * (lin + 1.0)
        _to_rows(ys_ref, _dot(act.astype(BF16), wout_s[...]) + bout_ref[0])

    @pl.when(b >= nu_ref[0])
    def _():
        ys_ref[...] = jnp.zeros_like(ys_ref)


def _experts(blk_exp, n_used, xs, layer, w_in, b_in, w_out, b_out):
    n_rows = xs.shape[0] // ROW_SUB
    depth, ne, d, f2 = w_in.shape
    dff = w_out.shape[2]
    tm = MOE_TM
    grid_spec = pltpu.PrefetchScalarGridSpec(
        num_scalar_prefetch=2,
        grid=(n_rows // tm,),
        in_specs=[
            pl.BlockSpec((tm * ROW_SUB, LANES), lambda b, be, nu: (jnp.minimum(b, nu[0] - 1), 0)),
            pl.BlockSpec((None, 1, d, f2), lambda b, be, nu: (layer, be[b], 0, 0)),
            pl.BlockSpec((None, 1, 1, f2), lambda b, be, nu: (layer, be[b], 0, 0)),
            pl.BlockSpec((None, 1, dff, d), lambda b, be, nu: (layer, be[b], 0, 0)),
            pl.BlockSpec((None, 1, 1, d), lambda b, be, nu: (layer, be[b], 0, 0)),
        ],
        out_specs=pl.BlockSpec((tm * ROW_SUB, LANES), lambda b, be, nu: (b, 0)),
        scratch_shapes=[pltpu.VMEM((d, f2), BF16), pltpu.VMEM((dff, d), BF16)],
    )
    return pl.pallas_call(
        _expert_body,
        grid_spec=grid_spec,
        out_shape=jax.ShapeDtypeStruct(xs.shape, F32),
        compiler_params=_cp(("arbitrary",)),
        name="moe_experts",
    )(blk_exp, n_used, xs, w_in, b_in.reshape(depth, ne, 1, f2), w_out, b_out.reshape(depth, ne, 1, d))


def _combine_body(dest_ref, ys_hbm, gt_ref, xl_ref, g2_ref, fg_ref, out_ref, buf, sem, *, tc, final):
    def issue(t, carry):
        for kk in range(TOP_K):
            d = dest_ref[0, kk, t]
            pltpu.make_async_copy(ys_hbm.at[_row_slice(d)], buf.at[_row_slice(kk * tc + t)], sem).start(
                priority=kk % 2)
        return carry

    lax.fori_loop(0, tc, issue, 0, unroll=2)

    def drain(t, carry):
        pltpu.make_async_copy(ys_hbm.at[_row_slice(0)], buf.at[_row_slice(0)], sem).wait()
        return carry

    lax.fori_loop(0, tc * TOP_K, drain, 0, unroll=8)
    gt = gt_ref[...]
    y = gt[:, 0:1] * _from_rows(buf, 0, tc)
    for kk in range(1, TOP_K):
        y = y + gt[:, kk:kk + 1] * _from_rows(buf, kk * tc, (kk + 1) * tc)
    xl = xl_ref[0] + g2_ref[0] * y
    out_ref[0] = _rms(xl, fg_ref[...]) if final else xl


def _combine(dest, ys, gates, xl, g2, fg, *, tc, final):
    bsz, n, d = xl.shape
    t = bsz * n
    nt = n // tc
    dest3 = _dest_blocks(dest, tc)
    return pl.pallas_call(
        functools.partial(_combine_body, tc=tc, final=final),
        grid=(bsz, nt),
        in_specs=[
            pl.BlockSpec((1, SUBLANES, tc), lambda b, i: (b * nt + i, 0, 0), memory_space=pltpu.SMEM),
            pl.BlockSpec(memory_space=pl.ANY),
            pl.BlockSpec((tc, TOP_K), lambda b, i: (b * nt + i, 0)),
            pl.BlockSpec((1, tc, d), lambda b, i: (b, i, 0)),
            pl.BlockSpec((1, 1, d), lambda b, i: (b, 0, 0)),
            pl.BlockSpec((1, d), lambda b, i: (0, 0)),
        ],
        out_specs=pl.BlockSpec((1, tc, d), lambda b, i: (b, i, 0)),
        out_shape=jax.ShapeDtypeStruct((bsz, n, d), F32),
        scratch_shapes=[pltpu.VMEM((TOP_K * tc * ROW_SUB, LANES), F32), pltpu.SemaphoreType.DMA(())],
        compiler_params=_cp(("arbitrary", "arbitrary")),
        name="moe_combine",
    )(dest3, ys, gates, xl, g2, fg)


def _moe(fl, topi, gates, rank, cnt, xl, g2, fg, layer, w_in, b_in, w_out, b_out, *, final):
    t = fl.shape[0] // ROW_SUB
    tm = MOE_TM
    counts = cnt[0, :N_EXPERTS].astype(jnp.int32)
    padded = (counts + tm - 1) // tm * tm
    pad_end = jnp.cumsum(padded)
    pad_start = pad_end - padded
    dest = rank
    for e in range(N_EXPERTS):
        dest = dest + jnp.where(topi == e, pad_start[e], 0)
    nb = t * TOP_K // tm + N_EXPERTS
    blk_start = jnp.arange(nb, dtype=jnp.int32) * tm
    blk_exp = jnp.minimum(jnp.sum((pad_end[None, :] <= blk_start[:, None]).astype(jnp.int32), axis=1), N_EXPERTS - 1)
    n_used = (pad_end[-1:] // tm).astype(jnp.int32)
    xs = _dispatch(pad_end, padded, dest, fl, nb * tm, td=512)
    ys = _experts(blk_exp, n_used, xs, layer, w_in, b_in, w_out, b_out)
    return _combine(dest, ys, gates, xl, g2, fg, tc=512, final=final)


def _hy_in_body(x_ref, xp_ref, xn_ref, g_ref, sh_ref, sc_ref, w_ref, b_ref, cw_ref, cb_ref, o_ref, *, nt):
    i = pl.program_id(1)
    d = x_ref.shape[2]

    def normed(xx):
        return (_rms(xx, g_ref[...]) * (1.0 + sc_ref[0]) + sh_ref[0]).astype(BF16)

    h = normed(x_ref[0])
    hh = normed(jnp.concatenate([xp_ref[0], xn_ref[0]], axis=0))
    tm = h.shape[0]
    row = lax.broadcasted_iota(jnp.int32, (tm, 1), 0)
    for j in range(3):
        cols = slice(j * d, (j + 1) * d)
        p = _dot(h, w_ref[:, cols]) + b_ref[:, cols]
        ph = _dot(hh, w_ref[:, cols]) + b_ref[:, cols]
        prev = jnp.where(i > 0, ph[7:8], 0.0)
        nxt = jnp.where(i < nt - 1, ph[8:9], 0.0)
        up = jnp.where(row == 0, prev, pltpu.roll(p, 1, axis=0))
        dn = jnp.where(row == tm - 1, nxt, pltpu.roll(p, tm - 1, axis=0))
        cw = cw_ref[:, cols]
        o_ref[j, 0] = up * cw[0:1] + p * cw[1:2] + dn * cw[2:3] + cb_ref[:, cols]


def _hy_in(x, g, sh, sc, w, b, cw, cb, *, tm):
    bsz, n, d = x.shape
    nt = n // tm
    hb = tm // 8
    per_b = pl.BlockSpec((1, 1, d), lambda bb, i: (bb, 0, 0))
    full = lambda a: pl.BlockSpec(a.shape, lambda bb, i: (0,) * a.ndim)
    return pl.pallas_call(
        functools.partial(_hy_in_body, nt=nt),
        grid=(bsz, nt),
        in_specs=[
            pl.BlockSpec((1, tm, d), lambda bb, i: (bb, i, 0)),
            pl.BlockSpec((1, 8, d), lambda bb, i: (bb, jnp.maximum(i * hb - 1, 0), 0)),
            pl.BlockSpec((1, 8, d), lambda bb, i: (bb, jnp.minimum((i + 1) * hb, n // 8 - 1), 0)),
            full(g), per_b, per_b, full(w), full(b), full(cw), full(cb),
        ],
        out_specs=pl.BlockSpec((3, 1, tm, d), lambda bb, i: (0, bb, i, 0)),
        out_shape=jax.ShapeDtypeStruct((3, bsz, n, d), F32),
        compiler_params=_cp(("arbitrary", "arbitrary")),
        name="hyena_in_proj",
    )(x, x, x, g, sh, sc, w, b, cw, cb)


def _filt_feat_body(w1_ref, b1_ref, f1_ref, w2_ref, b2_ref, f2_ref, o_ref, *, n_lat):
    na = o_ref.shape[1]
    a = lax.broadcasted_iota(jnp.int32, (na, 1), 0)
    lane = lax.broadcasted_iota(jnp.int32, (na, LANES), 1)
    band_idx = jnp.where(lane <= HY_BANDS, lane - 1, lane - 1 - HY_BANDS).astype(F32)
    band = 1e-4 + band_idx * ((HY_BANDS - 1 - 1e-4) / (HY_BANDS - 1))
    for j in range(B_GROUP):
        r = a * FFT_N2 + (pl.program_id(0) * B_GROUP + j)
        pos = jnp.where(r < n_lat, r, 2 * n_lat - r).astype(F32)
        tn = pos / float(max(n_lat - 1, 1))
        ang = ((2.0 * math.pi / n_lat) * pos) * band
        z = jnp.where(lane == 0, tn, jnp.where(lane <= HY_BANDS, jnp.cos(ang),
                                               jnp.where(lane < HY_EMB, -jnp.sin(ang), 0.0)))
        h1 = jnp.sin(f1_ref[...] * (_dot_hi(z, w1_ref[...]) + b1_ref[...]))
        h2 = jnp.sin(f2_ref[...] * (_dot_hi(h1, w2_ref[...]) + b2_ref[...]))
        valid = (r != n_lat).astype(F32)
        o_ref[j] = jnp.where(lane == HY_HID, tn, jnp.where(lane == HY_HID + 1, valid, h2))


def _filt_feat(w1, b1, f1, w2, b2, f2, *, n_lat):
    na = 2 * n_lat // FFT_N2
    w1p = jnp.zeros((LANES, LANES), F32).at[:HY_EMB, :HY_HID].set(w1)
    w2p = jnp.zeros((LANES, LANES), F32).at[:HY_HID, :HY_HID].set(w2)
    padv = lambda v: jnp.zeros((1, LANES), F32).at[0, :HY_HID].set(v)
    full = lambda shp: pl.BlockSpec(shp, lambda i: (0,) * len(shp))
    return pl.pallas_call(
        functools.partial(_filt_feat_body, n_lat=n_lat),
        grid=(FFT_N2 // B_GROUP,),
        in_specs=[full((LANES, LANES)), full((1, LANES)), full((1, LANES)),
                  full((LANES, LANES)), full((1, LANES)), full((1, LANES))],
        out_specs=pl.BlockSpec((B_GROUP, na, LANES), lambda i: (i, 0, 0)),
        out_shape=jax.ShapeDtypeStruct((FFT_N2, na, LANES), F32),
        compiler_params=_cp(("arbitrary",)),
        name="hyena_filter_features",
    )(w1p, padv(b1), padv(f1), w2p, padv(b2), padv(f2))


U32 = jnp.uint32
HI16 = 0xFFFF0000


def _pack_c(re, im):
    lo = lax.bitcast_convert_type(re.astype(BF16).astype(F32), U32) >> 16
    hi = lax.bitcast_convert_type(im.astype(BF16).astype(F32), U32) & U32(HI16)
    return hi | lo


def _unpack_c(u):
    re = lax.bitcast_convert_type(u << 16, F32)
    im = lax.bitcast_convert_type(u & U32(HI16), F32)
    return jnp.concatenate([re, im], axis=0).astype(BF16)


def _interleave_tiles(xa, xb):
    tiles = []
    for c0 in range(0, xa.shape[-1], LANES):
        tiles += [xa[..., c0:c0 + LANES], xb[..., c0:c0 + LANES]]
    return jnp.concatenate(tiles, axis=-1)


def _filt_s1_body(hd_ref, w3_ref, dec_ref, tab_ref, oa_ref, ob_ref):
    na = hd_ref.shape[1]
    ha = na // 2
    oa2, ob2 = _rows2d(oa_ref), _rows2d(ob_ref)
    ft = hd_ref[:, :ha, :].reshape(B_GROUP * ha, LANES)
    fb = hd_ref[:, ha:, :].reshape(B_GROUP * ha, LANES)
    top = _dot(ft.astype(BF16), w3_ref[0, 0].astype(BF16))
    top = top * jnp.exp(-ft[:, HY_HID:HY_HID + 1] * jnp.abs(dec_ref[0, 0]))
    bot = _dot(fb.astype(BF16), w3_ref[0, 1].astype(BF16))
    bot = bot * (jnp.exp(-fb[:, HY_HID:HY_HID + 1] * jnp.abs(dec_ref[0, 1])) * fb[:, HY_HID + 1:HY_HID + 2])
    for j in range(B_GROUP):
        hb = jnp.concatenate([top[j * ha:(j + 1) * ha], bot[j * ha:(j + 1) * ha]], axis=0).astype(BF16)
        r = _dot(tab_ref[j], hb)
        packed = _pack_c(r[:na], r[na:])
        oa2[pl.ds(j, na, stride=B_GROUP), :] = packed[:, :LANES]
        ob2[pl.ds(j, na, stride=B_GROUP), :] = packed[:, LANES:]


def _filt_s1(hd, w3r, dec, tab):
    _, na, _ = hd.shape
    d = w3r.shape[-1]
    ct = 2 * LANES
    half = pl.BlockSpec((None, na, B_GROUP, LANES), lambda o, g, c: (o, 0, g, c))
    return pl.pallas_call(
        _filt_s1_body,
        grid=(2, FFT_N2 // B_GROUP, d // ct),
        in_specs=[
            pl.BlockSpec((B_GROUP, na, LANES), lambda o, g, c: (g, 0, 0)),
            pl.BlockSpec((1, 2, LANES, ct), lambda o, g, c: (o, 0, 0, c)),
            pl.BlockSpec((1, 2, 1, ct), lambda o, g, c: (o, 0, 0, c)),
            pl.BlockSpec((B_GROUP, 2 * na, na), lambda o, g, c: (g, 0, 0)),
        ],
        out_specs=[half, half],
        out_shape=[jax.ShapeDtypeStruct((2, na, FFT_N2, d // 2), U32)] * 2,
        compiler_params=_cp(("arbitrary", "arbitrary", "arbitrary")),
        name="hyena_filter_dft1",
    )(hd, w3r, dec, tab)


S2_KB = 4


def _s2_body(*refs, conv):
    if conv:
        oa_ref, ob_ref, kf_ref, ff_ref, fi_ref, g_ref = refs
    else:
        oa_ref, ob_ref, ff_ref, g_ref = refs
    for u in range(S2_KB):
        xf = _dot(ff_ref[...], _unpack_c(_interleave_tiles(oa_ref[u], ob_ref[u])))
        if conv:
            xr, xi = xf[:FFT_N2], xf[FFT_N2:]
            kr = kf_ref[0, u].astype(F32)
            ki = kf_ref[1, u].astype(F32)
            y = jnp.concatenate([xr * kr - xi * ki, xr * ki + xi * kr], axis=0).astype(BF16)
            xf = _dot(fi_ref[...], y)
            g_ref[u] = _pack_c(xf[:FFT_N2], xf[FFT_N2:])
        else:
            g_ref[0, u] = xf[:FFT_N2].astype(BF16)
            g_ref[1, u] = xf[FFT_N2:].astype(BF16)


def _s2(o3, kf, order, ff, fi, *, ct, conv):
    n1 = o3[0].shape[-3]
    d = 2 * o3[0].shape[-1]
    full = lambda a: pl.BlockSpec(a.shape, lambda k, c: (0,) * a.ndim)
    nk = n1 // S2_KB
    if conv:
        hblk = pl.BlockSpec((S2_KB, FFT_N2, ct // 2), lambda k, c: (k, 0, c))
        in_specs = [hblk, hblk, pl.BlockSpec((None, 2, S2_KB, FFT_N2, ct), lambda k, c: (order, 0, k, 0, c)),
                    full(ff), full(fi)]
        args = (o3[0], o3[1], kf, ff, fi)
        grid = (nk, d // ct)
        out_specs = pl.BlockSpec((S2_KB, FFT_N2, ct), lambda k, c: (k, 0, c))
        out_shape = jax.ShapeDtypeStruct((n1, FFT_N2, d), U32)
    else:
        no = o3[0].shape[0]
        hblk = pl.BlockSpec((None, S2_KB, FFT_N2, ct // 2), lambda k, c: (k // nk, k % nk, 0, c))
        in_specs = [hblk, hblk, full(ff)]
        args = (o3[0], o3[1], ff)
        grid = (no * nk, d // ct)
        out_specs = pl.BlockSpec((None, 2, S2_KB, FFT_N2, ct), lambda k, c: (k // nk, 0, k % nk, 0, c))
        out_shape = jax.ShapeDtypeStruct((no, 2, n1, FFT_N2, d), BF16)
    return pl.pallas_call(
        functools.partial(_s2_body, conv=conv),
        grid=grid,
        in_specs=in_specs,
        out_specs=out_specs,
        out_shape=out_shape,
        compiler_params=_cp(("arbitrary", "arbitrary")),
        name="hyena_conv_dft2" if conv else "hyena_filter_dft2",
    )(*args)


def _rows2d(ref):
    lead = ref.shape[:-3]
    return ref.reshape(lead + (ref.shape[-3] * B_GROUP, ref.shape[-1]))


def _s1_body(za_ref, zb_ref, tab_ref, oa_ref, ob_ref):
    rows, n1 = za_ref.shape[0], oa_ref.shape[0]
    za2, zb2, oa2, ob2 = _rows2d(za_ref), _rows2d(zb_ref), _rows2d(oa_ref), _rows2d(ob_ref)
    for j in range(B_GROUP):
        sl = pl.ds(j, rows, stride=B_GROUP)
        zj = jnp.concatenate([za2[sl, :], zb2[sl, :]], axis=1)
        r = _dot(tab_ref[j], zj.astype(BF16))
        packed = _pack_c(r[:n1], r[n1:])
        oa2[pl.ds(j, n1, stride=B_GROUP), :] = packed[:, :LANES]
        ob2[pl.ds(j, n1, stride=B_GROUP), :] = packed[:, LANES:]


def _s1(z4, zi, tab):
    _, rows, _, d = z4.shape
    n1 = tab.shape[1] // 2
    half = pl.BlockSpec((n1, B_GROUP, LANES), lambda g, c: (0, g, c))
    return pl.pallas_call(
        _s1_body,
        grid=(FFT_N2 // B_GROUP, d // (2 * LANES)),
        in_specs=[
            pl.BlockSpec((None, rows, B_GROUP, LANES), lambda g, c: (zi, 0, g, 2 * c)),
            pl.BlockSpec((None, rows, B_GROUP, LANES), lambda g, c: (zi, 0, g, 2 * c + 1)),
            pl.BlockSpec((B_GROUP, 2 * n1, rows), lambda g, c: (g, 0, 0)),
        ],
        out_specs=[half, half],
        out_shape=[jax.ShapeDtypeStruct((n1, FFT_N2, d // 2), U32)] * 2,
        compiler_params=_cp(("arbitrary", "arbitrary")),
        name="hyena_conv_dft1",
    )(z4, z4, tab)


def _s3_body(ga_ref, gb_ref, tab_ref, gta_ref, gtb_ref, za_ref, zb_ref, fb_ref, *rest, chain):
    if chain:
        tab1_ref, oa_ref, ob_ref, qa_ref, qb_ref = rest
    else:
        oa_ref, ob_ref = rest
    n1, rows = ga_ref.shape[0], oa_ref.shape[0]
    ga2, gb2, oa2, ob2 = _rows2d(ga_ref), _rows2d(gb_ref), _rows2d(oa_ref), _rows2d(ob_ref)
    for j in range(B_GROUP):
        sl = pl.ds(j, n1, stride=B_GROUP)
        gj = _unpack_c(jnp.concatenate([ga2[sl, :], gb2[sl, :]], axis=1))
        y = _dot(tab_ref[j], gj)
        oa2[pl.ds(j, rows, stride=B_GROUP), :] = y[:, :LANES]
        ob2[pl.ds(j, rows, stride=B_GROUP), :] = y[:, LANES:]
    fb = fb_ref[...]
    oa_ref[...] = gta_ref[...] * (oa_ref[...] + za_ref[...] * fb[:, :, :LANES])
    ob_ref[...] = gtb_ref[...] * (ob_ref[...] + zb_ref[...] * fb[:, :, LANES:])
    if chain:
        qa2, qb2 = _rows2d(qa_ref), _rows2d(qb_ref)
        for j in range(B_GROUP):
            sl = pl.ds(j, rows, stride=B_GROUP)
            zj = jnp.concatenate([oa2[sl, :], ob2[sl, :]], axis=1)
            r = _dot(tab1_ref[j], zj.astype(BF16))
            packed = _pack_c(r[:n1], r[n1:])
            qa2[pl.ds(j, n1, stride=B_GROUP), :] = packed[:, :LANES]
            qb2[pl.ds(j, n1, stride=B_GROUP), :] = packed[:, LANES:]


def _s3(g3, tab, gate4, gi, zsrc, fb, tab1):
    n1, _, d = g3.shape
    rows = tab.shape[1]
    chain = tab1 is not None
    half = pl.BlockSpec((rows, B_GROUP, LANES), lambda g, c: (0, g, c))
    chalf = pl.BlockSpec((n1, B_GROUP, LANES), lambda g, c: (0, g, c))
    nat = lambda idx, par: pl.BlockSpec((None, rows, B_GROUP, LANES), lambda g, c: (idx, 0, g, 2 * c + par))
    if isinstance(zsrc[1], int):
        z_specs, z_args = [nat(zsrc[1], 0), nat(zsrc[1], 1)], [zsrc[0], zsrc[0]]
    else:
        z_specs, z_args = [half, half], list(zsrc)
    in_specs = [
        pl.BlockSpec((n1, B_GROUP, LANES), lambda g, c: (0, g, 2 * c)),
        pl.BlockSpec((n1, B_GROUP, LANES), lambda g, c: (0, g, 2 * c + 1)),
        pl.BlockSpec((B_GROUP, rows, 2 * n1), lambda g, c: (g, 0, 0)),
        nat(gi, 0), nat(gi, 1), *z_specs,
        pl.BlockSpec((1, 1, 2 * LANES), lambda g, c: (0, 0, c)),
    ]
    args = [g3, g3, tab, gate4, gate4, *z_args, fb]
    zshape = jax.ShapeDtypeStruct((rows, FFT_N2, d // 2), F32)
    out_specs, out_shape = [half, half], [zshape, zshape]
    if chain:
        in_specs.append(pl.BlockSpec((B_GROUP, 2 * n1, rows), lambda g, c: (g, 0, 0)))
        args.append(tab1)
        out_specs += [chalf, chalf]
        out_shape += [jax.ShapeDtypeStruct((n1, FFT_N2, d // 2), U32)] * 2
    return pl.pallas_call(
        functools.partial(_s3_body, chain=chain),
        grid=(FFT_N2 // B_GROUP, d // (2 * LANES)),
        in_specs=in_specs,
        out_specs=out_specs,
        out_shape=out_shape,
        compiler_params=_cp(("arbitrary", "arbitrary")),
        name="hyena_conv_idft1_dft1" if chain else "hyena_conv_idft1",
    )(*args)


def _dft_tables(n_lat):
    n = 2 * n_lat
    n1 = n // FFT_N2
    k1 = jnp.arange(n1, dtype=jnp.int32)
    th_a = ((k1[:, None] * k1[None, :]) % n1).astype(F32) * (2.0 * math.pi / n1)
    th_b = (jnp.arange(FFT_N2, dtype=jnp.int32)[:, None] * k1[None, :]).astype(F32) * (2.0 * math.pi / n)
    ca, sa = jnp.cos(th_a)[None], jnp.sin(th_a)[None]
    cb, sb = jnp.cos(th_b)[:, :, None], jnp.sin(th_b)[:, :, None]
    cr = ca * cb - sa * sb
    sn = sa * cb + ca * sb
    ha = n1 // 2
    crh, snh = cr[:, :, :ha], sn[:, :, :ha]
    w1 = jnp.concatenate([jnp.concatenate([crh, snh], axis=2), jnp.concatenate([-snh, crh], axis=2)], axis=1)
    w1f = jnp.concatenate([cr, -sn], axis=1)
    v = jnp.swapaxes(w1, 1, 2) * (1.0 / n)
    k2 = jnp.arange(FFT_N2, dtype=jnp.int32)
    th2 = ((k2[:, None] * k2[None, :]) % FFT_N2).astype(F32) * (2.0 * math.pi / FFT_N2)
    c2, s2 = jnp.cos(th2), jnp.sin(th2)
    ff = jnp.concatenate([jnp.concatenate([c2, s2], axis=1), jnp.concatenate([-s2, c2], axis=1)], axis=0)
    fi = jnp.concatenate([jnp.concatenate([c2, -s2], axis=1), jnp.concatenate([s2, c2], axis=1)], axis=0)
    return w1.astype(BF16), w1f.astype(BF16), v.astype(BF16), ff.astype(BF16), fi.astype(BF16)


def _hyena_mix(proj3, fparams, fbias, *, n_lat):
    _, bsz, _, d = proj3.shape
    f_w1, f_b1, f_f1, f_w2, f_b2, f_f2, f_w3, decay = fparams
    na = 2 * n_lat // FFT_N2
    w1, w1f, v, ff, fi = _dft_tables(n_lat)
    hd = _filt_feat(f_w1, f_b1, f_f1, f_w2, f_b2, f_f2, n_lat=n_lat)
    w3r = jnp.transpose(f_w3.reshape(HY_HID, 2, 2, d), (1, 2, 0, 3))
    w3r = jnp.zeros((2, 2, LANES, d), F32).at[:, :, :HY_HID].set(w3r)
    kf1 = _filt_s1(hd, w3r, decay.reshape(2, 2, 1, d), w1f)
    kf = _s2(kf1, None, 0, ff, None, ct=d, conv=False)
    p3 = proj3.reshape(3, bsz * (n_lat // FFT_N2), FFT_N2, d)
    o1 = _s1(p3, 2, w1)
    g = _s2(o1, kf, 0, ff, fi, ct=d, conv=True)
    za, zb, *o1 = _s3(g, v, p3, 0, (p3, 2), fbias[0].reshape(1, 1, d), w1)
    g = _s2(o1, kf, 1, ff, fi, ct=d, conv=True)
    za, zb = _s3(g, v, p3, 1, (za, zb), fbias[1].reshape(1, 1, d), None)
    return za.reshape(bsz, n_lat, d // 2), zb.reshape(bsz, n_lat, d // 2)


def _rope_tables(n_tokens):
    rows = n_tokens // GRID_W
    row = jnp.broadcast_to(jnp.arange(rows, dtype=F32)[:, None], (rows, GRID_W)).reshape(-1)
    col = jnp.broadcast_to(jnp.arange(GRID_W, dtype=F32)[None, :], (rows, GRID_W)).reshape(-1)
    axis_dim = QK_ROPE // 2
    inv_freq = 1.0 / (ROPE_THETA ** (jnp.arange(0, axis_dim, 2, dtype=F32) / axis_dim))
    ang = jnp.concatenate([row[:, None] * inv_freq, col[:, None] * inv_freq], axis=-1)
    return jnp.cos(ang), jnp.sin(ang)


def _mla_weights(w_down, g_q, w_uq, g_kv, w_ukv):
    d = w_down.shape[0]
    nh = MLA_HEADS
    kpe = w_down[:, Q_LORA + KV_LORA:]
    w1, w2 = kpe[:, 0::2], kpe[:, 1::2]
    z = jnp.zeros((d, LANES - QK_ROPE), w_down.dtype)
    wd = jnp.concatenate([w_down[:, :Q_LORA + KV_LORA], w1, w2, z, w2, w1, z], axis=1).astype(BF16)
    uq = w_uq.reshape(Q_LORA, nh, QK_NOPE + QK_ROPE)
    pe = uq[:, :, QK_NOPE:]
    uq = jnp.concatenate([uq[:, :, :QK_NOPE], pe[:, :, 0::2], pe[:, :, 1::2]], axis=2)
    wuqT = uq.reshape(Q_LORA, nh * (QK_NOPE + QK_ROPE)).T.astype(BF16)
    ukv = w_ukv.reshape(KV_LORA, nh, QK_NOPE + V_DIM)
    wuk = ukv[:, :, :QK_NOPE].reshape(KV_LORA, nh * QK_NOPE).astype(BF16)
    wuvT = ukv[:, :, QK_NOPE:].reshape(KV_LORA, nh * V_DIM).T.astype(BF16)
    return wd, g_q.reshape(1, -1), g_kv.reshape(1, -1), wuk, wuqT, wuvT


def kernel(x, c, ctx, c_ctx, ada_w, ada_b, norm_mix_g, norm_ffn_g, mla_w_down, mla_g_q, mla_w_uq, mla_g_kv, mla_w_ukv, mla_w_o, hy_w_in, hy_b_in, hy_conv_w, hy_conv_b, hy_f_w1, hy_f_b1, hy_f_freq1, hy_f_w2, hy_f_b2, hy_f_freq2, hy_f_w3, hy_decay, hy_bias, hy_w_out, hy_b_out, moe_w_r, moe_b_r, moe_w_in, moe_b_in, moe_w_out, moe_b_out, final_g):
    bsz, n_lat, d = x.shape
    n_ctx = ctx.shape[1]
    depth = ada_w.shape[0]
    assert bsz == 2 and d == MLA_HEADS * V_DIM and n_lat % 512 == 0 and n_ctx % 128 == 0
    assert depth == 2

    cond8 = jnp.zeros((8, d), F32).at[:bsz].set(c).at[bsz].set(c_ctx)
    mods = _ada(cond8, ada_w, ada_b)

    def mod(i, j, rows):
        return mods[i, rows, j * d:(j + 1) * d][:, None, :]

    lat_rows = slice(0, bsz)
    ctx_rows = slice(bsz, bsz + 1)
    xl = x
    for i in range(depth):
        kind, j = i % 2, i // 2
        sh1, sc1, g1 = (mod(i, m, lat_rows) for m in range(3))
        sh2, sc2, g2 = (mod(i, m, lat_rows) for m in range(3, 6))
        gm = norm_mix_g[i].reshape(1, d)
        if kind == 0:
            wts = _mla_weights(mla_w_down[j], mla_g_q[j], mla_w_uq[j], mla_g_kv[j], mla_w_ukv[j])
            cos, sin = _rope_tables(n_lat)
            zl = jnp.zeros((n_lat, LANES - QK_ROPE), F32)
            tabs = (jnp.concatenate([cos, cos, zl], axis=1), jnp.concatenate([-sin, sin, zl], axis=1), cos.T, sin.T)
            tq = tv = 512
            tk = 2048 if n_lat % 4096 == 0 else 512
            qT, k, vT = _mla_proj(xl, gm, sh1, sc1, wts, tabs, need_q=True, tm=tv, tk=tv)
            half = QK_ROPE // 2
            one_c = jnp.concatenate([jnp.ones((n_ctx, QK_ROPE), F32), jnp.zeros((n_ctx, LANES - QK_ROPE), F32)], axis=1)
            tabs_c = (one_c, jnp.zeros((n_ctx, LANES), F32), jnp.ones((half, n_ctx), F32), jnp.zeros((half, n_ctx), F32))
            kc, vTc = _mla_proj(ctx, gm, mod(i, 0, ctx_rows), mod(i, 1, ctx_rows), wts, tabs_c,
                                need_q=False, tm=n_ctx, tk=n_ctx)
            o = _attention(qT, k, vT, kc, vTc, tq=tq, tk=tk)
            wo = mla_w_o[j].astype(BF16)
            bo = jnp.zeros((1, d), F32)
            transposed = True
        else:
            proj3 = _hy_in(xl, gm, sh1, sc1, hy_w_in[j].astype(BF16), hy_b_in[j].reshape(1, -1), hy_conv_w[j],
                           hy_conv_b[j].reshape(1, -1), tm=512)
            fparams = (hy_f_w1[j], hy_f_b1[j], hy_f_freq1[j], hy_f_w2[j], hy_f_b2[j], hy_f_freq2[j], hy_f_w3[j],
                       hy_decay[j])
            o = _hyena_mix(proj3, fparams, hy_bias[j], n_lat=n_lat)
            wo = hy_w_out[j].astype(BF16)
            bo = hy_b_out[j].reshape(1, d)
            transposed = False
        wr = jnp.zeros((d, LANES), F32).at[:, :N_EXPERTS].set(moe_w_r[i])
        wrh = wr.astype(BF16)
        wrl = (wr - wrh.astype(F32)).astype(BF16)
        br = jnp.zeros((1, LANES), F32).at[0, :N_EXPERTS].set(moe_b_r[i])
        xl, fl, topi, gates, rank, cnt = _post(o, wo, bo, xl, g1, norm_ffn_g[i].reshape(1, d), sh2, sc2, wrh, wrl, br,
                                               transposed=transposed, tm=512)
        xl = _moe(fl, topi, gates, rank, cnt, xl, g2, final_g.reshape(1, d), i, moe_w_in, moe_b_in,
                  moe_w_out, moe_b_out, final=(i == depth - 1))
    return xl
```

```python
import functools
import math

import jax
import jax.numpy as jnp
from jax import lax
from jax.experimental import pallas as pl
from jax.experimental.pallas import tpu as pltpu

F32 = jnp.float32
BF16 = jnp.bfloat16

EPS = 1e-6
GRID_W = 64
MLA_HEADS = 8
QK_NOPE = 128
QK_ROPE = 64
V_DIM = 128
Q_LORA = 512
KV_LORA = 256
ROPE_THETA = 10000.0
MLA_SCALE = (QK_NOPE + QK_ROPE) ** -0.5
QK_PAD = 256

HY_EMB = 33
HY_BANDS = (HY_EMB - 1) // 2
HY_HID = 64
FFT_N2 = 128
B_GROUP = 8

N_EXPERTS = 32
TOP_K = 4
SWIGLU_LIMIT = 7.0
SWIGLU_ALPHA = 1.702
MOE_TM = 512
LANES = 128

VMEM_LIMIT = 56 * 1024 * 1024


def _cp(sem, vmem=VMEM_LIMIT):
    return pltpu.CompilerParams(dimension_semantics=sem, vmem_limit_bytes=vmem)


def _dot(a, b):
    return jnp.dot(a, b, preferred_element_type=F32)


def _dot_hi(a, b):
    return jnp.dot(a, b, preferred_element_type=F32, precision=lax.Precision.HIGHEST)


def _rms(x, g):
    return x * lax.rsqrt(jnp.mean(x * x, axis=-1, keepdims=True) + EPS) * g


def _ada_body(c_ref, w_ref, b_ref, o_ref):
    c = c_ref[...]
    s = c * jax.nn.sigmoid(c)
    o_ref[0] = _dot(s.astype(BF16), w_ref[0].astype(BF16)) + b_ref[0]


def _ada(cond8, ada_w, ada_b):
    depth, d, n = ada_w.shape
    tn = n // 4
    return pl.pallas_call(
        _ada_body,
        grid=(depth, n // tn),
        in_specs=[
            pl.BlockSpec((8, d), lambda i, j: (0, 0)),
            pl.BlockSpec((1, d, tn), lambda i, j: (i, 0, j)),
            pl.BlockSpec((1, 1, tn), lambda i, j: (i, 0, j)),
        ],
        out_specs=pl.BlockSpec((1, 8, tn), lambda i, j: (i, 0, j)),
        out_shape=jax.ShapeDtypeStruct((depth, 8, n), F32),
        compiler_params=_cp(("arbitrary", "arbitrary")),
        name="ada_mod",
    )(cond8, ada_w, ada_b.reshape(depth, 1, n))


def _mla_proj_body(x_ref, g_ref, sh_ref, sc_ref, wd_ref, gq_ref, gkv_ref, wuk_ref, wuqT_ref, wuvT_ref,
                   ct_ref, st_ref, cT_ref, sT_ref, *out_refs, need_q, tk):
    if need_q:
        qT_ref, k_ref, vT_ref = out_refs
    else:
        k_ref, vT_ref = out_refs
    nh = MLA_HEADS
    x = x_ref[0]
    h = _rms(x, g_ref[...]) * (1.0 + sc_ref[0]) + sh_ref[0]
    lat = _dot(h.astype(BF16), wd_ref[...])
    o_kv = Q_LORA
    o_a = Q_LORA + KV_LORA
    kvn = _rms(lat[:, o_kv:o_a], gkv_ref[...])
    kr = (lat[:, o_a:o_a + LANES] * ct_ref[...] + lat[:, o_a + LANES:o_a + 2 * LANES] * st_ref[...]).astype(BF16)
    knope = _dot(kvn.astype(BF16), wuk_ref[...])
    for hh in range(nh):
        k_ref[0, hh, :, 0:QK_NOPE] = knope[:, hh * QK_NOPE:(hh + 1) * QK_NOPE].astype(BF16)
        k_ref[0, hh, :, QK_NOPE:QK_PAD] = kr
    vT = _dot(wuvT_ref[...], kvn.T.astype(BF16))
    tm = x.shape[0]
    for hh in range(nh):
        for c in range(tm // tk):
            vT_ref[0, hh, c] = vT[hh * V_DIM:(hh + 1) * V_DIM, c * tk:(c + 1) * tk].astype(BF16)
    if need_q:
        qn = _rms(lat[:, :Q_LORA], gq_ref[...])
        qT = _dot(wuqT_ref[...], qn.T.astype(BF16)) * (MLA_SCALE * math.log2(math.e))
        c = cT_ref[...]
        s = sT_ref[...]
        hw = QK_NOPE + QK_ROPE
        half = QK_ROPE // 2
        for hh in range(nh):
            base = hh * hw
            x1 = qT[base + QK_NOPE:base + QK_NOPE + half]
            x2 = qT[base + QK_NOPE + half:base + hw]
            qT_ref[0, hh, 0:QK_NOPE] = qT[base:base + QK_NOPE].astype(BF16)
            qT_ref[0, hh, QK_NOPE:QK_NOPE + half] = (x1 * c - x2 * s).astype(BF16)
            qT_ref[0, hh, QK_NOPE + half:hw] = (x1 * s + x2 * c).astype(BF16)
            qT_ref[0, hh, hw:QK_PAD] = jnp.zeros((QK_PAD - hw, tm), BF16)


def _mla_proj(x, g, sh, sc, wts, tabs, *, need_q, tm, tk):
    bsz, n, d = x.shape
    nh = MLA_HEADS
    wd, gq, gkv, wuk, wuqT, wuvT = wts
    ct, st, cT, sT = tabs
    nsh = sh.shape[0]
    full = lambda a: pl.BlockSpec(a.shape, lambda b, i: (0,) * a.ndim)
    in_specs = [
        pl.BlockSpec((1, tm, d), lambda b, i: (b, i, 0)),
        full(g),
        pl.BlockSpec((1, 1, d), lambda b, i: (b % nsh, 0, 0)),
        pl.BlockSpec((1, 1, d), lambda b, i: (b % nsh, 0, 0)),
        full(wd), full(gq), full(gkv), full(wuk), full(wuqT), full(wuvT),
        pl.BlockSpec((tm, LANES), lambda b, i: (i, 0)),
        pl.BlockSpec((tm, LANES), lambda b, i: (i, 0)),
        pl.BlockSpec((QK_ROPE // 2, tm), lambda b, i: (0, i)),
        pl.BlockSpec((QK_ROPE // 2, tm), lambda b, i: (0, i)),
    ]
    out_specs = [
        pl.BlockSpec((1, nh, tm, QK_PAD), lambda b, i: (b, 0, i, 0)),
        pl.BlockSpec((1, nh, tm // tk, V_DIM, tk), lambda b, i: (b, 0, i, 0, 0)),
    ]
    out_shape = [
        jax.ShapeDtypeStruct((bsz, nh, n, QK_PAD), BF16),
        jax.ShapeDtypeStruct((bsz, nh, n // tk, V_DIM, tk), BF16),
    ]
    if need_q:
        out_specs = [pl.BlockSpec((1, nh, QK_PAD, tm), lambda b, i: (b, 0, 0, i))] + out_specs
        out_shape = [jax.ShapeDtypeStruct((bsz, nh, QK_PAD, n), BF16)] + out_shape
    return pl.pallas_call(
        functools.partial(_mla_proj_body, need_q=need_q, tk=tk),
        grid=(bsz, n // tm),
        in_specs=in_specs,
        out_specs=out_specs,
        out_shape=out_shape,
        compiler_params=_cp(("arbitrary", "arbitrary")),
        name="mla_proj_q" if need_q else "mla_proj_ctx",
    )(x, g, sh, sc, wd, gq, gkv, wuk, wuqT, wuvT, ct, st, cT, sT)


SM_STRIP = 64
SUBLANES = 8


def _attn_body(qT_ref, k_ref, vT_ref, kc_ref, vTc_ref, o_ref, s0, s1, p0, p1, sc, pc, acc, m_scr, x0, x1, xc,
               a0, a1, ac, l_scr, d0, d1, dc, *, tk):
    nchunk = k_ref.shape[2] // tk

    def scores(kblk, s_ref, mx_ref):
        r = _dot(kblk, qT_ref[0, 0])
        s_ref[...] = r
        mx_ref[...] = jnp.max(r, axis=0, keepdims=True)

    def probs(s_ref, mx_ref, p_ref, a_ref, d_ref):
        m_old = m_scr[...]
        m_new = jnp.maximum(m_old, mx_ref[...])
        m_scr[...] = m_new
        alpha = jnp.exp2(m_old - m_new)
        a_ref[...] = alpha
        part = None
        for r in range(0, s_ref.shape[0], SM_STRIP):
            p = jnp.exp2(s_ref[r:r + SM_STRIP] - m_new)
            p_ref[r:r + SM_STRIP] = p.astype(BF16)
            ps = jnp.sum(p.reshape(SM_STRIP // SUBLANES, SUBLANES, p.shape[1]), axis=0)
            part = ps if part is None else part + ps
        d_ref[...] = part

    def accumulate(p_ref, a_ref, d_ref, vblk):
        acc[...] = a_ref[...] * acc[...] + _dot(vblk, p_ref[...])
        l_scr[...] = a_ref[...] * l_scr[...] + d_ref[...]

    def kchunk(i):
        return k_ref[0, 0, pl.ds(pl.multiple_of(i * tk, tk), tk), :]

    def vchunk(i):
        nsub = tk // vT_ref.shape[-1]
        return jnp.concatenate([vT_ref[0, 0, i * nsub + u] for u in range(nsub)], axis=1)

    m_scr[...] = jnp.full(m_scr.shape, -jnp.inf, F32)
    acc[...] = jnp.zeros(acc.shape, F32)
    l_scr[...] = jnp.zeros(l_scr.shape, F32)
    scores(kc_ref[0, 0], sc, xc)
    scores(kchunk(0), s0, x0)
    probs(sc, xc, pc, ac, dc)
    scores(kchunk(1), s1, x1)
    accumulate(pc, ac, dc, vTc_ref[0, 0, 0])
    probs(s0, x0, p0, a0, d0)

    def body(j, carry):
        t = 2 * j
        scores(kchunk(t + 2), s0, x0)
        accumulate(p0, a0, d0, vchunk(t))
        probs(s1, x1, p1, a1, d1)
        scores(kchunk(t + 3), s1, x1)
        accumulate(p1, a1, d1, vchunk(t + 1))
        probs(s0, x0, p0, a0, d0)
        return carry

    lax.fori_loop(0, nchunk // 2 - 1, body, 0)
    accumulate(p0, a0, d0, vchunk(nchunk - 2))
    probs(s1, x1, p1, a1, d1)
    accumulate(p1, a1, d1, vchunk(nchunk - 1))
    o_ref[0, 0] = (acc[...] / jnp.sum(l_scr[...], axis=0, keepdims=True)).astype(BF16)


def _attention(qT, k, vT, kc, vTc, *, tq, tk):
    bsz, nh, _, n = qT.shape
    nc = kc.shape[2]
    tv = vT.shape[-1]
    assert (n // tk) % 2 == 0 and tk % tv == 0
    return pl.pallas_call(
        functools.partial(_attn_body, tk=tk),
        grid=(bsz, nh, n // tq),
        in_specs=[
            pl.BlockSpec((1, 1, QK_PAD, tq), lambda b, h, i: (b, h, 0, i)),
            pl.BlockSpec((1, 1, n, QK_PAD), lambda b, h, i: (b, h, 0, 0)),
            pl.BlockSpec((1, 1, n // tv, V_DIM, tv), lambda b, h, i: (b, h, 0, 0, 0)),
            pl.BlockSpec((1, 1, nc, QK_PAD), lambda b, h, i: (b, h, 0, 0)),
            pl.BlockSpec((1, 1, 1, V_DIM, nc), lambda b, h, i: (b, h, 0, 0, 0)),
        ],
        out_specs=pl.BlockSpec((1, 1, V_DIM, tq), lambda b, h, i: (b, h, 0, i)),
        out_shape=jax.ShapeDtypeStruct((bsz, nh, V_DIM, n), BF16),
        scratch_shapes=[pltpu.VMEM((tk, tq), F32), pltpu.VMEM((tk, tq), F32),
                        pltpu.VMEM((tk, tq), BF16), pltpu.VMEM((tk, tq), BF16),
                        pltpu.VMEM((nc, tq), F32), pltpu.VMEM((nc, tq), BF16),
                        pltpu.VMEM((V_DIM, tq), F32), pltpu.VMEM((1, tq), F32),
                        pltpu.VMEM((1, tq), F32), pltpu.VMEM((1, tq), F32), pltpu.VMEM((1, tq), F32),
                        pltpu.VMEM((1, tq), F32), pltpu.VMEM((1, tq), F32), pltpu.VMEM((1, tq), F32),
                        pltpu.VMEM((SUBLANES, tq), F32), pltpu.VMEM((SUBLANES, tq), F32),
                        pltpu.VMEM((SUBLANES, tq), F32), pltpu.VMEM((SUBLANES, tq), F32)],
        compiler_params=_cp(("arbitrary", "arbitrary", "arbitrary")),
        name="mla_attention",
    )(qT, k, vT, kc, vTc)


def _post_body(*refs, transposed):
    if transposed:
        o_ref, *refs = refs
    else:
        o_ref, ob_ref, *refs = refs
    (wo_ref, bo_ref, x_ref, g1_ref, gf_ref, sh_ref, sc_ref, wrh_ref, wrl_ref, br_ref, tri_ref,
     xl_ref, fl_ref, ti_ref, gt_ref, rk_ref, cnt_ref) = refs

    @pl.when((pl.program_id(0) == 0) & (pl.program_id(1) == 0))
    def _():
        cnt_ref[...] = jnp.zeros_like(cnt_ref)

    tm = x_ref.shape[1]
    if transposed:
        oT = o_ref[0].astype(F32).reshape(MLA_HEADS * V_DIM, tm)
        o = oT.T.astype(BF16)
    else:
        o = _interleave_tiles(o_ref[0], ob_ref[0]).astype(BF16)
    y = _dot(o, wo_ref[...]) + bo_ref[...]
    xl = x_ref[0] + g1_ref[0] * y
    xl_ref[0] = xl
    fl = _rms(xl, gf_ref[...]) * (1.0 + sc_ref[0]) + sh_ref[0]
    _to_rows(fl_ref, fl)
    flh = fl.astype(BF16)
    fll = (fl - flh.astype(F32)).astype(BF16)
    logits = _dot(flh, wrh_ref[...]) + (_dot(fll, wrh_ref[...]) + _dot(flh, wrl_ref[...])) + br_ref[...]
    lane = lax.broadcasted_iota(jnp.int32, (tm, LANES), 1).astype(F32)
    neg = jnp.float32(-jnp.inf)
    work = jnp.where(lane < N_EXPERTS, logits, neg)
    vals, idxs = [], []
    onehot = jnp.zeros((tm, LANES), F32)
    for _ in range(TOP_K):
        mk = jnp.max(work, axis=-1, keepdims=True)
        ik = jnp.min(jnp.where(work == mk, lane, float(LANES)), axis=-1, keepdims=True)
        sel = lane == ik
        onehot = jnp.where(sel, 1.0, onehot)
        work = jnp.where(sel, neg, work)
        vals.append(mk)
        idxs.append(ik)
    es = [jnp.exp(v - vals[0]) for v in vals]
    den = es[0] + es[1] + es[2] + es[3]
    pre = _dot(tri_ref[...], onehot.astype(BF16)) + cnt_ref[...]
    ti = jnp.zeros((tm, LANES), F32)
    gt = jnp.zeros((tm, LANES), F32)
    rk = jnp.zeros((tm, LANES), F32)
    for kk in range(TOP_K):
        rank = jnp.sum(jnp.where(lane == idxs[kk], pre, 0.0), axis=-1, keepdims=True)
        ti = jnp.where(lane == kk, idxs[kk], ti)
        gt = jnp.where(lane == kk, es[kk] / den, gt)
        rk = jnp.where(lane == kk, rank, rk)
    ti_ref[...] = ti[:, :TOP_K].astype(jnp.int32)
    gt_ref[...] = gt[:, :TOP_K]
    rk_ref[...] = rk[:, :TOP_K].astype(jnp.int32)
    cnt_ref[...] += jnp.sum(onehot, axis=0, keepdims=True)


def _post(o, wo, bo, x, g1, gf, sh, sc, wrh, wrl, br, *, transposed, tm):
    bsz, n, d = x.shape
    t = bsz * n
    nt = n // tm
    tri = (lax.broadcasted_iota(jnp.int32, (tm, tm), 0) > lax.broadcasted_iota(jnp.int32, (tm, tm), 1)).astype(BF16)
    full = lambda a: pl.BlockSpec(a.shape, lambda b, i: (0,) * a.ndim)
    per_b = pl.BlockSpec((1, 1, d), lambda b, i: (b, 0, 0))
    if transposed:
        o_args = [o]
        o_specs = [pl.BlockSpec((1, MLA_HEADS, V_DIM, tm), lambda b, i: (b, 0, 0, i))]
    else:
        o_args = list(o)
        o_specs = [pl.BlockSpec((1, tm, d // 2), lambda b, i: (b, i, 0))] * 2
    tok = lambda w: pl.BlockSpec((tm, w), lambda b, i: (b * nt + i, 0))
    return pl.pallas_call(
        functools.partial(_post_body, transposed=transposed),
        grid=(bsz, nt),
        in_specs=[*o_specs, full(wo), full(bo), pl.BlockSpec((1, tm, d), lambda b, i: (b, i, 0)), per_b, full(gf),
                  per_b, per_b, full(wrh), full(wrl), full(br), full(tri)],
        out_specs=[pl.BlockSpec((1, tm, d), lambda b, i: (b, i, 0)),
                   pl.BlockSpec((tm * ROW_SUB, LANES), lambda b, i: (b * nt + i, 0)),
                   tok(TOP_K), tok(TOP_K), tok(TOP_K), pl.BlockSpec((1, LANES), lambda b, i: (0, 0))],
        out_shape=[jax.ShapeDtypeStruct((bsz, n, d), F32), jax.ShapeDtypeStruct((t * ROW_SUB, LANES), F32),
                   jax.ShapeDtypeStruct((t, TOP_K), jnp.int32), jax.ShapeDtypeStruct((t, TOP_K), F32),
                   jax.ShapeDtypeStruct((t, TOP_K), jnp.int32), jax.ShapeDtypeStruct((1, LANES), F32)],
        compiler_params=_cp(("arbitrary", "arbitrary")),
        name="post_attn" if transposed else "post_hyena",
    )(*o_args, wo, bo, x, g1, gf, sh, sc, wrh, wrl, br, tri)


ROW_SUB = 8


def _row_slice(i):
    return pl.ds(pl.multiple_of(i * ROW_SUB, ROW_SUB), ROW_SUB)


def _to_rows(ref, x):
    for s in range(ROW_SUB):
        ref[pl.ds(s, x.shape[0], stride=ROW_SUB), :] = x[:, s * LANES:(s + 1) * LANES]


def _from_rows(ref, lo, hi):
    return jnp.concatenate([ref[pl.ds(lo * ROW_SUB + s, hi - lo, stride=ROW_SUB), :] for s in range(ROW_SUB)], axis=1)


def _dispatch_body(pe_ref, pd_ref, dest_ref, fl_ref, xs_out, zbuf, sem, *, td):
    @pl.when(pl.program_id(0) == 0)
    def _():
        zbuf[...] = jnp.zeros(zbuf.shape, zbuf.dtype)
        for e in range(N_EXPERTS):
            @pl.when(pd_ref[e] > 0)
            def _():
                start = pl.multiple_of((pe_ref[e] - MOE_TM) * ROW_SUB, ROW_SUB)
                cp = pltpu.make_async_copy(zbuf, xs_out.at[pl.ds(start, MOE_TM * ROW_SUB)], sem)
                cp.start()
                cp.wait()

    def issue(t, carry):
        for kk in range(TOP_K):
            d = dest_ref[0, 0, t * TOP_K + kk]
            pltpu.make_async_copy(fl_ref.at[_row_slice(t)], xs_out.at[_row_slice(d)], sem).start(priority=kk % 2)
        return carry

    lax.fori_loop(0, td, issue, 0, unroll=2)

    def drain(t, carry):
        pltpu.make_async_copy(fl_ref.at[_row_slice(0)], xs_out.at[_row_slice(0)], sem).wait()
        return carry

    lax.fori_loop(0, td * TOP_K, drain, 0, unroll=8)


def _dispatch(pad_end, padded, dest, fl, n_rows, *, td):
    t = fl.shape[0] // ROW_SUB
    dest3 = dest.reshape(t // td, 1, td * TOP_K)
    grid_spec = pltpu.PrefetchScalarGridSpec(
        num_scalar_prefetch=2,
        grid=(t // td,),
        in_specs=[
            pl.BlockSpec((1, 1, td * TOP_K), lambda i, pe, pd: (i, 0, 0), memory_space=pltpu.SMEM),
            pl.BlockSpec((td * ROW_SUB, LANES), lambda i, pe, pd: (i, 0)),
        ],
        out_specs=pl.BlockSpec(memory_space=pl.ANY),
        scratch_shapes=[pltpu.VMEM((MOE_TM * ROW_SUB, LANES), fl.dtype), pltpu.SemaphoreType.DMA(())],
    )
    return pl.pallas_call(
        functools.partial(_dispatch_body, td=td),
        grid_spec=grid_spec,
        out_shape=jax.ShapeDtypeStruct((n_rows * ROW_SUB, LANES), fl.dtype),
        compiler_params=_cp(("arbitrary",)),
        name="moe_dispatch",
    )(pad_end, padded, dest3, fl)


def _expert_body(be_ref, nu_ref, xs_ref, win_ref, bin_ref, wout_ref, bout_ref, ys_ref, win_s, wout_s):
    b = pl.program_id(0)
    dff = wout_ref.shape[1]

    @pl.when(b < nu_ref[0])
    def _():
        prev = be_ref[jnp.maximum(b - 1, 0)]

        @pl.when((b == 0) | (prev != be_ref[b]))
        def _():
            win_s[...] = win_ref[0].astype(BF16)
            wout_s[...] = wout_ref[0].astype(BF16)

        x = _from_rows(xs_ref, 0, xs_ref.shape[0] // ROW_SUB).astype(BF16)
        gu = _dot(x, win_s[...]) + bin_ref[0]
        gate = jnp.minimum(gu[:, :dff], SWIGLU_LIMIT)
        lin = jnp.clip(gu[:, dff:], -SWIGLU_LIMIT, SWIGLU_LIMIT)
        act = gate * jax.nn.sigmoid(SWIGLU_ALPHA * gate) * (lin + 1.0)
        _to_rows(ys_ref, _dot(act.astype(BF16), wout_s[...]) + bout_ref[0])

    @pl.when(b >= nu_ref[0])
    def _():
        ys_ref[...] = jnp.zeros_like(ys_ref)


def _experts(blk_exp, n_used, xs, layer, w_in, b_in, w_out, b_out):
    n_rows = xs.shape[0] // ROW_SUB
    depth, ne, d, f2 = w_in.shape
    dff = w_out.shape[2]
    tm = MOE_TM
    grid_spec = pltpu.PrefetchScalarGridSpec(
        num_scalar_prefetch=2,
        grid=(n_rows // tm,),
        in_specs=[
            pl.BlockSpec((tm * ROW_SUB, LANES), lambda b, be, nu: (jnp.minimum(b, nu[0] - 1), 0)),
            pl.BlockSpec((None, 1, d, f2), lambda b, be, nu: (layer, be[b], 0, 0)),
            pl.BlockSpec((None, 1, 1, f2), lambda b, be, nu: (layer, be[b], 0, 0)),
            pl.BlockSpec((None, 1, dff, d), lambda b, be, nu: (layer, be[b], 0, 0)),
            pl.BlockSpec((None, 1, 1, d), lambda b, be, nu: (layer, be[b], 0, 0)),
        ],
        out_specs=pl.BlockSpec((tm * ROW_SUB, LANES), lambda b, be, nu: (b, 0)),
        scratch_shapes=[pltpu.VMEM((d, f2), BF16), pltpu.VMEM((dff, d), BF16)],
    )
    return pl.pallas_call(
        _expert_body,
        grid_spec=grid_spec,
        out_shape=jax.ShapeDtypeStruct(xs.shape, F32),
        compiler_params=_cp(("arbitrary",)),
        name="moe_experts",
    )(blk_exp, n_used, xs, w_in, b_in.reshape(depth, ne, 1, f2), w_out, b_out.reshape(depth, ne, 1, d))


def _combine_body(dest_ref, ys_hbm, gt_ref, xl_ref, g2_ref, fg_ref, out_ref, buf, sem, *, tc, final):
    def issue(t, carry):
        for kk in range(TOP_K):
            d = dest_ref[0, 0, t * TOP_K + kk]
            pltpu.make_async_copy(ys_hbm.at[_row_slice(d)], buf.at[_row_slice(kk * tc + t)], sem).start(
                priority=kk % 2)
        return carry

    lax.fori_loop(0, tc, issue, 0, unroll=2)

    def drain(t, carry):
        pltpu.make_async_copy(ys_hbm.at[_row_slice(0)], buf.at[_row_slice(0)], sem).wait()
        return carry

    lax.fori_loop(0, tc * TOP_K, drain, 0, unroll=8)
    gt = gt_ref[...]
    y = gt[:, 0:1] * _from_rows(buf, 0, tc)
    for kk in range(1, TOP_K):
        y = y + gt[:, kk:kk + 1] * _from_rows(buf, kk * tc, (kk + 1) * tc)
    xl = xl_ref[0] + g2_ref[0] * y
    out_ref[0] = _rms(xl, fg_ref[...]) if final else xl


def _combine(dest, ys, gates, xl, g2, fg, *, tc, final):
    bsz, n, d = xl.shape
    t = bsz * n
    nt = n // tc
    dest3 = dest.reshape(t // tc, 1, tc * TOP_K)
    return pl.pallas_call(
        functools.partial(_combine_body, tc=tc, final=final),
        grid=(bsz, nt),
        in_specs=[
            pl.BlockSpec((1, 1, tc * TOP_K), lambda b, i: (b * nt + i, 0, 0), memory_space=pltpu.SMEM),
            pl.BlockSpec(memory_space=pl.ANY),
            pl.BlockSpec((tc, TOP_K), lambda b, i: (b * nt + i, 0)),
            pl.BlockSpec((1, tc, d), lambda b, i: (b, i, 0)),
            pl.BlockSpec((1, 1, d), lambda b, i: (b, 0, 0)),
            pl.BlockSpec((1, d), lambda b, i: (0, 0)),
        ],
        out_specs=pl.BlockSpec((1, tc, d), lambda b, i: (b, i, 0)),
        out_shape=jax.ShapeDtypeStruct((bsz, n, d), F32),
        scratch_shapes=[pltpu.VMEM((TOP_K * tc * ROW_SUB, LANES), F32), pltpu.SemaphoreType.DMA(())],
        compiler_params=_cp(("arbitrary", "arbitrary")),
        name="moe_combine",
    )(dest3, ys, gates, xl, g2, fg)


def _moe(fl, topi, gates, rank, cnt, xl, g2, fg, layer, w_in, b_in, w_out, b_out, *, final):
    t = fl.shape[0] // ROW_SUB
    tm = MOE_TM
    counts = cnt[0, :N_EXPERTS].astype(jnp.int32)
    padded = (counts + tm - 1) // tm * tm
    pad_end = jnp.cumsum(padded)
    pad_start = pad_end - padded
    dest = jnp.take(pad_start, topi) + rank
    nb = t * TOP_K // tm + N_EXPERTS
    blk_start = jnp.arange(nb, dtype=jnp.int32) * tm
    blk_exp = jnp.minimum(jnp.sum((pad_end[None, :] <= blk_start[:, None]).astype(jnp.int32), axis=1), N_EXPERTS - 1)
    n_used = (pad_end[-1:] // tm).astype(jnp.int32)
    xs = _dispatch(pad_end, padded, dest, fl, nb * tm, td=512)
    ys = _experts(blk_exp, n_used, xs, layer, w_in, b_in, w_out, b_out)
    return _combine(dest, ys, gates, xl, g2, fg, tc=512, final=final)


def _hy_in_body(x_ref, xp_ref, xn_ref, g_ref, sh_ref, sc_ref, w_ref, b_ref, cw_ref, cb_ref, o_ref, *, nt):
    i = pl.program_id(1)
    d = x_ref.shape[2]

    def normed(xx):
        return (_rms(xx, g_ref[...]) * (1.0 + sc_ref[0]) + sh_ref[0]).astype(BF16)

    h = normed(x_ref[0])
    hh = normed(jnp.concatenate([xp_ref[0], xn_ref[0]], axis=0))
    tm = h.shape[0]
    row = lax.broadcasted_iota(jnp.int32, (tm, 1), 0)
    for j in range(3):
        cols = slice(j * d, (j + 1) * d)
        p = _dot(h, w_ref[:, cols]) + b_ref[:, cols]
        ph = _dot(hh, w_ref[:, cols]) + b_ref[:, cols]
        prev = jnp.where(i > 0, ph[7:8], 0.0)
        nxt = jnp.where(i < nt - 1, ph[8:9], 0.0)
        up = jnp.where(row == 0, prev, pltpu.roll(p, 1, axis=0))
        dn = jnp.where(row == tm - 1, nxt, pltpu.roll(p, tm - 1, axis=0))
        cw = cw_ref[:, cols]
        o_ref[j, 0] = up * cw[0:1] + p * cw[1:2] + dn * cw[2:3] + cb_ref[:, cols]


def _hy_in(x, g, sh, sc, w, b, cw, cb, *, tm):
    bsz, n, d = x.shape
    nt = n // tm
    hb = tm // 8
    per_b = pl.BlockSpec((1, 1, d), lambda bb, i: (bb, 0, 0))
    full = lambda a: pl.BlockSpec(a.shape, lambda bb, i: (0,) * a.ndim)
    return pl.pallas_call(
        functools.partial(_hy_in_body, nt=nt),
        grid=(bsz, nt),
        in_specs=[
            pl.BlockSpec((1, tm, d), lambda bb, i: (bb, i, 0)),
            pl.BlockSpec((1, 8, d), lambda bb, i: (bb, jnp.maximum(i * hb - 1, 0), 0)),
            pl.BlockSpec((1, 8, d), lambda bb, i: (bb, jnp.minimum((i + 1) * hb, n // 8 - 1), 0)),
            full(g), per_b, per_b, full(w), full(b), full(cw), full(cb),
        ],
        out_specs=pl.BlockSpec((3, 1, tm, d), lambda bb, i: (0, bb, i, 0)),
        out_shape=jax.ShapeDtypeStruct((3, bsz, n, d), F32),
        compiler_params=_cp(("arbitrary", "arbitrary")),
        name="hyena_in_proj",
    )(x, x, x, g, sh, sc, w, b, cw, cb)


def _filt_feat_body(w1_ref, b1_ref, f1_ref, w2_ref, b2_ref, f2_ref, o_ref, *, n_lat):
    na = o_ref.shape[1]
    a = lax.broadcasted_iota(jnp.int32, (na, 1), 0)
    lane = lax.broadcasted_iota(jnp.int32, (na, LANES), 1)
    band_idx = jnp.where(lane <= HY_BANDS, lane - 1, lane - 1 - HY_BANDS).astype(F32)
    band = 1e-4 + band_idx * ((HY_BANDS - 1 - 1e-4) / (HY_BANDS - 1))
    for j in range(B_GROUP):
        r = a * FFT_N2 + (pl.program_id(0) * B_GROUP + j)
        pos = jnp.where(r < n_lat, r, 2 * n_lat - r).astype(F32)
        tn = pos / float(max(n_lat - 1, 1))
        ang = ((2.0 * math.pi / n_lat) * pos) * band
        z = jnp.where(lane == 0, tn, jnp.where(lane <= HY_BANDS, jnp.cos(ang),
                                               jnp.where(lane < HY_EMB, -jnp.sin(ang), 0.0)))
        h1 = jnp.sin(f1_ref[...] * (_dot_hi(z, w1_ref[...]) + b1_ref[...]))
        h2 = jnp.sin(f2_ref[...] * (_dot_hi(h1, w2_ref[...]) + b2_ref[...]))
        valid = (r != n_lat).astype(F32)
        o_ref[j] = jnp.where(lane == HY_HID, tn, jnp.where(lane == HY_HID + 1, valid, h2))


def _filt_feat(w1, b1, f1, w2, b2, f2, *, n_lat):
    na = 2 * n_lat // FFT_N2
    w1p = jnp.zeros((LANES, LANES), F32).at[:HY_EMB, :HY_HID].set(w1)
    w2p = jnp.zeros((LANES, LANES), F32).at[:HY_HID, :HY_HID].set(w2)
    padv = lambda v: jnp.zeros((1, LANES), F32).at[0, :HY_HID].set(v)
    full = lambda shp: pl.BlockSpec(shp, lambda i: (0,) * len(shp))
    return pl.pallas_call(
        functools.partial(_filt_feat_body, n_lat=n_lat),
        grid=(FFT_N2 // B_GROUP,),
        in_specs=[full((LANES, LANES)), full((1, LANES)), full((1, LANES)),
                  full((LANES, LANES)), full((1, LANES)), full((1, LANES))],
        out_specs=pl.BlockSpec((B_GROUP, na, LANES), lambda i: (i, 0, 0)),
        out_shape=jax.ShapeDtypeStruct((FFT_N2, na, LANES), F32),
        compiler_params=_cp(("arbitrary",)),
        name="hyena_filter_features",
    )(w1p, padv(b1), padv(f1), w2p, padv(b2), padv(f2))


U32 = jnp.uint32
HI16 = 0xFFFF0000


def _pack_c(re, im):
    lo = lax.bitcast_convert_type(re.astype(BF16).astype(F32), U32) >> 16
    hi = lax.bitcast_convert_type(im.astype(BF16).astype(F32), U32) & U32(HI16)
    return hi | lo


def _unpack_c(u):
    re = lax.bitcast_convert_type(u << 16, F32)
    im = lax.bitcast_convert_type(u & U32(HI16), F32)
    return jnp.concatenate([re, im], axis=0).astype(BF16)


def _interleave_tiles(xa, xb):
    tiles = []
    for c0 in range(0, xa.shape[-1], LANES):
        tiles += [xa[..., c0:c0 + LANES], xb[..., c0:c0 + LANES]]
    return jnp.concatenate(tiles, axis=-1)


def _filt_s1_body(hd_ref, w3_ref, dec_ref, tab_ref, oa_ref, ob_ref):
    na = hd_ref.shape[1]
    ha = na // 2
    oa2, ob2 = _rows2d(oa_ref), _rows2d(ob_ref)
    ft = hd_ref[:, :ha, :].reshape(B_GROUP * ha, LANES)
    fb = hd_ref[:, ha:, :].reshape(B_GROUP * ha, LANES)
    top = _dot(ft.astype(BF16), w3_ref[0, 0].astype(BF16))
    top = top * jnp.exp(-ft[:, HY_HID:HY_HID + 1] * jnp.abs(dec_ref[0, 0]))
    bot = _dot(fb.astype(BF16), w3_ref[0, 1].astype(BF16))
    bot = bot * (jnp.exp(-fb[:, HY_HID:HY_HID + 1] * jnp.abs(dec_ref[0, 1])) * fb[:, HY_HID + 1:HY_HID + 2])
    for j in range(B_GROUP):
        hb = jnp.concatenate([top[j * ha:(j + 1) * ha], bot[j * ha:(j + 1) * ha]], axis=0).astype(BF16)
        r = _dot(tab_ref[j], hb)
        packed = _pack_c(r[:na], r[na:])
        oa2[pl.ds(j, na, stride=B_GROUP), :] = packed[:, :LANES]
        ob2[pl.ds(j, na, stride=B_GROUP), :] = packed[:, LANES:]


def _filt_s1(hd, w3r, dec, tab):
    _, na, _ = hd.shape
    d = w3r.shape[-1]
    ct = 2 * LANES
    half = pl.BlockSpec((None, na, B_GROUP, LANES), lambda o, g, c: (o, 0, g, c))
    return pl.pallas_call(
        _filt_s1_body,
        grid=(2, FFT_N2 // B_GROUP, d // ct),
        in_specs=[
            pl.BlockSpec((B_GROUP, na, LANES), lambda o, g, c: (g, 0, 0)),
            pl.BlockSpec((1, 2, LANES, ct), lambda o, g, c: (o, 0, 0, c)),
            pl.BlockSpec((1, 2, 1, ct), lambda o, g, c: (o, 0, 0, c)),
            pl.BlockSpec((B_GROUP, 2 * na, na), lambda o, g, c: (g, 0, 0)),
        ],
        out_specs=[half, half],
        out_shape=[jax.ShapeDtypeStruct((2, na, FFT_N2, d // 2), U32)] * 2,
        compiler_params=_cp(("arbitrary", "arbitrary", "arbitrary")),
        name="hyena_filter_dft1",
    )(hd, w3r, dec, tab)


S2_KB = 4


def _s2_body(*refs, conv):
    if conv:
        oa_ref, ob_ref, kf_ref, ff_ref, fi_ref, g_ref = refs
    else:
        oa_ref, ob_ref, ff_ref, g_ref = refs
    for u in range(S2_KB):
        xf = _dot(ff_ref[...], _unpack_c(_interleave_tiles(oa_ref[u], ob_ref[u])))
        if conv:
            xr, xi = xf[:FFT_N2], xf[FFT_N2:]
            kr = kf_ref[0, u].astype(F32)
            ki = kf_ref[1, u].astype(F32)
            y = jnp.concatenate([xr * kr - xi * ki, xr * ki + xi * kr], axis=0).astype(BF16)
            xf = _dot(fi_ref[...], y)
            g_ref[u] = _pack_c(xf[:FFT_N2], xf[FFT_N2:])
        else:
            g_ref[0, u] = xf[:FFT_N2].astype(BF16)
            g_ref[1, u] = xf[FFT_N2:].astype(BF16)


def _s2(o3, kf, order, ff, fi, *, ct, conv):
    n1 = o3[0].shape[-3]
    d = 2 * o3[0].shape[-1]
    full = lambda a: pl.BlockSpec(a.shape, lambda k, c: (0,) * a.ndim)
    nk = n1 // S2_KB
    if conv:
        hblk = pl.BlockSpec((S2_KB, FFT_N2, ct // 2), lambda k, c: (k, 0, c))
        in_specs = [hblk, hblk, pl.BlockSpec((None, 2, S2_KB, FFT_N2, ct), lambda k, c: (order, 0, k, 0, c)),
                    full(ff), full(fi)]
        args = (o3[0], o3[1], kf, ff, fi)
        grid = (nk, d // ct)
        out_specs = pl.BlockSpec((S2_KB, FFT_N2, ct), lambda k, c: (k, 0, c))
        out_shape = jax.ShapeDtypeStruct((n1, FFT_N2, d), U32)
    else:
        no = o3[0].shape[0]
        hblk = pl.BlockSpec((None, S2_KB, FFT_N2, ct // 2), lambda k, c: (k // nk, k % nk, 0, c))
        in_specs = [hblk, hblk, full(ff)]
        args = (o3[0], o3[1], ff)
        grid = (no * nk, d // ct)
        out_specs = pl.BlockSpec((None, 2, S2_KB, FFT_N2, ct), lambda k, c: (k // nk, 0, k % nk, 0, c))
        out_shape = jax.ShapeDtypeStruct((no, 2, n1, FFT_N2, d), BF16)
    return pl.pallas_call(
        functools.partial(_s2_body, conv=conv),
        grid=grid,
        in_specs=in_specs,
        out_specs=out_specs,
        out_shape=out_shape,
        compiler_params=_cp(("arbitrary", "arbitrary")),
        name="hyena_conv_dft2" if conv else "hyena_filter_dft2",
    )(*args)


def _rows2d(ref):
    lead = ref.shape[:-3]
    return ref.reshape(lead + (ref.shape[-3] * B_GROUP, ref.shape[-1]))


def _s1_body(za_ref, zb_ref, tab_ref, oa_ref, ob_ref):
    rows, n1 = za_ref.shape[0], oa_ref.shape[0]
    za2, zb2, oa2, ob2 = _rows2d(za_ref), _rows2d(zb_ref), _rows2d(oa_ref), _rows2d(ob_ref)
    for j in range(B_GROUP):
        sl = pl.ds(j, rows, stride=B_GROUP)
        zj = jnp.concatenate([za2[sl, :], zb2[sl, :]], axis=1)
        r = _dot(tab_ref[j], zj.astype(BF16))
        packed = _pack_c(r[:n1], r[n1:])
        oa2[pl.ds(j, n1, stride=B_GROUP), :] = packed[:, :LANES]
        ob2[pl.ds(j, n1, stride=B_GROUP), :] = packed[:, LANES:]


def _s1(z4, zi, tab):
    _, rows, _, d = z4.shape
    n1 = tab.shape[1] // 2
    half = pl.BlockSpec((n1, B_GROUP, LANES), lambda g, c: (0, g, c))
    return pl.pallas_call(
        _s1_body,
        grid=(FFT_N2 // B_GROUP, d // (2 * LANES)),
        in_specs=[
            pl.BlockSpec((None, rows, B_GROUP, LANES), lambda g, c: (zi, 0, g, 2 * c)),
            pl.BlockSpec((None, rows, B_GROUP, LANES), lambda g, c: (zi, 0, g, 2 * c + 1)),
            pl.BlockSpec((B_GROUP, 2 * n1, rows), lambda g, c: (g, 0, 0)),
        ],
        out_specs=[half, half],
        out_shape=[jax.ShapeDtypeStruct((n1, FFT_N2, d // 2), U32)] * 2,
        compiler_params=_cp(("arbitrary", "arbitrary")),
        name="hyena_conv_dft1",
    )(z4, z4, tab)


def _s3_body(ga_ref, gb_ref, tab_ref, gta_ref, gtb_ref, za_ref, zb_ref, fb_ref, *rest, chain):
    if chain:
        tab1_ref, oa_ref, ob_ref, qa_ref, qb_ref = rest
    else:
        oa_ref, ob_ref = rest
    n1, rows = ga_ref.shape[0], oa_ref.shape[0]
    ga2, gb2, oa2, ob2 = _rows2d(ga_ref), _rows2d(gb_ref), _rows2d(oa_ref), _rows2d(ob_ref)
    for j in range(B_GROUP):
        sl = pl.ds(j, n1, stride=B_GROUP)
        gj = _unpack_c(jnp.concatenate([ga2[sl, :], gb2[sl, :]], axis=1))
        y = _dot(tab_ref[j], gj)
        oa2[pl.ds(j, rows, stride=B_GROUP), :] = y[:, :LANES]
        ob2[pl.ds(j, rows, stride=B_GROUP), :] = y[:, LANES:]
    fb = fb_ref[...]
    oa_ref[...] = gta_ref[...] * (oa_ref[...] + za_ref[...] * fb[:, :, :LANES])
    ob_ref[...] = gtb_ref[...] * (ob_ref[...] + zb_ref[...] * fb[:, :, LANES:])
    if chain:
        qa2, qb2 = _rows2d(qa_ref), _rows2d(qb_ref)
        for j in range(B_GROUP):
            sl = pl.ds(j, rows, stride=B_GROUP)
            zj = jnp.concatenate([oa2[sl, :], ob2[sl, :]], axis=1)
            r = _dot(tab1_ref[j], zj.astype(BF16))
            packed = _pack_c(r[:n1], r[n1:])
            qa2[pl.ds(j, n1, stride=B_GROUP), :] = packed[:, :LANES]
            qb2[pl.ds(j, n1, stride=B_GROUP), :] = packed[:, LANES:]


def _s3(g3, tab, gate4, gi, zsrc, fb, tab1):
    n1, _, d = g3.shape
    rows = tab.shape[1]
    chain = tab1 is not None
    half = pl.BlockSpec((rows, B_GROUP, LANES), lambda g, c: (0, g, c))
    chalf = pl.BlockSpec((n1, B_GROUP, LANES), lambda g, c: (0, g, c))
    nat = lambda idx, par: pl.BlockSpec((None, rows, B_GROUP, LANES), lambda g, c: (idx, 0, g, 2 * c + par))
    if isinstance(zsrc[1], int):
        z_specs, z_args = [nat(zsrc[1], 0), nat(zsrc[1], 1)], [zsrc[0], zsrc[0]]
    else:
        z_specs, z_args = [half, half], list(zsrc)
    in_specs = [
        pl.BlockSpec((n1, B_GROUP, LANES), lambda g, c: (0, g, 2 * c)),
        pl.BlockSpec((n1, B_GROUP, LANES), lambda g, c: (0, g, 2 * c + 1)),
        pl.BlockSpec((B_GROUP, rows, 2 * n1), lambda g, c: (g, 0, 0)),
        nat(gi, 0), nat(gi, 1), *z_specs,
        pl.BlockSpec((1, 1, 2 * LANES), lambda g, c: (0, 0, c)),
    ]
    args = [g3, g3, tab, gate4, gate4, *z_args, fb]
    zshape = jax.ShapeDtypeStruct((rows, FFT_N2, d // 2), F32)
    out_specs, out_shape = [half, half], [zshape, zshape]
    if chain:
        in_specs.append(pl.BlockSpec((B_GROUP, 2 * n1, rows), lambda g, c: (g, 0, 0)))
        args.append(tab1)
        out_specs += [chalf, chalf]
        out_shape += [jax.ShapeDtypeStruct((n1, FFT_N2, d // 2), U32)] * 2
    return pl.pallas_call(
        functools.partial(_s3_body, chain=chain),
        grid=(FFT_N2 // B_GROUP, d // (2 * LANES)),
        in_specs=in_specs,
        out_specs=out_specs,
        out_shape=out_shape,
        compiler_params=_cp(("arbitrary", "arbitrary")),
        name="hyena_conv_idft1_dft1" if chain else "hyena_conv_idft1",
    )(*args)


def _dft_tables(n_lat):
    n = 2 * n_lat
    n1 = n // FFT_N2
    k1 = jnp.arange(n1, dtype=jnp.int32)
    th_a = ((k1[:, None] * k1[None, :]) % n1).astype(F32) * (2.0 * math.pi / n1)
    th_b = (jnp.arange(FFT_N2, dtype=jnp.int32)[:, None] * k1[None, :]).astype(F32) * (2.0 * math.pi / n)
    ca, sa = jnp.cos(th_a)[None], jnp.sin(th_a)[None]
    cb, sb = jnp.cos(th_b)[:, :, None], jnp.sin(th_b)[:, :, None]
    cr = ca * cb - sa * sb
    sn = sa * cb + ca * sb
    ha = n1 // 2
    crh, snh = cr[:, :, :ha], sn[:, :, :ha]
    w1 = jnp.concatenate([jnp.concatenate([crh, snh], axis=2), jnp.concatenate([-snh, crh], axis=2)], axis=1)
    w1f = jnp.concatenate([cr, -sn], axis=1)
    v = jnp.swapaxes(w1, 1, 2) * (1.0 / n)
    k2 = jnp.arange(FFT_N2, dtype=jnp.int32)
    th2 = ((k2[:, None] * k2[None, :]) % FFT_N2).astype(F32) * (2.0 * math.pi / FFT_N2)
    c2, s2 = jnp.cos(th2), jnp.sin(th2)
    ff = jnp.concatenate([jnp.concatenate([c2, s2], axis=1), jnp.concatenate([-s2, c2], axis=1)], axis=0)
    fi = jnp.concatenate([jnp.concatenate([c2, -s2], axis=1), jnp.concatenate([s2, c2], axis=1)], axis=0)
    return w1.astype(BF16), w1f.astype(BF16), v.astype(BF16), ff.astype(BF16), fi.astype(BF16)


def _hyena_mix(proj3, fparams, fbias, *, n_lat):
    _, bsz, _, d = proj3.shape
    f_w1, f_b1, f_f1, f_w2, f_b2, f_f2, f_w3, decay = fparams
    na = 2 * n_lat // FFT_N2
    w1, w1f, v, ff, fi = _dft_tables(n_lat)
    hd = _filt_feat(f_w1, f_b1, f_f1, f_w2, f_b2, f_f2, n_lat=n_lat)
    w3r = jnp.transpose(f_w3.reshape(HY_HID, 2, 2, d), (1, 2, 0, 3))
    w3r = jnp.zeros((2, 2, LANES, d), F32).at[:, :, :HY_HID].set(w3r)
    kf1 = _filt_s1(hd, w3r, decay.reshape(2, 2, 1, d), w1f)
    kf = _s2(kf1, None, 0, ff, None, ct=d, conv=False)
    p3 = proj3.reshape(3, bsz * (n_lat // FFT_N2), FFT_N2, d)
    o1 = _s1(p3, 2, w1)
    g = _s2(o1, kf, 0, ff, fi, ct=d, conv=True)
    za, zb, *o1 = _s3(g, v, p3, 0, (p3, 2), fbias[0].reshape(1, 1, d), w1)
    g = _s2(o1, kf, 1, ff, fi, ct=d, conv=True)
    za, zb = _s3(g, v, p3, 1, (za, zb), fbias[1].reshape(1, 1, d), None)
    return za.reshape(bsz, n_lat, d // 2), zb.reshape(bsz, n_lat, d // 2)


def _rope_tables(n_tokens):
    rows = n_tokens // GRID_W
    row = jnp.broadcast_to(jnp.arange(rows, dtype=F32)[:, None], (rows, GRID_W)).reshape(-1)
    col = jnp.broadcast_to(jnp.arange(GRID_W, dtype=F32)[None, :], (rows, GRID_W)).reshape(-1)
    axis_dim = QK_ROPE // 2
    inv_freq = 1.0 / (ROPE_THETA ** (jnp.arange(0, axis_dim, 2, dtype=F32) / axis_dim))
    ang = jnp.concatenate([row[:, None] * inv_freq, col[:, None] * inv_freq], axis=-1)
    return jnp.cos(ang), jnp.sin(ang)


def _mla_weights(w_down, g_q, w_uq, g_kv, w_ukv):
    d = w_down.shape[0]
    nh = MLA_HEADS
    kpe = w_down[:, Q_LORA + KV_LORA:]
    w1, w2 = kpe[:, 0::2], kpe[:, 1::2]
    z = jnp.zeros((d, LANES - QK_ROPE), w_down.dtype)
    wd = jnp.concatenate([w_down[:, :Q_LORA + KV_LORA], w1, w2, z, w2, w1, z], axis=1).astype(BF16)
    uq = w_uq.reshape(Q_LORA, nh, QK_NOPE + QK_ROPE)
    pe = uq[:, :, QK_NOPE:]
    uq = jnp.concatenate([uq[:, :, :QK_NOPE], pe[:, :, 0::2], pe[:, :, 1::2]], axis=2)
    wuqT = uq.reshape(Q_LORA, nh * (QK_NOPE + QK_ROPE)).T.astype(BF16)
    ukv = w_ukv.reshape(KV_LORA, nh, QK_NOPE + V_DIM)
    wuk = ukv[:, :, :QK_NOPE].reshape(KV_LORA, nh * QK_NOPE).astype(BF16)
    wuvT = ukv[:, :, QK_NOPE:].reshape(KV_LORA, nh * V_DIM).T.astype(BF16)
    return wd, g_q.reshape(1, -1), g_kv.reshape(1, -1), wuk, wuqT, wuvT


def kernel(x, c, ctx, c_ctx, ada_w, ada_b, norm_mix_g, norm_ffn_g, mla_w_down, mla_g_q, mla_w_uq, mla_g_kv, mla_w_ukv, mla_w_o, hy_w_in, hy_b_in, hy_conv_w, hy_conv_b, hy_f_w1, hy_f_b1, hy_f_freq1, hy_f_w2, hy_f_b2, hy_f_freq2, hy_f_w3, hy_decay, hy_bias, hy_w_out, hy_b_out, moe_w_r, moe_b_r, moe_w_in, moe_b_in, moe_w_out, moe_b_out, final_g):
    bsz, n_lat, d = x.shape
    n_ctx = ctx.shape[1]
    depth = ada_w.shape[0]
    assert bsz == 2 and d == MLA_HEADS * V_DIM and n_lat % 512 == 0 and n_ctx % 128 == 0
    assert depth == 2

    cond8 = jnp.zeros((8, d), F32).at[:bsz].set(c).at[bsz].set(c_ctx)
    mods = _ada(cond8, ada_w, ada_b)

    def mod(i, j, rows):
        return mods[i, rows, j * d:(j + 1) * d][:, None, :]

    lat_rows = slice(0, bsz)
    ctx_rows = slice(bsz, bsz + 1)
    xl = x
    for i in range(depth):
        kind, j = i % 2, i // 2
        sh1, sc1, g1 = (mod(i, m, lat_rows) for m in range(3))
        sh2, sc2, g2 = (mod(i, m, lat_rows) for m in range(3, 6))
        gm = norm_mix_g[i].reshape(1, d)
        if kind == 0:
            wts = _mla_weights(mla_w_down[j], mla_g_q[j], mla_w_uq[j], mla_g_kv[j], mla_w_ukv[j])
            cos, sin = _rope_tables(n_lat)
            zl = jnp.zeros((n_lat, LANES - QK_ROPE), F32)
            tabs = (jnp.concatenate([cos, cos, zl], axis=1), jnp.concatenate([-sin, sin, zl], axis=1), cos.T, sin.T)
            tq = tv = 512
            tk = 2048 if n_lat % 4096 == 0 else 512
            qT, k, vT = _mla_proj(xl, gm, sh1, sc1, wts, tabs, need_q=True, tm=tv, tk=tv)
            half = QK_ROPE // 2
            one_c = jnp.concatenate([jnp.ones((n_ctx, QK_ROPE), F32), jnp.zeros((n_ctx, LANES - QK_ROPE), F32)], axis=1)
            tabs_c = (one_c, jnp.zeros((n_ctx, LANES), F32), jnp.ones((half, n_ctx), F32), jnp.zeros((half, n_ctx), F32))
            kc, vTc = _mla_proj(ctx, gm, mod(i, 0, ctx_rows), mod(i, 1, ctx_rows), wts, tabs_c,
                                need_q=False, tm=n_ctx, tk=n_ctx)
            o = _attention(qT, k, vT, kc, vTc, tq=tq, tk=tk)
            wo = mla_w_o[j].astype(BF16)
            bo = jnp.zeros((1, d), F32)
            transposed = True
        else:
            proj3 = _hy_in(xl, gm, sh1, sc1, hy_w_in[j].astype(BF16), hy_b_in[j].reshape(1, -1), hy_conv_w[j],
                           hy_conv_b[j].reshape(1, -1), tm=512)
            fparams = (hy_f_w1[j], hy_f_b1[j], hy_f_freq1[j], hy_f_w2[j], hy_f_b2[j], hy_f_freq2[j], hy_f_w3[j],
                       hy_decay[j])
            o = _hyena_mix(proj3, fparams, hy_bias[j], n_lat=n_lat)
            wo = hy_w_out[j].astype(BF16)
            bo = hy_b_out[j].reshape(1, d)
            transposed = False
        wr = jnp.zeros((d, LANES), F32).at[:, :N_EXPERTS].set(moe_w_r[i])
        wrh = wr.astype(BF16)
        wrl = (wr - wrh.astype(F32)).astype(BF16)
        br = jnp.zeros((1, LANES), F32).at[0, :N_EXPERTS].set(moe_b_r[i])
        xl, fl, topi, gates, rank, cnt = _post(o, wo, bo, xl, g1, norm_ffn_g[i].reshape(1, d), sh2, sc2, wrh, wrl, br,
                                               transposed=transposed, tm=512)
        xl = _moe(fl, topi, gates, rank, cnt, xl, g2, final_g.reshape(1, d), i, moe_w_in, moe_b_in,
                  moe_w_out, moe_b_out, final=(i == depth - 1))
    return xl
```

```python
import functools
import math

import jax
import jax.numpy as jnp
from jax import lax
from jax.experimental import pallas as pl
from jax.experimental.pallas import tpu as pltpu

F32 = jnp.float32
BF16 = jnp.bfloat16

EPS = 1e-6
GRID_W = 64
MLA_HEADS = 8
QK_NOPE = 128
QK_ROPE = 64
V_DIM = 128
Q_LORA = 512
KV_LORA = 256
ROPE_THETA = 10000.0
MLA_SCALE = (QK_NOPE + QK_ROPE) ** -0.5
QK_PAD = 256

HY_EMB = 33
HY_BANDS = (HY_EMB - 1) // 2
HY_HID = 64
FFT_N2 = 128
B_GROUP = 8

N_EXPERTS = 32
TOP_K = 4
SWIGLU_LIMIT = 7.0
SWIGLU_ALPHA = 1.702
MOE_TM = 512
LANES = 128

VMEM_LIMIT = 56 * 1024 * 1024


def _cp(sem, vmem=VMEM_LIMIT):
    return pltpu.CompilerParams(dimension_semantics=sem, vmem_limit_bytes=vmem)


def _dot(a, b):
    return jnp.dot(a, b, preferred_element_type=F32)


def _dot_hi(a, b):
    return jnp.dot(a, b, preferred_element_type=F32, precision=lax.Precision.HIGHEST)


def _rms(x, g):
    return x * lax.rsqrt(jnp.mean(x * x, axis=-1, keepdims=True) + EPS) * g


def _ada_body(c_ref, w_ref, b_ref, o_ref):
    c = c_ref[...]
    s = c * jax.nn.sigmoid(c)
    o_ref[0] = _dot(s.astype(BF16), w_ref[0].astype(BF16)) + b_ref[0]


def _ada(cond8, ada_w, ada_b):
    depth, d, n = ada_w.shape
    tn = n // 4
    return pl.pallas_call(
        _ada_body,
        grid=(depth, n // tn),
        in_specs=[
            pl.BlockSpec((8, d), lambda i, j: (0, 0)),
            pl.BlockSpec((1, d, tn), lambda i, j: (i, 0, j)),
            pl.BlockSpec((1, 1, tn), lambda i, j: (i, 0, j)),
        ],
        out_specs=pl.BlockSpec((1, 8, tn), lambda i, j: (i, 0, j)),
        out_shape=jax.ShapeDtypeStruct((depth, 8, n), F32),
        compiler_params=_cp(("arbitrary", "arbitrary")),
        name="ada_mod",
    )(cond8, ada_w, ada_b.reshape(depth, 1, n))


def _mla_proj_body(x_ref, g_ref, sh_ref, sc_ref, wd_ref, gq_ref, gkv_ref, wuk_ref, wuqT_ref, wuvT_ref,
                   ct_ref, st_ref, cT_ref, sT_ref, *out_refs, need_q, tk):
    if need_q:
        qT_ref, k_ref, vT_ref = out_refs
    else:
        k_ref, vT_ref = out_refs
    nh = MLA_HEADS
    x = x_ref[0]
    h = _rms(x, g_ref[...]) * (1.0 + sc_ref[0]) + sh_ref[0]
    lat = _dot(h.astype(BF16), wd_ref[...])
    o_kv = Q_LORA
    o_a = Q_LORA + KV_LORA
    kvn = _rms(lat[:, o_kv:o_a], gkv_ref[...])
    kr = (lat[:, o_a:o_a + LANES] * ct_ref[...] + lat[:, o_a + LANES:o_a + 2 * LANES] * st_ref[...]).astype(BF16)
    knope = _dot(kvn.astype(BF16), wuk_ref[...])
    for hh in range(nh):
        k_ref[0, hh, :, 0:QK_NOPE] = knope[:, hh * QK_NOPE:(hh + 1) * QK_NOPE].astype(BF16)
        k_ref[0, hh, :, QK_NOPE:QK_PAD] = kr
    vT = _dot(wuvT_ref[...], kvn.T.astype(BF16))
    tm = x.shape[0]
    for hh in range(nh):
        for c in range(tm // tk):
            vT_ref[0, hh, c] = vT[hh * V_DIM:(hh + 1) * V_DIM, c * tk:(c + 1) * tk].astype(BF16)
    if need_q:
        qn = _rms(lat[:, :Q_LORA], gq_ref[...])
        qT = _dot(wuqT_ref[...], qn.T.astype(BF16)) * (MLA_SCALE * math.log2(math.e))
        c = cT_ref[...]
        s = sT_ref[...]
        hw = QK_NOPE + QK_ROPE
        half = QK_ROPE // 2
        for hh in range(nh):
            base = hh * hw
            x1 = qT[base + QK_NOPE:base + QK_NOPE + half]
            x2 = qT[base + QK_NOPE + half:base + hw]
            qT_ref[0, hh, 0:QK_NOPE] = qT[base:base + QK_NOPE].astype(BF16)
            qT_ref[0, hh, QK_NOPE:QK_NOPE + half] = (x1 * c - x2 * s).astype(BF16)
            qT_ref[0, hh, QK_NOPE + half:hw] = (x1 * s + x2 * c).astype(BF16)
            qT_ref[0, hh, hw:QK_PAD] = jnp.zeros((QK_PAD - hw, tm), BF16)


def _mla_proj(x, g, sh, sc, wts, tabs, *, need_q, tm, tk):
    bsz, n, d = x.shape
    nh = MLA_HEADS
    wd, gq, gkv, wuk, wuqT, wuvT = wts
    ct, st, cT, sT = tabs
    nsh = sh.shape[0]
    full = lambda a: pl.BlockSpec(a.shape, lambda b, i: (0,) * a.ndim)
    in_specs = [
        pl.BlockSpec((1, tm, d), lambda b, i: (b, i, 0)),
        full(g),
        pl.BlockSpec((1, 1, d), lambda b, i: (b % nsh, 0, 0)),
        pl.BlockSpec((1, 1, d), lambda b, i: (b % nsh, 0, 0)),
        full(wd), full(gq), full(gkv), full(wuk), full(wuqT), full(wuvT),
        pl.BlockSpec((tm, LANES), lambda b, i: (i, 0)),
        pl.BlockSpec((tm, LANES), lambda b, i: (i, 0)),
        pl.BlockSpec((QK_ROPE // 2, tm), lambda b, i: (0, i)),
        pl.BlockSpec((QK_ROPE // 2, tm), lambda b, i: (0, i)),
    ]
    out_specs = [
        pl.BlockSpec((1, nh, tm, QK_PAD), lambda b, i: (b, 0, i, 0)),
        pl.BlockSpec((1, nh, tm // tk, V_DIM, tk), lambda b, i: (b, 0, i, 0, 0)),
    ]
    out_shape = [
        jax.ShapeDtypeStruct((bsz, nh, n, QK_PAD), BF16),
        jax.ShapeDtypeStruct((bsz, nh, n // tk, V_DIM, tk), BF16),
    ]
    if need_q:
        out_specs = [pl.BlockSpec((1, nh, QK_PAD, tm), lambda b, i: (b, 0, 0, i))] + out_specs
        out_shape = [jax.ShapeDtypeStruct((bsz, nh, QK_PAD, n), BF16)] + out_shape
    return pl.pallas_call(
        functools.partial(_mla_proj_body, need_q=need_q, tk=tk),
        grid=(bsz, n // tm),
        in_specs=in_specs,
        out_specs=out_specs,
        out_shape=out_shape,
        compiler_params=_cp(("arbitrary", "arbitrary")),
        name="mla_proj_q" if need_q else "mla_proj_ctx",
    )(x, g, sh, sc, wd, gq, gkv, wuk, wuqT, wuvT, ct, st, cT, sT)


SM_STRIP = 64
SUBLANES = 8


def _attn_body(qT_ref, k_ref, vT_ref, kc_ref, vTc_ref, o_ref, s0, s1, p0, p1, sc, pc, acc, m_scr, x0, x1, xc,
               a0, a1, ac, l_scr, d0, d1, dc, *, tk):
    nchunk = k_ref.shape[2] // tk

    def scores(kblk, s_ref, mx_ref):
        r = _dot(kblk, qT_ref[0, 0])
        s_ref[...] = r
        mx_ref[...] = jnp.max(r, axis=0, keepdims=True)

    def probs(s_ref, mx_ref, p_ref, a_ref, d_ref):
        m_old = m_scr[...]
        m_new = jnp.maximum(m_old, mx_ref[...])
        m_scr[...] = m_new
        alpha = jnp.exp2(m_old - m_new)
        a_ref[...] = alpha
        part = None
        for r in range(0, s_ref.shape[0], SM_STRIP):
            p = jnp.exp2(s_ref[r:r + SM_STRIP] - m_new)
            p_ref[r:r + SM_STRIP] = p.astype(BF16)
            ps = jnp.sum(p.reshape(SM_STRIP // SUBLANES, SUBLANES, p.shape[1]), axis=0)
            part = ps if part is None else part + ps
        d_ref[...] = part

    def accumulate(p_ref, a_ref, d_ref, vblk):
        acc[...] = a_ref[...] * acc[...] + _dot(vblk, p_ref[...])
        l_scr[...] = a_ref[...] * l_scr[...] + d_ref[...]

    def kchunk(i):
        return k_ref[0, 0, pl.ds(pl.multiple_of(i * tk, tk), tk), :]

    def vchunk(i):
        nsub = tk // vT_ref.shape[-1]
        return jnp.concatenate([vT_ref[0, 0, i * nsub + u] for u in range(nsub)], axis=1)

    m_scr[...] = jnp.full(m_scr.shape, -jnp.inf, F32)
    acc[...] = jnp.zeros(acc.shape, F32)
    l_scr[...] = jnp.zeros(l_scr.shape, F32)
    scores(kc_ref[0, 0], sc, xc)
    scores(kchunk(0), s0, x0)
    probs(sc, xc, pc, ac, dc)
    scores(kchunk(1), s1, x1)
    accumulate(pc, ac, dc, vTc_ref[0, 0, 0])
    probs(s0, x0, p0, a0, d0)

    def body(j, carry):
        t = 2 * j
        scores(kchunk(t + 2), s0, x0)
        accumulate(p0, a0, d0, vchunk(t))
        probs(s1, x1, p1, a1, d1)
        scores(kchunk(t + 3), s1, x1)
        accumulate(p1, a1, d1, vchunk(t + 1))
        probs(s0, x0, p0, a0, d0)
        return carry

    lax.fori_loop(0, nchunk // 2 - 1, body, 0)
    accumulate(p0, a0, d0, vchunk(nchunk - 2))
    probs(s1, x1, p1, a1, d1)
    accumulate(p1, a1, d1, vchunk(nchunk - 1))
    o_ref[0, 0] = (acc[...] / jnp.sum(l_scr[...], axis=0, keepdims=True)).astype(BF16)


def _attention(qT, k, vT, kc, vTc, *, tq, tk):
    bsz, nh, _, n = qT.shape
    nc = kc.shape[2]
    tv = vT.shape[-1]
    assert (n // tk) % 2 == 0 and tk % tv == 0
    return pl.pallas_call(
        functools.partial(_attn_body, tk=tk),
        grid=(bsz, nh, n // tq),
        in_specs=[
            pl.BlockSpec((1, 1, QK_PAD, tq), lambda b, h, i: (b, h, 0, i)),
            pl.BlockSpec((1, 1, n, QK_PAD), lambda b, h, i: (b, h, 0, 0)),
            pl.BlockSpec((1, 1, n // tv, V_DIM, tv), lambda b, h, i: (b, h, 0, 0, 0)),
            pl.BlockSpec((1, 1, nc, QK_PAD), lambda b, h, i: (b, h, 0, 0)),
            pl.BlockSpec((1, 1, 1, V_DIM, nc), lambda b, h, i: (b, h, 0, 0, 0)),
        ],
        out_specs=pl.BlockSpec((1, 1, V_DIM, tq), lambda b, h, i: (b, h, 0, i)),
        out_shape=jax.ShapeDtypeStruct((bsz, nh, V_DIM, n), BF16),
        scratch_shapes=[pltpu.VMEM((tk, tq), F32), pltpu.VMEM((tk, tq), F32),
                        pltpu.VMEM((tk, tq), BF16), pltpu.VMEM((tk, tq), BF16),
                        pltpu.VMEM((nc, tq), F32), pltpu.VMEM((nc, tq), BF16),
                        pltpu.VMEM((V_DIM, tq), F32), pltpu.VMEM((1, tq), F32),
                        pltpu.VMEM((1, tq), F32), pltpu.VMEM((1, tq), F32), pltpu.VMEM((1, tq), F32),
                        pltpu.VMEM((1, tq), F32), pltpu.VMEM((1, tq), F32), pltpu.VMEM((1, tq), F32),
                        pltpu.VMEM((SUBLANES, tq), F32), pltpu.VMEM((SUBLANES, tq), F32),
                        pltpu.VMEM((SUBLANES, tq), F32), pltpu.VMEM((SUBLANES, tq), F32)],
        compiler_params=_cp(("arbitrary", "arbitrary", "arbitrary")),
        name="mla_attention",
    )(qT, k, vT, kc, vTc)


def _post_body(*refs, transposed):
    if transposed:
        o_ref, *refs = refs
    else:
        o_ref, ob_ref, *refs = refs
    (wo_ref, bo_ref, x_ref, g1_ref, gf_ref, sh_ref, sc_ref, wrh_ref, wrl_ref, br_ref, tri_ref,
     xl_ref, fl_ref, ti_ref, gt_ref, rk_ref, cnt_ref) = refs

    @pl.when((pl.program_id(0) == 0) & (pl.program_id(1) == 0))
    def _():
        cnt_ref[...] = jnp.zeros_like(cnt_ref)

    tm = x_ref.shape[1]
    if transposed:
        oT = o_ref[0].astype(F32).reshape(MLA_HEADS * V_DIM, tm)
        o = oT.T.astype(BF16)
    else:
        o = _interleave_tiles(o_ref[0], ob_ref[0]).astype(BF16)
    y = _dot(o, wo_ref[...]) + bo_ref[...]
    xl = x_ref[0] + g1_ref[0] * y
    xl_ref[0] = xl
    fl = _rms(xl, gf_ref[...]) * (1.0 + sc_ref[0]) + sh_ref[0]
    _to_rows(fl_ref, fl)
    flh = fl.astype(BF16)
    fll = (fl - flh.astype(F32)).astype(BF16)
    logits = _dot(flh, wrh_ref[...]) + (_dot(fll, wrh_ref[...]) + _dot(flh, wrl_ref[...])) + br_ref[...]
    lane = lax.broadcasted_iota(jnp.int32, (tm, LANES), 1).astype(F32)
    neg = jnp.float32(-jnp.inf)
    work = jnp.where(lane < N_EXPERTS, logits, neg)
    vals, idxs = [], []
    onehot = jnp.zeros((tm, LANES), F32)
    for _ in range(TOP_K):
        mk = jnp.max(work, axis=-1, keepdims=True)
        ik = jnp.min(jnp.where(work == mk, lane, float(LANES)), axis=-1, keepdims=True)
        sel = lane == ik
        onehot = jnp.where(sel, 1.0, onehot)
        work = jnp.where(sel, neg, work)
        vals.append(mk)
        idxs.append(ik)
    es = [jnp.exp(v - vals[0]) for v in vals]
    den = es[0] + es[1] + es[2] + es[3]
    pre = _dot(tri_ref[...], onehot.astype(BF16)) + cnt_ref[...]
    ti = jnp.zeros((tm, LANES), F32)
    gt = jnp.zeros((tm, LANES), F32)
    rk = jnp.zeros((tm, LANES), F32)
    for kk in range(TOP_K):
        rank = jnp.sum(jnp.where(lane == idxs[kk], pre, 0.0), axis=-1, keepdims=True)
        ti = jnp.where(lane == kk, idxs[kk], ti)
        gt = jnp.where(lane == kk, es[kk] / den, gt)
        rk = jnp.where(lane == kk, rank, rk)
    ti_ref[...] = ti[:, :TOP_K].astype(jnp.int32)
    gt_ref[...] = gt[:, :TOP_K]
    rk_ref[...] = rk[:, :TOP_K].astype(jnp.int32)
    cnt_ref[...] += jnp.sum(onehot, axis=0, keepdims=True)


def _post(o, wo, bo, x, g1, gf, sh, sc, wrh, wrl, br, *, transposed, tm):
    bsz, n, d = x.shape
    t = bsz * n
    nt = n // tm
    tri = (lax.broadcasted_iota(jnp.int32, (tm, tm), 0) > lax.broadcasted_iota(jnp.int32, (tm, tm), 1)).astype(BF16)
    full = lambda a: pl.BlockSpec(a.shape, lambda b, i: (0,) * a.ndim)
    per_b = pl.BlockSpec((1, 1, d), lambda b, i: (b, 0, 0))
    if transposed:
        o_args = [o]
        o_specs = [pl.BlockSpec((1, MLA_HEADS, V_DIM, tm), lambda b, i: (b, 0, 0, i))]
    else:
        o_args = list(o)
        o_specs = [pl.BlockSpec((1, tm, d // 2), lambda b, i: (b, i, 0))] * 2
    tok = lambda w: pl.BlockSpec((tm, w), lambda b, i: (b * nt + i, 0))
    return pl.pallas_call(
        functools.partial(_post_body, transposed=transposed),
        grid=(bsz, nt),
        in_specs=[*o_specs, full(wo), full(bo), pl.BlockSpec((1, tm, d), lambda b, i: (b, i, 0)), per_b, full(gf),
                  per_b, per_b, full(wrh), full(wrl), full(br), full(tri)],
        out_specs=[pl.BlockSpec((1, tm, d), lambda b, i: (b, i, 0)),
                   pl.BlockSpec((tm * ROW_SUB, LANES), lambda b, i: (b * nt + i, 0)),
                   tok(TOP_K), tok(TOP_K), tok(TOP_K), pl.BlockSpec((1, LANES), lambda b, i: (0, 0))],
        out_shape=[jax.ShapeDtypeStruct((bsz, n, d), F32), jax.ShapeDtypeStruct((t * ROW_SUB, LANES), F32),
                   jax.ShapeDtypeStruct((t, TOP_K), jnp.int32), jax.ShapeDtypeStruct((t, TOP_K), F32),
                   jax.ShapeDtypeStruct((t, TOP_K), jnp.int32), jax.ShapeDtypeStruct((1, LANES), F32)],
        compiler_params=_cp(("arbitrary", "arbitrary")),
        name="post_attn" if transposed else "post_hyena",
    )(*o_args, wo, bo, x, g1, gf, sh, sc, wrh, wrl, br, tri)


ROW_SUB = 8


def _row_slice(i):
    return pl.ds(pl.multiple_of(i * ROW_SUB, ROW_SUB), ROW_SUB)


def _to_rows(ref, x):
    for s in range(ROW_SUB):
        ref[pl.ds(s, x.shape[0], stride=ROW_SUB), :] = x[:, s * LANES:(s + 1) * LANES]


def _from_rows(ref, lo, hi):
    return jnp.concatenate([ref[pl.ds(lo * ROW_SUB + s, hi - lo, stride=ROW_SUB), :] for s in range(ROW_SUB)], axis=1)


def _dispatch_body(pe_ref, pd_ref, dest_ref, fl_ref, xs_out, zbuf, sem, *, td):
    @pl.when(pl.program_id(0) == 0)
    def _():
        zbuf[...] = jnp.zeros(zbuf.shape, zbuf.dtype)
        for e in range(N_EXPERTS):
            @pl.when(pd_ref[e] > 0)
            def _():
                start = pl.multiple_of((pe_ref[e] - MOE_TM) * ROW_SUB, ROW_SUB)
                cp = pltpu.make_async_copy(zbuf, xs_out.at[pl.ds(start, MOE_TM * ROW_SUB)], sem)
                cp.start()
                cp.wait()

    def issue(t, carry):
        for kk in range(TOP_K):
            d = dest_ref[0, 0, t * TOP_K + kk]
            pltpu.make_async_copy(fl_ref.at[_row_slice(t)], xs_out.at[_row_slice(d)], sem).start(priority=kk % 2)
        return carry

    lax.fori_loop(0, td, issue, 0, unroll=2)

    def drain(t, carry):
        pltpu.make_async_copy(fl_ref.at[_row_slice(0)], xs_out.at[_row_slice(0)], sem).wait()
        return carry

    lax.fori_loop(0, td * TOP_K, drain, 0, unroll=8)


def _dispatch(pad_end, padded, dest, fl, n_rows, *, td):
    t = fl.shape[0] // ROW_SUB
    dest3 = dest.reshape(t // td, 1, td * TOP_K)
    grid_spec = pltpu.PrefetchScalarGridSpec(
        num_scalar_prefetch=2,
        grid=(t // td,),
        in_specs=[
            pl.BlockSpec((1, 1, td * TOP_K), lambda i, pe, pd: (i, 0, 0), memory_space=pltpu.SMEM),
            pl.BlockSpec((td * ROW_SUB, LANES), lambda i, pe, pd: (i, 0)),
        ],
        out_specs=pl.BlockSpec(memory_space=pl.ANY),
        scratch_shapes=[pltpu.VMEM((MOE_TM * ROW_SUB, LANES), fl.dtype), pltpu.SemaphoreType.DMA(())],
    )
    return pl.pallas_call(
        functools.partial(_dispatch_body, td=td),
        grid_spec=grid_spec,
        out_shape=jax.ShapeDtypeStruct((n_rows * ROW_SUB, LANES), fl.dtype),
        compiler_params=_cp(("arbitrary",)),
        name="moe_dispatch",
    )(pad_end, padded, dest3, fl)


def _expert_body(be_ref, nu_ref, xs_ref, win_ref, bin_ref, wout_ref, bout_ref, ys_ref, win_s, wout_s):
    b = pl.program_id(0)
    dff = wout_ref.shape[1]

    @pl.when(b < nu_ref[0])
    def _():
        prev = be_ref[jnp.maximum(b - 1, 0)]

        @pl.when((b == 0) | (prev != be_ref[b]))
        def _():
            win_s[...] = win_ref[0].astype(BF16)
            wout_s[...] = wout_ref[0].astype(BF16)

        x = _from_rows(xs_ref, 0, xs_ref.shape[0] // ROW_SUB).astype(BF16)
        gu = _dot(x, win_s[...]) + bin_ref[0]
        gate = jnp.minimum(gu[:, :dff], SWIGLU_LIMIT)
        lin = jnp.clip(gu[:, dff:], -SWIGLU_LIMIT, SWIGLU_LIMIT)
        act = gate * jax.nn.sigmoid(SWIGLU_ALPHA * gate) * (lin + 1.0)
        _to_rows(ys_ref, _dot(act.astype(BF16), wout_s[...]) + bout_ref[0])

    @pl.when(b >= nu_ref[0])
    def _():
        ys_ref[...] = jnp.zeros_like(ys_ref)


def _experts(blk_exp, n_used, xs, layer, w_in, b_in, w_out, b_out):
    n_rows = xs.shape[0] // ROW_SUB
    depth, ne, d, f2 = w_in.shape
    dff = w_out.shape[2]
    tm = MOE_TM
    grid_spec = pltpu.PrefetchScalarGridSpec(
        num_scalar_prefetch=2,
        grid=(n_rows // tm,),
        in_specs=[
            pl.BlockSpec((tm * ROW_SUB, LANES), lambda b, be, nu: (jnp.minimum(b, nu[0] - 1), 0)),
            pl.BlockSpec((None, 1, d, f2), lambda b, be, nu: (layer, be[b], 0, 0)),
            pl.BlockSpec((None, 1, 1, f2), lambda b, be, nu: (layer, be[b], 0, 0)),
            pl.BlockSpec((None, 1, dff, d), lambda b, be, nu: (layer, be[b], 0, 0)),
            pl.BlockSpec((None, 1, 1, d), lambda b, be, nu: (layer, be[b], 0, 0)),
        ],
        out_specs=pl.BlockSpec((tm * ROW_SUB, LANES), lambda b, be, nu: (b, 0)),
        scratch_shapes=[pltpu.VMEM((d, f2), BF16), pltpu.VMEM((dff, d), BF16)],
    )
    return pl.pallas_call(
        _expert_body,
        grid_spec=grid_spec,
        out_shape=jax.ShapeDtypeStruct(xs.shape, F32),
        compiler_params=_cp(("arbitrary",)),
        name="moe_experts",
    )(blk_exp, n_used, xs, w_in, b_in.reshape(depth, ne, 1, f2), w_out, b_out.reshape(depth, ne, 1, d))


def _combine_body(dest_ref, ys_hbm, gt_ref, xl_ref, g2_ref, fg_ref, out_ref, buf, sem, *, tc, final):
    def issue(t, carry):
        for kk in range(TOP_K):
            d = dest_ref[0, 0, t * TOP_K + kk]
            pltpu.make_async_copy(ys_hbm.at[_row_slice(d)], buf.at[_row_slice(kk * tc + t)], sem).start(
                priority=kk % 2)
        return carry

    lax.fori_loop(0, tc, issue, 0, unroll=2)

    def drain(t, carry):
        pltpu.make_async_copy(ys_hbm.at[_row_slice(0)], buf.at[_row_slice(0)], sem).wait()
        return carry

    lax.fori_loop(0, tc * TOP_K, drain, 0, unroll=8)
    gt = gt_ref[...]
    y = gt[:, 0:1] * _from_rows(buf, 0, tc)
    for kk in range(1, TOP_K):
        y = y + gt[:, kk:kk + 1] * _from_rows(buf, kk * tc, (kk + 1) * tc)
    xl = xl_ref[0] + g2_ref[0] * y
    out_ref[0] = _rms(xl, fg_ref[...]) if final else xl


def _combine(dest, ys, gates, xl, g2, fg, *, tc, final):
    bsz, n, d = xl.shape
    t = bsz * n
    nt = n // tc
    dest3 = dest.reshape(t // tc, 1, tc * TOP_K)
    return pl.pallas_call(
        functools.partial(_combine_body, tc=tc, final=final),
        grid=(bsz, nt),
        in_specs=[
            pl.BlockSpec((1, 1, tc * TOP_K), lambda b, i: (b * nt + i, 0, 0), memory_space=pltpu.SMEM),
            pl.BlockSpec(memory_space=pl.ANY),
            pl.BlockSpec((tc, TOP_K), lambda b, i: (b * nt + i, 0)),
            pl.BlockSpec((1, tc, d), lambda b, i: (b, i, 0)),
            pl.BlockSpec((1, 1, d), lambda b, i: (b, 0, 0)),
            pl.BlockSpec((1, d), lambda b, i: (0, 0)),
        ],
        out_specs=pl.BlockSpec((1, tc, d), lambda b, i: (b, i, 0)),
        out_shape=jax.ShapeDtypeStruct((bsz, n, d), F32),
        scratch_shapes=[pltpu.VMEM((TOP_K * tc * ROW_SUB, LANES), F32), pltpu.SemaphoreType.DMA(())],
        compiler_params=_cp(("arbitrary", "arbitrary")),
        name="moe_combine",
    )(dest3, ys, gates, xl, g2, fg)


def _moe(fl, topi, gates, rank, cnt, xl, g2, fg, layer, w_in, b_in, w_out, b_out, *, final):
    t = fl.shape[0] // ROW_SUB
    tm = MOE_TM
    counts = cnt[0, :N_EXPERTS].astype(jnp.int32)
    padded = (counts + tm - 1) // tm * tm
    pad_end = jnp.cumsum(padded)
    pad_start = pad_end - padded
    dest = jnp.take(pad_start, topi) + rank
    nb = t * TOP_K // tm + N_EXPERTS
    blk_start = jnp.arange(nb, dtype=jnp.int32) * tm
    blk_exp = jnp.minimum(jnp.sum((pad_end[None, :] <= blk_start[:, None]).astype(jnp.int32), axis=1), N_EXPERTS - 1)
    n_used = (pad_end[-1:] // tm).astype(jnp.int32)
    xs = _dispatch(pad_end, padded, dest, fl, nb * tm, td=512)
    ys = _experts(blk_exp, n_used, xs, layer, w_in, b_in, w_out, b_out)
    return _combine(dest, ys, gates, xl, g2, fg, tc=512, final=final)


def _hy_in_body(x_ref, xp_ref, xn_ref, g_ref, sh_ref, sc_ref, w_ref, b_ref, cw_ref, cb_ref, o_ref, *, nt):
    i = pl.program_id(1)
    d = x_ref.shape[2]

    def normed(xx):
        return (_rms(xx, g_ref[...]) * (1.0 + sc_ref[0]) + sh_ref[0]).astype(BF16)

    h = normed(x_ref[0])
    hh = normed(jnp.concatenate([xp_ref[0], xn_ref[0]], axis=0))
    tm = h.shape[0]
    row = lax.broadcasted_iota(jnp.int32, (tm, 1), 0)
    for j in range(3):
        cols = slice(j * d, (j + 1) * d)
        p = _dot(h, w_ref[:, cols]) + b_ref[:, cols]
        ph = _dot(hh, w_ref[:, cols]) + b_ref[:, cols]
        prev = jnp.where(i > 0, ph[7:8], 0.0)
        nxt = jnp.where(i < nt - 1, ph[8:9], 0.0)
        up = jnp.where(row == 0, prev, pltpu.roll(p, 1, axis=0))
        dn = jnp.where(row == tm - 1, nxt, pltpu.roll(p, tm - 1, axis=0))
        cw = cw_ref[:, cols]
        o_ref[j, 0] = up * cw[0:1] + p * cw[1:2] + dn * cw[2:3] + cb_ref[:, cols]


def _hy_in(x, g, sh, sc, w, b, cw, cb, *, tm):
    bsz, n, d = x.shape
    nt = n // tm
    hb = tm // 8
    per_b = pl.BlockSpec((1, 1, d), lambda bb, i: (bb, 0, 0))
    full = lambda a: pl.BlockSpec(a.shape, lambda bb, i: (0,) * a.ndim)
    return pl.pallas_call(
        functools.partial(_hy_in_body, nt=nt),
        grid=(bsz, nt),
        in_specs=[
            pl.BlockSpec((1, tm, d), lambda bb, i: (bb, i, 0)),
            pl.BlockSpec((1, 8, d), lambda bb, i: (bb, jnp.maximum(i * hb - 1, 0), 0)),
            pl.BlockSpec((1, 8, d), lambda bb, i: (bb, jnp.minimum((i + 1) * hb, n // 8 - 1), 0)),
            full(g), per_b, per_b, full(w), full(b), full(cw), full(cb),
        ],
        out_specs=pl.BlockSpec((3, 1, tm, d), lambda bb, i: (0, bb, i, 0)),
        out_shape=jax.ShapeDtypeStruct((3, bsz, n, d), F32),
        compiler_params=_cp(("arbitrary", "arbitrary")),
        name="hyena_in_proj",
    )(x, x, x, g, sh, sc, w, b, cw, cb)


def _filt_feat_body(w1_ref, b1_ref, f1_ref, w2_ref, b2_ref, f2_ref, o_ref, *, n_lat):
    na = o_ref.shape[1]
    a = lax.broadcasted_iota(jnp.int32, (na, 1), 0)
    lane = lax.broadcasted_iota(jnp.int32, (na, LANES), 1)
    band_idx = jnp.where(lane <= HY_BANDS, lane - 1, lane - 1 - HY_BANDS).astype(F32)
    band = 1e-4 + band_idx * ((HY_BANDS - 1 - 1e-4) / (HY_BANDS - 1))
    phase = jnp.where(lane > HY_BANDS, 0.5 * math.pi, 0.0)
    for j in range(B_GROUP):
        r = a * FFT_N2 + (pl.program_id(0) * B_GROUP + j)
        pos = jnp.where(r < n_lat, r, 2 * n_lat - r).astype(F32)
        tn = pos / float(max(n_lat - 1, 1))
        ang = ((2.0 * math.pi / n_lat) * pos) * band
        z = jnp.where(lane == 0, tn, jnp.where(lane < HY_EMB, jnp.cos(ang + phase), 0.0))
        h1 = jnp.sin(f1_ref[...] * (_dot_hi(z, w1_ref[...]) + b1_ref[...]))
        h2 = jnp.sin(f2_ref[...] * (_dot_hi(h1, w2_ref[...]) + b2_ref[...]))
        valid = (r != n_lat).astype(F32)
        o_ref[j] = jnp.where(lane == HY_HID, tn, jnp.where(lane == HY_HID + 1, valid, h2))


def _filt_feat(w1, b1, f1, w2, b2, f2, *, n_lat):
    na = 2 * n_lat // FFT_N2
    w1p = jnp.zeros((LANES, LANES), F32).at[:HY_EMB, :HY_HID].set(w1)
    w2p = jnp.zeros((LANES, LANES), F32).at[:HY_HID, :HY_HID].set(w2)
    padv = lambda v: jnp.zeros((1, LANES), F32).at[0, :HY_HID].set(v)
    full = lambda shp: pl.BlockSpec(shp, lambda i: (0,) * len(shp))
    return pl.pallas_call(
        functools.partial(_filt_feat_body, n_lat=n_lat),
        grid=(FFT_N2 // B_GROUP,),
        in_specs=[full((LANES, LANES)), full((1, LANES)), full((1, LANES)),
                  full((LANES, LANES)), full((1, LANES)), full((1, LANES))],
        out_specs=pl.BlockSpec((B_GROUP, na, LANES), lambda i: (i, 0, 0)),
        out_shape=jax.ShapeDtypeStruct((FFT_N2, na, LANES), F32),
        compiler_params=_cp(("arbitrary",)),
        name="hyena_filter_features",
    )(w1p, padv(b1), padv(f1), w2p, padv(b2), padv(f2))


U32 = jnp.uint32
HI16 = 0xFFFF0000


def _pack_c(re, im):
    lo = lax.bitcast_convert_type(re.astype(BF16).astype(F32), U32) >> 16
    hi = lax.bitcast_convert_type(im.astype(BF16).astype(F32), U32) & U32(HI16)
    return hi | lo


def _unpack_c(u):
    re = lax.bitcast_convert_type(u << 16, F32)
    im = lax.bitcast_convert_type(u & U32(HI16), F32)
    return jnp.concatenate([re, im], axis=0).astype(BF16)


def _interleave_tiles(xa, xb):
    tiles = []
    for c0 in range(0, xa.shape[-1], LANES):
        tiles += [xa[..., c0:c0 + LANES], xb[..., c0:c0 + LANES]]
    return jnp.concatenate(tiles, axis=-1)


def _filt_s1_body(hd_ref, w3_ref, dec_ref, tab_ref, oa_ref, ob_ref):
    na = hd_ref.shape[1]
    ha = na // 2
    oa2, ob2 = _rows2d(oa_ref), _rows2d(ob_ref)
    ft = hd_ref[:, :ha, :].reshape(B_GROUP * ha, LANES)
    fb = hd_ref[:, ha:, :].reshape(B_GROUP * ha, LANES)
    top = _dot(ft.astype(BF16), w3_ref[0, 0].astype(BF16))
    top = top * jnp.exp(-ft[:, HY_HID:HY_HID + 1] * jnp.abs(dec_ref[0, 0]))
    bot = _dot(fb.astype(BF16), w3_ref[0, 1].astype(BF16))
    bot = bot * (jnp.exp(-fb[:, HY_HID:HY_HID + 1] * jnp.abs(dec_ref[0, 1])) * fb[:, HY_HID + 1:HY_HID + 2])
    for j in range(B_GROUP):
        hb = jnp.concatenate([top[j * ha:(j + 1) * ha], bot[j * ha:(j + 1) * ha]], axis=0).astype(BF16)
        r = _dot(tab_ref[j], hb)
        packed = _pack_c(r[:na], r[na:])
        oa2[pl.ds(j, na, stride=B_GROUP), :] = packed[:, :LANES]
        ob2[pl.ds(j, na, stride=B_GROUP), :] = packed[:, LANES:]


def _filt_s1(hd, w3r, dec, tab):
    _, na, _ = hd.shape
    d = w3r.shape[-1]
    ct = 2 * LANES
    half = pl.BlockSpec((None, na, B_GROUP, LANES), lambda o, g, c: (o, 0, g, c))
    return pl.pallas_call(
        _filt_s1_body,
        grid=(2, FFT_N2 // B_GROUP, d // ct),
        in_specs=[
            pl.BlockSpec((B_GROUP, na, LANES), lambda o, g, c: (g, 0, 0)),
            pl.BlockSpec((1, 2, LANES, ct), lambda o, g, c: (o, 0, 0, c)),
            pl.BlockSpec((1, 2, 1, ct), lambda o, g, c: (o, 0, 0, c)),
            pl.BlockSpec((B_GROUP, 2 * na, na), lambda o, g, c: (g, 0, 0)),
        ],
        out_specs=[half, half],
        out_shape=[jax.ShapeDtypeStruct((2, na, FFT_N2, d // 2), U32)] * 2,
        compiler_params=_cp(("arbitrary", "arbitrary", "arbitrary")),
        name="hyena_filter_dft1",
    )(hd, w3r, dec, tab)


S2_KB = 4


def _s2_body(*refs, conv):
    if conv:
        oa_ref, ob_ref, kf_ref, ff_ref, fi_ref, g_ref = refs
    else:
        oa_ref, ob_ref, ff_ref, g_ref = refs
    for u in range(S2_KB):
        xf = _dot(ff_ref[...], _unpack_c(_interleave_tiles(oa_ref[u], ob_ref[u])))
        if conv:
            xr, xi = xf[:FFT_N2], xf[FFT_N2:]
            kr = kf_ref[0, u].astype(F32)
            ki = kf_ref[1, u].astype(F32)
            y = jnp.concatenate([xr * kr - xi * ki, xr * ki + xi * kr], axis=0).astype(BF16)
            xf = _dot(fi_ref[...], y)
            g_ref[u] = _pack_c(xf[:FFT_N2], xf[FFT_N2:])
        else:
            g_ref[0, u] = xf[:FFT_N2].astype(BF16)
            g_ref[1, u] = xf[FFT_N2:].astype(BF16)


def _s2(o3, kf, order, ff, fi, *, ct, conv):
    n1 = o3[0].shape[-3]
    d = 2 * o3[0].shape[-1]
    full = lambda a: pl.BlockSpec(a.shape, lambda k, c: (0,) * a.ndim)
    nk = n1 // S2_KB
    if conv:
        hblk = pl.BlockSpec((S2_KB, FFT_N2, ct // 2), lambda k, c: (k, 0, c))
        in_specs = [hblk, hblk, pl.BlockSpec((None, 2, S2_KB, FFT_N2, ct), lambda k, c: (order, 0, k, 0, c)),
                    full(ff), full(fi)]
        args = (o3[0], o3[1], kf, ff, fi)
        grid = (nk, d // ct)
        out_specs = pl.BlockSpec((S2_KB, FFT_N2, ct), lambda k, c: (k, 0, c))
        out_shape = jax.ShapeDtypeStruct((n1, FFT_N2, d), U32)
    else:
        no = o3[0].shape[0]
        hblk = pl.BlockSpec((None, S2_KB, FFT_N2, ct // 2), lambda k, c: (k // nk, k % nk, 0, c))
        in_specs = [hblk, hblk, full(ff)]
        args = (o3[0], o3[1], ff)
        grid = (no * nk, d // ct)
        out_specs = pl.BlockSpec((None, 2, S2_KB, FFT_N2, ct), lambda k, c: (k // nk, 0, k % nk, 0, c))
        out_shape = jax.ShapeDtypeStruct((no, 2, n1, FFT_N2, d), BF16)
    return pl.pallas_call(
        functools.partial(_s2_body, conv=conv),
        grid=grid,
        in_specs=in_specs,
        out_specs=out_specs,
        out_shape=out_shape,
        compiler_params=_cp(("arbitrary", "arbitrary")),
        name="hyena_conv_dft2" if conv else "hyena_filter_dft2",
    )(*args)


def _rows2d(ref):
    lead = ref.shape[:-3]
    return ref.reshape(lead + (ref.shape[-3] * B_GROUP, ref.shape[-1]))


def _s1_body(za_ref, zb_ref, tab_ref, oa_ref, ob_ref):
    rows, n1 = za_ref.shape[0], oa_ref.shape[0]
    za2, zb2, oa2, ob2 = _rows2d(za_ref), _rows2d(zb_ref), _rows2d(oa_ref), _rows2d(ob_ref)
    for j in range(B_GROUP):
        sl = pl.ds(j, rows, stride=B_GROUP)
        zj = jnp.concatenate([za2[sl, :], zb2[sl, :]], axis=1)
        r = _dot(tab_ref[j], zj.astype(BF16))
        packed = _pack_c(r[:n1], r[n1:])
        oa2[pl.ds(j, n1, stride=B_GROUP), :] = packed[:, :LANES]
        ob2[pl.ds(j, n1, stride=B_GROUP), :] = packed[:, LANES:]


def _s1(z4, zi, tab):
    _, rows, _, d = z4.shape
    n1 = tab.shape[1] // 2
    half = pl.BlockSpec((n1, B_GROUP, LANES), lambda g, c: (0, g, c))
    return pl.pallas_call(
        _s1_body,
        grid=(FFT_N2 // B_GROUP, d // (2 * LANES)),
        in_specs=[
            pl.BlockSpec((None, rows, B_GROUP, LANES), lambda g, c: (zi, 0, g, 2 * c)),
            pl.BlockSpec((None, rows, B_GROUP, LANES), lambda g, c: (zi, 0, g, 2 * c + 1)),
            pl.BlockSpec((B_GROUP, 2 * n1, rows), lambda g, c: (g, 0, 0)),
        ],
        out_specs=[half, half],
        out_shape=[jax.ShapeDtypeStruct((n1, FFT_N2, d // 2), U32)] * 2,
        compiler_params=_cp(("arbitrary", "arbitrary")),
        name="hyena_conv_dft1",
    )(z4, z4, tab)


def _s3_body(ga_ref, gb_ref, tab_ref, gta_ref, gtb_ref, za_ref, zb_ref, fb_ref, *rest, chain):
    if chain:
        tab1_ref, oa_ref, ob_ref, qa_ref, qb_ref = rest
    else:
        oa_ref, ob_ref = rest
    n1, rows = ga_ref.shape[0], oa_ref.shape[0]
    ga2, gb2, oa2, ob2 = _rows2d(ga_ref), _rows2d(gb_ref), _rows2d(oa_ref), _rows2d(ob_ref)
    for j in range(B_GROUP):
        sl = pl.ds(j, n1, stride=B_GROUP)
        gj = _unpack_c(jnp.concatenate([ga2[sl, :], gb2[sl, :]], axis=1))
        y = _dot(tab_ref[j], gj)
        oa2[pl.ds(j, rows, stride=B_GROUP), :] = y[:, :LANES]
        ob2[pl.ds(j, rows, stride=B_GROUP), :] = y[:, LANES:]
    fb = fb_ref[...]
    oa_ref[...] = gta_ref[...] * (oa_ref[...] + za_ref[...] * fb[:, :, :LANES])
    ob_ref[...] = gtb_ref[...] * (ob_ref[...] + zb_ref[...] * fb[:, :, LANES:])
    if chain:
        qa2, qb2 = _rows2d(qa_ref), _rows2d(qb_ref)
        for j in range(B_GROUP):
            sl = pl.ds(j, rows, stride=B_GROUP)
            zj = jnp.concatenate([oa2[sl, :], ob2[sl, :]], axis=1)
            r = _dot(tab1_ref[j], zj.astype(BF16))
            packed = _pack_c(r[:n1], r[n1:])
            qa2[pl.ds(j, n1, stride=B_GROUP), :] = packed[:, :LANES]
            qb2[pl.ds(j, n1, stride=B_GROUP), :] = packed[:, LANES:]


def _s3(g3, tab, gate4, gi, zsrc, fb, tab1):
    n1, _, d = g3.shape
    rows = tab.shape[1]
    chain = tab1 is not None
    half = pl.BlockSpec((rows, B_GROUP, LANES), lambda g, c: (0, g, c))
    chalf = pl.BlockSpec((n1, B_GROUP, LANES), lambda g, c: (0, g, c))
    nat = lambda idx, par: pl.BlockSpec((None, rows, B_GROUP, LANES), lambda g, c: (idx, 0, g, 2 * c + par))
    if isinstance(zsrc[1], int):
        z_specs, z_args = [nat(zsrc[1], 0), nat(zsrc[1], 1)], [zsrc[0], zsrc[0]]
    else:
        z_specs, z_args = [half, half], list(zsrc)
    in_specs = [
        pl.BlockSpec((n1, B_GROUP, LANES), lambda g, c: (0, g, 2 * c)),
        pl.BlockSpec((n1, B_GROUP, LANES), lambda g, c: (0, g, 2 * c + 1)),
        pl.BlockSpec((B_GROUP, rows, 2 * n1), lambda g, c: (g, 0, 0)),
        nat(gi, 0), nat(gi, 1), *z_specs,
        pl.BlockSpec((1, 1, 2 * LANES), lambda g, c: (0, 0, c)),
    ]
    args = [g3, g3, tab, gate4, gate4, *z_args, fb]
    zshape = jax.ShapeDtypeStruct((rows, FFT_N2, d // 2), F32)
    out_specs, out_shape = [half, half], [zshape, zshape]
    if chain:
        in_specs.append(pl.BlockSpec((B_GROUP, 2 * n1, rows), lambda g, c: (g, 0, 0)))
        args.append(tab1)
        out_specs += [chalf, chalf]
        out_shape += [jax.ShapeDtypeStruct((n1, FFT_N2, d // 2), U32)] * 2
    return pl.pallas_call(
        functools.partial(_s3_body, chain=chain),
        grid=(FFT_N2 // B_GROUP, d // (2 * LANES)),
        in_specs=in_specs,
        out_specs=out_specs,
        out_shape=out_shape,
        compiler_params=_cp(("arbitrary", "arbitrary")),
        name="hyena_conv_idft1_dft1" if chain else "hyena_conv_idft1",
    )(*args)


def _dft_tables(n_lat):
    n = 2 * n_lat
    n1 = n // FFT_N2
    k1 = jnp.arange(n1, dtype=jnp.int32)
    th_a = ((k1[:, None] * k1[None, :]) % n1).astype(F32) * (2.0 * math.pi / n1)
    th_b = (jnp.arange(FFT_N2, dtype=jnp.int32)[:, None] * k1[None, :]).astype(F32) * (2.0 * math.pi / n)
    ca, sa = jnp.cos(th_a)[None], jnp.sin(th_a)[None]
    cb, sb = jnp.cos(th_b)[:, :, None], jnp.sin(th_b)[:, :, None]
    cr = ca * cb - sa * sb
    sn = sa * cb + ca * sb
    ha = n1 // 2
    crh, snh = cr[:, :, :ha], sn[:, :, :ha]
    w1 = jnp.concatenate([jnp.concatenate([crh, snh], axis=2), jnp.concatenate([-snh, crh], axis=2)], axis=1)
    w1f = jnp.concatenate([cr, -sn], axis=1)
    v = jnp.swapaxes(w1, 1, 2) * (1.0 / n)
    k2 = jnp.arange(FFT_N2, dtype=jnp.int32)
    th2 = ((k2[:, None] * k2[None, :]) % FFT_N2).astype(F32) * (2.0 * math.pi / FFT_N2)
    c2, s2 = jnp.cos(th2), jnp.sin(th2)
    ff = jnp.concatenate([jnp.concatenate([c2, s2], axis=1), jnp.concatenate([-s2, c2], axis=1)], axis=0)
    fi = jnp.concatenate([jnp.concatenate([c2, -s2], axis=1), jnp.concatenate([s2, c2], axis=1)], axis=0)
    return w1.astype(BF16), w1f.astype(BF16), v.astype(BF16), ff.astype(BF16), fi.astype(BF16)


def _hyena_mix(proj3, fparams, fbias, *, n_lat):
    _, bsz, _, d = proj3.shape
    f_w1, f_b1, f_f1, f_w2, f_b2, f_f2, f_w3, decay = fparams
    w1, w1f, v, ff, fi = _dft_tables(n_lat)
    hd = _filt_feat(f_w1, f_b1, f_f1, f_w2, f_b2, f_f2, n_lat=n_lat)
    w3r = jnp.transpose(f_w3.reshape(HY_HID, 2, 2, d), (1, 2, 0, 3))
    w3r = jnp.zeros((2, 2, LANES, d), F32).at[:, :, :HY_HID].set(w3r)
    kf1 = _filt_s1(hd, w3r, decay.reshape(2, 2, 1, d), w1f)
    kf = _s2(kf1, None, 0, ff, None, ct=d, conv=False)
    p3 = proj3.reshape(3, bsz * (n_lat // FFT_N2), FFT_N2, d)
    o1 = _s1(p3, 2, w1)
    g = _s2(o1, kf, 0, ff, fi, ct=d, conv=True)
    za, zb, *o1 = _s3(g, v, p3, 0, (p3, 2), fbias[0].reshape(1, 1, d), w1)
    g = _s2(o1, kf, 1, ff, fi, ct=d, conv=True)
    za, zb = _s3(g, v, p3, 1, (za, zb), fbias[1].reshape(1, 1, d), None)
    return za.reshape(bsz, n_lat, d // 2), zb.reshape(bsz, n_lat, d // 2)


def _rope_tables(n_tokens):
    rows = n_tokens // GRID_W
    row = jnp.broadcast_to(jnp.arange(rows, dtype=F32)[:, None], (rows, GRID_W)).reshape(-1)
    col = jnp.broadcast_to(jnp.arange(GRID_W, dtype=F32)[None, :], (rows, GRID_W)).reshape(-1)
    axis_dim = QK_ROPE // 2
    inv_freq = 1.0 / (ROPE_THETA ** (jnp.arange(0, axis_dim, 2, dtype=F32) / axis_dim))
    ang = jnp.concatenate([row[:, None] * inv_freq, col[:, None] * inv_freq], axis=-1)
    return jnp.cos(ang), jnp.sin(ang)


def _mla_weights(w_down, g_q, w_uq, g_kv, w_ukv):
    d = w_down.shape[0]
    nh = MLA_HEADS
    kpe = w_down[:, Q_LORA + KV_LORA:]
    w1, w2 = kpe[:, 0::2], kpe[:, 1::2]
    z = jnp.zeros((d, LANES - QK_ROPE), w_down.dtype)
    wd = jnp.concatenate([w_down[:, :Q_LORA + KV_LORA], w1, w2, z, w2, w1, z], axis=1).astype(BF16)
    uq = w_uq.reshape(Q_LORA, nh, QK_NOPE + QK_ROPE)
    pe = uq[:, :, QK_NOPE:]
    uq = jnp.concatenate([uq[:, :, :QK_NOPE], pe[:, :, 0::2], pe[:, :, 1::2]], axis=2)
    wuqT = uq.reshape(Q_LORA, nh * (QK_NOPE + QK_ROPE)).T.astype(BF16)
    ukv = w_ukv.reshape(KV_LORA, nh, QK_NOPE + V_DIM)
    wuk = ukv[:, :, :QK_NOPE].reshape(KV_LORA, nh * QK_NOPE).astype(BF16)
    wuvT = ukv[:, :, QK_NOPE:].reshape(KV_LORA, nh * V_DIM).T.astype(BF16)
    return wd, g_q.reshape(1, -1), g_kv.reshape(1, -1), wuk, wuqT, wuvT


def kernel(x, c, ctx, c_ctx, ada_w, ada_b, norm_mix_g, norm_ffn_g, mla_w_down, mla_g_q, mla_w_uq, mla_g_kv, mla_w_ukv, mla_w_o, hy_w_in, hy_b_in, hy_conv_w, hy_conv_b, hy_f_w1, hy_f_b1, hy_f_freq1, hy_f_w2, hy_f_b2, hy_f_freq2, hy_f_w3, hy_decay, hy_bias, hy_w_out, hy_b_out, moe_w_r, moe_b_r, moe_w_in, moe_b_in, moe_w_out, moe_b_out, final_g):
    bsz, n_lat, d = x.shape
    n_ctx = ctx.shape[1]
    depth = ada_w.shape[0]
    assert bsz == 2 and d == MLA_HEADS * V_DIM and n_lat % 512 == 0 and n_ctx % 128 == 0
    assert depth == 2

    cond8 = jnp.zeros((8, d), F32).at[:bsz].set(c).at[bsz].set(c_ctx)
    mods = _ada(cond8, ada_w, ada_b)

    def mod(i, j, rows):
        return mods[i, rows, j * d:(j + 1) * d][:, None, :]

    lat_rows = slice(0, bsz)
    ctx_rows = slice(bsz, bsz + 1)
    xl = x
    for i in range(depth):
        kind, j = i % 2, i // 2
        sh1, sc1, g1 = (mod(i, m, lat_rows) for m in range(3))
        sh2, sc2, g2 = (mod(i, m, lat_rows) for m in range(3, 6))
        gm = norm_mix_g[i].reshape(1, d)
        if kind == 0:
            wts = _mla_weights(mla_w_down[j], mla_g_q[j], mla_w_uq[j], mla_g_kv[j], mla_w_ukv[j])
            cos, sin = _rope_tables(n_lat)
            zl = jnp.zeros((n_lat, LANES - QK_ROPE), F32)
            tabs = (jnp.concatenate([cos, cos, zl], axis=1), jnp.concatenate([-sin, sin, zl], axis=1), cos.T, sin.T)
            tq = tv = 512
            tk = 2048 if n_lat % 4096 == 0 else 512
            qT, k, vT = _mla_proj(xl, gm, sh1, sc1, wts, tabs, need_q=True, tm=tv, tk=tv)
            half = QK_ROPE // 2
            one_c = jnp.concatenate([jnp.ones((n_ctx, QK_ROPE), F32), jnp.zeros((n_ctx, LANES - QK_ROPE), F32)], axis=1)
            tabs_c = (one_c, jnp.zeros((n_ctx, LANES), F32), jnp.ones((half, n_ctx), F32), jnp.zeros((half, n_ctx), F32))
            kc, vTc = _mla_proj(ctx, gm, mod(i, 0, ctx_rows), mod(i, 1, ctx_rows), wts, tabs_c,
                                need_q=False, tm=n_ctx, tk=n_ctx)
            o = _attention(qT, k, vT, kc, vTc, tq=tq, tk=tk)
            wo = mla_w_o[j].astype(BF16)
            bo = jnp.zeros((1, d), F32)
            transposed = True
        else:
            proj3 = _hy_in(xl, gm, sh1, sc1, hy_w_in[j].astype(BF16), hy_b_in[j].reshape(1, -1), hy_conv_w[j],
                           hy_conv_b[j].reshape(1, -1), tm=512)
            fparams = (hy_f_w1[j], hy_f_b1[j], hy_f_freq1[j], hy_f_w2[j], hy_f_b2[j], hy_f_freq2[j], hy_f_w3[j],
                       hy_decay[j])
            o = _hyena_mix(proj3, fparams, hy_bias[j], n_lat=n_lat)
            wo = hy_w_out[j].astype(BF16)
            bo = hy_b_out[j].reshape(1, d)
            transposed = False
        wr = jnp.zeros((d, LANES), F32).at[:, :N_EXPERTS].set(moe_w_r[i])
        wrh = wr.astype(BF16)
        wrl = (wr - wrh.astype(F32)).astype(BF16)
        br = jnp.zeros((1, LANES), F32).at[0, :N_EXPERTS].set(moe_b_r[i])
        xl, fl, topi, gates, rank, cnt = _post(o, wo, bo, xl, g1, norm_ffn_g[i].reshape(1, d), sh2, sc2, wrh, wrl, br,
                                               transposed=transposed, tm=512)
        xl = _moe(fl, topi, gates, rank, cnt, xl, g2, final_g.reshape(1, d), i, moe_w_in, moe_b_in,
                  moe_w_out, moe_b_out, final=(i == depth - 1))
    return xl
```

```python
import functools
import math

import jax
import jax.numpy as jnp
from jax import lax
from jax.experimental import pallas as pl
from jax.experimental.pallas import tpu as pltpu

F32 = jnp.float32
BF16 = jnp.bfloat16

EPS = 1e-6
GRID_W = 64
MLA_HEADS = 8
QK_NOPE = 128
QK_ROPE = 64
V_DIM = 128
Q_LORA = 512
KV_LORA = 256
ROPE_THETA = 10000.0
MLA_SCALE = (QK_NOPE + QK_ROPE) ** -0.5
QK_PAD = 256

HY_EMB = 33
HY_BANDS = (HY_EMB - 1) // 2
HY_HID = 64
FFT_N2 = 128
B_GROUP = 8

N_EXPERTS = 32
TOP_K = 4
SWIGLU_LIMIT = 7.0
SWIGLU_ALPHA = 1.702
MOE_TM = 512
LANES = 128

VMEM_LIMIT = 56 * 1024 * 1024


def _cp(sem, vmem=VMEM_LIMIT):
    return pltpu.CompilerParams(dimension_semantics=sem, vmem_limit_bytes=vmem)


def _dot(a, b):
    return jnp.dot(a, b, preferred_element_type=F32)


def _dot_hi(a, b):
    return jnp.dot(a, b, preferred_element_type=F32, precision=lax.Precision.HIGHEST)


def _rms(x, g):
    return x * lax.rsqrt(jnp.mean(x * x, axis=-1, keepdims=True) + EPS) * g


def _ada_body(c_ref, w_ref, b_ref, o_ref):
    c = c_ref[...]
    s = c * jax.nn.sigmoid(c)
    o_ref[0] = _dot(s.astype(BF16), w_ref[0].astype(BF16)) + b_ref[0]


def _ada(cond8, ada_w, ada_b):
    depth, d, n = ada_w.shape
    tn = n // 4
    return pl.pallas_call(
        _ada_body,
        grid=(depth, n // tn),
        in_specs=[
            pl.BlockSpec((8, d), lambda i, j: (0, 0)),
            pl.BlockSpec((1, d, tn), lambda i, j: (i, 0, j)),
            pl.BlockSpec((1, 1, tn), lambda i, j: (i, 0, j)),
        ],
        out_specs=pl.BlockSpec((1, 8, tn), lambda i, j: (i, 0, j)),
        out_shape=jax.ShapeDtypeStruct((depth, 8, n), F32),
        compiler_params=_cp(("arbitrary", "arbitrary")),
        name="ada_mod",
    )(cond8, ada_w, ada_b.reshape(depth, 1, n))


def _mla_proj_body(x_ref, g_ref, sh_ref, sc_ref, wd_ref, gq_ref, gkv_ref, wuk_ref, wuqT_ref, wuvT_ref,
                   ct_ref, st_ref, cT_ref, sT_ref, *out_refs, need_q, tk):
    if need_q:
        qT_ref, k_ref, vT_ref = out_refs
    else:
        k_ref, vT_ref = out_refs
    nh = MLA_HEADS
    x = x_ref[0]
    h = _rms(x, g_ref[...]) * (1.0 + sc_ref[0]) + sh_ref[0]
    lat = _dot(h.astype(BF16), wd_ref[...])
    o_kv = Q_LORA
    o_a = Q_LORA + KV_LORA
    kvn = _rms(lat[:, o_kv:o_a], gkv_ref[...])
    kr = (lat[:, o_a:o_a + LANES] * ct_ref[...] + lat[:, o_a + LANES:o_a + 2 * LANES] * st_ref[...]).astype(BF16)
    knope = _dot(kvn.astype(BF16), wuk_ref[...])
    for hh in range(nh):
        k_ref[0, hh, :, 0:QK_NOPE] = knope[:, hh * QK_NOPE:(hh + 1) * QK_NOPE].astype(BF16)
        k_ref[0, hh, :, QK_NOPE:QK_PAD] = kr
    vT = _dot(wuvT_ref[...], kvn.T.astype(BF16))
    tm = x.shape[0]
    for hh in range(nh):
        for c in range(tm // tk):
            vT_ref[0, hh, c] = vT[hh * V_DIM:(hh + 1) * V_DIM, c * tk:(c + 1) * tk].astype(BF16)
    if need_q:
        qn = _rms(lat[:, :Q_LORA], gq_ref[...])
        qT = _dot(wuqT_ref[...], qn.T.astype(BF16)) * (MLA_SCALE * math.log2(math.e))
        c = cT_ref[...]
        s = sT_ref[...]
        hw = QK_NOPE + QK_ROPE
        half = QK_ROPE // 2
        for hh in range(nh):
            base = hh * hw
            x1 = qT[base + QK_NOPE:base + QK_NOPE + half]
            x2 = qT[base + QK_NOPE + half:base + hw]
            qT_ref[0, hh, 0:QK_NOPE] = qT[base:base + QK_NOPE].astype(BF16)
            qT_ref[0, hh, QK_NOPE:QK_NOPE + half] = (x1 * c - x2 * s).astype(BF16)
            qT_ref[0, hh, QK_NOPE + half:hw] = (x1 * s + x2 * c).astype(BF16)
            qT_ref[0, hh, hw:QK_PAD] = jnp.zeros((QK_PAD - hw, tm), BF16)


def _mla_proj(x, g, sh, sc, wts, tabs, *, need_q, tm, tk):
    bsz, n, d = x.shape
    nh = MLA_HEADS
    wd, gq, gkv, wuk, wuqT, wuvT = wts
    ct, st, cT, sT = tabs
    nsh = sh.shape[0]
    full = lambda a: pl.BlockSpec(a.shape, lambda b, i: (0,) * a.ndim)
    in_specs = [
        pl.BlockSpec((1, tm, d), lambda b, i: (b, i, 0)),
        full(g),
        pl.BlockSpec((1, 1, d), lambda b, i: (b % nsh, 0, 0)),
        pl.BlockSpec((1, 1, d), lambda b, i: (b % nsh, 0, 0)),
        full(wd), full(gq), full(gkv), full(wuk), full(wuqT), full(wuvT),
        pl.BlockSpec((tm, LANES), lambda b, i: (i, 0)),
        pl.BlockSpec((tm, LANES), lambda b, i: (i, 0)),
        pl.BlockSpec((QK_ROPE // 2, tm), lambda b, i: (0, i)),
        pl.BlockSpec((QK_ROPE // 2, tm), lambda b, i: (0, i)),
    ]
    out_specs = [
        pl.BlockSpec((1, nh, tm, QK_PAD), lambda b, i: (b, 0, i, 0)),
        pl.BlockSpec((1, nh, tm // tk, V_DIM, tk), lambda b, i: (b, 0, i, 0, 0)),
    ]
    out_shape = [
        jax.ShapeDtypeStruct((bsz, nh, n, QK_PAD), BF16),
        jax.ShapeDtypeStruct((bsz, nh, n // tk, V_DIM, tk), BF16),
    ]
    if need_q:
        out_specs = [pl.BlockSpec((1, nh, QK_PAD, tm), lambda b, i: (b, 0, 0, i))] + out_specs
        out_shape = [jax.ShapeDtypeStruct((bsz, nh, QK_PAD, n), BF16)] + out_shape
    return pl.pallas_call(
        functools.partial(_mla_proj_body, need_q=need_q, tk=tk),
        grid=(bsz, n // tm),
        in_specs=in_specs,
        out_specs=out_specs,
        out_shape=out_shape,
        compiler_params=_cp(("arbitrary", "arbitrary")),
        name="mla_proj_q" if need_q else "mla_proj_ctx",
    )(x, g, sh, sc, wd, gq, gkv, wuk, wuqT, wuvT, ct, st, cT, sT)


SM_STRIP = 64
SUBLANES = 8


def _attn_body(qT_ref, k_ref, vT_ref, kc_ref, vTc_ref, o_ref, s0, s1, p0, p1, sc, pc, acc, m_scr, x0, x1, xc,
               a0, a1, ac, l_scr, d0, d1, dc, *, tk):
    nchunk = k_ref.shape[2] // tk

    def scores(kblk, s_ref, mx_ref):
        r = _dot(kblk, qT_ref[0, 0])
        s_ref[...] = r
        mx_ref[...] = jnp.max(r, axis=0, keepdims=True)

    def probs(s_ref, mx_ref, p_ref, a_ref, d_ref):
        m_old = m_scr[...]
        m_new = jnp.maximum(m_old, mx_ref[...])
        m_scr[...] = m_new
        alpha = jnp.exp2(m_old - m_new)
        a_ref[...] = alpha
        part = None
        for r in range(0, s_ref.shape[0], SM_STRIP):
            p = jnp.exp2(s_ref[r:r + SM_STRIP] - m_new)
            p_ref[r:r + SM_STRIP] = p.astype(BF16)
            ps = jnp.sum(p.reshape(SM_STRIP // SUBLANES, SUBLANES, p.shape[1]), axis=0)
            part = ps if part is None else part + ps
        d_ref[...] = part

    def accumulate(p_ref, a_ref, d_ref, vblk):
        acc[...] = a_ref[...] * acc[...] + _dot(vblk, p_ref[...])
        l_scr[...] = a_ref[...] * l_scr[...] + d_ref[...]

    def kchunk(i):
        return k_ref[0, 0, pl.ds(pl.multiple_of(i * tk, tk), tk), :]

    def vchunk(i):
        nsub = tk // vT_ref.shape[-1]
        return jnp.concatenate([vT_ref[0, 0, i * nsub + u] for u in range(nsub)], axis=1)

    m_scr[...] = jnp.full(m_scr.shape, -jnp.inf, F32)
    acc[...] = jnp.zeros(acc.shape, F32)
    l_scr[...] = jnp.zeros(l_scr.shape, F32)
    scores(kc_ref[0, 0], sc, xc)
    scores(kchunk(0), s0, x0)
    probs(sc, xc, pc, ac, dc)
    scores(kchunk(1), s1, x1)
    accumulate(pc, ac, dc, vTc_ref[0, 0, 0])
    probs(s0, x0, p0, a0, d0)

    def body(j, carry):
        t = 2 * j
        scores(kchunk(t + 2), s0, x0)
        accumulate(p0, a0, d0, vchunk(t))
        probs(s1, x1, p1, a1, d1)
        scores(kchunk(t + 3), s1, x1)
        accumulate(p1, a1, d1, vchunk(t + 1))
        probs(s0, x0, p0, a0, d0)
        return carry

    lax.fori_loop(0, nchunk // 2 - 1, body, 0)
    accumulate(p0, a0, d0, vchunk(nchunk - 2))
    probs(s1, x1, p1, a1, d1)
    accumulate(p1, a1, d1, vchunk(nchunk - 1))
    o_ref[0, 0] = (acc[...] / jnp.sum(l_scr[...], axis=0, keepdims=True)).astype(BF16)


def _attention(qT, k, vT, kc, vTc, *, tq, tk):
    bsz, nh, _, n = qT.shape
    nc = kc.shape[2]
    tv = vT.shape[-1]
    assert (n // tk) % 2 == 0 and tk % tv == 0
    return pl.pallas_call(
        functools.partial(_attn_body, tk=tk),
        grid=(bsz, nh, n // tq),
        in_specs=[
            pl.BlockSpec((1, 1, QK_PAD, tq), lambda b, h, i: (b, h, 0, i)),
            pl.BlockSpec((1, 1, n, QK_PAD), lambda b, h, i: (b, h, 0, 0)),
            pl.BlockSpec((1, 1, n // tv, V_DIM, tv), lambda b, h, i: (b, h, 0, 0, 0)),
            pl.BlockSpec((1, 1, nc, QK_PAD), lambda b, h, i: (b, h, 0, 0)),
            pl.BlockSpec((1, 1, 1, V_DIM, nc), lambda b, h, i: (b, h, 0, 0, 0)),
        ],
        out_specs=pl.BlockSpec((1, 1, V_DIM, tq), lambda b, h, i: (b, h, 0, i)),
        out_shape=jax.ShapeDtypeStruct((bsz, nh, V_DIM, n), BF16),
        scratch_shapes=[pltpu.VMEM((tk, tq), F32), pltpu.VMEM((tk, tq), F32),
                        pltpu.VMEM((tk, tq), BF16), pltpu.VMEM((tk, tq), BF16),
                        pltpu.VMEM((nc, tq), F32), pltpu.VMEM((nc, tq), BF16),
                        pltpu.VMEM((V_DIM, tq), F32), pltpu.VMEM((1, tq), F32),
                        pltpu.VMEM((1, tq), F32), pltpu.VMEM((1, tq), F32), pltpu.VMEM((1, tq), F32),
                        pltpu.VMEM((1, tq), F32), pltpu.VMEM((1, tq), F32), pltpu.VMEM((1, tq), F32),
                        pltpu.VMEM((SUBLANES, tq), F32), pltpu.VMEM((SUBLANES, tq), F32),
                        pltpu.VMEM((SUBLANES, tq), F32), pltpu.VMEM((SUBLANES, tq), F32)],
        compiler_params=_cp(("arbitrary", "arbitrary", "arbitrary")),
        name="mla_attention",
    )(qT, k, vT, kc, vTc)


def _post_body(*refs, transposed):
    if transposed:
        o_ref, *refs = refs
    else:
        o_ref, ob_ref, *refs = refs
    (wo_ref, bo_ref, x_ref, g1_ref, gf_ref, sh_ref, sc_ref, wrh_ref, wrl_ref, br_ref, tri_ref,
     xl_ref, fl_ref, ti_ref, gt_ref, rk_ref, cnt_ref) = refs

    @pl.when((pl.program_id(0) == 0) & (pl.program_id(1) == 0))
    def _():
        cnt_ref[...] = jnp.zeros_like(cnt_ref)

    tm = x_ref.shape[1]
    if transposed:
        oT = o_ref[0].astype(F32).reshape(MLA_HEADS * V_DIM, tm)
        o = oT.T.astype(BF16)
    else:
        o = _interleave_tiles(o_ref[0], ob_ref[0]).astype(BF16)
    y = _dot(o, wo_ref[...]) + bo_ref[...]
    xl = x_ref[0] + g1_ref[0] * y
    xl_ref[0] = xl
    fl = _rms(xl, gf_ref[...]) * (1.0 + sc_ref[0]) + sh_ref[0]
    _to_rows(fl_ref, fl)
    flh = fl.astype(BF16)
    fll = (fl - flh.astype(F32)).astype(BF16)
    logits = _dot(flh, wrh_ref[...]) + (_dot(fll, wrh_ref[...]) + _dot(flh, wrl_ref[...])) + br_ref[...]
    lane = lax.broadcasted_iota(jnp.int32, (tm, LANES), 1).astype(F32)
    neg = jnp.float32(-jnp.inf)
    work = jnp.where(lane < N_EXPERTS, logits, neg)
    vals, idxs = [], []
    onehot = jnp.zeros((tm, LANES), F32)
    for _ in range(TOP_K):
        mk = jnp.max(work, axis=-1, keepdims=True)
        ik = jnp.min(jnp.where(work == mk, lane, float(LANES)), axis=-1, keepdims=True)
        sel = lane == ik
        onehot = jnp.where(sel, 1.0, onehot)
        work = jnp.where(sel, neg, work)
        vals.append(mk)
        idxs.append(ik)
    es = [jnp.exp(v - vals[0]) for v in vals]
    den = es[0] + es[1] + es[2] + es[3]
    pre = _dot(tri_ref[...], onehot.astype(BF16)) + cnt_ref[...]
    ti = jnp.zeros((tm, LANES), F32)
    gt = jnp.zeros((tm, LANES), F32)
    rk = jnp.zeros((tm, LANES), F32)
    for kk in range(TOP_K):
        rank = jnp.sum(jnp.where(lane == idxs[kk], pre, 0.0), axis=-1, keepdims=True)
        ti = jnp.where(lane == kk, idxs[kk], ti)
        gt = jnp.where(lane == kk, es[kk] / den, gt)
        rk = jnp.where(lane == kk, rank, rk)
    ti_ref[...] = ti[:, :TOP_K].astype(jnp.int32)
    gt_ref[...] = gt[:, :TOP_K]
    rk_ref[...] = rk[:, :TOP_K].astype(jnp.int32)
    cnt_ref[...] += jnp.sum(onehot, axis=0, keepdims=True)


def _post(o, wo, bo, x, g1, gf, sh, sc, wrh, wrl, br, *, transposed, tm):
    bsz, n, d = x.shape
    t = bsz * n
    nt = n // tm
    tri = (lax.broadcasted_iota(jnp.int32, (tm, tm), 0) > lax.broadcasted_iota(jnp.int32, (tm, tm), 1)).astype(BF16)
    full = lambda a: pl.BlockSpec(a.shape, lambda b, i: (0,) * a.ndim)
    per_b = pl.BlockSpec((1, 1, d), lambda b, i: (b, 0, 0))
    if transposed:
        o_args = [o]
        o_specs = [pl.BlockSpec((1, MLA_HEADS, V_DIM, tm), lambda b, i: (b, 0, 0, i))]
    else:
        o_args = list(o)
        o_specs = [pl.BlockSpec((1, tm, d // 2), lambda b, i: (b, i, 0))] * 2
    tok = lambda w: pl.BlockSpec((tm, w), lambda b, i: (b * nt + i, 0))
    return pl.pallas_call(
        functools.partial(_post_body, transposed=transposed),
        grid=(bsz, nt),
        in_specs=[*o_specs, full(wo), full(bo), pl.BlockSpec((1, tm, d), lambda b, i: (b, i, 0)), per_b, full(gf),
                  per_b, per_b, full(wrh), full(wrl), full(br), full(tri)],
        out_specs=[pl.BlockSpec((1, tm, d), lambda b, i: (b, i, 0)),
                   pl.BlockSpec((tm * ROW_SUB, LANES), lambda b, i: (b * nt + i, 0)),
                   tok(TOP_K), tok(TOP_K), tok(TOP_K), pl.BlockSpec((1, LANES), lambda b, i: (0, 0))],
        out_shape=[jax.ShapeDtypeStruct((bsz, n, d), F32), jax.ShapeDtypeStruct((t * ROW_SUB, LANES), F32),
                   jax.ShapeDtypeStruct((t, TOP_K), jnp.int32), jax.ShapeDtypeStruct((t, TOP_K), F32),
                   jax.ShapeDtypeStruct((t, TOP_K), jnp.int32), jax.ShapeDtypeStruct((1, LANES), F32)],
        compiler_params=_cp(("arbitrary", "arbitrary")),
        name="post_attn" if transposed else "post_hyena",
    )(*o_args, wo, bo, x, g1, gf, sh, sc, wrh, wrl, br, tri)


ROW_SUB = 8


def _row_slice(i):
    return pl.ds(pl.multiple_of(i * ROW_SUB, ROW_SUB), ROW_SUB)


def _to_rows(ref, x):
    for s in range(ROW_SUB):
        ref[pl.ds(s, x.shape[0], stride=ROW_SUB), :] = x[:, s * LANES:(s + 1) * LANES]


def _from_rows(ref, lo, hi):
    return jnp.concatenate([ref[pl.ds(lo * ROW_SUB + s, hi - lo, stride=ROW_SUB), :] for s in range(ROW_SUB)], axis=1)


def _dispatch_body(pe_ref, pd_ref, dest_ref, fl_ref, xs_out, zbuf, sem, *, td):
    @pl.when(pl.program_id(0) == 0)
    def _():
        zbuf[...] = jnp.zeros(zbuf.shape, zbuf.dtype)
        for e in range(N_EXPERTS):
            @pl.when(pd_ref[e] > 0)
            def _():
                start = pl.multiple_of((pe_ref[e] - MOE_TM) * ROW_SUB, ROW_SUB)
                cp = pltpu.make_async_copy(zbuf, xs_out.at[pl.ds(start, MOE_TM * ROW_SUB)], sem)
                cp.start()
                cp.wait()

    def issue(t, carry):
        for kk in range(TOP_K):
            d = dest_ref[0, 0, t * TOP_K + kk]
            pltpu.make_async_copy(fl_ref.at[_row_slice(t)], xs_out.at[_row_slice(d)], sem).start(priority=kk % 2)
        return carry

    lax.fori_loop(0, td, issue, 0, unroll=2)

    def drain(t, carry):
        pltpu.make_async_copy(fl_ref.at[_row_slice(0)], xs_out.at[_row_slice(0)], sem).wait()
        return carry

    lax.fori_loop(0, td * TOP_K, drain, 0, unroll=8)


def _dispatch(pad_end, padded, dest, fl, n_rows, *, td):
    t = fl.shape[0] // ROW_SUB
    dest3 = dest.reshape(t // td, 1, td * TOP_K)
    grid_spec = pltpu.PrefetchScalarGridSpec(
        num_scalar_prefetch=2,
        grid=(t // td,),
        in_specs=[
            pl.BlockSpec((1, 1, td * TOP_K), lambda i, pe, pd: (i, 0, 0), memory_space=pltpu.SMEM),
            pl.BlockSpec((td * ROW_SUB, LANES), lambda i, pe, pd: (i, 0)),
        ],
        out_specs=pl.BlockSpec(memory_space=pl.ANY),
        scratch_shapes=[pltpu.VMEM((MOE_TM * ROW_SUB, LANES), fl.dtype), pltpu.SemaphoreType.DMA(())],
    )
    return pl.pallas_call(
        functools.partial(_dispatch_body, td=td),
        grid_spec=grid_spec,
        out_shape=jax.ShapeDtypeStruct((n_rows * ROW_SUB, LANES), fl.dtype),
        compiler_params=_cp(("arbitrary",)),
        name="moe_dispatch",
    )(pad_end, padded, dest3, fl)


def _expert_body(be_ref, nu_ref, xs_ref, win_ref, bin_ref, wout_ref, bout_ref, ys_ref, win_s, wout_s):
    b = pl.program_id(0)
    dff = wout_ref.shape[1]

    @pl.when(b < nu_ref[0])
    def _():
        prev = be_ref[jnp.maximum(b - 1, 0)]

        @pl.when((b == 0) | (prev != be_ref[b]))
        def _():
            win_s[...] = win_ref[0].astype(BF16)
            wout_s[...] = wout_ref[0].astype(BF16)

        x = _from_rows(xs_ref, 0, xs_ref.shape[0] // ROW_SUB).astype(BF16)
        gu = _dot(x, win_s[...]) + bin_ref[0]
        gate = jnp.minimum(gu[:, :dff], SWIGLU_LIMIT)
        lin = jnp.clip(gu[:, dff:], -SWIGLU_LIMIT, SWIGLU_LIMIT)
        act = gate * jax.nn.sigmoid(SWIGLU_ALPHA * gate) * (lin + 1.0)
        _to_rows(ys_ref, _dot(act.astype(BF16), wout_s[...]) + bout_ref[0])

    @pl.when(b >= nu_ref[0])
    def _():
        ys_ref[...] = jnp.zeros_like(ys_ref)


def _experts(blk_exp, n_used, xs, layer, w_in, b_in, w_out, b_out):
    n_rows = xs.shape[0] // ROW_SUB
    depth, ne, d, f2 = w_in.shape
    dff = w_out.shape[2]
    tm = MOE_TM
    grid_spec = pltpu.PrefetchScalarGridSpec(
        num_scalar_prefetch=2,
        grid=(n_rows // tm,),
        in_specs=[
            pl.BlockSpec((tm * ROW_SUB, LANES), lambda b, be, nu: (jnp.minimum(b, nu[0] - 1), 0)),
            pl.BlockSpec((None, 1, d, f2), lambda b, be, nu: (layer, be[b], 0, 0)),
            pl.BlockSpec((None, 1, 1, f2), lambda b, be, nu: (layer, be[b], 0, 0)),
            pl.BlockSpec((None, 1, dff, d), lambda b, be, nu: (layer, be[b], 0, 0)),
            pl.BlockSpec((None, 1, 1, d), lambda b, be, nu: (layer, be[b], 0, 0)),
        ],
        out_specs=pl.BlockSpec((tm * ROW_SUB, LANES), lambda b, be, nu: (b, 0)),
        scratch_shapes=[pltpu.VMEM((d, f2), BF16), pltpu.VMEM((dff, d), BF16)],
    )
    return pl.pallas_call(
        _expert_body,
        grid_spec=grid_spec,
        out_shape=jax.ShapeDtypeStruct(xs.shape, F32),
        compiler_params=_cp(("arbitrary",)),
        name="moe_experts",
    )(blk_exp, n_used, xs, w_in, b_in.reshape(depth, ne, 1, f2), w_out, b_out.reshape(depth, ne, 1, d))


def _combine_body(dest_ref, ys_hbm, gt_ref, xl_ref, g2_ref, fg_ref, out_ref, buf, sem, *, tc, final):
    def issue(t, carry):
        for kk in range(TOP_K):
            d = dest_ref[0, 0, t * TOP_K + kk]
            pltpu.make_async_copy(ys_hbm.at[_row_slice(d)], buf.at[_row_slice(kk * tc + t)], sem).start(
                priority=kk % 2)
        return carry

    lax.fori_loop(0, tc, issue, 0, unroll=2)

    def drain(t, carry):
        pltpu.make_async_copy(ys_hbm.at[_row_slice(0)], buf.at[_row_slice(0)], sem).wait()
        return carry

    lax.fori_loop(0, tc * TOP_K, drain, 0, unroll=8)
    gt = gt_ref[...]
    y = gt[:, 0:1] * _from_rows(buf, 0, tc)
    for kk in range(1, TOP_K):
        y = y + gt[:, kk:kk + 1] * _from_rows(buf, kk * tc, (kk + 1) * tc)
    xl = xl_ref[0] + g2_ref[0] * y
    out_ref[0] = _rms(xl, fg_ref[...]) if final else xl


def _combine(dest, ys, gates, xl, g2, fg, *, tc, final):
    bsz, n, d = xl.shape
    t = bsz * n
    nt = n // tc
    dest3 = dest.reshape(t // tc, 1, tc * TOP_K)
    return pl.pallas_call(
        functools.partial(_combine_body, tc=tc, final=final),
        grid=(bsz, nt),
        in_specs=[
            pl.BlockSpec((1, 1, tc * TOP_K), lambda b, i: (b * nt + i, 0, 0), memory_space=pltpu.SMEM),
            pl.BlockSpec(memory_space=pl.ANY),
            pl.BlockSpec((tc, TOP_K), lambda b, i: (b * nt + i, 0)),
            pl.BlockSpec((1, tc, d), lambda b, i: (b, i, 0)),
            pl.BlockSpec((1, 1, d), lambda b, i: (b, 0, 0)),
            pl.BlockSpec((1, d), lambda b, i: (0, 0)),
        ],
        out_specs=pl.BlockSpec((1, tc, d), lambda b, i: (b, i, 0)),
        out_shape=jax.ShapeDtypeStruct((bsz, n, d), F32),
        scratch_shapes=[pltpu.VMEM((TOP_K * tc * ROW_SUB, LANES), F32), pltpu.SemaphoreType.DMA(())],
        compiler_params=_cp(("arbitrary", "arbitrary")),
        name="moe_combine",
    )(dest3, ys, gates, xl, g2, fg)


def _moe(fl, topi, gates, rank, cnt, xl, g2, fg, layer, w_in, b_in, w_out, b_out, *, final):
    t = fl.shape[0] // ROW_SUB
    tm = MOE_TM
    counts = cnt[0, :N_EXPERTS].astype(jnp.int32)
    padded = (counts + tm - 1) // tm * tm
    pad_end = jnp.cumsum(padded)
    pad_start = pad_end - padded
    dest = jnp.take(pad_start, topi) + rank
    nb = t * TOP_K // tm + N_EXPERTS
    blk_start = jnp.arange(nb, dtype=jnp.int32) * tm
    blk_exp = jnp.minimum(jnp.sum((pad_end[None, :] <= blk_start[:, None]).astype(jnp.int32), axis=1), N_EXPERTS - 1)
    n_used = (pad_end[-1:] // tm).astype(jnp.int32)
    xs = _dispatch(pad_end, padded, dest, fl, nb * tm, td=512)
    ys = _experts(blk_exp, n_used, xs, layer, w_in, b_in, w_out, b_out)
    return _combine(dest, ys, gates, xl, g2, fg, tc=512, final=final)


def _hy_in_body(x_ref, xp_ref, xn_ref, g_ref, sh_ref, sc_ref, w_ref, b_ref, cw_ref, cb_ref, o_ref, *, nt):
    i = pl.program_id(1)
    d = x_ref.shape[2]

    def normed(xx):
        return (_rms(xx, g_ref[...]) * (1.0 + sc_ref[0]) + sh_ref[0]).astype(BF16)

    h = normed(x_ref[0])
    hh = normed(jnp.concatenate([xp_ref[0], xn_ref[0]], axis=0))
    tm = h.shape[0]
    row = lax.broadcasted_iota(jnp.int32, (tm, 1), 0)
    for j in range(3):
        cols = slice(j * d, (j + 1) * d)
        p = _dot(h, w_ref[:, cols]) + b_ref[:, cols]
        ph = _dot(hh, w_ref[:, cols]) + b_ref[:, cols]
        prev = jnp.where(i > 0, ph[7:8], 0.0)
        nxt = jnp.where(i < nt - 1, ph[8:9], 0.0)
        up = jnp.where(row == 0, prev, pltpu.roll(p, 1, axis=0))
        dn = jnp.where(row == tm - 1, nxt, pltpu.roll(p, tm - 1, axis=0))
        cw = cw_ref[:, cols]
        o_ref[j, 0] = up * cw[0:1] + p * cw[1:2] + dn * cw[2:3] + cb_ref[:, cols]


def _hy_in(x, g, sh, sc, w, b, cw, cb, *, tm):
    bsz, n, d = x.shape
    nt = n // tm
    hb = tm // 8
    per_b = pl.BlockSpec((1, 1, d), lambda bb, i: (bb, 0, 0))
    full = lambda a: pl.BlockSpec(a.shape, lambda bb, i: (0,) * a.ndim)
    return pl.pallas_call(
        functools.partial(_hy_in_body, nt=nt),
        grid=(bsz, nt),
        in_specs=[
            pl.BlockSpec((1, tm, d), lambda bb, i: (bb, i, 0)),
            pl.BlockSpec((1, 8, d), lambda bb, i: (bb, jnp.maximum(i * hb - 1, 0), 0)),
            pl.BlockSpec((1, 8, d), lambda bb, i: (bb, jnp.minimum((i + 1) * hb, n // 8 - 1), 0)),
            full(g), per_b, per_b, full(w), full(b), full(cw), full(cb),
        ],
        out_specs=pl.BlockSpec((3, 1, tm, d), lambda bb, i: (0, bb, i, 0)),
        out_shape=jax.ShapeDtypeStruct((3, bsz, n, d), F32),
        compiler_params=_cp(("arbitrary", "arbitrary")),
        name="hyena_in_proj",
    )(x, x, x, g, sh, sc, w, b, cw, cb)


def _filt_feat_body(w1_ref, b1_ref, f1_ref, w2_ref, b2_ref, f2_ref, o_ref, *, n_lat):
    na = o_ref.shape[1]
    a = lax.broadcasted_iota(jnp.int32, (na, 1), 0)
    lane = lax.broadcasted_iota(jnp.int32, (na, LANES), 1)
    band_idx = jnp.where(lane <= HY_BANDS, lane - 1, lane - 1 - HY_BANDS).astype(F32)
    band = 1e-4 + band_idx * ((HY_BANDS - 1 - 1e-4) / (HY_BANDS - 1))
    phase = jnp.where(lane > HY_BANDS, 0.5 * math.pi, 0.0)
    for j in range(B_GROUP):
        r = a * FFT_N2 + (pl.program_id(0) * B_GROUP + j)
        pos = jnp.where(r < n_lat, r, 2 * n_lat - r).astype(F32)
        tn = pos / float(max(n_lat - 1, 1))
        ang = ((2.0 * math.pi / n_lat) * pos) * band
        z = jnp.where(lane == 0, tn, jnp.where(lane < HY_EMB, jnp.cos(ang + phase), 0.0))
        h1 = jnp.sin(f1_ref[...] * (_dot_hi(z, w1_ref[...]) + b1_ref[...]))
        h2 = jnp.sin(f2_ref[...] * (_dot_hi(h1, w2_ref[...]) + b2_ref[...]))
        valid = (r != n_lat).astype(F32)
        o_ref[j] = jnp.where(lane == HY_HID, tn, jnp.where(lane == HY_HID + 1, valid, h2))


def _filt_feat(w1, b1, f1, w2, b2, f2, *, n_lat):
    na = 2 * n_lat // FFT_N2
    w1p = jnp.zeros((LANES, LANES), F32).at[:HY_EMB, :HY_HID].set(w1)
    w2p = jnp.zeros((LANES, LANES), F32).at[:HY_HID, :HY_HID].set(w2)
    padv = lambda v: jnp.zeros((1, LANES), F32).at[0, :HY_HID].set(v)
    full = lambda shp: pl.BlockSpec(shp, lambda i: (0,) * len(shp))
    return pl.pallas_call(
        functools.partial(_filt_feat_body, n_lat=n_lat),
        grid=(FFT_N2 // B_GROUP,),
        in_specs=[full((LANES, LANES)), full((1, LANES)), full((1, LANES)),
                  full((LANES, LANES)), full((1, LANES)), full((1, LANES))],
        out_specs=pl.BlockSpec((B_GROUP, na, LANES), lambda i: (i, 0, 0)),
        out_shape=jax.ShapeDtypeStruct((FFT_N2, na, LANES), F32),
        compiler_params=_cp(("arbitrary",)),
        name="hyena_filter_features",
    )(w1p, padv(b1), padv(f1), w2p, padv(b2), padv(f2))


U32 = jnp.uint32
HI16 = 0xFFFF0000


def _pack_c(re, im):
    lo = lax.bitcast_convert_type(re.astype(BF16).astype(F32), U32) >> 16
    hi = lax.bitcast_convert_type(im.astype(BF16).astype(F32), U32) & U32(HI16)
    return hi | lo


def _unpack_c(u):
    re = lax.bitcast_convert_type(u << 16, F32)
    im = lax.bitcast_convert_type(u & U32(HI16), F32)
    return jnp.concatenate([re, im], axis=0).astype(BF16)


def _interleave_tiles(xa, xb):
    tiles = []
    for c0 in range(0, xa.shape[-1], LANES):
        tiles += [xa[..., c0:c0 + LANES], xb[..., c0:c0 + LANES]]
    return jnp.concatenate(tiles, axis=-1)


def _filt_s1_body(hd_ref, w3_ref, dec_ref, tab_ref, oa_ref, ob_ref):
    na = hd_ref.shape[1]
    ha = na // 2
    oa2, ob2 = _rows2d(oa_ref), _rows2d(ob_ref)
    ft = hd_ref[:, :ha, :].reshape(B_GROUP * ha, LANES)
    fb = hd_ref[:, ha:, :].reshape(B_GROUP * ha, LANES)
    top = _dot(ft.astype(BF16), w3_ref[0, 0].astype(BF16))
    top = top * jnp.exp(-ft[:, HY_HID:HY_HID + 1] * jnp.abs(dec_ref[0, 0]))
    bot = _dot(fb.astype(BF16), w3_ref[0, 1].astype(BF16))
    bot = bot * (jnp.exp(-fb[:, HY_HID:HY_HID + 1] * jnp.abs(dec_ref[0, 1])) * fb[:, HY_HID + 1:HY_HID + 2])
    for j in range(B_GROUP):
        hb = jnp.concatenate([top[j * ha:(j + 1) * ha], bot[j * ha:(j + 1) * ha]], axis=0).astype(BF16)
        r = _dot(tab_ref[j], hb)
        packed = _pack_c(r[:na], r[na:])
        oa2[pl.ds(j, na, stride=B_GROUP), :] = packed[:, :LANES]
        ob2[pl.ds(j, na, stride=B_GROUP), :] = packed[:, LANES:]


def _filt_s1(hd, w3r, dec, tab):
    _, na, _ = hd.shape
    d = w3r.shape[-1]
    ct = 2 * LANES
    half = pl.BlockSpec((None, na, B_GROUP, LANES), lambda o, g, c: (o, 0, g, c))
    return pl.pallas_call(
        _filt_s1_body,
        grid=(2, FFT_N2 // B_GROUP, d // ct),
        in_specs=[
            pl.BlockSpec((B_GROUP, na, LANES), lambda o, g, c: (g, 0, 0)),
            pl.BlockSpec((1, 2, LANES, ct), lambda o, g, c: (o, 0, 0, c)),
            pl.BlockSpec((1, 2, 1, ct), lambda o, g, c: (o, 0, 0, c)),
            pl.BlockSpec((B_GROUP, 2 * na, na), lambda o, g, c: (g, 0, 0)),
        ],
        out_specs=[half, half],
        out_shape=[jax.ShapeDtypeStruct((2, na, FFT_N2, d // 2), U32)] * 2,
        compiler_params=_cp(("arbitrary", "arbitrary", "arbitrary")),
        name="hyena_filter_dft1",
    )(hd, w3r, dec, tab)


S2_KB = 8


def _s2_body(*refs, conv):
    if conv:
        oa_ref, ob_ref, kf_ref, ff_ref, fi_ref, g_ref = refs
    else:
        oa_ref, ob_ref, ff_ref, g_ref = refs
    for u in range(S2_KB):
        xf = _dot(ff_ref[...], _unpack_c(_interleave_tiles(oa_ref[u], ob_ref[u])))
        if conv:
            xr, xi = xf[:FFT_N2], xf[FFT_N2:]
            kr = kf_ref[0, u].astype(F32)
            ki = kf_ref[1, u].astype(F32)
            y = jnp.concatenate([xr * kr - xi * ki, xr * ki + xi * kr], axis=0).astype(BF16)
            xf = _dot(fi_ref[...], y)
            g_ref[u] = _pack_c(xf[:FFT_N2], xf[FFT_N2:])
        else:
            g_ref[0, u] = xf[:FFT_N2].astype(BF16)
            g_ref[1, u] = xf[FFT_N2:].astype(BF16)


def _s2(o3, kf, order, ff, fi, *, ct, conv):
    n1 = o3[0].shape[-3]
    d = 2 * o3[0].shape[-1]
    full = lambda a: pl.BlockSpec(a.shape, lambda k, c: (0,) * a.ndim)
    nk = n1 // S2_KB
    if conv:
        hblk = pl.BlockSpec((S2_KB, FFT_N2, ct // 2), lambda k, c: (k, 0, c))
        in_specs = [hblk, hblk, pl.BlockSpec((None, 2, S2_KB, FFT_N2, ct), lambda k, c: (order, 0, k, 0, c)),
                    full(ff), full(fi)]
        args = (o3[0], o3[1], kf, ff, fi)
        grid = (nk, d // ct)
        out_specs = pl.BlockSpec((S2_KB, FFT_N2, ct), lambda k, c: (k, 0, c))
        out_shape = jax.ShapeDtypeStruct((n1, FFT_N2, d), U32)
    else:
        no = o3[0].shape[0]
        hblk = pl.BlockSpec((None, S2_KB, FFT_N2, ct // 2), lambda k, c: (k // nk, k % nk, 0, c))
        in_specs = [hblk, hblk, full(ff)]
        args = (o3[0], o3[1], ff)
        grid = (no * nk, d // ct)
        out_specs = pl.BlockSpec((None, 2, S2_KB, FFT_N2, ct), lambda k, c: (k // nk, 0, k % nk, 0, c))
        out_shape = jax.ShapeDtypeStruct((no, 2, n1, FFT_N2, d), BF16)
    return pl.pallas_call(
        functools.partial(_s2_body, conv=conv),
        grid=grid,
        in_specs=in_specs,
        out_specs=out_specs,
        out_shape=out_shape,
        compiler_params=_cp(("arbitrary", "arbitrary")),
        name="hyena_conv_dft2" if conv else "hyena_filter_dft2",
    )(*args)


def _rows2d(ref):
    lead = ref.shape[:-3]
    return ref.reshape(lead + (ref.shape[-3] * B_GROUP, ref.shape[-1]))


def _s1_body(za_ref, zb_ref, tab_ref, oa_ref, ob_ref):
    rows, n1 = za_ref.shape[0], oa_ref.shape[0]
    za2, zb2, oa2, ob2 = _rows2d(za_ref), _rows2d(zb_ref), _rows2d(oa_ref), _rows2d(ob_ref)
    for j in range(B_GROUP):
        sl = pl.ds(j, rows, stride=B_GROUP)
        zj = jnp.concatenate([za2[sl, :], zb2[sl, :]], axis=1)
        r = _dot(tab_ref[j], zj.astype(BF16))
        packed = _pack_c(r[:n1], r[n1:])
        oa2[pl.ds(j, n1, stride=B_GROUP), :] = packed[:, :LANES]
        ob2[pl.ds(j, n1, stride=B_GROUP), :] = packed[:, LANES:]


def _s1(z4, zi, tab):
    _, rows, _, d = z4.shape
    n1 = tab.shape[1] // 2
    half = pl.BlockSpec((n1, B_GROUP, LANES), lambda g, c: (0, g, c))
    return pl.pallas_call(
        _s1_body,
        grid=(FFT_N2 // B_GROUP, d // (2 * LANES)),
        in_specs=[
            pl.BlockSpec((None, rows, B_GROUP, LANES), lambda g, c: (zi, 0, g, 2 * c)),
            pl.BlockSpec((None, rows, B_GROUP, LANES), lambda g, c: (zi, 0, g, 2 * c + 1)),
            pl.BlockSpec((B_GROUP, 2 * n1, rows), lambda g, c: (g, 0, 0)),
        ],
        out_specs=[half, half],
        out_shape=[jax.ShapeDtypeStruct((n1, FFT_N2, d // 2), U32)] * 2,
        compiler_params=_cp(("arbitrary", "arbitrary")),
        name="hyena_conv_dft1",
    )(z4, z4, tab)


def _s3_body(ga_ref, gb_ref, tab_ref, gta_ref, gtb_ref, za_ref, zb_ref, fb_ref, *rest, chain):
    if chain:
        tab1_ref, oa_ref, ob_ref, qa_ref, qb_ref = rest
    else:
        oa_ref, ob_ref = rest
    n1, rows = ga_ref.shape[0], oa_ref.shape[0]
    ga2, gb2, oa2, ob2 = _rows2d(ga_ref), _rows2d(gb_ref), _rows2d(oa_ref), _rows2d(ob_ref)
    for j in range(B_GROUP):
        sl = pl.ds(j, n1, stride=B_GROUP)
        gj = _unpack_c(jnp.concatenate([ga2[sl, :], gb2[sl, :]], axis=1))
        y = _dot(tab_ref[j], gj)
        oa2[pl.ds(j, rows, stride=B_GROUP), :] = y[:, :LANES]
        ob2[pl.ds(j, rows, stride=B_GROUP), :] = y[:, LANES:]
    fb = fb_ref[...]
    oa_ref[...] = gta_ref[...] * (oa_ref[...] + za_ref[...] * fb[:, :, :LANES])
    ob_ref[...] = gtb_ref[...] * (ob_ref[...] + zb_ref[...] * fb[:, :, LANES:])
    if chain:
        qa2, qb2 = _rows2d(qa_ref), _rows2d(qb_ref)
        for j in range(B_GROUP):
            sl = pl.ds(j, rows, stride=B_GROUP)
            zj = jnp.concatenate([oa2[sl, :], ob2[sl, :]], axis=1)
            r = _dot(tab1_ref[j], zj.astype(BF16))
            packed = _pack_c(r[:n1], r[n1:])
            qa2[pl.ds(j, n1, stride=B_GROUP), :] = packed[:, :LANES]
            qb2[pl.ds(j, n1, stride=B_GROUP), :] = packed[:, LANES:]


def _s3(g3, tab, gate4, gi, zsrc, fb, tab1):
    n1, _, d = g3.shape
    rows = tab.shape[1]
    chain = tab1 is not None
    half = pl.BlockSpec((rows, B_GROUP, LANES), lambda g, c: (0, g, c))
    chalf = pl.BlockSpec((n1, B_GROUP, LANES), lambda g, c: (0, g, c))
    nat = lambda idx, par: pl.BlockSpec((None, rows, B_GROUP, LANES), lambda g, c: (idx, 0, g, 2 * c + par))
    if isinstance(zsrc[1], int):
        z_specs, z_args = [nat(zsrc[1], 0), nat(zsrc[1], 1)], [zsrc[0], zsrc[0]]
    else:
        z_specs, z_args = [half, half], list(zsrc)
    in_specs = [
        pl.BlockSpec((n1, B_GROUP, LANES), lambda g, c: (0, g, 2 * c)),
        pl.BlockSpec((n1, B_GROUP, LANES), lambda g, c: (0, g, 2 * c + 1)),
        pl.BlockSpec((B_GROUP, rows, 2 * n1), lambda g, c: (g, 0, 0)),
        nat(gi, 0), nat(gi, 1), *z_specs,
        pl.BlockSpec((1, 1, 2 * LANES), lambda g, c: (0, 0, c)),
    ]
    args = [g3, g3, tab, gate4, gate4, *z_args, fb]
    zshape = jax.ShapeDtypeStruct((rows, FFT_N2, d // 2), F32)
    out_specs, out_shape = [half, half], [zshape, zshape]
    if chain:
        in_specs.append(pl.BlockSpec((B_GROUP, 2 * n1, rows), lambda g, c: (g, 0, 0)))
        args.append(tab1)
        out_specs += [chalf, chalf]
        out_shape += [jax.ShapeDtypeStruct((n1, FFT_N2, d // 2), U32)] * 2
    return pl.pallas_call(
        functools.partial(_s3_body, chain=chain),
        grid=(FFT_N2 // B_GROUP, d // (2 * LANES)),
        in_specs=in_specs,
        out_specs=out_specs,
        out_shape=out_shape,
        compiler_params=_cp(("arbitrary", "arbitrary")),
        name="hyena_conv_idft1_dft1" if chain else "hyena_conv_idft1",
    )(*args)


def _dft_tables(n_lat):
    n = 2 * n_lat
    n1 = n // FFT_N2
    k1 = jnp.arange(n1, dtype=jnp.int32)
    th_a = ((k1[:, None] * k1[None, :]) % n1).astype(F32) * (2.0 * math.pi / n1)
    th_b = (jnp.arange(FFT_N2, dtype=jnp.int32)[:, None] * k1[None, :]).astype(F32) * (2.0 * math.pi / n)
    ca, sa = jnp.cos(th_a)[None], jnp.sin(th_a)[None]
    cb, sb = jnp.cos(th_b)[:, :, None], jnp.sin(th_b)[:, :, None]
    cr = ca * cb - sa * sb
    sn = sa * cb + ca * sb
    ha = n1 // 2
    crh, snh = cr[:, :, :ha], sn[:, :, :ha]
    w1 = jnp.concatenate([jnp.concatenate([crh, snh], axis=2), jnp.concatenate([-snh, crh], axis=2)], axis=1)
    w1f = jnp.concatenate([cr, -sn], axis=1)
    v = jnp.swapaxes(w1, 1, 2) * (1.0 / n)
    k2 = jnp.arange(FFT_N2, dtype=jnp.int32)
    th2 = ((k2[:, None] * k2[None, :]) % FFT_N2).astype(F32) * (2.0 * math.pi / FFT_N2)
    c2, s2 = jnp.cos(th2), jnp.sin(th2)
    ff = jnp.concatenate([jnp.concatenate([c2, s2], axis=1), jnp.concatenate([-s2, c2], axis=1)], axis=0)
    fi = jnp.concatenate([jnp.concatenate([c2, -s2], axis=1), jnp.concatenate([s2, c2], axis=1)], axis=0)
    return w1.astype(BF16), w1f.astype(BF16), v.astype(BF16), ff.astype(BF16), fi.astype(BF16)


def _hyena_mix(proj3, fparams, fbias, *, n_lat):
    _, bsz, _, d = proj3.shape
    f_w1, f_b1, f_f1, f_w2, f_b2, f_f2, f_w3, decay = fparams
    w1, w1f, v, ff, fi = _dft_tables(n_lat)
    hd = _filt_feat(f_w1, f_b1, f_f1, f_w2, f_b2, f_f2, n_lat=n_lat)
    w3r = jnp.transpose(f_w3.reshape(HY_HID, 2, 2, d), (1, 2, 0, 3))
    w3r = jnp.zeros((2, 2, LANES, d), F32).at[:, :, :HY_HID].set(w3r)
    kf1 = _filt_s1(hd, w3r, decay.reshape(2, 2, 1, d), w1f)
    kf = _s2(kf1, None, 0, ff, None, ct=d, conv=False)
    p3 = proj3.reshape(3, bsz * (n_lat // FFT_N2), FFT_N2, d)
    o1 = _s1(p3, 2, w1)
    g = _s2(o1, kf, 0, ff, fi, ct=d, conv=True)
    za, zb, *o1 = _s3(g, v, p3, 0, (p3, 2), fbias[0].reshape(1, 1, d), w1)
    g = _s2(o1, kf, 1, ff, fi, ct=d, conv=True)
    za, zb = _s3(g, v, p3, 1, (za, zb), fbias[1].reshape(1, 1, d), None)
    return za.reshape(bsz, n_lat, d // 2), zb.reshape(bsz, n_lat, d // 2)


def _rope_tables(n_tokens):
    rows = n_tokens // GRID_W
    row = jnp.broadcast_to(jnp.arange(rows, dtype=F32)[:, None], (rows, GRID_W)).reshape(-1)
    col = jnp.broadcast_to(jnp.arange(GRID_W, dtype=F32)[None, :], (rows, GRID_W)).reshape(-1)
    axis_dim = QK_ROPE // 2
    inv_freq = 1.0 / (ROPE_THETA ** (jnp.arange(0, axis_dim, 2, dtype=F32) / axis_dim))
    ang = jnp.concatenate([row[:, None] * inv_freq, col[:, None] * inv_freq], axis=-1)
    return jnp.cos(ang), jnp.sin(ang)


def _mla_weights(w_down, g_q, w_uq, g_kv, w_ukv):
    d = w_down.shape[0]
    nh = MLA_HEADS
    kpe = w_down[:, Q_LORA + KV_LORA:]
    w1, w2 = kpe[:, 0::2], kpe[:, 1::2]
    z = jnp.zeros((d, LANES - QK_ROPE), w_down.dtype)
    wd = jnp.concatenate([w_down[:, :Q_LORA + KV_LORA], w1, w2, z, w2, w1, z], axis=1).astype(BF16)
    uq = w_uq.reshape(Q_LORA, nh, QK_NOPE + QK_ROPE)
    pe = uq[:, :, QK_NOPE:]
    uq = jnp.concatenate([uq[:, :, :QK_NOPE], pe[:, :, 0::2], pe[:, :, 1::2]], axis=2)
    wuqT = uq.reshape(Q_LORA, nh * (QK_NOPE + QK_ROPE)).T.astype(BF16)
    ukv = w_ukv.reshape(KV_LORA, nh, QK_NOPE + V_DIM)
    wuk = ukv[:, :, :QK_NOPE].reshape(KV_LORA, nh * QK_NOPE).astype(BF16)
    wuvT = ukv[:, :, QK_NOPE:].reshape(KV_LORA, nh * V_DIM).T.astype(BF16)
    return wd, g_q.reshape(1, -1), g_kv.reshape(1, -1), wuk, wuqT, wuvT


def kernel(x, c, ctx, c_ctx, ada_w, ada_b, norm_mix_g, norm_ffn_g, mla_w_down, mla_g_q, mla_w_uq, mla_g_kv, mla_w_ukv, mla_w_o, hy_w_in, hy_b_in, hy_conv_w, hy_conv_b, hy_f_w1, hy_f_b1, hy_f_freq1, hy_f_w2, hy_f_b2, hy_f_freq2, hy_f_w3, hy_decay, hy_bias, hy_w_out, hy_b_out, moe_w_r, moe_b_r, moe_w_in, moe_b_in, moe_w_out, moe_b_out, final_g):
    bsz, n_lat, d = x.shape
    n_ctx = ctx.shape[1]
    depth = ada_w.shape[0]
    assert bsz == 2 and d == MLA_HEADS * V_DIM and n_lat % 512 == 0 and n_ctx % 128 == 0
    assert depth == 2

    cond8 = jnp.zeros((8, d), F32).at[:bsz].set(c).at[bsz].set(c_ctx)
    mods = _ada(cond8, ada_w, ada_b)

    def mod(i, j, rows):
        return mods[i, rows, j * d:(j + 1) * d][:, None, :]

    lat_rows = slice(0, bsz)
    ctx_rows = slice(bsz, bsz + 1)
    xl = x
    for i in range(depth):
        kind, j = i % 2, i // 2
        sh1, sc1, g1 = (mod(i, m, lat_rows) for m in range(3))
        sh2, sc2, g2 = (mod(i, m, lat_rows) for m in range(3, 6))
        gm = norm_mix_g[i].reshape(1, d)
        if kind == 0:
            wts = _mla_weights(mla_w_down[j], mla_g_q[j], mla_w_uq[j], mla_g_kv[j], mla_w_ukv[j])
            cos, sin = _rope_tables(n_lat)
            zl = jnp.zeros((n_lat, LANES - QK_ROPE), F32)
            tabs = (jnp.concatenate([cos, cos, zl], axis=1), jnp.concatenate([-sin, sin, zl], axis=1), cos.T, sin.T)
            tq = tv = 512
            tk = 2048 if n_lat % 4096 == 0 else 512
            qT, k, vT = _mla_proj(xl, gm, sh1, sc1, wts, tabs, need_q=True, tm=tv, tk=tv)
            half = QK_ROPE // 2
            one_c = jnp.concatenate([jnp.ones((n_ctx, QK_ROPE), F32), jnp.zeros((n_ctx, LANES - QK_ROPE), F32)], axis=1)
            tabs_c = (one_c, jnp.zeros((n_ctx, LANES), F32), jnp.ones((half, n_ctx), F32), jnp.zeros((half, n_ctx), F32))
            kc, vTc = _mla_proj(ctx, gm, mod(i, 0, ctx_rows), mod(i, 1, ctx_rows), wts, tabs_c,
                                need_q=False, tm=n_ctx, tk=n_ctx)
            o = _attention(qT, k, vT, kc, vTc, tq=tq, tk=tk)
            wo = mla_w_o[j].astype(BF16)
            bo = jnp.zeros((1, d), F32)
            transposed = True
        else:
            proj3 = _hy_in(xl, gm, sh1, sc1, hy_w_in[j].astype(BF16), hy_b_in[j].reshape(1, -1), hy_conv_w[j],
                           hy_conv_b[j].reshape(1, -1), tm=512)
            fparams = (hy_f_w1[j], hy_f_b1[j], hy_f_freq1[j], hy_f_w2[j], hy_f_b2[j], hy_f_freq2[j], hy_f_w3[j],
                       hy_decay[j])
            o = _hyena_mix(proj3, fparams, hy_bias[j], n_lat=n_lat)
            wo = hy_w_out[j].astype(BF16)
            bo = hy_b_out[j].reshape(1, d)
            transposed = False
        wr = jnp.zeros((d, LANES), F32).at[:, :N_EXPERTS].set(moe_w_r[i])
        wrh = wr.astype(BF16)
        wrl = (wr - wrh.astype(F32)).astype(BF16)
        br = jnp.zeros((1, LANES), F32).at[0, :N_EXPERTS].set(moe_b_r[i])
        xl, fl, topi, gates, rank, cnt = _post(o, wo, bo, xl, g1, norm_ffn_g[i].reshape(1, d), sh2, sc2, wrh, wrl, br,
                                               transposed=transposed, tm=512)
        xl = _moe(fl, topi, gates, rank, cnt, xl, g2, final_g.reshape(1, d), i, moe_w_in, moe_b_in,
                  moe_w_out, moe_b_out, final=(i == depth - 1))
    return xl
```

```python
import functools
import math

import jax
import jax.numpy as jnp
from jax import lax
from jax.experimental import pallas as pl
from jax.experimental.pallas import tpu as pltpu

F32 = jnp.float32
BF16 = jnp.bfloat16

EPS = 1e-6
GRID_W = 64
MLA_HEADS = 8
QK_NOPE = 128
QK_ROPE = 64
V_DIM = 128
Q_LORA = 512
KV_LORA = 256
ROPE_THETA = 10000.0
MLA_SCALE = (QK_NOPE + QK_ROPE) ** -0.5
QK_PAD = 256

HY_EMB = 33
HY_BANDS = (HY_EMB - 1) // 2
HY_HID = 64
FFT_N2 = 128
B_GROUP = 8

N_EXPERTS = 32
TOP_K = 4
SWIGLU_LIMIT = 7.0
SWIGLU_ALPHA = 1.702
MOE_TM = 512
LANES = 128

VMEM_LIMIT = 56 * 1024 * 1024


def _cp(sem, vmem=VMEM_LIMIT):
    return pltpu.CompilerParams(dimension_semantics=sem, vmem_limit_bytes=vmem)


def _dot(a, b):
    return jnp.dot(a, b, preferred_element_type=F32)


def _dot_hi(a, b):
    return jnp.dot(a, b, preferred_element_type=F32, precision=lax.Precision.HIGHEST)


def _rms(x, g):
    return x * lax.rsqrt(jnp.mean(x * x, axis=-1, keepdims=True) + EPS) * g


def _ada_body(c_ref, w_ref, b_ref, o_ref):
    c = c_ref[...]
    s = c * jax.nn.sigmoid(c)
    o_ref[0] = _dot(s.astype(BF16), w_ref[0].astype(BF16)) + b_ref[0]


def _ada(cond8, ada_w, ada_b):
    depth, d, n = ada_w.shape
    tn = n // 4
    return pl.pallas_call(
        _ada_body,
        grid=(depth, n // tn),
        in_specs=[
            pl.BlockSpec((8, d), lambda i, j: (0, 0)),
            pl.BlockSpec((1, d, tn), lambda i, j: (i, 0, j)),
            pl.BlockSpec((1, 1, tn), lambda i, j: (i, 0, j)),
        ],
        out_specs=pl.BlockSpec((1, 8, tn), lambda i, j: (i, 0, j)),
        out_shape=jax.ShapeDtypeStruct((depth, 8, n), F32),
        compiler_params=_cp(("arbitrary", "arbitrary")),
        name="ada_mod",
    )(cond8, ada_w, ada_b.reshape(depth, 1, n))


def _mla_proj_body(x_ref, g_ref, sh_ref, sc_ref, wd_ref, gq_ref, gkv_ref, wuk_ref, wuqT_ref, wuvT_ref,
                   ct_ref, st_ref, cT_ref, sT_ref, *out_refs, need_q, tk):
    if need_q:
        qT_ref, k_ref, vT_ref = out_refs
    else:
        k_ref, vT_ref = out_refs
    nh = MLA_HEADS
    x = x_ref[0]
    h = _rms(x, g_ref[...]) * (1.0 + sc_ref[0]) + sh_ref[0]
    lat = _dot(h.astype(BF16), wd_ref[...])
    o_kv = Q_LORA
    o_a = Q_LORA + KV_LORA
    kvn = _rms(lat[:, o_kv:o_a], gkv_ref[...])
    kr = (lat[:, o_a:o_a + LANES] * ct_ref[...] + lat[:, o_a + LANES:o_a + 2 * LANES] * st_ref[...]).astype(BF16)
    knope = _dot(kvn.astype(BF16), wuk_ref[...])
    for hh in range(nh):
        k_ref[0, hh, :, 0:QK_NOPE] = knope[:, hh * QK_NOPE:(hh + 1) * QK_NOPE].astype(BF16)
        k_ref[0, hh, :, QK_NOPE:QK_PAD] = kr
    vT = _dot(wuvT_ref[...], kvn.T.astype(BF16))
    tm = x.shape[0]
    for hh in range(nh):
        for c in range(tm // tk):
            vT_ref[0, hh, c] = vT[hh * V_DIM:(hh + 1) * V_DIM, c * tk:(c + 1) * tk].astype(BF16)
    if need_q:
        qn = _rms(lat[:, :Q_LORA], gq_ref[...])
        qT = _dot(wuqT_ref[...], qn.T.astype(BF16)) * (MLA_SCALE * math.log2(math.e))
        c = cT_ref[...]
        s = sT_ref[...]
        hw = QK_NOPE + QK_ROPE
        half = QK_ROPE // 2
        for hh in range(nh):
            base = hh * hw
            x1 = qT[base + QK_NOPE:base + QK_NOPE + half]
            x2 = qT[base + QK_NOPE + half:base + hw]
            qT_ref[0, hh, 0:QK_NOPE] = qT[base:base + QK_NOPE].astype(BF16)
            qT_ref[0, hh, QK_NOPE:QK_NOPE + half] = (x1 * c - x2 * s).astype(BF16)
            qT_ref[0, hh, QK_NOPE + half:hw] = (x1 * s + x2 * c).astype(BF16)
            qT_ref[0, hh, hw:QK_PAD] = jnp.zeros((QK_PAD - hw, tm), BF16)


def _mla_proj(x, g, sh, sc, wts, tabs, *, need_q, tm, tk):
    bsz, n, d = x.shape
    nh = MLA_HEADS
    wd, gq, gkv, wuk, wuqT, wuvT = wts
    ct, st, cT, sT = tabs
    nsh = sh.shape[0]
    full = lambda a: pl.BlockSpec(a.shape, lambda b, i: (0,) * a.ndim)
    in_specs = [
        pl.BlockSpec((1, tm, d), lambda b, i: (b, i, 0)),
        full(g),
        pl.BlockSpec((1, 1, d), lambda b, i: (b % nsh, 0, 0)),
        pl.BlockSpec((1, 1, d), lambda b, i: (b % nsh, 0, 0)),
        full(wd), full(gq), full(gkv), full(wuk), full(wuqT), full(wuvT),
        pl.BlockSpec((tm, LANES), lambda b, i: (i, 0)),
        pl.BlockSpec((tm, LANES), lambda b, i: (i, 0)),
        pl.BlockSpec((QK_ROPE // 2, tm), lambda b, i: (0, i)),
        pl.BlockSpec((QK_ROPE // 2, tm), lambda b, i: (0, i)),
    ]
    out_specs = [
        pl.BlockSpec((1, nh, tm, QK_PAD), lambda b, i: (b, 0, i, 0)),
        pl.BlockSpec((1, nh, tm // tk, V_DIM, tk), lambda b, i: (b, 0, i, 0, 0)),
    ]
    out_shape = [
        jax.ShapeDtypeStruct((bsz, nh, n, QK_PAD), BF16),
        jax.ShapeDtypeStruct((bsz, nh, n // tk, V_DIM, tk), BF16),
    ]
    if need_q:
        out_specs = [pl.BlockSpec((1, nh, QK_PAD, tm), lambda b, i: (b, 0, 0, i))] + out_specs
        out_shape = [jax.ShapeDtypeStruct((bsz, nh, QK_PAD, n), BF16)] + out_shape
    return pl.pallas_call(
        functools.partial(_mla_proj_body, need_q=need_q, tk=tk),
        grid=(bsz, n // tm),
        in_specs=in_specs,
        out_specs=out_specs,
        out_shape=out_shape,
        compiler_params=_cp(("arbitrary", "arbitrary")),
        name="mla_proj_q" if need_q else "mla_proj_ctx",
    )(x, g, sh, sc, wd, gq, gkv, wuk, wuqT, wuvT, ct, st, cT, sT)


SM_STRIP = 64
SUBLANES = 8


def _attn_body(qT_ref, k_ref, vT_ref, kc_ref, vTc_ref, o_ref, s0, s1, p0, p1, sc, pc, acc, m_scr, x0, x1, xc,
               a0, a1, ac, l_scr, d0, d1, dc, *, tk):
    nchunk = k_ref.shape[2] // tk

    def scores(kblk, s_ref, mx_ref):
        r = _dot(kblk, qT_ref[0, 0])
        s_ref[...] = r
        mx_ref[...] = jnp.max(r, axis=0, keepdims=True)

    def probs(s_ref, mx_ref, p_ref, a_ref, d_ref):
        m_old = m_scr[...]
        m_new = jnp.maximum(m_old, mx_ref[...])
        m_scr[...] = m_new
        alpha = jnp.exp2(m_old - m_new)
        a_ref[...] = alpha
        part = None
        for r in range(0, s_ref.shape[0], SM_STRIP):
            p = jnp.exp2(s_ref[r:r + SM_STRIP] - m_new)
            p_ref[r:r + SM_STRIP] = p.astype(BF16)
            ps = jnp.sum(p.reshape(SM_STRIP // SUBLANES, SUBLANES, p.shape[1]), axis=0)
            part = ps if part is None else part + ps
        d_ref[...] = part

    def accumulate(p_ref, a_ref, d_ref, vblk):
        acc[...] = a_ref[...] * acc[...] + _dot(vblk, p_ref[...])
        l_scr[...] = a_ref[...] * l_scr[...] + d_ref[...]

    def kchunk(i):
        return k_ref[0, 0, pl.ds(pl.multiple_of(i * tk, tk), tk), :]

    def vchunk(i):
        nsub = tk // vT_ref.shape[-1]
        return jnp.concatenate([vT_ref[0, 0, i * nsub + u] for u in range(nsub)], axis=1)

    m_scr[...] = jnp.full(m_scr.shape, -jnp.inf, F32)
    acc[...] = jnp.zeros(acc.shape, F32)
    l_scr[...] = jnp.zeros(l_scr.shape, F32)
    scores(kc_ref[0, 0], sc, xc)
    scores(kchunk(0), s0, x0)
    probs(sc, xc, pc, ac, dc)
    scores(kchunk(1), s1, x1)
    accumulate(pc, ac, dc, vTc_ref[0, 0, 0])
    probs(s0, x0, p0, a0, d0)

    def body(j, carry):
        t = 2 * j
        scores(kchunk(t + 2), s0, x0)
        accumulate(p0, a0, d0, vchunk(t))
        probs(s1, x1, p1, a1, d1)
        scores(kchunk(t + 3), s1, x1)
        accumulate(p1, a1, d1, vchunk(t + 1))
        probs(s0, x0, p0, a0, d0)
        return carry

    lax.fori_loop(0, nchunk // 2 - 1, body, 0)
    accumulate(p0, a0, d0, vchunk(nchunk - 2))
    probs(s1, x1, p1, a1, d1)
    accumulate(p1, a1, d1, vchunk(nchunk - 1))
    o_ref[0, 0] = (acc[...] / jnp.sum(l_scr[...], axis=0, keepdims=True)).astype(BF16)


def _attention(qT, k, vT, kc, vTc, *, tq, tk):
    bsz, nh, _, n = qT.shape
    nc = kc.shape[2]
    tv = vT.shape[-1]
    assert (n // tk) % 2 == 0 and tk % tv == 0
    return pl.pallas_call(
        functools.partial(_attn_body, tk=tk),
        grid=(bsz, nh, n // tq),
        in_specs=[
            pl.BlockSpec((1, 1, QK_PAD, tq), lambda b, h, i: (b, h, 0, i)),
            pl.BlockSpec((1, 1, n, QK_PAD), lambda b, h, i: (b, h, 0, 0)),
            pl.BlockSpec((1, 1, n // tv, V_DIM, tv), lambda b, h, i: (b, h, 0, 0, 0)),
            pl.BlockSpec((1, 1, nc, QK_PAD), lambda b, h, i: (b, h, 0, 0)),
            pl.BlockSpec((1, 1, 1, V_DIM, nc), lambda b, h, i: (b, h, 0, 0, 0)),
        ],
        out_specs=pl.BlockSpec((1, 1, V_DIM, tq), lambda b, h, i: (b, h, 0, i)),
        out_shape=jax.ShapeDtypeStruct((bsz, nh, V_DIM, n), BF16),
        scratch_shapes=[pltpu.VMEM((tk, tq), F32), pltpu.VMEM((tk, tq), F32),
                        pltpu.VMEM((tk, tq), BF16), pltpu.VMEM((tk, tq), BF16),
                        pltpu.VMEM((nc, tq), F32), pltpu.VMEM((nc, tq), BF16),
                        pltpu.VMEM((V_DIM, tq), F32), pltpu.VMEM((1, tq), F32),
                        pltpu.VMEM((1, tq), F32), pltpu.VMEM((1, tq), F32), pltpu.VMEM((1, tq), F32),
                        pltpu.VMEM((1, tq), F32), pltpu.VMEM((1, tq), F32), pltpu.VMEM((1, tq), F32),
                        pltpu.VMEM((SUBLANES, tq), F32), pltpu.VMEM((SUBLANES, tq), F32),
                        pltpu.VMEM((SUBLANES, tq), F32), pltpu.VMEM((SUBLANES, tq), F32)],
        compiler_params=_cp(("arbitrary", "arbitrary", "arbitrary")),
        name="mla_attention",
    )(qT, k, vT, kc, vTc)


def _post_body(*refs, transposed):
    if transposed:
        o_ref, *refs = refs
    else:
        o_ref, ob_ref, *refs = refs
    (wo_ref, bo_ref, x_ref, g1_ref, gf_ref, sh_ref, sc_ref, wrh_ref, wrl_ref, br_ref, tri_ref,
     xl_ref, fl_ref, ti_ref, gt_ref, rk_ref, cnt_ref) = refs

    @pl.when((pl.program_id(0) == 0) & (pl.program_id(1) == 0))
    def _():
        cnt_ref[...] = jnp.zeros_like(cnt_ref)

    tm = x_ref.shape[1]
    if transposed:
        oT = o_ref[0].astype(F32).reshape(MLA_HEADS * V_DIM, tm)
        o = oT.T.astype(BF16)
    else:
        o = _interleave_tiles(o_ref[0], ob_ref[0]).astype(BF16)
    y = _dot(o, wo_ref[...]) + bo_ref[...]
    xl = x_ref[0] + g1_ref[0] * y
    xl_ref[0] = xl
    fl = _rms(xl, gf_ref[...]) * (1.0 + sc_ref[0]) + sh_ref[0]
    _to_rows(fl_ref, fl)
    flh = fl.astype(BF16)
    fll = (fl - flh.astype(F32)).astype(BF16)
    logits = _dot(flh, wrh_ref[...]) + (_dot(fll, wrh_ref[...]) + _dot(flh, wrl_ref[...])) + br_ref[...]
    lane = lax.broadcasted_iota(jnp.int32, (tm, LANES), 1).astype(F32)
    neg = jnp.float32(-jnp.inf)
    work = jnp.where(lane < N_EXPERTS, logits, neg)
    vals, idxs = [], []
    onehot = jnp.zeros((tm, LANES), F32)
    for _ in range(TOP_K):
        mk = jnp.max(work, axis=-1, keepdims=True)
        ik = jnp.min(jnp.where(work == mk, lane, float(LANES)), axis=-1, keepdims=True)
        sel = lane == ik
        onehot = jnp.where(sel, 1.0, onehot)
        work = jnp.where(sel, neg, work)
        vals.append(mk)
        idxs.append(ik)
    es = [jnp.exp(v - vals[0]) for v in vals]
    den = es[0] + es[1] + es[2] + es[3]
    pre = _dot(tri_ref[...], onehot.astype(BF16)) + cnt_ref[...]
    ti = jnp.zeros((tm, LANES), F32)
    gt = jnp.zeros((tm, LANES), F32)
    rk = jnp.zeros((tm, LANES), F32)
    for kk in range(TOP_K):
        rank = jnp.sum(jnp.where(lane == idxs[kk], pre, 0.0), axis=-1, keepdims=True)
        ti = jnp.where(lane == kk, idxs[kk], ti)
        gt = jnp.where(lane == kk, es[kk] / den, gt)
        rk = jnp.where(lane == kk, rank, rk)
    ti_ref[...] = ti[:, :TOP_K].astype(jnp.int32)
    gt_ref[...] = gt[:, :TOP_K]
    rk_ref[...] = rk[:, :TOP_K].astype(jnp.int32)
    cnt_ref[...] += jnp.sum(onehot, axis=0, keepdims=True)


def _post(o, wo, bo, x, g1, gf, sh, sc, wrh, wrl, br, *, transposed, tm):
    bsz, n, d = x.shape
    t = bsz * n
    nt = n // tm
    tri = (lax.broadcasted_iota(jnp.int32, (tm, tm), 0) > lax.broadcasted_iota(jnp.int32, (tm, tm), 1)).astype(BF16)
    full = lambda a: pl.BlockSpec(a.shape, lambda b, i: (0,) * a.ndim)
    per_b = pl.BlockSpec((1, 1, d), lambda b, i: (b, 0, 0))
    if transposed:
        o_args = [o]
        o_specs = [pl.BlockSpec((1, MLA_HEADS, V_DIM, tm), lambda b, i: (b, 0, 0, i))]
    else:
        o_args = list(o)
        o_specs = [pl.BlockSpec((1, tm, d // 2), lambda b, i: (b, i, 0))] * 2
    tok = lambda w: pl.BlockSpec((tm, w), lambda b, i: (b * nt + i, 0))
    return pl.pallas_call(
        functools.partial(_post_body, transposed=transposed),
        grid=(bsz, nt),
        in_specs=[*o_specs, full(wo), full(bo), pl.BlockSpec((1, tm, d), lambda b, i: (b, i, 0)), per_b, full(gf),
                  per_b, per_b, full(wrh), full(wrl), full(br), full(tri)],
        out_specs=[pl.BlockSpec((1, tm, d), lambda b, i: (b, i, 0)),
                   pl.BlockSpec((tm * ROW_SUB, LANES), lambda b, i: (b * nt + i, 0)),
                   tok(TOP_K), tok(TOP_K), tok(TOP_K), pl.BlockSpec((1, LANES), lambda b, i: (0, 0))],
        out_shape=[jax.ShapeDtypeStruct((bsz, n, d), F32), jax.ShapeDtypeStruct((t * ROW_SUB, LANES), F32),
                   jax.ShapeDtypeStruct((t, TOP_K), jnp.int32), jax.ShapeDtypeStruct((t, TOP_K), F32),
                   jax.ShapeDtypeStruct((t, TOP_K), jnp.int32), jax.ShapeDtypeStruct((1, LANES), F32)],
        compiler_params=_cp(("arbitrary", "arbitrary")),
        name="post_attn" if transposed else "post_hyena",
    )(*o_args, wo, bo, x, g1, gf, sh, sc, wrh, wrl, br, tri)


ROW_SUB = 8


def _row_slice(i):
    return pl.ds(pl.multiple_of(i * ROW_SUB, ROW_SUB), ROW_SUB)


def _to_rows(ref, x):
    for s in range(ROW_SUB):
        ref[pl.ds(s, x.shape[0], stride=ROW_SUB), :] = x[:, s * LANES:(s + 1) * LANES]


def _from_rows(ref, lo, hi):
    return jnp.concatenate([ref[pl.ds(lo * ROW_SUB + s, hi - lo, stride=ROW_SUB), :] for s in range(ROW_SUB)], axis=1)


def _dispatch_body(pe_ref, pd_ref, dest_ref, fl_ref, xs_out, zbuf, sem, *, td):
    @pl.when(pl.program_id(0) == 0)
    def _():
        zbuf[...] = jnp.zeros(zbuf.shape, zbuf.dtype)

        def fill(e):
            start = pl.multiple_of((pe_ref[e] - MOE_TM) * ROW_SUB, ROW_SUB)
            return pltpu.make_async_copy(zbuf, xs_out.at[pl.ds(start, MOE_TM * ROW_SUB)], sem)

        for e in range(N_EXPERTS):
            @pl.when(pd_ref[e] > 0)
            def _():
                fill(e).start()
        for e in range(N_EXPERTS):
            @pl.when(pd_ref[e] > 0)
            def _():
                fill(e).wait()

    def issue(t, carry):
        for kk in range(TOP_K):
            d = dest_ref[0, 0, t * TOP_K + kk]
            pltpu.make_async_copy(fl_ref.at[_row_slice(t)], xs_out.at[_row_slice(d)], sem).start(priority=kk % 2)
        return carry

    lax.fori_loop(0, td, issue, 0, unroll=2)

    def drain(t, carry):
        pltpu.make_async_copy(fl_ref.at[_row_slice(0)], xs_out.at[_row_slice(0)], sem).wait()
        return carry

    lax.fori_loop(0, td * TOP_K, drain, 0, unroll=8)


def _dispatch(pad_end, padded, dest, fl, n_rows, *, td):
    t = fl.shape[0] // ROW_SUB
    dest3 = dest.reshape(t // td, 1, td * TOP_K)
    grid_spec = pltpu.PrefetchScalarGridSpec(
        num_scalar_prefetch=2,
        grid=(t // td,),
        in_specs=[
            pl.BlockSpec((1, 1, td * TOP_K), lambda i, pe, pd: (i, 0, 0), memory_space=pltpu.SMEM),
            pl.BlockSpec((td * ROW_SUB, LANES), lambda i, pe, pd: (i, 0)),
        ],
        out_specs=pl.BlockSpec(memory_space=pl.ANY),
        scratch_shapes=[pltpu.VMEM((MOE_TM * ROW_SUB, LANES), fl.dtype), pltpu.SemaphoreType.DMA(())],
    )
    return pl.pallas_call(
        functools.partial(_dispatch_body, td=td),
        grid_spec=grid_spec,
        out_shape=jax.ShapeDtypeStruct((n_rows * ROW_SUB, LANES), fl.dtype),
        compiler_params=_cp(("arbitrary",)),
        name="moe_dispatch",
    )(pad_end, padded, dest3, fl)


def _expert_body(be_ref, nu_ref, xs_ref, win_ref, bin_ref, wout_ref, bout_ref, ys_ref, win_s, wout_s):
    b = pl.program_id(0)
    dff = wout_ref.shape[1]

    @pl.when(b < nu_ref[0])
    def _():
        prev = be_ref[jnp.maximum(b - 1, 0)]

        @pl.when((b == 0) | (prev != be_ref[b]))
        def _():
            win_s[...] = win_ref[0].astype(BF16)
            wout_s[...] = wout_ref[0].astype(BF16)

        x = _from_rows(xs_ref, 0, xs_ref.shape[0] // ROW_SUB).astype(BF16)
        gu = _dot(x, win_s[...]) + bin_ref[0]
        gate = jnp.minimum(gu[:, :dff], SWIGLU_LIMIT)
        lin = jnp.clip(gu[:, dff:], -SWIGLU_LIMIT, SWIGLU_LIMIT)
        act = gate * jax.nn.sigmoid(SWIGLU_ALPHA * gate) * (lin + 1.0)
        _to_rows(ys_ref, _dot(act.astype(BF16), wout_s[...]) + bout_ref[0])

    @pl.when(b >= nu_ref[0])
    def _():
        ys_ref[...] = jnp.zeros_like(ys_ref)


def _experts(blk_exp, n_used, xs, layer, w_in, b_in, w_out, b_out):
    n_rows = xs.shape[0] // ROW_SUB
    depth, ne, d, f2 = w_in.shape
    dff = w_out.shape[2]
    tm = MOE_TM
    grid_spec = pltpu.PrefetchScalarGridSpec(
        num_scalar_prefetch=2,
        grid=(n_rows // tm,),
        in_specs=[
            pl.BlockSpec((tm * ROW_SUB, LANES), lambda b, be, nu: (jnp.minimum(b, nu[0] - 1), 0)),
            pl.BlockSpec((None, 1, d, f2), lambda b, be, nu: (layer, be[b], 0, 0)),
            pl.BlockSpec((None, 1, 1, f2), lambda b, be, nu: (layer, be[b], 0, 0)),
            pl.BlockSpec((None, 1, dff, d), lambda b, be, nu: (layer, be[b], 0, 0)),
            pl.BlockSpec((None, 1, 1, d), lambda b, be, nu: (layer, be[b], 0, 0)),
        ],
        out_specs=pl.BlockSpec((tm * ROW_SUB, LANES), lambda b, be, nu: (b, 0)),
        scratch_shapes=[pltpu.VMEM((d, f2), BF16), pltpu.VMEM((dff, d), BF16)],
    )
    return pl.pallas_call(
        _expert_body,
        grid_spec=grid_spec,
        out_shape=jax.ShapeDtypeStruct(xs.shape, F32),
        compiler_params=_cp(("arbitrary",)),
        name="moe_experts",
    )(blk_exp, n_used, xs, w_in, b_in.reshape(depth, ne, 1, f2), w_out, b_out.reshape(depth, ne, 1, d))


def _combine_body(dest_ref, ys_hbm, gt_ref, xl_ref, g2_ref, fg_ref, out_ref, buf, sem, *, tc, final):
    def issue(t, carry):
        for kk in range(TOP_K):
            d = dest_ref[0, 0, t * TOP_K + kk]
            pltpu.make_async_copy(ys_hbm.at[_row_slice(d)], buf.at[_row_slice(kk * tc + t)], sem).start(
                priority=kk % 2)
        return carry

    lax.fori_loop(0, tc, issue, 0, unroll=2)

    def drain(t, carry):
        pltpu.make_async_copy(ys_hbm.at[_row_slice(0)], buf.at[_row_slice(0)], sem).wait()
        return carry

    lax.fori_loop(0, tc * TOP_K, drain, 0, unroll=8)
    gt = gt_ref[...]
    y = gt[:, 0:1] * _from_rows(buf, 0, tc)
    for kk in range(1, TOP_K):
        y = y + gt[:, kk:kk + 1] * _from_rows(buf, kk * tc, (kk + 1) * tc)
    xl = xl_ref[0] + g2_ref[0] * y
    out_ref[0] = _rms(xl, fg_ref[...]) if final else xl


def _combine(dest, ys, gates, xl, g2, fg, *, tc, final):
    bsz, n, d = xl.shape
    t = bsz * n
    nt = n // tc
    dest3 = dest.reshape(t // tc, 1, tc * TOP_K)
    return pl.pallas_call(
        functools.partial(_combine_body, tc=tc, final=final),
        grid=(bsz, nt),
        in_specs=[
            pl.BlockSpec((1, 1, tc * TOP_K), lambda b, i: (b * nt + i, 0, 0), memory_space=pltpu.SMEM),
            pl.BlockSpec(memory_space=pl.ANY),
            pl.BlockSpec((tc, TOP_K), lambda b, i: (b * nt + i, 0)),
            pl.BlockSpec((1, tc, d), lambda b, i: (b, i, 0)),
            pl.BlockSpec((1, 1, d), lambda b, i: (b, 0, 0)),
            pl.BlockSpec((1, d), lambda b, i: (0, 0)),
        ],
        out_specs=pl.BlockSpec((1, tc, d), lambda b, i: (b, i, 0)),
        out_shape=jax.ShapeDtypeStruct((bsz, n, d), F32),
        scratch_shapes=[pltpu.VMEM((TOP_K * tc * ROW_SUB, LANES), F32), pltpu.SemaphoreType.DMA(())],
        compiler_params=_cp(("arbitrary", "arbitrary")),
        name="moe_combine",
    )(dest3, ys, gates, xl, g2, fg)


def _moe(fl, topi, gates, rank, cnt, xl, g2, fg, layer, w_in, b_in, w_out, b_out, *, final):
    t = fl.shape[0] // ROW_SUB
    tm = MOE_TM
    counts = cnt[0, :N_EXPERTS].astype(jnp.int32)
    padded = (counts + tm - 1) // tm * tm
    pad_end = jnp.cumsum(padded)
    pad_start = pad_end - padded
    dest = jnp.take(pad_start, topi) + rank
    nb = t * TOP_K // tm + N_EXPERTS
    blk_start = jnp.arange(nb, dtype=jnp.int32) * tm
    blk_exp = jnp.minimum(jnp.sum((pad_end[None, :] <= blk_start[:, None]).astype(jnp.int32), axis=1), N_EXPERTS - 1)
    n_used = (pad_end[-1:] // tm).astype(jnp.int32)
    xs = _dispatch(pad_end, padded, dest, fl, nb * tm, td=512)
    ys = _experts(blk_exp, n_used, xs, layer, w_in, b_in, w_out, b_out)
    return _combine(dest, ys, gates, xl, g2, fg, tc=512, final=final)


def _hy_in_body(x_ref, xp_ref, xn_ref, g_ref, sh_ref, sc_ref, w_ref, b_ref, cw_ref, cb_ref, o_ref, *, nt):
    i = pl.program_id(1)
    d = x_ref.shape[2]

    def normed(xx):
        return (_rms(xx, g_ref[...]) * (1.0 + sc_ref[0]) + sh_ref[0]).astype(BF16)

    h = normed(x_ref[0])
    hh = normed(jnp.concatenate([xp_ref[0], xn_ref[0]], axis=0))
    tm = h.shape[0]
    row = lax.broadcasted_iota(jnp.int32, (tm, 1), 0)
    for j in range(3):
        cols = slice(j * d, (j + 1) * d)
        p = _dot(h, w_ref[:, cols]) + b_ref[:, cols]
        ph = _dot(hh, w_ref[:, cols]) + b_ref[:, cols]
        prev = jnp.where(i > 0, ph[7:8], 0.0)
        nxt = jnp.where(i < nt - 1, ph[8:9], 0.0)
        up = jnp.where(row == 0, prev, pltpu.roll(p, 1, axis=0))
        dn = jnp.where(row == tm - 1, nxt, pltpu.roll(p, tm - 1, axis=0))
        cw = cw_ref[:, cols]
        o_ref[j, 0] = up * cw[0:1] + p * cw[1:2] + dn * cw[2:3] + cb_ref[:, cols]


def _hy_in(x, g, sh, sc, w, b, cw, cb, *, tm):
    bsz, n, d = x.shape
    nt = n // tm
    hb = tm // 8
    per_b = pl.BlockSpec((1, 1, d), lambda bb, i: (bb, 0, 0))
    full = lambda a: pl.BlockSpec(a.shape, lambda bb, i: (0,) * a.ndim)
    return pl.pallas_call(
        functools.partial(_hy_in_body, nt=nt),
        grid=(bsz, nt),
        in_specs=[
            pl.BlockSpec((1, tm, d), lambda bb, i: (bb, i, 0)),
            pl.BlockSpec((1, 8, d), lambda bb, i: (bb, jnp.maximum(i * hb - 1, 0), 0)),
            pl.BlockSpec((1, 8, d), lambda bb, i: (bb, jnp.minimum((i + 1) * hb, n // 8 - 1), 0)),
            full(g), per_b, per_b, full(w), full(b), full(cw), full(cb),
        ],
        out_specs=pl.BlockSpec((3, 1, tm, d), lambda bb, i: (0, bb, i, 0)),
        out_shape=jax.ShapeDtypeStruct((3, bsz, n, d), F32),
        compiler_params=_cp(("arbitrary", "arbitrary")),
        name="hyena_in_proj",
    )(x, x, x, g, sh, sc, w, b, cw, cb)


def _filt_feat_body(w1_ref, b1_ref, f1_ref, w2_ref, b2_ref, f2_ref, o_ref, *, n_lat):
    na = o_ref.shape[1]
    a = lax.broadcasted_iota(jnp.int32, (na, 1), 0)
    lane = lax.broadcasted_iota(jnp.int32, (na, LANES), 1)
    band_idx = jnp.where(lane <= HY_BANDS, lane - 1, lane - 1 - HY_BANDS).astype(F32)
    band = 1e-4 + band_idx * ((HY_BANDS - 1 - 1e-4) / (HY_BANDS - 1))
    phase = jnp.where(lane > HY_BANDS, 0.5 * math.pi, 0.0)
    for j in range(B_GROUP):
        r = a * FFT_N2 + (pl.program_id(0) * B_GROUP + j)
        pos = jnp.where(r < n_lat, r, 2 * n_lat - r).astype(F32)
        tn = pos / float(max(n_lat - 1, 1))
        ang = ((2.0 * math.pi / n_lat) * pos) * band
        z = jnp.where(lane == 0, tn, jnp.where(lane < HY_EMB, jnp.cos(ang + phase), 0.0))
        h1 = jnp.sin(f1_ref[...] * (_dot_hi(z, w1_ref[...]) + b1_ref[...]))
        h2 = jnp.sin(f2_ref[...] * (_dot_hi(h1, w2_ref[...]) + b2_ref[...]))
        valid = (r != n_lat).astype(F32)
        o_ref[j] = jnp.where(lane == HY_HID, tn, jnp.where(lane == HY_HID + 1, valid, h2))


def _filt_feat(w1, b1, f1, w2, b2, f2, *, n_lat):
    na = 2 * n_lat // FFT_N2
    w1p = jnp.zeros((LANES, LANES), F32).at[:HY_EMB, :HY_HID].set(w1)
    w2p = jnp.zeros((LANES, LANES), F32).at[:HY_HID, :HY_HID].set(w2)
    padv = lambda v: jnp.zeros((1, LANES), F32).at[0, :HY_HID].set(v)
    full = lambda shp: pl.BlockSpec(shp, lambda i: (0,) * len(shp))
    return pl.pallas_call(
        functools.partial(_filt_feat_body, n_lat=n_lat),
        grid=(FFT_N2 // B_GROUP,),
        in_specs=[full((LANES, LANES)), full((1, LANES)), full((1, LANES)),
                  full((LANES, LANES)), full((1, LANES)), full((1, LANES))],
        out_specs=pl.BlockSpec((B_GROUP, na, LANES), lambda i: (i, 0, 0)),
        out_shape=jax.ShapeDtypeStruct((FFT_N2, na, LANES), F32),
        compiler_params=_cp(("arbitrary",)),
        name="hyena_filter_features",
    )(w1p, padv(b1), padv(f1), w2p, padv(b2), padv(f2))


U32 = jnp.uint32
HI16 = 0xFFFF0000


def _pack_c(re, im):
    lo = lax.bitcast_convert_type(re.astype(BF16).astype(F32), U32) >> 16
    hi = lax.bitcast_convert_type(im.astype(BF16).astype(F32), U32) & U32(HI16)
    return hi | lo


def _unpack_c(u):
    re = lax.bitcast_convert_type(u << 16, F32)
    im = lax.bitcast_convert_type(u & U32(HI16), F32)
    return jnp.concatenate([re, im], axis=0).astype(BF16)


def _interleave_tiles(xa, xb):
    tiles = []
    for c0 in range(0, xa.shape[-1], LANES):
        tiles += [xa[..., c0:c0 + LANES], xb[..., c0:c0 + LANES]]
    return jnp.concatenate(tiles, axis=-1)


def _filt_s1_body(hd_ref, w3_ref, dec_ref, tab_ref, oa_ref, ob_ref):
    na = hd_ref.shape[1]
    ha = na // 2
    oa2, ob2 = _rows2d(oa_ref), _rows2d(ob_ref)
    ft = hd_ref[:, :ha, :].reshape(B_GROUP * ha, LANES)
    fb = hd_ref[:, ha:, :].reshape(B_GROUP * ha, LANES)
    top = _dot(ft.astype(BF16), w3_ref[0, 0].astype(BF16))
    top = top * jnp.exp(-ft[:, HY_HID:HY_HID + 1] * jnp.abs(dec_ref[0, 0]))
    bot = _dot(fb.astype(BF16), w3_ref[0, 1].astype(BF16))
    bot = bot * (jnp.exp(-fb[:, HY_HID:HY_HID + 1] * jnp.abs(dec_ref[0, 1])) * fb[:, HY_HID + 1:HY_HID + 2])
    for j in range(B_GROUP):
        hb = jnp.concatenate([top[j * ha:(j + 1) * ha], bot[j * ha:(j + 1) * ha]], axis=0).astype(BF16)
        r = _dot(tab_ref[j], hb)
        packed = _pack_c(r[:na], r[na:])
        oa2[pl.ds(j, na, stride=B_GROUP), :] = packed[:, :LANES]
        ob2[pl.ds(j, na, stride=B_GROUP), :] = packed[:, LANES:]


def _filt_s1(hd, w3r, dec, tab):
    _, na, _ = hd.shape
    d = w3r.shape[-1]
    ct = 2 * LANES
    half = pl.BlockSpec((None, na, B_GROUP, LANES), lambda o, g, c: (o, 0, g, c))
    return pl.pallas_call(
        _filt_s1_body,
        grid=(2, FFT_N2 // B_GROUP, d // ct),
        in_specs=[
            pl.BlockSpec((B_GROUP, na, LANES), lambda o, g, c: (g, 0, 0)),
            pl.BlockSpec((1, 2, LANES, ct), lambda o, g, c: (o, 0, 0, c)),
            pl.BlockSpec((1, 2, 1, ct), lambda o, g, c: (o, 0, 0, c)),
            pl.BlockSpec((B_GROUP, 2 * na, na), lambda o, g, c: (g, 0, 0)),
        ],
        out_specs=[half, half],
        out_shape=[jax.ShapeDtypeStruct((2, na, FFT_N2, d // 2), U32)] * 2,
        compiler_params=_cp(("arbitrary", "arbitrary", "arbitrary")),
        name="hyena_filter_dft1",
    )(hd, w3r, dec, tab)


S2_KB = 4


def _s2_body(*refs, conv):
    if conv:
        oa_ref, ob_ref, kf_ref, ff_ref, fi_ref, g_ref = refs
    else:
        oa_ref, ob_ref, ff_ref, g_ref = refs
    for u in range(S2_KB):
        xf = _dot(ff_ref[...], _unpack_c(_interleave_tiles(oa_ref[u], ob_ref[u])))
        if conv:
            xr, xi = xf[:FFT_N2], xf[FFT_N2:]
            kr = kf_ref[0, u].astype(F32)
            ki = kf_ref[1, u].astype(F32)
            y = jnp.concatenate([xr * kr - xi * ki, xr * ki + xi * kr], axis=0).astype(BF16)
            xf = _dot(fi_ref[...], y)
            g_ref[u] = _pack_c(xf[:FFT_N2], xf[FFT_N2:])
        else:
            g_ref[0, u] = xf[:FFT_N2].astype(BF16)
            g_ref[1, u] = xf[FFT_N2:].astype(BF16)


def _s2(o3, kf, order, ff, fi, *, ct, conv):
    n1 = o3[0].shape[-3]
    d = 2 * o3[0].shape[-1]
    full = lambda a: pl.BlockSpec(a.shape, lambda k, c: (0,) * a.ndim)
    nk = n1 // S2_KB
    if conv:
        hblk = pl.BlockSpec((S2_KB, FFT_N2, ct // 2), lambda k, c: (k, 0, c))
        in_specs = [hblk, hblk, pl.BlockSpec((None, 2, S2_KB, FFT_N2, ct), lambda k, c: (order, 0, k, 0, c)),
                    full(ff), full(fi)]
        args = (o3[0], o3[1], kf, ff, fi)
        grid = (nk, d // ct)
        out_specs = pl.BlockSpec((S2_KB, FFT_N2, ct), lambda k, c: (k, 0, c))
        out_shape = jax.ShapeDtypeStruct((n1, FFT_N2, d), U32)
    else:
        no = o3[0].shape[0]
        hblk = pl.BlockSpec((None, S2_KB, FFT_N2, ct // 2), lambda k, c: (k // nk, k % nk, 0, c))
        in_specs = [hblk, hblk, full(ff)]
        args = (o3[0], o3[1], ff)
        grid = (no * nk, d // ct)
        out_specs = pl.BlockSpec((None, 2, S2_KB, FFT_N2, ct), lambda k, c: (k // nk, 0, k % nk, 0, c))
        out_shape = jax.ShapeDtypeStruct((no, 2, n1, FFT_N2, d), BF16)
    return pl.pallas_call(
        functools.partial(_s2_body, conv=conv),
        grid=grid,
        in_specs=in_specs,
        out_specs=out_specs,
        out_shape=out_shape,
        compiler_params=_cp(("arbitrary", "arbitrary")),
        name="hyena_conv_dft2" if conv else "hyena_filter_dft2",
    )(*args)


def _rows2d(ref):
    lead = ref.shape[:-3]
    return ref.reshape(lead + (ref.shape[-3] * B_GROUP, ref.shape[-1]))


def _s1_body(za_ref, zb_ref, tab_ref, oa_ref, ob_ref):
    rows, n1 = za_ref.shape[0], oa_ref.shape[0]
    za2, zb2, oa2, ob2 = _rows2d(za_ref), _rows2d(zb_ref), _rows2d(oa_ref), _rows2d(ob_ref)
    for j in range(B_GROUP):
        sl = pl.ds(j, rows, stride=B_GROUP)
        zj = jnp.concatenate([za2[sl, :], zb2[sl, :]], axis=1)
        r = _dot(tab_ref[j], zj.astype(BF16))
        packed = _pack_c(r[:n1], r[n1:])
        oa2[pl.ds(j, n1, stride=B_GROUP), :] = packed[:, :LANES]
        ob2[pl.ds(j, n1, stride=B_GROUP), :] = packed[:, LANES:]


def _s1(z4, zi, tab):
    _, rows, _, d = z4.shape
    n1 = tab.shape[1] // 2
    half = pl.BlockSpec((n1, B_GROUP, LANES), lambda g, c: (0, g, c))
    return pl.pallas_call(
        _s1_body,
        grid=(FFT_N2 // B_GROUP, d // (2 * LANES)),
        in_specs=[
            pl.BlockSpec((None, rows, B_GROUP, LANES), lambda g, c: (zi, 0, g, 2 * c)),
            pl.BlockSpec((None, rows, B_GROUP, LANES), lambda g, c: (zi, 0, g, 2 * c + 1)),
            pl.BlockSpec((B_GROUP, 2 * n1, rows), lambda g, c: (g, 0, 0)),
        ],
        out_specs=[half, half],
        out_shape=[jax.ShapeDtypeStruct((n1, FFT_N2, d // 2), U32)] * 2,
        compiler_params=_cp(("arbitrary", "arbitrary")),
        name="hyena_conv_dft1",
    )(z4, z4, tab)


def _s3_body(ga_ref, gb_ref, tab_ref, gta_ref, gtb_ref, za_ref, zb_ref, fb_ref, *rest, chain):
    if chain:
        tab1_ref, oa_ref, ob_ref, qa_ref, qb_ref = rest
    else:
        oa_ref, ob_ref = rest
    n1, rows = ga_ref.shape[0], oa_ref.shape[0]
    ga2, gb2, oa2, ob2 = _rows2d(ga_ref), _rows2d(gb_ref), _rows2d(oa_ref), _rows2d(ob_ref)
    for j in range(B_GROUP):
        sl = pl.ds(j, n1, stride=B_GROUP)
        gj = _unpack_c(jnp.concatenate([ga2[sl, :], gb2[sl, :]], axis=1))
        y = _dot(tab_ref[j], gj)
        oa2[pl.ds(j, rows, stride=B_GROUP), :] = y[:, :LANES]
        ob2[pl.ds(j, rows, stride=B_GROUP), :] = y[:, LANES:]
    fb = fb_ref[...]
    oa_ref[...] = gta_ref[...] * (oa_ref[...] + za_ref[...] * fb[:, :, :LANES])
    ob_ref[...] = gtb_ref[...] * (ob_ref[...] + zb_ref[...] * fb[:, :, LANES:])
    if chain:
        qa2, qb2 = _rows2d(qa_ref), _rows2d(qb_ref)
        for j in range(B_GROUP):
            sl = pl.ds(j, rows, stride=B_GROUP)
            zj = jnp.concatenate([oa2[sl, :], ob2[sl, :]], axis=1)
            r = _dot(tab1_ref[j], zj.astype(BF16))
            packed = _pack_c(r[:n1], r[n1:])
            qa2[pl.ds(j, n1, stride=B_GROUP), :] = packed[:, :LANES]
            qb2[pl.ds(j, n1, stride=B_GROUP), :] = packed[:, LANES:]


def _s3(g3, tab, gate4, gi, zsrc, fb, tab1):
    n1, _, d = g3.shape
    rows = tab.shape[1]
    chain = tab1 is not None
    half = pl.BlockSpec((rows, B_GROUP, LANES), lambda g, c: (0, g, c))
    chalf = pl.BlockSpec((n1, B_GROUP, LANES), lambda g, c: (0, g, c))
    nat = lambda idx, par: pl.BlockSpec((None, rows, B_GROUP, LANES), lambda g, c: (idx, 0, g, 2 * c + par))
    if isinstance(zsrc[1], int):
        z_specs, z_args = [nat(zsrc[1], 0), nat(zsrc[1], 1)], [zsrc[0], zsrc[0]]
    else:
        z_specs, z_args = [half, half], list(zsrc)
    in_specs = [
        pl.BlockSpec((n1, B_GROUP, LANES), lambda g, c: (0, g, 2 * c)),
        pl.BlockSpec((n1, B_GROUP, LANES), lambda g, c: (0, g, 2 * c + 1)),
        pl.BlockSpec((B_GROUP, rows, 2 * n1), lambda g, c: (g, 0, 0)),
        nat(gi, 0), nat(gi, 1), *z_specs,
        pl.BlockSpec((1, 1, 2 * LANES), lambda g, c: (0, 0, c)),
    ]
    args = [g3, g3, tab, gate4, gate4, *z_args, fb]
    zshape = jax.ShapeDtypeStruct((rows, FFT_N2, d // 2), F32)
    out_specs, out_shape = [half, half], [zshape, zshape]
    if chain:
        in_specs.append(pl.BlockSpec((B_GROUP, 2 * n1, rows), lambda g, c: (g, 0, 0)))
        args.append(tab1)
        out_specs += [chalf, chalf]
        out_shape += [jax.ShapeDtypeStruct((n1, FFT_N2, d // 2), U32)] * 2
    return pl.pallas_call(
        functools.partial(_s3_body, chain=chain),
        grid=(FFT_N2 // B_GROUP, d // (2 * LANES)),
        in_specs=in_specs,
        out_specs=out_specs,
        out_shape=out_shape,
        compiler_params=_cp(("arbitrary", "arbitrary")),
        name="hyena_conv_idft1_dft1" if chain else "hyena_conv_idft1",
    )(*args)


def _dft_tables(n_lat):
    n = 2 * n_lat
    n1 = n // FFT_N2
    k1 = jnp.arange(n1, dtype=jnp.int32)
    th_a = ((k1[:, None] * k1[None, :]) % n1).astype(F32) * (2.0 * math.pi / n1)
    th_b = (jnp.arange(FFT_N2, dtype=jnp.int32)[:, None] * k1[None, :]).astype(F32) * (2.0 * math.pi / n)
    ca, sa = jnp.cos(th_a)[None], jnp.sin(th_a)[None]
    cb, sb = jnp.cos(th_b)[:, :, None], jnp.sin(th_b)[:, :, None]
    cr = ca * cb - sa * sb
    sn = sa * cb + ca * sb
    ha = n1 // 2
    crh, snh = cr[:, :, :ha], sn[:, :, :ha]
    w1 = jnp.concatenate([jnp.concatenate([crh, snh], axis=2), jnp.concatenate([-snh, crh], axis=2)], axis=1)
    w1f = jnp.concatenate([cr, -sn], axis=1)
    v = jnp.swapaxes(w1, 1, 2) * (1.0 / n)
    k2 = jnp.arange(FFT_N2, dtype=jnp.int32)
    th2 = ((k2[:, None] * k2[None, :]) % FFT_N2).astype(F32) * (2.0 * math.pi / FFT_N2)
    c2, s2 = jnp.cos(th2), jnp.sin(th2)
    ff = jnp.concatenate([jnp.concatenate([c2, s2], axis=1), jnp.concatenate([-s2, c2], axis=1)], axis=0)
    fi = jnp.concatenate([jnp.concatenate([c2, -s2], axis=1), jnp.concatenate([s2, c2], axis=1)], axis=0)
    return w1.astype(BF16), w1f.astype(BF16), v.astype(BF16), ff.astype(BF16), fi.astype(BF16)


def _hyena_mix(proj3, fparams, fbias, *, n_lat):
    _, bsz, _, d = proj3.shape
    f_w1, f_b1, f_f1, f_w2, f_b2, f_f2, f_w3, decay = fparams
    w1, w1f, v, ff, fi = _dft_tables(n_lat)
    hd = _filt_feat(f_w1, f_b1, f_f1, f_w2, f_b2, f_f2, n_lat=n_lat)
    w3r = jnp.transpose(f_w3.reshape(HY_HID, 2, 2, d), (1, 2, 0, 3))
    w3r = jnp.zeros((2, 2, LANES, d), F32).at[:, :, :HY_HID].set(w3r)
    kf1 = _filt_s1(hd, w3r, decay.reshape(2, 2, 1, d), w1f)
    kf = _s2(kf1, None, 0, ff, None, ct=d, conv=False)
    p3 = proj3.reshape(3, bsz * (n_lat // FFT_N2), FFT_N2, d)
    o1 = _s1(p3, 2, w1)
    g = _s2(o1, kf, 0, ff, fi, ct=d, conv=True)
    za, zb, *o1 = _s3(g, v, p3, 0, (p3, 2), fbias[0].reshape(1, 1, d), w1)
    g = _s2(o1, kf, 1, ff, fi, ct=d, conv=True)
    za, zb = _s3(g, v, p3, 1, (za, zb), fbias[1].reshape(1, 1, d), None)
    return za.reshape(bsz, n_lat, d // 2), zb.reshape(bsz, n_lat, d // 2)


def _rope_tables(n_tokens):
    rows = n_tokens // GRID_W
    row = jnp.broadcast_to(jnp.arange(rows, dtype=F32)[:, None], (rows, GRID_W)).reshape(-1)
    col = jnp.broadcast_to(jnp.arange(GRID_W, dtype=F32)[None, :], (rows, GRID_W)).reshape(-1)
    axis_dim = QK_ROPE // 2
    inv_freq = 1.0 / (ROPE_THETA ** (jnp.arange(0, axis_dim, 2, dtype=F32) / axis_dim))
    ang = jnp.concatenate([row[:, None] * inv_freq, col[:, None] * inv_freq], axis=-1)
    return jnp.cos(ang), jnp.sin(ang)


def _mla_weights(w_down, g_q, w_uq, g_kv, w_ukv):
    d = w_down.shape[0]
    nh = MLA_HEADS
    kpe = w_down[:, Q_LORA + KV_LORA:]
    w1, w2 = kpe[:, 0::2], kpe[:, 1::2]
    z = jnp.zeros((d, LANES - QK_ROPE), w_down.dtype)
    wd = jnp.concatenate([w_down[:, :Q_LORA + KV_LORA], w1, w2, z, w2, w1, z], axis=1).astype(BF16)
    uq = w_uq.reshape(Q_LORA, nh, QK_NOPE + QK_ROPE)
    pe = uq[:, :, QK_NOPE:]
    uq = jnp.concatenate([uq[:, :, :QK_NOPE], pe[:, :, 0::2], pe[:, :, 1::2]], axis=2)
    wuqT = uq.reshape(Q_LORA, nh * (QK_NOPE + QK_ROPE)).T.astype(BF16)
    ukv = w_ukv.reshape(KV_LORA, nh, QK_NOPE + V_DIM)
    wuk = ukv[:, :, :QK_NOPE].reshape(KV_LORA, nh * QK_NOPE).astype(BF16)
    wuvT = ukv[:, :, QK_NOPE:].reshape(KV_LORA, nh * V_DIM).T.astype(BF16)
    return wd, g_q.reshape(1, -1), g_kv.reshape(1, -1), wuk, wuqT, wuvT


def kernel(x, c, ctx, c_ctx, ada_w, ada_b, norm_mix_g, norm_ffn_g, mla_w_down, mla_g_q, mla_w_uq, mla_g_kv, mla_w_ukv, mla_w_o, hy_w_in, hy_b_in, hy_conv_w, hy_conv_b, hy_f_w1, hy_f_b1, hy_f_freq1, hy_f_w2, hy_f_b2, hy_f_freq2, hy_f_w3, hy_decay, hy_bias, hy_w_out, hy_b_out, moe_w_r, moe_b_r, moe_w_in, moe_b_in, moe_w_out, moe_b_out, final_g):
    bsz, n_lat, d = x.shape
    n_ctx = ctx.shape[1]
    depth = ada_w.shape[0]
    assert bsz == 2 and d == MLA_HEADS * V_DIM and n_lat % 512 == 0 and n_ctx % 128 == 0
    assert depth == 2

    cond8 = jnp.zeros((8, d), F32).at[:bsz].set(c).at[bsz].set(c_ctx)
    mods = _ada(cond8, ada_w, ada_b)

    def mod(i, j, rows):
        return mods[i, rows, j * d:(j + 1) * d][:, None, :]

    lat_rows = slice(0, bsz)
    ctx_rows = slice(bsz, bsz + 1)
    xl = x
    for i in range(depth):
        kind, j = i % 2, i // 2
        sh1, sc1, g1 = (mod(i, m, lat_rows) for m in range(3))
        sh2, sc2, g2 = (mod(i, m, lat_rows) for m in range(3, 6))
        gm = norm_mix_g[i].reshape(1, d)
        if kind == 0:
            wts = _mla_weights(mla_w_down[j], mla_g_q[j], mla_w_uq[j], mla_g_kv[j], mla_w_ukv[j])
            cos, sin = _rope_tables(n_lat)
            zl = jnp.zeros((n_lat, LANES - QK_ROPE), F32)
            tabs = (jnp.concatenate([cos, cos, zl], axis=1), jnp.concatenate([-sin, sin, zl], axis=1), cos.T, sin.T)
            tq = tv = 512
            tk = 2048 if n_lat % 4096 == 0 else 512
            qT, k, vT = _mla_proj(xl, gm, sh1, sc1, wts, tabs, need_q=True, tm=tv, tk=tv)
            half = QK_ROPE // 2
            one_c = jnp.concatenate([jnp.ones((n_ctx, QK_ROPE), F32), jnp.zeros((n_ctx, LANES - QK_ROPE), F32)], axis=1)
            tabs_c = (one_c, jnp.zeros((n_ctx, LANES), F32), jnp.ones((half, n_ctx), F32), jnp.zeros((half, n_ctx), F32))
            kc, vTc = _mla_proj(ctx, gm, mod(i, 0, ctx_rows), mod(i, 1, ctx_rows), wts, tabs_c,
                                need_q=False, tm=n_ctx, tk=n_ctx)
            o = _attention(qT, k, vT, kc, vTc, tq=tq, tk=tk)
            wo = mla_w_o[j].astype(BF16)
            bo = jnp.zeros((1, d), F32)
            transposed = True
        else:
            proj3 = _hy_in(xl, gm, sh1, sc1, hy_w_in[j].astype(BF16), hy_b_in[j].reshape(1, -1), hy_conv_w[j],
                           hy_conv_b[j].reshape(1, -1), tm=512)
            fparams = (hy_f_w1[j], hy_f_b1[j], hy_f_freq1[j], hy_f_w2[j], hy_f_b2[j], hy_f_freq2[j], hy_f_w3[j],
                       hy_decay[j])
            o = _hyena_mix(proj3, fparams, hy_bias[j], n_lat=n_lat)
            wo = hy_w_out[j].astype(BF16)
            bo = hy_b_out[j].reshape(1, d)
            transposed = False
        wr = jnp.zeros((d, LANES), F32).at[:, :N_EXPERTS].set(moe_w_r[i])
        wrh = wr.astype(BF16)
        wrl = (wr - wrh.astype(F32)).astype(BF16)
        br = jnp.zeros((1, LANES), F32).at[0, :N_EXPERTS].set(moe_b_r[i])
        xl, fl, topi, gates, rank, cnt = _post(o, wo, bo, xl, g1, norm_ffn_g[i].reshape(1, d), sh2, sc2, wrh, wrl, br,
                                               transposed=transposed, tm=512)
        xl = _moe(fl, topi, gates, rank, cnt, xl, g2, final_g.reshape(1, d), i, moe_w_in, moe_b_in,
                  moe_w_out, moe_b_out, final=(i == depth - 1))
    return xl
```

```python
import functools
import math

import jax
import jax.numpy as jnp
from jax import lax
from jax.experimental import pallas as pl
from jax.experimental.pallas import tpu as pltpu

F32 = jnp.float32
BF16 = jnp.bfloat16

EPS = 1e-6
GRID_W = 64
MLA_HEADS = 8
QK_NOPE = 128
QK_ROPE = 64
V_DIM = 128
Q_LORA = 512
KV_LORA = 256
ROPE_THETA = 10000.0
MLA_SCALE = (QK_NOPE + QK_ROPE) ** -0.5
QK_PAD = 256

HY_EMB = 33
HY_BANDS = (HY_EMB - 1) // 2
HY_HID = 64
FFT_N2 = 128
B_GROUP = 8

N_EXPERTS = 32
TOP_K = 4
SWIGLU_LIMIT = 7.0
SWIGLU_ALPHA = 1.702
MOE_TM = 512
LANES = 128

VMEM_LIMIT = 56 * 1024 * 1024


def _cp(sem, vmem=VMEM_LIMIT):
    return pltpu.CompilerParams(dimension_semantics=sem, vmem_limit_bytes=vmem)


def _dot(a, b):
    return jnp.dot(a, b, preferred_element_type=F32)


def _dot_hi(a, b):
    return jnp.dot(a, b, preferred_element_type=F32, precision=lax.Precision.HIGHEST)


def _rms(x, g):
    return x * lax.rsqrt(jnp.mean(x * x, axis=-1, keepdims=True) + EPS) * g


def _ada_body(c_ref, w_ref, b_ref, o_ref):
    c = c_ref[...]
    s = c * jax.nn.sigmoid(c)
    o_ref[0] = _dot(s.astype(BF16), w_ref[0].astype(BF16)) + b_ref[0]


def _ada(cond8, ada_w, ada_b):
    depth, d, n = ada_w.shape
    tn = n // 4
    return pl.pallas_call(
        _ada_body,
        grid=(depth, n // tn),
        in_specs=[
            pl.BlockSpec((8, d), lambda i, j: (0, 0)),
            pl.BlockSpec((1, d, tn), lambda i, j: (i, 0, j)),
            pl.BlockSpec((1, 1, tn), lambda i, j: (i, 0, j)),
        ],
        out_specs=pl.BlockSpec((1, 8, tn), lambda i, j: (i, 0, j)),
        out_shape=jax.ShapeDtypeStruct((depth, 8, n), F32),
        compiler_params=_cp(("arbitrary", "arbitrary")),
        name="ada_mod",
    )(cond8, ada_w, ada_b.reshape(depth, 1, n))


def _mla_proj_body(x_ref, g_ref, sh_ref, sc_ref, wd_ref, gq_ref, gkv_ref, wuk_ref, wuqT_ref, wuvT_ref,
                   ct_ref, st_ref, cT_ref, sT_ref, *out_refs, need_q, tk):
    if need_q:
        qT_ref, k_ref, vT_ref = out_refs
    else:
        k_ref, vT_ref = out_refs
    nh = MLA_HEADS
    x = x_ref[0]
    h = _rms(x, g_ref[...]) * (1.0 + sc_ref[0]) + sh_ref[0]
    lat = _dot(h.astype(BF16), wd_ref[...])
    o_kv = Q_LORA
    o_a = Q_LORA + KV_LORA
    kvn = _rms(lat[:, o_kv:o_a], gkv_ref[...])
    kr = (lat[:, o_a:o_a + LANES] * ct_ref[...] + lat[:, o_a + LANES:o_a + 2 * LANES] * st_ref[...]).astype(BF16)
    knope = _dot(kvn.astype(BF16), wuk_ref[...])
    for hh in range(nh):
        k_ref[0, hh, :, 0:QK_NOPE] = knope[:, hh * QK_NOPE:(hh + 1) * QK_NOPE].astype(BF16)
        k_ref[0, hh, :, QK_NOPE:QK_PAD] = kr
    vT = _dot(wuvT_ref[...], kvn.T.astype(BF16))
    tm = x.shape[0]
    for hh in range(nh):
        for c in range(tm // tk):
            vT_ref[0, hh, c] = vT[hh * V_DIM:(hh + 1) * V_DIM, c * tk:(c + 1) * tk].astype(BF16)
    if need_q:
        qn = _rms(lat[:, :Q_LORA], gq_ref[...])
        qT = _dot(wuqT_ref[...], qn.T.astype(BF16)) * (MLA_SCALE * math.log2(math.e))
        c = cT_ref[...]
        s = sT_ref[...]
        hw = QK_NOPE + QK_ROPE
        half = QK_ROPE // 2
        for hh in range(nh):
            base = hh * hw
            x1 = qT[base + QK_NOPE:base + QK_NOPE + half]
            x2 = qT[base + QK_NOPE + half:base + hw]
            qT_ref[0, hh, 0:QK_NOPE] = qT[base:base + QK_NOPE].astype(BF16)
            qT_ref[0, hh, QK_NOPE:QK_NOPE + half] = (x1 * c - x2 * s).astype(BF16)
            qT_ref[0, hh, QK_NOPE + half:hw] = (x1 * s + x2 * c).astype(BF16)
            qT_ref[0, hh, hw:QK_PAD] = jnp.zeros((QK_PAD - hw, tm), BF16)


def _mla_proj(x, g, sh, sc, wts, tabs, *, need_q, tm, tk):
    bsz, n, d = x.shape
    nh = MLA_HEADS
    wd, gq, gkv, wuk, wuqT, wuvT = wts
    ct, st, cT, sT = tabs
    nsh = sh.shape[0]
    full = lambda a: pl.BlockSpec(a.shape, lambda b, i: (0,) * a.ndim)
    in_specs = [
        pl.BlockSpec((1, tm, d), lambda b, i: (b, i, 0)),
        full(g),
        pl.BlockSpec((1, 1, d), lambda b, i: (b % nsh, 0, 0)),
        pl.BlockSpec((1, 1, d), lambda b, i: (b % nsh, 0, 0)),
        full(wd), full(gq), full(gkv), full(wuk), full(wuqT), full(wuvT),
        pl.BlockSpec((tm, LANES), lambda b, i: (i, 0)),
        pl.BlockSpec((tm, LANES), lambda b, i: (i, 0)),
        pl.BlockSpec((QK_ROPE // 2, tm), lambda b, i: (0, i)),
        pl.BlockSpec((QK_ROPE // 2, tm), lambda b, i: (0, i)),
    ]
    out_specs = [
        pl.BlockSpec((1, nh, tm, QK_PAD), lambda b, i: (b, 0, i, 0)),
        pl.BlockSpec((1, nh, tm // tk, V_DIM, tk), lambda b, i: (b, 0, i, 0, 0)),
    ]
    out_shape = [
        jax.ShapeDtypeStruct((bsz, nh, n, QK_PAD), BF16),
        jax.ShapeDtypeStruct((bsz, nh, n // tk, V_DIM, tk), BF16),
    ]
    if need_q:
        out_specs = [pl.BlockSpec((1, nh, QK_PAD, tm), lambda b, i: (b, 0, 0, i))] + out_specs
        out_shape = [jax.ShapeDtypeStruct((bsz, nh, QK_PAD, n), BF16)] + out_shape
    return pl.pallas_call(
        functools.partial(_mla_proj_body, need_q=need_q, tk=tk),
        grid=(bsz, n // tm),
        in_specs=in_specs,
        out_specs=out_specs,
        out_shape=out_shape,
        compiler_params=_cp(("arbitrary", "arbitrary")),
        name="mla_proj_q" if need_q else "mla_proj_ctx",
    )(x, g, sh, sc, wd, gq, gkv, wuk, wuqT, wuvT, ct, st, cT, sT)


SM_STRIP = 64
SUBLANES = 8


def _attn_body(qT_ref, k_ref, vT_ref, kc_ref, vTc_ref, o_ref, s0, s1, p0, p1, sc, pc, acc, m_scr, x0, x1, xc,
               a0, a1, ac, l_scr, d0, d1, dc, *, tk):
    nchunk = k_ref.shape[2] // tk

    def scores(kblk, s_ref, mx_ref):
        r = _dot(kblk, qT_ref[0, 0])
        s_ref[...] = r
        mx_ref[...] = jnp.max(r, axis=0, keepdims=True)

    def probs(s_ref, mx_ref, p_ref, a_ref, d_ref):
        m_old = m_scr[...]
        m_new = jnp.maximum(m_old, mx_ref[...])
        m_scr[...] = m_new
        alpha = jnp.exp2(m_old - m_new)
        a_ref[...] = alpha
        part = None
        for r in range(0, s_ref.shape[0], SM_STRIP):
            p = jnp.exp2(s_ref[r:r + SM_STRIP] - m_new)
            p_ref[r:r + SM_STRIP] = p.astype(BF16)
            ps = jnp.sum(p.reshape(SM_STRIP // SUBLANES, SUBLANES, p.shape[1]), axis=0)
            part = ps if part is None else part + ps
        d_ref[...] = part

    def accumulate(p_ref, a_ref, d_ref, vblk):
        acc[...] = a_ref[...] * acc[...] + _dot(vblk, p_ref[...])
        l_scr[...] = a_ref[...] * l_scr[...] + d_ref[...]

    def kchunk(i):
        return k_ref[0, 0, pl.ds(pl.multiple_of(i * tk, tk), tk), :]

    def vchunk(i):
        nsub = tk // vT_ref.shape[-1]
        return jnp.concatenate([vT_ref[0, 0, i * nsub + u] for u in range(nsub)], axis=1)

    m_scr[...] = jnp.full(m_scr.shape, -jnp.inf, F32)
    acc[...] = jnp.zeros(acc.shape, F32)
    l_scr[...] = jnp.zeros(l_scr.shape, F32)
    scores(kc_ref[0, 0], sc, xc)
    scores(kchunk(0), s0, x0)
    probs(sc, xc, pc, ac, dc)
    scores(kchunk(1), s1, x1)
    accumulate(pc, ac, dc, vTc_ref[0, 0, 0])
    probs(s0, x0, p0, a0, d0)

    def body(j, carry):
        t = 2 * j
        scores(kchunk(t + 2), s0, x0)
        accumulate(p0, a0, d0, vchunk(t))
        probs(s1, x1, p1, a1, d1)
        scores(kchunk(t + 3), s1, x1)
        accumulate(p1, a1, d1, vchunk(t + 1))
        probs(s0, x0, p0, a0, d0)
        return carry

    lax.fori_loop(0, nchunk // 2 - 1, body, 0)
    accumulate(p0, a0, d0, vchunk(nchunk - 2))
    probs(s1, x1, p1, a1, d1)
    accumulate(p1, a1, d1, vchunk(nchunk - 1))
    o_ref[0, 0] = (acc[...] / jnp.sum(l_scr[...], axis=0, keepdims=True)).astype(BF16)


def _attention(qT, k, vT, kc, vTc, *, tq, tk):
    bsz, nh, _, n = qT.shape
    nc = kc.shape[2]
    tv = vT.shape[-1]
    assert (n // tk) % 2 == 0 and tk % tv == 0
    return pl.pallas_call(
        functools.partial(_attn_body, tk=tk),
        grid=(bsz, nh, n // tq),
        in_specs=[
            pl.BlockSpec((1, 1, QK_PAD, tq), lambda b, h, i: (b, h, 0, i)),
            pl.BlockSpec((1, 1, n, QK_PAD), lambda b, h, i: (b, h, 0, 0)),
            pl.BlockSpec((1, 1, n // tv, V_DIM, tv), lambda b, h, i: (b, h, 0, 0, 0)),
            pl.BlockSpec((1, 1, nc, QK_PAD), lambda b, h, i: (b, h, 0, 0)),
            pl.BlockSpec((1, 1, 1, V_DIM, nc), lambda b, h, i: (b, h, 0, 0, 0)),
        ],
        out_specs=pl.BlockSpec((1, 1, V_DIM, tq), lambda b, h, i: (b, h, 0, i)),
        out_shape=jax.ShapeDtypeStruct((bsz, nh, V_DIM, n), BF16),
        scratch_shapes=[pltpu.VMEM((tk, tq), F32), pltpu.VMEM((tk, tq), F32),
                        pltpu.VMEM((tk, tq), BF16), pltpu.VMEM((tk, tq), BF16),
                        pltpu.VMEM((nc, tq), F32), pltpu.VMEM((nc, tq), BF16),
                        pltpu.VMEM((V_DIM, tq), F32), pltpu.VMEM((1, tq), F32),
                        pltpu.VMEM((1, tq), F32), pltpu.VMEM((1, tq), F32), pltpu.VMEM((1, tq), F32),
                        pltpu.VMEM((1, tq), F32), pltpu.VMEM((1, tq), F32), pltpu.VMEM((1, tq), F32),
                        pltpu.VMEM((SUBLANES, tq), F32), pltpu.VMEM((SUBLANES, tq), F32),
                        pltpu.VMEM((SUBLANES, tq), F32), pltpu.VMEM((SUBLANES, tq), F32)],
        compiler_params=_cp(("arbitrary", "arbitrary", "arbitrary")),
        name="mla_attention",
    )(qT, k, vT, kc, vTc)


def _post_body(*refs, transposed):
    if transposed:
        o_ref, *refs = refs
    else:
        o_ref, ob_ref, *refs = refs
    (wo_ref, bo_ref, x_ref, g1_ref, gf_ref, sh_ref, sc_ref, wrh_ref, wrl_ref, br_ref, tri_ref,
     xl_ref, fl_ref, ti_ref, gt_ref, rk_ref, cnt_ref) = refs

    @pl.when((pl.program_id(0) == 0) & (pl.program_id(1) == 0))
    def _():
        cnt_ref[...] = jnp.zeros_like(cnt_ref)

    tm = x_ref.shape[1]
    if transposed:
        oT = o_ref[0].astype(F32).reshape(MLA_HEADS * V_DIM, tm)
        o = oT.T.astype(BF16)
    else:
        o = _interleave_tiles(o_ref[0], ob_ref[0]).astype(BF16)
    y = _dot(o, wo_ref[...]) + bo_ref[...]
    xl = x_ref[0] + g1_ref[0] * y
    xl_ref[0] = xl
    fl = _rms(xl, gf_ref[...]) * (1.0 + sc_ref[0]) + sh_ref[0]
    _to_rows(fl_ref, fl)
    flh = fl.astype(BF16)
    fll = (fl - flh.astype(F32)).astype(BF16)
    logits = _dot(flh, wrh_ref[...]) + (_dot(fll, wrh_ref[...]) + _dot(flh, wrl_ref[...])) + br_ref[...]
    lane = lax.broadcasted_iota(jnp.int32, (tm, LANES), 1).astype(F32)
    neg = jnp.float32(-jnp.inf)
    work = jnp.where(lane < N_EXPERTS, logits, neg)
    vals, idxs = [], []
    onehot = jnp.zeros((tm, LANES), F32)
    for _ in range(TOP_K):
        mk = jnp.max(work, axis=-1, keepdims=True)
        ik = jnp.min(jnp.where(work == mk, lane, float(LANES)), axis=-1, keepdims=True)
        sel = lane == ik
        onehot = jnp.where(sel, 1.0, onehot)
        work = jnp.where(sel, neg, work)
        vals.append(mk)
        idxs.append(ik)
    es = [jnp.exp(v - vals[0]) for v in vals]
    den = es[0] + es[1] + es[2] + es[3]
    pre = _dot(tri_ref[...], onehot.astype(BF16)) + cnt_ref[...]
    ti = jnp.zeros((tm, LANES), F32)
    gt = jnp.zeros((tm, LANES), F32)
    rk = jnp.zeros((tm, LANES), F32)
    for kk in range(TOP_K):
        rank = jnp.sum(jnp.where(lane == idxs[kk], pre, 0.0), axis=-1, keepdims=True)
        ti = jnp.where(lane == kk, idxs[kk], ti)
        gt = jnp.where(lane == kk, es[kk] / den, gt)
        rk = jnp.where(lane == kk, rank, rk)
    ti_ref[...] = ti[:, :TOP_K].astype(jnp.int32)
    gt_ref[...] = gt[:, :TOP_K]
    rk_ref[...] = rk[:, :TOP_K].astype(jnp.int32)
    cnt_ref[...] += jnp.sum(onehot, axis=0, keepdims=True)


def _post(o, wo, bo, x, g1, gf, sh, sc, wrh, wrl, br, *, transposed, tm):
    bsz, n, d = x.shape
    t = bsz * n
    nt = n // tm
    tri = (lax.broadcasted_iota(jnp.int32, (tm, tm), 0) > lax.broadcasted_iota(jnp.int32, (tm, tm), 1)).astype(BF16)
    full = lambda a: pl.BlockSpec(a.shape, lambda b, i: (0,) * a.ndim)
    per_b = pl.BlockSpec((1, 1, d), lambda b, i: (b, 0, 0))
    if transposed:
        o_args = [o]
        o_specs = [pl.BlockSpec((1, MLA_HEADS, V_DIM, tm), lambda b, i: (b, 0, 0, i))]
    else:
        o_args = list(o)
        o_specs = [pl.BlockSpec((1, tm, d // 2), lambda b, i: (b, i, 0))] * 2
    tok = lambda w: pl.BlockSpec((tm, w), lambda b, i: (b * nt + i, 0))
    return pl.pallas_call(
        functools.partial(_post_body, transposed=transposed),
        grid=(bsz, nt),
        in_specs=[*o_specs, full(wo), full(bo), pl.BlockSpec((1, tm, d), lambda b, i: (b, i, 0)), per_b, full(gf),
                  per_b, per_b, full(wrh), full(wrl), full(br), full(tri)],
        out_specs=[pl.BlockSpec((1, tm, d), lambda b, i: (b, i, 0)),
                   pl.BlockSpec((tm * ROW_SUB, LANES), lambda b, i: (b * nt + i, 0)),
                   tok(TOP_K), tok(TOP_K), tok(TOP_K), pl.BlockSpec((1, LANES), lambda b, i: (0, 0))],
        out_shape=[jax.ShapeDtypeStruct((bsz, n, d), F32), jax.ShapeDtypeStruct((t * ROW_SUB, LANES), F32),
                   jax.ShapeDtypeStruct((t, TOP_K), jnp.int32), jax.ShapeDtypeStruct((t, TOP_K), F32),
                   jax.ShapeDtypeStruct((t, TOP_K), jnp.int32), jax.ShapeDtypeStruct((1, LANES), F32)],
        compiler_params=_cp(("arbitrary", "arbitrary")),
        name="post_attn" if transposed else "post_hyena",
    )(*o_args, wo, bo, x, g1, gf, sh, sc, wrh, wrl, br, tri)


ROW_SUB = 8


def _row_slice(i):
    return pl.ds(pl.multiple_of(i * ROW_SUB, ROW_SUB), ROW_SUB)


def _to_rows(ref, x):
    for s in range(ROW_SUB):
        ref[pl.ds(s, x.shape[0], stride=ROW_SUB), :] = x[:, s * LANES:(s + 1) * LANES]


def _from_rows(ref, lo, hi):
    return jnp.concatenate([ref[pl.ds(lo * ROW_SUB + s, hi - lo, stride=ROW_SUB), :] for s in range(ROW_SUB)], axis=1)


def _dispatch_body(pe_ref, pd_ref, dest_ref, fl_ref, xs_out, zbuf, sem, *, td):
    @pl.when(pl.program_id(0) == 0)
    def _():
        zbuf[...] = jnp.zeros(zbuf.shape, zbuf.dtype)

        def fill(e):
            start = pl.multiple_of((pe_ref[e] - MOE_TM) * ROW_SUB, ROW_SUB)
            return pltpu.make_async_copy(zbuf, xs_out.at[pl.ds(start, MOE_TM * ROW_SUB)], sem)

        for e in range(N_EXPERTS):
            @pl.when(pd_ref[e] > 0)
            def _():
                fill(e).start()
        for e in range(N_EXPERTS):
            @pl.when(pd_ref[e] > 0)
            def _():
                fill(e).wait()

    def issue(t, carry):
        for kk in range(TOP_K):
            d = dest_ref[0, 0, t * TOP_K + kk]
            pltpu.make_async_copy(fl_ref.at[_row_slice(t)], xs_out.at[_row_slice(d)], sem).start(priority=kk % 2)
        return carry

    lax.fori_loop(0, td, issue, 0, unroll=2)

    def drain(t, carry):
        pltpu.make_async_copy(fl_ref.at[_row_slice(0)], xs_out.at[_row_slice(0)], sem).wait()
        return carry

    lax.fori_loop(0, td * TOP_K, drain, 0, unroll=8)


def _dispatch(pad_end, padded, dest, fl, n_rows, *, td):
    t = fl.shape[0] // ROW_SUB
    dest3 = dest.reshape(t // td, 1, td * TOP_K)
    grid_spec = pltpu.PrefetchScalarGridSpec(
        num_scalar_prefetch=2,
        grid=(t // td,),
        in_specs=[
            pl.BlockSpec((1, 1, td * TOP_K), lambda i, pe, pd: (i, 0, 0), memory_space=pltpu.SMEM),
            pl.BlockSpec((td * ROW_SUB, LANES), lambda i, pe, pd: (i, 0)),
        ],
        out_specs=pl.BlockSpec(memory_space=pl.ANY),
        scratch_shapes=[pltpu.VMEM((MOE_TM * ROW_SUB, LANES), fl.dtype), pltpu.SemaphoreType.DMA(())],
    )
    return pl.pallas_call(
        functools.partial(_dispatch_body, td=td),
        grid_spec=grid_spec,
        out_shape=jax.ShapeDtypeStruct((n_rows * ROW_SUB, LANES), fl.dtype),
        compiler_params=_cp(("arbitrary",)),
        name="moe_dispatch",
    )(pad_end, padded, dest3, fl)


def _expert_body(be_ref, nu_ref, xs_ref, win_ref, bin_ref, wout_ref, bout_ref, ys_ref, win_s, wout_s):
    b = pl.program_id(0)
    dff = wout_ref.shape[1]

    @pl.when(b < nu_ref[0])
    def _():
        prev = be_ref[jnp.maximum(b - 1, 0)]

        @pl.when((b == 0) | (prev != be_ref[b]))
        def _():
            win_s[...] = win_ref[0].astype(BF16)
            wout_s[...] = wout_ref[0].astype(BF16)

        x = _from_rows(xs_ref, 0, xs_ref.shape[0] // ROW_SUB).astype(BF16)
        gu = _dot(x, win_s[...]) + bin_ref[0]
        gate = jnp.minimum(gu[:, :dff], SWIGLU_LIMIT)
        lin = jnp.clip(gu[:, dff:], -SWIGLU_LIMIT, SWIGLU_LIMIT)
        act = gate * jax.nn.sigmoid(SWIGLU_ALPHA * gate) * (lin + 1.0)
        _to_rows(ys_ref, _dot(act.astype(BF16), wout_s[...]) + bout_ref[0])

    @pl.when(b >= nu_ref[0])
    def _():
        ys_ref[...] = jnp.zeros_like(ys_ref)


def _experts(blk_exp, n_used, xs, layer, w_in, b_in, w_out, b_out):
    n_rows = xs.shape[0] // ROW_SUB
    depth, ne, d, f2 = w_in.shape
    dff = w_out.shape[2]
    tm = MOE_TM
    grid_spec = pltpu.PrefetchScalarGridSpec(
        num_scalar_prefetch=2,
        grid=(n_rows // tm,),
        in_specs=[
            pl.BlockSpec((tm * ROW_SUB, LANES), lambda b, be, nu: (jnp.minimum(b, nu[0] - 1), 0)),
            pl.BlockSpec((None, 1, d, f2), lambda b, be, nu: (layer, be[b], 0, 0)),
            pl.BlockSpec((None, 1, 1, f2), lambda b, be, nu: (layer, be[b], 0, 0)),
            pl.BlockSpec((None, 1, dff, d), lambda b, be, nu: (layer, be[b], 0, 0)),
            pl.BlockSpec((None, 1, 1, d), lambda b, be, nu: (layer, be[b], 0, 0)),
        ],
        out_specs=pl.BlockSpec((tm * ROW_SUB, LANES), lambda b, be, nu: (b, 0)),
        scratch_shapes=[pltpu.VMEM((d, f2), BF16), pltpu.VMEM((dff, d), BF16)],
    )
    return pl.pallas_call(
        _expert_body,
        grid_spec=grid_spec,
        out_shape=jax.ShapeDtypeStruct(xs.shape, F32),
        compiler_params=_cp(("arbitrary",)),
        name="moe_experts",
    )(blk_exp, n_used, xs, w_in, b_in.reshape(depth, ne, 1, f2), w_out, b_out.reshape(depth, ne, 1, d))


def _combine_body(dest_ref, ys_hbm, gt_ref, xl_ref, g2_ref, fg_ref, out_ref, buf, sem, *, tc, final):
    def issue(t, carry):
        for kk in range(TOP_K):
            d = dest_ref[0, 0, t * TOP_K + kk]
            pltpu.make_async_copy(ys_hbm.at[_row_slice(d)], buf.at[_row_slice(kk * tc + t)], sem).start(
                priority=kk % 2)
        return carry

    lax.fori_loop(0, tc, issue, 0, unroll=2)

    def drain(t, carry):
        pltpu.make_async_copy(ys_hbm.at[_row_slice(0)], buf.at[_row_slice(0)], sem).wait()
        return carry

    lax.fori_loop(0, tc * TOP_K, drain, 0, unroll=8)
    gt = gt_ref[...]
    y = gt[:, 0:1] * _from_rows(buf, 0, tc)
    for kk in range(1, TOP_K):
        y = y + gt[:, kk:kk + 1] * _from_rows(buf, kk * tc, (kk + 1) * tc)
    xl = xl_ref[0] + g2_ref[0] * y
    out_ref[0] = _rms(xl, fg_ref[...]) if final else xl


def _combine(dest, ys, gates, xl, g2, fg, *, tc, final):
    bsz, n, d = xl.shape
    t = bsz * n
    nt = n // tc
    dest3 = dest.reshape(t // tc, 1, tc * TOP_K)
    return pl.pallas_call(
        functools.partial(_combine_body, tc=tc, final=final),
        grid=(bsz, nt),
        in_specs=[
            pl.BlockSpec((1, 1, tc * TOP_K), lambda b, i: (b * nt + i, 0, 0), memory_space=pltpu.SMEM),
            pl.BlockSpec(memory_space=pl.ANY),
            pl.BlockSpec((tc, TOP_K), lambda b, i: (b * nt + i, 0)),
            pl.BlockSpec((1, tc, d), lambda b, i: (b, i, 0)),
            pl.BlockSpec((1, 1, d), lambda b, i: (b, 0, 0)),
            pl.BlockSpec((1, d), lambda b, i: (0, 0)),
        ],
        out_specs=pl.BlockSpec((1, tc, d), lambda b, i: (b, i, 0)),
        out_shape=jax.ShapeDtypeStruct((bsz, n, d), F32),
        scratch_shapes=[pltpu.VMEM((TOP_K * tc * ROW_SUB, LANES), F32), pltpu.SemaphoreType.DMA(())],
        compiler_params=_cp(("arbitrary", "arbitrary")),
        name="moe_combine",
    )(dest3, ys, gates, xl, g2, fg)


def _moe(fl, topi, gates, rank, cnt, xl, g2, fg, layer, w_in, b_in, w_out, b_out, *, final):
    t = fl.shape[0] // ROW_SUB
    tm = MOE_TM
    counts = cnt[0, :N_EXPERTS].astype(jnp.int32)
    padded = (counts + tm - 1) // tm * tm
    pad_end = jnp.cumsum(padded)
    pad_start = pad_end - padded
    dest = jnp.take(pad_start, topi) + rank
    nb = t * TOP_K // tm + N_EXPERTS
    blk_start = jnp.arange(nb, dtype=jnp.int32) * tm
    blk_exp = jnp.minimum(jnp.sum((pad_end[None, :] <= blk_start[:, None]).astype(jnp.int32), axis=1), N_EXPERTS - 1)
    n_used = (pad_end[-1:] // tm).astype(jnp.int32)
    xs = _dispatch(pad_end, padded, dest, fl, nb * tm, td=1024)
    ys = _experts(blk_exp, n_used, xs, layer, w_in, b_in, w_out, b_out)
    return _combine(dest, ys, gates, xl, g2, fg, tc=1024, final=final)


def _hy_in_body(x_ref, xp_ref, xn_ref, g_ref, sh_ref, sc_ref, w_ref, b_ref, cw_ref, cb_ref, o_ref, *, nt):
    i = pl.program_id(1)
    d = x_ref.shape[2]

    def normed(xx):
        return (_rms(xx, g_ref[...]) * (1.0 + sc_ref[0]) + sh_ref[0]).astype(BF16)

    h = normed(x_ref[0])
    hh = normed(jnp.concatenate([xp_ref[0], xn_ref[0]], axis=0))
    tm = h.shape[0]
    row = lax.broadcasted_iota(jnp.int32, (tm, 1), 0)
    for j in range(3):
        cols = slice(j * d, (j + 1) * d)
        p = _dot(h, w_ref[:, cols]) + b_ref[:, cols]
        ph = _dot(hh, w_ref[:, cols]) + b_ref[:, cols]
        prev = jnp.where(i > 0, ph[7:8], 0.0)
        nxt = jnp.where(i < nt - 1, ph[8:9], 0.0)
        up = jnp.where(row == 0, prev, pltpu.roll(p, 1, axis=0))
        dn = jnp.where(row == tm - 1, nxt, pltpu.roll(p, tm - 1, axis=0))
        cw = cw_ref[:, cols]
        o_ref[j, 0] = up * cw[0:1] + p * cw[1:2] + dn * cw[2:3] + cb_ref[:, cols]


def _hy_in(x, g, sh, sc, w, b, cw, cb, *, tm):
    bsz, n, d = x.shape
    nt = n // tm
    hb = tm // 8
    per_b = pl.BlockSpec((1, 1, d), lambda bb, i: (bb, 0, 0))
    full = lambda a: pl.BlockSpec(a.shape, lambda bb, i: (0,) * a.ndim)
    return pl.pallas_call(
        functools.partial(_hy_in_body, nt=nt),
        grid=(bsz, nt),
        in_specs=[
            pl.BlockSpec((1, tm, d), lambda bb, i: (bb, i, 0)),
            pl.BlockSpec((1, 8, d), lambda bb, i: (bb, jnp.maximum(i * hb - 1, 0), 0)),
            pl.BlockSpec((1, 8, d), lambda bb, i: (bb, jnp.minimum((i + 1) * hb, n // 8 - 1), 0)),
            full(g), per_b, per_b, full(w), full(b), full(cw), full(cb),
        ],
        out_specs=pl.BlockSpec((3, 1, tm, d), lambda bb, i: (0, bb, i, 0)),
        out_shape=jax.ShapeDtypeStruct((3, bsz, n, d), F32),
        compiler_params=_cp(("arbitrary", "arbitrary")),
        name="hyena_in_proj",
    )(x, x, x, g, sh, sc, w, b, cw, cb)


def _filt_feat_body(w1_ref, b1_ref, f1_ref, w2_ref, b2_ref, f2_ref, o_ref, *, n_lat):
    na = o_ref.shape[1]
    a = lax.broadcasted_iota(jnp.int32, (na, 1), 0)
    lane = lax.broadcasted_iota(jnp.int32, (na, LANES), 1)
    band_idx = jnp.where(lane <= HY_BANDS, lane - 1, lane - 1 - HY_BANDS).astype(F32)
    band = 1e-4 + band_idx * ((HY_BANDS - 1 - 1e-4) / (HY_BANDS - 1))
    phase = jnp.where(lane > HY_BANDS, 0.5 * math.pi, 0.0)
    for j in range(B_GROUP):
        r = a * FFT_N2 + (pl.program_id(0) * B_GROUP + j)
        pos = jnp.where(r < n_lat, r, 2 * n_lat - r).astype(F32)
        tn = pos / float(max(n_lat - 1, 1))
        ang = ((2.0 * math.pi / n_lat) * pos) * band
        z = jnp.where(lane == 0, tn, jnp.where(lane < HY_EMB, jnp.cos(ang + phase), 0.0))
        h1 = jnp.sin(f1_ref[...] * (_dot_hi(z, w1_ref[...]) + b1_ref[...]))
        h2 = jnp.sin(f2_ref[...] * (_dot_hi(h1, w2_ref[...]) + b2_ref[...]))
        valid = (r != n_lat).astype(F32)
        o_ref[j] = jnp.where(lane == HY_HID, tn, jnp.where(lane == HY_HID + 1, valid, h2))


def _filt_feat(w1, b1, f1, w2, b2, f2, *, n_lat):
    na = 2 * n_lat // FFT_N2
    w1p = jnp.zeros((LANES, LANES), F32).at[:HY_EMB, :HY_HID].set(w1)
    w2p = jnp.zeros((LANES, LANES), F32).at[:HY_HID, :HY_HID].set(w2)
    padv = lambda v: jnp.zeros((1, LANES), F32).at[0, :HY_HID].set(v)
    full = lambda shp: pl.BlockSpec(shp, lambda i: (0,) * len(shp))
    return pl.pallas_call(
        functools.partial(_filt_feat_body, n_lat=n_lat),
        grid=(FFT_N2 // B_GROUP,),
        in_specs=[full((LANES, LANES)), full((1, LANES)), full((1, LANES)),
                  full((LANES, LANES)), full((1, LANES)), full((1, LANES))],
        out_specs=pl.BlockSpec((B_GROUP, na, LANES), lambda i: (i, 0, 0)),
        out_shape=jax.ShapeDtypeStruct((FFT_N2, na, LANES), F32),
        compiler_params=_cp(("arbitrary",)),
        name="hyena_filter_features",
    )(w1p, padv(b1), padv(f1), w2p, padv(b2), padv(f2))


U32 = jnp.uint32
HI16 = 0xFFFF0000


def _pack_c(re, im):
    lo = lax.bitcast_convert_type(re.astype(BF16).astype(F32), U32) >> 16
    hi = lax.bitcast_convert_type(im.astype(BF16).astype(F32), U32) & U32(HI16)
    return hi | lo


def _unpack_c(u):
    re = lax.bitcast_convert_type(u << 16, F32)
    im = lax.bitcast_convert_type(u & U32(HI16), F32)
    return jnp.concatenate([re, im], axis=0).astype(BF16)


def _interleave_tiles(xa, xb):
    tiles = []
    for c0 in range(0, xa.shape[-1], LANES):
        tiles += [xa[..., c0:c0 + LANES], xb[..., c0:c0 + LANES]]
    return jnp.concatenate(tiles, axis=-1)


def _filt_s1_body(hd_ref, w3_ref, dec_ref, tab_ref, oa_ref, ob_ref):
    na = hd_ref.shape[1]
    ha = na // 2
    oa2, ob2 = _rows2d(oa_ref), _rows2d(ob_ref)
    ft = hd_ref[:, :ha, :].reshape(B_GROUP * ha, LANES)
    fb = hd_ref[:, ha:, :].reshape(B_GROUP * ha, LANES)
    top = _dot(ft.astype(BF16), w3_ref[0, 0].astype(BF16))
    top = top * jnp.exp(-ft[:, HY_HID:HY_HID + 1] * jnp.abs(dec_ref[0, 0]))
    bot = _dot(fb.astype(BF16), w3_ref[0, 1].astype(BF16))
    bot = bot * (jnp.exp(-fb[:, HY_HID:HY_HID + 1] * jnp.abs(dec_ref[0, 1])) * fb[:, HY_HID + 1:HY_HID + 2])
    for j in range(B_GROUP):
        hb = jnp.concatenate([top[j * ha:(j + 1) * ha], bot[j * ha:(j + 1) * ha]], axis=0).astype(BF16)
        r = _dot(tab_ref[j], hb)
        packed = _pack_c(r[:na], r[na:])
        oa2[pl.ds(j, na, stride=B_GROUP), :] = packed[:, :LANES]
        ob2[pl.ds(j, na, stride=B_GROUP), :] = packed[:, LANES:]


def _filt_s1(hd, w3r, dec, tab):
    _, na, _ = hd.shape
    d = w3r.shape[-1]
    ct = 2 * LANES
    half = pl.BlockSpec((None, na, B_GROUP, LANES), lambda o, g, c: (o, 0, g, c))
    return pl.pallas_call(
        _filt_s1_body,
        grid=(2, FFT_N2 // B_GROUP, d // ct),
        in_specs=[
            pl.BlockSpec((B_GROUP, na, LANES), lambda o, g, c: (g, 0, 0)),
            pl.BlockSpec((1, 2, LANES, ct), lambda o, g, c: (o, 0, 0, c)),
            pl.BlockSpec((1, 2, 1, ct), lambda o, g, c: (o, 0, 0, c)),
            pl.BlockSpec((B_GROUP, 2 * na, na), lambda o, g, c: (g, 0, 0)),
        ],
        out_specs=[half, half],
        out_shape=[jax.ShapeDtypeStruct((2, na, FFT_N2, d // 2), U32)] * 2,
        compiler_params=_cp(("arbitrary", "arbitrary", "arbitrary")),
        name="hyena_filter_dft1",
    )(hd, w3r, dec, tab)


S2_KB = 4


def _s2_body(*refs, conv):
    if conv:
        oa_ref, ob_ref, kf_ref, ff_ref, fi_ref, g_ref = refs
    else:
        oa_ref, ob_ref, ff_ref, g_ref = refs
    for u in range(S2_KB):
        xf = _dot(ff_ref[...], _unpack_c(_interleave_tiles(oa_ref[u], ob_ref[u])))
        if conv:
            xr, xi = xf[:FFT_N2], xf[FFT_N2:]
            kr = kf_ref[0, u].astype(F32)
            ki = kf_ref[1, u].astype(F32)
            y = jnp.concatenate([xr * kr - xi * ki, xr * ki + xi * kr], axis=0).astype(BF16)
            xf = _dot(fi_ref[...], y)
            g_ref[u] = _pack_c(xf[:FFT_N2], xf[FFT_N2:])
        else:
            g_ref[0, u] = xf[:FFT_N2].astype(BF16)
            g_ref[1, u] = xf[FFT_N2:].astype(BF16)


def _s2(o3, kf, order, ff, fi, *, ct, conv):
    n1 = o3[0].shape[-3]
    d = 2 * o3[0].shape[-1]
    full = lambda a: pl.BlockSpec(a.shape, lambda k, c: (0,) * a.ndim)
    nk = n1 // S2_KB
    if conv:
        hblk = pl.BlockSpec((S2_KB, FFT_N2, ct // 2), lambda k, c: (k, 0, c))
        in_specs = [hblk, hblk, pl.BlockSpec((None, 2, S2_KB, FFT_N2, ct), lambda k, c: (order, 0, k, 0, c)),
                    full(ff), full(fi)]
        args = (o3[0], o3[1], kf, ff, fi)
        grid = (nk, d // ct)
        out_specs = pl.BlockSpec((S2_KB, FFT_N2, ct), lambda k, c: (k, 0, c))
        out_shape = jax.ShapeDtypeStruct((n1, FFT_N2, d), U32)
    else:
        no = o3[0].shape[0]
        hblk = pl.BlockSpec((None, S2_KB, FFT_N2, ct // 2), lambda k, c: (k // nk, k % nk, 0, c))
        in_specs = [hblk, hblk, full(ff)]
        args = (o3[0], o3[1], ff)
        grid = (no * nk, d // ct)
        out_specs = pl.BlockSpec((None, 2, S2_KB, FFT_N2, ct), lambda k, c: (k // nk, 0, k % nk, 0, c))
        out_shape = jax.ShapeDtypeStruct((no, 2, n1, FFT_N2, d), BF16)
    return pl.pallas_call(
        functools.partial(_s2_body, conv=conv),
        grid=grid,
        in_specs=in_specs,
        out_specs=out_specs,
        out_shape=out_shape,
        compiler_params=_cp(("arbitrary", "arbitrary")),
        name="hyena_conv_dft2" if conv else "hyena_filter_dft2",
    )(*args)


def _rows2d(ref):
    lead = ref.shape[:-3]
    return ref.reshape(lead + (ref.shape[-3] * B_GROUP, ref.shape[-1]))


def _s1_body(za_ref, zb_ref, tab_ref, oa_ref, ob_ref):
    rows, n1 = za_ref.shape[0], oa_ref.shape[0]
    za2, zb2, oa2, ob2 = _rows2d(za_ref), _rows2d(zb_ref), _rows2d(oa_ref), _rows2d(ob_ref)
    for j in range(B_GROUP):
        sl = pl.ds(j, rows, stride=B_GROUP)
        zj = jnp.concatenate([za2[sl, :], zb2[sl, :]], axis=1)
        r = _dot(tab_ref[j], zj.astype(BF16))
        packed = _pack_c(r[:n1], r[n1:])
        oa2[pl.ds(j, n1, stride=B_GROUP), :] = packed[:, :LANES]
        ob2[pl.ds(j, n1, stride=B_GROUP), :] = packed[:, LANES:]


def _s1(z4, zi, tab):
    _, rows, _, d = z4.shape
    n1 = tab.shape[1] // 2
    half = pl.BlockSpec((n1, B_GROUP, LANES), lambda g, c: (0, g, c))
    return pl.pallas_call(
        _s1_body,
        grid=(FFT_N2 // B_GROUP, d // (2 * LANES)),
        in_specs=[
            pl.BlockSpec((None, rows, B_GROUP, LANES), lambda g, c: (zi, 0, g, 2 * c)),
            pl.BlockSpec((None, rows, B_GROUP, LANES), lambda g, c: (zi, 0, g, 2 * c + 1)),
            pl.BlockSpec((B_GROUP, 2 * n1, rows), lambda g, c: (g, 0, 0)),
        ],
        out_specs=[half, half],
        out_shape=[jax.ShapeDtypeStruct((n1, FFT_N2, d // 2), U32)] * 2,
        compiler_params=_cp(("arbitrary", "arbitrary")),
        name="hyena_conv_dft1",
    )(z4, z4, tab)


def _s3_body(ga_ref, gb_ref, tab_ref, gta_ref, gtb_ref, za_ref, zb_ref, fb_ref, *rest, chain):
    if chain:
        tab1_ref, oa_ref, ob_ref, qa_ref, qb_ref = rest
    else:
        oa_ref, ob_ref = rest
    n1, rows = ga_ref.shape[0], oa_ref.shape[0]
    ga2, gb2, oa2, ob2 = _rows2d(ga_ref), _rows2d(gb_ref), _rows2d(oa_ref), _rows2d(ob_ref)
    for j in range(B_GROUP):
        sl = pl.ds(j, n1, stride=B_GROUP)
        gj = _unpack_c(jnp.concatenate([ga2[sl, :], gb2[sl, :]], axis=1))
        y = _dot(tab_ref[j], gj)
        oa2[pl.ds(j, rows, stride=B_GROUP), :] = y[:, :LANES]
        ob2[pl.ds(j, rows, stride=B_GROUP), :] = y[:, LANES:]
    fb = fb_ref[...]
    oa_ref[...] = gta_ref[...] * (oa_ref[...] + za_ref[...] * fb[:, :, :LANES])
    ob_ref[...] = gtb_ref[...] * (ob_ref[...] + zb_ref[...] * fb[:, :, LANES:])
    if chain:
        qa2, qb2 = _rows2d(qa_ref), _rows2d(qb_ref)
        for j in range(B_GROUP):
            sl = pl.ds(j, rows, stride=B_GROUP)
            zj = jnp.concatenate([oa2[sl, :], ob2[sl, :]], axis=1)
            r = _dot(tab1_ref[j], zj.astype(BF16))
            packed = _pack_c(r[:n1], r[n1:])
            qa2[pl.ds(j, n1, stride=B_GROUP), :] = packed[:, :LANES]
            qb2[pl.ds(j, n1, stride=B_GROUP), :] = packed[:, LANES:]


def _s3(g3, tab, gate4, gi, zsrc, fb, tab1):
    n1, _, d = g3.shape
    rows = tab.shape[1]
    chain = tab1 is not None
    half = pl.BlockSpec((rows, B_GROUP, LANES), lambda g, c: (0, g, c))
    chalf = pl.BlockSpec((n1, B_GROUP, LANES), lambda g, c: (0, g, c))
    nat = lambda idx, par: pl.BlockSpec((None, rows, B_GROUP, LANES), lambda g, c: (idx, 0, g, 2 * c + par))
    if isinstance(zsrc[1], int):
        z_specs, z_args = [nat(zsrc[1], 0), nat(zsrc[1], 1)], [zsrc[0], zsrc[0]]
    else:
        z_specs, z_args = [half, half], list(zsrc)
    in_specs = [
        pl.BlockSpec((n1, B_GROUP, LANES), lambda g, c: (0, g, 2 * c)),
        pl.BlockSpec((n1, B_GROUP, LANES), lambda g, c: (0, g, 2 * c + 1)),
        pl.BlockSpec((B_GROUP, rows, 2 * n1), lambda g, c: (g, 0, 0)),
        nat(gi, 0), nat(gi, 1), *z_specs,
        pl.BlockSpec((1, 1, 2 * LANES), lambda g, c: (0, 0, c)),
    ]
    args = [g3, g3, tab, gate4, gate4, *z_args, fb]
    zshape = jax.ShapeDtypeStruct((rows, FFT_N2, d // 2), F32)
    out_specs, out_shape = [half, half], [zshape, zshape]
    if chain:
        in_specs.append(pl.BlockSpec((B_GROUP, 2 * n1, rows), lambda g, c: (g, 0, 0)))
        args.append(tab1)
        out_specs += [chalf, chalf]
        out_shape += [jax.ShapeDtypeStruct((n1, FFT_N2, d // 2), U32)] * 2
    return pl.pallas_call(
        functools.partial(_s3_body, chain=chain),
        grid=(FFT_N2 // B_GROUP, d // (2 * LANES)),
        in_specs=in_specs,
        out_specs=out_specs,
        out_shape=out_shape,
        compiler_params=_cp(("arbitrary", "arbitrary")),
        name="hyena_conv_idft1_dft1" if chain else "hyena_conv_idft1",
    )(*args)


def _dft_tables(n_lat):
    n = 2 * n_lat
    n1 = n // FFT_N2
    k1 = jnp.arange(n1, dtype=jnp.int32)
    th_a = ((k1[:, None] * k1[None, :]) % n1).astype(F32) * (2.0 * math.pi / n1)
    th_b = (jnp.arange(FFT_N2, dtype=jnp.int32)[:, None] * k1[None, :]).astype(F32) * (2.0 * math.pi / n)
    ca, sa = jnp.cos(th_a)[None], jnp.sin(th_a)[None]
    cb, sb = jnp.cos(th_b)[:, :, None], jnp.sin(th_b)[:, :, None]
    cr = ca * cb - sa * sb
    sn = sa * cb + ca * sb
    ha = n1 // 2
    crh, snh = cr[:, :, :ha], sn[:, :, :ha]
    w1 = jnp.concatenate([jnp.concatenate([crh, snh], axis=2), jnp.concatenate([-snh, crh], axis=2)], axis=1)
    w1f = jnp.concatenate([cr, -sn], axis=1)
    v = jnp.swapaxes(w1, 1, 2) * (1.0 / n)
    k2 = jnp.arange(FFT_N2, dtype=jnp.int32)
    th2 = ((k2[:, None] * k2[None, :]) % FFT_N2).astype(F32) * (2.0 * math.pi / FFT_N2)
    c2, s2 = jnp.cos(th2), jnp.sin(th2)
    ff = jnp.concatenate([jnp.concatenate([c2, s2], axis=1), jnp.concatenate([-s2, c2], axis=1)], axis=0)
    fi = jnp.concatenate([jnp.concatenate([c2, -s2], axis=1), jnp.concatenate([s2, c2], axis=1)], axis=0)
    return w1.astype(BF16), w1f.astype(BF16), v.astype(BF16), ff.astype(BF16), fi.astype(BF16)


def _hyena_mix(proj3, fparams, fbias, *, n_lat):
    _, bsz, _, d = proj3.shape
    f_w1, f_b1, f_f1, f_w2, f_b2, f_f2, f_w3, decay = fparams
    w1, w1f, v, ff, fi = _dft_tables(n_lat)
    hd = _filt_feat(f_w1, f_b1, f_f1, f_w2, f_b2, f_f2, n_lat=n_lat)
    w3r = jnp.transpose(f_w3.reshape(HY_HID, 2, 2, d), (1, 2, 0, 3))
    w3r = jnp.zeros((2, 2, LANES, d), F32).at[:, :, :HY_HID].set(w3r)
    kf1 = _filt_s1(hd, w3r, decay.reshape(2, 2, 1, d), w1f)
    kf = _s2(kf1, None, 0, ff, None, ct=d, conv=False)
    p3 = proj3.reshape(3, bsz * (n_lat // FFT_N2), FFT_N2, d)
    o1 = _s1(p3, 2, w1)
    g = _s2(o1, kf, 0, ff, fi, ct=d, conv=True)
    za, zb, *o1 = _s3(g, v, p3, 0, (p3, 2), fbias[0].reshape(1, 1, d), w1)
    g = _s2(o1, kf, 1, ff, fi, ct=d, conv=True)
    za, zb = _s3(g, v, p3, 1, (za, zb), fbias[1].reshape(1, 1, d), None)
    return za.reshape(bsz, n_lat, d // 2), zb.reshape(bsz, n_lat, d // 2)


def _rope_tables(n_tokens):
    rows = n_tokens // GRID_W
    row = jnp.broadcast_to(jnp.arange(rows, dtype=F32)[:, None], (rows, GRID_W)).reshape(-1)
    col = jnp.broadcast_to(jnp.arange(GRID_W, dtype=F32)[None, :], (rows, GRID_W)).reshape(-1)
    axis_dim = QK_ROPE // 2
    inv_freq = 1.0 / (ROPE_THETA ** (jnp.arange(0, axis_dim, 2, dtype=F32) / axis_dim))
    ang = jnp.concatenate([row[:, None] * inv_freq, col[:, None] * inv_freq], axis=-1)
    return jnp.cos(ang), jnp.sin(ang)


def _mla_weights(w_down, g_q, w_uq, g_kv, w_ukv):
    d = w_down.shape[0]
    nh = MLA_HEADS
    kpe = w_down[:, Q_LORA + KV_LORA:]
    w1, w2 = kpe[:, 0::2], kpe[:, 1::2]
    z = jnp.zeros((d, LANES - QK_ROPE), w_down.dtype)
    wd = jnp.concatenate([w_down[:, :Q_LORA + KV_LORA], w1, w2, z, w2, w1, z], axis=1).astype(BF16)
    uq = w_uq.reshape(Q_LORA, nh, QK_NOPE + QK_ROPE)
    pe = uq[:, :, QK_NOPE:]
    uq = jnp.concatenate([uq[:, :, :QK_NOPE], pe[:, :, 0::2], pe[:, :, 1::2]], axis=2)
    wuqT = uq.reshape(Q_LORA, nh * (QK_NOPE + QK_ROPE)).T.astype(BF16)
    ukv = w_ukv.reshape(KV_LORA, nh, QK_NOPE + V_DIM)
    wuk = ukv[:, :, :QK_NOPE].reshape(KV_LORA, nh * QK_NOPE).astype(BF16)
    wuvT = ukv[:, :, QK_NOPE:].reshape(KV_LORA, nh * V_DIM).T.astype(BF16)
    return wd, g_q.reshape(1, -1), g_kv.reshape(1, -1), wuk, wuqT, wuvT


def kernel(x, c, ctx, c_ctx, ada_w, ada_b, norm_mix_g, norm_ffn_g, mla_w_down, mla_g_q, mla_w_uq, mla_g_kv, mla_w_ukv, mla_w_o, hy_w_in, hy_b_in, hy_conv_w, hy_conv_b, hy_f_w1, hy_f_b1, hy_f_freq1, hy_f_w2, hy_f_b2, hy_f_freq2, hy_f_w3, hy_decay, hy_bias, hy_w_out, hy_b_out, moe_w_r, moe_b_r, moe_w_in, moe_b_in, moe_w_out, moe_b_out, final_g):
    bsz, n_lat, d = x.shape
    n_ctx = ctx.shape[1]
    depth = ada_w.shape[0]
    assert bsz == 2 and d == MLA_HEADS * V_DIM and n_lat % 512 == 0 and n_ctx % 128 == 0
    assert depth == 2

    cond8 = jnp.zeros((8, d), F32).at[:bsz].set(c).at[bsz].set(c_ctx)
    mods = _ada(cond8, ada_w, ada_b)

    def mod(i, j, rows):
        return mods[i, rows, j * d:(j + 1) * d][:, None, :]

    lat_rows = slice(0, bsz)
    ctx_rows = slice(bsz, bsz + 1)
    xl = x
    for i in range(depth):
        kind, j = i % 2, i // 2
        sh1, sc1, g1 = (mod(i, m, lat_rows) for m in range(3))
        sh2, sc2, g2 = (mod(i, m, lat_rows) for m in range(3, 6))
        gm = norm_mix_g[i].reshape(1, d)
        if kind == 0:
            wts = _mla_weights(mla_w_down[j], mla_g_q[j], mla_w_uq[j], mla_g_kv[j], mla_w_ukv[j])
            cos, sin = _rope_tables(n_lat)
            zl = jnp.zeros((n_lat, LANES - QK_ROPE), F32)
            tabs = (jnp.concatenate([cos, cos, zl], axis=1), jnp.concatenate([-sin, sin, zl], axis=1), cos.T, sin.T)
            tq = tv = 512
            tk = 2048 if n_lat % 4096 == 0 else 512
            qT, k, vT = _mla_proj(xl, gm, sh1, sc1, wts, tabs, need_q=True, tm=tv, tk=tv)
            half = QK_ROPE // 2
            one_c = jnp.concatenate([jnp.ones((n_ctx, QK_ROPE), F32), jnp.zeros((n_ctx, LANES - QK_ROPE), F32)], axis=1)
            tabs_c = (one_c, jnp.zeros((n_ctx, LANES), F32), jnp.ones((half, n_ctx), F32), jnp.zeros((half, n_ctx), F32))
            kc, vTc = _mla_proj(ctx, gm, mod(i, 0, ctx_rows), mod(i, 1, ctx_rows), wts, tabs_c,
                                need_q=False, tm=n_ctx, tk=n_ctx)
            o = _attention(qT, k, vT, kc, vTc, tq=tq, tk=tk)
            wo = mla_w_o[j].astype(BF16)
            bo = jnp.zeros((1, d), F32)
            transposed = True
        else:
            proj3 = _hy_in(xl, gm, sh1, sc1, hy_w_in[j].astype(BF16), hy_b_in[j].reshape(1, -1), hy_conv_w[j],
                           hy_conv_b[j].reshape(1, -1), tm=512)
            fparams = (hy_f_w1[j], hy_f_b1[j], hy_f_freq1[j], hy_f_w2[j], hy_f_b2[j], hy_f_freq2[j], hy_f_w3[j],
                       hy_decay[j])
            o = _hyena_mix(proj3, fparams, hy_bias[j], n_lat=n_lat)
            wo = hy_w_out[j].astype(BF16)
            bo = hy_b_out[j].reshape(1, d)
            transposed = False
        wr = jnp.zeros((d, LANES), F32).at[:, :N_EXPERTS].set(moe_w_r[i])
        wrh = wr.astype(BF16)
        wrl = (wr - wrh.astype(F32)).astype(BF16)
        br = jnp.zeros((1, LANES), F32).at[0, :N_EXPERTS].set(moe_b_r[i])
        xl, fl, topi, gates, rank, cnt = _post(o, wo, bo, xl, g1, norm_ffn_g[i].reshape(1, d), sh2, sc2, wrh, wrl, br,
                                               transposed=transposed, tm=512)
        xl = _moe(fl, topi, gates, rank, cnt, xl, g2, final_g.reshape(1, d), i, moe_w_in, moe_b_in,
                  moe_w_out, moe_b_out, final=(i == depth - 1))
    return xl
```
